```python
import math
import jax
import jax.numpy as jnp
from jax import lax
import numpy as np

D_MODEL = 2048
BATCH = 8
SEQ = 8192
DEPTH = 2

NUM_MIXERS = 2
EPS = 1e-6

ATT_HEAD_DIM = 64
ATT_Q_HEADS = D_MODEL // ATT_HEAD_DIM
ATT_KV_HEADS = ATT_Q_HEADS // 8
ATT_GROUP = ATT_Q_HEADS // ATT_KV_HEADS
WINDOW = 128
ATT_BLOCK = WINDOW
ROPE_THETA = 10000.0
QKV_WIDTH = (ATT_Q_HEADS + 2 * ATT_KV_HEADS) * ATT_HEAD_DIM

SSM_D_INNER = 2 * D_MODEL
SSM_HEAD_DIM = 64
SSM_HEADS = SSM_D_INNER // SSM_HEAD_DIM
SSM_GROUPS = 8
SSM_HEADS_PER_GROUP = SSM_HEADS // SSM_GROUPS
SSM_STATE = 128
SSM_CONV = 4
SSM_CHUNK = 256
SSM_CONV_DIM = SSM_D_INNER + 2 * SSM_GROUPS * SSM_STATE
SSM_IN_WIDTH = SSM_D_INNER + SSM_CONV_DIM + SSM_HEADS
SSM_NORM_GROUP = SSM_D_INNER // SSM_GROUPS

D_FF = -(-(8 * D_MODEL) // (3 * 256)) * 256

N_ATTN_LAYERS = (DEPTH + NUM_MIXERS - 1) // NUM_MIXERS
N_SSM_LAYERS = DEPTH // NUM_MIXERS

kernel_name = 'hybrid_swa_sink_mamba2_swiglu'


def rms_norm(x, gain):
    xf = x.astype(jnp.float32)
    y = xf * lax.rsqrt(jnp.mean(xf * xf, axis=-1, keepdims=True) + EPS)
    return (y * gain.astype(jnp.float32)).astype(x.dtype)


def rope_tables(positions):
    inv_freq = ROPE_THETA ** (-jnp.arange(0, ATT_HEAD_DIM, 2, dtype=jnp.float32) / ATT_HEAD_DIM)
    ang = positions.astype(jnp.float32)[..., None] * inv_freq
    return jnp.cos(ang)[:, :, None, :], jnp.sin(ang)[:, :, None, :]


def apply_rope(t, cos, sin):
    tf = t.astype(jnp.float32)
    t1, t2 = jnp.split(tf, 2, axis=-1)
    return jnp.concatenate([t1 * cos - t2 * sin, t2 * cos + t1 * sin], axis=-1).astype(t.dtype)


def sliding_window_attention(h, positions, w_qkv, q_norm, k_norm, sinks, w_o):
    b, s, _ = h.shape
    qkv = h @ w_qkv
    q, k, v = jnp.split(qkv, [ATT_Q_HEADS * ATT_HEAD_DIM, (ATT_Q_HEADS + ATT_KV_HEADS) * ATT_HEAD_DIM], axis=-1)
    q = q.reshape(b, s, ATT_Q_HEADS, ATT_HEAD_DIM)
    k = k.reshape(b, s, ATT_KV_HEADS, ATT_HEAD_DIM)
    v = v.reshape(b, s, ATT_KV_HEADS, ATT_HEAD_DIM)
    q = rms_norm(q, q_norm)
    k = rms_norm(k, k_norm)
    cos, sin = rope_tables(positions)
    q = apply_rope(q, cos, sin)
    k = apply_rope(k, cos, sin)

    nb = s // ATT_BLOCK
    qb = q.reshape(b, nb, ATT_BLOCK, ATT_KV_HEADS, ATT_GROUP, ATT_HEAD_DIM)
    kb = k.reshape(b, nb, ATT_BLOCK, ATT_KV_HEADS, ATT_HEAD_DIM)
    vb = v.reshape(b, nb, ATT_BLOCK, ATT_KV_HEADS, ATT_HEAD_DIM)
    shift = ((0, 0), (1, 0), (0, 0), (0, 0), (0, 0))
    kw = jnp.concatenate([jnp.pad(kb, shift)[:, :-1], kb], axis=2)
    vw = jnp.concatenate([jnp.pad(vb, shift)[:, :-1], vb], axis=2)

    scale = ATT_HEAD_DIM ** -0.5
    scores = jnp.einsum('bnqhgd,bnkhd->bnhgqk', qb, kw).astype(jnp.float32) * scale
    qi = jnp.arange(ATT_BLOCK)[:, None]
    kj = jnp.arange(2 * ATT_BLOCK)[None, :]
    band = (kj > qi) & (kj <= qi + ATT_BLOCK)
    not_before_start = (jnp.arange(nb) > 0)[:, None, None] | (kj >= ATT_BLOCK)[None]
    valid = band[None] & not_before_start
    scores = jnp.where(valid[None, :, None, None], scores, -jnp.inf)

    sink = sinks.astype(jnp.float32).reshape(ATT_KV_HEADS, ATT_GROUP)[None, None, :, :, None, None]
    m = jnp.maximum(jnp.max(scores, axis=-1, keepdims=True), sink)
    p = jnp.exp(scores - m)
    probs = p / (jnp.sum(p, axis=-1, keepdims=True) + jnp.exp(sink - m))
    out = jnp.einsum('bnhgqk,bnkhd->bnqhgd', probs.astype(vw.dtype), vw)
    return out.reshape(b, s, ATT_Q_HEADS * ATT_HEAD_DIM) @ w_o


def causal_depthwise_conv(u, w, bias):
    out = lax.conv_general_dilated(u, w[:, None, :].astype(u.dtype), window_strides=(1,),
                                   padding=[(SSM_CONV - 1, 0)],
                                   dimension_numbers=('NWC', 'WIO', 'NWC'),
                                   feature_group_count=u.shape[-1])
    return out + bias.astype(u.dtype)


def mamba2_ssd(h, w_in, conv_w, conv_b, dt_bias, a_log, d_skip, norm_g, w_out):
    b, s, _ = h.shape
    G, R, P, N, L = SSM_GROUPS, SSM_HEADS_PER_GROUP, SSM_HEAD_DIM, SSM_STATE, SSM_CHUNK
    zxbcdt = h @ w_in
    z = zxbcdt[..., :SSM_D_INNER]
    xbc = zxbcdt[..., SSM_D_INNER:SSM_D_INNER + SSM_CONV_DIM]
    dt = zxbcdt[..., SSM_D_INNER + SSM_CONV_DIM:]
    xbc = jax.nn.silu(causal_depthwise_conv(xbc, conv_w, conv_b))
    xs, bm, cm = jnp.split(xbc, [SSM_D_INNER, SSM_D_INNER + G * N], axis=-1)

    xs = xs.astype(jnp.float32).reshape(b, s, G, R, P)
    bm = bm.astype(jnp.float32).reshape(b, s, G, N)
    cm = cm.astype(jnp.float32).reshape(b, s, G, N)
    dt = jax.nn.softplus(dt.astype(jnp.float32) + dt_bias.astype(jnp.float32)).reshape(b, s, G, R)
    a = -jnp.exp(a_log.astype(jnp.float32)).reshape(G, R)

    pad = (-s) % L
    nc = (s + pad) // L

    def to_chunks(t):
        t = jnp.pad(t, [(0, 0), (0, pad)] + [(0, 0)] * (t.ndim - 2))
        return jnp.moveaxis(t.reshape((b, nc, L) + t.shape[2:]), 1, 0)

    causal = jnp.tril(jnp.ones((L, L), dtype=bool))[None, :, :, None, None]

    def chunk_step(state, inp):
        xc, dtc, bc, cc = inp
        acum = jnp.cumsum(dtc * a, axis=1)
        seg = acum[:, :, None] - acum[:, None, :]
        decay = jnp.exp(jnp.where(causal, seg, -jnp.inf))
        xdt = xc * dtc[..., None]
        cb = jnp.einsum('blgn,bsgn->blsg', cc, bc)
        y_diag = jnp.einsum('blsgr,bsgrp->blgrp', cb[..., None] * decay, xdt)
        y_off = jnp.einsum('blgn,bgrpn->blgrp', cc, state) * jnp.exp(acum)[..., None]
        to_end = jnp.exp(acum[:, -1:] - acum)
        new_state = (state * jnp.exp(acum[:, -1])[..., None, None]
                     + jnp.einsum('bsgn,bsgr,bsgrp->bgrpn', bc, to_end, xdt))
        return new_state, y_diag + y_off

    init = jnp.zeros((b, G, R, P, N), jnp.float32)
    _, y = lax.scan(chunk_step, init, (to_chunks(xs), to_chunks(dt), to_chunks(bm), to_chunks(cm)))
    y = jnp.moveaxis(y, 0, 1).reshape(b, nc * L, G, R, P)[:, :s]
    y = y + d_skip.astype(jnp.float32).reshape(G, R)[:, :, None] * xs
    y = y.reshape(b, s, SSM_D_INNER) * jax.nn.silu(z.astype(jnp.float32))
    y = y.reshape(b, s, G, SSM_NORM_GROUP)
    y = y * lax.rsqrt(jnp.mean(y * y, axis=-1, keepdims=True) + EPS)
    y = y.reshape(b, s, SSM_D_INNER) * norm_g.astype(jnp.float32)
    return y.astype(h.dtype) @ w_out


def swiglu(h, w_gate, w_up, w_down):
    return (jax.nn.silu(h @ w_gate) * (h @ w_up)) @ w_down


def _fwd_setup_inputs(seed: int = 0) -> dict:
    key = jax.random.key(seed)
    ks = jax.random.split(key, 24)
    f32 = jnp.float32
    resid = (2 * DEPTH) ** -0.5

    def nrm(k, shape, fan_in, scale=1.0):
        return jax.random.normal(k, shape, f32) * (scale * fan_in ** -0.5)

    def gain(k, shape):
        return 1.0 + 0.02 * jax.random.normal(k, shape, f32)

    x = jax.random.normal(ks[0], (BATCH, SEQ, D_MODEL), f32)
    start = jax.random.randint(ks[1], (BATCH, 1), 0, 4096)
    positions = (start + jnp.arange(SEQ)[None, :]).astype(jnp.int32)

    dt0 = jnp.exp(jax.random.uniform(ks[14], (N_SSM_LAYERS, SSM_HEADS), f32)
                  * (math.log(0.1) - math.log(0.001)) + math.log(0.001))
    return {
        'x': x,
        'positions': positions,
        'mixer_norm': gain(ks[2], (DEPTH, D_MODEL)),
        'ffn_norm': gain(ks[3], (DEPTH, D_MODEL)),
        'attn_w_qkv': nrm(ks[4], (N_ATTN_LAYERS, D_MODEL, QKV_WIDTH), D_MODEL),
        'attn_q_norm': gain(ks[5], (N_ATTN_LAYERS, ATT_HEAD_DIM)),
        'attn_k_norm': gain(ks[6], (N_ATTN_LAYERS, ATT_HEAD_DIM)),
        'attn_sinks': 0.5 * jax.random.normal(ks[7], (N_ATTN_LAYERS, ATT_Q_HEADS), f32),
        'attn_w_o': nrm(ks[8], (N_ATTN_LAYERS, ATT_Q_HEADS * ATT_HEAD_DIM, D_MODEL), ATT_Q_HEADS * ATT_HEAD_DIM, resid),
        'ssm_w_in': nrm(ks[9], (N_SSM_LAYERS, D_MODEL, SSM_IN_WIDTH), D_MODEL),
        'ssm_conv_w': nrm(ks[10], (N_SSM_LAYERS, SSM_CONV, SSM_CONV_DIM), SSM_CONV),
        'ssm_conv_b': 0.02 * jax.random.normal(ks[11], (N_SSM_LAYERS, SSM_CONV_DIM), f32),
        'ssm_dt_bias': dt0 + jnp.log(-jnp.expm1(-dt0)),
        'ssm_a_log': jnp.log(jax.random.uniform(ks[12], (N_SSM_LAYERS, SSM_HEADS), f32, 1.0, 16.0)),
        'ssm_d': gain(ks[13], (N_SSM_LAYERS, SSM_HEADS)),
        'ssm_norm': gain(ks[15], (N_SSM_LAYERS, SSM_D_INNER)),
        'ssm_w_out': nrm(ks[16], (N_SSM_LAYERS, SSM_D_INNER, D_MODEL), SSM_D_INNER, resid),
        'ffn_w_gate': nrm(ks[17], (DEPTH, D_MODEL, D_FF), D_MODEL),
        'ffn_w_up': nrm(ks[18], (DEPTH, D_MODEL, D_FF), D_MODEL),
        'ffn_w_down': nrm(ks[19], (DEPTH, D_FF, D_MODEL), D_FF, resid),
    }


def _fwd_reference(x, positions, mixer_norm, ffn_norm, attn_w_qkv, attn_q_norm, attn_k_norm, attn_sinks,
              attn_w_o, ssm_w_in, ssm_conv_w, ssm_conv_b, ssm_dt_bias, ssm_a_log, ssm_d, ssm_norm,
              ssm_w_out, ffn_w_gate, ffn_w_up, ffn_w_down):
    for i in range(DEPTH):
        h = rms_norm(x, mixer_norm[i])
        j = i // NUM_MIXERS
        if i % NUM_MIXERS == 0:
            x = x + sliding_window_attention(h, positions, attn_w_qkv[j], attn_q_norm[j], attn_k_norm[j],
                                             attn_sinks[j], attn_w_o[j])
        else:
            x = x + mamba2_ssd(h, ssm_w_in[j], ssm_conv_w[j], ssm_conv_b[j], ssm_dt_bias[j], ssm_a_log[j],
                               ssm_d[j], ssm_norm[j], ssm_w_out[j])
        x = x + swiglu(rms_norm(x, ffn_norm[i]), ffn_w_gate[i], ffn_w_up[i], ffn_w_down[i])
    return x


import jax as _jax
import jax.numpy as _jnp

TWIN_FORMAT = 'train_step'
FWD_PARAMS = ['x', 'positions', 'mixer_norm', 'ffn_norm', 'attn_w_qkv', 'attn_q_norm', 'attn_k_norm', 'attn_sinks', 'attn_w_o', 'ssm_w_in', 'ssm_conv_w', 'ssm_conv_b', 'ssm_dt_bias', 'ssm_a_log', 'ssm_d', 'ssm_norm', 'ssm_w_out', 'ffn_w_gate', 'ffn_w_up', 'ffn_w_down']
TWIN_WEIGHTS = ['mixer_norm', 'ffn_norm', 'attn_w_qkv', 'attn_q_norm', 'attn_k_norm', 'attn_sinks', 'attn_w_o', 'ssm_w_in', 'ssm_conv_w', 'ssm_conv_b', 'ssm_dt_bias', 'ssm_a_log', 'ssm_d', 'ssm_norm', 'ssm_w_out', 'ffn_w_gate', 'ffn_w_up', 'ffn_w_down']
TWIN_DIFF_INPUT = 'x'
TWIN_INPUTS = ['x', 'positions', 'mixer_norm', 'ffn_norm', 'attn_w_qkv', 'attn_q_norm', 'attn_k_norm', 'attn_sinks', 'attn_w_o', 'ssm_w_in', 'ssm_conv_w', 'ssm_conv_b', 'ssm_dt_bias', 'ssm_a_log', 'ssm_d', 'ssm_norm', 'ssm_w_out', 'ffn_w_gate', 'ffn_w_up', 'ffn_w_down', 'loss_target', 'm_mixer_norm', 'm_ffn_norm', 'm_attn_w_qkv', 'm_attn_q_norm', 'm_attn_k_norm', 'm_attn_sinks', 'm_attn_w_o', 'm_ssm_w_in', 'm_ssm_conv_w', 'm_ssm_conv_b', 'm_ssm_dt_bias', 'm_ssm_a_log', 'm_ssm_d', 'm_ssm_norm', 'm_ssm_w_out', 'm_ffn_w_gate', 'm_ffn_w_up', 'm_ffn_w_down', 'v_mixer_norm', 'v_ffn_norm', 'v_attn_w_qkv', 'v_attn_q_norm', 'v_attn_k_norm', 'v_attn_sinks', 'v_attn_w_o', 'v_ssm_w_in', 'v_ssm_conv_w', 'v_ssm_conv_b', 'v_ssm_dt_bias', 'v_ssm_a_log', 'v_ssm_d', 'v_ssm_norm', 'v_ssm_w_out', 'v_ffn_w_gate', 'v_ffn_w_up', 'v_ffn_w_down']
TWIN_OUTPUTS = ['loss', 'grad_x', 'grad_mixer_norm', 'grad_ffn_norm', 'grad_attn_w_qkv', 'grad_attn_q_norm', 'grad_attn_k_norm', 'grad_attn_sinks', 'grad_attn_w_o', 'grad_ssm_w_in', 'grad_ssm_conv_w', 'grad_ssm_conv_b', 'grad_ssm_dt_bias', 'grad_ssm_a_log', 'grad_ssm_d', 'grad_ssm_norm', 'grad_ssm_w_out', 'grad_ffn_w_gate', 'grad_ffn_w_up', 'grad_ffn_w_down', 'delta_mixer_norm', 'delta_ffn_norm', 'delta_attn_w_qkv', 'delta_attn_q_norm', 'delta_attn_k_norm', 'delta_attn_sinks', 'delta_attn_w_o', 'delta_ssm_w_in', 'delta_ssm_conv_w', 'delta_ssm_conv_b', 'delta_ssm_dt_bias', 'delta_ssm_a_log', 'delta_ssm_d', 'delta_ssm_norm', 'delta_ssm_w_out', 'delta_ffn_w_gate', 'delta_ffn_w_up', 'delta_ffn_w_down', 'new_m_mixer_norm', 'new_m_ffn_norm', 'new_m_attn_w_qkv', 'new_m_attn_q_norm', 'new_m_attn_k_norm', 'new_m_attn_sinks', 'new_m_attn_w_o', 'new_m_ssm_w_in', 'new_m_ssm_conv_w', 'new_m_ssm_conv_b', 'new_m_ssm_dt_bias', 'new_m_ssm_a_log', 'new_m_ssm_d', 'new_m_ssm_norm', 'new_m_ssm_w_out', 'new_m_ffn_w_gate', 'new_m_ffn_w_up', 'new_m_ffn_w_down', 'new_v_mixer_norm', 'new_v_ffn_norm', 'new_v_attn_w_qkv', 'new_v_attn_q_norm', 'new_v_attn_k_norm', 'new_v_attn_sinks', 'new_v_attn_w_o', 'new_v_ssm_w_in', 'new_v_ssm_conv_w', 'new_v_ssm_conv_b', 'new_v_ssm_dt_bias', 'new_v_ssm_a_log', 'new_v_ssm_d', 'new_v_ssm_norm', 'new_v_ssm_w_out', 'new_v_ffn_w_gate', 'new_v_ffn_w_up', 'new_v_ffn_w_down']
TWIN_LEAF_KINDS = {'loss': 'loss', 'grad_x': 'grad_x', 'grad_mixer_norm': 'grad_w', 'grad_ffn_norm': 'grad_w', 'grad_attn_w_qkv': 'grad_w', 'grad_attn_q_norm': 'grad_w', 'grad_attn_k_norm': 'grad_w', 'grad_attn_sinks': 'grad_w', 'grad_attn_w_o': 'grad_w', 'grad_ssm_w_in': 'grad_w', 'grad_ssm_conv_w': 'grad_w', 'grad_ssm_conv_b': 'grad_w', 'grad_ssm_dt_bias': 'grad_w', 'grad_ssm_a_log': 'grad_w', 'grad_ssm_d': 'grad_w', 'grad_ssm_norm': 'grad_w', 'grad_ssm_w_out': 'grad_w', 'grad_ffn_w_gate': 'grad_w', 'grad_ffn_w_up': 'grad_w', 'grad_ffn_w_down': 'grad_w', 'delta_mixer_norm': 'delta_w', 'delta_ffn_norm': 'delta_w', 'delta_attn_w_qkv': 'delta_w', 'delta_attn_q_norm': 'delta_w', 'delta_attn_k_norm': 'delta_w', 'delta_attn_sinks': 'delta_w', 'delta_attn_w_o': 'delta_w', 'delta_ssm_w_in': 'delta_w', 'delta_ssm_conv_w': 'delta_w', 'delta_ssm_conv_b': 'delta_w', 'delta_ssm_dt_bias': 'delta_w', 'delta_ssm_a_log': 'delta_w', 'delta_ssm_d': 'delta_w', 'delta_ssm_norm': 'delta_w', 'delta_ssm_w_out': 'delta_w', 'delta_ffn_w_gate': 'delta_w', 'delta_ffn_w_up': 'delta_w', 'delta_ffn_w_down': 'delta_w', 'new_m_mixer_norm': 'new_m', 'new_m_ffn_norm': 'new_m', 'new_m_attn_w_qkv': 'new_m', 'new_m_attn_q_norm': 'new_m', 'new_m_attn_k_norm': 'new_m', 'new_m_attn_sinks': 'new_m', 'new_m_attn_w_o': 'new_m', 'new_m_ssm_w_in': 'new_m', 'new_m_ssm_conv_w': 'new_m', 'new_m_ssm_conv_b': 'new_m', 'new_m_ssm_dt_bias': 'new_m', 'new_m_ssm_a_log': 'new_m', 'new_m_ssm_d': 'new_m', 'new_m_ssm_norm': 'new_m', 'new_m_ssm_w_out': 'new_m', 'new_m_ffn_w_gate': 'new_m', 'new_m_ffn_w_up': 'new_m', 'new_m_ffn_w_down': 'new_m', 'new_v_mixer_norm': 'new_v', 'new_v_ffn_norm': 'new_v', 'new_v_attn_w_qkv': 'new_v', 'new_v_attn_q_norm': 'new_v', 'new_v_attn_k_norm': 'new_v', 'new_v_attn_sinks': 'new_v', 'new_v_attn_w_o': 'new_v', 'new_v_ssm_w_in': 'new_v', 'new_v_ssm_conv_w': 'new_v', 'new_v_ssm_conv_b': 'new_v', 'new_v_ssm_dt_bias': 'new_v', 'new_v_ssm_a_log': 'new_v', 'new_v_ssm_d': 'new_v', 'new_v_ssm_norm': 'new_v', 'new_v_ssm_w_out': 'new_v', 'new_v_ffn_w_gate': 'new_v', 'new_v_ffn_w_up': 'new_v', 'new_v_ffn_w_down': 'new_v'}


def _forward(args):
    return _fwd_reference(*[args[k] for k in FWD_PARAMS])


def _output_shape():
    def fwd():
        inp = _fwd_setup_inputs(0)
        return _fwd_reference(*[inp[k] for k in FWD_PARAMS])
    out = _jax.eval_shape(fwd)
    return out.shape, out.dtype

N_MICROBATCH = 1
ADAM_LR = 0.001
ADAM_B1 = 0.9
ADAM_B2 = 0.999
ADAM_EPS = 1e-08
ADAM_WD = 0.01
ADAM_STEP = 10
PER_EXAMPLE_BATCH_AXIS = {'x': 0, 'positions': 0, 'loss_target': 0}
SHARED_INPUTS = []
_WEIGHT_DTYPES = {'mixer_norm': _jnp.float32, 'ffn_norm': _jnp.float32, 'attn_w_qkv': _jnp.float32, 'attn_q_norm': _jnp.float32, 'attn_k_norm': _jnp.float32, 'attn_sinks': _jnp.float32, 'attn_w_o': _jnp.float32, 'ssm_w_in': _jnp.float32, 'ssm_conv_w': _jnp.float32, 'ssm_conv_b': _jnp.float32, 'ssm_dt_bias': _jnp.float32, 'ssm_a_log': _jnp.float32, 'ssm_d': _jnp.float32, 'ssm_norm': _jnp.float32, 'ssm_w_out': _jnp.float32, 'ffn_w_gate': _jnp.float32, 'ffn_w_up': _jnp.float32, 'ffn_w_down': _jnp.float32}
MOMENT_SCALE = {'mixer_norm': 1.515871e-01, 'ffn_norm': 6.174965e+00, 'attn_w_qkv': 4.486520e-02, 'attn_q_norm': 2.895645e+00, 'attn_k_norm': 2.904616e+00, 'attn_sinks': 1.702440e-01, 'attn_w_o': 6.302226e-02, 'ssm_w_in': 8.169029e-02, 'ssm_conv_w': 1.249057e-01, 'ssm_conv_b': 3.956674e-01, 'ssm_dt_bias': 1.602329e-01, 'ssm_a_log': 7.488287e-01, 'ssm_d': 8.340755e-01, 'ssm_norm': 5.344284e+00, 'ssm_w_out': 5.610258e-01, 'ffn_w_gate': 5.339361e-02, 'ffn_w_up': 5.333678e-02, 'ffn_w_down': 1.729728e-01}


def _to_microbatches(a, axis):
    t = _jnp.moveaxis(a, axis, 0)
    t = t.reshape((N_MICROBATCH, t.shape[0] // N_MICROBATCH) + t.shape[1:])
    return _jnp.moveaxis(t, 1, axis + 1)


def setup_inputs(seed: int = 0) -> dict:
    inp = _fwd_setup_inputs(seed)
    key = _jax.random.fold_in(_jax.random.key(seed), 7919)
    shape, _ = _output_shape()
    out = dict(inp)
    out["loss_target"] = _jax.random.normal(_jax.random.fold_in(key, 0), shape, _jnp.float32)
    for i, name in enumerate(TWIN_WEIGHTS):
        w = inp[name].astype(_jnp.float32)
        if MOMENT_SCALE is None:
            s = _jnp.sqrt(_jnp.mean(_jnp.square(w)) + 1e-30)
        else:
            s = MOMENT_SCALE[name]
        km, kv = _jax.random.split(_jax.random.fold_in(key, i + 1))
        out[name] = w
        out["m_" + name] = s * _jax.random.normal(km, w.shape, _jnp.float32)
        out["v_" + name] = (s * s) * _jax.random.uniform(kv, w.shape, _jnp.float32, 0.5, 1.5)
    if N_MICROBATCH > 1:
        for name, axis in PER_EXAMPLE_BATCH_AXIS.items():
            out[name] = _to_microbatches(out[name], axis)
    return {'x': out['x'], 'positions': out['positions'], 'mixer_norm': out['mixer_norm'], 'ffn_norm': out['ffn_norm'], 'attn_w_qkv': out['attn_w_qkv'], 'attn_q_norm': out['attn_q_norm'], 'attn_k_norm': out['attn_k_norm'], 'attn_sinks': out['attn_sinks'], 'attn_w_o': out['attn_w_o'], 'ssm_w_in': out['ssm_w_in'], 'ssm_conv_w': out['ssm_conv_w'], 'ssm_conv_b': out['ssm_conv_b'], 'ssm_dt_bias': out['ssm_dt_bias'], 'ssm_a_log': out['ssm_a_log'], 'ssm_d': out['ssm_d'], 'ssm_norm': out['ssm_norm'], 'ssm_w_out': out['ssm_w_out'], 'ffn_w_gate': out['ffn_w_gate'], 'ffn_w_up': out['ffn_w_up'], 'ffn_w_down': out['ffn_w_down'], 'loss_target': out['loss_target'], 'm_mixer_norm': out['m_mixer_norm'], 'm_ffn_norm': out['m_ffn_norm'], 'm_attn_w_qkv': out['m_attn_w_qkv'], 'm_attn_q_norm': out['m_attn_q_norm'], 'm_attn_k_norm': out['m_attn_k_norm'], 'm_attn_sinks': out['m_attn_sinks'], 'm_attn_w_o': out['m_attn_w_o'], 'm_ssm_w_in': out['m_ssm_w_in'], 'm_ssm_conv_w': out['m_ssm_conv_w'], 'm_ssm_conv_b': out['m_ssm_conv_b'], 'm_ssm_dt_bias': out['m_ssm_dt_bias'], 'm_ssm_a_log': out['m_ssm_a_log'], 'm_ssm_d': out['m_ssm_d'], 'm_ssm_norm': out['m_ssm_norm'], 'm_ssm_w_out': out['m_ssm_w_out'], 'm_ffn_w_gate': out['m_ffn_w_gate'], 'm_ffn_w_up': out['m_ffn_w_up'], 'm_ffn_w_down': out['m_ffn_w_down'], 'v_mixer_norm': out['v_mixer_norm'], 'v_ffn_norm': out['v_ffn_norm'], 'v_attn_w_qkv': out['v_attn_w_qkv'], 'v_attn_q_norm': out['v_attn_q_norm'], 'v_attn_k_norm': out['v_attn_k_norm'], 'v_attn_sinks': out['v_attn_sinks'], 'v_attn_w_o': out['v_attn_w_o'], 'v_ssm_w_in': out['v_ssm_w_in'], 'v_ssm_conv_w': out['v_ssm_conv_w'], 'v_ssm_conv_b': out['v_ssm_conv_b'], 'v_ssm_dt_bias': out['v_ssm_dt_bias'], 'v_ssm_a_log': out['v_ssm_a_log'], 'v_ssm_d': out['v_ssm_d'], 'v_ssm_norm': out['v_ssm_norm'], 'v_ssm_w_out': out['v_ssm_w_out'], 'v_ffn_w_gate': out['v_ffn_w_gate'], 'v_ffn_w_up': out['v_ffn_w_up'], 'v_ffn_w_down': out['v_ffn_w_down']}


def _loss(weights, diff, rest, loss_target):
    with _jax.named_scope("forward"):
        args = {**rest, TWIN_DIFF_INPUT: diff, **{k: w.astype(_WEIGHT_DTYPES[k]) for k, w in weights.items()}}
        y = _forward(args)
    with _jax.named_scope("loss_head"):
        err = _jnp.square(y.astype(_jnp.float32) - loss_target)
        return 0.5 * _jnp.sum(_jnp.mean(err, axis=-1)) if err.ndim else 0.5 * err


def _adamw(w, g, m, v):
    m = ADAM_B1 * m + (1.0 - ADAM_B1) * g
    v = ADAM_B2 * v + (1.0 - ADAM_B2) * _jnp.square(g)
    m_hat = m / (1.0 - ADAM_B1 ** ADAM_STEP)
    v_hat = v / (1.0 - ADAM_B2 ** ADAM_STEP)
    delta = -ADAM_LR * (m_hat / (_jnp.sqrt(v_hat) + ADAM_EPS) + ADAM_WD * w)
    return delta, m, v


def reference(x, positions, mixer_norm, ffn_norm, attn_w_qkv, attn_q_norm, attn_k_norm, attn_sinks, attn_w_o, ssm_w_in, ssm_conv_w, ssm_conv_b, ssm_dt_bias, ssm_a_log, ssm_d, ssm_norm, ssm_w_out, ffn_w_gate, ffn_w_up, ffn_w_down, loss_target, m_mixer_norm, m_ffn_norm, m_attn_w_qkv, m_attn_q_norm, m_attn_k_norm, m_attn_sinks, m_attn_w_o, m_ssm_w_in, m_ssm_conv_w, m_ssm_conv_b, m_ssm_dt_bias, m_ssm_a_log, m_ssm_d, m_ssm_norm, m_ssm_w_out, m_ffn_w_gate, m_ffn_w_up, m_ffn_w_down, v_mixer_norm, v_ffn_norm, v_attn_w_qkv, v_attn_q_norm, v_attn_k_norm, v_attn_sinks, v_attn_w_o, v_ssm_w_in, v_ssm_conv_w, v_ssm_conv_b, v_ssm_dt_bias, v_ssm_a_log, v_ssm_d, v_ssm_norm, v_ssm_w_out, v_ffn_w_gate, v_ffn_w_up, v_ffn_w_down):
    given = dict(x=x, positions=positions, mixer_norm=mixer_norm, ffn_norm=ffn_norm, attn_w_qkv=attn_w_qkv, attn_q_norm=attn_q_norm, attn_k_norm=attn_k_norm, attn_sinks=attn_sinks, attn_w_o=attn_w_o, ssm_w_in=ssm_w_in, ssm_conv_w=ssm_conv_w, ssm_conv_b=ssm_conv_b, ssm_dt_bias=ssm_dt_bias, ssm_a_log=ssm_a_log, ssm_d=ssm_d, ssm_norm=ssm_norm, ssm_w_out=ssm_w_out, ffn_w_gate=ffn_w_gate, ffn_w_up=ffn_w_up, ffn_w_down=ffn_w_down, loss_target=loss_target, m_mixer_norm=m_mixer_norm, m_ffn_norm=m_ffn_norm, m_attn_w_qkv=m_attn_w_qkv, m_attn_q_norm=m_attn_q_norm, m_attn_k_norm=m_attn_k_norm, m_attn_sinks=m_attn_sinks, m_attn_w_o=m_attn_w_o, m_ssm_w_in=m_ssm_w_in, m_ssm_conv_w=m_ssm_conv_w, m_ssm_conv_b=m_ssm_conv_b, m_ssm_dt_bias=m_ssm_dt_bias, m_ssm_a_log=m_ssm_a_log, m_ssm_d=m_ssm_d, m_ssm_norm=m_ssm_norm, m_ssm_w_out=m_ssm_w_out, m_ffn_w_gate=m_ffn_w_gate, m_ffn_w_up=m_ffn_w_up, m_ffn_w_down=m_ffn_w_down, v_mixer_norm=v_mixer_norm, v_ffn_norm=v_ffn_norm, v_attn_w_qkv=v_attn_w_qkv, v_attn_q_norm=v_attn_q_norm, v_attn_k_norm=v_attn_k_norm, v_attn_sinks=v_attn_sinks, v_attn_w_o=v_attn_w_o, v_ssm_w_in=v_ssm_w_in, v_ssm_conv_w=v_ssm_conv_w, v_ssm_conv_b=v_ssm_conv_b, v_ssm_dt_bias=v_ssm_dt_bias, v_ssm_a_log=v_ssm_a_log, v_ssm_d=v_ssm_d, v_ssm_norm=v_ssm_norm, v_ssm_w_out=v_ssm_w_out, v_ffn_w_gate=v_ffn_w_gate, v_ffn_w_up=v_ffn_w_up, v_ffn_w_down=v_ffn_w_down)
    weights = {n: given[n] for n in TWIN_WEIGHTS}
    shared = {n: given[n] for n in SHARED_INPUTS}
    per_example = {n: given[n] for n in ['x', 'positions']}
    grad_fn = _jax.value_and_grad(_loss, argnums=(0, 1))

    def one_microbatch(ex, loss_target):
        ex = dict(ex)
        diff = ex.pop(TWIN_DIFF_INPUT)
        return grad_fn(weights, diff, {**shared, **ex}, loss_target)

    if N_MICROBATCH == 1:
        loss, (grad_w, grad_x) = one_microbatch(per_example, given["loss_target"])
    else:
        def body(carry, xs):
            loss_sum, grad_sum = carry
            l_k, (gw_k, gx_k) = one_microbatch(xs[0], xs[1])
            with _jax.named_scope("update"):
                return (loss_sum + l_k, _jax.tree.map(_jnp.add, grad_sum, gw_k)), gx_k

        init = (_jnp.zeros((), _jnp.float32), _jax.tree.map(_jnp.zeros_like, weights))
        (loss, grad_w), grad_x = _jax.lax.scan(body, init, (per_example, given["loss_target"]))
    with _jax.named_scope("update"):
        delta_w, new_m, new_v = {}, {}, {}
        for n in TWIN_WEIGHTS:
            delta_w[n], new_m[n], new_v[n] = _adamw(weights[n], grad_w[n], given["m_" + n], given["v_" + n])
    return (loss, grad_x, *[grad_w[n] for n in TWIN_WEIGHTS], *[delta_w[n] for n in TWIN_WEIGHTS],
            *[new_m[n] for n in TWIN_WEIGHTS], *[new_v[n] for n in TWIN_WEIGHTS])
```

```python
import functools
import math

import jax
import jax.numpy as jnp
from jax import lax
from jax.experimental import pallas as pl
from jax.experimental.pallas import tpu as pltpu

F32 = jnp.float32
BF16 = jnp.bfloat16

D_MODEL = 2048
EPS = 1e-6
ATT_HEAD_DIM = 64
ATT_Q_HEADS = 32
ATT_KV_HEADS = 4
ATT_GROUP = 8
ATT_BLOCK = 128
ROPE_THETA = 10000.0
Q_WIDTH = ATT_Q_HEADS * ATT_HEAD_DIM
KV_WIDTH = ATT_KV_HEADS * ATT_HEAD_DIM
SSM_D_INNER = 4096
SSM_HEADS = 64
SSM_GROUPS = 8
SSM_HPG = 8
SSM_P = 64
SSM_STATE = 128
SSM_CONV = 4
SSM_CHUNK = 256
SSM_CONV_DIM = 6144
SSM_GN = SSM_D_INNER // SSM_GROUPS
D_FF = 5632
LANES = 128
N_SHARDS = 4
N_DEV = 8

ADAM_LR = 0.001
ADAM_B1 = 0.9
ADAM_B2 = 0.999
ADAM_EPS = 1e-08
ADAM_WD = 0.01
ADAM_STEP = 10

VMEM_LIMIT = 56 * 1024 * 1024
MESH = pl.DeviceIdType.MESH


def _params(*sem):
    return pltpu.CompilerParams(dimension_semantics=sem, vmem_limit_bytes=VMEM_LIMIT)


def _tile(dim, target, unit=LANES):
    if dim <= target:
        return dim
    t = (target // unit) * unit
    while t >= unit:
        if dim % t == 0:
            return t
        t -= unit
    return dim


def _dot(a, b):
    return lax.dot_general(a, b, (((1,), (0,)), ((), ())), preferred_element_type=F32)


def _dot_nt(a, b):
    return lax.dot_general(a, b, (((1,), (1,)), ((), ())), preferred_element_type=F32)


def _dot_tn(a, b):
    return lax.dot_general(a, b, (((0,), (0,)), ((), ())), preferred_element_type=F32)


def _split3(x):
    hi = x.astype(BF16)
    r1 = x - hi.astype(F32)
    mid = r1.astype(BF16)
    lo = (r1 - mid.astype(F32)).astype(BF16)
    return hi, mid, lo


def _dot_x(x, m):
    hi, mid, lo = _split3(x)
    return _dot(hi, m) + _dot(mid, m) + _dot(lo, m)


def _xdot(m, x):
    hi, mid, lo = _split3(x)
    return _dot(m, hi) + _dot(m, mid) + _dot(m, lo)


def _dot_x_nt(x, m):
    hi, mid, lo = _split3(x)
    return _dot_nt(hi, m) + _dot_nt(mid, m) + _dot_nt(lo, m)


def _iota(shape, dim):
    return lax.broadcasted_iota(jnp.int32, shape, dim)


def _sigmoid(x):
    return 1.0 / (1.0 + jnp.exp(-x))


def _softplus(x):
    return jnp.maximum(x, 0.0) + jnp.log(1.0 + jnp.exp(-jnp.abs(x)))


def _mm(a, b, mode, name, add=None, out_dtype=F32):
    if mode == "nn":
        (m, k), (k2, n) = a.shape, b.shape
    elif mode == "nt":
        (m, k), (n, k2) = a.shape, b.shape
    else:
        (k, m), (k2, n) = a.shape, b.shape
    assert k == k2, (a.shape, b.shape, mode)
    tm, tn, tk = _tile(m, 1024), _tile(n, 1024), _tile(k, 1024)
    nk = k // tk
    if mode == "tn":
        a_spec = pl.BlockSpec((tk, tm), lambda i, j, q: (q, i))
    else:
        a_spec = pl.BlockSpec((tm, tk), lambda i, j, q: (i, q))
    if mode == "nt":
        b_spec = pl.BlockSpec((tn, tk), lambda i, j, q: (j, q))
    else:
        b_spec = pl.BlockSpec((tk, tn), lambda i, j, q: (q, j))
    o_spec = pl.BlockSpec((tm, tn), lambda i, j, q: (i, j))
    dot = {"nn": _dot, "nt": _dot_nt, "tn": _dot_tn}[mode]
    has_add = add is not None

    def body(*refs):
        if has_add:
            a_ref, b_ref, add_ref, o_ref, acc_ref = refs
        else:
            a_ref, b_ref, o_ref, acc_ref = refs
            add_ref = None
        part = dot(a_ref[...].astype(BF16), b_ref[...].astype(BF16))

        def finish(total):
            if has_add:
                total = total + add_ref[...].astype(F32)
            o_ref[...] = total.astype(out_dtype)

        if nk == 1:
            finish(part)
        else:
            q = pl.program_id(2)

            @pl.when(q == 0)
            def _():
                acc_ref[...] = part

            @pl.when(jnp.logical_and(q > 0, q < nk - 1))
            def _():
                acc_ref[...] += part

            @pl.when(q == nk - 1)
            def _():
                finish(acc_ref[...] + part)

    in_specs = [a_spec, b_spec] + ([o_spec] if has_add else [])
    args = (a, b) + ((add,) if has_add else ())
    return pl.pallas_call(
        body, name=name, grid=(m // tm, n // tn, nk),
        in_specs=in_specs, out_specs=o_spec,
        out_shape=jax.ShapeDtypeStruct((m, n), out_dtype),
        scratch_shapes=[pltpu.VMEM((tm, tn) if nk > 1 else (8, LANES), F32)],
        compiler_params=_params("parallel", "parallel", "arbitrary"),
    )(*args)


def _rms_fwd(x, g, name):
    s, d = x.shape
    ts = _tile(s, 512, 8)

    def body(x_ref, g_ref, o_ref):
        xv = x_ref[...]
        r = lax.rsqrt(jnp.mean(xv * xv, axis=-1, keepdims=True) + EPS)
        o_ref[...] = (xv * r * g_ref[...]).astype(BF16)

    return pl.pallas_call(
        body, name=name, grid=(s // ts,),
        in_specs=[pl.BlockSpec((ts, d), lambda i: (i, 0)), pl.BlockSpec((1, d), lambda i: (0, 0))],
        out_specs=pl.BlockSpec((ts, d), lambda i: (i, 0)),
        out_shape=jax.ShapeDtypeStruct((s, d), BF16),
        compiler_params=_params("parallel"),
    )(x, g)


def _rms_bwd(x, g, dh, dres, name):
    s, d = x.shape
    ts = _tile(s, 512, 8)

    def body(x_ref, g_ref, dh_ref, dres_ref, dx_ref, dg_ref):
        xv = x_ref[...]
        r = lax.rsqrt(jnp.mean(xv * xv, axis=-1, keepdims=True) + EPS)
        xhat = xv * r
        dhv = dh_ref[...].astype(F32)
        part = jnp.sum(dhv * xhat, axis=0, keepdims=True)

        @pl.when(pl.program_id(0) == 0)
        def _():
            dg_ref[...] = part

        @pl.when(pl.program_id(0) > 0)
        def _():
            dg_ref[...] += part

        dxh = dhv * g_ref[...]
        dx = r * (dxh - xhat * jnp.mean(dxh * xhat, axis=-1, keepdims=True))
        dx_ref[...] = dres_ref[...] + dx

    row = pl.BlockSpec((ts, d), lambda i: (i, 0))
    vec = pl.BlockSpec((1, d), lambda i: (0, 0))
    return pl.pallas_call(
        body, name=name, grid=(s // ts,),
        in_specs=[row, vec, row, row], out_specs=[row, vec],
        out_shape=[jax.ShapeDtypeStruct((s, d), F32), jax.ShapeDtypeStruct((1, d), F32)],
        compiler_params=_params("arbitrary"),
    )(x, g, dh, dres)


def _act_fwd(g, u, name):
    s, f = g.shape
    ts, tf = _tile(s, 512, 8), _tile(f, 1408)

    def body(g_ref, u_ref, o_ref):
        gv = g_ref[...]
        o_ref[...] = (gv * _sigmoid(gv) * u_ref[...]).astype(BF16)

    blk = pl.BlockSpec((ts, tf), lambda i, j: (i, j))
    return pl.pallas_call(
        body, name=name, grid=(s // ts, f // tf), in_specs=[blk, blk], out_specs=blk,
        out_shape=jax.ShapeDtypeStruct((s, f), BF16), compiler_params=_params("parallel", "parallel"),
    )(g, u)


def _act_bwd(g, u, da, name):
    s, f = g.shape
    ts, tf = _tile(s, 512, 8), _tile(f, 1408)

    def body(g_ref, u_ref, da_ref, dg_ref, du_ref):
        gv, uv, dav = g_ref[...], u_ref[...], da_ref[...].astype(F32)
        sg = _sigmoid(gv)
        silu = gv * sg
        du_ref[...] = (dav * silu).astype(BF16)
        dg_ref[...] = (dav * uv * sg * (1.0 + gv * (1.0 - sg))).astype(BF16)

    blk = pl.BlockSpec((ts, tf), lambda i, j: (i, j))
    return pl.pallas_call(
        body, name=name, grid=(s // ts, f // tf), in_specs=[blk, blk, blk], out_specs=[blk, blk],
        out_shape=[jax.ShapeDtypeStruct((s, f), BF16)] * 2, compiler_params=_params("parallel", "parallel"),
    )(g, u, da)


def _loss_fwd_bwd(y, target):
    s, d = y.shape
    ts = _tile(s, 512, 8)

    def body(y_ref, t_ref, l_ref, dy_ref):
        diff = y_ref[...] - t_ref[...]
        dy_ref[...] = diff * (1.0 / d)
        part = jnp.full((1, LANES), 0.5 * jnp.sum(jnp.mean(diff * diff, axis=-1, keepdims=True)), F32)

        @pl.when(pl.program_id(0) == 0)
        def _():
            l_ref[...] = part

        @pl.when(pl.program_id(0) > 0)
        def _():
            l_ref[...] += part

    row = pl.BlockSpec((ts, d), lambda i: (i, 0))
    acc = pl.BlockSpec((1, LANES), lambda i: (0, 0))
    return pl.pallas_call(
        body, name="loss", grid=(s // ts,), in_specs=[row, row], out_specs=[acc, row],
        out_shape=[jax.ShapeDtypeStruct((1, LANES), F32), jax.ShapeDtypeStruct((s, d), F32)],
        compiler_params=_params("arbitrary"),
    )(y, target)


def _lane_consts():
    r, c = _iota((LANES, LANES), 0), _iota((LANES, LANES), 1)
    same = (r >> 6) == (c >> 6)
    rin, cin = r & 63, c & 63
    one = lambda cond: jnp.where(cond, 1.0, 0.0).astype(BF16)
    return dict(
        seg=one(same),
        rot=(jnp.where(same & (rin == cin + 32), -1.0, 0.0)
             + jnp.where(same & (cin == rin + 32), 1.0, 0.0)).astype(BF16),
        dup_lo=one(r == cin), dup_hi=one(r == cin + 64),
        up=one((c >= 64) & (r == c - 64)), down=one((c < 64) & (r == c + 64)),
        fold_lo=one((c < 64) & (rin == c)), fold_hi=one((c >= 64) & (rin == c - 64)),
    )


def _norm_rope(xc, gain, cos, sin, k):
    ss = _dot_x(xc * xc, k["seg"])
    rinv = lax.rsqrt(ss * (1.0 / ATT_HEAD_DIM) + EPS)
    xhat = xc * rinv
    y = xhat * gain
    return y * cos + _dot_x(y, k["rot"]) * sin, xhat, rinv


def _norm_rope_bwd(dr, xhat, rinv, gain, cos, sin, k):
    dy = dr * cos - _dot_x(dr * sin, k["rot"])
    dgain = jnp.sum(dy * xhat, axis=0, keepdims=True)
    dxh = dy * gain
    dx = rinv * (dxh - xhat * (_dot_x(dxh * xhat, k["seg"]) * (1.0 / ATT_HEAD_DIM)))
    return dx, dgain


def _attn_prep(qkv, cos, sin, gq, gk):
    s = qkv.shape[0]
    tr = _tile(s, 256, 8)

    def body(x_ref, cos_ref, sin_ref, gq_ref, gk_ref, q_ref, kk_ref, vlo_ref, vhi_ref):
        k = _lane_consts()
        cosv, sinv = cos_ref[...], sin_ref[...]
        lane = _iota((tr, LANES), 1)
        for j in range(Q_WIDTH // LANES):
            r, _, _ = _norm_rope(x_ref[:, j * LANES:(j + 1) * LANES], gq_ref[...], cosv, sinv, k)
            q_ref[:, j * LANES:(j + 1) * LANES] = r.astype(BF16)
        for i in range(KV_WIDTH // LANES):
            off = Q_WIDTH + i * LANES
            r, _, _ = _norm_rope(x_ref[:, off:off + LANES], gk_ref[...], cosv, sinv, k)
            rb = r.astype(BF16)
            kk_ref[:, (2 * i) * LANES:(2 * i + 1) * LANES] = _dot(rb, k["dup_lo"]).astype(BF16)
            kk_ref[:, (2 * i + 1) * LANES:(2 * i + 2) * LANES] = _dot(rb, k["dup_hi"]).astype(BF16)
            off = Q_WIDTH + KV_WIDTH + i * LANES
            vb = x_ref[:, off:off + LANES].astype(BF16)
            zero = jnp.zeros_like(vb)
            vlo_ref[:, (2 * i) * LANES:(2 * i + 1) * LANES] = jnp.where(lane < 64, vb, zero)
            vhi_ref[:, (2 * i) * LANES:(2 * i + 1) * LANES] = _dot(vb, k["up"]).astype(BF16)
            vlo_ref[:, (2 * i + 1) * LANES:(2 * i + 2) * LANES] = _dot(vb, k["down"]).astype(BF16)
            vhi_ref[:, (2 * i + 1) * LANES:(2 * i + 2) * LANES] = jnp.where(lane >= 64, vb, zero)

    w = qkv.shape[1]
    row = lambda width: pl.BlockSpec((tr, width), lambda i: (i, 0))
    vec = pl.BlockSpec((1, LANES), lambda i: (0, 0))
    kw = ATT_KV_HEADS * LANES
    return pl.pallas_call(
        body, name="attn_prep", grid=(s // tr,),
        in_specs=[row(w), row(LANES), row(LANES), vec, vec],
        out_specs=[row(Q_WIDTH), row(kw), row(kw), row(kw)],
        out_shape=[jax.ShapeDtypeStruct((s, Q_WIDTH), BF16)] + [jax.ShapeDtypeStruct((s, kw), BF16)] * 3,
        compiler_params=_params("parallel"),
    )(qkv, cos, sin, gq, gk)


def _band_mask(n):
    qi = _iota((ATT_BLOCK, 2 * ATT_BLOCK), 0)
    kj = _iota((ATT_BLOCK, 2 * ATT_BLOCK), 1)
    band = (kj > qi) & (kj <= qi + ATT_BLOCK)
    return band & ((kj >= ATT_BLOCK) | (n > 0))


def _softmax_sink(s, valid, sink):
    s = jnp.where(valid, s, -jnp.inf)
    m = jnp.maximum(jnp.max(s, axis=-1, keepdims=True), sink)
    p = jnp.exp(s - m)
    esink = jnp.exp(sink - m)
    inv = 1.0 / (jnp.sum(p, axis=-1, keepdims=True) + esink)
    return p * inv, esink * inv


def _attn_specs(order):
    if order == "nh":
        cur = lambda n, h: (n, h)
        prev = lambda n, h: (jnp.maximum(n - 1, 0), h)
    else:
        cur = lambda h, n: (n, h)
        prev = lambda h, n: (jnp.maximum(n - 1, 0), h)
    qs = pl.BlockSpec((ATT_BLOCK, ATT_GROUP * ATT_HEAD_DIM), cur)
    kc = pl.BlockSpec((ATT_BLOCK, LANES), cur)
    kp = pl.BlockSpec((ATT_BLOCK, LANES), prev)
    return qs, kc, kp


def _attn_fwd(q, kk, vlo, vhi, sinks):
    s = q.shape[0]
    nb = s // ATT_BLOCK
    scale = ATT_HEAD_DIM ** -0.5

    def body(sink_ref, q_ref, kc_ref, kp_ref, vloc_ref, vlop_ref, vhic_ref, vhip_ref, o_ref):
        n, h = pl.program_id(0), pl.program_id(1)
        valid = _band_mask(n)
        kw = jnp.concatenate([kp_ref[...], kc_ref[...]], axis=0)
        vw = (jnp.concatenate([vlop_ref[...], vloc_ref[...]], axis=0),
              jnp.concatenate([vhip_ref[...], vhic_ref[...]], axis=0))
        lane = _iota((ATT_BLOCK, LANES), 1)
        for jp in range(ATT_GROUP // 2):
            qp = q_ref[:, jp * LANES:(jp + 1) * LANES]
            acc = jnp.zeros((ATT_BLOCK, LANES), F32)
            for hf in range(2):
                qm = jnp.where((lane >= 64) == (hf == 1), qp, jnp.zeros_like(qp))
                sc = _dot_nt(qm, kw) * scale
                probs, _ = _softmax_sink(sc, valid, sink_ref[h * ATT_GROUP + 2 * jp + hf])
                acc = acc + _dot(probs.astype(BF16), vw[hf])
            o_ref[:, jp * LANES:(jp + 1) * LANES] = acc.astype(BF16)

    qs, kc, kp = _attn_specs("nh")
    return pl.pallas_call(
        body, name="attn_fwd", grid=(nb, ATT_KV_HEADS),
        in_specs=[pl.BlockSpec(memory_space=pltpu.SMEM), qs, kc, kp, kc, kp, kc, kp],
        out_specs=qs, out_shape=jax.ShapeDtypeStruct((s, Q_WIDTH), BF16),
        compiler_params=_params("parallel", "parallel"),
    )(sinks, q, kk, kk, vlo, vlo, vhi, vhi)


def _attn_bwd(q, kk, vlo, vhi, sinks, do):
    s = q.shape[0]
    nb = s // ATT_BLOCK
    scale = ATT_HEAD_DIM ** -0.5

    def body(sink_ref, q_ref, kc_ref, kp_ref, vloc_ref, vlop_ref, vhic_ref, vhip_ref, do_ref,
             dq_ref, dkc_ref, dkp_ref, dvloc_ref, dvlop_ref, dvhic_ref, dvhip_ref, dsink_ref):
        h, n = pl.program_id(0), pl.program_id(1)
        valid = _band_mask(n)
        kw = jnp.concatenate([kp_ref[...], kc_ref[...]], axis=0)
        vw = (jnp.concatenate([vlop_ref[...], vloc_ref[...]], axis=0),
              jnp.concatenate([vhip_ref[...], vhic_ref[...]], axis=0))
        lane = _iota((ATT_BLOCK, LANES), 1)
        sub = _iota((ATT_GROUP, LANES), 0)
        dkk = jnp.zeros((2 * ATT_BLOCK, LANES), F32)
        dv = [jnp.zeros((2 * ATT_BLOCK, LANES), F32), jnp.zeros((2 * ATT_BLOCK, LANES), F32)]
        dsink = jnp.zeros((ATT_GROUP, LANES), F32)
        for jp in range(ATT_GROUP // 2):
            qp = q_ref[:, jp * LANES:(jp + 1) * LANES]
            dop = do_ref[:, jp * LANES:(jp + 1) * LANES]
            dq = jnp.zeros((ATT_BLOCK, LANES), F32)
            for hf in range(2):
                mine = (lane >= 64) == (hf == 1)
                qm = jnp.where(mine, qp, jnp.zeros_like(qp))
                sc = _dot_nt(qm, kw) * scale
                probs, psink = _softmax_sink(sc, valid, sink_ref[h * ATT_GROUP + 2 * jp + hf])
                pb = probs.astype(BF16)
                dprobs = _dot_nt(dop, vw[hf])
                dv[hf] = dv[hf] + _dot_tn(pb, dop)
                delta = jnp.sum(probs * dprobs, axis=-1, keepdims=True)
                ds = (probs * (dprobs - delta) * scale).astype(BF16)
                dsink = dsink + jnp.where(sub == 2 * jp + hf, -jnp.sum(psink * delta), 0.0)
                dq = dq + jnp.where(mine, _dot(ds, kw), 0.0)
                dkk = dkk + _dot_tn(ds, qm)
            dq_ref[:, jp * LANES:(jp + 1) * LANES] = dq
        dkp_ref[...], dkc_ref[...] = dkk[:ATT_BLOCK], dkk[ATT_BLOCK:]
        dvlop_ref[...], dvloc_ref[...] = dv[0][:ATT_BLOCK], dv[0][ATT_BLOCK:]
        dvhip_ref[...], dvhic_ref[...] = dv[1][:ATT_BLOCK], dv[1][ATT_BLOCK:]

        @pl.when(n == 0)
        def _():
            dsink_ref[0] = dsink

        @pl.when(n > 0)
        def _():
            dsink_ref[0] += dsink

    qs, kc, kp = _attn_specs("hn")
    kw_shape = jax.ShapeDtypeStruct((s, ATT_KV_HEADS * LANES), F32)
    return pl.pallas_call(
        body, name="attn_bwd", grid=(ATT_KV_HEADS, nb),
        in_specs=[pl.BlockSpec(memory_space=pltpu.SMEM), qs, kc, kp, kc, kp, kc, kp, qs],
        out_specs=[qs] + [kc] * 6 + [pl.BlockSpec((1, ATT_GROUP, LANES), lambda h, n: (h, 0, 0))],
        out_shape=[jax.ShapeDtypeStruct((s, Q_WIDTH), F32)] + [kw_shape] * 6
        + [jax.ShapeDtypeStruct((ATT_KV_HEADS, ATT_GROUP, LANES), F32)],
        compiler_params=_params("parallel", "arbitrary"),
    )(sinks, q, kk, kk, vlo, vlo, vhi, vhi, do)


def _attn_prep_bwd(qkv, cos, sin, gq, gk, dq, dks, dvlos, dvhis):
    s, w = qkv.shape
    tr = ATT_BLOCK
    nb = s // tr

    def body(x_ref, cos_ref, sin_ref, gq_ref, gk_ref, dq_ref, dkc_ref, dkn_ref, dvloc_ref, dvlon_ref,
             dvhic_ref, dvhin_ref, dx_ref, dgq_ref, dgk_ref):
        n = pl.program_id(0)
        k = _lane_consts()
        cosv, sinv = cos_ref[...], sin_ref[...]
        nxt = jnp.where(n < nb - 1, 1.0, 0.0)
        lane = _iota((tr, LANES), 1)
        dgq = jnp.zeros((1, LANES), F32)
        dgk = jnp.zeros((1, LANES), F32)
        for j in range(Q_WIDTH // LANES):
            sl = slice(j * LANES, (j + 1) * LANES)
            _, xhat, rinv = _norm_rope(x_ref[:, sl], gq_ref[...], cosv, sinv, k)
            dx, dg = _norm_rope_bwd(dq_ref[:, sl], xhat, rinv, gq_ref[...], cosv, sinv, k)
            dx_ref[:, sl] = dx.astype(BF16)
            dgq = dgq + dg
        for i in range(KV_WIDTH // LANES):
            a, b = slice(2 * i * LANES, (2 * i + 1) * LANES), slice((2 * i + 1) * LANES, (2 * i + 2) * LANES)
            dr = (_dot_x(dkc_ref[:, a] + nxt * dkn_ref[:, a], k["fold_lo"])
                  + _dot_x(dkc_ref[:, b] + nxt * dkn_ref[:, b], k["fold_hi"]))
            sl = slice(Q_WIDTH + i * LANES, Q_WIDTH + (i + 1) * LANES)
            _, xhat, rinv = _norm_rope(x_ref[:, sl], gk_ref[...], cosv, sinv, k)
            dx, dg = _norm_rope_bwd(dr, xhat, rinv, gk_ref[...], cosv, sinv, k)
            dx_ref[:, sl] = dx.astype(BF16)
            dgk = dgk + dg
            ta = jnp.where(lane < 64, dvloc_ref[:, a] + nxt * dvlon_ref[:, a], dvhic_ref[:, a] + nxt * dvhin_ref[:, a])
            tb = jnp.where(lane < 64, dvloc_ref[:, b] + nxt * dvlon_ref[:, b], dvhic_ref[:, b] + nxt * dvhin_ref[:, b])
            sl = slice(Q_WIDTH + KV_WIDTH + i * LANES, Q_WIDTH + KV_WIDTH + (i + 1) * LANES)
            dx_ref[:, sl] = (_dot_x(ta, k["fold_lo"]) + _dot_x(tb, k["fold_hi"])).astype(BF16)

        @pl.when(n == 0)
        def _():
            dgq_ref[...] = dgq
            dgk_ref[...] = dgk

        @pl.when(n > 0)
        def _():
            dgq_ref[...] += dgq
            dgk_ref[...] += dgk

    row = lambda width: pl.BlockSpec((tr, width), lambda i: (i, 0))
    nxt_row = pl.BlockSpec((tr, ATT_KV_HEADS * LANES), lambda i: (jnp.minimum(i + 1, nb - 1), 0))
    vec = pl.BlockSpec((1, LANES), lambda i: (0, 0))
    kw = ATT_KV_HEADS * LANES
    return pl.pallas_call(
        body, name="attn_prep_bwd", grid=(nb,),
        in_specs=[row(w), row(LANES), row(LANES), vec, vec, row(Q_WIDTH),
                  row(kw), nxt_row, row(kw), nxt_row, row(kw), nxt_row],
        out_specs=[row(w), vec, vec],
        out_shape=[jax.ShapeDtypeStruct((s, w), BF16), jax.ShapeDtypeStruct((1, LANES), F32),
                   jax.ShapeDtypeStruct((1, LANES), F32)],
        compiler_params=_params("arbitrary"),
    )(qkv, cos, sin, gq, gk, dq, dks[0], dks[1], dvlos[0], dvlos[1], dvhis[0], dvhis[1])


CONV_HALO = 8


def _conv_pre(ext, w_ref, b_ref, ts):
    pre = b_ref[...] + w_ref[SSM_CONV - 1:SSM_CONV, :] * ext[CONV_HALO:]
    for kk in range(SSM_CONV - 1):
        pre = pre + w_ref[kk:kk + 1, :] * pltpu.roll(ext, SSM_CONV - 1 - kk, 0)[CONV_HALO:]
    return pre


def _conv_fwd(u, w, b):
    s, c = u.shape
    ts, tc = _tile(s, 512, 8), _tile(c, 512)

    def body(u_ref, halo_ref, w_ref, b_ref, o_ref):
        halo = jnp.where(pl.program_id(1) > 0, halo_ref[...], 0.0)
        pre = _conv_pre(jnp.concatenate([halo, u_ref[...]], axis=0), w_ref, b_ref, ts)
        o_ref[...] = pre * _sigmoid(pre)

    blk = pl.BlockSpec((ts, tc), lambda j, i: (i, j))
    halo = pl.BlockSpec((CONV_HALO, tc), lambda j, i: (jnp.maximum(i * (ts // CONV_HALO) - 1, 0), j))
    return pl.pallas_call(
        body, name="conv_fwd", grid=(c // tc, s // ts),
        in_specs=[blk, halo, pl.BlockSpec((SSM_CONV, tc), lambda j, i: (0, j)), pl.BlockSpec((1, tc), lambda j, i: (0, j))],
        out_specs=blk, out_shape=jax.ShapeDtypeStruct((s, c), F32),
        compiler_params=_params("parallel", "parallel"),
    )(u, u, w, b)


def _conv_bwd_pre(u, w, b, dact):
    s, c = u.shape
    ts, tc = _tile(s, 512, 8), _tile(c, 512)

    def body(u_ref, halo_ref, w_ref, b_ref, da_ref, dpre_ref, dw_ref, db_ref):
        i = pl.program_id(1)
        halo = jnp.where(i > 0, halo_ref[...], 0.0)
        ext = jnp.concatenate([halo, u_ref[...]], axis=0)
        pre = _conv_pre(ext, w_ref, b_ref, ts)
        sg = _sigmoid(pre)
        dpre = da_ref[...] * sg * (1.0 + pre * (1.0 - sg))
        dpre_ref[...] = dpre
        rows = [jnp.sum(dpre * pltpu.roll(ext, SSM_CONV - 1 - kk, 0)[CONV_HALO:], axis=0, keepdims=True)
                for kk in range(SSM_CONV - 1)]
        rows.append(jnp.sum(dpre * ext[CONV_HALO:], axis=0, keepdims=True))
        dwp = jnp.concatenate(rows, axis=0)
        dbp = jnp.sum(dpre, axis=0, keepdims=True)

        @pl.when(i == 0)
        def _():
            dw_ref[...] = dwp
            db_ref[...] = dbp

        @pl.when(i > 0)
        def _():
            dw_ref[...] += dwp
            db_ref[...] += dbp

    blk = pl.BlockSpec((ts, tc), lambda j, i: (i, j))
    halo = pl.BlockSpec((CONV_HALO, tc), lambda j, i: (jnp.maximum(i * (ts // CONV_HALO) - 1, 0), j))
    wspec = pl.BlockSpec((SSM_CONV, tc), lambda j, i: (0, j))
    bspec = pl.BlockSpec((1, tc), lambda j, i: (0, j))
    return pl.pallas_call(
        body, name="conv_bwd_pre", grid=(c // tc, s // ts),
        in_specs=[blk, halo, wspec, bspec, blk], out_specs=[blk, wspec, bspec],
        out_shape=[jax.ShapeDtypeStruct((s, c), F32), jax.ShapeDtypeStruct((SSM_CONV, c), F32),
                   jax.ShapeDtypeStruct((1, c), F32)],
        compiler_params=_params("parallel", "arbitrary"),
    )(u, u, w, b, dact)


def _conv_bwd_in(dpre, w):
    s, c = dpre.shape
    ts, tc = _tile(s, 512, 8), _tile(c, 512)
    ns = s // ts

    def body(d_ref, halo_ref, w_ref, o_ref):
        halo = jnp.where(pl.program_id(1) < ns - 1, halo_ref[...], 0.0)
        ext = jnp.concatenate([d_ref[...], halo], axis=0)
        du = w_ref[SSM_CONV - 1:SSM_CONV, :] * ext[:ts]
        for kk in range(SSM_CONV - 1):
            du = du + w_ref[kk:kk + 1, :] * pltpu.roll(ext, ts + CONV_HALO - (SSM_CONV - 1 - kk), 0)[:ts]
        o_ref[...] = du.astype(BF16)

    blk = pl.BlockSpec((ts, tc), lambda j, i: (i, j))
    halo = pl.BlockSpec((CONV_HALO, tc), lambda j, i: (jnp.minimum((i + 1) * (ts // CONV_HALO), s // CONV_HALO - 1), j))
    return pl.pallas_call(
        body, name="conv_bwd_in", grid=(c // tc, ns),
        in_specs=[blk, halo, pl.BlockSpec((SSM_CONV, tc), lambda j, i: (0, j))],
        out_specs=blk, out_shape=jax.ShapeDtypeStruct((s, c), BF16),
        compiler_params=_params("parallel", "parallel"),
    )(dpre, dpre, w)


def _ssd_common(dt_ref, dtt_ref, bias_ref, biast_ref, alog_ref, alogt_ref):
    ln = SSM_CHUNK
    raw, rawt = dt_ref[0] + bias_ref[0], dtt_ref[0] + biast_ref[0]
    dt, dtt = _softplus(raw), _softplus(rawt)
    a, at = -jnp.exp(alog_ref[0]), -jnp.exp(alogt_ref[0])
    tri = jnp.where(_iota((ln, ln), 0) >= _iota((ln, ln), 1), 1.0, 0.0).astype(BF16)
    return dict(raw=raw, rawt=rawt, dt=dt, dtt=dtt, a=a, at=at, tri=tri,
                acum=_xdot(tri, dt * a), acumt=_dot_x_nt(dtt * at, tri))


def _ssd_specs(nc, rev):
    cidx = (lambda c: nc - 1 - c) if rev else (lambda c: c)
    ln = SSM_CHUNK
    xs = pl.BlockSpec((ln, SSM_GN), lambda g, c: (cidx(c), g))
    bs = pl.BlockSpec((ln, SSM_STATE), lambda g, c: (cidx(c), SSM_D_INNER // SSM_STATE + g))
    cs = pl.BlockSpec((ln, SSM_STATE), lambda g, c: (cidx(c), SSM_D_INNER // SSM_STATE + SSM_GROUPS + g))
    dt = pl.BlockSpec((1, ln, SSM_HPG), lambda g, c: (g, cidx(c), 0))
    dtt = pl.BlockSpec((1, SSM_HPG, ln), lambda g, c: (g, 0, cidx(c)))
    row = pl.BlockSpec((1, 1, SSM_HPG), lambda g, c: (g, 0, 0))
    col = pl.BlockSpec((1, SSM_HPG, 1), lambda g, c: (g, 0, 0))
    st = pl.BlockSpec((1, SSM_HPG, SSM_P, SSM_STATE), lambda g, c: (cidx(c), g, 0, 0))
    return xs, bs, cs, dt, dtt, row, col, st


def _ssd_fwd(xbc, dt_g, dt_gt, bias_r, bias_c, alog_r, alog_c, d_r):
    s = xbc.shape[0]
    ln = SSM_CHUNK
    nc = s // ln

    def body(x_ref, b_ref, c_ref, dt_ref, dtt_ref, bias_ref, biast_ref, alog_ref, alogt_ref, d_ref,
             y_ref, st_ref, state):
        @pl.when(pl.program_id(1) == 0)
        def _():
            state[...] = jnp.zeros_like(state)

        cm = _ssd_common(dt_ref, dtt_ref, bias_ref, biast_ref, alog_ref, alogt_ref)
        dt, acum, acumt = cm["dt"], cm["acum"], cm["acumt"]
        bb, cb = b_ref[...].astype(BF16), c_ref[...].astype(BF16)
        cbm = _dot_nt(cb, bb)
        causal = _iota((ln, ln), 0) >= _iota((ln, ln), 1)
        st_ref[0] = state[...]
        for r in range(SSM_HPG):
            xr = x_ref[:, r * SSM_P:(r + 1) * SSM_P]
            ac, last = acum[:, r:r + 1], acum[ln - 1:ln, r:r + 1]
            decay = jnp.exp(jnp.where(causal, ac - acumt[r:r + 1, :], -jnp.inf))
            xdt = xr * dt[:, r:r + 1]
            sr = state[r]
            y = (_dot((cbm * decay).astype(BF16), xdt.astype(BF16))
                 + _dot_nt(cb, sr.astype(BF16)) * jnp.exp(ac) + d_ref[0][:, r:r + 1] * xr)
            y_ref[:, r * SSM_P:(r + 1) * SSM_P] = y
            state[r] = sr * jnp.exp(last) + _dot_tn((xdt * jnp.exp(last - ac)).astype(BF16), bb)

    xs, bs, cs, dts, dtts, row, col, st = _ssd_specs(nc, False)
    return pl.pallas_call(
        body, name="ssd_fwd", grid=(SSM_GROUPS, nc),
        in_specs=[xs, bs, cs, dts, dtts, row, col, row, col, row],
        out_specs=[xs, st],
        out_shape=[jax.ShapeDtypeStruct((s, SSM_D_INNER), F32),
                   jax.ShapeDtypeStruct((nc, SSM_HEADS, SSM_P, SSM_STATE), F32)],
        scratch_shapes=[pltpu.VMEM((SSM_HPG, SSM_P, SSM_STATE), F32)],
        compiler_params=_params("parallel", "arbitrary"),
    )(xbc, xbc, xbc, dt_g, dt_gt, bias_r, bias_c, alog_r, alog_c, d_r)


def _ssd_bwd(xbc, dt_g, dt_gt, bias_r, bias_c, alog_r, alog_c, d_r, states, dy):
    s = xbc.shape[0]
    ln = SSM_CHUNK
    nc = s // ln

    def body(x_ref, b_ref, c_ref, dt_ref, dtt_ref, bias_ref, biast_ref, alog_ref, alogt_ref, d_ref,
             st_ref, dy_ref, dx_ref, db_ref, dc_ref, ddt_ref, ddtt_ref, dbias_ref, dbiast_ref,
             dalog_ref, dalogt_ref, dd_ref, dstate):
        step = pl.program_id(1)

        @pl.when(step == 0)
        def _():
            dstate[...] = jnp.zeros_like(dstate)

        cm = _ssd_common(dt_ref, dtt_ref, bias_ref, biast_ref, alog_ref, alogt_ref)
        dt, acum, acumt = cm["dt"], cm["acum"], cm["acumt"]
        bb, cb = b_ref[...].astype(BF16), c_ref[...].astype(BF16)
        cbm = _dot_nt(cb, bb)
        causal = _iota((ln, ln), 0) >= _iota((ln, ln), 1)
        lane8 = _iota((ln, SSM_HPG), 1)
        sub8 = _iota((SSM_HPG, ln), 0)
        lane1 = _iota((1, SSM_HPG), 1)
        is_last = _iota((ln, 1), 0) == ln - 1
        dcb = jnp.zeros((ln, ln), F32)
        dc_acc = jnp.zeros((ln, SSM_STATE), F32)
        db_acc = jnp.zeros((ln, SSM_STATE), F32)
        dac_rows = jnp.zeros((ln, SSM_HPG), F32)
        dac_cols = jnp.zeros((SSM_HPG, ln), F32)
        ddt_all = jnp.zeros((ln, SSM_HPG), F32)
        dd_all = jnp.zeros((1, SSM_HPG), F32)
        for r in range(SSM_HPG):
            sl = slice(r * SSM_P, (r + 1) * SSM_P)
            xr, dyr = x_ref[:, sl], dy_ref[:, sl]
            dtc, dr = dt[:, r:r + 1], d_ref[0][:, r:r + 1]
            ac, last = acum[:, r:r + 1], acum[ln - 1:ln, r:r + 1]
            decay = jnp.exp(jnp.where(causal, ac - acumt[r:r + 1, :], -jnp.inf))
            w = (cbm * decay).astype(BF16)
            xdt = xr * dtc
            xdtb, dyb = xdt.astype(BF16), dyr.astype(BF16)
            eac, to_end, elast = jnp.exp(ac), jnp.exp(last - ac), jnp.exp(last)
            sr, dsr = st_ref[0, r], dstate[r]
            srb, dsrb = sr.astype(BF16), dsr.astype(BF16)
            dxdt_state = _dot_nt(bb, dsrb) * to_end
            dxdt = _dot_tn(w, dyb) + dxdt_state
            dcb_r = _dot_nt(dyb, xdtb) * decay
            dcb = dcb + dcb_r
            e = dcb_r * cbm
            dc_acc = dc_acc + _dot(dyb, srb) * eac
            db_acc = db_acc + _dot((xdt * to_end).astype(BF16), dsrb)
            yoff = _dot_nt(cb, srb) * eac
            f_rows = jnp.sum(xdt * dxdt_state, axis=-1, keepdims=True)
            dlast = jnp.sum(f_rows) + elast * jnp.sum(dsr * sr)
            dac = (jnp.sum(e, axis=-1, keepdims=True) + jnp.sum(dyr * yoff, axis=-1, keepdims=True) - f_rows
                   + jnp.where(is_last, dlast, 0.0))
            dac_rows = dac_rows + jnp.where(lane8 == r, dac, 0.0)
            dac_cols = dac_cols + jnp.where(sub8 == r, jnp.sum(e, axis=0, keepdims=True), 0.0)
            ddt_all = ddt_all + jnp.where(lane8 == r, jnp.sum(dxdt * xr, axis=-1, keepdims=True), 0.0)
            dd_all = dd_all + jnp.where(lane1 == r, jnp.sum(dyr * xr), 0.0)
            dx_ref[:, sl] = dxdt * dtc + dr * dyr
            dstate[r] = elast * dsr + _dot_tn((dyr * eac).astype(BF16), cb)
        dcbb = dcb.astype(BF16)
        dc_ref[...] = dc_acc + _dot(dcbb, bb)
        db_ref[...] = db_acc + _dot_tn(dcbb, cb)
        triu = jnp.where(_iota((ln, ln), 0) <= _iota((ln, ln), 1), 1.0, 0.0).astype(BF16)
        g_rows = _xdot(triu, dac_rows)
        g_cols = _dot_x(dac_cols, cm["tri"])
        d_rows = (ddt_all + g_rows * cm["a"]) * _sigmoid(cm["raw"])
        d_cols = -(g_cols * cm["at"]) * _sigmoid(cm["rawt"])
        ddt_ref[0] = d_rows
        ddtt_ref[0] = d_cols
        parts = (jnp.sum(d_rows, axis=0, keepdims=True), jnp.sum(d_cols, axis=1, keepdims=True),
                 jnp.sum(g_rows * dt, axis=0, keepdims=True) * cm["a"],
                 -jnp.sum(g_cols * cm["dtt"], axis=1, keepdims=True) * cm["at"], dd_all)
        outs = (dbias_ref, dbiast_ref, dalog_ref, dalogt_ref, dd_ref)

        @pl.when(step == 0)
        def _():
            for o_ref, p in zip(outs, parts):
                o_ref[0] = p

        @pl.when(step > 0)
        def _():
            for o_ref, p in zip(outs, parts):
                o_ref[0] += p

    xs, bs, cs, dts, dtts, row, col, st = _ssd_specs(nc, True)
    grp = pl.BlockSpec((ln, SSM_STATE), lambda g, c: (nc - 1 - c, g))
    rows = jax.ShapeDtypeStruct((SSM_GROUPS, 1, SSM_HPG), F32)
    cols = jax.ShapeDtypeStruct((SSM_GROUPS, SSM_HPG, 1), F32)
    return pl.pallas_call(
        body, name="ssd_bwd", grid=(SSM_GROUPS, nc),
        in_specs=[xs, bs, cs, dts, dtts, row, col, row, col, row, st, xs],
        out_specs=[xs, grp, grp, dts, dtts, row, col, row, col, row],
        out_shape=[jax.ShapeDtypeStruct((s, SSM_D_INNER), F32),
                   jax.ShapeDtypeStruct((s, SSM_GROUPS * SSM_STATE), F32),
                   jax.ShapeDtypeStruct((s, SSM_GROUPS * SSM_STATE), F32),
                   jax.ShapeDtypeStruct((SSM_GROUPS, s, SSM_HPG), F32),
                   jax.ShapeDtypeStruct((SSM_GROUPS, SSM_HPG, s), F32), rows, cols, rows, cols, rows],
        scratch_shapes=[pltpu.VMEM((SSM_HPG, SSM_P, SSM_STATE), F32)],
        compiler_params=_params("parallel", "arbitrary"),
    )(xbc, xbc, xbc, dt_g, dt_gt, bias_r, bias_c, alog_r, alog_c, d_r, states, dy)


def _gate_norm_fwd(y, z, g):
    s = y.shape[0]
    ts = _tile(s, 512, 8)

    def body(y_ref, z_ref, g_ref, o_ref):
        zv = z_ref[...]
        yg = y_ref[...] * (zv * _sigmoid(zv))
        r = lax.rsqrt(jnp.mean(yg * yg, axis=-1, keepdims=True) + EPS)
        o_ref[...] = (yg * r * g_ref[...]).astype(BF16)

    blk = pl.BlockSpec((ts, SSM_GN), lambda j, i: (i, j))
    vec = pl.BlockSpec((1, SSM_GN), lambda j, i: (0, j))
    return pl.pallas_call(
        body, name="gate_norm_fwd", grid=(SSM_GROUPS, s // ts), in_specs=[blk, blk, vec], out_specs=blk,
        out_shape=jax.ShapeDtypeStruct((s, SSM_D_INNER), BF16), compiler_params=_params("parallel", "parallel"),
    )(y, z, g)


def _gate_norm_bwd(y, z, g, dout):
    s = y.shape[0]
    ts = _tile(s, 512, 8)

    def body(y_ref, z_ref, g_ref, do_ref, dy_ref, dz_ref, dg_ref):
        yv, zv, dov = y_ref[...], z_ref[...], do_ref[...].astype(F32)
        sg = _sigmoid(zv)
        silu = zv * sg
        yg = yv * silu
        r = lax.rsqrt(jnp.mean(yg * yg, axis=-1, keepdims=True) + EPS)
        ygn = yg * r
        part = jnp.sum(dov * ygn, axis=0, keepdims=True)

        @pl.when(pl.program_id(1) == 0)
        def _():
            dg_ref[...] = part

        @pl.when(pl.program_id(1) > 0)
        def _():
            dg_ref[...] += part

        dn = dov * g_ref[...]
        dyg = r * (dn - ygn * jnp.mean(dn * ygn, axis=-1, keepdims=True))
        dy_ref[...] = dyg * silu
        dz_ref[...] = (dyg * yv * sg * (1.0 + zv * (1.0 - sg))).astype(BF16)

    blk = pl.BlockSpec((ts, SSM_GN), lambda j, i: (i, j))
    vec = pl.BlockSpec((1, SSM_GN), lambda j, i: (0, j))
    return pl.pallas_call(
        body, name="gate_norm_bwd", grid=(SSM_GROUPS, s // ts), in_specs=[blk, blk, vec, blk],
        out_specs=[blk, blk, vec],
        out_shape=[jax.ShapeDtypeStruct((s, SSM_D_INNER), F32), jax.ShapeDtypeStruct((s, SSM_D_INNER), BF16),
                   jax.ShapeDtypeStruct((1, SSM_D_INNER), F32)],
        compiler_params=_params("parallel", "arbitrary"),
    )(y, z, g, dout)


def _rope_tables(positions):
    inv_freq = ROPE_THETA ** (-jnp.arange(0, ATT_HEAD_DIM, 2, dtype=F32) / ATT_HEAD_DIM)
    ang = positions.astype(F32)[:, None] * inv_freq
    return jnp.tile(jnp.cos(ang), (1, 4)), jnp.tile(jnp.sin(ang), (1, 4))


def _group_views(v):
    return v.reshape(SSM_GROUPS, 1, SSM_HPG), v.reshape(SSM_GROUPS, SSM_HPG, 1)


def _ffn_fwd(x, norm_g, wg, wu, wd, tag):
    h = _rms_fwd(x, norm_g, f"ffn_norm_{tag}")
    g = _mm(h, wg, "nn", f"ffn_gate_{tag}")
    u = _mm(h, wu, "nn", f"ffn_up_{tag}")
    a = _act_fwd(g, u, f"ffn_act_{tag}")
    return _mm(a, wd, "nn", f"ffn_down_{tag}", add=x), (h, g, u, a)


def _ffn_bwd(x, norm_g, wg, wu, wd, saved, dout, tag):
    h, g, u, a = saved
    da = _mm(dout, wd, "nt", f"ffn_down_dx_{tag}", out_dtype=BF16)
    dwd = _mm(a, dout, "tn", f"ffn_down_dw_{tag}", out_dtype=BF16)
    dg, du = _act_bwd(g, u, da, f"ffn_act_bwd_{tag}")
    dwg = _mm(h, dg, "tn", f"ffn_gate_dw_{tag}", out_dtype=BF16)
    dwu = _mm(h, du, "tn", f"ffn_up_dw_{tag}", out_dtype=BF16)
    dh = _mm(dg, wg, "nt", f"ffn_gate_dx_{tag}")
    dh = _mm(du, wu, "nt", f"ffn_up_dx_{tag}", add=dh)
    dx, dgain = _rms_bwd(x, norm_g, dh, dout, f"ffn_norm_bwd_{tag}")
    return dx, dgain, dwg, dwu, dwd


def _local_step(x, positions, target, w):
    cos, sin = _rope_tables(positions)
    row = lambda v: v.reshape(1, -1)
    gq, gk = jnp.tile(row(w["attn_q_norm"]), (1, 2)), jnp.tile(row(w["attn_k_norm"]), (1, 2))
    sinks = w["attn_sinks"].reshape(-1)

    h0 = _rms_fwd(x, row(w["mixer_norm"][0]), "mixer_norm_0")
    qkv = _mm(h0, w["attn_w_qkv"], "nn", "attn_qkv")
    q, kk, vlo, vhi = _attn_prep(qkv, cos, sin, gq, gk)
    o = _attn_fwd(q, kk, vlo, vhi, sinks)
    x1 = _mm(o, w["attn_w_o"], "nn", "attn_out", add=x)
    x2, ffn0 = _ffn_fwd(x1, row(w["ffn_norm"][0]), w["ffn_w_gate"][0], w["ffn_w_up"][0], w["ffn_w_down"][0], 0)

    h2 = _rms_fwd(x2, row(w["mixer_norm"][1]), "mixer_norm_1")
    z = _mm(h2, w["ssm_w_z"], "nn", "ssm_in_z")
    xbc_raw = _mm(h2, w["ssm_w_xbc"], "nn", "ssm_in_xbc")
    dt_raw = _mm(h2, w["ssm_w_dt"], "nn", "ssm_in_dt")
    s = x.shape[0]
    dt_g = dt_raw[:, :SSM_HEADS].reshape(s, SSM_GROUPS, SSM_HPG).transpose(1, 0, 2)
    dt_gt = dt_g.transpose(0, 2, 1)
    bias_r, bias_c = _group_views(w["ssm_dt_bias"].reshape(-1))
    alog_r, alog_c = _group_views(w["ssm_a_log"].reshape(-1))
    d_r, _ = _group_views(w["ssm_d"].reshape(-1))
    xbc = _conv_fwd(xbc_raw, w["ssm_conv_w"], row(w["ssm_conv_b"]))
    ssd_args = (xbc, dt_g, dt_gt, bias_r, bias_c, alog_r, alog_c, d_r)
    y, states = _ssd_fwd(*ssd_args)
    yn = _gate_norm_fwd(y, z, row(w["ssm_norm"]))
    x3 = _mm(yn, w["ssm_w_out"], "nn", "ssm_out", add=x2)
    x4, ffn1 = _ffn_fwd(x3, row(w["ffn_norm"][1]), w["ffn_w_gate"][1], w["ffn_w_up"][1], w["ffn_w_down"][1], 1)

    loss_row, dx4 = _loss_fwd_bwd(x4, target)

    dx3, dfn1, dwg1, dwu1, dwd1 = _ffn_bwd(x3, row(w["ffn_norm"][1]), w["ffn_w_gate"][1], w["ffn_w_up"][1],
                                           w["ffn_w_down"][1], ffn1, dx4, 1)
    dyn = _mm(dx3, w["ssm_w_out"], "nt", "ssm_out_dx")
    dw_out = _mm(yn, dx3, "tn", "ssm_out_dw", out_dtype=BF16)
    dy, dz, dssm_norm = _gate_norm_bwd(y, z, row(w["ssm_norm"]), dyn)
    dxs, db, dc, ddt_g, ddt_gt, dbias, dbias_t, dalog, dalog_t, dd = _ssd_bwd(*ssd_args, states, dy)
    ddt_g = ddt_g + ddt_gt.transpose(0, 2, 1)
    dbias = dbias.reshape(-1) + dbias_t.reshape(-1)
    dalog = dalog.reshape(-1) + dalog_t.reshape(-1)
    dpre, dconv_w, dconv_b = _conv_bwd_pre(xbc_raw, w["ssm_conv_w"], row(w["ssm_conv_b"]),
                                           jnp.concatenate([dxs, db, dc], axis=1))
    dxbc_raw = _conv_bwd_in(dpre, w["ssm_conv_w"])
    ddt_raw = jnp.pad(ddt_g.transpose(1, 0, 2).reshape(s, SSM_HEADS), ((0, 0), (0, LANES - SSM_HEADS)))
    dw_z = _mm(h2, dz, "tn", "ssm_in_z_dw", out_dtype=BF16)
    dw_xbc = _mm(h2, dxbc_raw, "tn", "ssm_in_xbc_dw", out_dtype=BF16)
    dw_dt = _mm(h2, ddt_raw, "tn", "ssm_in_dt_dw", out_dtype=BF16)
    dh2 = _mm(dz, w["ssm_w_z"], "nt", "ssm_in_z_dx")
    dh2 = _mm(dxbc_raw, w["ssm_w_xbc"], "nt", "ssm_in_xbc_dx", add=dh2)
    dh2 = _mm(ddt_raw, w["ssm_w_dt"], "nt", "ssm_in_dt_dx", add=dh2)
    dx2, dmn1 = _rms_bwd(x2, row(w["mixer_norm"][1]), dh2, dx3, "mixer_norm_bwd_1")

    dx1, dfn0, dwg0, dwu0, dwd0 = _ffn_bwd(x1, row(w["ffn_norm"][0]), w["ffn_w_gate"][0], w["ffn_w_up"][0],
                                           w["ffn_w_down"][0], ffn0, dx2, 0)
    do = _mm(dx1, w["attn_w_o"], "nt", "attn_out_dx", out_dtype=BF16)
    dw_o = _mm(o, dx1, "tn", "attn_out_dw", out_dtype=BF16)
    dq, dkc, dkp, dvloc, dvlop, dvhic, dvhip, dsink = _attn_bwd(q, kk, vlo, vhi, sinks, do)
    dqkv, dgq, dgk = _attn_prep_bwd(qkv, cos, sin, gq, gk, dq, (dkc, dkp), (dvloc, dvlop), (dvhic, dvhip))
    dw_qkv = _mm(h0, dqkv, "tn", "attn_qkv_dw", out_dtype=BF16)
    dh0 = _mm(dqkv, w["attn_w_qkv"], "nt", "attn_qkv_dx")
    dx0, dmn0 = _rms_bwd(x, row(w["mixer_norm"][0]), dh0, dx1, "mixer_norm_bwd_0")

    fold = lambda v: v[0, :ATT_HEAD_DIM] + v[0, ATT_HEAD_DIM:]
    grads = {
        "mixer_norm": jnp.concatenate([dmn0, dmn1], axis=0),
        "ffn_norm": jnp.concatenate([dfn0, dfn1], axis=0),
        "attn_w_qkv": dw_qkv,
        "attn_q_norm": fold(dgq), "attn_k_norm": fold(dgk),
        "attn_sinks": dsink[:, :, 0].reshape(-1),
        "attn_w_o": dw_o,
        "ssm_w_in": jnp.concatenate([dw_z, dw_xbc, dw_dt[:, :SSM_HEADS]], axis=1),
        "ssm_conv_w": dconv_w, "ssm_conv_b": dconv_b.reshape(-1),
        "ssm_dt_bias": dbias, "ssm_a_log": dalog, "ssm_d": dd.reshape(-1),
        "ssm_norm": dssm_norm.reshape(-1),
        "ssm_w_out": dw_out,
        "ffn_w_gate": jnp.stack([dwg0, dwg1]), "ffn_w_up": jnp.stack([dwu0, dwu1]),
        "ffn_w_down": jnp.stack([dwd0, dwd1]),
    }
    return loss_row[0, 0], dx0, grads


ANY = pl.BlockSpec(memory_space=pl.ANY)
OTHER_CHIPS = ((1, 0), (0, 1), (1, 1))


def _position():
    return lax.axis_index("x"), lax.axis_index("y"), lax.axis_index("c")


def _gather_shards(packed):
    r, cols = packed.shape
    half = r // 2

    def body(p_ref, out_ref, send_sems, recv_sems, local_sem):
        x, y, c = _position()
        me, sibling = (x, y, c), (x, y, 1 - c)
        chips = [(x ^ fx, y ^ fy) for fx, fy in OTHER_CHIPS]

        def rows(px, py, pc):
            return out_ref.at[pl.ds((2 * px + py) * r + pc * half, half), :]

        def copy(k, block, to, src=None):
            return pltpu.make_async_remote_copy(
                src_ref=rows(*block) if src is None else src, dst_ref=rows(*block),
                send_sem=send_sems.at[k], recv_sem=recv_sems.at[k], device_id=to, device_id_type=MESH)

        mine = pltpu.make_async_copy(p_ref, out_ref.at[pl.ds((2 * x + y) * r, r), :], local_sem)
        mine.start()
        my_half = p_ref.at[pl.ds(c * half, half), :]
        first = [copy(j, me, (*chip, c), src=my_half) for j, chip in enumerate(chips)]
        for cp in first:
            cp.start()
        passed = [copy(3 + j, (*chip, c), sibling) for j, chip in enumerate(chips)]
        for j, chip in enumerate(chips):
            copy(j, (*chip, c), me).wait_recv()
            passed[j].start()
        for j, chip in enumerate(chips):
            copy(3 + j, (*chip, 1 - c), me).wait_recv()
        for cp in first + passed:
            cp.wait_send()
        mine.wait()

    return pl.pallas_call(
        body, name="gather_weights", in_specs=[ANY], out_specs=ANY,
        out_shape=jax.ShapeDtypeStruct((N_SHARDS * r, cols), packed.dtype),
        scratch_shapes=[pltpu.SemaphoreType.DMA((6,)), pltpu.SemaphoreType.DMA((6,)), pltpu.SemaphoreType.DMA],
    )(packed)


def _all_gather8(block, name):
    m_per, n = block.shape

    def body(x_ref, out_ref, send_sems, recv_sems, local_sem):
        x, y, c = _position()
        me, sibling = (x, y, c), (x, y, 1 - c)
        chips = [(x ^ fx, y ^ fy) for fx, fy in OTHER_CHIPS]

        def rows(px, py, pc):
            return out_ref.at[pl.ds((4 * px + 2 * py + pc) * m_per, m_per), :]

        def copy(k, blk, to, src=None):
            return pltpu.make_async_remote_copy(
                src_ref=rows(*blk) if src is None else src, dst_ref=rows(*blk),
                send_sem=send_sems.at[k], recv_sem=recv_sems.at[k], device_id=to, device_id_type=MESH)

        mine = pltpu.make_async_copy(x_ref, rows(*me), local_sem)
        mine.start()
        first = [copy(0, me, sibling, src=x_ref)]
        first += [copy(1 + j, me, (*chip, c), src=x_ref) for j, chip in enumerate(chips)]
        for cp in first:
            cp.start()
        passed = [copy(4 + j, (*chip, c), sibling) for j, chip in enumerate(chips)]
        for j, chip in enumerate(chips):
            copy(1 + j, (*chip, c), me).wait_recv()
            passed[j].start()
        copy(0, sibling, me).wait_recv()
        for j, chip in enumerate(chips):
            copy(4 + j, (*chip, 1 - c), me).wait_recv()
        for cp in first + passed:
            cp.wait_send()
        mine.wait()

    return pl.pallas_call(
        body, name=name, out_shape=jax.ShapeDtypeStruct((N_DEV * m_per, n), block.dtype),
        in_specs=[pl.BlockSpec(memory_space=pltpu.VMEM)], out_specs=pl.BlockSpec(memory_space=pltpu.VMEM),
        scratch_shapes=[pltpu.SemaphoreType.DMA((7,)), pltpu.SemaphoreType.DMA((7,)), pltpu.SemaphoreType.DMA],
    )(block)


def _swap_other_half(halves):
    _, r, cols = halves.shape

    def body(h_ref, out_ref, send_sem, recv_sem):
        x, y, c = _position()
        cp = pltpu.make_async_remote_copy(src_ref=h_ref.at[1 - c], dst_ref=out_ref, send_sem=send_sem,
                                          recv_sem=recv_sem, device_id=(x, y, 1 - c), device_id_type=MESH)
        cp.start()
        cp.wait()

    return pl.pallas_call(
        body, name="grads_to_sibling", in_specs=[ANY], out_specs=ANY,
        out_shape=jax.ShapeDtypeStruct((r, cols), halves.dtype),
        scratch_shapes=[pltpu.SemaphoreType.DMA, pltpu.SemaphoreType.DMA],
    )(halves)


def _send_to_owners(partial):
    _, r, cols = partial.shape

    def body(p_ref, out_ref, send_sems, recv_sems):
        x, y, c = _position()
        copies = []
        for k, (fx, fy) in enumerate(OTHER_CHIPS):
            px, py = x ^ fx, y ^ fy
            copies.append(pltpu.make_async_remote_copy(
                src_ref=p_ref.at[2 * px + py], dst_ref=out_ref.at[k], send_sem=send_sems.at[k],
                recv_sem=recv_sems.at[k], device_id=(px, py, c), device_id_type=MESH))
        for cp in copies:
            cp.start()
        for cp in copies:
            cp.wait()

    return pl.pallas_call(
        body, name="grads_to_owners", in_specs=[ANY], out_specs=ANY,
        out_shape=jax.ShapeDtypeStruct((len(OTHER_CHIPS), r, cols), partial.dtype),
        scratch_shapes=[pltpu.SemaphoreType.DMA((3,)), pltpu.SemaphoreType.DMA((3,))],
    )(partial)


def _share_halves(total_half):
    r, cols = total_half.shape

    def body(t_ref, out_ref, send_sem, recv_sem, local_sem):
        x, y, c = _position()
        local = pltpu.make_async_copy(t_ref, out_ref.at[c], local_sem)
        local.start()
        cp = pltpu.make_async_remote_copy(src_ref=t_ref, dst_ref=out_ref.at[c], send_sem=send_sem,
                                          recv_sem=recv_sem, device_id=(x, y, 1 - c), device_id_type=MESH)
        cp.start()
        cp.wait()
        local.wait()

    return pl.pallas_call(
        body, name="grads_share_halves", in_specs=[ANY], out_specs=ANY,
        out_shape=jax.ShapeDtypeStruct((2, r, cols), total_half.dtype),
        scratch_shapes=[pltpu.SemaphoreType.DMA, pltpu.SemaphoreType.DMA, pltpu.SemaphoreType.DMA],
    )(total_half)


def _add_pair(halves, recv, c_idx):
    _, r, cols = halves.shape
    tr = _tile(r, 1024, 16)

    def body(c_ref, a_ref, b_ref, o_ref):
        o_ref[...] = (a_ref[...].astype(F32) + b_ref[...].astype(F32)).astype(o_ref.dtype)

    return pl.pallas_call(
        body, name="grads_add_pair",
        grid_spec=pltpu.PrefetchScalarGridSpec(
            num_scalar_prefetch=1, grid=(r // tr,),
            in_specs=[pl.BlockSpec((None, tr, cols), lambda i, c_ref: (c_ref[0], i, 0)),
                      pl.BlockSpec((tr, cols), lambda i, c_ref: (i, 0))],
            out_specs=pl.BlockSpec((tr, cols), lambda i, c_ref: (i, 0))),
        out_shape=jax.ShapeDtypeStruct((r, cols), halves.dtype),
        compiler_params=_params("parallel"),
    )(c_idx, halves, recv)


def _add_owned(partial, recv, s_idx):
    _, r, cols = partial.shape
    tr = _tile(r, 1024, 16)

    def body(s_ref, a_ref, r0_ref, r1_ref, r2_ref, o_ref):
        o_ref[...] = (((a_ref[...].astype(F32) + r0_ref[...].astype(F32)) + r1_ref[...].astype(F32))
                      + r2_ref[...].astype(F32))

    slot = lambda k: pl.BlockSpec((None, tr, cols), lambda i, s_ref: (k, i, 0))
    return pl.pallas_call(
        body, name="grads_add_owned",
        grid_spec=pltpu.PrefetchScalarGridSpec(
            num_scalar_prefetch=1, grid=(r // tr,),
            in_specs=[pl.BlockSpec((None, tr, cols), lambda i, s_ref: (s_ref[0], i, 0)), slot(0), slot(1), slot(2)],
            out_specs=pl.BlockSpec((tr, cols), lambda i, s_ref: (i, 0))),
        out_shape=jax.ShapeDtypeStruct((r, cols), F32),
        compiler_params=_params("parallel"),
    )(s_idx, partial, recv, recv, recv)


def _sum8(gathered):
    m = gathered.shape[0] // N_DEV

    def body(g_ref, o_ref):
        total = g_ref[0:m, :]
        for d in range(1, N_DEV):
            total = total + g_ref[d * m:(d + 1) * m, :]
        o_ref[...] = total

    return pl.pallas_call(
        body, name="small_grads_sum", out_shape=jax.ShapeDtypeStruct((m, LANES), F32),
        in_specs=[pl.BlockSpec(memory_space=pltpu.VMEM)], out_specs=pl.BlockSpec(memory_space=pltpu.VMEM),
    )(gathered)


ADAMW_BLOCK_ELEMS = 1 << 18


def _adamw(w, g, m, v, name):
    r, cols = w.shape
    tr = _tile(r, max(8, ADAMW_BLOCK_ELEMS // cols // 8 * 8), 8)

    def body(w_ref, g_ref, m_ref, v_ref, d_ref, nm_ref, nv_ref):
        gv = g_ref[...]
        nm = ADAM_B1 * m_ref[...] + (1.0 - ADAM_B1) * gv
        nv = ADAM_B2 * v_ref[...] + (1.0 - ADAM_B2) * jnp.square(gv)
        m_hat = nm / (1.0 - ADAM_B1 ** ADAM_STEP)
        v_hat = nv / (1.0 - ADAM_B2 ** ADAM_STEP)
        d_ref[...] = -ADAM_LR * (m_hat / (jnp.sqrt(v_hat) + ADAM_EPS) + ADAM_WD * w_ref[...])
        nm_ref[...] = nm
        nv_ref[...] = nv

    blk = pl.BlockSpec((tr, cols), lambda i: (i, 0))
    return pl.pallas_call(
        body, name=name, grid=(r // tr,), in_specs=[blk] * 4, out_specs=[blk] * 3,
        out_shape=[jax.ShapeDtypeStruct((r, cols), F32)] * 3, compiler_params=_params("parallel"),
    )(w, g, m, v)


WEIGHTS = ("mixer_norm", "ffn_norm", "attn_w_qkv", "attn_q_norm", "attn_k_norm", "attn_sinks", "attn_w_o",
           "ssm_w_in", "ssm_conv_w", "ssm_conv_b", "ssm_dt_bias", "ssm_a_log", "ssm_d", "ssm_norm", "ssm_w_out",
           "ffn_w_gate", "ffn_w_up", "ffn_w_down")
BIG = (("attn_w_qkv", 2), ("attn_w_o", 1), ("ssm_w_in", 2), ("ssm_w_out", 1),
       ("ffn_w_gate", 2), ("ffn_w_up", 2), ("ffn_w_down", 1))
PACK_COLS = 1024
SMALL_SHARDED = ("ssm_conv_w", "ssm_conv_b", "ssm_norm")
SMALL = tuple(n for n in WEIGHTS if n not in dict(BIG))


def _unshard(seg, axis):
    _, l, k, n = seg.shape
    if axis == 2:
        return seg.transpose(1, 2, 0, 3).reshape(l, k, N_SHARDS * n)
    return seg.transpose(1, 0, 2, 3).reshape(l, N_SHARDS * k, n)


def _to_shards(full, axis):
    l, k, n = full.shape
    if axis == 2:
        return full.reshape(l, k, N_SHARDS, n // N_SHARDS).transpose(2, 0, 1, 3).reshape(N_SHARDS, -1)
    return full.reshape(l, N_SHARDS, k // N_SHARDS, n).transpose(1, 0, 2, 3).reshape(N_SHARDS, -1)


def _pack_rows(parts, width, row_unit):
    flat = jnp.concatenate([p.reshape(-1) for p in parts])
    pad = (-flat.shape[0]) % (width * row_unit)
    return jnp.pad(flat, (0, pad)).reshape(-1, width)


def _unpack(flat, shapes):
    out, off = [], 0
    for shp in shapes:
        size = math.prod(shp)
        out.append(flat[off:off + size].reshape(shp))
        off += size
    return out


def kernel(x, positions, mixer_norm, ffn_norm, attn_w_qkv, attn_q_norm, attn_k_norm, attn_sinks, attn_w_o, ssm_w_in, ssm_conv_w, ssm_conv_b, ssm_dt_bias, ssm_a_log, ssm_d, ssm_norm, ssm_w_out, ffn_w_gate, ffn_w_up, ffn_w_down, loss_target, m_mixer_norm, m_ffn_norm, m_attn_w_qkv, m_attn_q_norm, m_attn_k_norm, m_attn_sinks, m_attn_w_o, m_ssm_w_in, m_ssm_conv_w, m_ssm_conv_b, m_ssm_dt_bias, m_ssm_a_log, m_ssm_d, m_ssm_norm, m_ssm_w_out, m_ffn_w_gate, m_ffn_w_up, m_ffn_w_down, v_mixer_norm, v_ffn_norm, v_attn_w_qkv, v_attn_q_norm, v_attn_k_norm, v_attn_sinks, v_attn_w_o, v_ssm_w_in, v_ssm_conv_w, v_ssm_conv_b, v_ssm_dt_bias, v_ssm_a_log, v_ssm_d, v_ssm_norm, v_ssm_w_out, v_ffn_w_gate, v_ffn_w_up, v_ffn_w_down):
    args = locals()
    w = {n: args[n] for n in WEIGHTS}
    m = {n: args["m_" + n] for n in WEIGHTS}
    v = {n: args["v_" + n] for n in WEIGHTS}
    ax, ay, ac = lax.axis_index("x"), lax.axis_index("y"), lax.axis_index("c")
    shard = 2 * ax + ay

    big_shapes = [w[n].shape for n, _ in BIG]
    packed = _pack_rows([w[n].astype(BF16) for n, _ in BIG], PACK_COLS, 32)
    gathered = _gather_shards(packed).reshape(N_SHARDS, -1)
    full, off = {}, 0
    for (n, axis), shp in zip(BIG, big_shapes):
        size = math.prod(shp)
        full[n] = _unshard(gathered[:, off:off + size].reshape((N_SHARDS,) + shp), axis)
        off += size
    small_shapes = [w[n].shape for n in SMALL_SHARDED]
    small_block = _pack_rows([w[n] for n in SMALL_SHARDED], LANES, 8)
    small_all = _all_gather8(small_block, "gather_small_params").reshape(N_DEV, -1)[::2]
    off = 0
    for n, shp in zip(SMALL_SHARDED, small_shapes):
        size = math.prod(shp)
        seg = small_all[:, off:off + size].reshape((N_SHARDS,) + shp)
        full[n] = jnp.moveaxis(seg, 0, -2).reshape(shp[:-1] + (N_SHARDS * shp[-1],))
        off += size

    w_in = full["ssm_w_in"][0]
    wl = {
        "mixer_norm": mixer_norm, "ffn_norm": ffn_norm, "attn_w_qkv": full["attn_w_qkv"][0],
        "attn_q_norm": attn_q_norm[0], "attn_k_norm": attn_k_norm[0], "attn_sinks": attn_sinks[0],
        "attn_w_o": full["attn_w_o"][0],
        "ssm_w_z": w_in[:, :SSM_D_INNER], "ssm_w_xbc": w_in[:, SSM_D_INNER:SSM_D_INNER + SSM_CONV_DIM],
        "ssm_w_dt": jnp.pad(w_in[:, SSM_D_INNER + SSM_CONV_DIM:], ((0, 0), (0, LANES - SSM_HEADS))),
        "ssm_conv_w": full["ssm_conv_w"][0], "ssm_conv_b": full["ssm_conv_b"][0],
        "ssm_dt_bias": ssm_dt_bias[0], "ssm_a_log": ssm_a_log[0], "ssm_d": ssm_d[0],
        "ssm_norm": full["ssm_norm"][0], "ssm_w_out": full["ssm_w_out"][0],
        "ffn_w_gate": full["ffn_w_gate"], "ffn_w_up": full["ffn_w_up"], "ffn_w_down": full["ffn_w_down"],
    }

    loss_part, dx, g_full = _local_step(x[0], positions[0], loss_target[0], wl)

    flat = jnp.concatenate([_to_shards(g_full[n].reshape((-1,) + g_full[n].shape[-2:]), axis) for n, axis in BIG],
                           axis=1)
    flat = jnp.pad(flat, ((0, 0), (0, packed.size - flat.shape[1])))
    half_rows = flat.shape[1] // PACK_COLS // 2
    halves = flat.reshape(N_SHARDS, 2, half_rows, PACK_COLS).transpose(1, 0, 2, 3)
    halves = halves.reshape(2, N_SHARDS * half_rows, PACK_COLS)
    c_idx, s_idx = ac.reshape(1).astype(jnp.int32), shard.reshape(1).astype(jnp.int32)
    partial = _add_pair(halves, _swap_other_half(halves), c_idx).reshape(N_SHARDS, half_rows, PACK_COLS)
    total_half = _add_owned(partial, _send_to_owners(partial), s_idx)
    g_shard = _unpack(_share_halves(total_half).reshape(-1), big_shapes)
    grads = dict(zip([n for n, _ in BIG], g_shard))

    small_full_shapes = [g_full[n].shape for n in SMALL] + [(1,)]
    small_g = _pack_rows([g_full[n] for n in SMALL] + [loss_part.reshape(1)], LANES, 8)
    small_sum = _sum8(_all_gather8(small_g, "gather_small_grads")).reshape(-1)
    *small_list, loss = _unpack(small_sum, small_full_shapes)
    for n, g in zip(SMALL, small_list):
        if n in SMALL_SHARDED:
            width = w[n].shape[-1]
            g = lax.dynamic_slice_in_dim(g, shard * width, width, axis=g.ndim - 1)
        grads[n] = g.reshape(w[n].shape)

    delta, new_m, new_v = {}, {}, {}
    for n, _ in BIG:
        two_d = lambda a: a.reshape(-1, a.shape[-1])
        d, nm, nv = _adamw(two_d(w[n]), two_d(grads[n]), two_d(m[n]), two_d(v[n]), "adamw_" + n)
        delta[n], new_m[n], new_v[n] = d.reshape(w[n].shape), nm.reshape(w[n].shape), nv.reshape(w[n].shape)
    small_local = [w[n].shape for n in SMALL]
    pk = lambda t: _pack_rows([t[n] for n in SMALL], LANES, 8)
    outs = _adamw(pk(w), pk(grads), pk(m), pk(v), "adamw_small")
    for res, o in zip((delta, new_m, new_v), outs):
        for n, a in zip(SMALL, _unpack(o.reshape(-1), small_local)):
            res[n] = a

    return (loss.reshape(()), dx[None], *[grads[n] for n in WEIGHTS], *[delta[n] for n in WEIGHTS],
            *[new_m[n] for n in WEIGHTS], *[new_v[n] for n in WEIGHTS])
```

```python
import math

import jax
import jax.numpy as jnp
from jax import lax
from jax.experimental import pallas as pl
from jax.experimental.pallas import tpu as pltpu

F32 = jnp.float32
BF16 = jnp.bfloat16

D_MODEL = 2048
EPS = 1e-6
ATT_HEAD_DIM = 64
ATT_Q_HEADS = 32
ATT_KV_HEADS = 4
ATT_GROUP = 8
ATT_BLOCK = 128
ROPE_THETA = 10000.0
Q_WIDTH = ATT_Q_HEADS * ATT_HEAD_DIM
KV_WIDTH = ATT_KV_HEADS * ATT_HEAD_DIM
SSM_D_INNER = 4096
SSM_HEADS = 64
SSM_GROUPS = 8
SSM_HPG = 8
SSM_P = 64
SSM_STATE = 128
SSM_CONV = 4
SSM_CHUNK = 256
SSM_CONV_DIM = 6144
SSM_GN = SSM_D_INNER // SSM_GROUPS
SSM_IN = SSM_D_INNER + SSM_CONV_DIM + SSM_HEADS
LANES = 128
SSM_IN_PAD = -(-SSM_IN // LANES) * LANES
N_SHARDS = 4
N_DEV = 8

ADAM_LR = 0.001
ADAM_B1 = 0.9
ADAM_B2 = 0.999
ADAM_EPS = 1e-08
ADAM_WD = 0.01
ADAM_STEP = 10

VMEM_LIMIT = 56 * 1024 * 1024
MESH = pl.DeviceIdType.MESH
ANY = pl.BlockSpec(memory_space=pl.ANY)


def _params(*sem):
    return pltpu.CompilerParams(dimension_semantics=sem, vmem_limit_bytes=VMEM_LIMIT)


def _tile(dim, target, unit=LANES):
    if dim <= target:
        return dim
    t = (target // unit) * unit
    while t >= unit:
        if dim % t == 0:
            return t
        t -= unit
    return dim


def _dot(a, b):
    return lax.dot_general(a, b, (((1,), (0,)), ((), ())), preferred_element_type=F32)


def _dot_nt(a, b):
    return lax.dot_general(a, b, (((1,), (1,)), ((), ())), preferred_element_type=F32)


def _dot_tn(a, b):
    return lax.dot_general(a, b, (((0,), (0,)), ((), ())), preferred_element_type=F32)


def _split3(x):
    hi = x.astype(BF16)
    r1 = x - hi.astype(F32)
    mid = r1.astype(BF16)
    lo = (r1 - mid.astype(F32)).astype(BF16)
    return hi, mid, lo


def _dot_x(x, m):
    hi, mid, lo = _split3(x)
    return _dot(hi, m) + _dot(mid, m) + _dot(lo, m)


def _xdot(m, x):
    hi, mid, lo = _split3(x)
    return _dot(m, hi) + _dot(m, mid) + _dot(m, lo)


def _dot_x_nt(x, m):
    hi, mid, lo = _split3(x)
    return _dot_nt(hi, m) + _dot_nt(mid, m) + _dot_nt(lo, m)


def _iota(shape, dim):
    return lax.broadcasted_iota(jnp.int32, shape, dim)


def _sigmoid(x):
    return 1.0 / (1.0 + jnp.exp(-x))


def _softplus(x):
    return jnp.maximum(x, 0.0) + jnp.log(1.0 + jnp.exp(-jnp.abs(x)))


MM_ROWS = 1024
MM_TILE = 1408


def _mm(a, b, mode, name, add=None, out_dtype=F32, b_cols=False, out_cols=False):
    bs = b.shape[-2:]
    if b_cols:
        bs = (bs[0], N_SHARDS * bs[1])
    if mode == "nn":
        (m, k), (k2, n) = a.shape, bs
    elif mode == "nt":
        (m, k), (n, k2) = a.shape, bs
    else:
        (k, m), (k2, n) = a.shape, bs
    assert k == k2, (a.shape, b.shape, mode)
    split_n = (b_cols and mode == "nn") or out_cols
    split_k = b_cols and mode == "nt"
    tm = _tile(m, MM_TILE if mode == "tn" else MM_ROWS)
    tn = _tile(n // N_SHARDS if split_n else n, MM_TILE)
    tk = _tile(k // N_SHARDS if split_k else k, MM_ROWS if mode == "tn" else MM_TILE)
    nk = k // tk
    nj, nq = (n // N_SHARDS) // tn, (k // N_SHARDS) // tk
    if mode == "tn":
        a_spec = pl.BlockSpec((tk, tm), lambda i, j, q: (q, i))
    else:
        a_spec = pl.BlockSpec((tm, tk), lambda i, j, q: (i, q))
    if mode == "nt":
        if b_cols:
            b_spec = pl.BlockSpec((None, tn, tk), lambda i, j, q: (q // nq, j, q % nq))
        else:
            b_spec = pl.BlockSpec((tn, tk), lambda i, j, q: (j, q))
    elif b_cols:
        b_spec = pl.BlockSpec((None, tk, tn), lambda i, j, q: (j // nj, q, j % nj))
    else:
        b_spec = pl.BlockSpec((tk, tn), lambda i, j, q: (q, j))
    add_spec = pl.BlockSpec((tm, tn), lambda i, j, q: (i, j))
    if out_cols:
        o_spec = pl.BlockSpec((None, tm, tn), lambda i, j, q: (j // nj, i, j % nj))
        o_shape = (N_SHARDS, m, n // N_SHARDS)
    else:
        o_spec, o_shape = add_spec, (m, n)
    dot = {"nn": _dot, "nt": _dot_nt, "tn": _dot_tn}[mode]
    has_add = add is not None

    def body(*refs):
        if has_add:
            a_ref, b_ref, add_ref, o_ref, acc_ref = refs
        else:
            a_ref, b_ref, o_ref, acc_ref = refs
            add_ref = None
        part = dot(a_ref[...].astype(BF16), b_ref[...].astype(BF16))

        def finish(total):
            if has_add:
                total = total + add_ref[...].astype(F32)
            o_ref[...] = total.astype(out_dtype)

        if nk == 1:
            finish(part)
        else:
            q = pl.program_id(2)

            @pl.when(q == 0)
            def _():
                acc_ref[...] = part

            @pl.when(jnp.logical_and(q > 0, q < nk - 1))
            def _():
                acc_ref[...] += part

            @pl.when(q == nk - 1)
            def _():
                finish(acc_ref[...] + part)

    in_specs = [a_spec, b_spec] + ([add_spec] if has_add else [])
    args = (a, b) + ((add,) if has_add else ())
    return pl.pallas_call(
        body, name=name, grid=(m // tm, n // tn, nk),
        in_specs=in_specs, out_specs=o_spec,
        out_shape=jax.ShapeDtypeStruct(o_shape, out_dtype),
        scratch_shapes=[pltpu.VMEM((tm, tn) if nk > 1 else (8, LANES), F32)],
        compiler_params=_params("parallel", "parallel", "arbitrary"),
    )(*args)


def _rms_fwd(x, g, name):
    s, d = x.shape
    ts = _tile(s, 512, 8)

    def body(x_ref, g_ref, o_ref):
        xv = x_ref[...]
        r = lax.rsqrt(jnp.mean(xv * xv, axis=-1, keepdims=True) + EPS)
        o_ref[...] = (xv * r * g_ref[...]).astype(BF16)

    return pl.pallas_call(
        body, name=name, grid=(s // ts,),
        in_specs=[pl.BlockSpec((ts, d), lambda i: (i, 0)), pl.BlockSpec((1, d), lambda i: (0, 0))],
        out_specs=pl.BlockSpec((ts, d), lambda i: (i, 0)),
        out_shape=jax.ShapeDtypeStruct((s, d), BF16),
        compiler_params=_params("parallel"),
    )(x, g)


def _rms_bwd(x, g, dh, dres, name):
    s, d = x.shape
    ts = _tile(s, 512, 8)

    def body(x_ref, g_ref, dh_ref, dres_ref, dx_ref, dg_ref):
        xv = x_ref[...]
        r = lax.rsqrt(jnp.mean(xv * xv, axis=-1, keepdims=True) + EPS)
        xhat = xv * r
        dhv = dh_ref[...].astype(F32)
        part = jnp.sum(dhv * xhat, axis=0, keepdims=True)

        @pl.when(pl.program_id(0) == 0)
        def _():
            dg_ref[...] = part

        @pl.when(pl.program_id(0) > 0)
        def _():
            dg_ref[...] += part

        dxh = dhv * g_ref[...]
        dx = r * (dxh - xhat * jnp.mean(dxh * xhat, axis=-1, keepdims=True))
        dx_ref[...] = dres_ref[...] + dx

    row = pl.BlockSpec((ts, d), lambda i: (i, 0))
    vec = pl.BlockSpec((1, d), lambda i: (0, 0))
    return pl.pallas_call(
        body, name=name, grid=(s // ts,),
        in_specs=[row, vec, row, row], out_specs=[row, vec],
        out_shape=[jax.ShapeDtypeStruct((s, d), F32), jax.ShapeDtypeStruct((1, d), F32)],
        compiler_params=_params("arbitrary"),
    )(x, g, dh, dres)


def _act_fwd(g, u, name):
    s, f = g.shape
    ts, tf = _tile(s, 512, 8), _tile(f, 1408)

    def body(g_ref, u_ref, o_ref):
        gv = g_ref[...]
        o_ref[...] = (gv * _sigmoid(gv) * u_ref[...]).astype(BF16)

    blk = pl.BlockSpec((ts, tf), lambda i, j: (i, j))
    return pl.pallas_call(
        body, name=name, grid=(s // ts, f // tf), in_specs=[blk, blk], out_specs=blk,
        out_shape=jax.ShapeDtypeStruct((s, f), BF16), compiler_params=_params("parallel", "parallel"),
    )(g, u)


def _act_bwd(g, u, da, name):
    s, f = g.shape
    ts, tf = _tile(s, 512, 8), _tile(f, 1408)

    def body(g_ref, u_ref, da_ref, dg_ref, du_ref):
        gv, uv, dav = g_ref[...], u_ref[...], da_ref[...].astype(F32)
        sg = _sigmoid(gv)
        silu = gv * sg
        du_ref[...] = (dav * silu).astype(BF16)
        dg_ref[...] = (dav * uv * sg * (1.0 + gv * (1.0 - sg))).astype(BF16)

    blk = pl.BlockSpec((ts, tf), lambda i, j: (i, j))
    return pl.pallas_call(
        body, name=name, grid=(s // ts, f // tf), in_specs=[blk, blk, blk], out_specs=[blk, blk],
        out_shape=[jax.ShapeDtypeStruct((s, f), BF16)] * 2, compiler_params=_params("parallel", "parallel"),
    )(g, u, da)


def _loss_fwd_bwd(y, target):
    s, d = y.shape
    ts = _tile(s, 512, 8)

    def body(y_ref, t_ref, l_ref, dy_ref):
        diff = y_ref[...] - t_ref[...]
        dy_ref[...] = diff * (1.0 / d)
        part = jnp.full((1, LANES), 0.5 * jnp.sum(jnp.mean(diff * diff, axis=-1, keepdims=True)), F32)

        @pl.when(pl.program_id(0) == 0)
        def _():
            l_ref[...] = part

        @pl.when(pl.program_id(0) > 0)
        def _():
            l_ref[...] += part

    row = pl.BlockSpec((ts, d), lambda i: (i, 0))
    acc = pl.BlockSpec((1, LANES), lambda i: (0, 0))
    return pl.pallas_call(
        body, name="loss", grid=(s // ts,), in_specs=[row, row], out_specs=[acc, row],
        out_shape=[jax.ShapeDtypeStruct((1, LANES), F32), jax.ShapeDtypeStruct((s, d), F32)],
        compiler_params=_params("arbitrary"),
    )(y, target)


def _lane_consts():
    r, c = _iota((LANES, LANES), 0), _iota((LANES, LANES), 1)
    same = (r >> 6) == (c >> 6)
    rin, cin = r & 63, c & 63
    one = lambda cond: jnp.where(cond, 1.0, 0.0).astype(BF16)
    return dict(
        seg=one(same),
        rot=(jnp.where(same & (rin == cin + 32), -1.0, 0.0)
             + jnp.where(same & (cin == rin + 32), 1.0, 0.0)).astype(BF16),
        dup_lo=one(r == cin), dup_hi=one(r == cin + 64),
        up=one((c >= 64) & (r == c - 64)), down=one((c < 64) & (r == c + 64)),
        fold_lo=one((c < 64) & (rin == c)), fold_hi=one((c >= 64) & (rin == c - 64)),
    )


def _norm_rope(xc, gain, cos, sin, k):
    ss = _dot_x(xc * xc, k["seg"])
    rinv = lax.rsqrt(ss * (1.0 / ATT_HEAD_DIM) + EPS)
    xhat = xc * rinv
    y = xhat * gain
    return y * cos + _dot_x(y, k["rot"]) * sin, xhat, rinv


def _norm_rope_bwd(dr, xhat, rinv, gain, cos, sin, k):
    dy = dr * cos - _dot_x(dr * sin, k["rot"])
    dgain = jnp.sum(dy * xhat, axis=0, keepdims=True)
    dxh = dy * gain
    dx = rinv * (dxh - xhat * (_dot_x(dxh * xhat, k["seg"]) * (1.0 / ATT_HEAD_DIM)))
    return dx, dgain


def _attn_prep(qkv, cos, sin, gq, gk):
    s = qkv.shape[0]
    tr = _tile(s, 256, 8)

    def body(x_ref, cos_ref, sin_ref, gq_ref, gk_ref, q_ref, kk_ref, vlo_ref, vhi_ref):
        k = _lane_consts()
        cosv, sinv = cos_ref[...], sin_ref[...]
        lane = _iota((tr, LANES), 1)
        for j in range(Q_WIDTH // LANES):
            r, _, _ = _norm_rope(x_ref[:, j * LANES:(j + 1) * LANES], gq_ref[...], cosv, sinv, k)
            q_ref[:, j * LANES:(j + 1) * LANES] = r.astype(BF16)
        for i in range(KV_WIDTH // LANES):
            off = Q_WIDTH + i * LANES
            r, _, _ = _norm_rope(x_ref[:, off:off + LANES], gk_ref[...], cosv, sinv, k)
            rb = r.astype(BF16)
            kk_ref[:, (2 * i) * LANES:(2 * i + 1) * LANES] = _dot(rb, k["dup_lo"]).astype(BF16)
            kk_ref[:, (2 * i + 1) * LANES:(2 * i + 2) * LANES] = _dot(rb, k["dup_hi"]).astype(BF16)
            off = Q_WIDTH + KV_WIDTH + i * LANES
            vb = x_ref[:, off:off + LANES].astype(BF16)
            zero = jnp.zeros_like(vb)
            vlo_ref[:, (2 * i) * LANES:(2 * i + 1) * LANES] = jnp.where(lane < 64, vb, zero)
            vhi_ref[:, (2 * i) * LANES:(2 * i + 1) * LANES] = _dot(vb, k["up"]).astype(BF16)
            vlo_ref[:, (2 * i + 1) * LANES:(2 * i + 2) * LANES] = _dot(vb, k["down"]).astype(BF16)
            vhi_ref[:, (2 * i + 1) * LANES:(2 * i + 2) * LANES] = jnp.where(lane >= 64, vb, zero)

    w = qkv.shape[1]
    row = lambda width: pl.BlockSpec((tr, width), lambda i: (i, 0))
    vec = pl.BlockSpec((1, LANES), lambda i: (0, 0))
    kw = ATT_KV_HEADS * LANES
    return pl.pallas_call(
        body, name="attn_prep", grid=(s // tr,),
        in_specs=[row(w), row(LANES), row(LANES), vec, vec],
        out_specs=[row(Q_WIDTH), row(kw), row(kw), row(kw)],
        out_shape=[jax.ShapeDtypeStruct((s, Q_WIDTH), BF16)] + [jax.ShapeDtypeStruct((s, kw), BF16)] * 3,
        compiler_params=_params("parallel"),
    )(qkv, cos, sin, gq, gk)


def _band_mask(n):
    qi = _iota((ATT_BLOCK, 2 * ATT_BLOCK), 0)
    kj = _iota((ATT_BLOCK, 2 * ATT_BLOCK), 1)
    band = (kj > qi) & (kj <= qi + ATT_BLOCK)
    return band & ((kj >= ATT_BLOCK) | (n > 0))


def _softmax_sink(s, valid, sink):
    s = jnp.where(valid, s, -jnp.inf)
    m = jnp.maximum(jnp.max(s, axis=-1, keepdims=True), sink)
    p = jnp.exp(s - m)
    esink = jnp.exp(sink - m)
    inv = 1.0 / (jnp.sum(p, axis=-1, keepdims=True) + esink)
    return p * inv, esink * inv


def _attn_specs(order):
    if order == "nh":
        cur = lambda n, h: (n, h)
        prev = lambda n, h: (jnp.maximum(n - 1, 0), h)
    else:
        cur = lambda h, n: (n, h)
        prev = lambda h, n: (jnp.maximum(n - 1, 0), h)
    qs = pl.BlockSpec((ATT_BLOCK, ATT_GROUP * ATT_HEAD_DIM), cur)
    kc = pl.BlockSpec((ATT_BLOCK, LANES), cur)
    kp = pl.BlockSpec((ATT_BLOCK, LANES), prev)
    return qs, kc, kp


def _attn_fwd(q, kk, vlo, vhi, sinks):
    s = q.shape[0]
    nb = s // ATT_BLOCK
    scale = ATT_HEAD_DIM ** -0.5

    def body(sink_ref, q_ref, kc_ref, kp_ref, vloc_ref, vlop_ref, vhic_ref, vhip_ref, o_ref):
        n, h = pl.program_id(0), pl.program_id(1)
        valid = _band_mask(n)
        kw = jnp.concatenate([kp_ref[...], kc_ref[...]], axis=0)
        vw = (jnp.concatenate([vlop_ref[...], vloc_ref[...]], axis=0),
              jnp.concatenate([vhip_ref[...], vhic_ref[...]], axis=0))
        lane = _iota((ATT_BLOCK, LANES), 1)
        for jp in range(ATT_GROUP // 2):
            qp = q_ref[:, jp * LANES:(jp + 1) * LANES]
            acc = jnp.zeros((ATT_BLOCK, LANES), F32)
            for hf in range(2):
                qm = jnp.where((lane >= 64) == (hf == 1), qp, jnp.zeros_like(qp))
                sc = _dot_nt(qm, kw) * scale
                probs, _ = _softmax_sink(sc, valid, sink_ref[h * ATT_GROUP + 2 * jp + hf])
                acc = acc + _dot(probs.astype(BF16), vw[hf])
            o_ref[:, jp * LANES:(jp + 1) * LANES] = acc.astype(BF16)

    qs, kc, kp = _attn_specs("nh")
    return pl.pallas_call(
        body, name="attn_fwd", grid=(nb, ATT_KV_HEADS),
        in_specs=[pl.BlockSpec(memory_space=pltpu.SMEM), qs, kc, kp, kc, kp, kc, kp],
        out_specs=qs, out_shape=jax.ShapeDtypeStruct((s, Q_WIDTH), BF16),
        compiler_params=_params("parallel", "parallel"),
    )(sinks, q, kk, kk, vlo, vlo, vhi, vhi)


def _attn_bwd(q, kk, vlo, vhi, sinks, do):
    s = q.shape[0]
    nb = s // ATT_BLOCK
    scale = ATT_HEAD_DIM ** -0.5

    def body(sink_ref, q_ref, kc_ref, kp_ref, vloc_ref, vlop_ref, vhic_ref, vhip_ref, do_ref,
             dq_ref, dkc_ref, dkp_ref, dvloc_ref, dvlop_ref, dvhic_ref, dvhip_ref, dsink_ref):
        h, n = pl.program_id(0), pl.program_id(1)
        valid = _band_mask(n)
        kw = jnp.concatenate([kp_ref[...], kc_ref[...]], axis=0)
        vw = (jnp.concatenate([vlop_ref[...], vloc_ref[...]], axis=0),
              jnp.concatenate([vhip_ref[...], vhic_ref[...]], axis=0))
        lane = _iota((ATT_BLOCK, LANES), 1)
        sub = _iota((ATT_GROUP, LANES), 0)
        dkk = jnp.zeros((2 * ATT_BLOCK, LANES), F32)
        dv = [jnp.zeros((2 * ATT_BLOCK, LANES), F32), jnp.zeros((2 * ATT_BLOCK, LANES), F32)]
        dsink = jnp.zeros((ATT_GROUP, LANES), F32)
        for jp in range(ATT_GROUP // 2):
            qp = q_ref[:, jp * LANES:(jp + 1) * LANES]
            dop = do_ref[:, jp * LANES:(jp + 1) * LANES]
            dq = jnp.zeros((ATT_BLOCK, LANES), F32)
            for hf in range(2):
                mine = (lane >= 64) == (hf == 1)
                qm = jnp.where(mine, qp, jnp.zeros_like(qp))
                sc = _dot_nt(qm, kw) * scale
                probs, psink = _softmax_sink(sc, valid, sink_ref[h * ATT_GROUP + 2 * jp + hf])
                pb = probs.astype(BF16)
                dprobs = _dot_nt(dop, vw[hf])
                dv[hf] = dv[hf] + _dot_tn(pb, dop)
                delta = jnp.sum(probs * dprobs, axis=-1, keepdims=True)
                ds = (probs * (dprobs - delta) * scale).astype(BF16)
                dsink = dsink + jnp.where(sub == 2 * jp + hf, -jnp.sum(psink * delta), 0.0)
                dq = dq + jnp.where(mine, _dot(ds, kw), 0.0)
                dkk = dkk + _dot_tn(ds, qm)
            dq_ref[:, jp * LANES:(jp + 1) * LANES] = dq
        dkp_ref[...], dkc_ref[...] = dkk[:ATT_BLOCK], dkk[ATT_BLOCK:]
        dvlop_ref[...], dvloc_ref[...] = dv[0][:ATT_BLOCK], dv[0][ATT_BLOCK:]
        dvhip_ref[...], dvhic_ref[...] = dv[1][:ATT_BLOCK], dv[1][ATT_BLOCK:]

        @pl.when(n == 0)
        def _():
            dsink_ref[0] = dsink

        @pl.when(n > 0)
        def _():
            dsink_ref[0] += dsink

    qs, kc, kp = _attn_specs("hn")
    kw_shape = jax.ShapeDtypeStruct((s, ATT_KV_HEADS * LANES), F32)
    return pl.pallas_call(
        body, name="attn_bwd", grid=(ATT_KV_HEADS, nb),
        in_specs=[pl.BlockSpec(memory_space=pltpu.SMEM), qs, kc, kp, kc, kp, kc, kp, qs],
        out_specs=[qs] + [kc] * 6 + [pl.BlockSpec((1, ATT_GROUP, LANES), lambda h, n: (h, 0, 0))],
        out_shape=[jax.ShapeDtypeStruct((s, Q_WIDTH), F32)] + [kw_shape] * 6
        + [jax.ShapeDtypeStruct((ATT_KV_HEADS, ATT_GROUP, LANES), F32)],
        compiler_params=_params("parallel", "arbitrary"),
    )(sinks, q, kk, kk, vlo, vlo, vhi, vhi, do)


def _attn_prep_bwd(qkv, cos, sin, gq, gk, dq, dks, dvlos, dvhis):
    s, w = qkv.shape
    tr = ATT_BLOCK
    nb = s // tr

    def body(x_ref, cos_ref, sin_ref, gq_ref, gk_ref, dq_ref, dkc_ref, dkn_ref, dvloc_ref, dvlon_ref,
             dvhic_ref, dvhin_ref, dx_ref, dgq_ref, dgk_ref):
        n = pl.program_id(0)
        k = _lane_consts()
        cosv, sinv = cos_ref[...], sin_ref[...]
        nxt = jnp.where(n < nb - 1, 1.0, 0.0)
        lane = _iota((tr, LANES), 1)
        dgq = jnp.zeros((1, LANES), F32)
        dgk = jnp.zeros((1, LANES), F32)
        for j in range(Q_WIDTH // LANES):
            sl = slice(j * LANES, (j + 1) * LANES)
            _, xhat, rinv = _norm_rope(x_ref[:, sl], gq_ref[...], cosv, sinv, k)
            dx, dg = _norm_rope_bwd(dq_ref[:, sl], xhat, rinv, gq_ref[...], cosv, sinv, k)
            dx_ref[:, sl] = dx.astype(BF16)
            dgq = dgq + dg
        for i in range(KV_WIDTH // LANES):
            a, b = slice(2 * i * LANES, (2 * i + 1) * LANES), slice((2 * i + 1) * LANES, (2 * i + 2) * LANES)
            dr = (_dot_x(dkc_ref[:, a] + nxt * dkn_ref[:, a], k["fold_lo"])
                  + _dot_x(dkc_ref[:, b] + nxt * dkn_ref[:, b], k["fold_hi"]))
            sl = slice(Q_WIDTH + i * LANES, Q_WIDTH + (i + 1) * LANES)
            _, xhat, rinv = _norm_rope(x_ref[:, sl], gk_ref[...], cosv, sinv, k)
            dx, dg = _norm_rope_bwd(dr, xhat, rinv, gk_ref[...], cosv, sinv, k)
            dx_ref[:, sl] = dx.astype(BF16)
            dgk = dgk + dg
            ta = jnp.where(lane < 64, dvloc_ref[:, a] + nxt * dvlon_ref[:, a], dvhic_ref[:, a] + nxt * dvhin_ref[:, a])
            tb = jnp.where(lane < 64, dvloc_ref[:, b] + nxt * dvlon_ref[:, b], dvhic_ref[:, b] + nxt * dvhin_ref[:, b])
            sl = slice(Q_WIDTH + KV_WIDTH + i * LANES, Q_WIDTH + KV_WIDTH + (i + 1) * LANES)
            dx_ref[:, sl] = (_dot_x(ta, k["fold_lo"]) + _dot_x(tb, k["fold_hi"])).astype(BF16)

        @pl.when(n == 0)
        def _():
            dgq_ref[...] = dgq
            dgk_ref[...] = dgk

        @pl.when(n > 0)
        def _():
            dgq_ref[...] += dgq
            dgk_ref[...] += dgk

    row = lambda width: pl.BlockSpec((tr, width), lambda i: (i, 0))
    nxt_row = pl.BlockSpec((tr, ATT_KV_HEADS * LANES), lambda i: (jnp.minimum(i + 1, nb - 1), 0))
    vec = pl.BlockSpec((1, LANES), lambda i: (0, 0))
    kw = ATT_KV_HEADS * LANES
    return pl.pallas_call(
        body, name="attn_prep_bwd", grid=(nb,),
        in_specs=[row(w), row(LANES), row(LANES), vec, vec, row(Q_WIDTH),
                  row(kw), nxt_row, row(kw), nxt_row, row(kw), nxt_row],
        out_specs=[row(w), vec, vec],
        out_shape=[jax.ShapeDtypeStruct((s, w), BF16), jax.ShapeDtypeStruct((1, LANES), F32),
                   jax.ShapeDtypeStruct((1, LANES), F32)],
        compiler_params=_params("arbitrary"),
    )(qkv, cos, sin, gq, gk, dq, dks[0], dks[1], dvlos[0], dvlos[1], dvhis[0], dvhis[1])


CONV_HALO = 8
CONV_TC = 512
XBC_OFF = SSM_D_INNER // CONV_TC
DT_OFF = SSM_D_INNER + SSM_CONV_DIM


def _conv_pre(ext, w_ref, b_ref, ts):
    pre = b_ref[...] + w_ref[SSM_CONV - 1:SSM_CONV, :] * ext[CONV_HALO:]
    for kk in range(SSM_CONV - 1):
        pre = pre + w_ref[kk:kk + 1, :] * pltpu.roll(ext, SSM_CONV - 1 - kk, 0)[CONV_HALO:]
    return pre


def _conv_specs(ts):
    tc = CONV_TC
    src = pl.BlockSpec((ts, tc), lambda j, i: (i, XBC_OFF + j))
    halo = pl.BlockSpec((CONV_HALO, tc), lambda j, i: (jnp.maximum(i * (ts // CONV_HALO) - 1, 0), XBC_OFF + j))
    blk = pl.BlockSpec((ts, tc), lambda j, i: (i, j))
    wspec = pl.BlockSpec((SSM_CONV, tc), lambda j, i: (0, j))
    bspec = pl.BlockSpec((1, tc), lambda j, i: (0, j))
    return src, halo, blk, wspec, bspec


def _conv_fwd(zx, w, b):
    s, c = zx.shape[0], SSM_CONV_DIM
    ts = _tile(s, 512, 8)

    def body(u_ref, halo_ref, w_ref, b_ref, o_ref):
        halo = jnp.where(pl.program_id(1) > 0, halo_ref[...], 0.0)
        pre = _conv_pre(jnp.concatenate([halo, u_ref[...]], axis=0), w_ref, b_ref, ts)
        o_ref[...] = pre * _sigmoid(pre)

    src, halo, blk, wspec, bspec = _conv_specs(ts)
    return pl.pallas_call(
        body, name="conv_fwd", grid=(c // CONV_TC, s // ts),
        in_specs=[src, halo, wspec, bspec], out_specs=blk, out_shape=jax.ShapeDtypeStruct((s, c), F32),
        compiler_params=_params("parallel", "parallel"),
    )(zx, zx, w, b)


def _conv_bwd_pre(zx, w, b, dact):
    s, c = zx.shape[0], SSM_CONV_DIM
    ts = _tile(s, 512, 8)

    def body(u_ref, halo_ref, w_ref, b_ref, da_ref, dpre_ref, dw_ref, db_ref):
        i = pl.program_id(1)
        halo = jnp.where(i > 0, halo_ref[...], 0.0)
        ext = jnp.concatenate([halo, u_ref[...]], axis=0)
        pre = _conv_pre(ext, w_ref, b_ref, ts)
        sg = _sigmoid(pre)
        dpre = da_ref[...] * sg * (1.0 + pre * (1.0 - sg))
        dpre_ref[...] = dpre
        rows = [jnp.sum(dpre * pltpu.roll(ext, SSM_CONV - 1 - kk, 0)[CONV_HALO:], axis=0, keepdims=True)
                for kk in range(SSM_CONV - 1)]
        rows.append(jnp.sum(dpre * ext[CONV_HALO:], axis=0, keepdims=True))
        dwp = jnp.concatenate(rows, axis=0)
        dbp = jnp.sum(dpre, axis=0, keepdims=True)

        @pl.when(i == 0)
        def _():
            dw_ref[...] = dwp
            db_ref[...] = dbp

        @pl.when(i > 0)
        def _():
            dw_ref[...] += dwp
            db_ref[...] += dbp

    src, halo, blk, wspec, bspec = _conv_specs(ts)
    return pl.pallas_call(
        body, name="conv_bwd_pre", grid=(c // CONV_TC, s // ts),
        in_specs=[src, halo, wspec, bspec, blk], out_specs=[blk, wspec, bspec],
        out_shape=[jax.ShapeDtypeStruct((s, c), F32), jax.ShapeDtypeStruct((SSM_CONV, c), F32),
                   jax.ShapeDtypeStruct((1, c), F32)],
        compiler_params=_params("parallel", "arbitrary"),
    )(zx, zx, w, b, dact)


def _conv_bwd_in(dpre, w, dzx):
    s, c = dpre.shape
    ts, tc = _tile(s, 512, 8), CONV_TC
    ns = s // ts

    def body(d_ref, halo_ref, w_ref, dzx_ref, o_ref):
        del dzx_ref
        halo = jnp.where(pl.program_id(1) < ns - 1, halo_ref[...], 0.0)
        ext = jnp.concatenate([d_ref[...], halo], axis=0)
        du = w_ref[SSM_CONV - 1:SSM_CONV, :] * ext[:ts]
        for kk in range(SSM_CONV - 1):
            du = du + w_ref[kk:kk + 1, :] * pltpu.roll(ext, ts + CONV_HALO - (SSM_CONV - 1 - kk), 0)[:ts]
        o_ref[...] = du.astype(BF16)

    blk = pl.BlockSpec((ts, tc), lambda j, i: (i, j))
    halo = pl.BlockSpec((CONV_HALO, tc), lambda j, i: (jnp.minimum((i + 1) * (ts // CONV_HALO), s // CONV_HALO - 1), j))
    return pl.pallas_call(
        body, name="conv_bwd_in", grid=(c // tc, ns),
        in_specs=[blk, halo, pl.BlockSpec((SSM_CONV, tc), lambda j, i: (0, j)), ANY],
        out_specs=pl.BlockSpec((ts, tc), lambda j, i: (i, XBC_OFF + j)),
        out_shape=jax.ShapeDtypeStruct(dzx.shape, BF16), input_output_aliases={3: 0},
        compiler_params=_params("parallel", "parallel"),
    )(dpre, dpre, w, dzx)


def _ssd_common(dt_ref, dtt_ref, bias_ref, biast_ref, alog_ref, alogt_ref):
    ln = SSM_CHUNK
    raw, rawt = dt_ref[0] + bias_ref[0], dtt_ref[0] + biast_ref[0]
    dt, dtt = _softplus(raw), _softplus(rawt)
    a, at = -jnp.exp(alog_ref[0]), -jnp.exp(alogt_ref[0])
    tri = jnp.where(_iota((ln, ln), 0) >= _iota((ln, ln), 1), 1.0, 0.0).astype(BF16)
    return dict(raw=raw, rawt=rawt, dt=dt, dtt=dtt, a=a, at=at, tri=tri,
                acum=_xdot(tri, dt * a), acumt=_dot_x_nt(dtt * at, tri))


def _ssd_specs(nc, rev):
    cidx = (lambda c: nc - 1 - c) if rev else (lambda c: c)
    ln = SSM_CHUNK
    xs = pl.BlockSpec((ln, SSM_GN), lambda g, c: (cidx(c), g))
    bs = pl.BlockSpec((ln, SSM_STATE), lambda g, c: (cidx(c), SSM_D_INNER // SSM_STATE + g))
    cs = pl.BlockSpec((ln, SSM_STATE), lambda g, c: (cidx(c), SSM_D_INNER // SSM_STATE + SSM_GROUPS + g))
    dt = pl.BlockSpec((1, ln, SSM_HPG), lambda g, c: (g, cidx(c), 0))
    dtt = pl.BlockSpec((1, SSM_HPG, ln), lambda g, c: (g, 0, cidx(c)))
    row = pl.BlockSpec((1, 1, SSM_HPG), lambda g, c: (g, 0, 0))
    col = pl.BlockSpec((1, SSM_HPG, 1), lambda g, c: (g, 0, 0))
    st = pl.BlockSpec((1, SSM_HPG, SSM_P, SSM_STATE), lambda g, c: (cidx(c), g, 0, 0))
    return xs, bs, cs, dt, dtt, row, col, st


def _ssd_fwd(xbc, dt_g, dt_gt, bias_r, bias_c, alog_r, alog_c, d_r):
    s = xbc.shape[0]
    ln = SSM_CHUNK
    nc = s // ln

    def body(x_ref, b_ref, c_ref, dt_ref, dtt_ref, bias_ref, biast_ref, alog_ref, alogt_ref, d_ref,
             y_ref, st_ref, state):
        @pl.when(pl.program_id(1) == 0)
        def _():
            state[...] = jnp.zeros_like(state)

        cm = _ssd_common(dt_ref, dtt_ref, bias_ref, biast_ref, alog_ref, alogt_ref)
        dt, acum, acumt = cm["dt"], cm["acum"], cm["acumt"]
        bb, cb = b_ref[...].astype(BF16), c_ref[...].astype(BF16)
        cbm = _dot_nt(cb, bb)
        causal = _iota((ln, ln), 0) >= _iota((ln, ln), 1)
        st_ref[0] = state[...]
        for r in range(SSM_HPG):
            xr = x_ref[:, r * SSM_P:(r + 1) * SSM_P]
            ac, last = acum[:, r:r + 1], acum[ln - 1:ln, r:r + 1]
            decay = jnp.exp(jnp.where(causal, ac - acumt[r:r + 1, :], -jnp.inf))
            xdt = xr * dt[:, r:r + 1]
            sr = state[r]
            y = (_dot((cbm * decay).astype(BF16), xdt.astype(BF16))
                 + _dot_nt(cb, sr.astype(BF16)) * jnp.exp(ac) + d_ref[0][:, r:r + 1] * xr)
            y_ref[:, r * SSM_P:(r + 1) * SSM_P] = y
            state[r] = sr * jnp.exp(last) + _dot_tn((xdt * jnp.exp(last - ac)).astype(BF16), bb)

    xs, bs, cs, dts, dtts, row, col, st = _ssd_specs(nc, False)
    return pl.pallas_call(
        body, name="ssd_fwd", grid=(SSM_GROUPS, nc),
        in_specs=[xs, bs, cs, dts, dtts, row, col, row, col, row],
        out_specs=[xs, st],
        out_shape=[jax.ShapeDtypeStruct((s, SSM_D_INNER), F32),
                   jax.ShapeDtypeStruct((nc, SSM_HEADS, SSM_P, SSM_STATE), F32)],
        scratch_shapes=[pltpu.VMEM((SSM_HPG, SSM_P, SSM_STATE), F32)],
        compiler_params=_params("parallel", "arbitrary"),
    )(xbc, xbc, xbc, dt_g, dt_gt, bias_r, bias_c, alog_r, alog_c, d_r)


def _ssd_bwd(xbc, dt_g, dt_gt, bias_r, bias_c, alog_r, alog_c, d_r, states, dy):
    s = xbc.shape[0]
    ln = SSM_CHUNK
    nc = s // ln

    def body(x_ref, b_ref, c_ref, dt_ref, dtt_ref, bias_ref, biast_ref, alog_ref, alogt_ref, d_ref,
             st_ref, dy_ref, dx_ref, db_ref, dc_ref, ddt_ref, ddtt_ref, dbias_ref, dbiast_ref,
             dalog_ref, dalogt_ref, dd_ref, dstate):
        step = pl.program_id(1)

        @pl.when(step == 0)
        def _():
            dstate[...] = jnp.zeros_like(dstate)

        cm = _ssd_common(dt_ref, dtt_ref, bias_ref, biast_ref, alog_ref, alogt_ref)
        dt, acum, acumt = cm["dt"], cm["acum"], cm["acumt"]
        bb, cb = b_ref[...].astype(BF16), c_ref[...].astype(BF16)
        cbm = _dot_nt(cb, bb)
        causal = _iota((ln, ln), 0) >= _iota((ln, ln), 1)
        lane8 = _iota((ln, SSM_HPG), 1)
        sub8 = _iota((SSM_HPG, ln), 0)
        lane1 = _iota((1, SSM_HPG), 1)
        is_last = _iota((ln, 1), 0) == ln - 1
        dcb = jnp.zeros((ln, ln), F32)
        dc_acc = jnp.zeros((ln, SSM_STATE), F32)
        db_acc = jnp.zeros((ln, SSM_STATE), F32)
        dac_rows = jnp.zeros((ln, SSM_HPG), F32)
        dac_cols = jnp.zeros((SSM_HPG, ln), F32)
        ddt_all = jnp.zeros((ln, SSM_HPG), F32)
        dd_all = jnp.zeros((1, SSM_HPG), F32)
        for r in range(SSM_HPG):
            sl = slice(r * SSM_P, (r + 1) * SSM_P)
            xr, dyr = x_ref[:, sl], dy_ref[:, sl]
            dtc, dr = dt[:, r:r + 1], d_ref[0][:, r:r + 1]
            ac, last = acum[:, r:r + 1], acum[ln - 1:ln, r:r + 1]
            decay = jnp.exp(jnp.where(causal, ac - acumt[r:r + 1, :], -jnp.inf))
            w = (cbm * decay).astype(BF16)
            xdt = xr * dtc
            xdtb, dyb = xdt.astype(BF16), dyr.astype(BF16)
            eac, to_end, elast = jnp.exp(ac), jnp.exp(last - ac), jnp.exp(last)
            sr, dsr = st_ref[0, r], dstate[r]
            srb, dsrb = sr.astype(BF16), dsr.astype(BF16)
            dxdt_state = _dot_nt(bb, dsrb) * to_end
            dxdt = _dot_tn(w, dyb) + dxdt_state
            dcb_r = _dot_nt(dyb, xdtb) * decay
            dcb = dcb + dcb_r
            e = dcb_r * cbm
            dc_acc = dc_acc + _dot(dyb, srb) * eac
            db_acc = db_acc + _dot((xdt * to_end).astype(BF16), dsrb)
            yoff = _dot_nt(cb, srb) * eac
            f_rows = jnp.sum(xdt * dxdt_state, axis=-1, keepdims=True)
            dlast = jnp.sum(f_rows) + elast * jnp.sum(dsr * sr)
            dac = (jnp.sum(e, axis=-1, keepdims=True) + jnp.sum(dyr * yoff, axis=-1, keepdims=True) - f_rows
                   + jnp.where(is_last, dlast, 0.0))
            dac_rows = dac_rows + jnp.where(lane8 == r, dac, 0.0)
            dac_cols = dac_cols + jnp.where(sub8 == r, jnp.sum(e, axis=0, keepdims=True), 0.0)
            ddt_all = ddt_all + jnp.where(lane8 == r, jnp.sum(dxdt * xr, axis=-1, keepdims=True), 0.0)
            dd_all = dd_all + jnp.where(lane1 == r, jnp.sum(dyr * xr), 0.0)
            dx_ref[:, sl] = dxdt * dtc + dr * dyr
            dstate[r] = elast * dsr + _dot_tn((dyr * eac).astype(BF16), cb)
        dcbb = dcb.astype(BF16)
        dc_ref[...] = dc_acc + _dot(dcbb, bb)
        db_ref[...] = db_acc + _dot_tn(dcbb, cb)
        triu = jnp.where(_iota((ln, ln), 0) <= _iota((ln, ln), 1), 1.0, 0.0).astype(BF16)
        g_rows = _xdot(triu, dac_rows)
        g_cols = _dot_x(dac_cols, cm["tri"])
        d_rows = (ddt_all + g_rows * cm["a"]) * _sigmoid(cm["raw"])
        d_cols = -(g_cols * cm["at"]) * _sigmoid(cm["rawt"])
        ddt_ref[0] = d_rows
        ddtt_ref[0] = d_cols
        parts = (jnp.sum(d_rows, axis=0, keepdims=True), jnp.sum(d_cols, axis=1, keepdims=True),
                 jnp.sum(g_rows * dt, axis=0, keepdims=True) * cm["a"],
                 -jnp.sum(g_cols * cm["dtt"], axis=1, keepdims=True) * cm["at"], dd_all)
        outs = (dbias_ref, dbiast_ref, dalog_ref, dalogt_ref, dd_ref)

        @pl.when(step == 0)
        def _():
            for o_ref, p in zip(outs, parts):
                o_ref[0] = p

        @pl.when(step > 0)
        def _():
            for o_ref, p in zip(outs, parts):
                o_ref[0] += p

    xs, bs, cs, dts, dtts, row, col, st = _ssd_specs(nc, True)
    grp = pl.BlockSpec((ln, SSM_STATE), lambda g, c: (nc - 1 - c, g))
    rows = jax.ShapeDtypeStruct((SSM_GROUPS, 1, SSM_HPG), F32)
    cols = jax.ShapeDtypeStruct((SSM_GROUPS, SSM_HPG, 1), F32)
    return pl.pallas_call(
        body, name="ssd_bwd", grid=(SSM_GROUPS, nc),
        in_specs=[xs, bs, cs, dts, dtts, row, col, row, col, row, st, xs],
        out_specs=[xs, grp, grp, dts, dtts, row, col, row, col, row],
        out_shape=[jax.ShapeDtypeStruct((s, SSM_D_INNER), F32),
                   jax.ShapeDtypeStruct((s, SSM_GROUPS * SSM_STATE), F32),
                   jax.ShapeDtypeStruct((s, SSM_GROUPS * SSM_STATE), F32),
                   jax.ShapeDtypeStruct((SSM_GROUPS, s, SSM_HPG), F32),
                   jax.ShapeDtypeStruct((SSM_GROUPS, SSM_HPG, s), F32), rows, cols, rows, cols, rows],
        scratch_shapes=[pltpu.VMEM((SSM_HPG, SSM_P, SSM_STATE), F32)],
        compiler_params=_params("parallel", "arbitrary"),
    )(xbc, xbc, xbc, dt_g, dt_gt, bias_r, bias_c, alog_r, alog_c, d_r, states, dy)


def _gate_norm_fwd(y, zx, g):
    s = y.shape[0]
    ts = _tile(s, 512, 8)

    def body(y_ref, z_ref, g_ref, o_ref):
        zv = z_ref[...]
        yg = y_ref[...] * (zv * _sigmoid(zv))
        r = lax.rsqrt(jnp.mean(yg * yg, axis=-1, keepdims=True) + EPS)
        o_ref[...] = (yg * r * g_ref[...]).astype(BF16)

    blk = pl.BlockSpec((ts, SSM_GN), lambda j, i: (i, j))
    vec = pl.BlockSpec((1, SSM_GN), lambda j, i: (0, j))
    return pl.pallas_call(
        body, name="gate_norm_fwd", grid=(SSM_GROUPS, s // ts), in_specs=[blk, blk, vec], out_specs=blk,
        out_shape=jax.ShapeDtypeStruct((s, SSM_D_INNER), BF16), compiler_params=_params("parallel", "parallel"),
    )(y, zx, g)


def _gate_norm_bwd(y, zx, g, dout):
    s = y.shape[0]
    ts = _tile(s, 512, 8)

    def body(y_ref, z_ref, g_ref, do_ref, dy_ref, dz_ref, dg_ref):
        yv, zv, dov = y_ref[...], z_ref[...], do_ref[...].astype(F32)
        sg = _sigmoid(zv)
        silu = zv * sg
        yg = yv * silu
        r = lax.rsqrt(jnp.mean(yg * yg, axis=-1, keepdims=True) + EPS)
        ygn = yg * r
        part = jnp.sum(dov * ygn, axis=0, keepdims=True)

        @pl.when(pl.program_id(1) == 0)
        def _():
            dg_ref[...] = part

        @pl.when(pl.program_id(1) > 0)
        def _():
            dg_ref[...] += part

        dn = dov * g_ref[...]
        dyg = r * (dn - ygn * jnp.mean(dn * ygn, axis=-1, keepdims=True))
        dy_ref[...] = dyg * silu
        dz_ref[...] = (dyg * yv * sg * (1.0 + zv * (1.0 - sg))).astype(BF16)

    blk = pl.BlockSpec((ts, SSM_GN), lambda j, i: (i, j))
    vec = pl.BlockSpec((1, SSM_GN), lambda j, i: (0, j))
    return pl.pallas_call(
        body, name="gate_norm_bwd", grid=(SSM_GROUPS, s // ts), in_specs=[blk, blk, vec, blk],
        out_specs=[blk, blk, vec],
        out_shape=[jax.ShapeDtypeStruct((s, SSM_D_INNER), F32), jax.ShapeDtypeStruct((s, SSM_IN_PAD), BF16),
                   jax.ShapeDtypeStruct((1, SSM_D_INNER), F32)],
        compiler_params=_params("parallel", "arbitrary"),
    )(y, zx, g, dout)


def _rope_tables(positions):
    inv_freq = ROPE_THETA ** (-jnp.arange(0, ATT_HEAD_DIM, 2, dtype=F32) / ATT_HEAD_DIM)
    ang = positions.astype(F32)[:, None] * inv_freq
    return jnp.tile(jnp.cos(ang), (1, 4)), jnp.tile(jnp.sin(ang), (1, 4))


def _group_views(v):
    return v.reshape(SSM_GROUPS, 1, SSM_HPG), v.reshape(SSM_GROUPS, SSM_HPG, 1)


def _ffn_fwd(x, norm_g, wg, wu, wd, tag):
    h = _rms_fwd(x, norm_g, f"ffn_norm_{tag}")
    g = _mm(h, wg, "nn", f"ffn_gate_{tag}", b_cols=True)
    u = _mm(h, wu, "nn", f"ffn_up_{tag}", b_cols=True)
    a = _act_fwd(g, u, f"ffn_act_{tag}")
    return _mm(a, wd, "nn", f"ffn_down_{tag}", add=x), (h, g, u, a)


def _ffn_bwd(x, norm_g, wg, wu, wd, saved, dout, tag):
    h, g, u, a = saved
    da = _mm(dout, wd, "nt", f"ffn_down_dx_{tag}", out_dtype=BF16)
    dwd = _mm(a, dout, "tn", f"ffn_down_dw_{tag}", out_dtype=BF16)
    dg, du = _act_bwd(g, u, da, f"ffn_act_bwd_{tag}")
    dwg = _mm(h, dg, "tn", f"ffn_gate_dw_{tag}", out_dtype=BF16, out_cols=True)
    dwu = _mm(h, du, "tn", f"ffn_up_dw_{tag}", out_dtype=BF16, out_cols=True)
    dh = _mm(dg, wg, "nt", f"ffn_gate_dx_{tag}", b_cols=True)
    dh = _mm(du, wu, "nt", f"ffn_up_dx_{tag}", add=dh, b_cols=True)
    dx, dgain = _rms_bwd(x, norm_g, dh, dout, f"ffn_norm_bwd_{tag}")
    return dx, dgain, dwg, dwu, dwd


def _local_step(x, positions, target, w):
    cos, sin = _rope_tables(positions)
    row = lambda v: v.reshape(1, -1)
    gq, gk = jnp.tile(row(w["attn_q_norm"]), (1, 2)), jnp.tile(row(w["attn_k_norm"]), (1, 2))
    sinks = w["attn_sinks"].reshape(-1)
    s = x.shape[0]

    h0 = _rms_fwd(x, row(w["mixer_norm"][0]), "mixer_norm_0")
    qkv = _mm(h0, w["attn_w_qkv"], "nn", "attn_qkv", b_cols=True)
    q, kk, vlo, vhi = _attn_prep(qkv, cos, sin, gq, gk)
    o = _attn_fwd(q, kk, vlo, vhi, sinks)
    x1 = _mm(o, w["attn_w_o"], "nn", "attn_out", add=x)
    ffn_w = lambda l: (row(w["ffn_norm"][l]), w["ffn_w_gate"][l], w["ffn_w_up"][l], w["ffn_w_down"][l])
    x2, ffn0 = _ffn_fwd(x1, *ffn_w(0), 0)

    h2 = _rms_fwd(x2, row(w["mixer_norm"][1]), "mixer_norm_1")
    zx = _mm(h2, w["ssm_w_in"], "nn", "ssm_in")
    dt_g = zx[:, DT_OFF:DT_OFF + SSM_HEADS].reshape(s, SSM_GROUPS, SSM_HPG).transpose(1, 0, 2)
    dt_gt = dt_g.transpose(0, 2, 1)
    bias_r, bias_c = _group_views(w["ssm_dt_bias"].reshape(-1))
    alog_r, alog_c = _group_views(w["ssm_a_log"].reshape(-1))
    d_r, _ = _group_views(w["ssm_d"].reshape(-1))
    xbc = _conv_fwd(zx, w["ssm_conv_w"], row(w["ssm_conv_b"]))
    ssd_args = (xbc, dt_g, dt_gt, bias_r, bias_c, alog_r, alog_c, d_r)
    y, states = _ssd_fwd(*ssd_args)
    yn = _gate_norm_fwd(y, zx, row(w["ssm_norm"]))
    x3 = _mm(yn, w["ssm_w_out"], "nn", "ssm_out", add=x2)
    x4, ffn1 = _ffn_fwd(x3, *ffn_w(1), 1)

    loss_row, dx4 = _loss_fwd_bwd(x4, target)

    dx3, dfn1, dwg1, dwu1, dwd1 = _ffn_bwd(x3, *ffn_w(1), ffn1, dx4, 1)
    dyn = _mm(dx3, w["ssm_w_out"], "nt", "ssm_out_dx")
    dw_out = _mm(yn, dx3, "tn", "ssm_out_dw", out_dtype=BF16)
    dy, dzx, dssm_norm = _gate_norm_bwd(y, zx, row(w["ssm_norm"]), dyn)
    dxs, db, dc, ddt_g, ddt_gt, dbias, dbias_t, dalog, dalog_t, dd = _ssd_bwd(*ssd_args, states, dy)
    ddt_g = ddt_g + ddt_gt.transpose(0, 2, 1)
    dpre, dconv_w, dconv_b = _conv_bwd_pre(zx, w["ssm_conv_w"], row(w["ssm_conv_b"]),
                                           jnp.concatenate([dxs, db, dc], axis=1))
    dzx = _conv_bwd_in(dpre, w["ssm_conv_w"], dzx)
    ddt_pad = jnp.pad(ddt_g.transpose(1, 0, 2).reshape(s, SSM_HEADS), ((0, 0), (0, SSM_IN_PAD - SSM_IN)))
    dzx = lax.dynamic_update_slice(dzx, ddt_pad.astype(BF16), (0, DT_OFF))
    dw_in = _mm(h2, dzx, "tn", "ssm_in_dw", out_dtype=BF16)
    dh2 = _mm(dzx, w["ssm_w_in"], "nt", "ssm_in_dx")
    dx2, dmn1 = _rms_bwd(x2, row(w["mixer_norm"][1]), dh2, dx3, "mixer_norm_bwd_1")

    dx1, dfn0, dwg0, dwu0, dwd0 = _ffn_bwd(x1, *ffn_w(0), ffn0, dx2, 0)
    do = _mm(dx1, w["attn_w_o"], "nt", "attn_out_dx", out_dtype=BF16)
    dw_o = _mm(o, dx1, "tn", "attn_out_dw", out_dtype=BF16)
    dq, dkc, dkp, dvloc, dvlop, dvhic, dvhip, dsink = _attn_bwd(q, kk, vlo, vhi, sinks, do)
    dqkv, dgq, dgk = _attn_prep_bwd(qkv, cos, sin, gq, gk, dq, (dkc, dkp), (dvloc, dvlop), (dvhic, dvhip))
    dw_qkv = _mm(h0, dqkv, "tn", "attn_qkv_dw", out_dtype=BF16, out_cols=True)
    dh0 = _mm(dqkv, w["attn_w_qkv"], "nt", "attn_qkv_dx", b_cols=True)
    dx0, dmn0 = _rms_bwd(x, row(w["mixer_norm"][0]), dh0, dx1, "mixer_norm_bwd_0")

    fold = lambda v: v[0, :ATT_HEAD_DIM] + v[0, ATT_HEAD_DIM:]
    row_stack = lambda g: g.reshape(N_SHARDS, g.shape[0] // N_SHARDS, g.shape[1])
    in_shard = SSM_IN // N_SHARDS
    grads = {
        "mixer_norm": jnp.concatenate([dmn0, dmn1], axis=0),
        "ffn_norm": jnp.concatenate([dfn0, dfn1], axis=0),
        "attn_w_qkv": [dw_qkv],
        "attn_q_norm": fold(dgq), "attn_k_norm": fold(dgk),
        "attn_sinks": dsink[:, :, 0].reshape(-1),
        "attn_w_o": [row_stack(dw_o)],
        "ssm_w_in": [jnp.stack([dw_in[:, i * in_shard:(i + 1) * in_shard] for i in range(N_SHARDS)])],
        "ssm_conv_w": dconv_w, "ssm_conv_b": dconv_b.reshape(-1),
        "ssm_dt_bias": dbias.reshape(-1) + dbias_t.reshape(-1),
        "ssm_a_log": dalog.reshape(-1) + dalog_t.reshape(-1), "ssm_d": dd.reshape(-1),
        "ssm_norm": dssm_norm.reshape(-1),
        "ssm_w_out": [row_stack(dw_out)],
        "ffn_w_gate": [dwg0, dwg1], "ffn_w_up": [dwu0, dwu1],
        "ffn_w_down": [row_stack(dwd0), row_stack(dwd1)],
    }
    return loss_row[0, 0], dx0, grads


OTHER_CHIPS = ((1, 0), (0, 1), (1, 1))


def _position():
    return lax.axis_index("x"), lax.axis_index("y"), lax.axis_index("c")


def _sems(n):
    return pltpu.SemaphoreType.DMA((n,))


def _gather_shards(weights, layers):
    n_in, n_mat = len(weights), len(layers)

    def body(*refs):
        p, out = refs[:n_in], refs[n_in:n_in + n_mat]
        send_sems, recv_sems, local_sems = refs[n_in + n_mat:]
        x, y, c = _position()
        me, sibling = (x, y, c), (x, y, 1 - c)
        chips = [(x ^ fx, y ^ fy) for fx, fy in OTHER_CHIPS]

        def rows(e, px, py, pc):
            half = out[e].shape[1] // 2
            return out[e].at[2 * px + py, pl.ds(pc * half, half), :]

        def copy(k, e, block, to, src=None):
            return pltpu.make_async_remote_copy(
                src_ref=rows(e, *block) if src is None else src, dst_ref=rows(e, *block),
                send_sem=send_sems.at[k * n_mat + e], recv_sem=recv_sems.at[k * n_mat + e],
                device_id=to, device_id_type=MESH)

        mine, first, passed = [], [], []
        for e, (i, l) in enumerate(layers):
            half = out[e].shape[1] // 2
            mine.append(pltpu.make_async_copy(p[i].at[l], out[e].at[2 * x + y], local_sems.at[e]))
            mine[-1].start()
            first.append([copy(j, e, me, (*chip, c), src=p[i].at[l, pl.ds(c * half, half), :])
                          for j, chip in enumerate(chips)])
            for cp in first[-1]:
                cp.start()
        for e in range(n_mat):
            passed.append([copy(3 + j, e, (*chip, c), sibling) for j, chip in enumerate(chips)])
            for j, chip in enumerate(chips):
                copy(j, e, (*chip, c), me).wait_recv()
                passed[e][j].start()
        for e in range(n_mat):
            for j, chip in enumerate(chips):
                copy(3 + j, e, (*chip, 1 - c), me).wait_recv()
        for e in range(n_mat):
            for cp in first[e] + passed[e]:
                cp.wait_send()
            mine[e].wait()

    return pl.pallas_call(
        body, name="gather_weights", in_specs=[ANY] * n_in, out_specs=[ANY] * n_mat,
        out_shape=[jax.ShapeDtypeStruct((N_SHARDS,) + weights[i].shape[1:], weights[i].dtype) for i, _ in layers],
        scratch_shapes=[_sems(6 * n_mat), _sems(6 * n_mat), _sems(n_mat)],
    )(*weights)


def _all_gather8(block, name):
    m_per, n = block.shape

    def body(x_ref, out_ref, send_sems, recv_sems, local_sem):
        x, y, c = _position()
        me, sibling = (x, y, c), (x, y, 1 - c)
        chips = [(x ^ fx, y ^ fy) for fx, fy in OTHER_CHIPS]

        def rows(px, py, pc):
            return out_ref.at[pl.ds((4 * px + 2 * py + pc) * m_per, m_per), :]

        def copy(k, blk, to, src=None):
            return pltpu.make_async_remote_copy(
                src_ref=rows(*blk) if src is None else src, dst_ref=rows(*blk),
                send_sem=send_sems.at[k], recv_sem=recv_sems.at[k], device_id=to, device_id_type=MESH)

        mine = pltpu.make_async_copy(x_ref, rows(*me), local_sem)
        mine.start()
        first = [copy(0, me, sibling, src=x_ref)]
        first += [copy(1 + j, me, (*chip, c), src=x_ref) for j, chip in enumerate(chips)]
        for cp in first:
            cp.start()
        passed = [copy(4 + j, (*chip, c), sibling) for j, chip in enumerate(chips)]
        for j, chip in enumerate(chips):
            copy(1 + j, (*chip, c), me).wait_recv()
            passed[j].start()
        copy(0, sibling, me).wait_recv()
        for j, chip in enumerate(chips):
            copy(4 + j, (*chip, 1 - c), me).wait_recv()
        for cp in first + passed:
            cp.wait_send()
        mine.wait()

    return pl.pallas_call(
        body, name=name, out_shape=jax.ShapeDtypeStruct((N_DEV * m_per, n), block.dtype),
        in_specs=[pl.BlockSpec(memory_space=pltpu.VMEM)], out_specs=pl.BlockSpec(memory_space=pltpu.VMEM),
        scratch_shapes=[_sems(7), _sems(7), pltpu.SemaphoreType.DMA],
    )(block)


def _swap_other_half(stacks):
    n_mat = len(stacks)

    def body(*refs):
        g, out, send_sems, recv_sems = refs[:n_mat], refs[n_mat:2 * n_mat], refs[-2], refs[-1]
        x, y, c = _position()
        copies = []
        for e in range(n_mat):
            half = g[e].shape[1] // 2
            copies.append(pltpu.make_async_remote_copy(
                src_ref=g[e].at[:, pl.ds((1 - c) * half, half), :], dst_ref=out[e], send_sem=send_sems.at[e],
                recv_sem=recv_sems.at[e], device_id=(x, y, 1 - c), device_id_type=MESH))
            copies[-1].start()
        for cp in copies:
            cp.wait()

    return pl.pallas_call(
        body, name="grads_to_sibling", in_specs=[ANY] * n_mat, out_specs=[ANY] * n_mat,
        out_shape=[jax.ShapeDtypeStruct((N_SHARDS, g.shape[1] // 2, g.shape[2]), g.dtype) for g in stacks],
        scratch_shapes=[_sems(n_mat), _sems(n_mat)],
    )(*stacks)


def _send_to_owners(partials):
    n_mat = len(partials)

    def body(*refs):
        p, out, send_sems, recv_sems = refs[:n_mat], refs[n_mat:2 * n_mat], refs[-2], refs[-1]
        x, y, c = _position()
        copies = []
        for e in range(n_mat):
            for k, (fx, fy) in enumerate(OTHER_CHIPS):
                px, py = x ^ fx, y ^ fy
                copies.append(pltpu.make_async_remote_copy(
                    src_ref=p[e].at[2 * px + py], dst_ref=out[e].at[k], send_sem=send_sems.at[3 * e + k],
                    recv_sem=recv_sems.at[3 * e + k], device_id=(px, py, c), device_id_type=MESH))
                copies[-1].start()
        for cp in copies:
            cp.wait()

    return pl.pallas_call(
        body, name="grads_to_owners", in_specs=[ANY] * n_mat, out_specs=[ANY] * n_mat,
        out_shape=[jax.ShapeDtypeStruct((len(OTHER_CHIPS),) + p.shape[1:], p.dtype) for p in partials],
        scratch_shapes=[_sems(3 * n_mat), _sems(3 * n_mat)],
    )(*partials)


def _share_halves(totals, layers, shapes):
    n_mat, n_out = len(totals), len(shapes)

    def body(*refs):
        t, out = refs[:n_mat], refs[n_mat:n_mat + n_out]
        send_sems, recv_sems, local_sems = refs[n_mat + n_out:]
        x, y, c = _position()
        copies = []
        for e, (i, l) in enumerate(layers):
            half = t[e].shape[0]
            dst = out[i].at[l, pl.ds(c * half, half), :]
            copies.append(pltpu.make_async_copy(t[e], dst, local_sems.at[e]))
            copies[-1].start()
            copies.append(pltpu.make_async_remote_copy(
                src_ref=t[e], dst_ref=dst, send_sem=send_sems.at[e], recv_sem=recv_sems.at[e],
                device_id=(x, y, 1 - c), device_id_type=MESH))
            copies[-1].start()
        for cp in copies:
            cp.wait()

    return pl.pallas_call(
        body, name="grads_share_halves", in_specs=[ANY] * n_mat, out_specs=[ANY] * n_out,
        out_shape=[jax.ShapeDtypeStruct(shp, F32) for shp in shapes],
        scratch_shapes=[_sems(n_mat), _sems(n_mat), _sems(n_mat)],
    )(*totals)


ADD_BLOCK_ELEMS = 1 << 19


def _add_rows(half, cols):
    return _tile(half, max(16, ADD_BLOCK_ELEMS // cols // 16 * 16), 16)


def _add_pair(stack, recv, c_idx, name):
    _, half, cols = recv.shape
    tr = _add_rows(half, cols)
    nt = half // tr

    def body(c_ref, a_ref, b_ref, o_ref):
        o_ref[...] = (a_ref[...].astype(F32) + b_ref[...].astype(F32)).astype(o_ref.dtype)

    blk = pl.BlockSpec((None, tr, cols), lambda s, i, c_ref: (s, i, 0))
    return pl.pallas_call(
        body, name=name,
        grid_spec=pltpu.PrefetchScalarGridSpec(
            num_scalar_prefetch=1, grid=(N_SHARDS, nt),
            in_specs=[pl.BlockSpec((None, tr, cols), lambda s, i, c_ref: (s, c_ref[0] * nt + i, 0)), blk],
            out_specs=blk),
        out_shape=jax.ShapeDtypeStruct(recv.shape, recv.dtype),
        compiler_params=_params("parallel", "parallel"),
    )(c_idx, stack, recv)


def _add_owned(partial, recv, s_idx, name):
    _, half, cols = partial.shape
    tr = _add_rows(half, cols)

    def body(s_ref, a_ref, r0_ref, r1_ref, r2_ref, o_ref):
        o_ref[...] = (((a_ref[...].astype(F32) + r0_ref[...].astype(F32)) + r1_ref[...].astype(F32))
                      + r2_ref[...].astype(F32))

    slot = lambda k: pl.BlockSpec((None, tr, cols), lambda i, s_ref: (k, i, 0))
    return pl.pallas_call(
        body, name=name,
        grid_spec=pltpu.PrefetchScalarGridSpec(
            num_scalar_prefetch=1, grid=(half // tr,),
            in_specs=[pl.BlockSpec((None, tr, cols), lambda i, s_ref: (s_ref[0], i, 0)), slot(0), slot(1), slot(2)],
            out_specs=pl.BlockSpec((tr, cols), lambda i, s_ref: (i, 0))),
        out_shape=jax.ShapeDtypeStruct((half, cols), F32),
        compiler_params=_params("parallel"),
    )(s_idx, partial, recv, recv, recv)


def _sum8(gathered):
    m = gathered.shape[0] // N_DEV

    def body(g_ref, o_ref):
        total = g_ref[0:m, :]
        for d in range(1, N_DEV):
            total = total + g_ref[d * m:(d + 1) * m, :]
        o_ref[...] = total

    return pl.pallas_call(
        body, name="small_grads_sum", out_shape=jax.ShapeDtypeStruct((m, LANES), F32),
        in_specs=[pl.BlockSpec(memory_space=pltpu.VMEM)], out_specs=pl.BlockSpec(memory_space=pltpu.VMEM),
    )(gathered)


ADAMW_BLOCK_ELEMS = 1 << 18


def _adamw(w, g, m, v, name):
    l, r, cols = w.shape
    tr = _tile(r, max(8, ADAMW_BLOCK_ELEMS // cols // 8 * 8), 8)

    def body(w_ref, g_ref, m_ref, v_ref, d_ref, nm_ref, nv_ref):
        gv = g_ref[...]
        nm = ADAM_B1 * m_ref[...] + (1.0 - ADAM_B1) * gv
        nv = ADAM_B2 * v_ref[...] + (1.0 - ADAM_B2) * jnp.square(gv)
        m_hat = nm / (1.0 - ADAM_B1 ** ADAM_STEP)
        v_hat = nv / (1.0 - ADAM_B2 ** ADAM_STEP)
        d_ref[...] = -ADAM_LR * (m_hat / (jnp.sqrt(v_hat) + ADAM_EPS) + ADAM_WD * w_ref[...])
        nm_ref[...] = nm
        nv_ref[...] = nv

    blk = pl.BlockSpec((None, tr, cols), lambda a, i: (a, i, 0))
    return pl.pallas_call(
        body, name=name, grid=(l, r // tr), in_specs=[blk] * 4, out_specs=[blk] * 3,
        out_shape=[jax.ShapeDtypeStruct(w.shape, F32)] * 3, compiler_params=_params("parallel", "parallel"),
    )(w, g, m, v)


WEIGHTS = ("mixer_norm", "ffn_norm", "attn_w_qkv", "attn_q_norm", "attn_k_norm", "attn_sinks", "attn_w_o",
           "ssm_w_in", "ssm_conv_w", "ssm_conv_b", "ssm_dt_bias", "ssm_a_log", "ssm_d", "ssm_norm", "ssm_w_out",
           "ffn_w_gate", "ffn_w_up", "ffn_w_down")
BIG = ("attn_w_qkv", "attn_w_o", "ffn_w_gate", "ffn_w_up", "ffn_w_down", "ssm_w_in", "ssm_w_out")
MATRICES = (("attn_w_qkv", 0), ("attn_w_o", 0), ("ffn_w_gate", 0), ("ffn_w_up", 0), ("ffn_w_down", 0),
            ("ssm_w_in", 0), ("ssm_w_out", 0), ("ffn_w_gate", 1), ("ffn_w_up", 1), ("ffn_w_down", 1))
MATRIX_LAYERS = tuple((BIG.index(n), l) for n, l in MATRICES)
SMALL_SHARDED = ("ssm_conv_w", "ssm_conv_b", "ssm_norm")
SMALL = tuple(n for n in WEIGHTS if n not in BIG)


def _pack_rows(parts, row_unit=8):
    flat = jnp.concatenate([p.reshape(-1) for p in parts])
    pad = (-flat.shape[0]) % (LANES * row_unit)
    return jnp.pad(flat, (0, pad)).reshape(-1, LANES)


def _unpack(flat, shapes):
    out, off = [], 0
    for shp in shapes:
        size = math.prod(shp)
        out.append(flat[off:off + size].reshape(shp))
        off += size
    return out


def kernel(x, positions, mixer_norm, ffn_norm, attn_w_qkv, attn_q_norm, attn_k_norm, attn_sinks, attn_w_o, ssm_w_in, ssm_conv_w, ssm_conv_b, ssm_dt_bias, ssm_a_log, ssm_d, ssm_norm, ssm_w_out, ffn_w_gate, ffn_w_up, ffn_w_down, loss_target, m_mixer_norm, m_ffn_norm, m_attn_w_qkv, m_attn_q_norm, m_attn_k_norm, m_attn_sinks, m_attn_w_o, m_ssm_w_in, m_ssm_conv_w, m_ssm_conv_b, m_ssm_dt_bias, m_ssm_a_log, m_ssm_d, m_ssm_norm, m_ssm_w_out, m_ffn_w_gate, m_ffn_w_up, m_ffn_w_down, v_mixer_norm, v_ffn_norm, v_attn_w_qkv, v_attn_q_norm, v_attn_k_norm, v_attn_sinks, v_attn_w_o, v_ssm_w_in, v_ssm_conv_w, v_ssm_conv_b, v_ssm_dt_bias, v_ssm_a_log, v_ssm_d, v_ssm_norm, v_ssm_w_out, v_ffn_w_gate, v_ffn_w_up, v_ffn_w_down):
    args = locals()
    w = {n: args[n] for n in WEIGHTS}
    m = {n: args["m_" + n] for n in WEIGHTS}
    v = {n: args["v_" + n] for n in WEIGHTS}
    ax, ay, ac = lax.axis_index("x"), lax.axis_index("y"), lax.axis_index("c")
    shard = 2 * ax + ay

    stacks = dict(zip(MATRICES, _gather_shards([w[n].astype(BF16) for n in BIG], MATRIX_LAYERS)))
    rows_merged = lambda st: st.reshape(st.shape[0] * st.shape[1], st.shape[2])
    w_in = stacks[("ssm_w_in", 0)]
    w_in = jnp.concatenate([w_in[i] for i in range(N_SHARDS)]
                           + [jnp.zeros((w_in.shape[1], SSM_IN_PAD - SSM_IN), BF16)], axis=1)
    small_shapes = [w[n].shape for n in SMALL_SHARDED]
    small_all = _all_gather8(_pack_rows([w[n] for n in SMALL_SHARDED]), "gather_small_params")
    small_all = small_all.reshape(N_DEV, -1)[::2]
    full, off = {}, 0
    for n, shp in zip(SMALL_SHARDED, small_shapes):
        size = math.prod(shp)
        seg = small_all[:, off:off + size].reshape((N_SHARDS,) + shp)
        full[n] = jnp.moveaxis(seg, 0, -2).reshape(shp[:-1] + (N_SHARDS * shp[-1],))
        off += size
    wl = {
        "mixer_norm": mixer_norm, "ffn_norm": ffn_norm, "attn_w_qkv": stacks[("attn_w_qkv", 0)],
        "attn_q_norm": attn_q_norm[0], "attn_k_norm": attn_k_norm[0], "attn_sinks": attn_sinks[0],
        "attn_w_o": rows_merged(stacks[("attn_w_o", 0)]), "ssm_w_in": w_in,
        "ssm_conv_w": full["ssm_conv_w"][0], "ssm_conv_b": full["ssm_conv_b"][0],
        "ssm_dt_bias": ssm_dt_bias[0], "ssm_a_log": ssm_a_log[0], "ssm_d": ssm_d[0],
        "ssm_norm": full["ssm_norm"][0], "ssm_w_out": rows_merged(stacks[("ssm_w_out", 0)]),
        "ffn_w_gate": [stacks[("ffn_w_gate", l)] for l in range(2)],
        "ffn_w_up": [stacks[("ffn_w_up", l)] for l in range(2)],
        "ffn_w_down": [rows_merged(stacks[("ffn_w_down", l)]) for l in range(2)],
    }

    loss_part, dx, g_full = _local_step(x[0], positions[0], loss_target[0], wl)

    g_stacks = [g_full[n][l] for n, l in MATRICES]
    c_idx, s_idx = ac.reshape(1).astype(jnp.int32), shard.reshape(1).astype(jnp.int32)
    recv = _swap_other_half(g_stacks)
    partials = [_add_pair(g, r, c_idx, f"grads_add_pair_{e}") for e, (g, r) in enumerate(zip(g_stacks, recv))]
    recv = _send_to_owners(partials)
    totals = [_add_owned(p, r, s_idx, f"grads_add_owned_{e}") for e, (p, r) in enumerate(zip(partials, recv))]
    grads = dict(zip(BIG, _share_halves(totals, MATRIX_LAYERS, [w[n].shape for n in BIG])))

    small_full_shapes = [g_full[n].shape for n in SMALL] + [(1,)]
    small_g = _pack_rows([g_full[n] for n in SMALL] + [loss_part.reshape(1)])
    small_sum = _sum8(_all_gather8(small_g, "gather_small_grads")).reshape(-1)
    *small_list, loss = _unpack(small_sum, small_full_shapes)
    for n, g in zip(SMALL, small_list):
        if n in SMALL_SHARDED:
            width = w[n].shape[-1]
            g = lax.dynamic_slice_in_dim(g, shard * width, width, axis=g.ndim - 1)
        grads[n] = g.reshape(w[n].shape)

    delta, new_m, new_v = {}, {}, {}
    for n in BIG:
        delta[n], new_m[n], new_v[n] = _adamw(w[n], grads[n], m[n], v[n], "adamw_" + n)
    small_local = [w[n].shape for n in SMALL]
    pk = lambda t: _pack_rows([t[n] for n in SMALL])[None]
    outs = _adamw(pk(w), pk(grads), pk(m), pk(v), "adamw_small")
    for res, o in zip((delta, new_m, new_v), outs):
        for n, a in zip(SMALL, _unpack(o.reshape(-1), small_local)):
            res[n] = a

    return (loss.reshape(()), dx[None], *[grads[n] for n in WEIGHTS], *[delta[n] for n in WEIGHTS],
            *[new_m[n] for n in WEIGHTS], *[new_v[n] for n in WEIGHTS])
```

```python
import math

import jax
import jax.numpy as jnp
from jax import lax
from jax.experimental import pallas as pl
from jax.experimental.pallas import tpu as pltpu

F32 = jnp.float32
BF16 = jnp.bfloat16

D_MODEL = 2048
EPS = 1e-6
ATT_HEAD_DIM = 64
ATT_Q_HEADS = 32
ATT_KV_HEADS = 4
ATT_GROUP = 8
ATT_BLOCK = 128
ROPE_THETA = 10000.0
Q_WIDTH = ATT_Q_HEADS * ATT_HEAD_DIM
KV_WIDTH = ATT_KV_HEADS * ATT_HEAD_DIM
SSM_D_INNER = 4096
SSM_HEADS = 64
SSM_GROUPS = 8
SSM_HPG = 8
SSM_P = 64
SSM_STATE = 128
SSM_CONV = 4
SSM_CHUNK = 256
SSM_CONV_DIM = 6144
SSM_GN = SSM_D_INNER // SSM_GROUPS
SSM_IN = SSM_D_INNER + SSM_CONV_DIM + SSM_HEADS
LANES = 128
SSM_IN_PAD = -(-SSM_IN // LANES) * LANES
N_SHARDS = 4
N_DEV = 8

ADAM_LR = 0.001
ADAM_B1 = 0.9
ADAM_B2 = 0.999
ADAM_EPS = 1e-08
ADAM_WD = 0.01
ADAM_STEP = 10

VMEM_LIMIT = 56 * 1024 * 1024
MESH = pl.DeviceIdType.MESH
ANY = pl.BlockSpec(memory_space=pl.ANY)


def _params(*sem):
    return pltpu.CompilerParams(dimension_semantics=sem, vmem_limit_bytes=VMEM_LIMIT)


def _tile(dim, target, unit=LANES):
    if dim <= target:
        return dim
    t = (target // unit) * unit
    while t >= unit:
        if dim % t == 0:
            return t
        t -= unit
    return dim


def _dot(a, b):
    return lax.dot_general(a, b, (((1,), (0,)), ((), ())), preferred_element_type=F32)


def _dot_nt(a, b):
    return lax.dot_general(a, b, (((1,), (1,)), ((), ())), preferred_element_type=F32)


def _dot_tn(a, b):
    return lax.dot_general(a, b, (((0,), (0,)), ((), ())), preferred_element_type=F32)


def _split3(x):
    hi = x.astype(BF16)
    r1 = x - hi.astype(F32)
    mid = r1.astype(BF16)
    lo = (r1 - mid.astype(F32)).astype(BF16)
    return hi, mid, lo


def _dot_x(x, m):
    hi, mid, lo = _split3(x)
    return _dot(hi, m) + _dot(mid, m) + _dot(lo, m)


def _xdot(m, x):
    hi, mid, lo = _split3(x)
    return _dot(m, hi) + _dot(m, mid) + _dot(m, lo)


def _dot_x_nt(x, m):
    hi, mid, lo = _split3(x)
    return _dot_nt(hi, m) + _dot_nt(mid, m) + _dot_nt(lo, m)


def _iota(shape, dim):
    return lax.broadcasted_iota(jnp.int32, shape, dim)


def _sigmoid(x):
    return 1.0 / (1.0 + jnp.exp(-x))


def _softplus(x):
    return jnp.maximum(x, 0.0) + jnp.log(1.0 + jnp.exp(-jnp.abs(x)))


MM_ROWS = 1024
MM_TILE = 1408
FUSED_ROWS = 512


def _mm(a, b, mode, name, add=None, out_dtype=F32, b_cols=False, out_cols=False, fuse=None, rows=MM_ROWS):
    bs = b.shape[-2:]
    if b_cols:
        bs = (bs[0], N_SHARDS * bs[1])
    if mode == "nn":
        (m, k), (k2, n) = a.shape, bs
    elif mode == "nt":
        (m, k), (n, k2) = a.shape, bs
    else:
        (k, m), (k2, n) = a.shape, bs
    assert k == k2, (a.shape, b.shape, mode)
    split_n = (b_cols and mode == "nn") or out_cols
    split_k = b_cols and mode == "nt"
    tm = _tile(m, MM_TILE if mode == "tn" else rows)
    tn = _tile(n // N_SHARDS if split_n else n, MM_TILE)
    tk = _tile(k // N_SHARDS if split_k else k, MM_ROWS if mode == "tn" else MM_TILE)
    nk = k // tk
    nj, nq = (n // N_SHARDS) // tn, (k // N_SHARDS) // tk
    if mode == "tn":
        a_spec = pl.BlockSpec((tk, tm), lambda i, j, q: (q, i))
    else:
        a_spec = pl.BlockSpec((tm, tk), lambda i, j, q: (i, q))
    if mode == "nt":
        if b_cols:
            b_spec = pl.BlockSpec((None, tn, tk), lambda i, j, q: (q // nq, j, q % nq))
        else:
            b_spec = pl.BlockSpec((tn, tk), lambda i, j, q: (j, q))
    elif b_cols:
        b_spec = pl.BlockSpec((None, tk, tn), lambda i, j, q: (j // nj, q, j % nj))
    else:
        b_spec = pl.BlockSpec((tk, tn), lambda i, j, q: (q, j))
    add_spec = pl.BlockSpec((tm, tn), lambda i, j, q: (i, j))
    if out_cols:
        o_spec = pl.BlockSpec((None, tm, tn), lambda i, j, q: (j // nj, i, j % nj))
        o_shape = (N_SHARDS, m, n // N_SHARDS)
    else:
        o_spec, o_shape = add_spec, (m, n)
    dot = {"nn": _dot, "nt": _dot_nt, "tn": _dot_tn}[mode]
    has_add = add is not None
    fuse_fn, extra, out_dtypes = fuse if fuse is not None else (None, [], [out_dtype])
    n_in, n_out = 2 + has_add + len(extra), len(out_dtypes)

    def body(*refs):
        a_ref, b_ref = refs[:2]
        add_ref = refs[2] if has_add else None
        extra_refs = refs[2 + has_add:n_in]
        o_refs, acc_ref = refs[n_in:n_in + n_out], refs[n_in + n_out]
        part = dot(a_ref[...].astype(BF16), b_ref[...].astype(BF16))

        def finish(total):
            if has_add:
                total = total + add_ref[...].astype(F32)
            outs = (total,) if fuse_fn is None else fuse_fn(total, *[r[...] for r in extra_refs])
            for o_ref, val in zip(o_refs, outs):
                o_ref[...] = val.astype(o_ref.dtype)

        if nk == 1:
            finish(part)
        else:
            q = pl.program_id(2)

            @pl.when(q == 0)
            def _():
                acc_ref[...] = part

            @pl.when(jnp.logical_and(q > 0, q < nk - 1))
            def _():
                acc_ref[...] += part

            @pl.when(q == nk - 1)
            def _():
                finish(acc_ref[...] + part)

    in_specs = [a_spec, b_spec] + [add_spec] * (has_add + len(extra))
    args = (a, b) + ((add,) if has_add else ()) + tuple(extra)
    res = pl.pallas_call(
        body, name=name, grid=(m // tm, n // tn, nk),
        in_specs=in_specs, out_specs=[o_spec] * n_out,
        out_shape=[jax.ShapeDtypeStruct(o_shape, dt) for dt in out_dtypes],
        scratch_shapes=[pltpu.VMEM((tm, tn) if nk > 1 else (8, LANES), F32)],
        compiler_params=_params("parallel", "parallel", "arbitrary"),
    )(*args)
    return res[0] if fuse is None else res


def _rms_fwd(x, g, name):
    s, d = x.shape
    ts = _tile(s, 512, 8)

    def body(x_ref, g_ref, o_ref):
        xv = x_ref[...]
        r = lax.rsqrt(jnp.mean(xv * xv, axis=-1, keepdims=True) + EPS)
        o_ref[...] = (xv * r * g_ref[...]).astype(BF16)

    return pl.pallas_call(
        body, name=name, grid=(s // ts,),
        in_specs=[pl.BlockSpec((ts, d), lambda i: (i, 0)), pl.BlockSpec((1, d), lambda i: (0, 0))],
        out_specs=pl.BlockSpec((ts, d), lambda i: (i, 0)),
        out_shape=jax.ShapeDtypeStruct((s, d), BF16),
        compiler_params=_params("parallel"),
    )(x, g)


def _rms_bwd(x, g, dh, dres, name):
    s, d = x.shape
    ts = _tile(s, 512, 8)

    def body(x_ref, g_ref, dh_ref, dres_ref, dx_ref, dg_ref):
        xv = x_ref[...]
        r = lax.rsqrt(jnp.mean(xv * xv, axis=-1, keepdims=True) + EPS)
        xhat = xv * r
        dhv = dh_ref[...].astype(F32)
        part = jnp.sum(dhv * xhat, axis=0, keepdims=True)

        @pl.when(pl.program_id(0) == 0)
        def _():
            dg_ref[...] = part

        @pl.when(pl.program_id(0) > 0)
        def _():
            dg_ref[...] += part

        dxh = dhv * g_ref[...]
        dx = r * (dxh - xhat * jnp.mean(dxh * xhat, axis=-1, keepdims=True))
        dx_ref[...] = dres_ref[...] + dx

    row = pl.BlockSpec((ts, d), lambda i: (i, 0))
    vec = pl.BlockSpec((1, d), lambda i: (0, 0))
    return pl.pallas_call(
        body, name=name, grid=(s // ts,),
        in_specs=[row, vec, row, row], out_specs=[row, vec],
        out_shape=[jax.ShapeDtypeStruct((s, d), F32), jax.ShapeDtypeStruct((1, d), F32)],
        compiler_params=_params("arbitrary"),
    )(x, g, dh, dres)


def _act_fwd(g, u, name):
    s, f = g.shape
    ts, tf = _tile(s, 512, 8), _tile(f, 1408)

    def body(g_ref, u_ref, o_ref):
        gv = g_ref[...]
        o_ref[...] = (gv * _sigmoid(gv) * u_ref[...]).astype(BF16)

    blk = pl.BlockSpec((ts, tf), lambda i, j: (i, j))
    return pl.pallas_call(
        body, name=name, grid=(s // ts, f // tf), in_specs=[blk, blk], out_specs=blk,
        out_shape=jax.ShapeDtypeStruct((s, f), BF16), compiler_params=_params("parallel", "parallel"),
    )(g, u)


def _act_bwd(g, u, da, name):
    s, f = g.shape
    ts, tf = _tile(s, 512, 8), _tile(f, 1408)

    def body(g_ref, u_ref, da_ref, dg_ref, du_ref):
        gv, uv, dav = g_ref[...], u_ref[...], da_ref[...].astype(F32)
        sg = _sigmoid(gv)
        silu = gv * sg
        du_ref[...] = (dav * silu).astype(BF16)
        dg_ref[...] = (dav * uv * sg * (1.0 + gv * (1.0 - sg))).astype(BF16)

    blk = pl.BlockSpec((ts, tf), lambda i, j: (i, j))
    return pl.pallas_call(
        body, name=name, grid=(s // ts, f // tf), in_specs=[blk, blk, blk], out_specs=[blk, blk],
        out_shape=[jax.ShapeDtypeStruct((s, f), BF16)] * 2, compiler_params=_params("parallel", "parallel"),
    )(g, u, da)


def _loss_fwd_bwd(y, target):
    s, d = y.shape
    ts = _tile(s, 512, 8)

    def body(y_ref, t_ref, l_ref, dy_ref):
        diff = y_ref[...] - t_ref[...]
        dy_ref[...] = diff * (1.0 / d)
        part = jnp.full((1, LANES), 0.5 * jnp.sum(jnp.mean(diff * diff, axis=-1, keepdims=True)), F32)

        @pl.when(pl.program_id(0) == 0)
        def _():
            l_ref[...] = part

        @pl.when(pl.program_id(0) > 0)
        def _():
            l_ref[...] += part

    row = pl.BlockSpec((ts, d), lambda i: (i, 0))
    acc = pl.BlockSpec((1, LANES), lambda i: (0, 0))
    return pl.pallas_call(
        body, name="loss", grid=(s // ts,), in_specs=[row, row], out_specs=[acc, row],
        out_shape=[jax.ShapeDtypeStruct((1, LANES), F32), jax.ShapeDtypeStruct((s, d), F32)],
        compiler_params=_params("arbitrary"),
    )(y, target)


def _lane_consts():
    r, c = _iota((LANES, LANES), 0), _iota((LANES, LANES), 1)
    same = (r >> 6) == (c >> 6)
    rin, cin = r & 63, c & 63
    one = lambda cond: jnp.where(cond, 1.0, 0.0).astype(BF16)
    return dict(
        seg=one(same),
        rot=(jnp.where(same & (rin == cin + 32), -1.0, 0.0)
             + jnp.where(same & (cin == rin + 32), 1.0, 0.0)).astype(BF16),
        dup_lo=one(r == cin), dup_hi=one(r == cin + 64),
        up=one((c >= 64) & (r == c - 64)), down=one((c < 64) & (r == c + 64)),
        fold_lo=one((c < 64) & (rin == c)), fold_hi=one((c >= 64) & (rin == c - 64)),
    )


def _norm_rope(xc, gain, cos, sin, k):
    ss = _dot_x(xc * xc, k["seg"])
    rinv = lax.rsqrt(ss * (1.0 / ATT_HEAD_DIM) + EPS)
    xhat = xc * rinv
    y = xhat * gain
    return y * cos + _dot_x(y, k["rot"]) * sin, xhat, rinv


def _norm_rope_bwd(dr, xhat, rinv, gain, cos, sin, k):
    dy = dr * cos - _dot_x(dr * sin, k["rot"])
    dgain = jnp.sum(dy * xhat, axis=0, keepdims=True)
    dxh = dy * gain
    dx = rinv * (dxh - xhat * (_dot_x(dxh * xhat, k["seg"]) * (1.0 / ATT_HEAD_DIM)))
    return dx, dgain


def _attn_prep(qkv, cos, sin, gq, gk):
    s = qkv.shape[0]
    tr = _tile(s, 256, 8)

    def body(x_ref, cos_ref, sin_ref, gq_ref, gk_ref, q_ref, kk_ref, vlo_ref, vhi_ref):
        k = _lane_consts()
        cosv, sinv = cos_ref[...], sin_ref[...]
        lane = _iota((tr, LANES), 1)
        for j in range(Q_WIDTH // LANES):
            r, _, _ = _norm_rope(x_ref[:, j * LANES:(j + 1) * LANES], gq_ref[...], cosv, sinv, k)
            q_ref[:, j * LANES:(j + 1) * LANES] = r.astype(BF16)
        for i in range(KV_WIDTH // LANES):
            off = Q_WIDTH + i * LANES
            r, _, _ = _norm_rope(x_ref[:, off:off + LANES], gk_ref[...], cosv, sinv, k)
            rb = r.astype(BF16)
            kk_ref[:, (2 * i) * LANES:(2 * i + 1) * LANES] = _dot(rb, k["dup_lo"]).astype(BF16)
            kk_ref[:, (2 * i + 1) * LANES:(2 * i + 2) * LANES] = _dot(rb, k["dup_hi"]).astype(BF16)
            off = Q_WIDTH + KV_WIDTH + i * LANES
            vb = x_ref[:, off:off + LANES].astype(BF16)
            zero = jnp.zeros_like(vb)
            vlo_ref[:, (2 * i) * LANES:(2 * i + 1) * LANES] = jnp.where(lane < 64, vb, zero)
            vhi_ref[:, (2 * i) * LANES:(2 * i + 1) * LANES] = _dot(vb, k["up"]).astype(BF16)
            vlo_ref[:, (2 * i + 1) * LANES:(2 * i + 2) * LANES] = _dot(vb, k["down"]).astype(BF16)
            vhi_ref[:, (2 * i + 1) * LANES:(2 * i + 2) * LANES] = jnp.where(lane >= 64, vb, zero)

    w = qkv.shape[1]
    row = lambda width: pl.BlockSpec((tr, width), lambda i: (i, 0))
    vec = pl.BlockSpec((1, LANES), lambda i: (0, 0))
    kw = ATT_KV_HEADS * LANES
    return pl.pallas_call(
        body, name="attn_prep", grid=(s // tr,),
        in_specs=[row(w), row(LANES), row(LANES), vec, vec],
        out_specs=[row(Q_WIDTH), row(kw), row(kw), row(kw)],
        out_shape=[jax.ShapeDtypeStruct((s, Q_WIDTH), BF16)] + [jax.ShapeDtypeStruct((s, kw), BF16)] * 3,
        compiler_params=_params("parallel"),
    )(qkv, cos, sin, gq, gk)


def _band_mask(n):
    qi = _iota((ATT_BLOCK, 2 * ATT_BLOCK), 0)
    kj = _iota((ATT_BLOCK, 2 * ATT_BLOCK), 1)
    band = (kj > qi) & (kj <= qi + ATT_BLOCK)
    return band & ((kj >= ATT_BLOCK) | (n > 0))


def _softmax_sink(s, valid, sink):
    s = jnp.where(valid, s, -jnp.inf)
    m = jnp.maximum(jnp.max(s, axis=-1, keepdims=True), sink)
    p = jnp.exp(s - m)
    esink = jnp.exp(sink - m)
    inv = 1.0 / (jnp.sum(p, axis=-1, keepdims=True) + esink)
    return p * inv, esink * inv


def _attn_specs(order):
    if order == "nh":
        cur = lambda n, h: (n, h)
        prev = lambda n, h: (jnp.maximum(n - 1, 0), h)
    else:
        cur = lambda h, n: (n, h)
        prev = lambda h, n: (jnp.maximum(n - 1, 0), h)
    qs = pl.BlockSpec((ATT_BLOCK, ATT_GROUP * ATT_HEAD_DIM), cur)
    kc = pl.BlockSpec((ATT_BLOCK, LANES), cur)
    kp = pl.BlockSpec((ATT_BLOCK, LANES), prev)
    return qs, kc, kp


def _attn_fwd(q, kk, vlo, vhi, sinks):
    s = q.shape[0]
    nb = s // ATT_BLOCK
    scale = ATT_HEAD_DIM ** -0.5

    def body(sink_ref, q_ref, kc_ref, kp_ref, vloc_ref, vlop_ref, vhic_ref, vhip_ref, o_ref):
        n, h = pl.program_id(0), pl.program_id(1)
        valid = _band_mask(n)
        kw = jnp.concatenate([kp_ref[...], kc_ref[...]], axis=0)
        vw = (jnp.concatenate([vlop_ref[...], vloc_ref[...]], axis=0),
              jnp.concatenate([vhip_ref[...], vhic_ref[...]], axis=0))
        lane = _iota((ATT_BLOCK, LANES), 1)
        for jp in range(ATT_GROUP // 2):
            qp = q_ref[:, jp * LANES:(jp + 1) * LANES]
            acc = jnp.zeros((ATT_BLOCK, LANES), F32)
            for hf in range(2):
                qm = jnp.where((lane >= 64) == (hf == 1), qp, jnp.zeros_like(qp))
                sc = _dot_nt(qm, kw) * scale
                probs, _ = _softmax_sink(sc, valid, sink_ref[h * ATT_GROUP + 2 * jp + hf])
                acc = acc + _dot(probs.astype(BF16), vw[hf])
            o_ref[:, jp * LANES:(jp + 1) * LANES] = acc.astype(BF16)

    qs, kc, kp = _attn_specs("nh")
    return pl.pallas_call(
        body, name="attn_fwd", grid=(nb, ATT_KV_HEADS),
        in_specs=[pl.BlockSpec(memory_space=pltpu.SMEM), qs, kc, kp, kc, kp, kc, kp],
        out_specs=qs, out_shape=jax.ShapeDtypeStruct((s, Q_WIDTH), BF16),
        compiler_params=_params("parallel", "parallel"),
    )(sinks, q, kk, kk, vlo, vlo, vhi, vhi)


def _attn_bwd(q, kk, vlo, vhi, sinks, do):
    s = q.shape[0]
    nb = s // ATT_BLOCK
    scale = ATT_HEAD_DIM ** -0.5

    def body(sink_ref, q_ref, kc_ref, kp_ref, vloc_ref, vlop_ref, vhic_ref, vhip_ref, do_ref,
             dq_ref, dkc_ref, dkp_ref, dvloc_ref, dvlop_ref, dvhic_ref, dvhip_ref, dsink_ref):
        h, n = pl.program_id(0), pl.program_id(1)
        valid = _band_mask(n)
        kw = jnp.concatenate([kp_ref[...], kc_ref[...]], axis=0)
        vw = (jnp.concatenate([vlop_ref[...], vloc_ref[...]], axis=0),
              jnp.concatenate([vhip_ref[...], vhic_ref[...]], axis=0))
        lane = _iota((ATT_BLOCK, LANES), 1)
        sub = _iota((ATT_GROUP, LANES), 0)
        dkk = jnp.zeros((2 * ATT_BLOCK, LANES), F32)
        dv = [jnp.zeros((2 * ATT_BLOCK, LANES), F32), jnp.zeros((2 * ATT_BLOCK, LANES), F32)]
        dsink = jnp.zeros((ATT_GROUP, LANES), F32)
        for jp in range(ATT_GROUP // 2):
            qp = q_ref[:, jp * LANES:(jp + 1) * LANES]
            dop = do_ref[:, jp * LANES:(jp + 1) * LANES]
            dq = jnp.zeros((ATT_BLOCK, LANES), F32)
            for hf in range(2):
                mine = (lane >= 64) == (hf == 1)
                qm = jnp.where(mine, qp, jnp.zeros_like(qp))
                sc = _dot_nt(qm, kw) * scale
                probs, psink = _softmax_sink(sc, valid, sink_ref[h * ATT_GROUP + 2 * jp + hf])
                pb = probs.astype(BF16)
                dprobs = _dot_nt(dop, vw[hf])
                dv[hf] = dv[hf] + _dot_tn(pb, dop)
                delta = jnp.sum(probs * dprobs, axis=-1, keepdims=True)
                ds = (probs * (dprobs - delta) * scale).astype(BF16)
                dsink = dsink + jnp.where(sub == 2 * jp + hf, -jnp.sum(psink * delta), 0.0)
                dq = dq + jnp.where(mine, _dot(ds, kw), 0.0)
                dkk = dkk + _dot_tn(ds, qm)
            dq_ref[:, jp * LANES:(jp + 1) * LANES] = dq
        dkp_ref[...], dkc_ref[...] = dkk[:ATT_BLOCK], dkk[ATT_BLOCK:]
        dvlop_ref[...], dvloc_ref[...] = dv[0][:ATT_BLOCK], dv[0][ATT_BLOCK:]
        dvhip_ref[...], dvhic_ref[...] = dv[1][:ATT_BLOCK], dv[1][ATT_BLOCK:]

        @pl.when(n == 0)
        def _():
            dsink_ref[0] = dsink

        @pl.when(n > 0)
        def _():
            dsink_ref[0] += dsink

    qs, kc, kp = _attn_specs("hn")
    kw_shape = jax.ShapeDtypeStruct((s, ATT_KV_HEADS * LANES), F32)
    return pl.pallas_call(
        body, name="attn_bwd", grid=(ATT_KV_HEADS, nb),
        in_specs=[pl.BlockSpec(memory_space=pltpu.SMEM), qs, kc, kp, kc, kp, kc, kp, qs],
        out_specs=[qs] + [kc] * 6 + [pl.BlockSpec((1, ATT_GROUP, LANES), lambda h, n: (h, 0, 0))],
        out_shape=[jax.ShapeDtypeStruct((s, Q_WIDTH), F32)] + [kw_shape] * 6
        + [jax.ShapeDtypeStruct((ATT_KV_HEADS, ATT_GROUP, LANES), F32)],
        compiler_params=_params("parallel", "arbitrary"),
    )(sinks, q, kk, kk, vlo, vlo, vhi, vhi, do)


def _attn_prep_bwd(qkv, cos, sin, gq, gk, dq, dks, dvlos, dvhis):
    s, w = qkv.shape
    tr = ATT_BLOCK
    nb = s // tr

    def body(x_ref, cos_ref, sin_ref, gq_ref, gk_ref, dq_ref, dkc_ref, dkn_ref, dvloc_ref, dvlon_ref,
             dvhic_ref, dvhin_ref, dx_ref, dgq_ref, dgk_ref):
        n = pl.program_id(0)
        k = _lane_consts()
        cosv, sinv = cos_ref[...], sin_ref[...]
        nxt = jnp.where(n < nb - 1, 1.0, 0.0)
        lane = _iota((tr, LANES), 1)
        dgq = jnp.zeros((1, LANES), F32)
        dgk = jnp.zeros((1, LANES), F32)
        for j in range(Q_WIDTH // LANES):
            sl = slice(j * LANES, (j + 1) * LANES)
            _, xhat, rinv = _norm_rope(x_ref[:, sl], gq_ref[...], cosv, sinv, k)
            dx, dg = _norm_rope_bwd(dq_ref[:, sl], xhat, rinv, gq_ref[...], cosv, sinv, k)
            dx_ref[:, sl] = dx.astype(BF16)
            dgq = dgq + dg
        for i in range(KV_WIDTH // LANES):
            a, b = slice(2 * i * LANES, (2 * i + 1) * LANES), slice((2 * i + 1) * LANES, (2 * i + 2) * LANES)
            dr = (_dot_x(dkc_ref[:, a] + nxt * dkn_ref[:, a], k["fold_lo"])
                  + _dot_x(dkc_ref[:, b] + nxt * dkn_ref[:, b], k["fold_hi"]))
            sl = slice(Q_WIDTH + i * LANES, Q_WIDTH + (i + 1) * LANES)
            _, xhat, rinv = _norm_rope(x_ref[:, sl], gk_ref[...], cosv, sinv, k)
            dx, dg = _norm_rope_bwd(dr, xhat, rinv, gk_ref[...], cosv, sinv, k)
            dx_ref[:, sl] = dx.astype(BF16)
            dgk = dgk + dg
            ta = jnp.where(lane < 64, dvloc_ref[:, a] + nxt * dvlon_ref[:, a], dvhic_ref[:, a] + nxt * dvhin_ref[:, a])
            tb = jnp.where(lane < 64, dvloc_ref[:, b] + nxt * dvlon_ref[:, b], dvhic_ref[:, b] + nxt * dvhin_ref[:, b])
            sl = slice(Q_WIDTH + KV_WIDTH + i * LANES, Q_WIDTH + KV_WIDTH + (i + 1) * LANES)
            dx_ref[:, sl] = (_dot_x(ta, k["fold_lo"]) + _dot_x(tb, k["fold_hi"])).astype(BF16)

        @pl.when(n == 0)
        def _():
            dgq_ref[...] = dgq
            dgk_ref[...] = dgk

        @pl.when(n > 0)
        def _():
            dgq_ref[...] += dgq
            dgk_ref[...] += dgk

    row = lambda width: pl.BlockSpec((tr, width), lambda i: (i, 0))
    nxt_row = pl.BlockSpec((tr, ATT_KV_HEADS * LANES), lambda i: (jnp.minimum(i + 1, nb - 1), 0))
    vec = pl.BlockSpec((1, LANES), lambda i: (0, 0))
    kw = ATT_KV_HEADS * LANES
    return pl.pallas_call(
        body, name="attn_prep_bwd", grid=(nb,),
        in_specs=[row(w), row(LANES), row(LANES), vec, vec, row(Q_WIDTH),
                  row(kw), nxt_row, row(kw), nxt_row, row(kw), nxt_row],
        out_specs=[row(w), vec, vec],
        out_shape=[jax.ShapeDtypeStruct((s, w), BF16), jax.ShapeDtypeStruct((1, LANES), F32),
                   jax.ShapeDtypeStruct((1, LANES), F32)],
        compiler_params=_params("arbitrary"),
    )(qkv, cos, sin, gq, gk, dq, dks[0], dks[1], dvlos[0], dvlos[1], dvhis[0], dvhis[1])


CONV_HALO = 8
CONV_TC = 512
XBC_OFF = SSM_D_INNER // CONV_TC
DT_OFF = SSM_D_INNER + SSM_CONV_DIM


def _conv_pre(ext, w_ref, b_ref, ts):
    pre = b_ref[...] + w_ref[SSM_CONV - 1:SSM_CONV, :] * ext[CONV_HALO:]
    for kk in range(SSM_CONV - 1):
        pre = pre + w_ref[kk:kk + 1, :] * pltpu.roll(ext, SSM_CONV - 1 - kk, 0)[CONV_HALO:]
    return pre


def _conv_specs(ts):
    tc = CONV_TC
    src = pl.BlockSpec((ts, tc), lambda j, i: (i, XBC_OFF + j))
    halo = pl.BlockSpec((CONV_HALO, tc), lambda j, i: (jnp.maximum(i * (ts // CONV_HALO) - 1, 0), XBC_OFF + j))
    blk = pl.BlockSpec((ts, tc), lambda j, i: (i, j))
    wspec = pl.BlockSpec((SSM_CONV, tc), lambda j, i: (0, j))
    bspec = pl.BlockSpec((1, tc), lambda j, i: (0, j))
    return src, halo, blk, wspec, bspec


def _conv_fwd(zx, w, b):
    s, c = zx.shape[0], SSM_CONV_DIM
    ts = _tile(s, 512, 8)

    def body(u_ref, halo_ref, w_ref, b_ref, o_ref):
        halo = jnp.where(pl.program_id(1) > 0, halo_ref[...], 0.0)
        pre = _conv_pre(jnp.concatenate([halo, u_ref[...]], axis=0), w_ref, b_ref, ts)
        o_ref[...] = pre * _sigmoid(pre)

    src, halo, blk, wspec, bspec = _conv_specs(ts)
    return pl.pallas_call(
        body, name="conv_fwd", grid=(c // CONV_TC, s // ts),
        in_specs=[src, halo, wspec, bspec], out_specs=blk, out_shape=jax.ShapeDtypeStruct((s, c), F32),
        compiler_params=_params("parallel", "parallel"),
    )(zx, zx, w, b)


def _conv_bwd_pre(zx, w, b, dact):
    s, c = zx.shape[0], SSM_CONV_DIM
    ts = _tile(s, 512, 8)

    def body(u_ref, halo_ref, w_ref, b_ref, da_ref, dpre_ref, dw_ref, db_ref):
        i = pl.program_id(1)
        halo = jnp.where(i > 0, halo_ref[...], 0.0)
        ext = jnp.concatenate([halo, u_ref[...]], axis=0)
        pre = _conv_pre(ext, w_ref, b_ref, ts)
        sg = _sigmoid(pre)
        dpre = da_ref[...] * sg * (1.0 + pre * (1.0 - sg))
        dpre_ref[...] = dpre
        rows = [jnp.sum(dpre * pltpu.roll(ext, SSM_CONV - 1 - kk, 0)[CONV_HALO:], axis=0, keepdims=True)
                for kk in range(SSM_CONV - 1)]
        rows.append(jnp.sum(dpre * ext[CONV_HALO:], axis=0, keepdims=True))
        dwp = jnp.concatenate(rows, axis=0)
        dbp = jnp.sum(dpre, axis=0, keepdims=True)

        @pl.when(i == 0)
        def _():
            dw_ref[...] = dwp
            db_ref[...] = dbp

        @pl.when(i > 0)
        def _():
            dw_ref[...] += dwp
            db_ref[...] += dbp

    src, halo, blk, wspec, bspec = _conv_specs(ts)
    return pl.pallas_call(
        body, name="conv_bwd_pre", grid=(c // CONV_TC, s // ts),
        in_specs=[src, halo, wspec, bspec, blk], out_specs=[blk, wspec, bspec],
        out_shape=[jax.ShapeDtypeStruct((s, c), F32), jax.ShapeDtypeStruct((SSM_CONV, c), F32),
                   jax.ShapeDtypeStruct((1, c), F32)],
        compiler_params=_params("parallel", "arbitrary"),
    )(zx, zx, w, b, dact)


def _conv_bwd_in(dpre, w, dzx):
    s, c = dpre.shape
    ts, tc = _tile(s, 512, 8), CONV_TC
    ns = s // ts

    def body(d_ref, halo_ref, w_ref, dzx_ref, o_ref):
        del dzx_ref
        halo = jnp.where(pl.program_id(1) < ns - 1, halo_ref[...], 0.0)
        ext = jnp.concatenate([d_ref[...], halo], axis=0)
        du = w_ref[SSM_CONV - 1:SSM_CONV, :] * ext[:ts]
        for kk in range(SSM_CONV - 1):
            du = du + w_ref[kk:kk + 1, :] * pltpu.roll(ext, ts + CONV_HALO - (SSM_CONV - 1 - kk), 0)[:ts]
        o_ref[...] = du.astype(BF16)

    blk = pl.BlockSpec((ts, tc), lambda j, i: (i, j))
    halo = pl.BlockSpec((CONV_HALO, tc), lambda j, i: (jnp.minimum((i + 1) * (ts // CONV_HALO), s // CONV_HALO - 1), j))
    return pl.pallas_call(
        body, name="conv_bwd_in", grid=(c // tc, ns),
        in_specs=[blk, halo, pl.BlockSpec((SSM_CONV, tc), lambda j, i: (0, j)), ANY],
        out_specs=pl.BlockSpec((ts, tc), lambda j, i: (i, XBC_OFF + j)),
        out_shape=jax.ShapeDtypeStruct(dzx.shape, BF16), input_output_aliases={3: 0},
        compiler_params=_params("parallel", "parallel"),
    )(dpre, dpre, w, dzx)


def _ssd_common(dt_ref, dtt_ref, bias_ref, biast_ref, alog_ref, alogt_ref):
    ln = SSM_CHUNK
    raw, rawt = dt_ref[0] + bias_ref[0], dtt_ref[0] + biast_ref[0]
    dt, dtt = _softplus(raw), _softplus(rawt)
    a, at = -jnp.exp(alog_ref[0]), -jnp.exp(alogt_ref[0])
    tri = jnp.where(_iota((ln, ln), 0) >= _iota((ln, ln), 1), 1.0, 0.0).astype(BF16)
    return dict(raw=raw, rawt=rawt, dt=dt, dtt=dtt, a=a, at=at, tri=tri,
                acum=_xdot(tri, dt * a), acumt=_dot_x_nt(dtt * at, tri))


def _ssd_specs(nc, rev):
    cidx = (lambda c: nc - 1 - c) if rev else (lambda c: c)
    ln = SSM_CHUNK
    xs = pl.BlockSpec((ln, SSM_GN), lambda g, c: (cidx(c), g))
    bs = pl.BlockSpec((ln, SSM_STATE), lambda g, c: (cidx(c), SSM_D_INNER // SSM_STATE + g))
    cs = pl.BlockSpec((ln, SSM_STATE), lambda g, c: (cidx(c), SSM_D_INNER // SSM_STATE + SSM_GROUPS + g))
    dt = pl.BlockSpec((1, ln, SSM_HPG), lambda g, c: (g, cidx(c), 0))
    dtt = pl.BlockSpec((1, SSM_HPG, ln), lambda g, c: (g, 0, cidx(c)))
    row = pl.BlockSpec((1, 1, SSM_HPG), lambda g, c: (g, 0, 0))
    col = pl.BlockSpec((1, SSM_HPG, 1), lambda g, c: (g, 0, 0))
    st = pl.BlockSpec((1, SSM_HPG, SSM_P, SSM_STATE), lambda g, c: (cidx(c), g, 0, 0))
    return xs, bs, cs, dt, dtt, row, col, st


def _ssd_fwd(xbc, dt_g, dt_gt, bias_r, bias_c, alog_r, alog_c, d_r):
    s = xbc.shape[0]
    ln = SSM_CHUNK
    nc = s // ln

    def body(x_ref, b_ref, c_ref, dt_ref, dtt_ref, bias_ref, biast_ref, alog_ref, alogt_ref, d_ref,
             y_ref, st_ref, state):
        @pl.when(pl.program_id(1) == 0)
        def _():
            state[...] = jnp.zeros_like(state)

        cm = _ssd_common(dt_ref, dtt_ref, bias_ref, biast_ref, alog_ref, alogt_ref)
        dt, acum, acumt = cm["dt"], cm["acum"], cm["acumt"]
        bb, cb = b_ref[...].astype(BF16), c_ref[...].astype(BF16)
        cbm = _dot_nt(cb, bb)
        causal = _iota((ln, ln), 0) >= _iota((ln, ln), 1)
        st_ref[0] = state[...]
        for r in range(SSM_HPG):
            xr = x_ref[:, r * SSM_P:(r + 1) * SSM_P]
            ac, last = acum[:, r:r + 1], acum[ln - 1:ln, r:r + 1]
            decay = jnp.exp(jnp.where(causal, ac - acumt[r:r + 1, :], -jnp.inf))
            xdt = xr * dt[:, r:r + 1]
            sr = state[r]
            y = (_dot((cbm * decay).astype(BF16), xdt.astype(BF16))
                 + _dot_nt(cb, sr.astype(BF16)) * jnp.exp(ac) + d_ref[0][:, r:r + 1] * xr)
            y_ref[:, r * SSM_P:(r + 1) * SSM_P] = y
            state[r] = sr * jnp.exp(last) + _dot_tn((xdt * jnp.exp(last - ac)).astype(BF16), bb)

    xs, bs, cs, dts, dtts, row, col, st = _ssd_specs(nc, False)
    return pl.pallas_call(
        body, name="ssd_fwd", grid=(SSM_GROUPS, nc),
        in_specs=[xs, bs, cs, dts, dtts, row, col, row, col, row],
        out_specs=[xs, st],
        out_shape=[jax.ShapeDtypeStruct((s, SSM_D_INNER), F32),
                   jax.ShapeDtypeStruct((nc, SSM_HEADS, SSM_P, SSM_STATE), F32)],
        scratch_shapes=[pltpu.VMEM((SSM_HPG, SSM_P, SSM_STATE), F32)],
        compiler_params=_params("parallel", "arbitrary"),
    )(xbc, xbc, xbc, dt_g, dt_gt, bias_r, bias_c, alog_r, alog_c, d_r)


def _ssd_bwd(xbc, dt_g, dt_gt, bias_r, bias_c, alog_r, alog_c, d_r, states, dy):
    s = xbc.shape[0]
    ln = SSM_CHUNK
    nc = s // ln

    def body(x_ref, b_ref, c_ref, dt_ref, dtt_ref, bias_ref, biast_ref, alog_ref, alogt_ref, d_ref,
             st_ref, dy_ref, dx_ref, db_ref, dc_ref, ddt_ref, ddtt_ref, dbias_ref, dbiast_ref,
             dalog_ref, dalogt_ref, dd_ref, dstate):
        step = pl.program_id(1)

        @pl.when(step == 0)
        def _():
            dstate[...] = jnp.zeros_like(dstate)

        cm = _ssd_common(dt_ref, dtt_ref, bias_ref, biast_ref, alog_ref, alogt_ref)
        dt, acum, acumt = cm["dt"], cm["acum"], cm["acumt"]
        bb, cb = b_ref[...].astype(BF16), c_ref[...].astype(BF16)
        cbm = _dot_nt(cb, bb)
        causal = _iota((ln, ln), 0) >= _iota((ln, ln), 1)
        lane8 = _iota((ln, SSM_HPG), 1)
        sub8 = _iota((SSM_HPG, ln), 0)
        lane1 = _iota((1, SSM_HPG), 1)
        is_last = _iota((ln, 1), 0) == ln - 1
        dcb = jnp.zeros((ln, ln), F32)
        dc_acc = jnp.zeros((ln, SSM_STATE), F32)
        db_acc = jnp.zeros((ln, SSM_STATE), F32)
        dac_rows = jnp.zeros((ln, SSM_HPG), F32)
        dac_cols = jnp.zeros((SSM_HPG, ln), F32)
        ddt_all = jnp.zeros((ln, SSM_HPG), F32)
        dd_all = jnp.zeros((1, SSM_HPG), F32)
        for r in range(SSM_HPG):
            sl = slice(r * SSM_P, (r + 1) * SSM_P)
            xr, dyr = x_ref[:, sl], dy_ref[:, sl]
            dtc, dr = dt[:, r:r + 1], d_ref[0][:, r:r + 1]
            ac, last = acum[:, r:r + 1], acum[ln - 1:ln, r:r + 1]
            decay = jnp.exp(jnp.where(causal, ac - acumt[r:r + 1, :], -jnp.inf))
            w = (cbm * decay).astype(BF16)
            xdt = xr * dtc
            xdtb, dyb = xdt.astype(BF16), dyr.astype(BF16)
            eac, to_end, elast = jnp.exp(ac), jnp.exp(last - ac), jnp.exp(last)
            sr, dsr = st_ref[0, r], dstate[r]
            srb, dsrb = sr.astype(BF16), dsr.astype(BF16)
            dxdt_state = _dot_nt(bb, dsrb) * to_end
            dxdt = _dot_tn(w, dyb) + dxdt_state
            dcb_r = _dot_nt(dyb, xdtb) * decay
            dcb = dcb + dcb_r
            e = dcb_r * cbm
            dc_acc = dc_acc + _dot(dyb, srb) * eac
            db_acc = db_acc + _dot((xdt * to_end).astype(BF16), dsrb)
            yoff = _dot_nt(cb, srb) * eac
            f_rows = jnp.sum(xdt * dxdt_state, axis=-1, keepdims=True)
            dlast = jnp.sum(f_rows) + elast * jnp.sum(dsr * sr)
            dac = (jnp.sum(e, axis=-1, keepdims=True) + jnp.sum(dyr * yoff, axis=-1, keepdims=True) - f_rows
                   + jnp.where(is_last, dlast, 0.0))
            dac_rows = dac_rows + jnp.where(lane8 == r, dac, 0.0)
            dac_cols = dac_cols + jnp.where(sub8 == r, jnp.sum(e, axis=0, keepdims=True), 0.0)
            ddt_all = ddt_all + jnp.where(lane8 == r, jnp.sum(dxdt * xr, axis=-1, keepdims=True), 0.0)
            dd_all = dd_all + jnp.where(lane1 == r, jnp.sum(dyr * xr), 0.0)
            dx_ref[:, sl] = dxdt * dtc + dr * dyr
            dstate[r] = elast * dsr + _dot_tn((dyr * eac).astype(BF16), cb)
        dcbb = dcb.astype(BF16)
        dc_ref[...] = dc_acc + _dot(dcbb, bb)
        db_ref[...] = db_acc + _dot_tn(dcbb, cb)
        triu = jnp.where(_iota((ln, ln), 0) <= _iota((ln, ln), 1), 1.0, 0.0).astype(BF16)
        g_rows = _xdot(triu, dac_rows)
        g_cols = _dot_x(dac_cols, cm["tri"])
        d_rows = (ddt_all + g_rows * cm["a"]) * _sigmoid(cm["raw"])
        d_cols = -(g_cols * cm["at"]) * _sigmoid(cm["rawt"])
        ddt_ref[0] = d_rows
        ddtt_ref[0] = d_cols
        parts = (jnp.sum(d_rows, axis=0, keepdims=True), jnp.sum(d_cols, axis=1, keepdims=True),
                 jnp.sum(g_rows * dt, axis=0, keepdims=True) * cm["a"],
                 -jnp.sum(g_cols * cm["dtt"], axis=1, keepdims=True) * cm["at"], dd_all)
        outs = (dbias_ref, dbiast_ref, dalog_ref, dalogt_ref, dd_ref)

        @pl.when(step == 0)
        def _():
            for o_ref, p in zip(outs, parts):
                o_ref[0] = p

        @pl.when(step > 0)
        def _():
            for o_ref, p in zip(outs, parts):
                o_ref[0] += p

    xs, bs, cs, dts, dtts, row, col, st = _ssd_specs(nc, True)
    grp = pl.BlockSpec((ln, SSM_STATE), lambda g, c: (nc - 1 - c, g))
    rows = jax.ShapeDtypeStruct((SSM_GROUPS, 1, SSM_HPG), F32)
    cols = jax.ShapeDtypeStruct((SSM_GROUPS, SSM_HPG, 1), F32)
    return pl.pallas_call(
        body, name="ssd_bwd", grid=(SSM_GROUPS, nc),
        in_specs=[xs, bs, cs, dts, dtts, row, col, row, col, row, st, xs],
        out_specs=[xs, grp, grp, dts, dtts, row, col, row, col, row],
        out_shape=[jax.ShapeDtypeStruct((s, SSM_D_INNER), F32),
                   jax.ShapeDtypeStruct((s, SSM_GROUPS * SSM_STATE), F32),
                   jax.ShapeDtypeStruct((s, SSM_GROUPS * SSM_STATE), F32),
                   jax.ShapeDtypeStruct((SSM_GROUPS, s, SSM_HPG), F32),
                   jax.ShapeDtypeStruct((SSM_GROUPS, SSM_HPG, s), F32), rows, cols, rows, cols, rows],
        scratch_shapes=[pltpu.VMEM((SSM_HPG, SSM_P, SSM_STATE), F32)],
        compiler_params=_params("parallel", "arbitrary"),
    )(xbc, xbc, xbc, dt_g, dt_gt, bias_r, bias_c, alog_r, alog_c, d_r, states, dy)


def _gate_norm_fwd(y, zx, g):
    s = y.shape[0]
    ts = _tile(s, 512, 8)

    def body(y_ref, z_ref, g_ref, o_ref):
        zv = z_ref[...]
        yg = y_ref[...] * (zv * _sigmoid(zv))
        r = lax.rsqrt(jnp.mean(yg * yg, axis=-1, keepdims=True) + EPS)
        o_ref[...] = (yg * r * g_ref[...]).astype(BF16)

    blk = pl.BlockSpec((ts, SSM_GN), lambda j, i: (i, j))
    vec = pl.BlockSpec((1, SSM_GN), lambda j, i: (0, j))
    return pl.pallas_call(
        body, name="gate_norm_fwd", grid=(SSM_GROUPS, s // ts), in_specs=[blk, blk, vec], out_specs=blk,
        out_shape=jax.ShapeDtypeStruct((s, SSM_D_INNER), BF16), compiler_params=_params("parallel", "parallel"),
    )(y, zx, g)


def _gate_norm_bwd(y, zx, g, dout):
    s = y.shape[0]
    ts = _tile(s, 512, 8)

    def body(y_ref, z_ref, g_ref, do_ref, dy_ref, dz_ref, dg_ref):
        yv, zv, dov = y_ref[...], z_ref[...], do_ref[...].astype(F32)
        sg = _sigmoid(zv)
        silu = zv * sg
        yg = yv * silu
        r = lax.rsqrt(jnp.mean(yg * yg, axis=-1, keepdims=True) + EPS)
        ygn = yg * r
        part = jnp.sum(dov * ygn, axis=0, keepdims=True)

        @pl.when(pl.program_id(1) == 0)
        def _():
            dg_ref[...] = part

        @pl.when(pl.program_id(1) > 0)
        def _():
            dg_ref[...] += part

        dn = dov * g_ref[...]
        dyg = r * (dn - ygn * jnp.mean(dn * ygn, axis=-1, keepdims=True))
        dy_ref[...] = dyg * silu
        dz_ref[...] = (dyg * yv * sg * (1.0 + zv * (1.0 - sg))).astype(BF16)

    blk = pl.BlockSpec((ts, SSM_GN), lambda j, i: (i, j))
    vec = pl.BlockSpec((1, SSM_GN), lambda j, i: (0, j))
    return pl.pallas_call(
        body, name="gate_norm_bwd", grid=(SSM_GROUPS, s // ts), in_specs=[blk, blk, vec, blk],
        out_specs=[blk, blk, vec],
        out_shape=[jax.ShapeDtypeStruct((s, SSM_D_INNER), F32), jax.ShapeDtypeStruct((s, SSM_IN_PAD), BF16),
                   jax.ShapeDtypeStruct((1, SSM_D_INNER), F32)],
        compiler_params=_params("parallel", "arbitrary"),
    )(y, zx, g, dout)


def _rope_tables(positions):
    inv_freq = ROPE_THETA ** (-jnp.arange(0, ATT_HEAD_DIM, 2, dtype=F32) / ATT_HEAD_DIM)
    ang = positions.astype(F32)[:, None] * inv_freq
    return jnp.tile(jnp.cos(ang), (1, 4)), jnp.tile(jnp.sin(ang), (1, 4))


def _group_views(v):
    return v.reshape(SSM_GROUPS, 1, SSM_HPG), v.reshape(SSM_GROUPS, SSM_HPG, 1)


def _ffn_fwd(x, norm_g, wg, wu, wd, tag):
    h = _rms_fwd(x, norm_g, f"ffn_norm_{tag}")
    g = _mm(h, wg, "nn", f"ffn_gate_{tag}", b_cols=True)
    u, a = _mm(h, wu, "nn", f"ffn_up_{tag}", b_cols=True, rows=FUSED_ROWS,
               fuse=(lambda uv, gv: (uv, gv * _sigmoid(gv) * uv), [g], [F32, BF16]))
    return _mm(a, wd, "nn", f"ffn_down_{tag}", add=x), (h, g, u, a)


def _ffn_bwd(x, norm_g, wg, wu, wd, saved, dout, tag):
    h, g, u, a = saved

    def act_bwd(da, gv, uv):
        sg = _sigmoid(gv)
        return da * uv * sg * (1.0 + gv * (1.0 - sg)), da * (gv * sg)

    dg, du = _mm(dout, wd, "nt", f"ffn_down_dx_{tag}", rows=FUSED_ROWS, fuse=(act_bwd, [g, u], [BF16, BF16]))
    dwd = _mm(a, dout, "tn", f"ffn_down_dw_{tag}", out_dtype=BF16)
    dwg = _mm(h, dg, "tn", f"ffn_gate_dw_{tag}", out_dtype=BF16, out_cols=True)
    dwu = _mm(h, du, "tn", f"ffn_up_dw_{tag}", out_dtype=BF16, out_cols=True)
    dh = _mm(dg, wg, "nt", f"ffn_gate_dx_{tag}", b_cols=True)
    dh = _mm(du, wu, "nt", f"ffn_up_dx_{tag}", add=dh, b_cols=True)
    dx, dgain = _rms_bwd(x, norm_g, dh, dout, f"ffn_norm_bwd_{tag}")
    return dx, dgain, dwg, dwu, dwd


def _local_step(x, positions, target, w):
    cos, sin = _rope_tables(positions)
    row = lambda v: v.reshape(1, -1)
    gq, gk = jnp.tile(row(w["attn_q_norm"]), (1, 2)), jnp.tile(row(w["attn_k_norm"]), (1, 2))
    sinks = w["attn_sinks"].reshape(-1)
    s = x.shape[0]

    h0 = _rms_fwd(x, row(w["mixer_norm"][0]), "mixer_norm_0")
    qkv = _mm(h0, w["attn_w_qkv"], "nn", "attn_qkv", b_cols=True)
    q, kk, vlo, vhi = _attn_prep(qkv, cos, sin, gq, gk)
    o = _attn_fwd(q, kk, vlo, vhi, sinks)
    x1 = _mm(o, w["attn_w_o"], "nn", "attn_out", add=x)
    ffn_w = lambda l: (row(w["ffn_norm"][l]), w["ffn_w_gate"][l], w["ffn_w_up"][l], w["ffn_w_down"][l])
    x2, ffn0 = _ffn_fwd(x1, *ffn_w(0), 0)

    h2 = _rms_fwd(x2, row(w["mixer_norm"][1]), "mixer_norm_1")
    zx = _mm(h2, w["ssm_w_in"], "nn", "ssm_in")
    dt_g = zx[:, DT_OFF:DT_OFF + SSM_HEADS].reshape(s, SSM_GROUPS, SSM_HPG).transpose(1, 0, 2)
    dt_gt = dt_g.transpose(0, 2, 1)
    bias_r, bias_c = _group_views(w["ssm_dt_bias"].reshape(-1))
    alog_r, alog_c = _group_views(w["ssm_a_log"].reshape(-1))
    d_r, _ = _group_views(w["ssm_d"].reshape(-1))
    xbc = _conv_fwd(zx, w["ssm_conv_w"], row(w["ssm_conv_b"]))
    ssd_args = (xbc, dt_g, dt_gt, bias_r, bias_c, alog_r, alog_c, d_r)
    y, states = _ssd_fwd(*ssd_args)
    yn = _gate_norm_fwd(y, zx, row(w["ssm_norm"]))
    x3 = _mm(yn, w["ssm_w_out"], "nn", "ssm_out", add=x2)
    x4, ffn1 = _ffn_fwd(x3, *ffn_w(1), 1)

    loss_row, dx4 = _loss_fwd_bwd(x4, target)

    dx3, dfn1, dwg1, dwu1, dwd1 = _ffn_bwd(x3, *ffn_w(1), ffn1, dx4, 1)
    dyn = _mm(dx3, w["ssm_w_out"], "nt", "ssm_out_dx")
    dw_out = _mm(yn, dx3, "tn", "ssm_out_dw", out_dtype=BF16)
    dy, dzx, dssm_norm = _gate_norm_bwd(y, zx, row(w["ssm_norm"]), dyn)
    dxs, db, dc, ddt_g, ddt_gt, dbias, dbias_t, dalog, dalog_t, dd = _ssd_bwd(*ssd_args, states, dy)
    ddt_g = ddt_g + ddt_gt.transpose(0, 2, 1)
    dpre, dconv_w, dconv_b = _conv_bwd_pre(zx, w["ssm_conv_w"], row(w["ssm_conv_b"]),
                                           jnp.concatenate([dxs, db, dc], axis=1))
    dzx = _conv_bwd_in(dpre, w["ssm_conv_w"], dzx)
    ddt_pad = jnp.pad(ddt_g.transpose(1, 0, 2).reshape(s, SSM_HEADS), ((0, 0), (0, SSM_IN_PAD - SSM_IN)))
    dzx = lax.dynamic_update_slice(dzx, ddt_pad.astype(BF16), (0, DT_OFF))
    dw_in = _mm(h2, dzx, "tn", "ssm_in_dw", out_dtype=BF16)
    dh2 = _mm(dzx, w["ssm_w_in"], "nt", "ssm_in_dx")
    dx2, dmn1 = _rms_bwd(x2, row(w["mixer_norm"][1]), dh2, dx3, "mixer_norm_bwd_1")

    dx1, dfn0, dwg0, dwu0, dwd0 = _ffn_bwd(x1, *ffn_w(0), ffn0, dx2, 0)
    do = _mm(dx1, w["attn_w_o"], "nt", "attn_out_dx", out_dtype=BF16)
    dw_o = _mm(o, dx1, "tn", "attn_out_dw", out_dtype=BF16)
    dq, dkc, dkp, dvloc, dvlop, dvhic, dvhip, dsink = _attn_bwd(q, kk, vlo, vhi, sinks, do)
    dqkv, dgq, dgk = _attn_prep_bwd(qkv, cos, sin, gq, gk, dq, (dkc, dkp), (dvloc, dvlop), (dvhic, dvhip))
    dw_qkv = _mm(h0, dqkv, "tn", "attn_qkv_dw", out_dtype=BF16, out_cols=True)
    dh0 = _mm(dqkv, w["attn_w_qkv"], "nt", "attn_qkv_dx", b_cols=True)
    dx0, dmn0 = _rms_bwd(x, row(w["mixer_norm"][0]), dh0, dx1, "mixer_norm_bwd_0")

    fold = lambda v: v[0, :ATT_HEAD_DIM] + v[0, ATT_HEAD_DIM:]
    row_stack = lambda g: g.reshape(N_SHARDS, g.shape[0] // N_SHARDS, g.shape[1])
    in_shard = SSM_IN // N_SHARDS
    grads = {
        "mixer_norm": jnp.concatenate([dmn0, dmn1], axis=0),
        "ffn_norm": jnp.concatenate([dfn0, dfn1], axis=0),
        "attn_w_qkv": [dw_qkv],
        "attn_q_norm": fold(dgq), "attn_k_norm": fold(dgk),
        "attn_sinks": dsink[:, :, 0].reshape(-1),
        "attn_w_o": [row_stack(dw_o)],
        "ssm_w_in": [jnp.stack([dw_in[:, i * in_shard:(i + 1) * in_shard] for i in range(N_SHARDS)])],
        "ssm_conv_w": dconv_w, "ssm_conv_b": dconv_b.reshape(-1),
        "ssm_dt_bias": dbias.reshape(-1) + dbias_t.reshape(-1),
        "ssm_a_log": dalog.reshape(-1) + dalog_t.reshape(-1), "ssm_d": dd.reshape(-1),
        "ssm_norm": dssm_norm.reshape(-1),
        "ssm_w_out": [row_stack(dw_out)],
        "ffn_w_gate": [dwg0, dwg1], "ffn_w_up": [dwu0, dwu1],
        "ffn_w_down": [row_stack(dwd0), row_stack(dwd1)],
    }
    return loss_row[0, 0], dx0, grads


OTHER_CHIPS = ((1, 0), (0, 1), (1, 1))


def _position():
    return lax.axis_index("x"), lax.axis_index("y"), lax.axis_index("c")


def _sems(n):
    return pltpu.SemaphoreType.DMA((n,))


def _gather_shards(weights, layers):
    n_in, n_mat = len(weights), len(layers)

    def body(*refs):
        p, out = refs[:n_in], refs[n_in:n_in + n_mat]
        send_sems, recv_sems = refs[n_in + n_mat:]
        x, y, c = _position()
        me, sibling = (x, y, c), (x, y, 1 - c)
        chips = [(x ^ fx, y ^ fy) for fx, fy in OTHER_CHIPS]

        def rows(e, px, py, pc):
            half = out[e].shape[1] // 2
            return out[e].at[2 * px + py, pl.ds(pc * half, half), :]

        def copy(k, e, block, to, src=None):
            return pltpu.make_async_remote_copy(
                src_ref=rows(e, *block) if src is None else src, dst_ref=rows(e, *block),
                send_sem=send_sems.at[k * n_mat + e], recv_sem=recv_sems.at[k * n_mat + e],
                device_id=to, device_id_type=MESH)

        def own(e):
            i, l = layers[e]
            return pltpu.make_async_remote_copy(
                src_ref=p[i].at[l], dst_ref=out[e].at[2 * x + y], send_sem=send_sems.at[6 * n_mat + e],
                recv_sem=recv_sems.at[6 * n_mat + e], device_id=sibling, device_id_type=MESH)

        first, passed = [], []
        for e, (i, l) in enumerate(layers):
            half = out[e].shape[1] // 2
            first.append([copy(j, e, me, (*chip, c), src=p[i].at[l, pl.ds(c * half, half), :])
                          for j, chip in enumerate(chips)])
            for cp in first[-1]:
                cp.start()
        for e in range(n_mat):
            own(e).start()
        for e in range(n_mat):
            passed.append([copy(3 + j, e, (*chip, c), sibling) for j, chip in enumerate(chips)])
            for j, chip in enumerate(chips):
                copy(j, e, (*chip, c), me).wait_recv()
                passed[e][j].start()
        for e in range(n_mat):
            own(e).wait()
            for j, chip in enumerate(chips):
                copy(3 + j, e, (*chip, 1 - c), me).wait_recv()
        for e in range(n_mat):
            for cp in first[e] + passed[e]:
                cp.wait_send()

    return pl.pallas_call(
        body, name="gather_weights", in_specs=[ANY] * n_in, out_specs=[ANY] * n_mat,
        out_shape=[jax.ShapeDtypeStruct((N_SHARDS,) + weights[i].shape[1:], weights[i].dtype) for i, _ in layers],
        scratch_shapes=[_sems(7 * n_mat), _sems(7 * n_mat)],
    )(*weights)


def _all_gather8(block, name):
    m_per, n = block.shape

    def body(x_ref, out_ref, send_sems, recv_sems, local_sem):
        x, y, c = _position()
        me, sibling = (x, y, c), (x, y, 1 - c)
        chips = [(x ^ fx, y ^ fy) for fx, fy in OTHER_CHIPS]

        def rows(px, py, pc):
            return out_ref.at[pl.ds((4 * px + 2 * py + pc) * m_per, m_per), :]

        def copy(k, blk, to, src=None):
            return pltpu.make_async_remote_copy(
                src_ref=rows(*blk) if src is None else src, dst_ref=rows(*blk),
                send_sem=send_sems.at[k], recv_sem=recv_sems.at[k], device_id=to, device_id_type=MESH)

        mine = pltpu.make_async_copy(x_ref, rows(*me), local_sem)
        mine.start()
        first = [copy(0, me, sibling, src=x_ref)]
        first += [copy(1 + j, me, (*chip, c), src=x_ref) for j, chip in enumerate(chips)]
        for cp in first:
            cp.start()
        passed = [copy(4 + j, (*chip, c), sibling) for j, chip in enumerate(chips)]
        for j, chip in enumerate(chips):
            copy(1 + j, (*chip, c), me).wait_recv()
            passed[j].start()
        copy(0, sibling, me).wait_recv()
        for j, chip in enumerate(chips):
            copy(4 + j, (*chip, 1 - c), me).wait_recv()
        for cp in first + passed:
            cp.wait_send()
        mine.wait()

    return pl.pallas_call(
        body, name=name, out_shape=jax.ShapeDtypeStruct((N_DEV * m_per, n), block.dtype),
        in_specs=[pl.BlockSpec(memory_space=pltpu.VMEM)], out_specs=pl.BlockSpec(memory_space=pltpu.VMEM),
        scratch_shapes=[_sems(7), _sems(7), pltpu.SemaphoreType.DMA],
    )(block)


def _swap_other_half(stacks):
    n_mat = len(stacks)

    def body(*refs):
        g, out, send_sems, recv_sems = refs[:n_mat], refs[n_mat:2 * n_mat], refs[-2], refs[-1]
        x, y, c = _position()
        copies = []
        for e in range(n_mat):
            half = g[e].shape[1] // 2
            copies.append(pltpu.make_async_remote_copy(
                src_ref=g[e].at[:, pl.ds((1 - c) * half, half), :], dst_ref=out[e], send_sem=send_sems.at[e],
                recv_sem=recv_sems.at[e], device_id=(x, y, 1 - c), device_id_type=MESH))
            copies[-1].start()
        for cp in copies:
            cp.wait()

    return pl.pallas_call(
        body, name="grads_to_sibling", in_specs=[ANY] * n_mat, out_specs=[ANY] * n_mat,
        out_shape=[jax.ShapeDtypeStruct((N_SHARDS, g.shape[1] // 2, g.shape[2]), g.dtype) for g in stacks],
        scratch_shapes=[_sems(n_mat), _sems(n_mat)],
    )(*stacks)


def _send_to_owners(partials):
    n_mat = len(partials)

    def body(*refs):
        p, out, send_sems, recv_sems = refs[:n_mat], refs[n_mat:2 * n_mat], refs[-2], refs[-1]
        x, y, c = _position()
        copies = []
        for e in range(n_mat):
            for k, (fx, fy) in enumerate(OTHER_CHIPS):
                px, py = x ^ fx, y ^ fy
                copies.append(pltpu.make_async_remote_copy(
                    src_ref=p[e].at[2 * px + py], dst_ref=out[e].at[k], send_sem=send_sems.at[3 * e + k],
                    recv_sem=recv_sems.at[3 * e + k], device_id=(px, py, c), device_id_type=MESH))
                copies[-1].start()
        for cp in copies:
            cp.wait()

    return pl.pallas_call(
        body, name="grads_to_owners", in_specs=[ANY] * n_mat, out_specs=[ANY] * n_mat,
        out_shape=[jax.ShapeDtypeStruct((len(OTHER_CHIPS),) + p.shape[1:], p.dtype) for p in partials],
        scratch_shapes=[_sems(3 * n_mat), _sems(3 * n_mat)],
    )(*partials)


def _share_halves(grads, layers):
    n = len(grads)

    def body(*refs):
        out, send_sems, recv_sems = refs[n:2 * n], refs[-2], refs[-1]
        x, y, c = _position()
        copies = []
        for e, (i, l) in enumerate(layers):
            half = out[i].shape[1] // 2
            rows = out[i].at[l, pl.ds(c * half, half), :]
            copies.append(pltpu.make_async_remote_copy(
                src_ref=rows, dst_ref=rows, send_sem=send_sems.at[e], recv_sem=recv_sems.at[e],
                device_id=(x, y, 1 - c), device_id_type=MESH))
            copies[-1].start()
        for cp in copies:
            cp.wait()

    return pl.pallas_call(
        body, name="grads_share_halves", in_specs=[ANY] * n, out_specs=[ANY] * n,
        out_shape=[jax.ShapeDtypeStruct(g.shape, g.dtype) for g in grads],
        input_output_aliases={i: i for i in range(n)},
        scratch_shapes=[_sems(len(layers)), _sems(len(layers))],
    )(*grads)


ADD_BLOCK_ELEMS = 1 << 19


def _add_rows(half, cols):
    return _tile(half, max(16, ADD_BLOCK_ELEMS // cols // 16 * 16), 16)


def _add_pair(stack, recv, c_idx, name):
    _, half, cols = recv.shape
    tr = _add_rows(half, cols)
    nt = half // tr

    def body(c_ref, a_ref, b_ref, o_ref):
        o_ref[...] = (a_ref[...].astype(F32) + b_ref[...].astype(F32)).astype(o_ref.dtype)

    blk = pl.BlockSpec((None, tr, cols), lambda s, i, c_ref: (s, i, 0))
    return pl.pallas_call(
        body, name=name,
        grid_spec=pltpu.PrefetchScalarGridSpec(
            num_scalar_prefetch=1, grid=(N_SHARDS, nt),
            in_specs=[pl.BlockSpec((None, tr, cols), lambda s, i, c_ref: (s, c_ref[0] * nt + i, 0)), blk],
            out_specs=blk),
        out_shape=jax.ShapeDtypeStruct(recv.shape, recv.dtype),
        compiler_params=_params("parallel", "parallel"),
    )(c_idx, stack, recv)


def _add_owned(partial, recv, sc_idx, layer, shape, into, name):
    _, half, cols = partial.shape
    tr = _add_rows(half, cols)
    nt = half // tr

    def body(sc_ref, a_ref, r0_ref, r1_ref, r2_ref, *rest):
        o_ref = rest[-1]
        o_ref[...] = (((a_ref[...].astype(F32) + r0_ref[...].astype(F32)) + r1_ref[...].astype(F32))
                      + r2_ref[...].astype(F32))

    slot = lambda k: pl.BlockSpec((None, tr, cols), lambda i, sc_ref: (k, i, 0))
    has_into = into is not None
    return pl.pallas_call(
        body, name=name,
        grid_spec=pltpu.PrefetchScalarGridSpec(
            num_scalar_prefetch=1, grid=(nt,),
            in_specs=[pl.BlockSpec((None, tr, cols), lambda i, sc_ref: (sc_ref[0], i, 0)), slot(0), slot(1), slot(2)]
            + ([ANY] if has_into else []),
            out_specs=pl.BlockSpec((None, tr, cols), lambda i, sc_ref: (layer, sc_ref[1] * nt + i, 0))),
        out_shape=jax.ShapeDtypeStruct(shape, F32),
        input_output_aliases={5: 0} if has_into else {},
        compiler_params=_params("parallel"),
    )(*((sc_idx, partial, recv, recv, recv) + ((into,) if has_into else ())))


def _sum8(gathered):
    m = gathered.shape[0] // N_DEV

    def body(g_ref, o_ref):
        total = g_ref[0:m, :]
        for d in range(1, N_DEV):
            total = total + g_ref[d * m:(d + 1) * m, :]
        o_ref[...] = total

    return pl.pallas_call(
        body, name="small_grads_sum", out_shape=jax.ShapeDtypeStruct((m, LANES), F32),
        in_specs=[pl.BlockSpec(memory_space=pltpu.VMEM)], out_specs=pl.BlockSpec(memory_space=pltpu.VMEM),
    )(gathered)


ADAMW_BLOCK_ELEMS = 1 << 18


def _adamw(w, g, m, v, name):
    l, r, cols = w.shape
    tr = _tile(r, max(8, ADAMW_BLOCK_ELEMS // cols // 8 * 8), 8)

    def body(w_ref, g_ref, m_ref, v_ref, d_ref, nm_ref, nv_ref):
        gv = g_ref[...]
        nm = ADAM_B1 * m_ref[...] + (1.0 - ADAM_B1) * gv
        nv = ADAM_B2 * v_ref[...] + (1.0 - ADAM_B2) * jnp.square(gv)
        m_hat = nm / (1.0 - ADAM_B1 ** ADAM_STEP)
        v_hat = nv / (1.0 - ADAM_B2 ** ADAM_STEP)
        d_ref[...] = -ADAM_LR * (m_hat / (jnp.sqrt(v_hat) + ADAM_EPS) + ADAM_WD * w_ref[...])
        nm_ref[...] = nm
        nv_ref[...] = nv

    blk = pl.BlockSpec((None, tr, cols), lambda a, i: (a, i, 0))
    return pl.pallas_call(
        body, name=name, grid=(l, r // tr), in_specs=[blk] * 4, out_specs=[blk] * 3,
        out_shape=[jax.ShapeDtypeStruct(w.shape, F32)] * 3, compiler_params=_params("parallel", "parallel"),
    )(w, g, m, v)


WEIGHTS = ("mixer_norm", "ffn_norm", "attn_w_qkv", "attn_q_norm", "attn_k_norm", "attn_sinks", "attn_w_o",
           "ssm_w_in", "ssm_conv_w", "ssm_conv_b", "ssm_dt_bias", "ssm_a_log", "ssm_d", "ssm_norm", "ssm_w_out",
           "ffn_w_gate", "ffn_w_up", "ffn_w_down")
BIG = ("attn_w_qkv", "attn_w_o", "ffn_w_gate", "ffn_w_up", "ffn_w_down", "ssm_w_in", "ssm_w_out")
MATRICES = (("attn_w_qkv", 0), ("attn_w_o", 0), ("ffn_w_gate", 0), ("ffn_w_up", 0), ("ffn_w_down", 0),
            ("ssm_w_in", 0), ("ssm_w_out", 0), ("ffn_w_gate", 1), ("ffn_w_up", 1), ("ffn_w_down", 1))
MATRIX_LAYERS = tuple((BIG.index(n), l) for n, l in MATRICES)
SMALL_SHARDED = ("ssm_conv_w", "ssm_conv_b", "ssm_norm")
SMALL = tuple(n for n in WEIGHTS if n not in BIG)


def _pack_rows(parts, row_unit=8):
    flat = jnp.concatenate([p.reshape(-1) for p in parts])
    pad = (-flat.shape[0]) % (LANES * row_unit)
    return jnp.pad(flat, (0, pad)).reshape(-1, LANES)


def _unpack(flat, shapes):
    out, off = [], 0
    for shp in shapes:
        size = math.prod(shp)
        out.append(flat[off:off + size].reshape(shp))
        off += size
    return out


def kernel(x, positions, mixer_norm, ffn_norm, attn_w_qkv, attn_q_norm, attn_k_norm, attn_sinks, attn_w_o, ssm_w_in, ssm_conv_w, ssm_conv_b, ssm_dt_bias, ssm_a_log, ssm_d, ssm_norm, ssm_w_out, ffn_w_gate, ffn_w_up, ffn_w_down, loss_target, m_mixer_norm, m_ffn_norm, m_attn_w_qkv, m_attn_q_norm, m_attn_k_norm, m_attn_sinks, m_attn_w_o, m_ssm_w_in, m_ssm_conv_w, m_ssm_conv_b, m_ssm_dt_bias, m_ssm_a_log, m_ssm_d, m_ssm_norm, m_ssm_w_out, m_ffn_w_gate, m_ffn_w_up, m_ffn_w_down, v_mixer_norm, v_ffn_norm, v_attn_w_qkv, v_attn_q_norm, v_attn_k_norm, v_attn_sinks, v_attn_w_o, v_ssm_w_in, v_ssm_conv_w, v_ssm_conv_b, v_ssm_dt_bias, v_ssm_a_log, v_ssm_d, v_ssm_norm, v_ssm_w_out, v_ffn_w_gate, v_ffn_w_up, v_ffn_w_down):
    args = locals()
    w = {n: args[n] for n in WEIGHTS}
    m = {n: args["m_" + n] for n in WEIGHTS}
    v = {n: args["v_" + n] for n in WEIGHTS}
    ax, ay, ac = lax.axis_index("x"), lax.axis_index("y"), lax.axis_index("c")
    shard = 2 * ax + ay

    stacks = dict(zip(MATRICES, _gather_shards([w[n].astype(BF16) for n in BIG], MATRIX_LAYERS)))
    rows_merged = lambda st: st.reshape(st.shape[0] * st.shape[1], st.shape[2])
    w_in = stacks[("ssm_w_in", 0)]
    w_in = jnp.concatenate([w_in[i] for i in range(N_SHARDS)]
                           + [jnp.zeros((w_in.shape[1], SSM_IN_PAD - SSM_IN), BF16)], axis=1)
    small_shapes = [w[n].shape for n in SMALL_SHARDED]
    small_all = _all_gather8(_pack_rows([w[n] for n in SMALL_SHARDED]), "gather_small_params")
    small_all = small_all.reshape(N_DEV, -1)[::2]
    full, off = {}, 0
    for n, shp in zip(SMALL_SHARDED, small_shapes):
        size = math.prod(shp)
        seg = small_all[:, off:off + size].reshape((N_SHARDS,) + shp)
        full[n] = jnp.moveaxis(seg, 0, -2).reshape(shp[:-1] + (N_SHARDS * shp[-1],))
        off += size
    wl = {
        "mixer_norm": mixer_norm, "ffn_norm": ffn_norm, "attn_w_qkv": stacks[("attn_w_qkv", 0)],
        "attn_q_norm": attn_q_norm[0], "attn_k_norm": attn_k_norm[0], "attn_sinks": attn_sinks[0],
        "attn_w_o": rows_merged(stacks[("attn_w_o", 0)]), "ssm_w_in": w_in,
        "ssm_conv_w": full["ssm_conv_w"][0], "ssm_conv_b": full["ssm_conv_b"][0],
        "ssm_dt_bias": ssm_dt_bias[0], "ssm_a_log": ssm_a_log[0], "ssm_d": ssm_d[0],
        "ssm_norm": full["ssm_norm"][0], "ssm_w_out": rows_merged(stacks[("ssm_w_out", 0)]),
        "ffn_w_gate": [stacks[("ffn_w_gate", l)] for l in range(2)],
        "ffn_w_up": [stacks[("ffn_w_up", l)] for l in range(2)],
        "ffn_w_down": [rows_merged(stacks[("ffn_w_down", l)]) for l in range(2)],
    }

    loss_part, dx, g_full = _local_step(x[0], positions[0], loss_target[0], wl)

    g_stacks = [g_full[n][l] for n, l in MATRICES]
    c_idx = ac.reshape(1).astype(jnp.int32)
    sc_idx = jnp.stack([shard, ac]).astype(jnp.int32)
    recv = _swap_other_half(g_stacks)
    partials = [_add_pair(g, r, c_idx, f"grads_add_pair_{e}") for e, (g, r) in enumerate(zip(g_stacks, recv))]
    recv = _send_to_owners(partials)
    halves = {n: None for n in BIG}
    for e, (n, l) in enumerate(MATRICES):
        halves[n] = _add_owned(partials[e], recv[e], sc_idx, l, w[n].shape, halves[n], f"grads_add_owned_{e}")
    grads = dict(zip(BIG, _share_halves([halves[n] for n in BIG], MATRIX_LAYERS)))

    small_full_shapes = [g_full[n].shape for n in SMALL] + [(1,)]
    small_g = _pack_rows([g_full[n] for n in SMALL] + [loss_part.reshape(1)])
    small_sum = _sum8(_all_gather8(small_g, "gather_small_grads")).reshape(-1)
    *small_list, loss = _unpack(small_sum, small_full_shapes)
    for n, g in zip(SMALL, small_list):
        if n in SMALL_SHARDED:
            width = w[n].shape[-1]
            g = lax.dynamic_slice_in_dim(g, shard * width, width, axis=g.ndim - 1)
        grads[n] = g.reshape(w[n].shape)

    delta, new_m, new_v = {}, {}, {}
    for n in BIG:
        delta[n], new_m[n], new_v[n] = _adamw(w[n], grads[n], m[n], v[n], "adamw_" + n)
    small_local = [w[n].shape for n in SMALL]
    pk = lambda t: _pack_rows([t[n] for n in SMALL])[None]
    outs = _adamw(pk(w), pk(grads), pk(m), pk(v), "adamw_small")
    for res, o in zip((delta, new_m, new_v), outs):
        for n, a in zip(SMALL, _unpack(o.reshape(-1), small_local)):
            res[n] = a

    return (loss.reshape(()), dx[None], *[grads[n] for n in WEIGHTS], *[delta[n] for n in WEIGHTS],
            *[new_m[n] for n in WEIGHTS], *[new_v[n] for n in WEIGHTS])
```

```python
import math

import jax
import jax.numpy as jnp
from jax import lax
from jax.experimental import pallas as pl
from jax.experimental.pallas import tpu as pltpu

F32 = jnp.float32
BF16 = jnp.bfloat16

D_MODEL = 2048
EPS = 1e-6
ATT_HEAD_DIM = 64
ATT_Q_HEADS = 32
ATT_KV_HEADS = 4
ATT_GROUP = 8
ATT_BLOCK = 128
ROPE_THETA = 10000.0
Q_WIDTH = ATT_Q_HEADS * ATT_HEAD_DIM
KV_WIDTH = ATT_KV_HEADS * ATT_HEAD_DIM
SSM_D_INNER = 4096
SSM_HEADS = 64
SSM_GROUPS = 8
SSM_HPG = 8
SSM_P = 64
SSM_STATE = 128
SSM_CONV = 4
SSM_CHUNK = 256
SSM_CONV_DIM = 6144
SSM_GN = SSM_D_INNER // SSM_GROUPS
SSM_IN = SSM_D_INNER + SSM_CONV_DIM + SSM_HEADS
LANES = 128
SSM_IN_PAD = -(-SSM_IN // LANES) * LANES
N_SHARDS = 4
N_DEV = 8

ADAM_LR = 0.001
ADAM_B1 = 0.9
ADAM_B2 = 0.999
ADAM_EPS = 1e-08
ADAM_WD = 0.01
ADAM_STEP = 10

VMEM_LIMIT = 56 * 1024 * 1024
MESH = pl.DeviceIdType.MESH
ANY = pl.BlockSpec(memory_space=pl.ANY)


def _params(*sem):
    return pltpu.CompilerParams(dimension_semantics=sem, vmem_limit_bytes=VMEM_LIMIT)


def _sems(n):
    return pltpu.SemaphoreType.DMA((n,))


def _call(body, carry, name, grid, in_specs, out_specs, out_shape, scratch_shapes, sem, args):
    if carry is None:
        return pl.pallas_call(body, name=name, grid=grid, in_specs=in_specs, out_specs=out_specs,
                              out_shape=out_shape, scratch_shapes=scratch_shapes,
                              compiler_params=_params(*sem))(*args)
    n_in, n_out, n_scr = len(in_specs), len(out_specs), len(scratch_shapes)
    c_arrays, c_shapes = list(carry["arrays"]), list(carry["out_shapes"])
    n_cin, n_cout = len(c_arrays), len(c_shapes)

    def carrying(*refs):
        ins, refs = refs[:n_in], refs[n_in:]
        cin, refs = refs[:n_cin], refs[n_cin:]
        outs, refs = refs[:n_out], refs[n_out:]
        cout, refs = refs[:n_cout], refs[n_cout:]
        scratch, (send_sems, recv_sems) = refs[:n_scr], refs[n_scr:]
        copies = carry["build"](cin, cout, send_sems, recv_sems)
        ids = [pl.program_id(d) for d in range(len(grid))]
        first, last = ids[0] == 0, ids[0] == grid[0] - 1
        for d in range(1, len(grid)):
            first = jnp.logical_and(first, ids[d] == 0)
            last = jnp.logical_and(last, ids[d] == grid[d] - 1)

        @pl.when(first)
        def _():
            for cp in copies:
                cp.start()

        body(*ins, *outs, *scratch)

        @pl.when(last)
        def _():
            for cp in copies:
                cp.wait()

    aliases = {n_in + i: n_out + o for i, o in carry.get("aliases", {}).items()}
    return pl.pallas_call(
        carrying, name=name, grid=grid, in_specs=list(in_specs) + [ANY] * n_cin,
        out_specs=list(out_specs) + [ANY] * n_cout, out_shape=list(out_shape) + c_shapes,
        scratch_shapes=list(scratch_shapes) + [_sems(carry["n_sems"]), _sems(carry["n_sems"])],
        input_output_aliases=aliases, compiler_params=_params(*(["arbitrary"] * len(grid))))(*args, *c_arrays)


def _tile(dim, target, unit=LANES):
    if dim <= target:
        return dim
    t = (target // unit) * unit
    while t >= unit:
        if dim % t == 0:
            return t
        t -= unit
    return dim


def _dot(a, b):
    return lax.dot_general(a, b, (((1,), (0,)), ((), ())), preferred_element_type=F32)


def _dot_nt(a, b):
    return lax.dot_general(a, b, (((1,), (1,)), ((), ())), preferred_element_type=F32)


def _dot_tn(a, b):
    return lax.dot_general(a, b, (((0,), (0,)), ((), ())), preferred_element_type=F32)


def _split3(x):
    hi = x.astype(BF16)
    r1 = x - hi.astype(F32)
    mid = r1.astype(BF16)
    lo = (r1 - mid.astype(F32)).astype(BF16)
    return hi, mid, lo


def _dot_x(x, m):
    hi, mid, lo = _split3(x)
    return _dot(hi, m) + _dot(mid, m) + _dot(lo, m)


def _xdot(m, x):
    hi, mid, lo = _split3(x)
    return _dot(m, hi) + _dot(m, mid) + _dot(m, lo)


def _dot_x_nt(x, m):
    hi, mid, lo = _split3(x)
    return _dot_nt(hi, m) + _dot_nt(mid, m) + _dot_nt(lo, m)


def _iota(shape, dim):
    return lax.broadcasted_iota(jnp.int32, shape, dim)


def _sigmoid(x):
    return 1.0 / (1.0 + jnp.exp(-x))


def _softplus(x):
    return jnp.maximum(x, 0.0) + jnp.log(1.0 + jnp.exp(-jnp.abs(x)))


MM_ROWS = 1024
MM_TILE = 1408
FUSED_ROWS = 512


def _mm(a, b, mode, name, add=None, out_dtype=F32, b_cols=False, out_cols=False, fuse=None, rows=MM_ROWS,
        carry=None):
    bs = b.shape[-2:]
    if b_cols:
        bs = (bs[0], N_SHARDS * bs[1])
    if mode == "nn":
        (m, k), (k2, n) = a.shape, bs
    elif mode == "nt":
        (m, k), (n, k2) = a.shape, bs
    else:
        (k, m), (k2, n) = a.shape, bs
    assert k == k2, (a.shape, b.shape, mode)
    split_n = (b_cols and mode == "nn") or out_cols
    split_k = b_cols and mode == "nt"
    tm = _tile(m, MM_TILE if mode == "tn" else rows)
    tn = _tile(n // N_SHARDS if split_n else n, MM_TILE)
    tk = _tile(k // N_SHARDS if split_k else k, MM_ROWS if mode == "tn" else MM_TILE)
    nk = k // tk
    nj, nq = (n // N_SHARDS) // tn, (k // N_SHARDS) // tk
    if mode == "tn":
        a_spec = pl.BlockSpec((tk, tm), lambda i, j, q: (q, i))
    else:
        a_spec = pl.BlockSpec((tm, tk), lambda i, j, q: (i, q))
    if mode == "nt":
        if b_cols:
            b_spec = pl.BlockSpec((None, tn, tk), lambda i, j, q: (q // nq, j, q % nq))
        else:
            b_spec = pl.BlockSpec((tn, tk), lambda i, j, q: (j, q))
    elif b_cols:
        b_spec = pl.BlockSpec((None, tk, tn), lambda i, j, q: (j // nj, q, j % nj))
    else:
        b_spec = pl.BlockSpec((tk, tn), lambda i, j, q: (q, j))
    add_spec = pl.BlockSpec((tm, tn), lambda i, j, q: (i, j))
    if out_cols:
        o_spec = pl.BlockSpec((None, tm, tn), lambda i, j, q: (j // nj, i, j % nj))
        o_shape = (N_SHARDS, m, n // N_SHARDS)
    else:
        o_spec, o_shape = add_spec, (m, n)
    dot = {"nn": _dot, "nt": _dot_nt, "tn": _dot_tn}[mode]
    has_add = add is not None
    fuse_fn, extra, out_dtypes = fuse if fuse is not None else (None, [], [out_dtype])
    n_in, n_out = 2 + has_add + len(extra), len(out_dtypes)

    def body(*refs):
        a_ref, b_ref = refs[:2]
        add_ref = refs[2] if has_add else None
        extra_refs = refs[2 + has_add:n_in]
        o_refs, acc_ref = refs[n_in:n_in + n_out], refs[n_in + n_out]
        part = dot(a_ref[...].astype(BF16), b_ref[...].astype(BF16))

        def finish(total):
            if has_add:
                total = total + add_ref[...].astype(F32)
            outs = (total,) if fuse_fn is None else fuse_fn(total, *[r[...] for r in extra_refs])
            for o_ref, val in zip(o_refs, outs):
                o_ref[...] = val.astype(o_ref.dtype)

        if nk == 1:
            finish(part)
        else:
            q = pl.program_id(2)

            @pl.when(q == 0)
            def _():
                acc_ref[...] = part

            @pl.when(jnp.logical_and(q > 0, q < nk - 1))
            def _():
                acc_ref[...] += part

            @pl.when(q == nk - 1)
            def _():
                finish(acc_ref[...] + part)

    in_specs = [a_spec, b_spec] + [add_spec] * (has_add + len(extra))
    args = (a, b) + ((add,) if has_add else ()) + tuple(extra)
    res = _call(body, carry, name, (m // tm, n // tn, nk), in_specs, [o_spec] * n_out,
                [jax.ShapeDtypeStruct(o_shape, dt) for dt in out_dtypes],
                [pltpu.VMEM((tm, tn) if nk > 1 else (8, LANES), F32)], ("parallel", "parallel", "arbitrary"), args)
    main = res[0] if fuse is None else res[:n_out]
    return main if carry is None else (main, res[n_out:])


def _rms_fwd(x, g, name):
    s, d = x.shape
    ts = _tile(s, 512, 8)

    def body(x_ref, g_ref, o_ref):
        xv = x_ref[...]
        r = lax.rsqrt(jnp.mean(xv * xv, axis=-1, keepdims=True) + EPS)
        o_ref[...] = (xv * r * g_ref[...]).astype(BF16)

    return pl.pallas_call(
        body, name=name, grid=(s // ts,),
        in_specs=[pl.BlockSpec((ts, d), lambda i: (i, 0)), pl.BlockSpec((1, d), lambda i: (0, 0))],
        out_specs=pl.BlockSpec((ts, d), lambda i: (i, 0)),
        out_shape=jax.ShapeDtypeStruct((s, d), BF16),
        compiler_params=_params("parallel"),
    )(x, g)


def _rms_bwd(x, g, dh, dres, name):
    s, d = x.shape
    ts = _tile(s, 512, 8)

    def body(x_ref, g_ref, dh_ref, dres_ref, dx_ref, dg_ref):
        xv = x_ref[...]
        r = lax.rsqrt(jnp.mean(xv * xv, axis=-1, keepdims=True) + EPS)
        xhat = xv * r
        dhv = dh_ref[...].astype(F32)
        part = jnp.sum(dhv * xhat, axis=0, keepdims=True)

        @pl.when(pl.program_id(0) == 0)
        def _():
            dg_ref[...] = part

        @pl.when(pl.program_id(0) > 0)
        def _():
            dg_ref[...] += part

        dxh = dhv * g_ref[...]
        dx = r * (dxh - xhat * jnp.mean(dxh * xhat, axis=-1, keepdims=True))
        dx_ref[...] = dres_ref[...] + dx

    row = pl.BlockSpec((ts, d), lambda i: (i, 0))
    vec = pl.BlockSpec((1, d), lambda i: (0, 0))
    return pl.pallas_call(
        body, name=name, grid=(s // ts,),
        in_specs=[row, vec, row, row], out_specs=[row, vec],
        out_shape=[jax.ShapeDtypeStruct((s, d), F32), jax.ShapeDtypeStruct((1, d), F32)],
        compiler_params=_params("arbitrary"),
    )(x, g, dh, dres)


def _act_fwd(g, u, name):
    s, f = g.shape
    ts, tf = _tile(s, 512, 8), _tile(f, 1408)

    def body(g_ref, u_ref, o_ref):
        gv = g_ref[...]
        o_ref[...] = (gv * _sigmoid(gv) * u_ref[...]).astype(BF16)

    blk = pl.BlockSpec((ts, tf), lambda i, j: (i, j))
    return pl.pallas_call(
        body, name=name, grid=(s // ts, f // tf), in_specs=[blk, blk], out_specs=blk,
        out_shape=jax.ShapeDtypeStruct((s, f), BF16), compiler_params=_params("parallel", "parallel"),
    )(g, u)


def _act_bwd(g, u, da, name):
    s, f = g.shape
    ts, tf = _tile(s, 512, 8), _tile(f, 1408)

    def body(g_ref, u_ref, da_ref, dg_ref, du_ref):
        gv, uv, dav = g_ref[...], u_ref[...], da_ref[...].astype(F32)
        sg = _sigmoid(gv)
        silu = gv * sg
        du_ref[...] = (dav * silu).astype(BF16)
        dg_ref[...] = (dav * uv * sg * (1.0 + gv * (1.0 - sg))).astype(BF16)

    blk = pl.BlockSpec((ts, tf), lambda i, j: (i, j))
    return pl.pallas_call(
        body, name=name, grid=(s // ts, f // tf), in_specs=[blk, blk, blk], out_specs=[blk, blk],
        out_shape=[jax.ShapeDtypeStruct((s, f), BF16)] * 2, compiler_params=_params("parallel", "parallel"),
    )(g, u, da)


def _loss_fwd_bwd(y, target):
    s, d = y.shape
    ts = _tile(s, 512, 8)

    def body(y_ref, t_ref, l_ref, dy_ref):
        diff = y_ref[...] - t_ref[...]
        dy_ref[...] = diff * (1.0 / d)
        part = jnp.full((1, LANES), 0.5 * jnp.sum(jnp.mean(diff * diff, axis=-1, keepdims=True)), F32)

        @pl.when(pl.program_id(0) == 0)
        def _():
            l_ref[...] = part

        @pl.when(pl.program_id(0) > 0)
        def _():
            l_ref[...] += part

    row = pl.BlockSpec((ts, d), lambda i: (i, 0))
    acc = pl.BlockSpec((1, LANES), lambda i: (0, 0))
    return pl.pallas_call(
        body, name="loss", grid=(s // ts,), in_specs=[row, row], out_specs=[acc, row],
        out_shape=[jax.ShapeDtypeStruct((1, LANES), F32), jax.ShapeDtypeStruct((s, d), F32)],
        compiler_params=_params("arbitrary"),
    )(y, target)


def _lane_consts():
    r, c = _iota((LANES, LANES), 0), _iota((LANES, LANES), 1)
    same = (r >> 6) == (c >> 6)
    rin, cin = r & 63, c & 63
    one = lambda cond: jnp.where(cond, 1.0, 0.0).astype(BF16)
    return dict(
        seg=one(same),
        rot=(jnp.where(same & (rin == cin + 32), -1.0, 0.0)
             + jnp.where(same & (cin == rin + 32), 1.0, 0.0)).astype(BF16),
        dup_lo=one(r == cin), dup_hi=one(r == cin + 64),
        up=one((c >= 64) & (r == c - 64)), down=one((c < 64) & (r == c + 64)),
        fold_lo=one((c < 64) & (rin == c)), fold_hi=one((c >= 64) & (rin == c - 64)),
    )


def _norm_rope(xc, gain, cos, sin, k):
    ss = _dot_x(xc * xc, k["seg"])
    rinv = lax.rsqrt(ss * (1.0 / ATT_HEAD_DIM) + EPS)
    xhat = xc * rinv
    y = xhat * gain
    return y * cos + _dot_x(y, k["rot"]) * sin, xhat, rinv


def _norm_rope_bwd(dr, xhat, rinv, gain, cos, sin, k):
    dy = dr * cos - _dot_x(dr * sin, k["rot"])
    dgain = jnp.sum(dy * xhat, axis=0, keepdims=True)
    dxh = dy * gain
    dx = rinv * (dxh - xhat * (_dot_x(dxh * xhat, k["seg"]) * (1.0 / ATT_HEAD_DIM)))
    return dx, dgain


def _attn_prep(qkv, cos, sin, gq, gk):
    s = qkv.shape[0]
    tr = _tile(s, 256, 8)

    def body(x_ref, cos_ref, sin_ref, gq_ref, gk_ref, q_ref, kk_ref, vlo_ref, vhi_ref):
        k = _lane_consts()
        cosv, sinv = cos_ref[...], sin_ref[...]
        lane = _iota((tr, LANES), 1)
        for j in range(Q_WIDTH // LANES):
            r, _, _ = _norm_rope(x_ref[:, j * LANES:(j + 1) * LANES], gq_ref[...], cosv, sinv, k)
            q_ref[:, j * LANES:(j + 1) * LANES] = r.astype(BF16)
        for i in range(KV_WIDTH // LANES):
            off = Q_WIDTH + i * LANES
            r, _, _ = _norm_rope(x_ref[:, off:off + LANES], gk_ref[...], cosv, sinv, k)
            rb = r.astype(BF16)
            kk_ref[:, (2 * i) * LANES:(2 * i + 1) * LANES] = _dot(rb, k["dup_lo"]).astype(BF16)
            kk_ref[:, (2 * i + 1) * LANES:(2 * i + 2) * LANES] = _dot(rb, k["dup_hi"]).astype(BF16)
            off = Q_WIDTH + KV_WIDTH + i * LANES
            vb = x_ref[:, off:off + LANES].astype(BF16)
            zero = jnp.zeros_like(vb)
            vlo_ref[:, (2 * i) * LANES:(2 * i + 1) * LANES] = jnp.where(lane < 64, vb, zero)
            vhi_ref[:, (2 * i) * LANES:(2 * i + 1) * LANES] = _dot(vb, k["up"]).astype(BF16)
            vlo_ref[:, (2 * i + 1) * LANES:(2 * i + 2) * LANES] = _dot(vb, k["down"]).astype(BF16)
            vhi_ref[:, (2 * i + 1) * LANES:(2 * i + 2) * LANES] = jnp.where(lane >= 64, vb, zero)

    w = qkv.shape[1]
    row = lambda width: pl.BlockSpec((tr, width), lambda i: (i, 0))
    vec = pl.BlockSpec((1, LANES), lambda i: (0, 0))
    kw = ATT_KV_HEADS * LANES
    return pl.pallas_call(
        body, name="attn_prep", grid=(s // tr,),
        in_specs=[row(w), row(LANES), row(LANES), vec, vec],
        out_specs=[row(Q_WIDTH), row(kw), row(kw), row(kw)],
        out_shape=[jax.ShapeDtypeStruct((s, Q_WIDTH), BF16)] + [jax.ShapeDtypeStruct((s, kw), BF16)] * 3,
        compiler_params=_params("parallel"),
    )(qkv, cos, sin, gq, gk)


def _band_mask(n):
    qi = _iota((ATT_BLOCK, 2 * ATT_BLOCK), 0)
    kj = _iota((ATT_BLOCK, 2 * ATT_BLOCK), 1)
    band = (kj > qi) & (kj <= qi + ATT_BLOCK)
    return band & ((kj >= ATT_BLOCK) | (n > 0))


def _softmax_sink(s, valid, sink):
    s = jnp.where(valid, s, -jnp.inf)
    m = jnp.maximum(jnp.max(s, axis=-1, keepdims=True), sink)
    p = jnp.exp(s - m)
    esink = jnp.exp(sink - m)
    inv = 1.0 / (jnp.sum(p, axis=-1, keepdims=True) + esink)
    return p * inv, esink * inv


def _attn_specs(order):
    if order == "nh":
        cur = lambda n, h: (n, h)
        prev = lambda n, h: (jnp.maximum(n - 1, 0), h)
    else:
        cur = lambda h, n: (n, h)
        prev = lambda h, n: (jnp.maximum(n - 1, 0), h)
    qs = pl.BlockSpec((ATT_BLOCK, ATT_GROUP * ATT_HEAD_DIM), cur)
    kc = pl.BlockSpec((ATT_BLOCK, LANES), cur)
    kp = pl.BlockSpec((ATT_BLOCK, LANES), prev)
    return qs, kc, kp


def _attn_fwd(q, kk, vlo, vhi, sinks, name="attn_fwd", carry=None):
    s = q.shape[0]
    nb = s // ATT_BLOCK
    scale = ATT_HEAD_DIM ** -0.5

    def body(sink_ref, q_ref, kc_ref, kp_ref, vloc_ref, vlop_ref, vhic_ref, vhip_ref, o_ref):
        n, h = pl.program_id(0), pl.program_id(1)
        valid = _band_mask(n)
        kw = jnp.concatenate([kp_ref[...], kc_ref[...]], axis=0)
        vw = (jnp.concatenate([vlop_ref[...], vloc_ref[...]], axis=0),
              jnp.concatenate([vhip_ref[...], vhic_ref[...]], axis=0))
        lane = _iota((ATT_BLOCK, LANES), 1)
        for jp in range(ATT_GROUP // 2):
            qp = q_ref[:, jp * LANES:(jp + 1) * LANES]
            acc = jnp.zeros((ATT_BLOCK, LANES), F32)
            for hf in range(2):
                qm = jnp.where((lane >= 64) == (hf == 1), qp, jnp.zeros_like(qp))
                sc = _dot_nt(qm, kw) * scale
                probs, _ = _softmax_sink(sc, valid, sink_ref[h * ATT_GROUP + 2 * jp + hf])
                acc = acc + _dot(probs.astype(BF16), vw[hf])
            o_ref[:, jp * LANES:(jp + 1) * LANES] = acc.astype(BF16)

    qs, kc, kp = _attn_specs("nh")
    res = _call(body, carry, name, (nb, ATT_KV_HEADS),
                [pl.BlockSpec(memory_space=pltpu.SMEM), qs, kc, kp, kc, kp, kc, kp], [qs],
                [jax.ShapeDtypeStruct((s, Q_WIDTH), BF16)], [], ("parallel", "parallel"),
                (sinks, q, kk, kk, vlo, vlo, vhi, vhi))
    return res[0] if carry is None else (res[0], res[1:])


def _attn_bwd(q, kk, vlo, vhi, sinks, do, name="attn_bwd", carry=None):
    s = q.shape[0]
    nb = s // ATT_BLOCK
    scale = ATT_HEAD_DIM ** -0.5

    def body(sink_ref, q_ref, kc_ref, kp_ref, vloc_ref, vlop_ref, vhic_ref, vhip_ref, do_ref,
             dq_ref, dkc_ref, dkp_ref, dvloc_ref, dvlop_ref, dvhic_ref, dvhip_ref, dsink_ref):
        h, n = pl.program_id(0), pl.program_id(1)
        valid = _band_mask(n)
        kw = jnp.concatenate([kp_ref[...], kc_ref[...]], axis=0)
        vw = (jnp.concatenate([vlop_ref[...], vloc_ref[...]], axis=0),
              jnp.concatenate([vhip_ref[...], vhic_ref[...]], axis=0))
        lane = _iota((ATT_BLOCK, LANES), 1)
        sub = _iota((ATT_GROUP, LANES), 0)
        dkk = jnp.zeros((2 * ATT_BLOCK, LANES), F32)
        dv = [jnp.zeros((2 * ATT_BLOCK, LANES), F32), jnp.zeros((2 * ATT_BLOCK, LANES), F32)]
        dsink = jnp.zeros((ATT_GROUP, LANES), F32)
        for jp in range(ATT_GROUP // 2):
            qp = q_ref[:, jp * LANES:(jp + 1) * LANES]
            dop = do_ref[:, jp * LANES:(jp + 1) * LANES]
            dq = jnp.zeros((ATT_BLOCK, LANES), F32)
            for hf in range(2):
                mine = (lane >= 64) == (hf == 1)
                qm = jnp.where(mine, qp, jnp.zeros_like(qp))
                sc = _dot_nt(qm, kw) * scale
                probs, psink = _softmax_sink(sc, valid, sink_ref[h * ATT_GROUP + 2 * jp + hf])
                pb = probs.astype(BF16)
                dprobs = _dot_nt(dop, vw[hf])
                dv[hf] = dv[hf] + _dot_tn(pb, dop)
                delta = jnp.sum(probs * dprobs, axis=-1, keepdims=True)
                ds = (probs * (dprobs - delta) * scale).astype(BF16)
                dsink = dsink + jnp.where(sub == 2 * jp + hf, -jnp.sum(psink * delta), 0.0)
                dq = dq + jnp.where(mine, _dot(ds, kw), 0.0)
                dkk = dkk + _dot_tn(ds, qm)
            dq_ref[:, jp * LANES:(jp + 1) * LANES] = dq
        dkp_ref[...], dkc_ref[...] = dkk[:ATT_BLOCK], dkk[ATT_BLOCK:]
        dvlop_ref[...], dvloc_ref[...] = dv[0][:ATT_BLOCK], dv[0][ATT_BLOCK:]
        dvhip_ref[...], dvhic_ref[...] = dv[1][:ATT_BLOCK], dv[1][ATT_BLOCK:]

        @pl.when(n == 0)
        def _():
            dsink_ref[0] = dsink

        @pl.when(n > 0)
        def _():
            dsink_ref[0] += dsink

    qs, kc, kp = _attn_specs("hn")
    kw_shape = jax.ShapeDtypeStruct((s, ATT_KV_HEADS * LANES), F32)
    res = _call(body, carry, name, (ATT_KV_HEADS, nb),
                [pl.BlockSpec(memory_space=pltpu.SMEM), qs, kc, kp, kc, kp, kc, kp, qs],
                [qs] + [kc] * 6 + [pl.BlockSpec((1, ATT_GROUP, LANES), lambda h, n: (h, 0, 0))],
                [jax.ShapeDtypeStruct((s, Q_WIDTH), F32)] + [kw_shape] * 6
                + [jax.ShapeDtypeStruct((ATT_KV_HEADS, ATT_GROUP, LANES), F32)], [], ("parallel", "arbitrary"),
                (sinks, q, kk, kk, vlo, vlo, vhi, vhi, do))
    return res if carry is None else (res[:8], res[8:])


def _attn_prep_bwd(qkv, cos, sin, gq, gk, dq, dks, dvlos, dvhis):
    s, w = qkv.shape
    tr = ATT_BLOCK
    nb = s // tr

    def body(x_ref, cos_ref, sin_ref, gq_ref, gk_ref, dq_ref, dkc_ref, dkn_ref, dvloc_ref, dvlon_ref,
             dvhic_ref, dvhin_ref, dx_ref, dgq_ref, dgk_ref):
        n = pl.program_id(0)
        k = _lane_consts()
        cosv, sinv = cos_ref[...], sin_ref[...]
        nxt = jnp.where(n < nb - 1, 1.0, 0.0)
        lane = _iota((tr, LANES), 1)
        dgq = jnp.zeros((1, LANES), F32)
        dgk = jnp.zeros((1, LANES), F32)
        for j in range(Q_WIDTH // LANES):
            sl = slice(j * LANES, (j + 1) * LANES)
            _, xhat, rinv = _norm_rope(x_ref[:, sl], gq_ref[...], cosv, sinv, k)
            dx, dg = _norm_rope_bwd(dq_ref[:, sl], xhat, rinv, gq_ref[...], cosv, sinv, k)
            dx_ref[:, sl] = dx.astype(BF16)
            dgq = dgq + dg
        for i in range(KV_WIDTH // LANES):
            a, b = slice(2 * i * LANES, (2 * i + 1) * LANES), slice((2 * i + 1) * LANES, (2 * i + 2) * LANES)
            dr = (_dot_x(dkc_ref[:, a] + nxt * dkn_ref[:, a], k["fold_lo"])
                  + _dot_x(dkc_ref[:, b] + nxt * dkn_ref[:, b], k["fold_hi"]))
            sl = slice(Q_WIDTH + i * LANES, Q_WIDTH + (i + 1) * LANES)
            _, xhat, rinv = _norm_rope(x_ref[:, sl], gk_ref[...], cosv, sinv, k)
            dx, dg = _norm_rope_bwd(dr, xhat, rinv, gk_ref[...], cosv, sinv, k)
            dx_ref[:, sl] = dx.astype(BF16)
            dgk = dgk + dg
            ta = jnp.where(lane < 64, dvloc_ref[:, a] + nxt * dvlon_ref[:, a], dvhic_ref[:, a] + nxt * dvhin_ref[:, a])
            tb = jnp.where(lane < 64, dvloc_ref[:, b] + nxt * dvlon_ref[:, b], dvhic_ref[:, b] + nxt * dvhin_ref[:, b])
            sl = slice(Q_WIDTH + KV_WIDTH + i * LANES, Q_WIDTH + KV_WIDTH + (i + 1) * LANES)
            dx_ref[:, sl] = (_dot_x(ta, k["fold_lo"]) + _dot_x(tb, k["fold_hi"])).astype(BF16)

        @pl.when(n == 0)
        def _():
            dgq_ref[...] = dgq
            dgk_ref[...] = dgk

        @pl.when(n > 0)
        def _():
            dgq_ref[...] += dgq
            dgk_ref[...] += dgk

    row = lambda width: pl.BlockSpec((tr, width), lambda i: (i, 0))
    nxt_row = pl.BlockSpec((tr, ATT_KV_HEADS * LANES), lambda i: (jnp.minimum(i + 1, nb - 1), 0))
    vec = pl.BlockSpec((1, LANES), lambda i: (0, 0))
    kw = ATT_KV_HEADS * LANES
    return pl.pallas_call(
        body, name="attn_prep_bwd", grid=(nb,),
        in_specs=[row(w), row(LANES), row(LANES), vec, vec, row(Q_WIDTH),
                  row(kw), nxt_row, row(kw), nxt_row, row(kw), nxt_row],
        out_specs=[row(w), vec, vec],
        out_shape=[jax.ShapeDtypeStruct((s, w), BF16), jax.ShapeDtypeStruct((1, LANES), F32),
                   jax.ShapeDtypeStruct((1, LANES), F32)],
        compiler_params=_params("arbitrary"),
    )(qkv, cos, sin, gq, gk, dq, dks[0], dks[1], dvlos[0], dvlos[1], dvhis[0], dvhis[1])


CONV_HALO = 8
CONV_TC = 512
XBC_OFF = SSM_D_INNER // CONV_TC
DT_OFF = SSM_D_INNER + SSM_CONV_DIM


def _conv_pre(ext, w_ref, b_ref, ts):
    pre = b_ref[...] + w_ref[SSM_CONV - 1:SSM_CONV, :] * ext[CONV_HALO:]
    for kk in range(SSM_CONV - 1):
        pre = pre + w_ref[kk:kk + 1, :] * pltpu.roll(ext, SSM_CONV - 1 - kk, 0)[CONV_HALO:]
    return pre


def _conv_specs(ts):
    tc = CONV_TC
    src = pl.BlockSpec((ts, tc), lambda j, i: (i, XBC_OFF + j))
    halo = pl.BlockSpec((CONV_HALO, tc), lambda j, i: (jnp.maximum(i * (ts // CONV_HALO) - 1, 0), XBC_OFF + j))
    blk = pl.BlockSpec((ts, tc), lambda j, i: (i, j))
    wspec = pl.BlockSpec((SSM_CONV, tc), lambda j, i: (0, j))
    bspec = pl.BlockSpec((1, tc), lambda j, i: (0, j))
    return src, halo, blk, wspec, bspec


def _conv_fwd(zx, w, b):
    s, c = zx.shape[0], SSM_CONV_DIM
    ts = _tile(s, 512, 8)

    def body(u_ref, halo_ref, w_ref, b_ref, o_ref):
        halo = jnp.where(pl.program_id(1) > 0, halo_ref[...], 0.0)
        pre = _conv_pre(jnp.concatenate([halo, u_ref[...]], axis=0), w_ref, b_ref, ts)
        o_ref[...] = pre * _sigmoid(pre)

    src, halo, blk, wspec, bspec = _conv_specs(ts)
    return pl.pallas_call(
        body, name="conv_fwd", grid=(c // CONV_TC, s // ts),
        in_specs=[src, halo, wspec, bspec], out_specs=blk, out_shape=jax.ShapeDtypeStruct((s, c), F32),
        compiler_params=_params("parallel", "parallel"),
    )(zx, zx, w, b)


def _conv_bwd_pre(zx, w, b, dact):
    s, c = zx.shape[0], SSM_CONV_DIM
    ts = _tile(s, 512, 8)

    def body(u_ref, halo_ref, w_ref, b_ref, da_ref, dpre_ref, dw_ref, db_ref):
        i = pl.program_id(1)
        halo = jnp.where(i > 0, halo_ref[...], 0.0)
        ext = jnp.concatenate([halo, u_ref[...]], axis=0)
        pre = _conv_pre(ext, w_ref, b_ref, ts)
        sg = _sigmoid(pre)
        dpre = da_ref[...] * sg * (1.0 + pre * (1.0 - sg))
        dpre_ref[...] = dpre
        rows = [jnp.sum(dpre * pltpu.roll(ext, SSM_CONV - 1 - kk, 0)[CONV_HALO:], axis=0, keepdims=True)
                for kk in range(SSM_CONV - 1)]
        rows.append(jnp.sum(dpre * ext[CONV_HALO:], axis=0, keepdims=True))
        dwp = jnp.concatenate(rows, axis=0)
        dbp = jnp.sum(dpre, axis=0, keepdims=True)

        @pl.when(i == 0)
        def _():
            dw_ref[...] = dwp
            db_ref[...] = dbp

        @pl.when(i > 0)
        def _():
            dw_ref[...] += dwp
            db_ref[...] += dbp

    src, halo, blk, wspec, bspec = _conv_specs(ts)
    return pl.pallas_call(
        body, name="conv_bwd_pre", grid=(c // CONV_TC, s // ts),
        in_specs=[src, halo, wspec, bspec, blk], out_specs=[blk, wspec, bspec],
        out_shape=[jax.ShapeDtypeStruct((s, c), F32), jax.ShapeDtypeStruct((SSM_CONV, c), F32),
                   jax.ShapeDtypeStruct((1, c), F32)],
        compiler_params=_params("parallel", "arbitrary"),
    )(zx, zx, w, b, dact)


def _conv_bwd_in(dpre, w, dzx):
    s, c = dpre.shape
    ts, tc = _tile(s, 512, 8), CONV_TC
    ns = s // ts

    def body(d_ref, halo_ref, w_ref, dzx_ref, o_ref):
        del dzx_ref
        halo = jnp.where(pl.program_id(1) < ns - 1, halo_ref[...], 0.0)
        ext = jnp.concatenate([d_ref[...], halo], axis=0)
        du = w_ref[SSM_CONV - 1:SSM_CONV, :] * ext[:ts]
        for kk in range(SSM_CONV - 1):
            du = du + w_ref[kk:kk + 1, :] * pltpu.roll(ext, ts + CONV_HALO - (SSM_CONV - 1 - kk), 0)[:ts]
        o_ref[...] = du.astype(BF16)

    blk = pl.BlockSpec((ts, tc), lambda j, i: (i, j))
    halo = pl.BlockSpec((CONV_HALO, tc), lambda j, i: (jnp.minimum((i + 1) * (ts // CONV_HALO), s // CONV_HALO - 1), j))
    return pl.pallas_call(
        body, name="conv_bwd_in", grid=(c // tc, ns),
        in_specs=[blk, halo, pl.BlockSpec((SSM_CONV, tc), lambda j, i: (0, j)), ANY],
        out_specs=pl.BlockSpec((ts, tc), lambda j, i: (i, XBC_OFF + j)),
        out_shape=jax.ShapeDtypeStruct(dzx.shape, BF16), input_output_aliases={3: 0},
        compiler_params=_params("parallel", "parallel"),
    )(dpre, dpre, w, dzx)


def _ssd_common(dt_ref, dtt_ref, bias_ref, biast_ref, alog_ref, alogt_ref):
    ln = SSM_CHUNK
    raw, rawt = dt_ref[0] + bias_ref[0], dtt_ref[0] + biast_ref[0]
    dt, dtt = _softplus(raw), _softplus(rawt)
    a, at = -jnp.exp(alog_ref[0]), -jnp.exp(alogt_ref[0])
    tri = jnp.where(_iota((ln, ln), 0) >= _iota((ln, ln), 1), 1.0, 0.0).astype(BF16)
    return dict(raw=raw, rawt=rawt, dt=dt, dtt=dtt, a=a, at=at, tri=tri,
                acum=_xdot(tri, dt * a), acumt=_dot_x_nt(dtt * at, tri))


def _ssd_specs(nc, rev):
    cidx = (lambda c: nc - 1 - c) if rev else (lambda c: c)
    ln = SSM_CHUNK
    xs = pl.BlockSpec((ln, SSM_GN), lambda g, c: (cidx(c), g))
    bs = pl.BlockSpec((ln, SSM_STATE), lambda g, c: (cidx(c), SSM_D_INNER // SSM_STATE + g))
    cs = pl.BlockSpec((ln, SSM_STATE), lambda g, c: (cidx(c), SSM_D_INNER // SSM_STATE + SSM_GROUPS + g))
    dt = pl.BlockSpec((1, ln, SSM_HPG), lambda g, c: (g, cidx(c), 0))
    dtt = pl.BlockSpec((1, SSM_HPG, ln), lambda g, c: (g, 0, cidx(c)))
    row = pl.BlockSpec((1, 1, SSM_HPG), lambda g, c: (g, 0, 0))
    col = pl.BlockSpec((1, SSM_HPG, 1), lambda g, c: (g, 0, 0))
    st = pl.BlockSpec((1, SSM_HPG, SSM_P, SSM_STATE), lambda g, c: (cidx(c), g, 0, 0))
    return xs, bs, cs, dt, dtt, row, col, st


def _ssd_fwd(xbc, dt_g, dt_gt, bias_r, bias_c, alog_r, alog_c, d_r):
    s = xbc.shape[0]
    ln = SSM_CHUNK
    nc = s // ln

    def body(x_ref, b_ref, c_ref, dt_ref, dtt_ref, bias_ref, biast_ref, alog_ref, alogt_ref, d_ref,
             y_ref, st_ref, state):
        @pl.when(pl.program_id(1) == 0)
        def _():
            state[...] = jnp.zeros_like(state)

        cm = _ssd_common(dt_ref, dtt_ref, bias_ref, biast_ref, alog_ref, alogt_ref)
        dt, acum, acumt = cm["dt"], cm["acum"], cm["acumt"]
        bb, cb = b_ref[...].astype(BF16), c_ref[...].astype(BF16)
        cbm = _dot_nt(cb, bb)
        causal = _iota((ln, ln), 0) >= _iota((ln, ln), 1)
        st_ref[0] = state[...]
        for r in range(SSM_HPG):
            xr = x_ref[:, r * SSM_P:(r + 1) * SSM_P]
            ac, last = acum[:, r:r + 1], acum[ln - 1:ln, r:r + 1]
            decay = jnp.exp(jnp.where(causal, ac - acumt[r:r + 1, :], -jnp.inf))
            xdt = xr * dt[:, r:r + 1]
            sr = state[r]
            y = (_dot((cbm * decay).astype(BF16), xdt.astype(BF16))
                 + _dot_nt(cb, sr.astype(BF16)) * jnp.exp(ac) + d_ref[0][:, r:r + 1] * xr)
            y_ref[:, r * SSM_P:(r + 1) * SSM_P] = y
            state[r] = sr * jnp.exp(last) + _dot_tn((xdt * jnp.exp(last - ac)).astype(BF16), bb)

    xs, bs, cs, dts, dtts, row, col, st = _ssd_specs(nc, False)
    return pl.pallas_call(
        body, name="ssd_fwd", grid=(SSM_GROUPS, nc),
        in_specs=[xs, bs, cs, dts, dtts, row, col, row, col, row],
        out_specs=[xs, st],
        out_shape=[jax.ShapeDtypeStruct((s, SSM_D_INNER), F32),
                   jax.ShapeDtypeStruct((nc, SSM_HEADS, SSM_P, SSM_STATE), F32)],
        scratch_shapes=[pltpu.VMEM((SSM_HPG, SSM_P, SSM_STATE), F32)],
        compiler_params=_params("parallel", "arbitrary"),
    )(xbc, xbc, xbc, dt_g, dt_gt, bias_r, bias_c, alog_r, alog_c, d_r)


def _ssd_bwd(xbc, dt_g, dt_gt, bias_r, bias_c, alog_r, alog_c, d_r, states, dy):
    s = xbc.shape[0]
    ln = SSM_CHUNK
    nc = s // ln

    def body(x_ref, b_ref, c_ref, dt_ref, dtt_ref, bias_ref, biast_ref, alog_ref, alogt_ref, d_ref,
             st_ref, dy_ref, dx_ref, db_ref, dc_ref, ddt_ref, ddtt_ref, dbias_ref, dbiast_ref,
             dalog_ref, dalogt_ref, dd_ref, dstate):
        step = pl.program_id(1)

        @pl.when(step == 0)
        def _():
            dstate[...] = jnp.zeros_like(dstate)

        cm = _ssd_common(dt_ref, dtt_ref, bias_ref, biast_ref, alog_ref, alogt_ref)
        dt, acum, acumt = cm["dt"], cm["acum"], cm["acumt"]
        bb, cb = b_ref[...].astype(BF16), c_ref[...].astype(BF16)
        cbm = _dot_nt(cb, bb)
        causal = _iota((ln, ln), 0) >= _iota((ln, ln), 1)
        lane8 = _iota((ln, SSM_HPG), 1)
        sub8 = _iota((SSM_HPG, ln), 0)
        lane1 = _iota((1, SSM_HPG), 1)
        is_last = _iota((ln, 1), 0) == ln - 1
        dcb = jnp.zeros((ln, ln), F32)
        dc_acc = jnp.zeros((ln, SSM_STATE), F32)
        db_acc = jnp.zeros((ln, SSM_STATE), F32)
        dac_rows = jnp.zeros((ln, SSM_HPG), F32)
        dac_cols = jnp.zeros((SSM_HPG, ln), F32)
        ddt_all = jnp.zeros((ln, SSM_HPG), F32)
        dd_all = jnp.zeros((1, SSM_HPG), F32)
        for r in range(SSM_HPG):
            sl = slice(r * SSM_P, (r + 1) * SSM_P)
            xr, dyr = x_ref[:, sl], dy_ref[:, sl]
            dtc, dr = dt[:, r:r + 1], d_ref[0][:, r:r + 1]
            ac, last = acum[:, r:r + 1], acum[ln - 1:ln, r:r + 1]
            decay = jnp.exp(jnp.where(causal, ac - acumt[r:r + 1, :], -jnp.inf))
            w = (cbm * decay).astype(BF16)
            xdt = xr * dtc
            xdtb, dyb = xdt.astype(BF16), dyr.astype(BF16)
            eac, to_end, elast = jnp.exp(ac), jnp.exp(last - ac), jnp.exp(last)
            sr, dsr = st_ref[0, r], dstate[r]
            srb, dsrb = sr.astype(BF16), dsr.astype(BF16)
            dxdt_state = _dot_nt(bb, dsrb) * to_end
            dxdt = _dot_tn(w, dyb) + dxdt_state
            dcb_r = _dot_nt(dyb, xdtb) * decay
            dcb = dcb + dcb_r
            e = dcb_r * cbm
            dc_acc = dc_acc + _dot(dyb, srb) * eac
            db_acc = db_acc + _dot((xdt * to_end).astype(BF16), dsrb)
            yoff = _dot_nt(cb, srb) * eac
            f_rows = jnp.sum(xdt * dxdt_state, axis=-1, keepdims=True)
            dlast = jnp.sum(f_rows) + elast * jnp.sum(dsr * sr)
            dac = (jnp.sum(e, axis=-1, keepdims=True) + jnp.sum(dyr * yoff, axis=-1, keepdims=True) - f_rows
                   + jnp.where(is_last, dlast, 0.0))
            dac_rows = dac_rows + jnp.where(lane8 == r, dac, 0.0)
            dac_cols = dac_cols + jnp.where(sub8 == r, jnp.sum(e, axis=0, keepdims=True), 0.0)
            ddt_all = ddt_all + jnp.where(lane8 == r, jnp.sum(dxdt * xr, axis=-1, keepdims=True), 0.0)
            dd_all = dd_all + jnp.where(lane1 == r, jnp.sum(dyr * xr), 0.0)
            dx_ref[:, sl] = dxdt * dtc + dr * dyr
            dstate[r] = elast * dsr + _dot_tn((dyr * eac).astype(BF16), cb)
        dcbb = dcb.astype(BF16)
        dc_ref[...] = dc_acc + _dot(dcbb, bb)
        db_ref[...] = db_acc + _dot_tn(dcbb, cb)
        triu = jnp.where(_iota((ln, ln), 0) <= _iota((ln, ln), 1), 1.0, 0.0).astype(BF16)
        g_rows = _xdot(triu, dac_rows)
        g_cols = _dot_x(dac_cols, cm["tri"])
        d_rows = (ddt_all + g_rows * cm["a"]) * _sigmoid(cm["raw"])
        d_cols = -(g_cols * cm["at"]) * _sigmoid(cm["rawt"])
        ddt_ref[0] = d_rows
        ddtt_ref[0] = d_cols
        parts = (jnp.sum(d_rows, axis=0, keepdims=True), jnp.sum(d_cols, axis=1, keepdims=True),
                 jnp.sum(g_rows * dt, axis=0, keepdims=True) * cm["a"],
                 -jnp.sum(g_cols * cm["dtt"], axis=1, keepdims=True) * cm["at"], dd_all)
        outs = (dbias_ref, dbiast_ref, dalog_ref, dalogt_ref, dd_ref)

        @pl.when(step == 0)
        def _():
            for o_ref, p in zip(outs, parts):
                o_ref[0] = p

        @pl.when(step > 0)
        def _():
            for o_ref, p in zip(outs, parts):
                o_ref[0] += p

    xs, bs, cs, dts, dtts, row, col, st = _ssd_specs(nc, True)
    grp = pl.BlockSpec((ln, SSM_STATE), lambda g, c: (nc - 1 - c, g))
    rows = jax.ShapeDtypeStruct((SSM_GROUPS, 1, SSM_HPG), F32)
    cols = jax.ShapeDtypeStruct((SSM_GROUPS, SSM_HPG, 1), F32)
    return pl.pallas_call(
        body, name="ssd_bwd", grid=(SSM_GROUPS, nc),
        in_specs=[xs, bs, cs, dts, dtts, row, col, row, col, row, st, xs],
        out_specs=[xs, grp, grp, dts, dtts, row, col, row, col, row],
        out_shape=[jax.ShapeDtypeStruct((s, SSM_D_INNER), F32),
                   jax.ShapeDtypeStruct((s, SSM_GROUPS * SSM_STATE), F32),
                   jax.ShapeDtypeStruct((s, SSM_GROUPS * SSM_STATE), F32),
                   jax.ShapeDtypeStruct((SSM_GROUPS, s, SSM_HPG), F32),
                   jax.ShapeDtypeStruct((SSM_GROUPS, SSM_HPG, s), F32), rows, cols, rows, cols, rows],
        scratch_shapes=[pltpu.VMEM((SSM_HPG, SSM_P, SSM_STATE), F32)],
        compiler_params=_params("parallel", "arbitrary"),
    )(xbc, xbc, xbc, dt_g, dt_gt, bias_r, bias_c, alog_r, alog_c, d_r, states, dy)


def _gate_norm_fwd(y, zx, g):
    s = y.shape[0]
    ts = _tile(s, 512, 8)

    def body(y_ref, z_ref, g_ref, o_ref):
        zv = z_ref[...]
        yg = y_ref[...] * (zv * _sigmoid(zv))
        r = lax.rsqrt(jnp.mean(yg * yg, axis=-1, keepdims=True) + EPS)
        o_ref[...] = (yg * r * g_ref[...]).astype(BF16)

    blk = pl.BlockSpec((ts, SSM_GN), lambda j, i: (i, j))
    vec = pl.BlockSpec((1, SSM_GN), lambda j, i: (0, j))
    return pl.pallas_call(
        body, name="gate_norm_fwd", grid=(SSM_GROUPS, s // ts), in_specs=[blk, blk, vec], out_specs=blk,
        out_shape=jax.ShapeDtypeStruct((s, SSM_D_INNER), BF16), compiler_params=_params("parallel", "parallel"),
    )(y, zx, g)


def _gate_norm_bwd(y, zx, g, dout):
    s = y.shape[0]
    ts = _tile(s, 512, 8)

    def body(y_ref, z_ref, g_ref, do_ref, dy_ref, dz_ref, dg_ref):
        yv, zv, dov = y_ref[...], z_ref[...], do_ref[...].astype(F32)
        sg = _sigmoid(zv)
        silu = zv * sg
        yg = yv * silu
        r = lax.rsqrt(jnp.mean(yg * yg, axis=-1, keepdims=True) + EPS)
        ygn = yg * r
        part = jnp.sum(dov * ygn, axis=0, keepdims=True)

        @pl.when(pl.program_id(1) == 0)
        def _():
            dg_ref[...] = part

        @pl.when(pl.program_id(1) > 0)
        def _():
            dg_ref[...] += part

        dn = dov * g_ref[...]
        dyg = r * (dn - ygn * jnp.mean(dn * ygn, axis=-1, keepdims=True))
        dy_ref[...] = dyg * silu
        dz_ref[...] = (dyg * yv * sg * (1.0 + zv * (1.0 - sg))).astype(BF16)

    blk = pl.BlockSpec((ts, SSM_GN), lambda j, i: (i, j))
    vec = pl.BlockSpec((1, SSM_GN), lambda j, i: (0, j))
    return pl.pallas_call(
        body, name="gate_norm_bwd", grid=(SSM_GROUPS, s // ts), in_specs=[blk, blk, vec, blk],
        out_specs=[blk, blk, vec],
        out_shape=[jax.ShapeDtypeStruct((s, SSM_D_INNER), F32), jax.ShapeDtypeStruct((s, SSM_IN_PAD), BF16),
                   jax.ShapeDtypeStruct((1, SSM_D_INNER), F32)],
        compiler_params=_params("parallel", "arbitrary"),
    )(y, zx, g, dout)


def _rope_tables(positions):
    inv_freq = ROPE_THETA ** (-jnp.arange(0, ATT_HEAD_DIM, 2, dtype=F32) / ATT_HEAD_DIM)
    ang = positions.astype(F32)[:, None] * inv_freq
    return jnp.tile(jnp.cos(ang), (1, 4)), jnp.tile(jnp.sin(ang), (1, 4))


def _group_views(v):
    return v.reshape(SSM_GROUPS, 1, SSM_HPG), v.reshape(SSM_GROUPS, SSM_HPG, 1)


def _ffn_fwd(run, x, norm_g, wg, wu, wd, tag):
    h = _rms_fwd(x, norm_g, f"ffn_norm_{tag}")
    g = run(f"ffn_gate_{tag}", _mm, h, wg, "nn", b_cols=True)
    u, a = run(f"ffn_up_{tag}", _mm, h, wu, "nn", b_cols=True, rows=FUSED_ROWS,
               fuse=(lambda uv, gv: (uv, gv * _sigmoid(gv) * uv), [g], [F32, BF16]))
    return run(f"ffn_down_{tag}", _mm, a, wd, "nn", add=x), (h, g, u, a)


def _ffn_bwd(run, mats, x, norm_g, wg, wu, wd, saved, dout, tag):
    h, g, u, a = saved

    def act_bwd(da, gv, uv):
        sg = _sigmoid(gv)
        return da * uv * sg * (1.0 + gv * (1.0 - sg)), da * (gv * sg)

    dg, du = run(f"ffn_down_dx_{tag}", _mm, dout, wd, "nt", rows=FUSED_ROWS,
                 fuse=(act_bwd, [g, u], [BF16, BF16]))
    dwd = run(f"ffn_down_dw_{tag}", _mm, a, dout, "tn", out_dtype=BF16)
    mats[("ffn_w_down", tag)] = dwd.reshape(N_SHARDS, dwd.shape[0] // N_SHARDS, dwd.shape[1])
    mats[("ffn_w_gate", tag)] = run(f"ffn_gate_dw_{tag}", _mm, h, dg, "tn", out_dtype=BF16, out_cols=True)
    mats[("ffn_w_up", tag)] = run(f"ffn_up_dw_{tag}", _mm, h, du, "tn", out_dtype=BF16, out_cols=True)
    dh = run(f"ffn_gate_dx_{tag}", _mm, dg, wg, "nt", b_cols=True)
    dh = run(f"ffn_up_dx_{tag}", _mm, du, wu, "nt", add=dh, b_cols=True)
    return _rms_bwd(x, norm_g, dh, dout, f"ffn_norm_bwd_{tag}")


class _Hook:
    def __init__(self, make, done):
        self.make, self.done = make, done


def _local_step(x, positions, target, w, hooks=None, mats=None):
    hooks = {} if hooks is None else hooks
    mats = {} if mats is None else mats

    def run(name, fn, *args, **kw):
        hook = hooks.get(name)
        if hook is None:
            return fn(*args, name=name, **kw)
        res, carried = fn(*args, name=name, carry=hook.make(), **kw)
        hook.done(carried)
        return res

    cos, sin = _rope_tables(positions)
    row = lambda v: v.reshape(1, -1)
    gq, gk = jnp.tile(row(w["attn_q_norm"]), (1, 2)), jnp.tile(row(w["attn_k_norm"]), (1, 2))
    sinks = w["attn_sinks"].reshape(-1)
    s = x.shape[0]
    row_stack = lambda g: g.reshape(N_SHARDS, g.shape[0] // N_SHARDS, g.shape[1])

    h0 = _rms_fwd(x, row(w["mixer_norm"][0]), "mixer_norm_0")
    qkv = run("attn_qkv", _mm, h0, w["attn_w_qkv"], "nn", b_cols=True)
    q, kk, vlo, vhi = _attn_prep(qkv, cos, sin, gq, gk)
    o = run("attn_fwd", _attn_fwd, q, kk, vlo, vhi, sinks)
    x1 = run("attn_out", _mm, o, w["attn_w_o"], "nn", add=x)
    ffn_w = lambda l: (row(w["ffn_norm"][l]), w["ffn_w_gate"][l], w["ffn_w_up"][l], w["ffn_w_down"][l])
    x2, ffn0 = _ffn_fwd(run, x1, *ffn_w(0), 0)

    h2 = _rms_fwd(x2, row(w["mixer_norm"][1]), "mixer_norm_1")
    zx = run("ssm_in", _mm, h2, w["ssm_w_in"], "nn")
    dt_g = zx[:, DT_OFF:DT_OFF + SSM_HEADS].reshape(s, SSM_GROUPS, SSM_HPG).transpose(1, 0, 2)
    dt_gt = dt_g.transpose(0, 2, 1)
    bias_r, bias_c = _group_views(w["ssm_dt_bias"].reshape(-1))
    alog_r, alog_c = _group_views(w["ssm_a_log"].reshape(-1))
    d_r, _ = _group_views(w["ssm_d"].reshape(-1))
    xbc = _conv_fwd(zx, w["ssm_conv_w"], row(w["ssm_conv_b"]))
    ssd_args = (xbc, dt_g, dt_gt, bias_r, bias_c, alog_r, alog_c, d_r)
    y, states = _ssd_fwd(*ssd_args)
    yn = _gate_norm_fwd(y, zx, row(w["ssm_norm"]))
    x3 = run("ssm_out", _mm, yn, w["ssm_w_out"], "nn", add=x2)
    x4, ffn1 = _ffn_fwd(run, x3, *ffn_w(1), 1)

    loss_row, dx4 = _loss_fwd_bwd(x4, target)

    dx3, dfn1 = _ffn_bwd(run, mats, x3, *ffn_w(1), ffn1, dx4, 1)
    dyn = run("ssm_out_dx", _mm, dx3, w["ssm_w_out"], "nt")
    mats[("ssm_w_out", 0)] = row_stack(run("ssm_out_dw", _mm, yn, dx3, "tn", out_dtype=BF16))
    dy, dzx, dssm_norm = _gate_norm_bwd(y, zx, row(w["ssm_norm"]), dyn)
    dxs, db, dc, ddt_g, ddt_gt, dbias, dbias_t, dalog, dalog_t, dd = _ssd_bwd(*ssd_args, states, dy)
    ddt_g = ddt_g + ddt_gt.transpose(0, 2, 1)
    dpre, dconv_w, dconv_b = _conv_bwd_pre(zx, w["ssm_conv_w"], row(w["ssm_conv_b"]),
                                           jnp.concatenate([dxs, db, dc], axis=1))
    dzx = _conv_bwd_in(dpre, w["ssm_conv_w"], dzx)
    ddt_pad = jnp.pad(ddt_g.transpose(1, 0, 2).reshape(s, SSM_HEADS), ((0, 0), (0, SSM_IN_PAD - SSM_IN)))
    dzx = lax.dynamic_update_slice(dzx, ddt_pad.astype(BF16), (0, DT_OFF))
    dw_in = run("ssm_in_dw", _mm, h2, dzx, "tn", out_dtype=BF16)
    in_shard = SSM_IN // N_SHARDS
    mats[("ssm_w_in", 0)] = jnp.stack([dw_in[:, i * in_shard:(i + 1) * in_shard] for i in range(N_SHARDS)])
    dh2 = run("ssm_in_dx", _mm, dzx, w["ssm_w_in"], "nt")
    dx2, dmn1 = _rms_bwd(x2, row(w["mixer_norm"][1]), dh2, dx3, "mixer_norm_bwd_1")

    dx1, dfn0 = _ffn_bwd(run, mats, x1, *ffn_w(0), ffn0, dx2, 0)
    do = run("attn_out_dx", _mm, dx1, w["attn_w_o"], "nt", out_dtype=BF16)
    mats[("attn_w_o", 0)] = row_stack(run("attn_out_dw", _mm, o, dx1, "tn", out_dtype=BF16))
    dq, dkc, dkp, dvloc, dvlop, dvhic, dvhip, dsink = run("attn_bwd", _attn_bwd, q, kk, vlo, vhi, sinks, do)
    dqkv, dgq, dgk = _attn_prep_bwd(qkv, cos, sin, gq, gk, dq, (dkc, dkp), (dvloc, dvlop), (dvhic, dvhip))
    mats[("attn_w_qkv", 0)] = run("attn_qkv_dw", _mm, h0, dqkv, "tn", out_dtype=BF16, out_cols=True)
    dh0 = run("attn_qkv_dx", _mm, dqkv, w["attn_w_qkv"], "nt", b_cols=True)
    dx0, dmn0 = _rms_bwd(x, row(w["mixer_norm"][0]), dh0, dx1, "mixer_norm_bwd_0")

    fold = lambda v: v[0, :ATT_HEAD_DIM] + v[0, ATT_HEAD_DIM:]
    grads = {
        "mixer_norm": jnp.concatenate([dmn0, dmn1], axis=0),
        "ffn_norm": jnp.concatenate([dfn0, dfn1], axis=0),
        "attn_q_norm": fold(dgq), "attn_k_norm": fold(dgk),
        "attn_sinks": dsink[:, :, 0].reshape(-1),
        "ssm_conv_w": dconv_w, "ssm_conv_b": dconv_b.reshape(-1),
        "ssm_dt_bias": dbias.reshape(-1) + dbias_t.reshape(-1),
        "ssm_a_log": dalog.reshape(-1) + dalog_t.reshape(-1), "ssm_d": dd.reshape(-1),
        "ssm_norm": dssm_norm.reshape(-1),
    }
    return loss_row[0, 0], dx0, grads


OTHER_CHIPS = ((1, 0), (0, 1), (1, 1))


def _position():
    return lax.axis_index("x"), lax.axis_index("y"), lax.axis_index("c")


def _sems(n):
    return pltpu.SemaphoreType.DMA((n,))


def _gather_shards(weights, layers):
    n_in, n_mat = len(weights), len(layers)

    def body(*refs):
        p, out = refs[:n_in], refs[n_in:n_in + n_mat]
        send_sems, recv_sems = refs[n_in + n_mat:]
        x, y, c = _position()
        me, sibling = (x, y, c), (x, y, 1 - c)
        chips = [(x ^ fx, y ^ fy) for fx, fy in OTHER_CHIPS]

        def rows(e, px, py, pc):
            half = out[e].shape[1] // 2
            return out[e].at[2 * px + py, pl.ds(pc * half, half), :]

        def copy(k, e, block, to, src=None):
            return pltpu.make_async_remote_copy(
                src_ref=rows(e, *block) if src is None else src, dst_ref=rows(e, *block),
                send_sem=send_sems.at[k * n_mat + e], recv_sem=recv_sems.at[k * n_mat + e],
                device_id=to, device_id_type=MESH)

        def own(e):
            i, l = layers[e]
            return pltpu.make_async_remote_copy(
                src_ref=p[i].at[l], dst_ref=out[e].at[2 * x + y], send_sem=send_sems.at[6 * n_mat + e],
                recv_sem=recv_sems.at[6 * n_mat + e], device_id=sibling, device_id_type=MESH)

        first, passed = [], []
        for e, (i, l) in enumerate(layers):
            half = out[e].shape[1] // 2
            first.append([copy(j, e, me, (*chip, c), src=p[i].at[l, pl.ds(c * half, half), :])
                          for j, chip in enumerate(chips)])
            for cp in first[-1]:
                cp.start()
        for e in range(n_mat):
            own(e).start()
        for e in range(n_mat):
            passed.append([copy(3 + j, e, (*chip, c), sibling) for j, chip in enumerate(chips)])
            for j, chip in enumerate(chips):
                copy(j, e, (*chip, c), me).wait_recv()
                passed[e][j].start()
        for e in range(n_mat):
            own(e).wait()
            for j, chip in enumerate(chips):
                copy(3 + j, e, (*chip, 1 - c), me).wait_recv()
        for e in range(n_mat):
            for cp in first[e] + passed[e]:
                cp.wait_send()

    return pl.pallas_call(
        body, name="gather_weights", in_specs=[ANY] * n_in, out_specs=[ANY] * n_mat,
        out_shape=[jax.ShapeDtypeStruct((N_SHARDS,) + weights[i].shape[1:], weights[i].dtype) for i, _ in layers],
        scratch_shapes=[_sems(7 * n_mat), _sems(7 * n_mat)],
    )(*weights)


def _all_gather8(block, name):
    m_per, n = block.shape

    def body(x_ref, out_ref, send_sems, recv_sems, local_sem):
        x, y, c = _position()
        me, sibling = (x, y, c), (x, y, 1 - c)
        chips = [(x ^ fx, y ^ fy) for fx, fy in OTHER_CHIPS]

        def rows(px, py, pc):
            return out_ref.at[pl.ds((4 * px + 2 * py + pc) * m_per, m_per), :]

        def copy(k, blk, to, src=None):
            return pltpu.make_async_remote_copy(
                src_ref=rows(*blk) if src is None else src, dst_ref=rows(*blk),
                send_sem=send_sems.at[k], recv_sem=recv_sems.at[k], device_id=to, device_id_type=MESH)

        mine = pltpu.make_async_copy(x_ref, rows(*me), local_sem)
        mine.start()
        first = [copy(0, me, sibling, src=x_ref)]
        first += [copy(1 + j, me, (*chip, c), src=x_ref) for j, chip in enumerate(chips)]
        for cp in first:
            cp.start()
        passed = [copy(4 + j, (*chip, c), sibling) for j, chip in enumerate(chips)]
        for j, chip in enumerate(chips):
            copy(1 + j, (*chip, c), me).wait_recv()
            passed[j].start()
        copy(0, sibling, me).wait_recv()
        for j, chip in enumerate(chips):
            copy(4 + j, (*chip, 1 - c), me).wait_recv()
        for cp in first + passed:
            cp.wait_send()
        mine.wait()

    return pl.pallas_call(
        body, name=name, out_shape=jax.ShapeDtypeStruct((N_DEV * m_per, n), block.dtype),
        in_specs=[pl.BlockSpec(memory_space=pltpu.VMEM)], out_specs=pl.BlockSpec(memory_space=pltpu.VMEM),
        scratch_shapes=[_sems(7), _sems(7), pltpu.SemaphoreType.DMA],
    )(block)


def _exchange(carry, name):
    n_in, n_out = len(carry["arrays"]), len(carry["out_shapes"])

    def body(*refs):
        copies = carry["build"](refs[:n_in], refs[n_in:n_in + n_out], refs[-2], refs[-1])
        for cp in copies:
            cp.start()
        for cp in copies:
            cp.wait()

    return pl.pallas_call(
        body, name=name, in_specs=[ANY] * n_in, out_specs=[ANY] * n_out, out_shape=list(carry["out_shapes"]),
        input_output_aliases=dict(carry.get("aliases", {})),
        scratch_shapes=[_sems(carry["n_sems"]), _sems(carry["n_sems"])],
    )(*carry["arrays"])


def _remote(src, dst, send_sems, recv_sems, k, to):
    return pltpu.make_async_remote_copy(src_ref=src, dst_ref=dst, send_sem=send_sems.at[k], recv_sem=recv_sems.at[k],
                                        device_id=to, device_id_type=MESH)


def _gather_over_ici(blocks, layers):
    def build(p, out, send_sems, recv_sems):
        x, y, c = _position()
        copies = []
        for e, l in enumerate(layers):
            half = out[e].shape[1] // 2
            rows = pl.ds(c * half, half)
            for j, (fx, fy) in enumerate(OTHER_CHIPS):
                copies.append(_remote(p[e].at[l, rows, :], out[e].at[2 * x + y, rows, :], send_sems, recv_sems,
                                      3 * e + j, (x ^ fx, y ^ fy, c)))
        return copies

    shapes = [jax.ShapeDtypeStruct((N_SHARDS,) + b.shape[1:], b.dtype) for b in blocks]
    return dict(build=build, arrays=list(blocks), out_shapes=shapes, n_sems=3 * len(layers))


def _gather_over_d2d(stacks, blocks, layers):
    n = len(stacks)

    def build(refs, out, send_sems, recv_sems):
        p = refs[n:]
        x, y, c = _position()
        sibling = (x, y, 1 - c)
        copies = []
        for e, l in enumerate(layers):
            half = out[e].shape[1] // 2
            for j, (fx, fy) in enumerate(OTHER_CHIPS):
                rows = out[e].at[2 * (x ^ fx) + (y ^ fy), pl.ds(c * half, half), :]
                copies.append(_remote(rows, rows, send_sems, recv_sems, 4 * e + j, sibling))
            copies.append(_remote(p[e].at[l], out[e].at[2 * x + y], send_sems, recv_sems, 4 * e + 3, sibling))
        return copies

    shapes = [jax.ShapeDtypeStruct(s.shape, s.dtype) for s in stacks]
    return dict(build=build, arrays=list(stacks) + list(blocks), out_shapes=shapes, n_sems=4 * n,
                aliases={i: i for i in range(n)})


def _grads_to_sibling(stacks):
    def build(g, out, send_sems, recv_sems):
        x, y, c = _position()
        copies = []
        for e in range(len(stacks)):
            half = g[e].shape[1] // 2
            copies.append(_remote(g[e].at[:, pl.ds((1 - c) * half, half), :], out[e], send_sems, recv_sems, e,
                                  (x, y, 1 - c)))
        return copies

    shapes = [jax.ShapeDtypeStruct((N_SHARDS, g.shape[1] // 2, g.shape[2]), g.dtype) for g in stacks]
    return dict(build=build, arrays=list(stacks), out_shapes=shapes, n_sems=len(stacks))


def _grads_to_owners(partials):
    def build(p, out, send_sems, recv_sems):
        x, y, c = _position()
        copies = []
        for e in range(len(partials)):
            for k, (fx, fy) in enumerate(OTHER_CHIPS):
                px, py = x ^ fx, y ^ fy
                copies.append(_remote(p[e].at[2 * px + py], out[e].at[k], send_sems, recv_sems, 3 * e + k,
                                      (px, py, c)))
        return copies

    shapes = [jax.ShapeDtypeStruct((len(OTHER_CHIPS),) + p.shape[1:], p.dtype) for p in partials]
    return dict(build=build, arrays=list(partials), out_shapes=shapes, n_sems=3 * len(partials))


def _share_halves(grads, layers):
    n = len(grads)

    def body(*refs):
        out, send_sems, recv_sems = refs[n:2 * n], refs[-2], refs[-1]
        x, y, c = _position()
        copies = []
        for e, (i, l) in enumerate(layers):
            half = out[i].shape[1] // 2
            rows = out[i].at[l, pl.ds(c * half, half), :]
            copies.append(pltpu.make_async_remote_copy(
                src_ref=rows, dst_ref=rows, send_sem=send_sems.at[e], recv_sem=recv_sems.at[e],
                device_id=(x, y, 1 - c), device_id_type=MESH))
            copies[-1].start()
        for cp in copies:
            cp.wait()

    return pl.pallas_call(
        body, name="grads_share_halves", in_specs=[ANY] * n, out_specs=[ANY] * n,
        out_shape=[jax.ShapeDtypeStruct(g.shape, g.dtype) for g in grads],
        input_output_aliases={i: i for i in range(n)},
        scratch_shapes=[_sems(len(layers)), _sems(len(layers))],
    )(*grads)


ADD_BLOCK_ELEMS = 1 << 19


def _add_rows(half, cols):
    return _tile(half, max(16, ADD_BLOCK_ELEMS // cols // 16 * 16), 16)


def _add_pair(stack, recv, c_idx, name):
    _, half, cols = recv.shape
    tr = _add_rows(half, cols)
    nt = half // tr

    def body(c_ref, a_ref, b_ref, o_ref):
        o_ref[...] = (a_ref[...].astype(F32) + b_ref[...].astype(F32)).astype(o_ref.dtype)

    blk = pl.BlockSpec((None, tr, cols), lambda s, i, c_ref: (s, i, 0))
    return pl.pallas_call(
        body, name=name,
        grid_spec=pltpu.PrefetchScalarGridSpec(
            num_scalar_prefetch=1, grid=(N_SHARDS, nt),
            in_specs=[pl.BlockSpec((None, tr, cols), lambda s, i, c_ref: (s, c_ref[0] * nt + i, 0)), blk],
            out_specs=blk),
        out_shape=jax.ShapeDtypeStruct(recv.shape, recv.dtype),
        compiler_params=_params("parallel", "parallel"),
    )(c_idx, stack, recv)


def _add_owned(partial, recv, sc_idx, layer, shape, into, name):
    _, half, cols = partial.shape
    tr = _add_rows(half, cols)
    nt = half // tr

    def body(sc_ref, a_ref, r0_ref, r1_ref, r2_ref, *rest):
        o_ref = rest[-1]
        o_ref[...] = (((a_ref[...].astype(F32) + r0_ref[...].astype(F32)) + r1_ref[...].astype(F32))
                      + r2_ref[...].astype(F32))

    slot = lambda k: pl.BlockSpec((None, tr, cols), lambda i, sc_ref: (k, i, 0))
    has_into = into is not None
    return pl.pallas_call(
        body, name=name,
        grid_spec=pltpu.PrefetchScalarGridSpec(
            num_scalar_prefetch=1, grid=(nt,),
            in_specs=[pl.BlockSpec((None, tr, cols), lambda i, sc_ref: (sc_ref[0], i, 0)), slot(0), slot(1), slot(2)]
            + ([ANY] if has_into else []),
            out_specs=pl.BlockSpec((None, tr, cols), lambda i, sc_ref: (layer, sc_ref[1] * nt + i, 0))),
        out_shape=jax.ShapeDtypeStruct(shape, F32),
        input_output_aliases={5: 0} if has_into else {},
        compiler_params=_params("parallel"),
    )(*((sc_idx, partial, recv, recv, recv) + ((into,) if has_into else ())))


def _sum8(gathered):
    m = gathered.shape[0] // N_DEV

    def body(g_ref, o_ref):
        total = g_ref[0:m, :]
        for d in range(1, N_DEV):
            total = total + g_ref[d * m:(d + 1) * m, :]
        o_ref[...] = total

    return pl.pallas_call(
        body, name="small_grads_sum", out_shape=jax.ShapeDtypeStruct((m, LANES), F32),
        in_specs=[pl.BlockSpec(memory_space=pltpu.VMEM)], out_specs=pl.BlockSpec(memory_space=pltpu.VMEM),
    )(gathered)


ADAMW_BLOCK_ELEMS = 1 << 18


def _adamw(w, g, m, v, name):
    l, r, cols = w.shape
    tr = _tile(r, max(8, ADAMW_BLOCK_ELEMS // cols // 8 * 8), 8)

    def body(w_ref, g_ref, m_ref, v_ref, d_ref, nm_ref, nv_ref):
        gv = g_ref[...]
        nm = ADAM_B1 * m_ref[...] + (1.0 - ADAM_B1) * gv
        nv = ADAM_B2 * v_ref[...] + (1.0 - ADAM_B2) * jnp.square(gv)
        m_hat = nm / (1.0 - ADAM_B1 ** ADAM_STEP)
        v_hat = nv / (1.0 - ADAM_B2 ** ADAM_STEP)
        d_ref[...] = -ADAM_LR * (m_hat / (jnp.sqrt(v_hat) + ADAM_EPS) + ADAM_WD * w_ref[...])
        nm_ref[...] = nm
        nv_ref[...] = nv

    blk = pl.BlockSpec((None, tr, cols), lambda a, i: (a, i, 0))
    return pl.pallas_call(
        body, name=name, grid=(l, r // tr), in_specs=[blk] * 4, out_specs=[blk] * 3,
        out_shape=[jax.ShapeDtypeStruct(w.shape, F32)] * 3, compiler_params=_params("parallel", "parallel"),
    )(w, g, m, v)


WEIGHTS = ("mixer_norm", "ffn_norm", "attn_w_qkv", "attn_q_norm", "attn_k_norm", "attn_sinks", "attn_w_o",
           "ssm_w_in", "ssm_conv_w", "ssm_conv_b", "ssm_dt_bias", "ssm_a_log", "ssm_d", "ssm_norm", "ssm_w_out",
           "ffn_w_gate", "ffn_w_up", "ffn_w_down")
BIG = ("attn_w_qkv", "attn_w_o", "ffn_w_gate", "ffn_w_up", "ffn_w_down", "ssm_w_in", "ssm_w_out")
MATRICES = (("attn_w_qkv", 0), ("attn_w_o", 0), ("ffn_w_gate", 0), ("ffn_w_up", 0), ("ffn_w_down", 0),
            ("ssm_w_in", 0), ("ssm_w_out", 0), ("ffn_w_gate", 1), ("ffn_w_up", 1), ("ffn_w_down", 1))
MATRIX_LAYERS = tuple((BIG.index(n), l) for n, l in MATRICES)
GROUPS = {"attn": MATRICES[0:2], "ffn0": MATRICES[2:5], "ssm": MATRICES[5:7], "ffn1": MATRICES[7:10]}
SMALL_SHARDED = ("ssm_conv_w", "ssm_conv_b", "ssm_norm")
SMALL = tuple(n for n in WEIGHTS if n not in BIG)


def _pack_rows(parts, row_unit=8):
    flat = jnp.concatenate([p.reshape(-1) for p in parts])
    pad = (-flat.shape[0]) % (LANES * row_unit)
    return jnp.pad(flat, (0, pad)).reshape(-1, LANES)


def _unpack(flat, shapes):
    out, off = [], 0
    for shp in shapes:
        size = math.prod(shp)
        out.append(flat[off:off + size].reshape(shp))
        off += size
    return out


def kernel(x, positions, mixer_norm, ffn_norm, attn_w_qkv, attn_q_norm, attn_k_norm, attn_sinks, attn_w_o, ssm_w_in, ssm_conv_w, ssm_conv_b, ssm_dt_bias, ssm_a_log, ssm_d, ssm_norm, ssm_w_out, ffn_w_gate, ffn_w_up, ffn_w_down, loss_target, m_mixer_norm, m_ffn_norm, m_attn_w_qkv, m_attn_q_norm, m_attn_k_norm, m_attn_sinks, m_attn_w_o, m_ssm_w_in, m_ssm_conv_w, m_ssm_conv_b, m_ssm_dt_bias, m_ssm_a_log, m_ssm_d, m_ssm_norm, m_ssm_w_out, m_ffn_w_gate, m_ffn_w_up, m_ffn_w_down, v_mixer_norm, v_ffn_norm, v_attn_w_qkv, v_attn_q_norm, v_attn_k_norm, v_attn_sinks, v_attn_w_o, v_ssm_w_in, v_ssm_conv_w, v_ssm_conv_b, v_ssm_dt_bias, v_ssm_a_log, v_ssm_d, v_ssm_norm, v_ssm_w_out, v_ffn_w_gate, v_ffn_w_up, v_ffn_w_down):
    args = locals()
    w = {n: args[n] for n in WEIGHTS}
    m = {n: args["m_" + n] for n in WEIGHTS}
    v = {n: args["v_" + n] for n in WEIGHTS}
    ax, ay, ac = lax.axis_index("x"), lax.axis_index("y"), lax.axis_index("c")
    shard = 2 * ax + ay

    wb = {n: w[n].astype(BF16) for n in BIG}
    wl, hooks = {"ffn_w_gate": [None, None], "ffn_w_up": [None, None], "ffn_w_down": [None, None]}, {}

    def gathered(keys, stacks):
        for (n, l), st in zip(keys, stacks):
            if n == "ssm_w_in":
                wl[n] = jnp.concatenate([st[i] for i in range(N_SHARDS)]
                                        + [jnp.zeros((st.shape[1], SSM_IN_PAD - SSM_IN), BF16)], axis=1)
            elif n in ("ffn_w_gate", "ffn_w_up"):
                wl[n][l] = st
            elif n == "ffn_w_down":
                wl[n][l] = st.reshape(st.shape[0] * st.shape[1], st.shape[2])
            elif n == "attn_w_qkv":
                wl[n] = st
            else:
                wl[n] = st.reshape(st.shape[0] * st.shape[1], st.shape[2])

    def gather_behind(keys, first_leg, second_leg):
        blocks, layers, got = [wb[n] for n, _ in keys], [l for _, l in keys], {}
        hooks[first_leg] = _Hook(lambda: _gather_over_ici(blocks, layers), lambda res: got.update(stacks=res))
        hooks[second_leg] = _Hook(lambda: _gather_over_d2d(got["stacks"], blocks, layers),
                                  lambda res: gathered(keys, res))

    gathered(GROUPS["attn"], _gather_shards([wb[n] for n, _ in GROUPS["attn"]],
                                            [(e, l) for e, (_, l) in enumerate(GROUPS["attn"])]))
    gather_behind(GROUPS["ffn0"], "attn_fwd", "attn_out")
    gather_behind(GROUPS["ssm"], "ffn_gate_0", "ffn_up_0")
    gather_behind(GROUPS["ffn1"], "ssm_in", "ssm_out")
    small_shapes = [w[n].shape for n in SMALL_SHARDED]
    small_all = _all_gather8(_pack_rows([w[n] for n in SMALL_SHARDED]), "gather_small_params")
    small_all = small_all.reshape(N_DEV, -1)[::2]
    full, off = {}, 0
    for n, shp in zip(SMALL_SHARDED, small_shapes):
        size = math.prod(shp)
        seg = small_all[:, off:off + size].reshape((N_SHARDS,) + shp)
        full[n] = jnp.moveaxis(seg, 0, -2).reshape(shp[:-1] + (N_SHARDS * shp[-1],))
        off += size
    wl.update({
        "mixer_norm": mixer_norm, "ffn_norm": ffn_norm,
        "attn_q_norm": attn_q_norm[0], "attn_k_norm": attn_k_norm[0], "attn_sinks": attn_sinks[0],
        "ssm_conv_w": full["ssm_conv_w"][0], "ssm_conv_b": full["ssm_conv_b"][0],
        "ssm_dt_bias": ssm_dt_bias[0], "ssm_a_log": ssm_a_log[0], "ssm_d": ssm_d[0],
        "ssm_norm": full["ssm_norm"][0],
    })

    c_idx = ac.reshape(1).astype(jnp.int32)
    sc_idx = jnp.stack([shard, ac]).astype(jnp.int32)
    mats, halves = {}, {n: None for n in BIG}

    def pair_sums(keys, recv):
        return [_add_pair(mats[k], r, c_idx, f"grads_add_pair_{k[0]}_{k[1]}") for k, r in zip(keys, recv)]

    def owner_sums(keys, partials, recv):
        for (n, l), p, r in zip(keys, partials, recv):
            halves[n] = _add_owned(p, r, sc_idx, l, w[n].shape, halves[n], f"grads_add_owned_{n}_{l}")

    def reduce_behind(keys, first_leg, second_leg):
        got = {}
        hooks[first_leg] = _Hook(lambda: _grads_to_sibling([mats[k] for k in keys]),
                                 lambda res: got.update(partials=pair_sums(keys, res)))
        hooks[second_leg] = _Hook(lambda: _grads_to_owners(got["partials"]),
                                  lambda res: owner_sums(keys, got["partials"], res))

    reduce_behind(GROUPS["ffn1"], "ssm_out_dx", "ssm_in_dw")
    reduce_behind(GROUPS["ssm"], "ssm_in_dx", "ffn_down_dx_0")
    reduce_behind(GROUPS["ffn0"], "attn_out_dx", "attn_bwd")
    loss_part, dx, g_full = _local_step(x[0], positions[0], loss_target[0], wl, hooks, mats)
    keys = GROUPS["attn"]
    partials = pair_sums(keys, _exchange(_grads_to_sibling([mats[k] for k in keys]), "grads_to_sibling"))
    owner_sums(keys, partials, _exchange(_grads_to_owners(partials), "grads_to_owners"))
    grads = dict(zip(BIG, _share_halves([halves[n] for n in BIG], MATRIX_LAYERS)))

    small_full_shapes = [g_full[n].shape for n in SMALL] + [(1,)]
    small_g = _pack_rows([g_full[n] for n in SMALL] + [loss_part.reshape(1)])
    small_sum = _sum8(_all_gather8(small_g, "gather_small_grads")).reshape(-1)
    *small_list, loss = _unpack(small_sum, small_full_shapes)
    for n, g in zip(SMALL, small_list):
        if n in SMALL_SHARDED:
            width = w[n].shape[-1]
            g = lax.dynamic_slice_in_dim(g, shard * width, width, axis=g.ndim - 1)
        grads[n] = g.reshape(w[n].shape)

    delta, new_m, new_v = {}, {}, {}
    for n in BIG:
        delta[n], new_m[n], new_v[n] = _adamw(w[n], grads[n], m[n], v[n], "adamw_" + n)
    small_local = [w[n].shape for n in SMALL]
    pk = lambda t: _pack_rows([t[n] for n in SMALL])[None]
    outs = _adamw(pk(w), pk(grads), pk(m), pk(v), "adamw_small")
    for res, o in zip((delta, new_m, new_v), outs):
        for n, a in zip(SMALL, _unpack(o.reshape(-1), small_local)):
            res[n] = a

    return (loss.reshape(()), dx[None], *[grads[n] for n in WEIGHTS], *[delta[n] for n in WEIGHTS],
            *[new_m[n] for n in WEIGHTS], *[new_v[n] for n in WEIGHTS])
```

```python
import math

import jax
import jax.numpy as jnp
from jax import lax
from jax.experimental import pallas as pl
from jax.experimental.pallas import tpu as pltpu

F32 = jnp.float32
BF16 = jnp.bfloat16

D_MODEL = 2048
EPS = 1e-6
ATT_HEAD_DIM = 64
ATT_Q_HEADS = 32
ATT_KV_HEADS = 4
ATT_GROUP = 8
ATT_BLOCK = 128
ROPE_THETA = 10000.0
Q_WIDTH = ATT_Q_HEADS * ATT_HEAD_DIM
KV_WIDTH = ATT_KV_HEADS * ATT_HEAD_DIM
SSM_D_INNER = 4096
SSM_HEADS = 64
SSM_GROUPS = 8
SSM_HPG = 8
SSM_P = 64
SSM_STATE = 128
SSM_CONV = 4
SSM_CHUNK = 256
SSM_CONV_DIM = 6144
SSM_GN = SSM_D_INNER // SSM_GROUPS
SSM_IN = SSM_D_INNER + SSM_CONV_DIM + SSM_HEADS
LANES = 128
SSM_IN_PAD = -(-SSM_IN // LANES) * LANES
N_SHARDS = 4
N_DEV = 8

ADAM_LR = 0.001
ADAM_B1 = 0.9
ADAM_B2 = 0.999
ADAM_EPS = 1e-08
ADAM_WD = 0.01
ADAM_STEP = 10

VMEM_LIMIT = 56 * 1024 * 1024
MESH = pl.DeviceIdType.MESH
ANY = pl.BlockSpec(memory_space=pl.ANY)


def _params(*sem):
    return pltpu.CompilerParams(dimension_semantics=sem, vmem_limit_bytes=VMEM_LIMIT)


def _sems(n):
    return pltpu.SemaphoreType.DMA((n,))


def _call(body, carry, name, grid, in_specs, out_specs, out_shape, scratch_shapes, sem, args):
    if carry is None:
        return pl.pallas_call(body, name=name, grid=grid, in_specs=in_specs, out_specs=out_specs,
                              out_shape=out_shape, scratch_shapes=scratch_shapes,
                              compiler_params=_params(*sem))(*args)
    n_in, n_out, n_scr = len(in_specs), len(out_specs), len(scratch_shapes)
    c_arrays, c_shapes = list(carry["arrays"]), list(carry["out_shapes"])
    n_cin, n_cout = len(c_arrays), len(c_shapes)

    def carrying(*refs):
        ins, refs = refs[:n_in], refs[n_in:]
        cin, refs = refs[:n_cin], refs[n_cin:]
        outs, refs = refs[:n_out], refs[n_out:]
        cout, refs = refs[:n_cout], refs[n_cout:]
        scratch, (send_sems, recv_sems) = refs[:n_scr], refs[n_scr:]
        copies = carry["build"](cin, cout, send_sems, recv_sems)
        ids = [pl.program_id(d) for d in range(len(grid))]
        first, last = ids[0] == 0, ids[0] == grid[0] - 1
        for d in range(1, len(grid)):
            first = jnp.logical_and(first, ids[d] == 0)
            last = jnp.logical_and(last, ids[d] == grid[d] - 1)

        @pl.when(first)
        def _():
            for cp in copies:
                cp.start()

        body(*ins, *outs, *scratch)

        @pl.when(last)
        def _():
            for cp in copies:
                cp.wait()

    aliases = {n_in + i: n_out + o for i, o in carry.get("aliases", {}).items()}
    return pl.pallas_call(
        carrying, name=name, grid=grid, in_specs=list(in_specs) + [ANY] * n_cin,
        out_specs=list(out_specs) + [ANY] * n_cout, out_shape=list(out_shape) + c_shapes,
        scratch_shapes=list(scratch_shapes) + [_sems(carry["n_sems"]), _sems(carry["n_sems"])],
        input_output_aliases=aliases, compiler_params=_params(*(["arbitrary"] * len(grid))))(*args, *c_arrays)


def _tile(dim, target, unit=LANES):
    if dim <= target:
        return dim
    t = (target // unit) * unit
    while t >= unit:
        if dim % t == 0:
            return t
        t -= unit
    return dim


def _dot(a, b):
    return lax.dot_general(a, b, (((1,), (0,)), ((), ())), preferred_element_type=F32)


def _dot_nt(a, b):
    return lax.dot_general(a, b, (((1,), (1,)), ((), ())), preferred_element_type=F32)


def _dot_tn(a, b):
    return lax.dot_general(a, b, (((0,), (0,)), ((), ())), preferred_element_type=F32)


def _split3(x):
    hi = x.astype(BF16)
    r1 = x - hi.astype(F32)
    mid = r1.astype(BF16)
    lo = (r1 - mid.astype(F32)).astype(BF16)
    return hi, mid, lo


def _dot_x(x, m):
    hi, mid, lo = _split3(x)
    return _dot(hi, m) + _dot(mid, m) + _dot(lo, m)


def _xdot(m, x):
    hi, mid, lo = _split3(x)
    return _dot(m, hi) + _dot(m, mid) + _dot(m, lo)


def _dot_x_nt(x, m):
    hi, mid, lo = _split3(x)
    return _dot_nt(hi, m) + _dot_nt(mid, m) + _dot_nt(lo, m)


def _iota(shape, dim):
    return lax.broadcasted_iota(jnp.int32, shape, dim)


def _sigmoid(x):
    return 1.0 / (1.0 + jnp.exp(-x))


def _softplus(x):
    return jnp.maximum(x, 0.0) + jnp.log(1.0 + jnp.exp(-jnp.abs(x)))


MM_ROWS = 1024
MM_TILE = 1408
MM_DEPTH = 2816
FUSED_ROWS = 512


def _mm(a, b, mode, name, add=None, out_dtype=F32, b_cols=False, out_cols=False, fuse=None, rows=MM_ROWS,
        carry=None):
    bs = b.shape[-2:]
    if b_cols:
        bs = (bs[0], N_SHARDS * bs[1])
    if mode == "nn":
        (m, k), (k2, n) = a.shape, bs
    elif mode == "nt":
        (m, k), (n, k2) = a.shape, bs
    else:
        (k, m), (k2, n) = a.shape, bs
    assert k == k2, (a.shape, b.shape, mode)
    split_n = (b_cols and mode == "nn") or out_cols
    split_k = b_cols and mode == "nt"
    tm = _tile(m, MM_TILE if mode == "tn" else rows)
    tn = _tile(n // N_SHARDS if split_n else n, MM_TILE)
    tk = _tile(k // N_SHARDS if split_k else k, MM_DEPTH)
    nk = k // tk
    nj, nq = (n // N_SHARDS) // tn, (k // N_SHARDS) // tk
    if mode == "tn":
        a_spec = pl.BlockSpec((tk, tm), lambda i, j, q: (q, i))
    else:
        a_spec = pl.BlockSpec((tm, tk), lambda i, j, q: (i, q))
    if mode == "nt":
        if b_cols:
            b_spec = pl.BlockSpec((None, tn, tk), lambda i, j, q: (q // nq, j, q % nq))
        else:
            b_spec = pl.BlockSpec((tn, tk), lambda i, j, q: (j, q))
    elif b_cols:
        b_spec = pl.BlockSpec((None, tk, tn), lambda i, j, q: (j // nj, q, j % nj))
    else:
        b_spec = pl.BlockSpec((tk, tn), lambda i, j, q: (q, j))
    add_spec = pl.BlockSpec((tm, tn), lambda i, j, q: (i, j))
    if out_cols:
        o_spec = pl.BlockSpec((None, tm, tn), lambda i, j, q: (j // nj, i, j % nj))
        o_shape = (N_SHARDS, m, n // N_SHARDS)
    else:
        o_spec, o_shape = add_spec, (m, n)
    dot = {"nn": _dot, "nt": _dot_nt, "tn": _dot_tn}[mode]
    has_add = add is not None
    fuse_fn, extra, out_dtypes = fuse if fuse is not None else (None, [], [out_dtype])
    n_in, n_out = 2 + has_add + len(extra), len(out_dtypes)

    def body(*refs):
        a_ref, b_ref = refs[:2]
        add_ref = refs[2] if has_add else None
        extra_refs = refs[2 + has_add:n_in]
        o_refs, acc_ref = refs[n_in:n_in + n_out], refs[n_in + n_out]
        part = dot(a_ref[...].astype(BF16), b_ref[...].astype(BF16))

        def finish(total):
            if has_add:
                total = total + add_ref[...].astype(F32)
            outs = (total,) if fuse_fn is None else fuse_fn(total, *[r[...] for r in extra_refs])
            for o_ref, val in zip(o_refs, outs):
                o_ref[...] = val.astype(o_ref.dtype)

        if nk == 1:
            finish(part)
        else:
            q = pl.program_id(2)

            @pl.when(q == 0)
            def _():
                acc_ref[...] = part

            @pl.when(jnp.logical_and(q > 0, q < nk - 1))
            def _():
                acc_ref[...] += part

            @pl.when(q == nk - 1)
            def _():
                finish(acc_ref[...] + part)

    in_specs = [a_spec, b_spec] + [add_spec] * (has_add + len(extra))
    args = (a, b) + ((add,) if has_add else ()) + tuple(extra)
    res = _call(body, carry, name, (m // tm, n // tn, nk), in_specs, [o_spec] * n_out,
                [jax.ShapeDtypeStruct(o_shape, dt) for dt in out_dtypes],
                [pltpu.VMEM((tm, tn) if nk > 1 else (8, LANES), F32)], ("parallel", "parallel", "arbitrary"), args)
    main = res[0] if fuse is None else res[:n_out]
    return main if carry is None else (main, res[n_out:])


def _rms_fwd(x, g, name):
    s, d = x.shape
    ts = _tile(s, 512, 8)

    def body(x_ref, g_ref, o_ref):
        xv = x_ref[...]
        r = lax.rsqrt(jnp.mean(xv * xv, axis=-1, keepdims=True) + EPS)
        o_ref[...] = (xv * r * g_ref[...]).astype(BF16)

    return pl.pallas_call(
        body, name=name, grid=(s // ts,),
        in_specs=[pl.BlockSpec((ts, d), lambda i: (i, 0)), pl.BlockSpec((1, d), lambda i: (0, 0))],
        out_specs=pl.BlockSpec((ts, d), lambda i: (i, 0)),
        out_shape=jax.ShapeDtypeStruct((s, d), BF16),
        compiler_params=_params("parallel"),
    )(x, g)


def _rms_bwd(x, g, dh, dres, name):
    s, d = x.shape
    ts = _tile(s, 512, 8)

    def body(x_ref, g_ref, dh_ref, dres_ref, dx_ref, dg_ref):
        xv = x_ref[...]
        r = lax.rsqrt(jnp.mean(xv * xv, axis=-1, keepdims=True) + EPS)
        xhat = xv * r
        dhv = dh_ref[...].astype(F32)
        part = jnp.sum(dhv * xhat, axis=0, keepdims=True)

        @pl.when(pl.program_id(0) == 0)
        def _():
            dg_ref[...] = part

        @pl.when(pl.program_id(0) > 0)
        def _():
            dg_ref[...] += part

        dxh = dhv * g_ref[...]
        dx = r * (dxh - xhat * jnp.mean(dxh * xhat, axis=-1, keepdims=True))
        dx_ref[...] = dres_ref[...] + dx

    row = pl.BlockSpec((ts, d), lambda i: (i, 0))
    vec = pl.BlockSpec((1, d), lambda i: (0, 0))
    return pl.pallas_call(
        body, name=name, grid=(s // ts,),
        in_specs=[row, vec, row, row], out_specs=[row, vec],
        out_shape=[jax.ShapeDtypeStruct((s, d), F32), jax.ShapeDtypeStruct((1, d), F32)],
        compiler_params=_params("arbitrary"),
    )(x, g, dh, dres)


def _act_fwd(g, u, name):
    s, f = g.shape
    ts, tf = _tile(s, 512, 8), _tile(f, 1408)

    def body(g_ref, u_ref, o_ref):
        gv = g_ref[...]
        o_ref[...] = (gv * _sigmoid(gv) * u_ref[...]).astype(BF16)

    blk = pl.BlockSpec((ts, tf), lambda i, j: (i, j))
    return pl.pallas_call(
        body, name=name, grid=(s // ts, f // tf), in_specs=[blk, blk], out_specs=blk,
        out_shape=jax.ShapeDtypeStruct((s, f), BF16), compiler_params=_params("parallel", "parallel"),
    )(g, u)


def _act_bwd(g, u, da, name):
    s, f = g.shape
    ts, tf = _tile(s, 512, 8), _tile(f, 1408)

    def body(g_ref, u_ref, da_ref, dg_ref, du_ref):
        gv, uv, dav = g_ref[...], u_ref[...], da_ref[...].astype(F32)
        sg = _sigmoid(gv)
        silu = gv * sg
        du_ref[...] = (dav * silu).astype(BF16)
        dg_ref[...] = (dav * uv * sg * (1.0 + gv * (1.0 - sg))).astype(BF16)

    blk = pl.BlockSpec((ts, tf), lambda i, j: (i, j))
    return pl.pallas_call(
        body, name=name, grid=(s // ts, f // tf), in_specs=[blk, blk, blk], out_specs=[blk, blk],
        out_shape=[jax.ShapeDtypeStruct((s, f), BF16)] * 2, compiler_params=_params("parallel", "parallel"),
    )(g, u, da)


def _loss_fwd_bwd(y, target):
    s, d = y.shape
    ts = _tile(s, 512, 8)

    def body(y_ref, t_ref, l_ref, dy_ref):
        diff = y_ref[...] - t_ref[...]
        dy_ref[...] = diff * (1.0 / d)
        part = jnp.full((1, LANES), 0.5 * jnp.sum(jnp.mean(diff * diff, axis=-1, keepdims=True)), F32)

        @pl.when(pl.program_id(0) == 0)
        def _():
            l_ref[...] = part

        @pl.when(pl.program_id(0) > 0)
        def _():
            l_ref[...] += part

    row = pl.BlockSpec((ts, d), lambda i: (i, 0))
    acc = pl.BlockSpec((1, LANES), lambda i: (0, 0))
    return pl.pallas_call(
        body, name="loss", grid=(s // ts,), in_specs=[row, row], out_specs=[acc, row],
        out_shape=[jax.ShapeDtypeStruct((1, LANES), F32), jax.ShapeDtypeStruct((s, d), F32)],
        compiler_params=_params("arbitrary"),
    )(y, target)


def _lane_consts():
    r, c = _iota((LANES, LANES), 0), _iota((LANES, LANES), 1)
    same = (r >> 6) == (c >> 6)
    rin, cin = r & 63, c & 63
    one = lambda cond: jnp.where(cond, 1.0, 0.0).astype(BF16)
    return dict(
        seg=one(same),
        rot=(jnp.where(same & (rin == cin + 32), -1.0, 0.0)
             + jnp.where(same & (cin == rin + 32), 1.0, 0.0)).astype(BF16),
        dup_lo=one(r == cin), dup_hi=one(r == cin + 64),
        up=one((c >= 64) & (r == c - 64)), down=one((c < 64) & (r == c + 64)),
        fold_lo=one((c < 64) & (rin == c)), fold_hi=one((c >= 64) & (rin == c - 64)),
    )


def _norm_rope(xc, gain, cos, sin, k):
    ss = _dot_x(xc * xc, k["seg"])
    rinv = lax.rsqrt(ss * (1.0 / ATT_HEAD_DIM) + EPS)
    xhat = xc * rinv
    y = xhat * gain
    return y * cos + _dot_x(y, k["rot"]) * sin, xhat, rinv


def _norm_rope_bwd(dr, xhat, rinv, gain, cos, sin, k):
    dy = dr * cos - _dot_x(dr * sin, k["rot"])
    dgain = jnp.sum(dy * xhat, axis=0, keepdims=True)
    dxh = dy * gain
    dx = rinv * (dxh - xhat * (_dot_x(dxh * xhat, k["seg"]) * (1.0 / ATT_HEAD_DIM)))
    return dx, dgain


def _attn_prep(qkv, cos, sin, gq, gk):
    s = qkv.shape[0]
    tr = _tile(s, 256, 8)

    def body(x_ref, cos_ref, sin_ref, gq_ref, gk_ref, q_ref, kk_ref, vlo_ref, vhi_ref):
        k = _lane_consts()
        cosv, sinv = cos_ref[...], sin_ref[...]
        lane = _iota((tr, LANES), 1)
        for j in range(Q_WIDTH // LANES):
            r, _, _ = _norm_rope(x_ref[:, j * LANES:(j + 1) * LANES], gq_ref[...], cosv, sinv, k)
            q_ref[:, j * LANES:(j + 1) * LANES] = r.astype(BF16)
        for i in range(KV_WIDTH // LANES):
            off = Q_WIDTH + i * LANES
            r, _, _ = _norm_rope(x_ref[:, off:off + LANES], gk_ref[...], cosv, sinv, k)
            rb = r.astype(BF16)
            kk_ref[:, (2 * i) * LANES:(2 * i + 1) * LANES] = _dot(rb, k["dup_lo"]).astype(BF16)
            kk_ref[:, (2 * i + 1) * LANES:(2 * i + 2) * LANES] = _dot(rb, k["dup_hi"]).astype(BF16)
            off = Q_WIDTH + KV_WIDTH + i * LANES
            vb = x_ref[:, off:off + LANES].astype(BF16)
            zero = jnp.zeros_like(vb)
            vlo_ref[:, (2 * i) * LANES:(2 * i + 1) * LANES] = jnp.where(lane < 64, vb, zero)
            vhi_ref[:, (2 * i) * LANES:(2 * i + 1) * LANES] = _dot(vb, k["up"]).astype(BF16)
            vlo_ref[:, (2 * i + 1) * LANES:(2 * i + 2) * LANES] = _dot(vb, k["down"]).astype(BF16)
            vhi_ref[:, (2 * i + 1) * LANES:(2 * i + 2) * LANES] = jnp.where(lane >= 64, vb, zero)

    w = qkv.shape[1]
    row = lambda width: pl.BlockSpec((tr, width), lambda i: (i, 0))
    vec = pl.BlockSpec((1, LANES), lambda i: (0, 0))
    kw = ATT_KV_HEADS * LANES
    return pl.pallas_call(
        body, name="attn_prep", grid=(s // tr,),
        in_specs=[row(w), row(LANES), row(LANES), vec, vec],
        out_specs=[row(Q_WIDTH), row(kw), row(kw), row(kw)],
        out_shape=[jax.ShapeDtypeStruct((s, Q_WIDTH), BF16)] + [jax.ShapeDtypeStruct((s, kw), BF16)] * 3,
        compiler_params=_params("parallel"),
    )(qkv, cos, sin, gq, gk)


def _band_mask(n):
    qi = _iota((ATT_BLOCK, 2 * ATT_BLOCK), 0)
    kj = _iota((ATT_BLOCK, 2 * ATT_BLOCK), 1)
    band = (kj > qi) & (kj <= qi + ATT_BLOCK)
    return band & ((kj >= ATT_BLOCK) | (n > 0))


def _softmax_sink(s, valid, sink):
    s = jnp.where(valid, s, -jnp.inf)
    m = jnp.maximum(jnp.max(s, axis=-1, keepdims=True), sink)
    p = jnp.exp(s - m)
    esink = jnp.exp(sink - m)
    inv = 1.0 / (jnp.sum(p, axis=-1, keepdims=True) + esink)
    return p * inv, esink * inv


def _attn_specs(order):
    if order == "nh":
        cur = lambda n, h: (n, h)
        prev = lambda n, h: (jnp.maximum(n - 1, 0), h)
    else:
        cur = lambda h, n: (n, h)
        prev = lambda h, n: (jnp.maximum(n - 1, 0), h)
    qs = pl.BlockSpec((ATT_BLOCK, ATT_GROUP * ATT_HEAD_DIM), cur)
    kc = pl.BlockSpec((ATT_BLOCK, LANES), cur)
    kp = pl.BlockSpec((ATT_BLOCK, LANES), prev)
    return qs, kc, kp


def _attn_fwd(q, kk, vlo, vhi, sinks, name="attn_fwd", carry=None):
    s = q.shape[0]
    nb = s // ATT_BLOCK
    scale = ATT_HEAD_DIM ** -0.5

    def body(sink_ref, q_ref, kc_ref, kp_ref, vloc_ref, vlop_ref, vhic_ref, vhip_ref, o_ref):
        n, h = pl.program_id(0), pl.program_id(1)
        valid = _band_mask(n)
        kw = jnp.concatenate([kp_ref[...], kc_ref[...]], axis=0)
        vw = (jnp.concatenate([vlop_ref[...], vloc_ref[...]], axis=0),
              jnp.concatenate([vhip_ref[...], vhic_ref[...]], axis=0))
        lane = _iota((ATT_BLOCK, LANES), 1)
        for jp in range(ATT_GROUP // 2):
            qp = q_ref[:, jp * LANES:(jp + 1) * LANES]
            acc = jnp.zeros((ATT_BLOCK, LANES), F32)
            for hf in range(2):
                qm = jnp.where((lane >= 64) == (hf == 1), qp, jnp.zeros_like(qp))
                sc = _dot_nt(qm, kw) * scale
                probs, _ = _softmax_sink(sc, valid, sink_ref[h * ATT_GROUP + 2 * jp + hf])
                acc = acc + _dot(probs.astype(BF16), vw[hf])
            o_ref[:, jp * LANES:(jp + 1) * LANES] = acc.astype(BF16)

    qs, kc, kp = _attn_specs("nh")
    res = _call(body, carry, name, (nb, ATT_KV_HEADS),
                [pl.BlockSpec(memory_space=pltpu.SMEM), qs, kc, kp, kc, kp, kc, kp], [qs],
                [jax.ShapeDtypeStruct((s, Q_WIDTH), BF16)], [], ("parallel", "parallel"),
                (sinks, q, kk, kk, vlo, vlo, vhi, vhi))
    return res[0] if carry is None else (res[0], res[1:])


def _attn_bwd(q, kk, vlo, vhi, sinks, do, name="attn_bwd", carry=None):
    s = q.shape[0]
    nb = s // ATT_BLOCK
    scale = ATT_HEAD_DIM ** -0.5

    def body(sink_ref, q_ref, kc_ref, kp_ref, vloc_ref, vlop_ref, vhic_ref, vhip_ref, do_ref,
             dq_ref, dkc_ref, dkp_ref, dvloc_ref, dvlop_ref, dvhic_ref, dvhip_ref, dsink_ref):
        h, n = pl.program_id(0), pl.program_id(1)
        valid = _band_mask(n)
        kw = jnp.concatenate([kp_ref[...], kc_ref[...]], axis=0)
        vw = (jnp.concatenate([vlop_ref[...], vloc_ref[...]], axis=0),
              jnp.concatenate([vhip_ref[...], vhic_ref[...]], axis=0))
        lane = _iota((ATT_BLOCK, LANES), 1)
        sub = _iota((ATT_GROUP, LANES), 0)
        dkk = jnp.zeros((2 * ATT_BLOCK, LANES), F32)
        dv = [jnp.zeros((2 * ATT_BLOCK, LANES), F32), jnp.zeros((2 * ATT_BLOCK, LANES), F32)]
        dsink = jnp.zeros((ATT_GROUP, LANES), F32)
        for jp in range(ATT_GROUP // 2):
            qp = q_ref[:, jp * LANES:(jp + 1) * LANES]
            dop = do_ref[:, jp * LANES:(jp + 1) * LANES]
            dq = jnp.zeros((ATT_BLOCK, LANES), F32)
            for hf in range(2):
                mine = (lane >= 64) == (hf == 1)
                qm = jnp.where(mine, qp, jnp.zeros_like(qp))
                sc = _dot_nt(qm, kw) * scale
                probs, psink = _softmax_sink(sc, valid, sink_ref[h * ATT_GROUP + 2 * jp + hf])
                pb = probs.astype(BF16)
                dprobs = _dot_nt(dop, vw[hf])
                dv[hf] = dv[hf] + _dot_tn(pb, dop)
                delta = jnp.sum(probs * dprobs, axis=-1, keepdims=True)
                ds = (probs * (dprobs - delta) * scale).astype(BF16)
                dsink = dsink + jnp.where(sub == 2 * jp + hf, -jnp.sum(psink * delta), 0.0)
                dq = dq + jnp.where(mine, _dot(ds, kw), 0.0)
                dkk = dkk + _dot_tn(ds, qm)
            dq_ref[:, jp * LANES:(jp + 1) * LANES] = dq
        dkp_ref[...], dkc_ref[...] = dkk[:ATT_BLOCK], dkk[ATT_BLOCK:]
        dvlop_ref[...], dvloc_ref[...] = dv[0][:ATT_BLOCK], dv[0][ATT_BLOCK:]
        dvhip_ref[...], dvhic_ref[...] = dv[1][:ATT_BLOCK], dv[1][ATT_BLOCK:]

        @pl.when(n == 0)
        def _():
            dsink_ref[0] = dsink

        @pl.when(n > 0)
        def _():
            dsink_ref[0] += dsink

    qs, kc, kp = _attn_specs("hn")
    kw_shape = jax.ShapeDtypeStruct((s, ATT_KV_HEADS * LANES), F32)
    res = _call(body, carry, name, (ATT_KV_HEADS, nb),
                [pl.BlockSpec(memory_space=pltpu.SMEM), qs, kc, kp, kc, kp, kc, kp, qs],
                [qs] + [kc] * 6 + [pl.BlockSpec((1, ATT_GROUP, LANES), lambda h, n: (h, 0, 0))],
                [jax.ShapeDtypeStruct((s, Q_WIDTH), F32)] + [kw_shape] * 6
                + [jax.ShapeDtypeStruct((ATT_KV_HEADS, ATT_GROUP, LANES), F32)], [], ("parallel", "arbitrary"),
                (sinks, q, kk, kk, vlo, vlo, vhi, vhi, do))
    return res if carry is None else (res[:8], res[8:])


def _attn_prep_bwd(qkv, cos, sin, gq, gk, dq, dks, dvlos, dvhis):
    s, w = qkv.shape
    tr = ATT_BLOCK
    nb = s // tr

    def body(x_ref, cos_ref, sin_ref, gq_ref, gk_ref, dq_ref, dkc_ref, dkn_ref, dvloc_ref, dvlon_ref,
             dvhic_ref, dvhin_ref, dx_ref, dgq_ref, dgk_ref):
        n = pl.program_id(0)
        k = _lane_consts()
        cosv, sinv = cos_ref[...], sin_ref[...]
        nxt = jnp.where(n < nb - 1, 1.0, 0.0)
        lane = _iota((tr, LANES), 1)
        dgq = jnp.zeros((1, LANES), F32)
        dgk = jnp.zeros((1, LANES), F32)
        for j in range(Q_WIDTH // LANES):
            sl = slice(j * LANES, (j + 1) * LANES)
            _, xhat, rinv = _norm_rope(x_ref[:, sl], gq_ref[...], cosv, sinv, k)
            dx, dg = _norm_rope_bwd(dq_ref[:, sl], xhat, rinv, gq_ref[...], cosv, sinv, k)
            dx_ref[:, sl] = dx.astype(BF16)
            dgq = dgq + dg
        for i in range(KV_WIDTH // LANES):
            a, b = slice(2 * i * LANES, (2 * i + 1) * LANES), slice((2 * i + 1) * LANES, (2 * i + 2) * LANES)
            dr = (_dot_x(dkc_ref[:, a] + nxt * dkn_ref[:, a], k["fold_lo"])
                  + _dot_x(dkc_ref[:, b] + nxt * dkn_ref[:, b], k["fold_hi"]))
            sl = slice(Q_WIDTH + i * LANES, Q_WIDTH + (i + 1) * LANES)
            _, xhat, rinv = _norm_rope(x_ref[:, sl], gk_ref[...], cosv, sinv, k)
            dx, dg = _norm_rope_bwd(dr, xhat, rinv, gk_ref[...], cosv, sinv, k)
            dx_ref[:, sl] = dx.astype(BF16)
            dgk = dgk + dg
            ta = jnp.where(lane < 64, dvloc_ref[:, a] + nxt * dvlon_ref[:, a], dvhic_ref[:, a] + nxt * dvhin_ref[:, a])
            tb = jnp.where(lane < 64, dvloc_ref[:, b] + nxt * dvlon_ref[:, b], dvhic_ref[:, b] + nxt * dvhin_ref[:, b])
            sl = slice(Q_WIDTH + KV_WIDTH + i * LANES, Q_WIDTH + KV_WIDTH + (i + 1) * LANES)
            dx_ref[:, sl] = (_dot_x(ta, k["fold_lo"]) + _dot_x(tb, k["fold_hi"])).astype(BF16)

        @pl.when(n == 0)
        def _():
            dgq_ref[...] = dgq
            dgk_ref[...] = dgk

        @pl.when(n > 0)
        def _():
            dgq_ref[...] += dgq
            dgk_ref[...] += dgk

    row = lambda width: pl.BlockSpec((tr, width), lambda i: (i, 0))
    nxt_row = pl.BlockSpec((tr, ATT_KV_HEADS * LANES), lambda i: (jnp.minimum(i + 1, nb - 1), 0))
    vec = pl.BlockSpec((1, LANES), lambda i: (0, 0))
    kw = ATT_KV_HEADS * LANES
    return pl.pallas_call(
        body, name="attn_prep_bwd", grid=(nb,),
        in_specs=[row(w), row(LANES), row(LANES), vec, vec, row(Q_WIDTH),
                  row(kw), nxt_row, row(kw), nxt_row, row(kw), nxt_row],
        out_specs=[row(w), vec, vec],
        out_shape=[jax.ShapeDtypeStruct((s, w), BF16), jax.ShapeDtypeStruct((1, LANES), F32),
                   jax.ShapeDtypeStruct((1, LANES), F32)],
        compiler_params=_params("arbitrary"),
    )(qkv, cos, sin, gq, gk, dq, dks[0], dks[1], dvlos[0], dvlos[1], dvhis[0], dvhis[1])


CONV_HALO = 8
CONV_TC = 512
XBC_OFF = SSM_D_INNER // CONV_TC
DT_OFF = SSM_D_INNER + SSM_CONV_DIM


def _conv_pre(ext, w_ref, b_ref, ts):
    pre = b_ref[...] + w_ref[SSM_CONV - 1:SSM_CONV, :] * ext[CONV_HALO:]
    for kk in range(SSM_CONV - 1):
        pre = pre + w_ref[kk:kk + 1, :] * pltpu.roll(ext, SSM_CONV - 1 - kk, 0)[CONV_HALO:]
    return pre


def _conv_specs(ts):
    tc = CONV_TC
    src = pl.BlockSpec((ts, tc), lambda j, i: (i, XBC_OFF + j))
    halo = pl.BlockSpec((CONV_HALO, tc), lambda j, i: (jnp.maximum(i * (ts // CONV_HALO) - 1, 0), XBC_OFF + j))
    blk = pl.BlockSpec((ts, tc), lambda j, i: (i, j))
    wspec = pl.BlockSpec((SSM_CONV, tc), lambda j, i: (0, j))
    bspec = pl.BlockSpec((1, tc), lambda j, i: (0, j))
    return src, halo, blk, wspec, bspec


def _conv_fwd(zx, w, b):
    s, c = zx.shape[0], SSM_CONV_DIM
    ts = _tile(s, 512, 8)

    def body(u_ref, halo_ref, w_ref, b_ref, o_ref):
        halo = jnp.where(pl.program_id(1) > 0, halo_ref[...], 0.0)
        pre = _conv_pre(jnp.concatenate([halo, u_ref[...]], axis=0), w_ref, b_ref, ts)
        o_ref[...] = pre * _sigmoid(pre)

    src, halo, blk, wspec, bspec = _conv_specs(ts)
    return pl.pallas_call(
        body, name="conv_fwd", grid=(c // CONV_TC, s // ts),
        in_specs=[src, halo, wspec, bspec], out_specs=blk, out_shape=jax.ShapeDtypeStruct((s, c), F32),
        compiler_params=_params("parallel", "parallel"),
    )(zx, zx, w, b)


def _conv_bwd_pre(zx, w, b, dact):
    s, c = zx.shape[0], SSM_CONV_DIM
    ts = _tile(s, 512, 8)

    def body(u_ref, halo_ref, w_ref, b_ref, da_ref, dpre_ref, dw_ref, db_ref):
        i = pl.program_id(1)
        halo = jnp.where(i > 0, halo_ref[...], 0.0)
        ext = jnp.concatenate([halo, u_ref[...]], axis=0)
        pre = _conv_pre(ext, w_ref, b_ref, ts)
        sg = _sigmoid(pre)
        dpre = da_ref[...] * sg * (1.0 + pre * (1.0 - sg))
        dpre_ref[...] = dpre
        rows = [jnp.sum(dpre * pltpu.roll(ext, SSM_CONV - 1 - kk, 0)[CONV_HALO:], axis=0, keepdims=True)
                for kk in range(SSM_CONV - 1)]
        rows.append(jnp.sum(dpre * ext[CONV_HALO:], axis=0, keepdims=True))
        dwp = jnp.concatenate(rows, axis=0)
        dbp = jnp.sum(dpre, axis=0, keepdims=True)

        @pl.when(i == 0)
        def _():
            dw_ref[...] = dwp
            db_ref[...] = dbp

        @pl.when(i > 0)
        def _():
            dw_ref[...] += dwp
            db_ref[...] += dbp

    src, halo, blk, wspec, bspec = _conv_specs(ts)
    return pl.pallas_call(
        body, name="conv_bwd_pre", grid=(c // CONV_TC, s // ts),
        in_specs=[src, halo, wspec, bspec, blk], out_specs=[blk, wspec, bspec],
        out_shape=[jax.ShapeDtypeStruct((s, c), F32), jax.ShapeDtypeStruct((SSM_CONV, c), F32),
                   jax.ShapeDtypeStruct((1, c), F32)],
        compiler_params=_params("parallel", "arbitrary"),
    )(zx, zx, w, b, dact)


def _conv_bwd_in(dpre, w, dzx):
    s, c = dpre.shape
    ts, tc = _tile(s, 512, 8), CONV_TC
    ns = s // ts

    def body(d_ref, halo_ref, w_ref, dzx_ref, o_ref):
        del dzx_ref
        halo = jnp.where(pl.program_id(1) < ns - 1, halo_ref[...], 0.0)
        ext = jnp.concatenate([d_ref[...], halo], axis=0)
        du = w_ref[SSM_CONV - 1:SSM_CONV, :] * ext[:ts]
        for kk in range(SSM_CONV - 1):
            du = du + w_ref[kk:kk + 1, :] * pltpu.roll(ext, ts + CONV_HALO - (SSM_CONV - 1 - kk), 0)[:ts]
        o_ref[...] = du.astype(BF16)

    blk = pl.BlockSpec((ts, tc), lambda j, i: (i, j))
    halo = pl.BlockSpec((CONV_HALO, tc), lambda j, i: (jnp.minimum((i + 1) * (ts // CONV_HALO), s // CONV_HALO - 1), j))
    return pl.pallas_call(
        body, name="conv_bwd_in", grid=(c // tc, ns),
        in_specs=[blk, halo, pl.BlockSpec((SSM_CONV, tc), lambda j, i: (0, j)), ANY],
        out_specs=pl.BlockSpec((ts, tc), lambda j, i: (i, XBC_OFF + j)),
        out_shape=jax.ShapeDtypeStruct(dzx.shape, BF16), input_output_aliases={3: 0},
        compiler_params=_params("parallel", "parallel"),
    )(dpre, dpre, w, dzx)


def _ssd_common(dt_ref, dtt_ref, bias_ref, biast_ref, alog_ref, alogt_ref):
    ln = SSM_CHUNK
    raw, rawt = dt_ref[0] + bias_ref[0], dtt_ref[0] + biast_ref[0]
    dt, dtt = _softplus(raw), _softplus(rawt)
    a, at = -jnp.exp(alog_ref[0]), -jnp.exp(alogt_ref[0])
    tri = jnp.where(_iota((ln, ln), 0) >= _iota((ln, ln), 1), 1.0, 0.0).astype(BF16)
    return dict(raw=raw, rawt=rawt, dt=dt, dtt=dtt, a=a, at=at, tri=tri,
                acum=_xdot(tri, dt * a), acumt=_dot_x_nt(dtt * at, tri))


def _ssd_specs(nc, rev):
    cidx = (lambda c: nc - 1 - c) if rev else (lambda c: c)
    ln = SSM_CHUNK
    xs = pl.BlockSpec((ln, SSM_GN), lambda g, c: (cidx(c), g))
    bs = pl.BlockSpec((ln, SSM_STATE), lambda g, c: (cidx(c), SSM_D_INNER // SSM_STATE + g))
    cs = pl.BlockSpec((ln, SSM_STATE), lambda g, c: (cidx(c), SSM_D_INNER // SSM_STATE + SSM_GROUPS + g))
    dt = pl.BlockSpec((1, ln, SSM_HPG), lambda g, c: (g, cidx(c), 0))
    dtt = pl.BlockSpec((1, SSM_HPG, ln), lambda g, c: (g, 0, cidx(c)))
    row = pl.BlockSpec((1, 1, SSM_HPG), lambda g, c: (g, 0, 0))
    col = pl.BlockSpec((1, SSM_HPG, 1), lambda g, c: (g, 0, 0))
    st = pl.BlockSpec((1, SSM_HPG, SSM_P, SSM_STATE), lambda g, c: (cidx(c), g, 0, 0))
    return xs, bs, cs, dt, dtt, row, col, st


def _ssd_fwd(xbc, dt_g, dt_gt, bias_r, bias_c, alog_r, alog_c, d_r):
    s = xbc.shape[0]
    ln = SSM_CHUNK
    nc = s // ln

    def body(x_ref, b_ref, c_ref, dt_ref, dtt_ref, bias_ref, biast_ref, alog_ref, alogt_ref, d_ref,
             y_ref, st_ref, state):
        @pl.when(pl.program_id(1) == 0)
        def _():
            state[...] = jnp.zeros_like(state)

        cm = _ssd_common(dt_ref, dtt_ref, bias_ref, biast_ref, alog_ref, alogt_ref)
        dt, acum, acumt = cm["dt"], cm["acum"], cm["acumt"]
        bb, cb = b_ref[...].astype(BF16), c_ref[...].astype(BF16)
        cbm = _dot_nt(cb, bb)
        causal = _iota((ln, ln), 0) >= _iota((ln, ln), 1)
        st_ref[0] = state[...]
        for r in range(SSM_HPG):
            xr = x_ref[:, r * SSM_P:(r + 1) * SSM_P]
            ac, last = acum[:, r:r + 1], acum[ln - 1:ln, r:r + 1]
            decay = jnp.exp(jnp.where(causal, ac - acumt[r:r + 1, :], -jnp.inf))
            xdt = xr * dt[:, r:r + 1]
            sr = state[r]
            y = (_dot((cbm * decay).astype(BF16), xdt.astype(BF16))
                 + _dot_nt(cb, sr.astype(BF16)) * jnp.exp(ac) + d_ref[0][:, r:r + 1] * xr)
            y_ref[:, r * SSM_P:(r + 1) * SSM_P] = y
            state[r] = sr * jnp.exp(last) + _dot_tn((xdt * jnp.exp(last - ac)).astype(BF16), bb)

    xs, bs, cs, dts, dtts, row, col, st = _ssd_specs(nc, False)
    return pl.pallas_call(
        body, name="ssd_fwd", grid=(SSM_GROUPS, nc),
        in_specs=[xs, bs, cs, dts, dtts, row, col, row, col, row],
        out_specs=[xs, st],
        out_shape=[jax.ShapeDtypeStruct((s, SSM_D_INNER), F32),
                   jax.ShapeDtypeStruct((nc, SSM_HEADS, SSM_P, SSM_STATE), F32)],
        scratch_shapes=[pltpu.VMEM((SSM_HPG, SSM_P, SSM_STATE), F32)],
        compiler_params=_params("parallel", "arbitrary"),
    )(xbc, xbc, xbc, dt_g, dt_gt, bias_r, bias_c, alog_r, alog_c, d_r)


def _ssd_bwd(xbc, dt_g, dt_gt, bias_r, bias_c, alog_r, alog_c, d_r, states, dy):
    s = xbc.shape[0]
    ln = SSM_CHUNK
    nc = s // ln

    def body(x_ref, b_ref, c_ref, dt_ref, dtt_ref, bias_ref, biast_ref, alog_ref, alogt_ref, d_ref,
             st_ref, dy_ref, dx_ref, db_ref, dc_ref, ddt_ref, ddtt_ref, dbias_ref, dbiast_ref,
             dalog_ref, dalogt_ref, dd_ref, dstate):
        step = pl.program_id(1)

        @pl.when(step == 0)
        def _():
            dstate[...] = jnp.zeros_like(dstate)

        cm = _ssd_common(dt_ref, dtt_ref, bias_ref, biast_ref, alog_ref, alogt_ref)
        dt, acum, acumt = cm["dt"], cm["acum"], cm["acumt"]
        bb, cb = b_ref[...].astype(BF16), c_ref[...].astype(BF16)
        cbm = _dot_nt(cb, bb)
        causal = _iota((ln, ln), 0) >= _iota((ln, ln), 1)
        lane8 = _iota((ln, SSM_HPG), 1)
        sub8 = _iota((SSM_HPG, ln), 0)
        lane1 = _iota((1, SSM_HPG), 1)
        is_last = _iota((ln, 1), 0) == ln - 1
        dcb = jnp.zeros((ln, ln), F32)
        dc_acc = jnp.zeros((ln, SSM_STATE), F32)
        db_acc = jnp.zeros((ln, SSM_STATE), F32)
        dac_rows = jnp.zeros((ln, SSM_HPG), F32)
        dac_cols = jnp.zeros((SSM_HPG, ln), F32)
        ddt_all = jnp.zeros((ln, SSM_HPG), F32)
        dd_all = jnp.zeros((1, SSM_HPG), F32)
        for r in range(SSM_HPG):
            sl = slice(r * SSM_P, (r + 1) * SSM_P)
            xr, dyr = x_ref[:, sl], dy_ref[:, sl]
            dtc, dr = dt[:, r:r + 1], d_ref[0][:, r:r + 1]
            ac, last = acum[:, r:r + 1], acum[ln - 1:ln, r:r + 1]
            decay = jnp.exp(jnp.where(causal, ac - acumt[r:r + 1, :], -jnp.inf))
            w = (cbm * decay).astype(BF16)
            xdt = xr * dtc
            xdtb, dyb = xdt.astype(BF16), dyr.astype(BF16)
            eac, to_end, elast = jnp.exp(ac), jnp.exp(last - ac), jnp.exp(last)
            sr, dsr = st_ref[0, r], dstate[r]
            srb, dsrb = sr.astype(BF16), dsr.astype(BF16)
            dxdt_state = _dot_nt(bb, dsrb) * to_end
            dxdt = _dot_tn(w, dyb) + dxdt_state
            dcb_r = _dot_nt(dyb, xdtb) * decay
            dcb = dcb + dcb_r
            e = dcb_r * cbm
            dc_acc = dc_acc + _dot(dyb, srb) * eac
            db_acc = db_acc + _dot((xdt * to_end).astype(BF16), dsrb)
            yoff = _dot_nt(cb, srb) * eac
            f_rows = jnp.sum(xdt * dxdt_state, axis=-1, keepdims=True)
            dlast = jnp.sum(f_rows) + elast * jnp.sum(dsr * sr)
            dac = (jnp.sum(e, axis=-1, keepdims=True) + jnp.sum(dyr * yoff, axis=-1, keepdims=True) - f_rows
                   + jnp.where(is_last, dlast, 0.0))
            dac_rows = dac_rows + jnp.where(lane8 == r, dac, 0.0)
            dac_cols = dac_cols + jnp.where(sub8 == r, jnp.sum(e, axis=0, keepdims=True), 0.0)
            ddt_all = ddt_all + jnp.where(lane8 == r, jnp.sum(dxdt * xr, axis=-1, keepdims=True), 0.0)
            dd_all = dd_all + jnp.where(lane1 == r, jnp.sum(dyr * xr), 0.0)
            dx_ref[:, sl] = dxdt * dtc + dr * dyr
            dstate[r] = elast * dsr + _dot_tn((dyr * eac).astype(BF16), cb)
        dcbb = dcb.astype(BF16)
        dc_ref[...] = dc_acc + _dot(dcbb, bb)
        db_ref[...] = db_acc + _dot_tn(dcbb, cb)
        triu = jnp.where(_iota((ln, ln), 0) <= _iota((ln, ln), 1), 1.0, 0.0).astype(BF16)
        g_rows = _xdot(triu, dac_rows)
        g_cols = _dot_x(dac_cols, cm["tri"])
        d_rows = (ddt_all + g_rows * cm["a"]) * _sigmoid(cm["raw"])
        d_cols = -(g_cols * cm["at"]) * _sigmoid(cm["rawt"])
        ddt_ref[0] = d_rows
        ddtt_ref[0] = d_cols
        parts = (jnp.sum(d_rows, axis=0, keepdims=True), jnp.sum(d_cols, axis=1, keepdims=True),
                 jnp.sum(g_rows * dt, axis=0, keepdims=True) * cm["a"],
                 -jnp.sum(g_cols * cm["dtt"], axis=1, keepdims=True) * cm["at"], dd_all)
        outs = (dbias_ref, dbiast_ref, dalog_ref, dalogt_ref, dd_ref)

        @pl.when(step == 0)
        def _():
            for o_ref, p in zip(outs, parts):
                o_ref[0] = p

        @pl.when(step > 0)
        def _():
            for o_ref, p in zip(outs, parts):
                o_ref[0] += p

    xs, bs, cs, dts, dtts, row, col, st = _ssd_specs(nc, True)
    grp = pl.BlockSpec((ln, SSM_STATE), lambda g, c: (nc - 1 - c, g))
    rows = jax.ShapeDtypeStruct((SSM_GROUPS, 1, SSM_HPG), F32)
    cols = jax.ShapeDtypeStruct((SSM_GROUPS, SSM_HPG, 1), F32)
    return pl.pallas_call(
        body, name="ssd_bwd", grid=(SSM_GROUPS, nc),
        in_specs=[xs, bs, cs, dts, dtts, row, col, row, col, row, st, xs],
        out_specs=[xs, grp, grp, dts, dtts, row, col, row, col, row],
        out_shape=[jax.ShapeDtypeStruct((s, SSM_D_INNER), F32),
                   jax.ShapeDtypeStruct((s, SSM_GROUPS * SSM_STATE), F32),
                   jax.ShapeDtypeStruct((s, SSM_GROUPS * SSM_STATE), F32),
                   jax.ShapeDtypeStruct((SSM_GROUPS, s, SSM_HPG), F32),
                   jax.ShapeDtypeStruct((SSM_GROUPS, SSM_HPG, s), F32), rows, cols, rows, cols, rows],
        scratch_shapes=[pltpu.VMEM((SSM_HPG, SSM_P, SSM_STATE), F32)],
        compiler_params=_params("parallel", "arbitrary"),
    )(xbc, xbc, xbc, dt_g, dt_gt, bias_r, bias_c, alog_r, alog_c, d_r, states, dy)


def _gate_norm_fwd(y, zx, g):
    s = y.shape[0]
    ts = _tile(s, 512, 8)

    def body(y_ref, z_ref, g_ref, o_ref):
        zv = z_ref[...]
        yg = y_ref[...] * (zv * _sigmoid(zv))
        r = lax.rsqrt(jnp.mean(yg * yg, axis=-1, keepdims=True) + EPS)
        o_ref[...] = (yg * r * g_ref[...]).astype(BF16)

    blk = pl.BlockSpec((ts, SSM_GN), lambda j, i: (i, j))
    vec = pl.BlockSpec((1, SSM_GN), lambda j, i: (0, j))
    return pl.pallas_call(
        body, name="gate_norm_fwd", grid=(SSM_GROUPS, s // ts), in_specs=[blk, blk, vec], out_specs=blk,
        out_shape=jax.ShapeDtypeStruct((s, SSM_D_INNER), BF16), compiler_params=_params("parallel", "parallel"),
    )(y, zx, g)


def _gate_norm_bwd(y, zx, g, dout):
    s = y.shape[0]
    ts = _tile(s, 512, 8)

    def body(y_ref, z_ref, g_ref, do_ref, dy_ref, dz_ref, dg_ref):
        yv, zv, dov = y_ref[...], z_ref[...], do_ref[...].astype(F32)
        sg = _sigmoid(zv)
        silu = zv * sg
        yg = yv * silu
        r = lax.rsqrt(jnp.mean(yg * yg, axis=-1, keepdims=True) + EPS)
        ygn = yg * r
        part = jnp.sum(dov * ygn, axis=0, keepdims=True)

        @pl.when(pl.program_id(1) == 0)
        def _():
            dg_ref[...] = part

        @pl.when(pl.program_id(1) > 0)
        def _():
            dg_ref[...] += part

        dn = dov * g_ref[...]
        dyg = r * (dn - ygn * jnp.mean(dn * ygn, axis=-1, keepdims=True))
        dy_ref[...] = dyg * silu
        dz_ref[...] = (dyg * yv * sg * (1.0 + zv * (1.0 - sg))).astype(BF16)

    blk = pl.BlockSpec((ts, SSM_GN), lambda j, i: (i, j))
    vec = pl.BlockSpec((1, SSM_GN), lambda j, i: (0, j))
    return pl.pallas_call(
        body, name="gate_norm_bwd", grid=(SSM_GROUPS, s // ts), in_specs=[blk, blk, vec, blk],
        out_specs=[blk, blk, vec],
        out_shape=[jax.ShapeDtypeStruct((s, SSM_D_INNER), F32), jax.ShapeDtypeStruct((s, SSM_IN_PAD), BF16),
                   jax.ShapeDtypeStruct((1, SSM_D_INNER), F32)],
        compiler_params=_params("parallel", "arbitrary"),
    )(y, zx, g, dout)


def _rope_tables(positions):
    inv_freq = ROPE_THETA ** (-jnp.arange(0, ATT_HEAD_DIM, 2, dtype=F32) / ATT_HEAD_DIM)
    ang = positions.astype(F32)[:, None] * inv_freq
    return jnp.tile(jnp.cos(ang), (1, 4)), jnp.tile(jnp.sin(ang), (1, 4))


def _group_views(v):
    return v.reshape(SSM_GROUPS, 1, SSM_HPG), v.reshape(SSM_GROUPS, SSM_HPG, 1)


def _ffn_fwd(run, x, norm_g, wg, wu, wd, tag):
    h = _rms_fwd(x, norm_g, f"ffn_norm_{tag}")
    g = run(f"ffn_gate_{tag}", _mm, h, wg, "nn", b_cols=True)
    u, a = run(f"ffn_up_{tag}", _mm, h, wu, "nn", b_cols=True, rows=FUSED_ROWS,
               fuse=(lambda uv, gv: (uv, gv * _sigmoid(gv) * uv), [g], [F32, BF16]))
    return run(f"ffn_down_{tag}", _mm, a, wd, "nn", add=x), (h, g, u, a)


def _ffn_bwd(run, mats, x, norm_g, wg, wu, wd, saved, dout, tag):
    h, g, u, a = saved

    def act_bwd(da, gv, uv):
        sg = _sigmoid(gv)
        return da * uv * sg * (1.0 + gv * (1.0 - sg)), da * (gv * sg)

    dg, du = run(f"ffn_down_dx_{tag}", _mm, dout, wd, "nt", rows=FUSED_ROWS,
                 fuse=(act_bwd, [g, u], [BF16, BF16]))
    dwd = run(f"ffn_down_dw_{tag}", _mm, a, dout, "tn", out_dtype=BF16)
    mats[("ffn_w_down", tag)] = dwd.reshape(N_SHARDS, dwd.shape[0] // N_SHARDS, dwd.shape[1])
    mats[("ffn_w_gate", tag)] = run(f"ffn_gate_dw_{tag}", _mm, h, dg, "tn", out_dtype=BF16, out_cols=True)
    mats[("ffn_w_up", tag)] = run(f"ffn_up_dw_{tag}", _mm, h, du, "tn", out_dtype=BF16, out_cols=True)
    dh = run(f"ffn_gate_dx_{tag}", _mm, dg, wg, "nt", b_cols=True)
    dh = run(f"ffn_up_dx_{tag}", _mm, du, wu, "nt", add=dh, b_cols=True)
    return _rms_bwd(x, norm_g, dh, dout, f"ffn_norm_bwd_{tag}")


class _Hook:
    def __init__(self, make, done):
        self.make, self.done = make, done


def _local_step(x, positions, target, w, hooks=None, mats=None):
    hooks = {} if hooks is None else hooks
    mats = {} if mats is None else mats

    def run(name, fn, *args, **kw):
        hook = hooks.get(name)
        if hook is None:
            return fn(*args, name=name, **kw)
        res, carried = fn(*args, name=name, carry=hook.make(), **kw)
        hook.done(carried)
        return res

    cos, sin = _rope_tables(positions)
    row = lambda v: v.reshape(1, -1)
    gq, gk = jnp.tile(row(w["attn_q_norm"]), (1, 2)), jnp.tile(row(w["attn_k_norm"]), (1, 2))
    sinks = w["attn_sinks"].reshape(-1)
    s = x.shape[0]
    row_stack = lambda g: g.reshape(N_SHARDS, g.shape[0] // N_SHARDS, g.shape[1])

    h0 = _rms_fwd(x, row(w["mixer_norm"][0]), "mixer_norm_0")
    qkv = run("attn_qkv", _mm, h0, w["attn_w_qkv"], "nn", b_cols=True)
    q, kk, vlo, vhi = _attn_prep(qkv, cos, sin, gq, gk)
    o = run("attn_fwd", _attn_fwd, q, kk, vlo, vhi, sinks)
    x1 = run("attn_out", _mm, o, w["attn_w_o"], "nn", add=x)
    ffn_w = lambda l: (row(w["ffn_norm"][l]), w["ffn_w_gate"][l], w["ffn_w_up"][l], w["ffn_w_down"][l])
    x2, ffn0 = _ffn_fwd(run, x1, *ffn_w(0), 0)

    h2 = _rms_fwd(x2, row(w["mixer_norm"][1]), "mixer_norm_1")
    zx = run("ssm_in", _mm, h2, w["ssm_w_in"], "nn")
    dt_g = zx[:, DT_OFF:DT_OFF + SSM_HEADS].reshape(s, SSM_GROUPS, SSM_HPG).transpose(1, 0, 2)
    dt_gt = dt_g.transpose(0, 2, 1)
    bias_r, bias_c = _group_views(w["ssm_dt_bias"].reshape(-1))
    alog_r, alog_c = _group_views(w["ssm_a_log"].reshape(-1))
    d_r, _ = _group_views(w["ssm_d"].reshape(-1))
    xbc = _conv_fwd(zx, w["ssm_conv_w"], row(w["ssm_conv_b"]))
    ssd_args = (xbc, dt_g, dt_gt, bias_r, bias_c, alog_r, alog_c, d_r)
    y, states = _ssd_fwd(*ssd_args)
    yn = _gate_norm_fwd(y, zx, row(w["ssm_norm"]))
    x3 = run("ssm_out", _mm, yn, w["ssm_w_out"], "nn", add=x2)
    x4, ffn1 = _ffn_fwd(run, x3, *ffn_w(1), 1)

    loss_row, dx4 = _loss_fwd_bwd(x4, target)

    dx3, dfn1 = _ffn_bwd(run, mats, x3, *ffn_w(1), ffn1, dx4, 1)
    dyn = run("ssm_out_dx", _mm, dx3, w["ssm_w_out"], "nt")
    mats[("ssm_w_out", 0)] = row_stack(run("ssm_out_dw", _mm, yn, dx3, "tn", out_dtype=BF16))
    dy, dzx, dssm_norm = _gate_norm_bwd(y, zx, row(w["ssm_norm"]), dyn)
    dxs, db, dc, ddt_g, ddt_gt, dbias, dbias_t, dalog, dalog_t, dd = _ssd_bwd(*ssd_args, states, dy)
    ddt_g = ddt_g + ddt_gt.transpose(0, 2, 1)
    dpre, dconv_w, dconv_b = _conv_bwd_pre(zx, w["ssm_conv_w"], row(w["ssm_conv_b"]),
                                           jnp.concatenate([dxs, db, dc], axis=1))
    dzx = _conv_bwd_in(dpre, w["ssm_conv_w"], dzx)
    ddt_pad = jnp.pad(ddt_g.transpose(1, 0, 2).reshape(s, SSM_HEADS), ((0, 0), (0, SSM_IN_PAD - SSM_IN)))
    dzx = lax.dynamic_update_slice(dzx, ddt_pad.astype(BF16), (0, DT_OFF))
    dw_in = run("ssm_in_dw", _mm, h2, dzx, "tn", out_dtype=BF16)
    in_shard = SSM_IN // N_SHARDS
    mats[("ssm_w_in", 0)] = jnp.stack([dw_in[:, i * in_shard:(i + 1) * in_shard] for i in range(N_SHARDS)])
    dh2 = run("ssm_in_dx", _mm, dzx, w["ssm_w_in"], "nt")
    dx2, dmn1 = _rms_bwd(x2, row(w["mixer_norm"][1]), dh2, dx3, "mixer_norm_bwd_1")

    dx1, dfn0 = _ffn_bwd(run, mats, x1, *ffn_w(0), ffn0, dx2, 0)
    do = run("attn_out_dx", _mm, dx1, w["attn_w_o"], "nt", out_dtype=BF16)
    mats[("attn_w_o", 0)] = row_stack(run("attn_out_dw", _mm, o, dx1, "tn", out_dtype=BF16))
    dq, dkc, dkp, dvloc, dvlop, dvhic, dvhip, dsink = run("attn_bwd", _attn_bwd, q, kk, vlo, vhi, sinks, do)
    dqkv, dgq, dgk = _attn_prep_bwd(qkv, cos, sin, gq, gk, dq, (dkc, dkp), (dvloc, dvlop), (dvhic, dvhip))
    mats[("attn_w_qkv", 0)] = run("attn_qkv_dw", _mm, h0, dqkv, "tn", out_dtype=BF16, out_cols=True)
    dh0 = run("attn_qkv_dx", _mm, dqkv, w["attn_w_qkv"], "nt", b_cols=True)
    dx0, dmn0 = _rms_bwd(x, row(w["mixer_norm"][0]), dh0, dx1, "mixer_norm_bwd_0")

    fold = lambda v: v[0, :ATT_HEAD_DIM] + v[0, ATT_HEAD_DIM:]
    grads = {
        "mixer_norm": jnp.concatenate([dmn0, dmn1], axis=0),
        "ffn_norm": jnp.concatenate([dfn0, dfn1], axis=0),
        "attn_q_norm": fold(dgq), "attn_k_norm": fold(dgk),
        "attn_sinks": dsink[:, :, 0].reshape(-1),
        "ssm_conv_w": dconv_w, "ssm_conv_b": dconv_b.reshape(-1),
        "ssm_dt_bias": dbias.reshape(-1) + dbias_t.reshape(-1),
        "ssm_a_log": dalog.reshape(-1) + dalog_t.reshape(-1), "ssm_d": dd.reshape(-1),
        "ssm_norm": dssm_norm.reshape(-1),
    }
    return loss_row[0, 0], dx0, grads


OTHER_CHIPS = ((1, 0), (0, 1), (1, 1))


def _position():
    return lax.axis_index("x"), lax.axis_index("y"), lax.axis_index("c")


def _sems(n):
    return pltpu.SemaphoreType.DMA((n,))


def _gather_shards(weights, layers):
    n_in, n_mat = len(weights), len(layers)

    def body(*refs):
        p, out = refs[:n_in], refs[n_in:n_in + n_mat]
        send_sems, recv_sems = refs[n_in + n_mat:]
        x, y, c = _position()
        me, sibling = (x, y, c), (x, y, 1 - c)
        chips = [(x ^ fx, y ^ fy) for fx, fy in OTHER_CHIPS]

        def rows(e, px, py, pc):
            half = out[e].shape[1] // 2
            return out[e].at[2 * px + py, pl.ds(pc * half, half), :]

        def copy(k, e, block, to, src=None):
            return pltpu.make_async_remote_copy(
                src_ref=rows(e, *block) if src is None else src, dst_ref=rows(e, *block),
                send_sem=send_sems.at[k * n_mat + e], recv_sem=recv_sems.at[k * n_mat + e],
                device_id=to, device_id_type=MESH)

        def own(e):
            i, l = layers[e]
            return pltpu.make_async_remote_copy(
                src_ref=p[i].at[l], dst_ref=out[e].at[2 * x + y], send_sem=send_sems.at[6 * n_mat + e],
                recv_sem=recv_sems.at[6 * n_mat + e], device_id=sibling, device_id_type=MESH)

        first, passed = [], []
        for e, (i, l) in enumerate(layers):
            half = out[e].shape[1] // 2
            first.append([copy(j, e, me, (*chip, c), src=p[i].at[l, pl.ds(c * half, half), :])
                          for j, chip in enumerate(chips)])
            for cp in first[-1]:
                cp.start()
        for e in range(n_mat):
            own(e).start()
        for e in range(n_mat):
            passed.append([copy(3 + j, e, (*chip, c), sibling) for j, chip in enumerate(chips)])
            for j, chip in enumerate(chips):
                copy(j, e, (*chip, c), me).wait_recv()
                passed[e][j].start()
        for e in range(n_mat):
            own(e).wait()
            for j, chip in enumerate(chips):
                copy(3 + j, e, (*chip, 1 - c), me).wait_recv()
        for e in range(n_mat):
            for cp in first[e] + passed[e]:
                cp.wait_send()

    return pl.pallas_call(
        body, name="gather_weights", in_specs=[ANY] * n_in, out_specs=[ANY] * n_mat,
        out_shape=[jax.ShapeDtypeStruct((N_SHARDS,) + weights[i].shape[1:], weights[i].dtype) for i, _ in layers],
        scratch_shapes=[_sems(7 * n_mat), _sems(7 * n_mat)],
    )(*weights)


def _all_gather8(block, name):
    m_per, n = block.shape

    def body(x_ref, out_ref, send_sems, recv_sems, local_sem):
        x, y, c = _position()
        me, sibling = (x, y, c), (x, y, 1 - c)
        chips = [(x ^ fx, y ^ fy) for fx, fy in OTHER_CHIPS]

        def rows(px, py, pc):
            return out_ref.at[pl.ds((4 * px + 2 * py + pc) * m_per, m_per), :]

        def copy(k, blk, to, src=None):
            return pltpu.make_async_remote_copy(
                src_ref=rows(*blk) if src is None else src, dst_ref=rows(*blk),
                send_sem=send_sems.at[k], recv_sem=recv_sems.at[k], device_id=to, device_id_type=MESH)

        mine = pltpu.make_async_copy(x_ref, rows(*me), local_sem)
        mine.start()
        first = [copy(0, me, sibling, src=x_ref)]
        first += [copy(1 + j, me, (*chip, c), src=x_ref) for j, chip in enumerate(chips)]
        for cp in first:
            cp.start()
        passed = [copy(4 + j, (*chip, c), sibling) for j, chip in enumerate(chips)]
        for j, chip in enumerate(chips):
            copy(1 + j, (*chip, c), me).wait_recv()
            passed[j].start()
        copy(0, sibling, me).wait_recv()
        for j, chip in enumerate(chips):
            copy(4 + j, (*chip, 1 - c), me).wait_recv()
        for cp in first + passed:
            cp.wait_send()
        mine.wait()

    return pl.pallas_call(
        body, name=name, out_shape=jax.ShapeDtypeStruct((N_DEV * m_per, n), block.dtype),
        in_specs=[pl.BlockSpec(memory_space=pltpu.VMEM)], out_specs=pl.BlockSpec(memory_space=pltpu.VMEM),
        scratch_shapes=[_sems(7), _sems(7), pltpu.SemaphoreType.DMA],
    )(block)


def _exchange(carry, name):
    n_in, n_out = len(carry["arrays"]), len(carry["out_shapes"])

    def body(*refs):
        copies = carry["build"](refs[:n_in], refs[n_in:n_in + n_out], refs[-2], refs[-1])
        for cp in copies:
            cp.start()
        for cp in copies:
            cp.wait()

    return pl.pallas_call(
        body, name=name, in_specs=[ANY] * n_in, out_specs=[ANY] * n_out, out_shape=list(carry["out_shapes"]),
        input_output_aliases=dict(carry.get("aliases", {})),
        scratch_shapes=[_sems(carry["n_sems"]), _sems(carry["n_sems"])],
    )(*carry["arrays"])


def _remote(src, dst, send_sems, recv_sems, k, to):
    return pltpu.make_async_remote_copy(src_ref=src, dst_ref=dst, send_sem=send_sems.at[k], recv_sem=recv_sems.at[k],
                                        device_id=to, device_id_type=MESH)


def _gather_over_ici(blocks, layers):
    def build(p, out, send_sems, recv_sems):
        x, y, c = _position()
        copies = []
        for e, l in enumerate(layers):
            half = out[e].shape[1] // 2
            rows = pl.ds(c * half, half)
            for j, (fx, fy) in enumerate(OTHER_CHIPS):
                copies.append(_remote(p[e].at[l, rows, :], out[e].at[2 * x + y, rows, :], send_sems, recv_sems,
                                      3 * e + j, (x ^ fx, y ^ fy, c)))
        return copies

    shapes = [jax.ShapeDtypeStruct((N_SHARDS,) + b.shape[1:], b.dtype) for b in blocks]
    return dict(build=build, arrays=list(blocks), out_shapes=shapes, n_sems=3 * len(layers))


def _gather_over_d2d(stacks, blocks, layers):
    n = len(stacks)

    def build(refs, out, send_sems, recv_sems):
        p = refs[n:]
        x, y, c = _position()
        sibling = (x, y, 1 - c)
        copies = []
        for e, l in enumerate(layers):
            half = out[e].shape[1] // 2
            for j, (fx, fy) in enumerate(OTHER_CHIPS):
                rows = out[e].at[2 * (x ^ fx) + (y ^ fy), pl.ds(c * half, half), :]
                copies.append(_remote(rows, rows, send_sems, recv_sems, 4 * e + j, sibling))
            copies.append(_remote(p[e].at[l], out[e].at[2 * x + y], send_sems, recv_sems, 4 * e + 3, sibling))
        return copies

    shapes = [jax.ShapeDtypeStruct(s.shape, s.dtype) for s in stacks]
    return dict(build=build, arrays=list(stacks) + list(blocks), out_shapes=shapes, n_sems=4 * n,
                aliases={i: i for i in range(n)})


def _grads_to_sibling(stacks):
    def build(g, out, send_sems, recv_sems):
        x, y, c = _position()
        copies = []
        for e in range(len(stacks)):
            half = g[e].shape[1] // 2
            copies.append(_remote(g[e].at[:, pl.ds((1 - c) * half, half), :], out[e], send_sems, recv_sems, e,
                                  (x, y, 1 - c)))
        return copies

    shapes = [jax.ShapeDtypeStruct((N_SHARDS, g.shape[1] // 2, g.shape[2]), g.dtype) for g in stacks]
    return dict(build=build, arrays=list(stacks), out_shapes=shapes, n_sems=len(stacks))


def _grads_to_owners(partials):
    def build(p, out, send_sems, recv_sems):
        x, y, c = _position()
        copies = []
        for e in range(len(partials)):
            for k, (fx, fy) in enumerate(OTHER_CHIPS):
                px, py = x ^ fx, y ^ fy
                copies.append(_remote(p[e].at[2 * px + py], out[e].at[k], send_sems, recv_sems, 3 * e + k,
                                      (px, py, c)))
        return copies

    shapes = [jax.ShapeDtypeStruct((len(OTHER_CHIPS),) + p.shape[1:], p.dtype) for p in partials]
    return dict(build=build, arrays=list(partials), out_shapes=shapes, n_sems=3 * len(partials))


def _share_halves(grads, layers):
    n = len(grads)

    def body(*refs):
        out, send_sems, recv_sems = refs[n:2 * n], refs[-2], refs[-1]
        x, y, c = _position()
        copies = []
        for e, (i, l) in enumerate(layers):
            half = out[i].shape[1] // 2
            rows = out[i].at[l, pl.ds(c * half, half), :]
            copies.append(pltpu.make_async_remote_copy(
                src_ref=rows, dst_ref=rows, send_sem=send_sems.at[e], recv_sem=recv_sems.at[e],
                device_id=(x, y, 1 - c), device_id_type=MESH))
            copies[-1].start()
        for cp in copies:
            cp.wait()

    return pl.pallas_call(
        body, name="grads_share_halves", in_specs=[ANY] * n, out_specs=[ANY] * n,
        out_shape=[jax.ShapeDtypeStruct(g.shape, g.dtype) for g in grads],
        input_output_aliases={i: i for i in range(n)},
        scratch_shapes=[_sems(len(layers)), _sems(len(layers))],
    )(*grads)


ADD_BLOCK_ELEMS = 1 << 19


def _add_rows(half, cols):
    return _tile(half, max(16, ADD_BLOCK_ELEMS // cols // 16 * 16), 16)


def _add_pair(stack, recv, c_idx, name):
    _, half, cols = recv.shape
    tr = _add_rows(half, cols)
    nt = half // tr

    def body(c_ref, a_ref, b_ref, o_ref):
        o_ref[...] = (a_ref[...].astype(F32) + b_ref[...].astype(F32)).astype(o_ref.dtype)

    blk = pl.BlockSpec((None, tr, cols), lambda s, i, c_ref: (s, i, 0))
    return pl.pallas_call(
        body, name=name,
        grid_spec=pltpu.PrefetchScalarGridSpec(
            num_scalar_prefetch=1, grid=(N_SHARDS, nt),
            in_specs=[pl.BlockSpec((None, tr, cols), lambda s, i, c_ref: (s, c_ref[0] * nt + i, 0)), blk],
            out_specs=blk),
        out_shape=jax.ShapeDtypeStruct(recv.shape, recv.dtype),
        compiler_params=_params("parallel", "parallel"),
    )(c_idx, stack, recv)


def _add_owned(partial, recv, sc_idx, layer, shape, into, name):
    _, half, cols = partial.shape
    tr = _add_rows(half, cols)
    nt = half // tr

    def body(sc_ref, a_ref, r0_ref, r1_ref, r2_ref, *rest):
        o_ref = rest[-1]
        o_ref[...] = (((a_ref[...].astype(F32) + r0_ref[...].astype(F32)) + r1_ref[...].astype(F32))
                      + r2_ref[...].astype(F32))

    slot = lambda k: pl.BlockSpec((None, tr, cols), lambda i, sc_ref: (k, i, 0))
    has_into = into is not None
    return pl.pallas_call(
        body, name=name,
        grid_spec=pltpu.PrefetchScalarGridSpec(
            num_scalar_prefetch=1, grid=(nt,),
            in_specs=[pl.BlockSpec((None, tr, cols), lambda i, sc_ref: (sc_ref[0], i, 0)), slot(0), slot(1), slot(2)]
            + ([ANY] if has_into else []),
            out_specs=pl.BlockSpec((None, tr, cols), lambda i, sc_ref: (layer, sc_ref[1] * nt + i, 0))),
        out_shape=jax.ShapeDtypeStruct(shape, F32),
        input_output_aliases={5: 0} if has_into else {},
        compiler_params=_params("parallel"),
    )(*((sc_idx, partial, recv, recv, recv) + ((into,) if has_into else ())))


def _sum8(gathered):
    m = gathered.shape[0] // N_DEV

    def body(g_ref, o_ref):
        total = g_ref[0:m, :]
        for d in range(1, N_DEV):
            total = total + g_ref[d * m:(d + 1) * m, :]
        o_ref[...] = total

    return pl.pallas_call(
        body, name="small_grads_sum", out_shape=jax.ShapeDtypeStruct((m, LANES), F32),
        in_specs=[pl.BlockSpec(memory_space=pltpu.VMEM)], out_specs=pl.BlockSpec(memory_space=pltpu.VMEM),
    )(gathered)


ADAMW_BLOCK_ELEMS = 1 << 18


def _adamw(w, g, m, v, name):
    l, r, cols = w.shape
    tr = _tile(r, max(8, ADAMW_BLOCK_ELEMS // cols // 8 * 8), 8)

    def body(w_ref, g_ref, m_ref, v_ref, d_ref, nm_ref, nv_ref):
        gv = g_ref[...]
        nm = ADAM_B1 * m_ref[...] + (1.0 - ADAM_B1) * gv
        nv = ADAM_B2 * v_ref[...] + (1.0 - ADAM_B2) * jnp.square(gv)
        m_hat = nm / (1.0 - ADAM_B1 ** ADAM_STEP)
        v_hat = nv / (1.0 - ADAM_B2 ** ADAM_STEP)
        d_ref[...] = -ADAM_LR * (m_hat / (jnp.sqrt(v_hat) + ADAM_EPS) + ADAM_WD * w_ref[...])
        nm_ref[...] = nm
        nv_ref[...] = nv

    blk = pl.BlockSpec((None, tr, cols), lambda a, i: (a, i, 0))
    return pl.pallas_call(
        body, name=name, grid=(l, r // tr), in_specs=[blk] * 4, out_specs=[blk] * 3,
        out_shape=[jax.ShapeDtypeStruct(w.shape, F32)] * 3, compiler_params=_params("parallel", "parallel"),
    )(w, g, m, v)


WEIGHTS = ("mixer_norm", "ffn_norm", "attn_w_qkv", "attn_q_norm", "attn_k_norm", "attn_sinks", "attn_w_o",
           "ssm_w_in", "ssm_conv_w", "ssm_conv_b", "ssm_dt_bias", "ssm_a_log", "ssm_d", "ssm_norm", "ssm_w_out",
           "ffn_w_gate", "ffn_w_up", "ffn_w_down")
BIG = ("attn_w_qkv", "attn_w_o", "ffn_w_gate", "ffn_w_up", "ffn_w_down", "ssm_w_in", "ssm_w_out")
MATRICES = (("attn_w_qkv", 0), ("attn_w_o", 0), ("ffn_w_gate", 0), ("ffn_w_up", 0), ("ffn_w_down", 0),
            ("ssm_w_in", 0), ("ssm_w_out", 0), ("ffn_w_gate", 1), ("ffn_w_up", 1), ("ffn_w_down", 1))
MATRIX_LAYERS = tuple((BIG.index(n), l) for n, l in MATRICES)
GROUPS = {"attn": MATRICES[0:2], "ffn0": MATRICES[2:5], "ssm": MATRICES[5:7], "ffn1": MATRICES[7:10]}
SMALL_SHARDED = ("ssm_conv_w", "ssm_conv_b", "ssm_norm")
SMALL = tuple(n for n in WEIGHTS if n not in BIG)


def _pack_rows(parts, row_unit=8):
    flat = jnp.concatenate([p.reshape(-1) for p in parts])
    pad = (-flat.shape[0]) % (LANES * row_unit)
    return jnp.pad(flat, (0, pad)).reshape(-1, LANES)


def _unpack(flat, shapes):
    out, off = [], 0
    for shp in shapes:
        size = math.prod(shp)
        out.append(flat[off:off + size].reshape(shp))
        off += size
    return out


def kernel(x, positions, mixer_norm, ffn_norm, attn_w_qkv, attn_q_norm, attn_k_norm, attn_sinks, attn_w_o, ssm_w_in, ssm_conv_w, ssm_conv_b, ssm_dt_bias, ssm_a_log, ssm_d, ssm_norm, ssm_w_out, ffn_w_gate, ffn_w_up, ffn_w_down, loss_target, m_mixer_norm, m_ffn_norm, m_attn_w_qkv, m_attn_q_norm, m_attn_k_norm, m_attn_sinks, m_attn_w_o, m_ssm_w_in, m_ssm_conv_w, m_ssm_conv_b, m_ssm_dt_bias, m_ssm_a_log, m_ssm_d, m_ssm_norm, m_ssm_w_out, m_ffn_w_gate, m_ffn_w_up, m_ffn_w_down, v_mixer_norm, v_ffn_norm, v_attn_w_qkv, v_attn_q_norm, v_attn_k_norm, v_attn_sinks, v_attn_w_o, v_ssm_w_in, v_ssm_conv_w, v_ssm_conv_b, v_ssm_dt_bias, v_ssm_a_log, v_ssm_d, v_ssm_norm, v_ssm_w_out, v_ffn_w_gate, v_ffn_w_up, v_ffn_w_down):
    args = locals()
    w = {n: args[n] for n in WEIGHTS}
    m = {n: args["m_" + n] for n in WEIGHTS}
    v = {n: args["v_" + n] for n in WEIGHTS}
    ax, ay, ac = lax.axis_index("x"), lax.axis_index("y"), lax.axis_index("c")
    shard = 2 * ax + ay

    wb = {n: w[n].astype(BF16) for n in BIG}
    wl, hooks = {"ffn_w_gate": [None, None], "ffn_w_up": [None, None], "ffn_w_down": [None, None]}, {}

    def gathered(keys, stacks):
        for (n, l), st in zip(keys, stacks):
            if n == "ssm_w_in":
                wl[n] = jnp.concatenate([st[i] for i in range(N_SHARDS)]
                                        + [jnp.zeros((st.shape[1], SSM_IN_PAD - SSM_IN), BF16)], axis=1)
            elif n in ("ffn_w_gate", "ffn_w_up"):
                wl[n][l] = st
            elif n == "ffn_w_down":
                wl[n][l] = st.reshape(st.shape[0] * st.shape[1], st.shape[2])
            elif n == "attn_w_qkv":
                wl[n] = st
            else:
                wl[n] = st.reshape(st.shape[0] * st.shape[1], st.shape[2])

    def gather_behind(keys, first_leg, second_leg):
        blocks, layers, got = [wb[n] for n, _ in keys], [l for _, l in keys], {}
        hooks[first_leg] = _Hook(lambda: _gather_over_ici(blocks, layers), lambda res: got.update(stacks=res))
        hooks[second_leg] = _Hook(lambda: _gather_over_d2d(got["stacks"], blocks, layers),
                                  lambda res: gathered(keys, res))

    gathered(GROUPS["attn"], _gather_shards([wb[n] for n, _ in GROUPS["attn"]],
                                            [(e, l) for e, (_, l) in enumerate(GROUPS["attn"])]))
    gather_behind(GROUPS["ffn0"], "attn_fwd", "attn_out")
    gather_behind(GROUPS["ssm"], "ffn_gate_0", "ffn_up_0")
    gather_behind(GROUPS["ffn1"], "ssm_in", "ssm_out")
    small_shapes = [w[n].shape for n in SMALL_SHARDED]
    small_all = _all_gather8(_pack_rows([w[n] for n in SMALL_SHARDED]), "gather_small_params")
    small_all = small_all.reshape(N_DEV, -1)[::2]
    full, off = {}, 0
    for n, shp in zip(SMALL_SHARDED, small_shapes):
        size = math.prod(shp)
        seg = small_all[:, off:off + size].reshape((N_SHARDS,) + shp)
        full[n] = jnp.moveaxis(seg, 0, -2).reshape(shp[:-1] + (N_SHARDS * shp[-1],))
        off += size
    wl.update({
        "mixer_norm": mixer_norm, "ffn_norm": ffn_norm,
        "attn_q_norm": attn_q_norm[0], "attn_k_norm": attn_k_norm[0], "attn_sinks": attn_sinks[0],
        "ssm_conv_w": full["ssm_conv_w"][0], "ssm_conv_b": full["ssm_conv_b"][0],
        "ssm_dt_bias": ssm_dt_bias[0], "ssm_a_log": ssm_a_log[0], "ssm_d": ssm_d[0],
        "ssm_norm": full["ssm_norm"][0],
    })

    c_idx = ac.reshape(1).astype(jnp.int32)
    sc_idx = jnp.stack([shard, ac]).astype(jnp.int32)
    mats, halves = {}, {n: None for n in BIG}

    def pair_sums(keys, recv):
        return [_add_pair(mats[k], r, c_idx, f"grads_add_pair_{k[0]}_{k[1]}") for k, r in zip(keys, recv)]

    def owner_sums(keys, partials, recv):
        for (n, l), p, r in zip(keys, partials, recv):
            halves[n] = _add_owned(p, r, sc_idx, l, w[n].shape, halves[n], f"grads_add_owned_{n}_{l}")

    def reduce_behind(keys, first_leg, second_leg):
        got = {}
        hooks[first_leg] = _Hook(lambda: _grads_to_sibling([mats[k] for k in keys]),
                                 lambda res: got.update(partials=pair_sums(keys, res)))
        hooks[second_leg] = _Hook(lambda: _grads_to_owners(got["partials"]),
                                  lambda res: owner_sums(keys, got["partials"], res))

    reduce_behind(GROUPS["ffn1"], "ssm_out_dx", "ssm_in_dw")
    reduce_behind(GROUPS["ssm"], "ssm_in_dx", "ffn_down_dx_0")
    reduce_behind(GROUPS["ffn0"], "attn_out_dx", "attn_bwd")
    loss_part, dx, g_full = _local_step(x[0], positions[0], loss_target[0], wl, hooks, mats)
    keys = GROUPS["attn"]
    partials = pair_sums(keys, _exchange(_grads_to_sibling([mats[k] for k in keys]), "grads_to_sibling"))
    owner_sums(keys, partials, _exchange(_grads_to_owners(partials), "grads_to_owners"))
    grads = dict(zip(BIG, _share_halves([halves[n] for n in BIG], MATRIX_LAYERS)))

    small_full_shapes = [g_full[n].shape for n in SMALL] + [(1,)]
    small_g = _pack_rows([g_full[n] for n in SMALL] + [loss_part.reshape(1)])
    small_sum = _sum8(_all_gather8(small_g, "gather_small_grads")).reshape(-1)
    *small_list, loss = _unpack(small_sum, small_full_shapes)
    for n, g in zip(SMALL, small_list):
        if n in SMALL_SHARDED:
            width = w[n].shape[-1]
            g = lax.dynamic_slice_in_dim(g, shard * width, width, axis=g.ndim - 1)
        grads[n] = g.reshape(w[n].shape)

    delta, new_m, new_v = {}, {}, {}
    for n in BIG:
        delta[n], new_m[n], new_v[n] = _adamw(w[n], grads[n], m[n], v[n], "adamw_" + n)
    small_local = [w[n].shape for n in SMALL]
    pk = lambda t: _pack_rows([t[n] for n in SMALL])[None]
    outs = _adamw(pk(w), pk(grads), pk(m), pk(v), "adamw_small")
    for res, o in zip((delta, new_m, new_v), outs):
        for n, a in zip(SMALL, _unpack(o.reshape(-1), small_local)):
            res[n] = a

    return (loss.reshape(()), dx[None], *[grads[n] for n in WEIGHTS], *[delta[n] for n in WEIGHTS],
            *[new_m[n] for n in WEIGHTS], *[new_v[n] for n in WEIGHTS])
```

```python
import math

import jax
import jax.numpy as jnp
from jax import lax
from jax.experimental import pallas as pl
from jax.experimental.pallas import tpu as pltpu

F32 = jnp.float32
BF16 = jnp.bfloat16

D_MODEL = 2048
EPS = 1e-6
ATT_HEAD_DIM = 64
ATT_Q_HEADS = 32
ATT_KV_HEADS = 4
ATT_GROUP = 8
ATT_BLOCK = 128
ROPE_THETA = 10000.0
Q_WIDTH = ATT_Q_HEADS * ATT_HEAD_DIM
KV_WIDTH = ATT_KV_HEADS * ATT_HEAD_DIM
SSM_D_INNER = 4096
SSM_HEADS = 64
SSM_GROUPS = 8
SSM_HPG = 8
SSM_P = 64
SSM_STATE = 128
SSM_CONV = 4
SSM_CHUNK = 256
SSM_CONV_DIM = 6144
SSM_GN = SSM_D_INNER // SSM_GROUPS
SSM_IN = SSM_D_INNER + SSM_CONV_DIM + SSM_HEADS
LANES = 128
SSM_IN_PAD = -(-SSM_IN // LANES) * LANES
N_SHARDS = 4
N_DEV = 8

ADAM_LR = 0.001
ADAM_B1 = 0.9
ADAM_B2 = 0.999
ADAM_EPS = 1e-08
ADAM_WD = 0.01
ADAM_STEP = 10

VMEM_LIMIT = 56 * 1024 * 1024
MESH = pl.DeviceIdType.MESH
ANY = pl.BlockSpec(memory_space=pl.ANY)


def _params(*sem):
    return pltpu.CompilerParams(dimension_semantics=sem, vmem_limit_bytes=VMEM_LIMIT)


def _sems(n):
    return pltpu.SemaphoreType.DMA((n,))


def _call(body, carry, name, grid, in_specs, out_specs, out_shape, scratch_shapes, sem, args):
    if carry is None:
        return pl.pallas_call(body, name=name, grid=grid, in_specs=in_specs, out_specs=out_specs,
                              out_shape=out_shape, scratch_shapes=scratch_shapes,
                              compiler_params=_params(*sem))(*args)
    n_in, n_out, n_scr = len(in_specs), len(out_specs), len(scratch_shapes)
    c_arrays, c_shapes = list(carry["arrays"]), list(carry["out_shapes"])
    n_cin, n_cout = len(c_arrays), len(c_shapes)

    def carrying(*refs):
        ins, refs = refs[:n_in], refs[n_in:]
        cin, refs = refs[:n_cin], refs[n_cin:]
        outs, refs = refs[:n_out], refs[n_out:]
        cout, refs = refs[:n_cout], refs[n_cout:]
        scratch, (send_sems, recv_sems) = refs[:n_scr], refs[n_scr:]
        copies = carry["build"](cin, cout, send_sems, recv_sems)
        ids = [pl.program_id(d) for d in range(len(grid))]
        first, last = ids[0] == 0, ids[0] == grid[0] - 1
        for d in range(1, len(grid)):
            first = jnp.logical_and(first, ids[d] == 0)
            last = jnp.logical_and(last, ids[d] == grid[d] - 1)

        @pl.when(first)
        def _():
            for cp in copies:
                cp.start()

        body(*ins, *outs, *scratch)

        @pl.when(last)
        def _():
            for cp in copies:
                cp.wait()

    aliases = {n_in + i: n_out + o for i, o in carry.get("aliases", {}).items()}
    return pl.pallas_call(
        carrying, name=name, grid=grid, in_specs=list(in_specs) + [ANY] * n_cin,
        out_specs=list(out_specs) + [ANY] * n_cout, out_shape=list(out_shape) + c_shapes,
        scratch_shapes=list(scratch_shapes) + [_sems(carry["n_sems"]), _sems(carry["n_sems"])],
        input_output_aliases=aliases, compiler_params=_params(*(["arbitrary"] * len(grid))))(*args, *c_arrays)


def _tile(dim, target, unit=LANES):
    if dim <= target:
        return dim
    t = (target // unit) * unit
    while t >= unit:
        if dim % t == 0:
            return t
        t -= unit
    return dim


def _dot(a, b):
    return lax.dot_general(a, b, (((1,), (0,)), ((), ())), preferred_element_type=F32)


def _dot_nt(a, b):
    return lax.dot_general(a, b, (((1,), (1,)), ((), ())), preferred_element_type=F32)


def _dot_tn(a, b):
    return lax.dot_general(a, b, (((0,), (0,)), ((), ())), preferred_element_type=F32)


def _split3(x):
    hi = x.astype(BF16)
    r1 = x - hi.astype(F32)
    mid = r1.astype(BF16)
    lo = (r1 - mid.astype(F32)).astype(BF16)
    return hi, mid, lo


def _dot_x(x, m):
    hi, mid, lo = _split3(x)
    return _dot(hi, m) + _dot(mid, m) + _dot(lo, m)


def _xdot(m, x):
    hi, mid, lo = _split3(x)
    return _dot(m, hi) + _dot(m, mid) + _dot(m, lo)


def _dot_x_nt(x, m):
    hi, mid, lo = _split3(x)
    return _dot_nt(hi, m) + _dot_nt(mid, m) + _dot_nt(lo, m)


def _iota(shape, dim):
    return lax.broadcasted_iota(jnp.int32, shape, dim)


def _sigmoid(x):
    return 1.0 / (1.0 + jnp.exp(-x))


def _softplus(x):
    return jnp.maximum(x, 0.0) + jnp.log(1.0 + jnp.exp(-jnp.abs(x)))


MM_ROWS = 1024
MM_TILE = 1408
MM_DEPTH = 2816
FUSED_ROWS = 512


def _mm(a, b, mode, name, add=None, out_dtype=F32, b_cols=False, out_cols=False, fuse=None, rows=MM_ROWS,
        carry=None):
    bs = b.shape[-2:]
    if b_cols:
        bs = (bs[0], N_SHARDS * bs[1])
    if mode == "nn":
        (m, k), (k2, n) = a.shape, bs
    elif mode == "nt":
        (m, k), (n, k2) = a.shape, bs
    else:
        (k, m), (k2, n) = a.shape, bs
    assert k == k2, (a.shape, b.shape, mode)
    split_n = (b_cols and mode == "nn") or out_cols
    split_k = b_cols and mode == "nt"
    tm = _tile(m, MM_TILE if mode == "tn" else rows)
    tn = _tile(n // N_SHARDS if split_n else n, MM_TILE)
    tk = _tile(k // N_SHARDS if split_k else k, MM_DEPTH)
    nk = k // tk
    nj, nq = (n // N_SHARDS) // tn, (k // N_SHARDS) // tk
    if mode == "tn":
        a_spec = pl.BlockSpec((tk, tm), lambda i, j, q: (q, i))
    else:
        a_spec = pl.BlockSpec((tm, tk), lambda i, j, q: (i, q))
    if mode == "nt":
        if b_cols:
            b_spec = pl.BlockSpec((None, tn, tk), lambda i, j, q: (q // nq, j, q % nq))
        else:
            b_spec = pl.BlockSpec((tn, tk), lambda i, j, q: (j, q))
    elif b_cols:
        b_spec = pl.BlockSpec((None, tk, tn), lambda i, j, q: (j // nj, q, j % nj))
    else:
        b_spec = pl.BlockSpec((tk, tn), lambda i, j, q: (q, j))
    add_spec = pl.BlockSpec((tm, tn), lambda i, j, q: (i, j))
    if out_cols:
        o_spec = pl.BlockSpec((None, tm, tn), lambda i, j, q: (j // nj, i, j % nj))
        o_shape = (N_SHARDS, m, n // N_SHARDS)
    else:
        o_spec, o_shape = add_spec, (m, n)
    dot = {"nn": _dot, "nt": _dot_nt, "tn": _dot_tn}[mode]
    has_add = add is not None
    fuse_fn, extra, out_dtypes = fuse if fuse is not None else (None, [], [out_dtype])
    n_in, n_out = 2 + has_add + len(extra), len(out_dtypes)

    def body(*refs):
        a_ref, b_ref = refs[:2]
        add_ref = refs[2] if has_add else None
        extra_refs = refs[2 + has_add:n_in]
        o_refs, acc_ref = refs[n_in:n_in + n_out], refs[n_in + n_out]
        part = dot(a_ref[...].astype(BF16), b_ref[...].astype(BF16))

        def finish(total):
            if has_add:
                total = total + add_ref[...].astype(F32)
            outs = (total,) if fuse_fn is None else fuse_fn(total, *[r[...] for r in extra_refs])
            for o_ref, val in zip(o_refs, outs):
                o_ref[...] = val.astype(o_ref.dtype)

        if nk == 1:
            finish(part)
        else:
            q = pl.program_id(2)

            @pl.when(q == 0)
            def _():
                acc_ref[...] = part

            @pl.when(jnp.logical_and(q > 0, q < nk - 1))
            def _():
                acc_ref[...] += part

            @pl.when(q == nk - 1)
            def _():
                finish(acc_ref[...] + part)

    in_specs = [a_spec, b_spec] + [add_spec] * (has_add + len(extra))
    args = (a, b) + ((add,) if has_add else ()) + tuple(extra)
    res = _call(body, carry, name, (m // tm, n // tn, nk), in_specs, [o_spec] * n_out,
                [jax.ShapeDtypeStruct(o_shape, dt) for dt in out_dtypes],
                [pltpu.VMEM((tm, tn) if nk > 1 else (8, LANES), F32)], ("parallel", "parallel", "arbitrary"), args)
    main = res[0] if fuse is None else res[:n_out]
    return main if carry is None else (main, res[n_out:])


def _rms_fwd(x, g, name):
    s, d = x.shape
    ts = _tile(s, 512, 8)

    def body(x_ref, g_ref, o_ref):
        xv = x_ref[...]
        r = lax.rsqrt(jnp.mean(xv * xv, axis=-1, keepdims=True) + EPS)
        o_ref[...] = (xv * r * g_ref[...]).astype(BF16)

    return pl.pallas_call(
        body, name=name, grid=(s // ts,),
        in_specs=[pl.BlockSpec((ts, d), lambda i: (i, 0)), pl.BlockSpec((1, d), lambda i: (0, 0))],
        out_specs=pl.BlockSpec((ts, d), lambda i: (i, 0)),
        out_shape=jax.ShapeDtypeStruct((s, d), BF16),
        compiler_params=_params("parallel"),
    )(x, g)


def _rms_bwd(x, g, dh, dres, name):
    s, d = x.shape
    ts = _tile(s, 512, 8)

    def body(x_ref, g_ref, dh_ref, dres_ref, dx_ref, dg_ref):
        xv = x_ref[...]
        r = lax.rsqrt(jnp.mean(xv * xv, axis=-1, keepdims=True) + EPS)
        xhat = xv * r
        dhv = dh_ref[...].astype(F32)
        part = jnp.sum(dhv * xhat, axis=0, keepdims=True)

        @pl.when(pl.program_id(0) == 0)
        def _():
            dg_ref[...] = part

        @pl.when(pl.program_id(0) > 0)
        def _():
            dg_ref[...] += part

        dxh = dhv * g_ref[...]
        dx = r * (dxh - xhat * jnp.mean(dxh * xhat, axis=-1, keepdims=True))
        dx_ref[...] = dres_ref[...] + dx

    row = pl.BlockSpec((ts, d), lambda i: (i, 0))
    vec = pl.BlockSpec((1, d), lambda i: (0, 0))
    return pl.pallas_call(
        body, name=name, grid=(s // ts,),
        in_specs=[row, vec, row, row], out_specs=[row, vec],
        out_shape=[jax.ShapeDtypeStruct((s, d), F32), jax.ShapeDtypeStruct((1, d), F32)],
        compiler_params=_params("arbitrary"),
    )(x, g, dh, dres)


def _act_fwd(g, u, name):
    s, f = g.shape
    ts, tf = _tile(s, 512, 8), _tile(f, 1408)

    def body(g_ref, u_ref, o_ref):
        gv = g_ref[...]
        o_ref[...] = (gv * _sigmoid(gv) * u_ref[...]).astype(BF16)

    blk = pl.BlockSpec((ts, tf), lambda i, j: (i, j))
    return pl.pallas_call(
        body, name=name, grid=(s // ts, f // tf), in_specs=[blk, blk], out_specs=blk,
        out_shape=jax.ShapeDtypeStruct((s, f), BF16), compiler_params=_params("parallel", "parallel"),
    )(g, u)


def _act_bwd(g, u, da, name):
    s, f = g.shape
    ts, tf = _tile(s, 512, 8), _tile(f, 1408)

    def body(g_ref, u_ref, da_ref, dg_ref, du_ref):
        gv, uv, dav = g_ref[...], u_ref[...], da_ref[...].astype(F32)
        sg = _sigmoid(gv)
        silu = gv * sg
        du_ref[...] = (dav * silu).astype(BF16)
        dg_ref[...] = (dav * uv * sg * (1.0 + gv * (1.0 - sg))).astype(BF16)

    blk = pl.BlockSpec((ts, tf), lambda i, j: (i, j))
    return pl.pallas_call(
        body, name=name, grid=(s // ts, f // tf), in_specs=[blk, blk, blk], out_specs=[blk, blk],
        out_shape=[jax.ShapeDtypeStruct((s, f), BF16)] * 2, compiler_params=_params("parallel", "parallel"),
    )(g, u, da)


def _loss_fwd_bwd(y, target):
    s, d = y.shape
    ts = _tile(s, 512, 8)

    def body(y_ref, t_ref, l_ref, dy_ref):
        diff = y_ref[...] - t_ref[...]
        dy_ref[...] = diff * (1.0 / d)
        part = jnp.full((1, LANES), 0.5 * jnp.sum(jnp.mean(diff * diff, axis=-1, keepdims=True)), F32)

        @pl.when(pl.program_id(0) == 0)
        def _():
            l_ref[...] = part

        @pl.when(pl.program_id(0) > 0)
        def _():
            l_ref[...] += part

    row = pl.BlockSpec((ts, d), lambda i: (i, 0))
    acc = pl.BlockSpec((1, LANES), lambda i: (0, 0))
    return pl.pallas_call(
        body, name="loss", grid=(s // ts,), in_specs=[row, row], out_specs=[acc, row],
        out_shape=[jax.ShapeDtypeStruct((1, LANES), F32), jax.ShapeDtypeStruct((s, d), F32)],
        compiler_params=_params("arbitrary"),
    )(y, target)


def _lane_consts():
    r, c = _iota((LANES, LANES), 0), _iota((LANES, LANES), 1)
    same = (r >> 6) == (c >> 6)
    rin, cin = r & 63, c & 63
    one = lambda cond: jnp.where(cond, 1.0, 0.0).astype(BF16)
    return dict(
        seg=one(same),
        rot=(jnp.where(same & (rin == cin + 32), -1.0, 0.0)
             + jnp.where(same & (cin == rin + 32), 1.0, 0.0)).astype(BF16),
        dup_lo=one(r == cin), dup_hi=one(r == cin + 64),
        up=one((c >= 64) & (r == c - 64)), down=one((c < 64) & (r == c + 64)),
        fold_lo=one((c < 64) & (rin == c)), fold_hi=one((c >= 64) & (rin == c - 64)),
    )


def _norm_rope(xc, gain, cos, sin, k):
    ss = _dot_x(xc * xc, k["seg"])
    rinv = lax.rsqrt(ss * (1.0 / ATT_HEAD_DIM) + EPS)
    xhat = xc * rinv
    y = xhat * gain
    return y * cos + _dot_x(y, k["rot"]) * sin, xhat, rinv


def _norm_rope_bwd(dr, xhat, rinv, gain, cos, sin, k):
    dy = dr * cos - _dot_x(dr * sin, k["rot"])
    dgain = jnp.sum(dy * xhat, axis=0, keepdims=True)
    dxh = dy * gain
    dx = rinv * (dxh - xhat * (_dot_x(dxh * xhat, k["seg"]) * (1.0 / ATT_HEAD_DIM)))
    return dx, dgain


def _attn_prep(qkv, cos, sin, gq, gk):
    s = qkv.shape[0]
    tr = _tile(s, 256, 8)

    def body(x_ref, cos_ref, sin_ref, gq_ref, gk_ref, q_ref, kk_ref, vlo_ref, vhi_ref):
        k = _lane_consts()
        cosv, sinv = cos_ref[...], sin_ref[...]
        lane = _iota((tr, LANES), 1)
        for j in range(Q_WIDTH // LANES):
            r, _, _ = _norm_rope(x_ref[:, j * LANES:(j + 1) * LANES], gq_ref[...], cosv, sinv, k)
            q_ref[:, j * LANES:(j + 1) * LANES] = r.astype(BF16)
        for i in range(KV_WIDTH // LANES):
            off = Q_WIDTH + i * LANES
            r, _, _ = _norm_rope(x_ref[:, off:off + LANES], gk_ref[...], cosv, sinv, k)
            rb = r.astype(BF16)
            kk_ref[:, (2 * i) * LANES:(2 * i + 1) * LANES] = _dot(rb, k["dup_lo"]).astype(BF16)
            kk_ref[:, (2 * i + 1) * LANES:(2 * i + 2) * LANES] = _dot(rb, k["dup_hi"]).astype(BF16)
            off = Q_WIDTH + KV_WIDTH + i * LANES
            vb = x_ref[:, off:off + LANES].astype(BF16)
            zero = jnp.zeros_like(vb)
            vlo_ref[:, (2 * i) * LANES:(2 * i + 1) * LANES] = jnp.where(lane < 64, vb, zero)
            vhi_ref[:, (2 * i) * LANES:(2 * i + 1) * LANES] = _dot(vb, k["up"]).astype(BF16)
            vlo_ref[:, (2 * i + 1) * LANES:(2 * i + 2) * LANES] = _dot(vb, k["down"]).astype(BF16)
            vhi_ref[:, (2 * i + 1) * LANES:(2 * i + 2) * LANES] = jnp.where(lane >= 64, vb, zero)

    w = qkv.shape[1]
    row = lambda width: pl.BlockSpec((tr, width), lambda i: (i, 0))
    vec = pl.BlockSpec((1, LANES), lambda i: (0, 0))
    kw = ATT_KV_HEADS * LANES
    return pl.pallas_call(
        body, name="attn_prep", grid=(s // tr,),
        in_specs=[row(w), row(LANES), row(LANES), vec, vec],
        out_specs=[row(Q_WIDTH), row(kw), row(kw), row(kw)],
        out_shape=[jax.ShapeDtypeStruct((s, Q_WIDTH), BF16)] + [jax.ShapeDtypeStruct((s, kw), BF16)] * 3,
        compiler_params=_params("parallel"),
    )(qkv, cos, sin, gq, gk)


def _band_mask(n):
    qi = _iota((ATT_BLOCK, 2 * ATT_BLOCK), 0)
    kj = _iota((ATT_BLOCK, 2 * ATT_BLOCK), 1)
    band = (kj > qi) & (kj <= qi + ATT_BLOCK)
    return band & ((kj >= ATT_BLOCK) | (n > 0))


def _softmax_sink(s, valid, sink):
    s = jnp.where(valid, s, -jnp.inf)
    m = jnp.maximum(jnp.max(s, axis=-1, keepdims=True), sink)
    p = jnp.exp(s - m)
    esink = jnp.exp(sink - m)
    inv = 1.0 / (jnp.sum(p, axis=-1, keepdims=True) + esink)
    return p * inv, esink * inv


def _attn_specs(order):
    if order == "nh":
        cur = lambda n, h: (n, h)
        prev = lambda n, h: (jnp.maximum(n - 1, 0), h)
    else:
        cur = lambda h, n: (n, h)
        prev = lambda h, n: (jnp.maximum(n - 1, 0), h)
    qs = pl.BlockSpec((ATT_BLOCK, ATT_GROUP * ATT_HEAD_DIM), cur)
    kc = pl.BlockSpec((ATT_BLOCK, LANES), cur)
    kp = pl.BlockSpec((ATT_BLOCK, LANES), prev)
    return qs, kc, kp


def _attn_fwd(q, kk, vlo, vhi, sinks, name="attn_fwd", carry=None):
    s = q.shape[0]
    nb = s // ATT_BLOCK
    scale = ATT_HEAD_DIM ** -0.5

    def body(sink_ref, q_ref, kc_ref, kp_ref, vloc_ref, vlop_ref, vhic_ref, vhip_ref, o_ref):
        n, h = pl.program_id(0), pl.program_id(1)
        valid = _band_mask(n)
        kw = jnp.concatenate([kp_ref[...], kc_ref[...]], axis=0)
        vw = (jnp.concatenate([vlop_ref[...], vloc_ref[...]], axis=0),
              jnp.concatenate([vhip_ref[...], vhic_ref[...]], axis=0))
        lane = _iota((ATT_BLOCK, LANES), 1)
        for jp in range(ATT_GROUP // 2):
            qp = q_ref[:, jp * LANES:(jp + 1) * LANES]
            acc = jnp.zeros((ATT_BLOCK, LANES), F32)
            for hf in range(2):
                qm = jnp.where((lane >= 64) == (hf == 1), qp, jnp.zeros_like(qp))
                sc = _dot_nt(qm, kw) * scale
                probs, _ = _softmax_sink(sc, valid, sink_ref[h * ATT_GROUP + 2 * jp + hf])
                acc = acc + _dot(probs.astype(BF16), vw[hf])
            o_ref[:, jp * LANES:(jp + 1) * LANES] = acc.astype(BF16)

    qs, kc, kp = _attn_specs("nh")
    res = _call(body, carry, name, (nb, ATT_KV_HEADS),
                [pl.BlockSpec(memory_space=pltpu.SMEM), qs, kc, kp, kc, kp, kc, kp], [qs],
                [jax.ShapeDtypeStruct((s, Q_WIDTH), BF16)], [], ("parallel", "parallel"),
                (sinks, q, kk, kk, vlo, vlo, vhi, vhi))
    return res[0] if carry is None else (res[0], res[1:])


def _attn_bwd(q, kk, vlo, vhi, sinks, do, name="attn_bwd", carry=None):
    s = q.shape[0]
    nb = s // ATT_BLOCK
    scale = ATT_HEAD_DIM ** -0.5

    def body(sink_ref, q_ref, kc_ref, kp_ref, vloc_ref, vlop_ref, vhic_ref, vhip_ref, do_ref,
             dq_ref, dkc_ref, dkp_ref, dvloc_ref, dvlop_ref, dvhic_ref, dvhip_ref, dsink_ref):
        h, n = pl.program_id(0), pl.program_id(1)
        valid = _band_mask(n)
        kw = jnp.concatenate([kp_ref[...], kc_ref[...]], axis=0)
        vw = (jnp.concatenate([vlop_ref[...], vloc_ref[...]], axis=0),
              jnp.concatenate([vhip_ref[...], vhic_ref[...]], axis=0))
        lane = _iota((ATT_BLOCK, LANES), 1)
        sub = _iota((ATT_GROUP, LANES), 0)
        dkk = jnp.zeros((2 * ATT_BLOCK, LANES), F32)
        dv = [jnp.zeros((2 * ATT_BLOCK, LANES), F32), jnp.zeros((2 * ATT_BLOCK, LANES), F32)]
        dsink = jnp.zeros((ATT_GROUP, LANES), F32)
        for jp in range(ATT_GROUP // 2):
            qp = q_ref[:, jp * LANES:(jp + 1) * LANES]
            dop = do_ref[:, jp * LANES:(jp + 1) * LANES]
            dq = jnp.zeros((ATT_BLOCK, LANES), F32)
            for hf in range(2):
                mine = (lane >= 64) == (hf == 1)
                qm = jnp.where(mine, qp, jnp.zeros_like(qp))
                sc = _dot_nt(qm, kw) * scale
                probs, psink = _softmax_sink(sc, valid, sink_ref[h * ATT_GROUP + 2 * jp + hf])
                pb = probs.astype(BF16)
                dprobs = _dot_nt(dop, vw[hf])
                dv[hf] = dv[hf] + _dot_tn(pb, dop)
                delta = jnp.sum(probs * dprobs, axis=-1, keepdims=True)
                ds = (probs * (dprobs - delta) * scale).astype(BF16)
                dsink = dsink + jnp.where(sub == 2 * jp + hf, -jnp.sum(psink * delta), 0.0)
                dq = dq + jnp.where(mine, _dot(ds, kw), 0.0)
                dkk = dkk + _dot_tn(ds, qm)
            dq_ref[:, jp * LANES:(jp + 1) * LANES] = dq
        dkp_ref[...], dkc_ref[...] = dkk[:ATT_BLOCK], dkk[ATT_BLOCK:]
        dvlop_ref[...], dvloc_ref[...] = dv[0][:ATT_BLOCK], dv[0][ATT_BLOCK:]
        dvhip_ref[...], dvhic_ref[...] = dv[1][:ATT_BLOCK], dv[1][ATT_BLOCK:]

        @pl.when(n == 0)
        def _():
            dsink_ref[0] = dsink

        @pl.when(n > 0)
        def _():
            dsink_ref[0] += dsink

    qs, kc, kp = _attn_specs("hn")
    kw_shape = jax.ShapeDtypeStruct((s, ATT_KV_HEADS * LANES), F32)
    res = _call(body, carry, name, (ATT_KV_HEADS, nb),
                [pl.BlockSpec(memory_space=pltpu.SMEM), qs, kc, kp, kc, kp, kc, kp, qs],
                [qs] + [kc] * 6 + [pl.BlockSpec((1, ATT_GROUP, LANES), lambda h, n: (h, 0, 0))],
                [jax.ShapeDtypeStruct((s, Q_WIDTH), F32)] + [kw_shape] * 6
                + [jax.ShapeDtypeStruct((ATT_KV_HEADS, ATT_GROUP, LANES), F32)], [], ("parallel", "arbitrary"),
                (sinks, q, kk, kk, vlo, vlo, vhi, vhi, do))
    return res if carry is None else (res[:8], res[8:])


def _attn_prep_bwd(qkv, cos, sin, gq, gk, dq, dks, dvlos, dvhis):
    s, w = qkv.shape
    tr = ATT_BLOCK
    nb = s // tr

    def body(x_ref, cos_ref, sin_ref, gq_ref, gk_ref, dq_ref, dkc_ref, dkn_ref, dvloc_ref, dvlon_ref,
             dvhic_ref, dvhin_ref, dx_ref, dgq_ref, dgk_ref):
        n = pl.program_id(0)
        k = _lane_consts()
        cosv, sinv = cos_ref[...], sin_ref[...]
        nxt = jnp.where(n < nb - 1, 1.0, 0.0)
        lane = _iota((tr, LANES), 1)
        dgq = jnp.zeros((1, LANES), F32)
        dgk = jnp.zeros((1, LANES), F32)
        for j in range(Q_WIDTH // LANES):
            sl = slice(j * LANES, (j + 1) * LANES)
            _, xhat, rinv = _norm_rope(x_ref[:, sl], gq_ref[...], cosv, sinv, k)
            dx, dg = _norm_rope_bwd(dq_ref[:, sl], xhat, rinv, gq_ref[...], cosv, sinv, k)
            dx_ref[:, sl] = dx.astype(BF16)
            dgq = dgq + dg
        for i in range(KV_WIDTH // LANES):
            a, b = slice(2 * i * LANES, (2 * i + 1) * LANES), slice((2 * i + 1) * LANES, (2 * i + 2) * LANES)
            dr = (_dot_x(dkc_ref[:, a] + nxt * dkn_ref[:, a], k["fold_lo"])
                  + _dot_x(dkc_ref[:, b] + nxt * dkn_ref[:, b], k["fold_hi"]))
            sl = slice(Q_WIDTH + i * LANES, Q_WIDTH + (i + 1) * LANES)
            _, xhat, rinv = _norm_rope(x_ref[:, sl], gk_ref[...], cosv, sinv, k)
            dx, dg = _norm_rope_bwd(dr, xhat, rinv, gk_ref[...], cosv, sinv, k)
            dx_ref[:, sl] = dx.astype(BF16)
            dgk = dgk + dg
            ta = jnp.where(lane < 64, dvloc_ref[:, a] + nxt * dvlon_ref[:, a], dvhic_ref[:, a] + nxt * dvhin_ref[:, a])
            tb = jnp.where(lane < 64, dvloc_ref[:, b] + nxt * dvlon_ref[:, b], dvhic_ref[:, b] + nxt * dvhin_ref[:, b])
            sl = slice(Q_WIDTH + KV_WIDTH + i * LANES, Q_WIDTH + KV_WIDTH + (i + 1) * LANES)
            dx_ref[:, sl] = (_dot_x(ta, k["fold_lo"]) + _dot_x(tb, k["fold_hi"])).astype(BF16)

        @pl.when(n == 0)
        def _():
            dgq_ref[...] = dgq
            dgk_ref[...] = dgk

        @pl.when(n > 0)
        def _():
            dgq_ref[...] += dgq
            dgk_ref[...] += dgk

    row = lambda width: pl.BlockSpec((tr, width), lambda i: (i, 0))
    nxt_row = pl.BlockSpec((tr, ATT_KV_HEADS * LANES), lambda i: (jnp.minimum(i + 1, nb - 1), 0))
    vec = pl.BlockSpec((1, LANES), lambda i: (0, 0))
    kw = ATT_KV_HEADS * LANES
    return pl.pallas_call(
        body, name="attn_prep_bwd", grid=(nb,),
        in_specs=[row(w), row(LANES), row(LANES), vec, vec, row(Q_WIDTH),
                  row(kw), nxt_row, row(kw), nxt_row, row(kw), nxt_row],
        out_specs=[row(w), vec, vec],
        out_shape=[jax.ShapeDtypeStruct((s, w), BF16), jax.ShapeDtypeStruct((1, LANES), F32),
                   jax.ShapeDtypeStruct((1, LANES), F32)],
        compiler_params=_params("arbitrary"),
    )(qkv, cos, sin, gq, gk, dq, dks[0], dks[1], dvlos[0], dvlos[1], dvhis[0], dvhis[1])


CONV_HALO = 8
CONV_TC = 512
XBC_OFF = SSM_D_INNER // CONV_TC
DT_OFF = SSM_D_INNER + SSM_CONV_DIM


def _conv_pre(ext, w_ref, b_ref, ts):
    pre = b_ref[...] + w_ref[SSM_CONV - 1:SSM_CONV, :] * ext[CONV_HALO:]
    for kk in range(SSM_CONV - 1):
        pre = pre + w_ref[kk:kk + 1, :] * pltpu.roll(ext, SSM_CONV - 1 - kk, 0)[CONV_HALO:]
    return pre


def _conv_specs(ts):
    tc = CONV_TC
    src = pl.BlockSpec((ts, tc), lambda j, i: (i, XBC_OFF + j))
    halo = pl.BlockSpec((CONV_HALO, tc), lambda j, i: (jnp.maximum(i * (ts // CONV_HALO) - 1, 0), XBC_OFF + j))
    blk = pl.BlockSpec((ts, tc), lambda j, i: (i, j))
    wspec = pl.BlockSpec((SSM_CONV, tc), lambda j, i: (0, j))
    bspec = pl.BlockSpec((1, tc), lambda j, i: (0, j))
    return src, halo, blk, wspec, bspec


def _conv_fwd(zx, w, b):
    s, c = zx.shape[0], SSM_CONV_DIM
    ts = _tile(s, 512, 8)

    def body(u_ref, halo_ref, w_ref, b_ref, o_ref):
        halo = jnp.where(pl.program_id(1) > 0, halo_ref[...], 0.0)
        pre = _conv_pre(jnp.concatenate([halo, u_ref[...]], axis=0), w_ref, b_ref, ts)
        o_ref[...] = pre * _sigmoid(pre)

    src, halo, blk, wspec, bspec = _conv_specs(ts)
    return pl.pallas_call(
        body, name="conv_fwd", grid=(c // CONV_TC, s // ts),
        in_specs=[src, halo, wspec, bspec], out_specs=blk, out_shape=jax.ShapeDtypeStruct((s, c), F32),
        compiler_params=_params("parallel", "parallel"),
    )(zx, zx, w, b)


def _conv_bwd_pre(zx, w, b, dact):
    s, c = zx.shape[0], SSM_CONV_DIM
    ts = _tile(s, 512, 8)

    def body(u_ref, halo_ref, w_ref, b_ref, da_ref, dpre_ref, dw_ref, db_ref):
        i = pl.program_id(1)
        halo = jnp.where(i > 0, halo_ref[...], 0.0)
        ext = jnp.concatenate([halo, u_ref[...]], axis=0)
        pre = _conv_pre(ext, w_ref, b_ref, ts)
        sg = _sigmoid(pre)
        dpre = da_ref[...] * sg * (1.0 + pre * (1.0 - sg))
        dpre_ref[...] = dpre
        rows = [jnp.sum(dpre * pltpu.roll(ext, SSM_CONV - 1 - kk, 0)[CONV_HALO:], axis=0, keepdims=True)
                for kk in range(SSM_CONV - 1)]
        rows.append(jnp.sum(dpre * ext[CONV_HALO:], axis=0, keepdims=True))
        dwp = jnp.concatenate(rows, axis=0)
        dbp = jnp.sum(dpre, axis=0, keepdims=True)

        @pl.when(i == 0)
        def _():
            dw_ref[...] = dwp
            db_ref[...] = dbp

        @pl.when(i > 0)
        def _():
            dw_ref[...] += dwp
            db_ref[...] += dbp

    src, halo, blk, wspec, bspec = _conv_specs(ts)
    return pl.pallas_call(
        body, name="conv_bwd_pre", grid=(c // CONV_TC, s // ts),
        in_specs=[src, halo, wspec, bspec, blk], out_specs=[blk, wspec, bspec],
        out_shape=[jax.ShapeDtypeStruct((s, c), F32), jax.ShapeDtypeStruct((SSM_CONV, c), F32),
                   jax.ShapeDtypeStruct((1, c), F32)],
        compiler_params=_params("parallel", "arbitrary"),
    )(zx, zx, w, b, dact)


def _conv_bwd_in(dpre, w, dzx):
    s, c = dpre.shape
    ts, tc = _tile(s, 512, 8), CONV_TC
    ns = s // ts

    def body(d_ref, halo_ref, w_ref, dzx_ref, o_ref):
        del dzx_ref
        halo = jnp.where(pl.program_id(1) < ns - 1, halo_ref[...], 0.0)
        ext = jnp.concatenate([d_ref[...], halo], axis=0)
        du = w_ref[SSM_CONV - 1:SSM_CONV, :] * ext[:ts]
        for kk in range(SSM_CONV - 1):
            du = du + w_ref[kk:kk + 1, :] * pltpu.roll(ext, ts + CONV_HALO - (SSM_CONV - 1 - kk), 0)[:ts]
        o_ref[...] = du.astype(BF16)

    blk = pl.BlockSpec((ts, tc), lambda j, i: (i, j))
    halo = pl.BlockSpec((CONV_HALO, tc), lambda j, i: (jnp.minimum((i + 1) * (ts // CONV_HALO), s // CONV_HALO - 1), j))
    return pl.pallas_call(
        body, name="conv_bwd_in", grid=(c // tc, ns),
        in_specs=[blk, halo, pl.BlockSpec((SSM_CONV, tc), lambda j, i: (0, j)), ANY],
        out_specs=pl.BlockSpec((ts, tc), lambda j, i: (i, XBC_OFF + j)),
        out_shape=jax.ShapeDtypeStruct(dzx.shape, BF16), input_output_aliases={3: 0},
        compiler_params=_params("parallel", "parallel"),
    )(dpre, dpre, w, dzx)


def _ssd_common(dt_ref, dtt_ref, bias_ref, biast_ref, alog_ref, alogt_ref):
    ln = SSM_CHUNK
    raw, rawt = dt_ref[0] + bias_ref[0], dtt_ref[0] + biast_ref[0]
    dt, dtt = _softplus(raw), _softplus(rawt)
    a, at = -jnp.exp(alog_ref[0]), -jnp.exp(alogt_ref[0])
    tri = jnp.where(_iota((ln, ln), 0) >= _iota((ln, ln), 1), 1.0, 0.0).astype(BF16)
    return dict(raw=raw, rawt=rawt, dt=dt, dtt=dtt, a=a, at=at, tri=tri,
                acum=_xdot(tri, dt * a), acumt=_dot_x_nt(dtt * at, tri))


def _ssd_specs(nc, rev):
    cidx = (lambda c: nc - 1 - c) if rev else (lambda c: c)
    ln = SSM_CHUNK
    xs = pl.BlockSpec((ln, SSM_GN), lambda g, c: (cidx(c), g))
    bs = pl.BlockSpec((ln, SSM_STATE), lambda g, c: (cidx(c), SSM_D_INNER // SSM_STATE + g))
    cs = pl.BlockSpec((ln, SSM_STATE), lambda g, c: (cidx(c), SSM_D_INNER // SSM_STATE + SSM_GROUPS + g))
    dt = pl.BlockSpec((1, ln, SSM_HPG), lambda g, c: (g, cidx(c), 0))
    dtt = pl.BlockSpec((1, SSM_HPG, ln), lambda g, c: (g, 0, cidx(c)))
    row = pl.BlockSpec((1, 1, SSM_HPG), lambda g, c: (g, 0, 0))
    col = pl.BlockSpec((1, SSM_HPG, 1), lambda g, c: (g, 0, 0))
    st = pl.BlockSpec((None, None, SSM_GN, SSM_STATE), lambda g, c: (cidx(c), g, 0, 0))
    return xs, bs, cs, dt, dtt, row, col, st


def _head_expand():
    return jnp.where((_iota((SSM_HPG, SSM_GN), 1) >> 6) == _iota((SSM_HPG, SSM_GN), 0), 1.0, 0.0).astype(BF16)


def _head_expand_t():
    return jnp.where((_iota((SSM_GN, SSM_HPG), 0) >> 6) == _iota((SSM_GN, SSM_HPG), 1), 1.0, 0.0).astype(BF16)


def _dot_x_tn(x, m):
    hi, mid, lo = _split3(x)
    return _dot_tn(hi, m) + _dot_tn(mid, m) + _dot_tn(lo, m)


def _ssd_fwd(xbc, dt_g, dt_gt, bias_r, bias_c, alog_r, alog_c, d_r):
    s = xbc.shape[0]
    ln = SSM_CHUNK
    nc = s // ln

    def body(x_ref, b_ref, c_ref, dt_ref, dtt_ref, bias_ref, biast_ref, alog_ref, alogt_ref, d_ref,
             y_ref, st_ref, state):
        @pl.when(pl.program_id(1) == 0)
        def _():
            state[...] = jnp.zeros_like(state)

        cm = _ssd_common(dt_ref, dtt_ref, bias_ref, biast_ref, alog_ref, alogt_ref)
        acum, acumt = cm["acum"], cm["acumt"]
        ex = _head_expand()
        acum_x = _dot_x(acum, ex)
        xv = x_ref[...]
        xdt = xv * _dot_x(cm["dt"], ex)
        xdtb = xdt.astype(BF16)
        bb, cb = b_ref[...].astype(BF16), c_ref[...].astype(BF16)
        cbm = _dot_nt(cb, bb)
        causal = _iota((ln, ln), 0) >= _iota((ln, ln), 1)
        s2 = state[...]
        st_ref[...] = s2
        for r in range(SSM_HPG):
            sl = slice(r * SSM_P, (r + 1) * SSM_P)
            decay = jnp.exp(jnp.where(causal, acum[:, r:r + 1] - acumt[r:r + 1, :], -jnp.inf))
            y_ref[:, sl] = _dot((cbm * decay).astype(BF16), xdtb[:, sl])
        y_ref[...] = (y_ref[...] + _dot_nt(cb, s2.astype(BF16)) * jnp.exp(acum_x) + _dot_x(d_ref[0], ex) * xv)
        last_x = acum_x[ln - 1:ln, :]
        elast = jnp.exp(_xdot(_head_expand_t(), acumt[:, ln - 1:ln]))
        state[...] = s2 * elast + _dot_tn((xdt * jnp.exp(last_x - acum_x)).astype(BF16), bb)

    xs, bs, cs, dts, dtts, row, col, st = _ssd_specs(nc, False)
    return pl.pallas_call(
        body, name="ssd_fwd", grid=(SSM_GROUPS, nc),
        in_specs=[xs, bs, cs, dts, dtts, row, col, row, col, row],
        out_specs=[xs, st],
        out_shape=[jax.ShapeDtypeStruct((s, SSM_D_INNER), F32),
                   jax.ShapeDtypeStruct((nc, SSM_GROUPS, SSM_GN, SSM_STATE), F32)],
        scratch_shapes=[pltpu.VMEM((SSM_GN, SSM_STATE), F32)],
        compiler_params=_params("parallel", "arbitrary"),
    )(xbc, xbc, xbc, dt_g, dt_gt, bias_r, bias_c, alog_r, alog_c, d_r)


def _ssd_bwd(xbc, dt_g, dt_gt, bias_r, bias_c, alog_r, alog_c, d_r, states, dy):
    s = xbc.shape[0]
    ln = SSM_CHUNK
    nc = s // ln

    def body(x_ref, b_ref, c_ref, dt_ref, dtt_ref, bias_ref, biast_ref, alog_ref, alogt_ref, d_ref,
             st_ref, dy_ref, dx_ref, db_ref, dc_ref, ddt_ref, ddtt_ref, dbias_ref, dbiast_ref,
             dalog_ref, dalogt_ref, dd_ref, dstate):
        step = pl.program_id(1)

        @pl.when(step == 0)
        def _():
            dstate[...] = jnp.zeros_like(dstate)

        cm = _ssd_common(dt_ref, dtt_ref, bias_ref, biast_ref, alog_ref, alogt_ref)
        dt, acum, acumt = cm["dt"], cm["acum"], cm["acumt"]
        ex, ext = _head_expand(), _head_expand_t()
        dt_x, acum_x = _dot_x(dt, ex), _dot_x(acum, ex)
        eac_x, to_end_x = jnp.exp(acum_x), jnp.exp(acum_x[ln - 1:ln, :] - acum_x)
        xv, dyv = x_ref[...], dy_ref[...]
        xdt = xv * dt_x
        xdtb, dyb = xdt.astype(BF16), dyv.astype(BF16)
        dyeb = (dyv * eac_x).astype(BF16)
        bb, cb = b_ref[...].astype(BF16), c_ref[...].astype(BF16)
        cbm = _dot_nt(cb, bb)
        s2, ds2 = st_ref[...], dstate[...]
        s2b, ds2b = s2.astype(BF16), ds2.astype(BF16)
        dxdt_state = _dot_nt(bb, ds2b) * to_end_x
        yoff = _dot_nt(cb, s2b) * eac_x
        dc_acc = _dot(dyeb, s2b)
        db_acc = _dot((xdt * to_end_x).astype(BF16), ds2b)
        f_rows = _dot_x_nt(xdt * dxdt_state, ex)
        elast = jnp.exp(acum[ln - 1:ln, :])
        dlast = (jnp.sum(f_rows, axis=0, keepdims=True)
                 + elast * jnp.sum(_dot_x_tn(ds2 * s2, ext), axis=0, keepdims=True))
        is_last = _iota((ln, 1), 0) == ln - 1
        dac_rows = _dot_x_nt(dyv * yoff, ex) - f_rows + jnp.where(is_last, dlast, 0.0)
        dstate[...] = ds2 * jnp.exp(_xdot(ext, acumt[:, ln - 1:ln])) + _dot_tn(dyeb, cb)
        causal = _iota((ln, ln), 0) >= _iota((ln, ln), 1)
        lane8 = _iota((ln, SSM_HPG), 1)
        sub8 = _iota((SSM_HPG, ln), 0)
        dcb = jnp.zeros((ln, ln), F32)
        dac_cols = jnp.zeros((SSM_HPG, ln), F32)
        for r in range(SSM_HPG):
            sl = slice(r * SSM_P, (r + 1) * SSM_P)
            decay = jnp.exp(jnp.where(causal, acum[:, r:r + 1] - acumt[r:r + 1, :], -jnp.inf))
            dx_ref[:, sl] = _dot_tn((cbm * decay).astype(BF16), dyb[:, sl])
            dcb_r = _dot_nt(dyb[:, sl], xdtb[:, sl]) * decay
            dcb = dcb + dcb_r
            e = dcb_r * cbm
            dac_rows = dac_rows + jnp.where(lane8 == r, jnp.sum(e, axis=-1, keepdims=True), 0.0)
            dac_cols = dac_cols + jnp.where(sub8 == r, jnp.sum(e, axis=0, keepdims=True), 0.0)
        dxdt = dx_ref[...] + dxdt_state
        ddt_all = _dot_x_nt(dxdt * xv, ex)
        dd_all = jnp.sum(_dot_x_nt(dyv * xv, ex), axis=0, keepdims=True)
        dx_ref[...] = dxdt * dt_x + _dot_x(d_ref[0], ex) * dyv
        dcbb = dcb.astype(BF16)
        dc_ref[...] = dc_acc + _dot(dcbb, bb)
        db_ref[...] = db_acc + _dot_tn(dcbb, cb)
        triu = jnp.where(_iota((ln, ln), 0) <= _iota((ln, ln), 1), 1.0, 0.0).astype(BF16)
        g_rows = _xdot(triu, dac_rows)
        g_cols = _dot_x(dac_cols, cm["tri"])
        d_rows = (ddt_all + g_rows * cm["a"]) * _sigmoid(cm["raw"])
        d_cols = -(g_cols * cm["at"]) * _sigmoid(cm["rawt"])
        ddt_ref[0] = d_rows
        ddtt_ref[0] = d_cols
        parts = (jnp.sum(d_rows, axis=0, keepdims=True), jnp.sum(d_cols, axis=1, keepdims=True),
                 jnp.sum(g_rows * dt, axis=0, keepdims=True) * cm["a"],
                 -jnp.sum(g_cols * cm["dtt"], axis=1, keepdims=True) * cm["at"], dd_all)
        outs = (dbias_ref, dbiast_ref, dalog_ref, dalogt_ref, dd_ref)

        @pl.when(step == 0)
        def _():
            for o_ref, p in zip(outs, parts):
                o_ref[0] = p

        @pl.when(step > 0)
        def _():
            for o_ref, p in zip(outs, parts):
                o_ref[0] += p

    xs, bs, cs, dts, dtts, row, col, st = _ssd_specs(nc, True)
    grp = pl.BlockSpec((ln, SSM_STATE), lambda g, c: (nc - 1 - c, g))
    rows = jax.ShapeDtypeStruct((SSM_GROUPS, 1, SSM_HPG), F32)
    cols = jax.ShapeDtypeStruct((SSM_GROUPS, SSM_HPG, 1), F32)
    return pl.pallas_call(
        body, name="ssd_bwd", grid=(SSM_GROUPS, nc),
        in_specs=[xs, bs, cs, dts, dtts, row, col, row, col, row, st, xs],
        out_specs=[xs, grp, grp, dts, dtts, row, col, row, col, row],
        out_shape=[jax.ShapeDtypeStruct((s, SSM_D_INNER), F32),
                   jax.ShapeDtypeStruct((s, SSM_GROUPS * SSM_STATE), F32),
                   jax.ShapeDtypeStruct((s, SSM_GROUPS * SSM_STATE), F32),
                   jax.ShapeDtypeStruct((SSM_GROUPS, s, SSM_HPG), F32),
                   jax.ShapeDtypeStruct((SSM_GROUPS, SSM_HPG, s), F32), rows, cols, rows, cols, rows],
        scratch_shapes=[pltpu.VMEM((SSM_GN, SSM_STATE), F32)],
        compiler_params=_params("parallel", "arbitrary"),
    )(xbc, xbc, xbc, dt_g, dt_gt, bias_r, bias_c, alog_r, alog_c, d_r, states, dy)


def _gate_norm_fwd(y, zx, g):
    s = y.shape[0]
    ts = _tile(s, 512, 8)

    def body(y_ref, z_ref, g_ref, o_ref):
        zv = z_ref[...]
        yg = y_ref[...] * (zv * _sigmoid(zv))
        r = lax.rsqrt(jnp.mean(yg * yg, axis=-1, keepdims=True) + EPS)
        o_ref[...] = (yg * r * g_ref[...]).astype(BF16)

    blk = pl.BlockSpec((ts, SSM_GN), lambda j, i: (i, j))
    vec = pl.BlockSpec((1, SSM_GN), lambda j, i: (0, j))
    return pl.pallas_call(
        body, name="gate_norm_fwd", grid=(SSM_GROUPS, s // ts), in_specs=[blk, blk, vec], out_specs=blk,
        out_shape=jax.ShapeDtypeStruct((s, SSM_D_INNER), BF16), compiler_params=_params("parallel", "parallel"),
    )(y, zx, g)


def _gate_norm_bwd(y, zx, g, dout):
    s = y.shape[0]
    ts = _tile(s, 512, 8)

    def body(y_ref, z_ref, g_ref, do_ref, dy_ref, dz_ref, dg_ref):
        yv, zv, dov = y_ref[...], z_ref[...], do_ref[...].astype(F32)
        sg = _sigmoid(zv)
        silu = zv * sg
        yg = yv * silu
        r = lax.rsqrt(jnp.mean(yg * yg, axis=-1, keepdims=True) + EPS)
        ygn = yg * r
        part = jnp.sum(dov * ygn, axis=0, keepdims=True)

        @pl.when(pl.program_id(1) == 0)
        def _():
            dg_ref[...] = part

        @pl.when(pl.program_id(1) > 0)
        def _():
            dg_ref[...] += part

        dn = dov * g_ref[...]
        dyg = r * (dn - ygn * jnp.mean(dn * ygn, axis=-1, keepdims=True))
        dy_ref[...] = dyg * silu
        dz_ref[...] = (dyg * yv * sg * (1.0 + zv * (1.0 - sg))).astype(BF16)

    blk = pl.BlockSpec((ts, SSM_GN), lambda j, i: (i, j))
    vec = pl.BlockSpec((1, SSM_GN), lambda j, i: (0, j))
    return pl.pallas_call(
        body, name="gate_norm_bwd", grid=(SSM_GROUPS, s // ts), in_specs=[blk, blk, vec, blk],
        out_specs=[blk, blk, vec],
        out_shape=[jax.ShapeDtypeStruct((s, SSM_D_INNER), F32), jax.ShapeDtypeStruct((s, SSM_IN_PAD), BF16),
                   jax.ShapeDtypeStruct((1, SSM_D_INNER), F32)],
        compiler_params=_params("parallel", "arbitrary"),
    )(y, zx, g, dout)


def _rope_tables(positions):
    inv_freq = ROPE_THETA ** (-jnp.arange(0, ATT_HEAD_DIM, 2, dtype=F32) / ATT_HEAD_DIM)
    ang = positions.astype(F32)[:, None] * inv_freq
    return jnp.tile(jnp.cos(ang), (1, 4)), jnp.tile(jnp.sin(ang), (1, 4))


def _group_views(v):
    return v.reshape(SSM_GROUPS, 1, SSM_HPG), v.reshape(SSM_GROUPS, SSM_HPG, 1)


def _ffn_fwd(run, x, norm_g, wg, wu, wd, tag):
    h = _rms_fwd(x, norm_g, f"ffn_norm_{tag}")
    g = run(f"ffn_gate_{tag}", _mm, h, wg, "nn", b_cols=True)
    u, a = run(f"ffn_up_{tag}", _mm, h, wu, "nn", b_cols=True, rows=FUSED_ROWS,
               fuse=(lambda uv, gv: (uv, gv * _sigmoid(gv) * uv), [g], [F32, BF16]))
    return run(f"ffn_down_{tag}", _mm, a, wd, "nn", add=x), (h, g, u, a)


def _ffn_bwd(run, mats, x, norm_g, wg, wu, wd, saved, dout, tag):
    h, g, u, a = saved

    def act_bwd(da, gv, uv):
        sg = _sigmoid(gv)
        return da * uv * sg * (1.0 + gv * (1.0 - sg)), da * (gv * sg)

    dg, du = run(f"ffn_down_dx_{tag}", _mm, dout, wd, "nt", rows=FUSED_ROWS,
                 fuse=(act_bwd, [g, u], [BF16, BF16]))
    dwd = run(f"ffn_down_dw_{tag}", _mm, a, dout, "tn", out_dtype=BF16)
    mats[("ffn_w_down", tag)] = dwd.reshape(N_SHARDS, dwd.shape[0] // N_SHARDS, dwd.shape[1])
    mats[("ffn_w_gate", tag)] = run(f"ffn_gate_dw_{tag}", _mm, h, dg, "tn", out_dtype=BF16, out_cols=True)
    mats[("ffn_w_up", tag)] = run(f"ffn_up_dw_{tag}", _mm, h, du, "tn", out_dtype=BF16, out_cols=True)
    dh = run(f"ffn_gate_dx_{tag}", _mm, dg, wg, "nt", b_cols=True)
    dh = run(f"ffn_up_dx_{tag}", _mm, du, wu, "nt", add=dh, b_cols=True)
    return _rms_bwd(x, norm_g, dh, dout, f"ffn_norm_bwd_{tag}")


class _Hook:
    def __init__(self, make, done):
        self.make, self.done = make, done


def _local_step(x, positions, target, w, hooks=None, mats=None):
    hooks = {} if hooks is None else hooks
    mats = {} if mats is None else mats

    def run(name, fn, *args, **kw):
        hook = hooks.get(name)
        if hook is None:
            return fn(*args, name=name, **kw)
        res, carried = fn(*args, name=name, carry=hook.make(), **kw)
        hook.done(carried)
        return res

    cos, sin = _rope_tables(positions)
    row = lambda v: v.reshape(1, -1)
    gq, gk = jnp.tile(row(w["attn_q_norm"]), (1, 2)), jnp.tile(row(w["attn_k_norm"]), (1, 2))
    sinks = w["attn_sinks"].reshape(-1)
    s = x.shape[0]
    row_stack = lambda g: g.reshape(N_SHARDS, g.shape[0] // N_SHARDS, g.shape[1])

    h0 = _rms_fwd(x, row(w["mixer_norm"][0]), "mixer_norm_0")
    qkv = run("attn_qkv", _mm, h0, w["attn_w_qkv"], "nn", b_cols=True)
    q, kk, vlo, vhi = _attn_prep(qkv, cos, sin, gq, gk)
    o = run("attn_fwd", _attn_fwd, q, kk, vlo, vhi, sinks)
    x1 = run("attn_out", _mm, o, w["attn_w_o"], "nn", add=x)
    ffn_w = lambda l: (row(w["ffn_norm"][l]), w["ffn_w_gate"][l], w["ffn_w_up"][l], w["ffn_w_down"][l])
    x2, ffn0 = _ffn_fwd(run, x1, *ffn_w(0), 0)

    h2 = _rms_fwd(x2, row(w["mixer_norm"][1]), "mixer_norm_1")
    zx = run("ssm_in", _mm, h2, w["ssm_w_in"], "nn")
    dt_g = zx[:, DT_OFF:DT_OFF + SSM_HEADS].reshape(s, SSM_GROUPS, SSM_HPG).transpose(1, 0, 2)
    dt_gt = dt_g.transpose(0, 2, 1)
    bias_r, bias_c = _group_views(w["ssm_dt_bias"].reshape(-1))
    alog_r, alog_c = _group_views(w["ssm_a_log"].reshape(-1))
    d_r, _ = _group_views(w["ssm_d"].reshape(-1))
    xbc = _conv_fwd(zx, w["ssm_conv_w"], row(w["ssm_conv_b"]))
    ssd_args = (xbc, dt_g, dt_gt, bias_r, bias_c, alog_r, alog_c, d_r)
    y, states = _ssd_fwd(*ssd_args)
    yn = _gate_norm_fwd(y, zx, row(w["ssm_norm"]))
    x3 = run("ssm_out", _mm, yn, w["ssm_w_out"], "nn", add=x2)
    x4, ffn1 = _ffn_fwd(run, x3, *ffn_w(1), 1)

    loss_row, dx4 = _loss_fwd_bwd(x4, target)

    dx3, dfn1 = _ffn_bwd(run, mats, x3, *ffn_w(1), ffn1, dx4, 1)
    dyn = run("ssm_out_dx", _mm, dx3, w["ssm_w_out"], "nt")
    mats[("ssm_w_out", 0)] = row_stack(run("ssm_out_dw", _mm, yn, dx3, "tn", out_dtype=BF16))
    dy, dzx, dssm_norm = _gate_norm_bwd(y, zx, row(w["ssm_norm"]), dyn)
    dxs, db, dc, ddt_g, ddt_gt, dbias, dbias_t, dalog, dalog_t, dd = _ssd_bwd(*ssd_args, states, dy)
    ddt_g = ddt_g + ddt_gt.transpose(0, 2, 1)
    dpre, dconv_w, dconv_b = _conv_bwd_pre(zx, w["ssm_conv_w"], row(w["ssm_conv_b"]),
                                           jnp.concatenate([dxs, db, dc], axis=1))
    dzx = _conv_bwd_in(dpre, w["ssm_conv_w"], dzx)
    ddt_pad = jnp.pad(ddt_g.transpose(1, 0, 2).reshape(s, SSM_HEADS), ((0, 0), (0, SSM_IN_PAD - SSM_IN)))
    dzx = lax.dynamic_update_slice(dzx, ddt_pad.astype(BF16), (0, DT_OFF))
    dw_in = run("ssm_in_dw", _mm, h2, dzx, "tn", out_dtype=BF16)
    in_shard = SSM_IN // N_SHARDS
    mats[("ssm_w_in", 0)] = jnp.stack([dw_in[:, i * in_shard:(i + 1) * in_shard] for i in range(N_SHARDS)])
    dh2 = run("ssm_in_dx", _mm, dzx, w["ssm_w_in"], "nt")
    dx2, dmn1 = _rms_bwd(x2, row(w["mixer_norm"][1]), dh2, dx3, "mixer_norm_bwd_1")

    dx1, dfn0 = _ffn_bwd(run, mats, x1, *ffn_w(0), ffn0, dx2, 0)
    do = run("attn_out_dx", _mm, dx1, w["attn_w_o"], "nt", out_dtype=BF16)
    mats[("attn_w_o", 0)] = row_stack(run("attn_out_dw", _mm, o, dx1, "tn", out_dtype=BF16))
    dq, dkc, dkp, dvloc, dvlop, dvhic, dvhip, dsink = run("attn_bwd", _attn_bwd, q, kk, vlo, vhi, sinks, do)
    dqkv, dgq, dgk = _attn_prep_bwd(qkv, cos, sin, gq, gk, dq, (dkc, dkp), (dvloc, dvlop), (dvhic, dvhip))
    mats[("attn_w_qkv", 0)] = run("attn_qkv_dw", _mm, h0, dqkv, "tn", out_dtype=BF16, out_cols=True)
    dh0 = run("attn_qkv_dx", _mm, dqkv, w["attn_w_qkv"], "nt", b_cols=True)
    dx0, dmn0 = _rms_bwd(x, row(w["mixer_norm"][0]), dh0, dx1, "mixer_norm_bwd_0")

    fold = lambda v: v[0, :ATT_HEAD_DIM] + v[0, ATT_HEAD_DIM:]
    grads = {
        "mixer_norm": jnp.concatenate([dmn0, dmn1], axis=0),
        "ffn_norm": jnp.concatenate([dfn0, dfn1], axis=0),
        "attn_q_norm": fold(dgq), "attn_k_norm": fold(dgk),
        "attn_sinks": dsink[:, :, 0].reshape(-1),
        "ssm_conv_w": dconv_w, "ssm_conv_b": dconv_b.reshape(-1),
        "ssm_dt_bias": dbias.reshape(-1) + dbias_t.reshape(-1),
        "ssm_a_log": dalog.reshape(-1) + dalog_t.reshape(-1), "ssm_d": dd.reshape(-1),
        "ssm_norm": dssm_norm.reshape(-1),
    }
    return loss_row[0, 0], dx0, grads


OTHER_CHIPS = ((1, 0), (0, 1), (1, 1))


def _position():
    return lax.axis_index("x"), lax.axis_index("y"), lax.axis_index("c")


def _sems(n):
    return pltpu.SemaphoreType.DMA((n,))


def _gather_shards(weights, layers):
    n_in, n_mat = len(weights), len(layers)

    def body(*refs):
        p, out = refs[:n_in], refs[n_in:n_in + n_mat]
        send_sems, recv_sems = refs[n_in + n_mat:]
        x, y, c = _position()
        me, sibling = (x, y, c), (x, y, 1 - c)
        chips = [(x ^ fx, y ^ fy) for fx, fy in OTHER_CHIPS]

        def rows(e, px, py, pc):
            half = out[e].shape[1] // 2
            return out[e].at[2 * px + py, pl.ds(pc * half, half), :]

        def copy(k, e, block, to, src=None):
            return pltpu.make_async_remote_copy(
                src_ref=rows(e, *block) if src is None else src, dst_ref=rows(e, *block),
                send_sem=send_sems.at[k * n_mat + e], recv_sem=recv_sems.at[k * n_mat + e],
                device_id=to, device_id_type=MESH)

        def own(e):
            i, l = layers[e]
            return pltpu.make_async_remote_copy(
                src_ref=p[i].at[l], dst_ref=out[e].at[2 * x + y], send_sem=send_sems.at[6 * n_mat + e],
                recv_sem=recv_sems.at[6 * n_mat + e], device_id=sibling, device_id_type=MESH)

        first, passed = [], []
        for e, (i, l) in enumerate(layers):
            half = out[e].shape[1] // 2
            first.append([copy(j, e, me, (*chip, c), src=p[i].at[l, pl.ds(c * half, half), :])
                          for j, chip in enumerate(chips)])
            for cp in first[-1]:
                cp.start()
        for e in range(n_mat):
            own(e).start()
        for e in range(n_mat):
            passed.append([copy(3 + j, e, (*chip, c), sibling) for j, chip in enumerate(chips)])
            for j, chip in enumerate(chips):
                copy(j, e, (*chip, c), me).wait_recv()
                passed[e][j].start()
        for e in range(n_mat):
            own(e).wait()
            for j, chip in enumerate(chips):
                copy(3 + j, e, (*chip, 1 - c), me).wait_recv()
        for e in range(n_mat):
            for cp in first[e] + passed[e]:
                cp.wait_send()

    return pl.pallas_call(
        body, name="gather_weights", in_specs=[ANY] * n_in, out_specs=[ANY] * n_mat,
        out_shape=[jax.ShapeDtypeStruct((N_SHARDS,) + weights[i].shape[1:], weights[i].dtype) for i, _ in layers],
        scratch_shapes=[_sems(7 * n_mat), _sems(7 * n_mat)],
    )(*weights)


def _all_gather8(block, name):
    m_per, n = block.shape

    def body(x_ref, out_ref, send_sems, recv_sems, local_sem):
        x, y, c = _position()
        me, sibling = (x, y, c), (x, y, 1 - c)
        chips = [(x ^ fx, y ^ fy) for fx, fy in OTHER_CHIPS]

        def rows(px, py, pc):
            return out_ref.at[pl.ds((4 * px + 2 * py + pc) * m_per, m_per), :]

        def copy(k, blk, to, src=None):
            return pltpu.make_async_remote_copy(
                src_ref=rows(*blk) if src is None else src, dst_ref=rows(*blk),
                send_sem=send_sems.at[k], recv_sem=recv_sems.at[k], device_id=to, device_id_type=MESH)

        mine = pltpu.make_async_copy(x_ref, rows(*me), local_sem)
        mine.start()
        first = [copy(0, me, sibling, src=x_ref)]
        first += [copy(1 + j, me, (*chip, c), src=x_ref) for j, chip in enumerate(chips)]
        for cp in first:
            cp.start()
        passed = [copy(4 + j, (*chip, c), sibling) for j, chip in enumerate(chips)]
        for j, chip in enumerate(chips):
            copy(1 + j, (*chip, c), me).wait_recv()
            passed[j].start()
        copy(0, sibling, me).wait_recv()
        for j, chip in enumerate(chips):
            copy(4 + j, (*chip, 1 - c), me).wait_recv()
        for cp in first + passed:
            cp.wait_send()
        mine.wait()

    return pl.pallas_call(
        body, name=name, out_shape=jax.ShapeDtypeStruct((N_DEV * m_per, n), block.dtype),
        in_specs=[pl.BlockSpec(memory_space=pltpu.VMEM)], out_specs=pl.BlockSpec(memory_space=pltpu.VMEM),
        scratch_shapes=[_sems(7), _sems(7), pltpu.SemaphoreType.DMA],
    )(block)


def _exchange(carry, name):
    n_in, n_out = len(carry["arrays"]), len(carry["out_shapes"])

    def body(*refs):
        copies = carry["build"](refs[:n_in], refs[n_in:n_in + n_out], refs[-2], refs[-1])
        for cp in copies:
            cp.start()
        for cp in copies:
            cp.wait()

    return pl.pallas_call(
        body, name=name, in_specs=[ANY] * n_in, out_specs=[ANY] * n_out, out_shape=list(carry["out_shapes"]),
        input_output_aliases=dict(carry.get("aliases", {})),
        scratch_shapes=[_sems(carry["n_sems"]), _sems(carry["n_sems"])],
    )(*carry["arrays"])


def _remote(src, dst, send_sems, recv_sems, k, to):
    return pltpu.make_async_remote_copy(src_ref=src, dst_ref=dst, send_sem=send_sems.at[k], recv_sem=recv_sems.at[k],
                                        device_id=to, device_id_type=MESH)


def _gather_over_ici(blocks, layers):
    def build(p, out, send_sems, recv_sems):
        x, y, c = _position()
        copies = []
        for e, l in enumerate(layers):
            half = out[e].shape[1] // 2
            rows = pl.ds(c * half, half)
            for j, (fx, fy) in enumerate(OTHER_CHIPS):
                copies.append(_remote(p[e].at[l, rows, :], out[e].at[2 * x + y, rows, :], send_sems, recv_sems,
                                      3 * e + j, (x ^ fx, y ^ fy, c)))
        return copies

    shapes = [jax.ShapeDtypeStruct((N_SHARDS,) + b.shape[1:], b.dtype) for b in blocks]
    return dict(build=build, arrays=list(blocks), out_shapes=shapes, n_sems=3 * len(layers))


def _gather_over_d2d(stacks, blocks, layers):
    n = len(stacks)

    def build(refs, out, send_sems, recv_sems):
        p = refs[n:]
        x, y, c = _position()
        sibling = (x, y, 1 - c)
        copies = []
        for e, l in enumerate(layers):
            half = out[e].shape[1] // 2
            for j, (fx, fy) in enumerate(OTHER_CHIPS):
                rows = out[e].at[2 * (x ^ fx) + (y ^ fy), pl.ds(c * half, half), :]
                copies.append(_remote(rows, rows, send_sems, recv_sems, 4 * e + j, sibling))
            copies.append(_remote(p[e].at[l], out[e].at[2 * x + y], send_sems, recv_sems, 4 * e + 3, sibling))
        return copies

    shapes = [jax.ShapeDtypeStruct(s.shape, s.dtype) for s in stacks]
    return dict(build=build, arrays=list(stacks) + list(blocks), out_shapes=shapes, n_sems=4 * n,
                aliases={i: i for i in range(n)})


def _grads_to_sibling(stacks):
    def build(g, out, send_sems, recv_sems):
        x, y, c = _position()
        copies = []
        for e in range(len(stacks)):
            half = g[e].shape[1] // 2
            copies.append(_remote(g[e].at[:, pl.ds((1 - c) * half, half), :], out[e], send_sems, recv_sems, e,
                                  (x, y, 1 - c)))
        return copies

    shapes = [jax.ShapeDtypeStruct((N_SHARDS, g.shape[1] // 2, g.shape[2]), g.dtype) for g in stacks]
    return dict(build=build, arrays=list(stacks), out_shapes=shapes, n_sems=len(stacks))


def _grads_to_owners(partials):
    def build(p, out, send_sems, recv_sems):
        x, y, c = _position()
        copies = []
        for e in range(len(partials)):
            for k, (fx, fy) in enumerate(OTHER_CHIPS):
                px, py = x ^ fx, y ^ fy
                copies.append(_remote(p[e].at[2 * px + py], out[e].at[k], send_sems, recv_sems, 3 * e + k,
                                      (px, py, c)))
        return copies

    shapes = [jax.ShapeDtypeStruct((len(OTHER_CHIPS),) + p.shape[1:], p.dtype) for p in partials]
    return dict(build=build, arrays=list(partials), out_shapes=shapes, n_sems=3 * len(partials))


def _share_halves(grads, layers):
    n = len(grads)

    def body(*refs):
        out, send_sems, recv_sems = refs[n:2 * n], refs[-2], refs[-1]
        x, y, c = _position()
        copies = []
        for e, (i, l) in enumerate(layers):
            half = out[i].shape[1] // 2
            rows = out[i].at[l, pl.ds(c * half, half), :]
            copies.append(pltpu.make_async_remote_copy(
                src_ref=rows, dst_ref=rows, send_sem=send_sems.at[e], recv_sem=recv_sems.at[e],
                device_id=(x, y, 1 - c), device_id_type=MESH))
            copies[-1].start()
        for cp in copies:
            cp.wait()

    return pl.pallas_call(
        body, name="grads_share_halves", in_specs=[ANY] * n, out_specs=[ANY] * n,
        out_shape=[jax.ShapeDtypeStruct(g.shape, g.dtype) for g in grads],
        input_output_aliases={i: i for i in range(n)},
        scratch_shapes=[_sems(len(layers)), _sems(len(layers))],
    )(*grads)


ADD_BLOCK_ELEMS = 1 << 19


def _add_rows(half, cols):
    return _tile(half, max(16, ADD_BLOCK_ELEMS // cols // 16 * 16), 16)


def _add_pair(stack, recv, c_idx, name):
    _, half, cols = recv.shape
    tr = _add_rows(half, cols)
    nt = half // tr

    def body(c_ref, a_ref, b_ref, o_ref):
        o_ref[...] = (a_ref[...].astype(F32) + b_ref[...].astype(F32)).astype(o_ref.dtype)

    blk = pl.BlockSpec((None, tr, cols), lambda s, i, c_ref: (s, i, 0))
    return pl.pallas_call(
        body, name=name,
        grid_spec=pltpu.PrefetchScalarGridSpec(
            num_scalar_prefetch=1, grid=(N_SHARDS, nt),
            in_specs=[pl.BlockSpec((None, tr, cols), lambda s, i, c_ref: (s, c_ref[0] * nt + i, 0)), blk],
            out_specs=blk),
        out_shape=jax.ShapeDtypeStruct(recv.shape, recv.dtype),
        compiler_params=_params("parallel", "parallel"),
    )(c_idx, stack, recv)


def _add_owned(partial, recv, sc_idx, layer, shape, into, name):
    _, half, cols = partial.shape
    tr = _add_rows(half, cols)
    nt = half // tr

    def body(sc_ref, a_ref, r0_ref, r1_ref, r2_ref, *rest):
        o_ref = rest[-1]
        o_ref[...] = (((a_ref[...].astype(F32) + r0_ref[...].astype(F32)) + r1_ref[...].astype(F32))
                      + r2_ref[...].astype(F32))

    slot = lambda k: pl.BlockSpec((None, tr, cols), lambda i, sc_ref: (k, i, 0))
    has_into = into is not None
    return pl.pallas_call(
        body, name=name,
        grid_spec=pltpu.PrefetchScalarGridSpec(
            num_scalar_prefetch=1, grid=(nt,),
            in_specs=[pl.BlockSpec((None, tr, cols), lambda i, sc_ref: (sc_ref[0], i, 0)), slot(0), slot(1), slot(2)]
            + ([ANY] if has_into else []),
            out_specs=pl.BlockSpec((None, tr, cols), lambda i, sc_ref: (layer, sc_ref[1] * nt + i, 0))),
        out_shape=jax.ShapeDtypeStruct(shape, F32),
        input_output_aliases={5: 0} if has_into else {},
        compiler_params=_params("parallel"),
    )(*((sc_idx, partial, recv, recv, recv) + ((into,) if has_into else ())))


def _sum8(gathered):
    m = gathered.shape[0] // N_DEV

    def body(g_ref, o_ref):
        total = g_ref[0:m, :]
        for d in range(1, N_DEV):
            total = total + g_ref[d * m:(d + 1) * m, :]
        o_ref[...] = total

    return pl.pallas_call(
        body, name="small_grads_sum", out_shape=jax.ShapeDtypeStruct((m, LANES), F32),
        in_specs=[pl.BlockSpec(memory_space=pltpu.VMEM)], out_specs=pl.BlockSpec(memory_space=pltpu.VMEM),
    )(gathered)


ADAMW_BLOCK_ELEMS = 1 << 18


def _adamw(w, g, m, v, name):
    l, r, cols = w.shape
    tr = _tile(r, max(8, ADAMW_BLOCK_ELEMS // cols // 8 * 8), 8)

    def body(w_ref, g_ref, m_ref, v_ref, d_ref, nm_ref, nv_ref):
        gv = g_ref[...]
        nm = ADAM_B1 * m_ref[...] + (1.0 - ADAM_B1) * gv
        nv = ADAM_B2 * v_ref[...] + (1.0 - ADAM_B2) * jnp.square(gv)
        m_hat = nm / (1.0 - ADAM_B1 ** ADAM_STEP)
        v_hat = nv / (1.0 - ADAM_B2 ** ADAM_STEP)
        d_ref[...] = -ADAM_LR * (m_hat / (jnp.sqrt(v_hat) + ADAM_EPS) + ADAM_WD * w_ref[...])
        nm_ref[...] = nm
        nv_ref[...] = nv

    blk = pl.BlockSpec((None, tr, cols), lambda a, i: (a, i, 0))
    return pl.pallas_call(
        body, name=name, grid=(l, r // tr), in_specs=[blk] * 4, out_specs=[blk] * 3,
        out_shape=[jax.ShapeDtypeStruct(w.shape, F32)] * 3, compiler_params=_params("parallel", "parallel"),
    )(w, g, m, v)


WEIGHTS = ("mixer_norm", "ffn_norm", "attn_w_qkv", "attn_q_norm", "attn_k_norm", "attn_sinks", "attn_w_o",
           "ssm_w_in", "ssm_conv_w", "ssm_conv_b", "ssm_dt_bias", "ssm_a_log", "ssm_d", "ssm_norm", "ssm_w_out",
           "ffn_w_gate", "ffn_w_up", "ffn_w_down")
BIG = ("attn_w_qkv", "attn_w_o", "ffn_w_gate", "ffn_w_up", "ffn_w_down", "ssm_w_in", "ssm_w_out")
MATRICES = (("attn_w_qkv", 0), ("attn_w_o", 0), ("ffn_w_gate", 0), ("ffn_w_up", 0), ("ffn_w_down", 0),
            ("ssm_w_in", 0), ("ssm_w_out", 0), ("ffn_w_gate", 1), ("ffn_w_up", 1), ("ffn_w_down", 1))
MATRIX_LAYERS = tuple((BIG.index(n), l) for n, l in MATRICES)
GROUPS = {"attn": MATRICES[0:2], "ffn0": MATRICES[2:5], "ssm": MATRICES[5:7], "ffn1": MATRICES[7:10]}
SMALL_SHARDED = ("ssm_conv_w", "ssm_conv_b", "ssm_norm")
SMALL = tuple(n for n in WEIGHTS if n not in BIG)


def _pack_rows(parts, row_unit=8):
    flat = jnp.concatenate([p.reshape(-1) for p in parts])
    pad = (-flat.shape[0]) % (LANES * row_unit)
    return jnp.pad(flat, (0, pad)).reshape(-1, LANES)


def _unpack(flat, shapes):
    out, off = [], 0
    for shp in shapes:
        size = math.prod(shp)
        out.append(flat[off:off + size].reshape(shp))
        off += size
    return out


def kernel(x, positions, mixer_norm, ffn_norm, attn_w_qkv, attn_q_norm, attn_k_norm, attn_sinks, attn_w_o, ssm_w_in, ssm_conv_w, ssm_conv_b, ssm_dt_bias, ssm_a_log, ssm_d, ssm_norm, ssm_w_out, ffn_w_gate, ffn_w_up, ffn_w_down, loss_target, m_mixer_norm, m_ffn_norm, m_attn_w_qkv, m_attn_q_norm, m_attn_k_norm, m_attn_sinks, m_attn_w_o, m_ssm_w_in, m_ssm_conv_w, m_ssm_conv_b, m_ssm_dt_bias, m_ssm_a_log, m_ssm_d, m_ssm_norm, m_ssm_w_out, m_ffn_w_gate, m_ffn_w_up, m_ffn_w_down, v_mixer_norm, v_ffn_norm, v_attn_w_qkv, v_attn_q_norm, v_attn_k_norm, v_attn_sinks, v_attn_w_o, v_ssm_w_in, v_ssm_conv_w, v_ssm_conv_b, v_ssm_dt_bias, v_ssm_a_log, v_ssm_d, v_ssm_norm, v_ssm_w_out, v_ffn_w_gate, v_ffn_w_up, v_ffn_w_down):
    args = locals()
    w = {n: args[n] for n in WEIGHTS}
    m = {n: args["m_" + n] for n in WEIGHTS}
    v = {n: args["v_" + n] for n in WEIGHTS}
    ax, ay, ac = lax.axis_index("x"), lax.axis_index("y"), lax.axis_index("c")
    shard = 2 * ax + ay

    wb = {n: w[n].astype(BF16) for n in BIG}
    wl, hooks = {"ffn_w_gate": [None, None], "ffn_w_up": [None, None], "ffn_w_down": [None, None]}, {}

    def gathered(keys, stacks):
        for (n, l), st in zip(keys, stacks):
            if n == "ssm_w_in":
                wl[n] = jnp.concatenate([st[i] for i in range(N_SHARDS)]
                                        + [jnp.zeros((st.shape[1], SSM_IN_PAD - SSM_IN), BF16)], axis=1)
            elif n in ("ffn_w_gate", "ffn_w_up"):
                wl[n][l] = st
            elif n == "ffn_w_down":
                wl[n][l] = st.reshape(st.shape[0] * st.shape[1], st.shape[2])
            elif n == "attn_w_qkv":
                wl[n] = st
            else:
                wl[n] = st.reshape(st.shape[0] * st.shape[1], st.shape[2])

    def gather_behind(keys, first_leg, second_leg):
        blocks, layers, got = [wb[n] for n, _ in keys], [l for _, l in keys], {}
        hooks[first_leg] = _Hook(lambda: _gather_over_ici(blocks, layers), lambda res: got.update(stacks=res))
        hooks[second_leg] = _Hook(lambda: _gather_over_d2d(got["stacks"], blocks, layers),
                                  lambda res: gathered(keys, res))

    gathered(GROUPS["attn"], _gather_shards([wb[n] for n, _ in GROUPS["attn"]],
                                            [(e, l) for e, (_, l) in enumerate(GROUPS["attn"])]))
    gather_behind(GROUPS["ffn0"], "attn_fwd", "attn_out")
    gather_behind(GROUPS["ssm"], "ffn_gate_0", "ffn_up_0")
    gather_behind(GROUPS["ffn1"], "ssm_in", "ssm_out")
    small_shapes = [w[n].shape for n in SMALL_SHARDED]
    small_all = _all_gather8(_pack_rows([w[n] for n in SMALL_SHARDED]), "gather_small_params")
    small_all = small_all.reshape(N_DEV, -1)[::2]
    full, off = {}, 0
    for n, shp in zip(SMALL_SHARDED, small_shapes):
        size = math.prod(shp)
        seg = small_all[:, off:off + size].reshape((N_SHARDS,) + shp)
        full[n] = jnp.moveaxis(seg, 0, -2).reshape(shp[:-1] + (N_SHARDS * shp[-1],))
        off += size
    wl.update({
        "mixer_norm": mixer_norm, "ffn_norm": ffn_norm,
        "attn_q_norm": attn_q_norm[0], "attn_k_norm": attn_k_norm[0], "attn_sinks": attn_sinks[0],
        "ssm_conv_w": full["ssm_conv_w"][0], "ssm_conv_b": full["ssm_conv_b"][0],
        "ssm_dt_bias": ssm_dt_bias[0], "ssm_a_log": ssm_a_log[0], "ssm_d": ssm_d[0],
        "ssm_norm": full["ssm_norm"][0],
    })

    c_idx = ac.reshape(1).astype(jnp.int32)
    sc_idx = jnp.stack([shard, ac]).astype(jnp.int32)
    mats, halves = {}, {n: None for n in BIG}

    def pair_sums(keys, recv):
        return [_add_pair(mats[k], r, c_idx, f"grads_add_pair_{k[0]}_{k[1]}") for k, r in zip(keys, recv)]

    def owner_sums(keys, partials, recv):
        for (n, l), p, r in zip(keys, partials, recv):
            halves[n] = _add_owned(p, r, sc_idx, l, w[n].shape, halves[n], f"grads_add_owned_{n}_{l}")

    def reduce_behind(keys, first_leg, second_leg):
        got = {}
        hooks[first_leg] = _Hook(lambda: _grads_to_sibling([mats[k] for k in keys]),
                                 lambda res: got.update(partials=pair_sums(keys, res)))
        hooks[second_leg] = _Hook(lambda: _grads_to_owners(got["partials"]),
                                  lambda res: owner_sums(keys, got["partials"], res))

    reduce_behind(GROUPS["ffn1"], "ssm_out_dx", "ssm_in_dw")
    reduce_behind(GROUPS["ssm"], "ssm_in_dx", "ffn_down_dx_0")
    reduce_behind(GROUPS["ffn0"], "attn_out_dx", "attn_bwd")
    loss_part, dx, g_full = _local_step(x[0], positions[0], loss_target[0], wl, hooks, mats)
    keys = GROUPS["attn"]
    partials = pair_sums(keys, _exchange(_grads_to_sibling([mats[k] for k in keys]), "grads_to_sibling"))
    owner_sums(keys, partials, _exchange(_grads_to_owners(partials), "grads_to_owners"))
    grads = dict(zip(BIG, _share_halves([halves[n] for n in BIG], MATRIX_LAYERS)))

    small_full_shapes = [g_full[n].shape for n in SMALL] + [(1,)]
    small_g = _pack_rows([g_full[n] for n in SMALL] + [loss_part.reshape(1)])
    small_sum = _sum8(_all_gather8(small_g, "gather_small_grads")).reshape(-1)
    *small_list, loss = _unpack(small_sum, small_full_shapes)
    for n, g in zip(SMALL, small_list):
        if n in SMALL_SHARDED:
            width = w[n].shape[-1]
            g = lax.dynamic_slice_in_dim(g, shard * width, width, axis=g.ndim - 1)
        grads[n] = g.reshape(w[n].shape)

    delta, new_m, new_v = {}, {}, {}
    for n in BIG:
        delta[n], new_m[n], new_v[n] = _adamw(w[n], grads[n], m[n], v[n], "adamw_" + n)
    small_local = [w[n].shape for n in SMALL]
    pk = lambda t: _pack_rows([t[n] for n in SMALL])[None]
    outs = _adamw(pk(w), pk(grads), pk(m), pk(v), "adamw_small")
    for res, o in zip((delta, new_m, new_v), outs):
        for n, a in zip(SMALL, _unpack(o.reshape(-1), small_local)):
            res[n] = a

    return (loss.reshape(()), dx[None], *[grads[n] for n in WEIGHTS], *[delta[n] for n in WEIGHTS],
            *[new_m[n] for n in WEIGHTS], *[new_v[n] for n in WEIGHTS])
```

```python
import math

import jax
import jax.numpy as jnp
from jax import lax
from jax.experimental import pallas as pl
from jax.experimental.pallas import tpu as pltpu

F32 = jnp.float32
BF16 = jnp.bfloat16

D_MODEL = 2048
EPS = 1e-6
ATT_HEAD_DIM = 64
ATT_Q_HEADS = 32
ATT_KV_HEADS = 4
ATT_GROUP = 8
ATT_BLOCK = 128
ROPE_THETA = 10000.0
Q_WIDTH = ATT_Q_HEADS * ATT_HEAD_DIM
KV_WIDTH = ATT_KV_HEADS * ATT_HEAD_DIM
SSM_D_INNER = 4096
SSM_HEADS = 64
SSM_GROUPS = 8
SSM_HPG = 8
SSM_P = 64
SSM_STATE = 128
SSM_CONV = 4
SSM_CHUNK = 256
SSM_CONV_DIM = 6144
SSM_GN = SSM_D_INNER // SSM_GROUPS
SSM_IN = SSM_D_INNER + SSM_CONV_DIM + SSM_HEADS
LANES = 128
SSM_IN_PAD = -(-SSM_IN // LANES) * LANES
N_SHARDS = 4
N_DEV = 8

ADAM_LR = 0.001
ADAM_B1 = 0.9
ADAM_B2 = 0.999
ADAM_EPS = 1e-08
ADAM_WD = 0.01
ADAM_STEP = 10

VMEM_LIMIT = 56 * 1024 * 1024
MESH = pl.DeviceIdType.MESH
ANY = pl.BlockSpec(memory_space=pl.ANY)


def _params(*sem):
    return pltpu.CompilerParams(dimension_semantics=sem, vmem_limit_bytes=VMEM_LIMIT)


def _sems(n):
    return pltpu.SemaphoreType.DMA((n,))


def _call(body, carry, name, grid, in_specs, out_specs, out_shape, scratch_shapes, sem, args):
    if carry is None:
        return pl.pallas_call(body, name=name, grid=grid, in_specs=in_specs, out_specs=out_specs,
                              out_shape=out_shape, scratch_shapes=scratch_shapes,
                              compiler_params=_params(*sem))(*args)
    n_in, n_out, n_scr = len(in_specs), len(out_specs), len(scratch_shapes)
    c_arrays, c_shapes = list(carry["arrays"]), list(carry["out_shapes"])
    n_cin, n_cout = len(c_arrays), len(c_shapes)

    def carrying(*refs):
        ins, refs = refs[:n_in], refs[n_in:]
        cin, refs = refs[:n_cin], refs[n_cin:]
        outs, refs = refs[:n_out], refs[n_out:]
        cout, refs = refs[:n_cout], refs[n_cout:]
        scratch, (send_sems, recv_sems) = refs[:n_scr], refs[n_scr:]
        copies = carry["build"](cin, cout, send_sems, recv_sems)
        ids = [pl.program_id(d) for d in range(len(grid))]
        first, last = ids[0] == 0, ids[0] == grid[0] - 1
        for d in range(1, len(grid)):
            first = jnp.logical_and(first, ids[d] == 0)
            last = jnp.logical_and(last, ids[d] == grid[d] - 1)

        @pl.when(first)
        def _():
            for cp in copies:
                cp.start()

        body(*ins, *outs, *scratch)

        @pl.when(last)
        def _():
            for cp in copies:
                cp.wait()

    aliases = {n_in + i: n_out + o for i, o in carry.get("aliases", {}).items()}
    return pl.pallas_call(
        carrying, name=name, grid=grid, in_specs=list(in_specs) + [ANY] * n_cin,
        out_specs=list(out_specs) + [ANY] * n_cout, out_shape=list(out_shape) + c_shapes,
        scratch_shapes=list(scratch_shapes) + [_sems(carry["n_sems"]), _sems(carry["n_sems"])],
        input_output_aliases=aliases, compiler_params=_params(*(["arbitrary"] * len(grid))))(*args, *c_arrays)


def _tile(dim, target, unit=LANES):
    if dim <= target:
        return dim
    t = (target // unit) * unit
    while t >= unit:
        if dim % t == 0:
            return t
        t -= unit
    return dim


def _dot(a, b):
    return lax.dot_general(a, b, (((1,), (0,)), ((), ())), preferred_element_type=F32)


def _dot_nt(a, b):
    return lax.dot_general(a, b, (((1,), (1,)), ((), ())), preferred_element_type=F32)


def _dot_tn(a, b):
    return lax.dot_general(a, b, (((0,), (0,)), ((), ())), preferred_element_type=F32)


def _split3(x):
    hi = x.astype(BF16)
    r1 = x - hi.astype(F32)
    mid = r1.astype(BF16)
    lo = (r1 - mid.astype(F32)).astype(BF16)
    return hi, mid, lo


def _dot_x(x, m):
    hi, mid, lo = _split3(x)
    return _dot(hi, m) + _dot(mid, m) + _dot(lo, m)


def _xdot(m, x):
    hi, mid, lo = _split3(x)
    return _dot(m, hi) + _dot(m, mid) + _dot(m, lo)


def _dot_x_nt(x, m):
    hi, mid, lo = _split3(x)
    return _dot_nt(hi, m) + _dot_nt(mid, m) + _dot_nt(lo, m)


def _iota(shape, dim):
    return lax.broadcasted_iota(jnp.int32, shape, dim)


def _sigmoid(x):
    return 1.0 / (1.0 + jnp.exp(-x))


def _softplus(x):
    return jnp.maximum(x, 0.0) + jnp.log(1.0 + jnp.exp(-jnp.abs(x)))


MM_ROWS = 1024
MM_TILE = 1408
MM_DEPTH = 2816
FUSED_ROWS = 512


def _mm(a, b, mode, name, add=None, out_dtype=F32, b_cols=False, out_cols=False, fuse=None, rows=MM_ROWS,
        carry=None):
    bs = b.shape[-2:]
    if b_cols:
        bs = (bs[0], N_SHARDS * bs[1])
    if mode == "nn":
        (m, k), (k2, n) = a.shape, bs
    elif mode == "nt":
        (m, k), (n, k2) = a.shape, bs
    else:
        (k, m), (k2, n) = a.shape, bs
    assert k == k2, (a.shape, b.shape, mode)
    split_n = (b_cols and mode == "nn") or out_cols
    split_k = b_cols and mode == "nt"
    tm = _tile(m, MM_TILE if mode == "tn" else rows)
    tn = _tile(n // N_SHARDS if split_n else n, MM_TILE)
    tk = _tile(k // N_SHARDS if split_k else k, MM_DEPTH)
    nk = k // tk
    nj, nq = (n // N_SHARDS) // tn, (k // N_SHARDS) // tk
    if mode == "tn":
        a_spec = pl.BlockSpec((tk, tm), lambda i, j, q: (q, i))
    else:
        a_spec = pl.BlockSpec((tm, tk), lambda i, j, q: (i, q))
    if mode == "nt":
        if b_cols:
            b_spec = pl.BlockSpec((None, tn, tk), lambda i, j, q: (q // nq, j, q % nq))
        else:
            b_spec = pl.BlockSpec((tn, tk), lambda i, j, q: (j, q))
    elif b_cols:
        b_spec = pl.BlockSpec((None, tk, tn), lambda i, j, q: (j // nj, q, j % nj))
    else:
        b_spec = pl.BlockSpec((tk, tn), lambda i, j, q: (q, j))
    add_spec = pl.BlockSpec((tm, tn), lambda i, j, q: (i, j))
    if out_cols:
        o_spec = pl.BlockSpec((None, tm, tn), lambda i, j, q: (j // nj, i, j % nj))
        o_shape = (N_SHARDS, m, n // N_SHARDS)
    else:
        o_spec, o_shape = add_spec, (m, n)
    dot = {"nn": _dot, "nt": _dot_nt, "tn": _dot_tn}[mode]
    has_add = add is not None
    fuse_fn, extra, out_dtypes = fuse if fuse is not None else (None, [], [out_dtype])
    n_in, n_out = 2 + has_add + len(extra), len(out_dtypes)

    def body(*refs):
        a_ref, b_ref = refs[:2]
        add_ref = refs[2] if has_add else None
        extra_refs = refs[2 + has_add:n_in]
        o_refs, acc_ref = refs[n_in:n_in + n_out], refs[n_in + n_out]
        part = dot(a_ref[...].astype(BF16), b_ref[...].astype(BF16))

        def finish(total):
            if has_add:
                total = total + add_ref[...].astype(F32)
            outs = (total,) if fuse_fn is None else fuse_fn(total, *[r[...] for r in extra_refs])
            for o_ref, val in zip(o_refs, outs):
                o_ref[...] = val.astype(o_ref.dtype)

        if nk == 1:
            finish(part)
        else:
            q = pl.program_id(2)

            @pl.when(q == 0)
            def _():
                acc_ref[...] = part

            @pl.when(jnp.logical_and(q > 0, q < nk - 1))
            def _():
                acc_ref[...] += part

            @pl.when(q == nk - 1)
            def _():
                finish(acc_ref[...] + part)

    in_specs = [a_spec, b_spec] + [add_spec] * (has_add + len(extra))
    args = (a, b) + ((add,) if has_add else ()) + tuple(extra)
    res = _call(body, carry, name, (m // tm, n // tn, nk), in_specs, [o_spec] * n_out,
                [jax.ShapeDtypeStruct(o_shape, dt) for dt in out_dtypes],
                [pltpu.VMEM((tm, tn) if nk > 1 else (8, LANES), F32)], ("parallel", "parallel", "arbitrary"), args)
    main = res[0] if fuse is None else res[:n_out]
    return main if carry is None else (main, res[n_out:])


def _rms_fwd(x, g, name):
    s, d = x.shape
    ts = _tile(s, 512, 8)

    def body(x_ref, g_ref, o_ref):
        xv = x_ref[...]
        r = lax.rsqrt(jnp.mean(xv * xv, axis=-1, keepdims=True) + EPS)
        o_ref[...] = (xv * r * g_ref[...]).astype(BF16)

    return pl.pallas_call(
        body, name=name, grid=(s // ts,),
        in_specs=[pl.BlockSpec((ts, d), lambda i: (i, 0)), pl.BlockSpec((1, d), lambda i: (0, 0))],
        out_specs=pl.BlockSpec((ts, d), lambda i: (i, 0)),
        out_shape=jax.ShapeDtypeStruct((s, d), BF16),
        compiler_params=_params("parallel"),
    )(x, g)


def _rms_bwd(x, g, dh, dres, name):
    s, d = x.shape
    ts = _tile(s, 512, 8)

    def body(x_ref, g_ref, dh_ref, dres_ref, dx_ref, dg_ref):
        xv = x_ref[...]
        r = lax.rsqrt(jnp.mean(xv * xv, axis=-1, keepdims=True) + EPS)
        xhat = xv * r
        dhv = dh_ref[...].astype(F32)
        part = jnp.sum(dhv * xhat, axis=0, keepdims=True)

        @pl.when(pl.program_id(0) == 0)
        def _():
            dg_ref[...] = part

        @pl.when(pl.program_id(0) > 0)
        def _():
            dg_ref[...] += part

        dxh = dhv * g_ref[...]
        dx = r * (dxh - xhat * jnp.mean(dxh * xhat, axis=-1, keepdims=True))
        dx_ref[...] = dres_ref[...] + dx

    row = pl.BlockSpec((ts, d), lambda i: (i, 0))
    vec = pl.BlockSpec((1, d), lambda i: (0, 0))
    return pl.pallas_call(
        body, name=name, grid=(s // ts,),
        in_specs=[row, vec, row, row], out_specs=[row, vec],
        out_shape=[jax.ShapeDtypeStruct((s, d), F32), jax.ShapeDtypeStruct((1, d), F32)],
        compiler_params=_params("arbitrary"),
    )(x, g, dh, dres)


def _act_fwd(g, u, name):
    s, f = g.shape
    ts, tf = _tile(s, 512, 8), _tile(f, 1408)

    def body(g_ref, u_ref, o_ref):
        gv = g_ref[...]
        o_ref[...] = (gv * _sigmoid(gv) * u_ref[...]).astype(BF16)

    blk = pl.BlockSpec((ts, tf), lambda i, j: (i, j))
    return pl.pallas_call(
        body, name=name, grid=(s // ts, f // tf), in_specs=[blk, blk], out_specs=blk,
        out_shape=jax.ShapeDtypeStruct((s, f), BF16), compiler_params=_params("parallel", "parallel"),
    )(g, u)


def _act_bwd(g, u, da, name):
    s, f = g.shape
    ts, tf = _tile(s, 512, 8), _tile(f, 1408)

    def body(g_ref, u_ref, da_ref, dg_ref, du_ref):
        gv, uv, dav = g_ref[...], u_ref[...], da_ref[...].astype(F32)
        sg = _sigmoid(gv)
        silu = gv * sg
        du_ref[...] = (dav * silu).astype(BF16)
        dg_ref[...] = (dav * uv * sg * (1.0 + gv * (1.0 - sg))).astype(BF16)

    blk = pl.BlockSpec((ts, tf), lambda i, j: (i, j))
    return pl.pallas_call(
        body, name=name, grid=(s // ts, f // tf), in_specs=[blk, blk, blk], out_specs=[blk, blk],
        out_shape=[jax.ShapeDtypeStruct((s, f), BF16)] * 2, compiler_params=_params("parallel", "parallel"),
    )(g, u, da)


def _loss_fwd_bwd(y, target):
    s, d = y.shape
    ts = _tile(s, 512, 8)

    def body(y_ref, t_ref, l_ref, dy_ref):
        diff = y_ref[...] - t_ref[...]
        dy_ref[...] = diff * (1.0 / d)
        part = jnp.full((1, LANES), 0.5 * jnp.sum(jnp.mean(diff * diff, axis=-1, keepdims=True)), F32)

        @pl.when(pl.program_id(0) == 0)
        def _():
            l_ref[...] = part

        @pl.when(pl.program_id(0) > 0)
        def _():
            l_ref[...] += part

    row = pl.BlockSpec((ts, d), lambda i: (i, 0))
    acc = pl.BlockSpec((1, LANES), lambda i: (0, 0))
    return pl.pallas_call(
        body, name="loss", grid=(s // ts,), in_specs=[row, row], out_specs=[acc, row],
        out_shape=[jax.ShapeDtypeStruct((1, LANES), F32), jax.ShapeDtypeStruct((s, d), F32)],
        compiler_params=_params("arbitrary"),
    )(y, target)


def _lane_consts():
    r, c = _iota((LANES, LANES), 0), _iota((LANES, LANES), 1)
    same = (r >> 6) == (c >> 6)
    rin, cin = r & 63, c & 63
    one = lambda cond: jnp.where(cond, 1.0, 0.0).astype(BF16)
    return dict(
        seg=one(same),
        rot=(jnp.where(same & (rin == cin + 32), -1.0, 0.0)
             + jnp.where(same & (cin == rin + 32), 1.0, 0.0)).astype(BF16),
        dup_lo=one(r == cin), dup_hi=one(r == cin + 64),
        up=one((c >= 64) & (r == c - 64)), down=one((c < 64) & (r == c + 64)),
        fold_lo=one((c < 64) & (rin == c)), fold_hi=one((c >= 64) & (rin == c - 64)),
    )


def _norm_rope(xc, gain, cos, sin, k):
    ss = _dot_x(xc * xc, k["seg"])
    rinv = lax.rsqrt(ss * (1.0 / ATT_HEAD_DIM) + EPS)
    xhat = xc * rinv
    y = xhat * gain
    return y * cos + _dot_x(y, k["rot"]) * sin, xhat, rinv


def _norm_rope_bwd(dr, xhat, rinv, gain, cos, sin, k):
    dy = dr * cos - _dot_x(dr * sin, k["rot"])
    dgain = jnp.sum(dy * xhat, axis=0, keepdims=True)
    dxh = dy * gain
    dx = rinv * (dxh - xhat * (_dot_x(dxh * xhat, k["seg"]) * (1.0 / ATT_HEAD_DIM)))
    return dx, dgain


def _attn_prep(qkv, cos, sin, gq, gk):
    s = qkv.shape[0]
    tr = _tile(s, 256, 8)

    def body(x_ref, cos_ref, sin_ref, gq_ref, gk_ref, q_ref, kk_ref, vlo_ref, vhi_ref):
        k = _lane_consts()
        cosv, sinv = cos_ref[...], sin_ref[...]
        lane = _iota((tr, LANES), 1)
        for j in range(Q_WIDTH // LANES):
            r, _, _ = _norm_rope(x_ref[:, j * LANES:(j + 1) * LANES], gq_ref[...], cosv, sinv, k)
            q_ref[:, j * LANES:(j + 1) * LANES] = r.astype(BF16)
        for i in range(KV_WIDTH // LANES):
            off = Q_WIDTH + i * LANES
            r, _, _ = _norm_rope(x_ref[:, off:off + LANES], gk_ref[...], cosv, sinv, k)
            rb = r.astype(BF16)
            kk_ref[:, (2 * i) * LANES:(2 * i + 1) * LANES] = _dot(rb, k["dup_lo"]).astype(BF16)
            kk_ref[:, (2 * i + 1) * LANES:(2 * i + 2) * LANES] = _dot(rb, k["dup_hi"]).astype(BF16)
            off = Q_WIDTH + KV_WIDTH + i * LANES
            vb = x_ref[:, off:off + LANES].astype(BF16)
            zero = jnp.zeros_like(vb)
            vlo_ref[:, (2 * i) * LANES:(2 * i + 1) * LANES] = jnp.where(lane < 64, vb, zero)
            vhi_ref[:, (2 * i) * LANES:(2 * i + 1) * LANES] = _dot(vb, k["up"]).astype(BF16)
            vlo_ref[:, (2 * i + 1) * LANES:(2 * i + 2) * LANES] = _dot(vb, k["down"]).astype(BF16)
            vhi_ref[:, (2 * i + 1) * LANES:(2 * i + 2) * LANES] = jnp.where(lane >= 64, vb, zero)

    w = qkv.shape[1]
    row = lambda width: pl.BlockSpec((tr, width), lambda i: (i, 0))
    vec = pl.BlockSpec((1, LANES), lambda i: (0, 0))
    kw = ATT_KV_HEADS * LANES
    return pl.pallas_call(
        body, name="attn_prep", grid=(s // tr,),
        in_specs=[row(w), row(LANES), row(LANES), vec, vec],
        out_specs=[row(Q_WIDTH), row(kw), row(kw), row(kw)],
        out_shape=[jax.ShapeDtypeStruct((s, Q_WIDTH), BF16)] + [jax.ShapeDtypeStruct((s, kw), BF16)] * 3,
        compiler_params=_params("parallel"),
    )(qkv, cos, sin, gq, gk)


def _band_mask(n):
    qi = _iota((ATT_BLOCK, 2 * ATT_BLOCK), 0)
    kj = _iota((ATT_BLOCK, 2 * ATT_BLOCK), 1)
    band = (kj > qi) & (kj <= qi + ATT_BLOCK)
    return band & ((kj >= ATT_BLOCK) | (n > 0))


def _softmax_sink(s, valid, sink):
    s = jnp.where(valid, s, -jnp.inf)
    m = jnp.maximum(jnp.max(s, axis=-1, keepdims=True), sink)
    p = jnp.exp(s - m)
    esink = jnp.exp(sink - m)
    inv = 1.0 / (jnp.sum(p, axis=-1, keepdims=True) + esink)
    return p * inv, esink * inv


def _attn_specs(order):
    if order == "nh":
        cur = lambda n, h: (n, h)
        prev = lambda n, h: (jnp.maximum(n - 1, 0), h)
    else:
        cur = lambda h, n: (n, h)
        prev = lambda h, n: (jnp.maximum(n - 1, 0), h)
    qs = pl.BlockSpec((ATT_BLOCK, ATT_GROUP * ATT_HEAD_DIM), cur)
    kc = pl.BlockSpec((ATT_BLOCK, LANES), cur)
    kp = pl.BlockSpec((ATT_BLOCK, LANES), prev)
    return qs, kc, kp


def _pair_rows(qp):
    lane = _iota((ATT_BLOCK, LANES), 1)
    zero = jnp.zeros_like(qp)
    return jnp.concatenate([jnp.where(lane < 64, qp, zero), jnp.where(lane >= 64, qp, zero)], axis=0)


def _pair_masks(n):
    qi = _iota((2 * ATT_BLOCK, 2 * ATT_BLOCK), 0) & (ATT_BLOCK - 1)
    kj = _iota((2 * ATT_BLOCK, 2 * ATT_BLOCK), 1)
    valid = (kj > qi) & (kj <= qi + ATT_BLOCK) & ((kj >= ATT_BLOCK) | (n > 0))
    return valid, _iota((2 * ATT_BLOCK, 1), 0) >= ATT_BLOCK


def _attn_fwd(q, kk, vlo, vhi, sinks, name="attn_fwd", carry=None):
    s = q.shape[0]
    nb = s // ATT_BLOCK
    scale = ATT_HEAD_DIM ** -0.5

    def body(sink_ref, q_ref, kc_ref, kp_ref, vloc_ref, vlop_ref, vhic_ref, vhip_ref, o_ref):
        n, h = pl.program_id(0), pl.program_id(1)
        valid, upper = _pair_masks(n)
        kw = jnp.concatenate([kp_ref[...], kc_ref[...]], axis=0)
        vcat = jnp.concatenate([vlop_ref[...], vloc_ref[...], vhip_ref[...], vhic_ref[...]], axis=0)
        for jp in range(ATT_GROUP // 2):
            q2 = _pair_rows(q_ref[:, jp * LANES:(jp + 1) * LANES])
            sink = jnp.where(upper, sink_ref[h * ATT_GROUP + 2 * jp + 1], sink_ref[h * ATT_GROUP + 2 * jp])
            probs, _ = _softmax_sink(_dot_nt(q2, kw) * scale, valid, sink)
            pcat = jnp.concatenate([probs[:ATT_BLOCK], probs[ATT_BLOCK:]], axis=1).astype(BF16)
            o_ref[:, jp * LANES:(jp + 1) * LANES] = _dot(pcat, vcat).astype(BF16)

    qs, kc, kp = _attn_specs("nh")
    res = _call(body, carry, name, (nb, ATT_KV_HEADS),
                [pl.BlockSpec(memory_space=pltpu.SMEM), qs, kc, kp, kc, kp, kc, kp], [qs],
                [jax.ShapeDtypeStruct((s, Q_WIDTH), BF16)], [], ("parallel", "parallel"),
                (sinks, q, kk, kk, vlo, vlo, vhi, vhi))
    return res[0] if carry is None else (res[0], res[1:])


def _attn_bwd(q, kk, vlo, vhi, sinks, do, name="attn_bwd", carry=None):
    s = q.shape[0]
    nb = s // ATT_BLOCK
    scale = ATT_HEAD_DIM ** -0.5

    def body(sink_ref, q_ref, kc_ref, kp_ref, vloc_ref, vlop_ref, vhic_ref, vhip_ref, do_ref,
             dq_ref, dkc_ref, dkp_ref, dvloc_ref, dvlop_ref, dvhic_ref, dvhip_ref, dsink_ref, dkk_acc, dv_acc):
        h, n = pl.program_id(0), pl.program_id(1)
        valid, upper = _pair_masks(n)
        kw = jnp.concatenate([kp_ref[...], kc_ref[...]], axis=0)
        vcat = jnp.concatenate([vlop_ref[...], vloc_ref[...], vhip_ref[...], vhic_ref[...]], axis=0)
        lane = _iota((ATT_BLOCK, LANES), 1)
        sub = _iota((ATT_GROUP, LANES), 0)
        dsink = jnp.zeros((ATT_GROUP, LANES), F32)
        for jp in range(ATT_GROUP // 2):
            q2 = _pair_rows(q_ref[:, jp * LANES:(jp + 1) * LANES])
            dop = do_ref[:, jp * LANES:(jp + 1) * LANES]
            sink = jnp.where(upper, sink_ref[h * ATT_GROUP + 2 * jp + 1], sink_ref[h * ATT_GROUP + 2 * jp])
            probs, psink = _softmax_sink(_dot_nt(q2, kw) * scale, valid, sink)
            pcat = jnp.concatenate([probs[:ATT_BLOCK], probs[ATT_BLOCK:]], axis=1).astype(BF16)
            dpc = _dot_nt(dop, vcat)
            dprobs = jnp.concatenate([dpc[:, :2 * ATT_BLOCK], dpc[:, 2 * ATT_BLOCK:]], axis=0)
            dv_part = _dot_tn(pcat, dop)
            delta = jnp.sum(probs * dprobs, axis=-1, keepdims=True)
            ds = (probs * (dprobs - delta) * scale).astype(BF16)
            sd = psink * delta
            dsink = (dsink + jnp.where(sub == 2 * jp, -jnp.sum(sd[:ATT_BLOCK]), 0.0)
                     + jnp.where(sub == 2 * jp + 1, -jnp.sum(sd[ATT_BLOCK:]), 0.0))
            dq2 = _dot(ds, kw)
            dq_ref[:, jp * LANES:(jp + 1) * LANES] = jnp.where(lane < 64, dq2[:ATT_BLOCK], dq2[ATT_BLOCK:])
            dkk_part = _dot_tn(ds, q2)
            if jp == 0:
                dkk_acc[...], dv_acc[...] = dkk_part, dv_part
            else:
                dkk_acc[...] += dkk_part
                dv_acc[...] += dv_part
        blk = ATT_BLOCK
        dkp_ref[...], dkc_ref[...] = dkk_acc[:blk], dkk_acc[blk:]
        dvlop_ref[...], dvloc_ref[...] = dv_acc[:blk], dv_acc[blk:2 * blk]
        dvhip_ref[...], dvhic_ref[...] = dv_acc[2 * blk:3 * blk], dv_acc[3 * blk:]

        @pl.when(n == 0)
        def _():
            dsink_ref[0] = dsink

        @pl.when(n > 0)
        def _():
            dsink_ref[0] += dsink

    qs, kc, kp = _attn_specs("hn")
    kw_shape = jax.ShapeDtypeStruct((s, ATT_KV_HEADS * LANES), F32)
    res = _call(body, carry, name, (ATT_KV_HEADS, nb),
                [pl.BlockSpec(memory_space=pltpu.SMEM), qs, kc, kp, kc, kp, kc, kp, qs],
                [qs] + [kc] * 6 + [pl.BlockSpec((1, ATT_GROUP, LANES), lambda h, n: (h, 0, 0))],
                [jax.ShapeDtypeStruct((s, Q_WIDTH), F32)] + [kw_shape] * 6
                + [jax.ShapeDtypeStruct((ATT_KV_HEADS, ATT_GROUP, LANES), F32)],
                [pltpu.VMEM((2 * ATT_BLOCK, LANES), F32), pltpu.VMEM((4 * ATT_BLOCK, LANES), F32)],
                ("parallel", "arbitrary"), (sinks, q, kk, kk, vlo, vlo, vhi, vhi, do))
    return res if carry is None else (res[:8], res[8:])


def _attn_prep_bwd(qkv, cos, sin, gq, gk, dq, dks, dvlos, dvhis):
    s, w = qkv.shape
    tr = ATT_BLOCK
    nb = s // tr

    def body(x_ref, cos_ref, sin_ref, gq_ref, gk_ref, dq_ref, dkc_ref, dkn_ref, dvloc_ref, dvlon_ref,
             dvhic_ref, dvhin_ref, dx_ref, dgq_ref, dgk_ref):
        n = pl.program_id(0)
        k = _lane_consts()
        cosv, sinv = cos_ref[...], sin_ref[...]
        nxt = jnp.where(n < nb - 1, 1.0, 0.0)
        lane = _iota((tr, LANES), 1)
        dgq = jnp.zeros((1, LANES), F32)
        dgk = jnp.zeros((1, LANES), F32)
        for j in range(Q_WIDTH // LANES):
            sl = slice(j * LANES, (j + 1) * LANES)
            _, xhat, rinv = _norm_rope(x_ref[:, sl], gq_ref[...], cosv, sinv, k)
            dx, dg = _norm_rope_bwd(dq_ref[:, sl], xhat, rinv, gq_ref[...], cosv, sinv, k)
            dx_ref[:, sl] = dx.astype(BF16)
            dgq = dgq + dg
        for i in range(KV_WIDTH // LANES):
            a, b = slice(2 * i * LANES, (2 * i + 1) * LANES), slice((2 * i + 1) * LANES, (2 * i + 2) * LANES)
            dr = (_dot_x(dkc_ref[:, a] + nxt * dkn_ref[:, a], k["fold_lo"])
                  + _dot_x(dkc_ref[:, b] + nxt * dkn_ref[:, b], k["fold_hi"]))
            sl = slice(Q_WIDTH + i * LANES, Q_WIDTH + (i + 1) * LANES)
            _, xhat, rinv = _norm_rope(x_ref[:, sl], gk_ref[...], cosv, sinv, k)
            dx, dg = _norm_rope_bwd(dr, xhat, rinv, gk_ref[...], cosv, sinv, k)
            dx_ref[:, sl] = dx.astype(BF16)
            dgk = dgk + dg
            ta = jnp.where(lane < 64, dvloc_ref[:, a] + nxt * dvlon_ref[:, a], dvhic_ref[:, a] + nxt * dvhin_ref[:, a])
            tb = jnp.where(lane < 64, dvloc_ref[:, b] + nxt * dvlon_ref[:, b], dvhic_ref[:, b] + nxt * dvhin_ref[:, b])
            sl = slice(Q_WIDTH + KV_WIDTH + i * LANES, Q_WIDTH + KV_WIDTH + (i + 1) * LANES)
            dx_ref[:, sl] = (_dot_x(ta, k["fold_lo"]) + _dot_x(tb, k["fold_hi"])).astype(BF16)

        @pl.when(n == 0)
        def _():
            dgq_ref[...] = dgq
            dgk_ref[...] = dgk

        @pl.when(n > 0)
        def _():
            dgq_ref[...] += dgq
            dgk_ref[...] += dgk

    row = lambda width: pl.BlockSpec((tr, width), lambda i: (i, 0))
    nxt_row = pl.BlockSpec((tr, ATT_KV_HEADS * LANES), lambda i: (jnp.minimum(i + 1, nb - 1), 0))
    vec = pl.BlockSpec((1, LANES), lambda i: (0, 0))
    kw = ATT_KV_HEADS * LANES
    return pl.pallas_call(
        body, name="attn_prep_bwd", grid=(nb,),
        in_specs=[row(w), row(LANES), row(LANES), vec, vec, row(Q_WIDTH),
                  row(kw), nxt_row, row(kw), nxt_row, row(kw), nxt_row],
        out_specs=[row(w), vec, vec],
        out_shape=[jax.ShapeDtypeStruct((s, w), BF16), jax.ShapeDtypeStruct((1, LANES), F32),
                   jax.ShapeDtypeStruct((1, LANES), F32)],
        compiler_params=_params("arbitrary"),
    )(qkv, cos, sin, gq, gk, dq, dks[0], dks[1], dvlos[0], dvlos[1], dvhis[0], dvhis[1])


CONV_HALO = 8
CONV_TC = 512
XBC_OFF = SSM_D_INNER // CONV_TC
DT_OFF = SSM_D_INNER + SSM_CONV_DIM


def _conv_pre(ext, w_ref, b_ref, ts):
    pre = b_ref[...] + w_ref[SSM_CONV - 1:SSM_CONV, :] * ext[CONV_HALO:]
    for kk in range(SSM_CONV - 1):
        pre = pre + w_ref[kk:kk + 1, :] * pltpu.roll(ext, SSM_CONV - 1 - kk, 0)[CONV_HALO:]
    return pre


def _conv_specs(ts):
    tc = CONV_TC
    src = pl.BlockSpec((ts, tc), lambda j, i: (i, XBC_OFF + j))
    halo = pl.BlockSpec((CONV_HALO, tc), lambda j, i: (jnp.maximum(i * (ts // CONV_HALO) - 1, 0), XBC_OFF + j))
    blk = pl.BlockSpec((ts, tc), lambda j, i: (i, j))
    wspec = pl.BlockSpec((SSM_CONV, tc), lambda j, i: (0, j))
    bspec = pl.BlockSpec((1, tc), lambda j, i: (0, j))
    return src, halo, blk, wspec, bspec


def _conv_fwd(zx, w, b):
    s, c = zx.shape[0], SSM_CONV_DIM
    ts = _tile(s, 512, 8)

    def body(u_ref, halo_ref, w_ref, b_ref, o_ref):
        halo = jnp.where(pl.program_id(1) > 0, halo_ref[...], 0.0)
        pre = _conv_pre(jnp.concatenate([halo, u_ref[...]], axis=0), w_ref, b_ref, ts)
        o_ref[...] = pre * _sigmoid(pre)

    src, halo, blk, wspec, bspec = _conv_specs(ts)
    return pl.pallas_call(
        body, name="conv_fwd", grid=(c // CONV_TC, s // ts),
        in_specs=[src, halo, wspec, bspec], out_specs=blk, out_shape=jax.ShapeDtypeStruct((s, c), F32),
        compiler_params=_params("parallel", "parallel"),
    )(zx, zx, w, b)


def _conv_bwd_pre(zx, w, b, dxs, dbm, dcm):
    s, c = zx.shape[0], SSM_CONV_DIM
    ts = _tile(s, 512, 8)
    nx, nb = dxs.shape[1] // CONV_TC, dbm.shape[1] // CONV_TC

    def body(u_ref, halo_ref, w_ref, b_ref, dx_ref, dbm_ref, dcm_ref, dpre_ref, dw_ref, db_ref):
        j, i = pl.program_id(0), pl.program_id(1)
        halo = jnp.where(i > 0, halo_ref[...], 0.0)
        ext = jnp.concatenate([halo, u_ref[...]], axis=0)
        pre = _conv_pre(ext, w_ref, b_ref, ts)
        sg = _sigmoid(pre)
        da = jnp.where(j < nx, dx_ref[...], jnp.where(j < nx + nb, dbm_ref[...], dcm_ref[...]))
        dpre = da * sg * (1.0 + pre * (1.0 - sg))
        dpre_ref[...] = dpre
        rows = [jnp.sum(dpre * pltpu.roll(ext, SSM_CONV - 1 - kk, 0)[CONV_HALO:], axis=0, keepdims=True)
                for kk in range(SSM_CONV - 1)]
        rows.append(jnp.sum(dpre * ext[CONV_HALO:], axis=0, keepdims=True))
        dwp = jnp.concatenate(rows, axis=0)
        dbp = jnp.sum(dpre, axis=0, keepdims=True)

        @pl.when(i == 0)
        def _():
            dw_ref[...] = dwp
            db_ref[...] = dbp

        @pl.when(i > 0)
        def _():
            dw_ref[...] += dwp
            db_ref[...] += dbp

    src, halo, blk, wspec, bspec = _conv_specs(ts)

    def part(lo, n):
        return pl.BlockSpec((ts, CONV_TC), lambda j, i: (jnp.where((j >= lo) & (j < lo + n), i, 0),
                                                         jnp.clip(j - lo, 0, n - 1)))

    return pl.pallas_call(
        body, name="conv_bwd_pre", grid=(c // CONV_TC, s // ts),
        in_specs=[src, halo, wspec, bspec, part(0, nx), part(nx, nb), part(nx + nb, nb)],
        out_specs=[blk, wspec, bspec],
        out_shape=[jax.ShapeDtypeStruct((s, c), F32), jax.ShapeDtypeStruct((SSM_CONV, c), F32),
                   jax.ShapeDtypeStruct((1, c), F32)],
        compiler_params=_params("parallel", "arbitrary"),
    )(zx, zx, w, b, dxs, dbm, dcm)


def _conv_bwd_in(dpre, w, dzx):
    s, c = dpre.shape
    ts, tc = _tile(s, 512, 8), CONV_TC
    ns = s // ts

    def body(d_ref, halo_ref, w_ref, dzx_ref, o_ref):
        del dzx_ref
        halo = jnp.where(pl.program_id(1) < ns - 1, halo_ref[...], 0.0)
        ext = jnp.concatenate([d_ref[...], halo], axis=0)
        du = w_ref[SSM_CONV - 1:SSM_CONV, :] * ext[:ts]
        for kk in range(SSM_CONV - 1):
            du = du + w_ref[kk:kk + 1, :] * pltpu.roll(ext, ts + CONV_HALO - (SSM_CONV - 1 - kk), 0)[:ts]
        o_ref[...] = du.astype(BF16)

    blk = pl.BlockSpec((ts, tc), lambda j, i: (i, j))
    halo = pl.BlockSpec((CONV_HALO, tc), lambda j, i: (jnp.minimum((i + 1) * (ts // CONV_HALO), s // CONV_HALO - 1), j))
    return pl.pallas_call(
        body, name="conv_bwd_in", grid=(c // tc, ns),
        in_specs=[blk, halo, pl.BlockSpec((SSM_CONV, tc), lambda j, i: (0, j)), ANY],
        out_specs=pl.BlockSpec((ts, tc), lambda j, i: (i, XBC_OFF + j)),
        out_shape=jax.ShapeDtypeStruct(dzx.shape, BF16), input_output_aliases={3: 0},
        compiler_params=_params("parallel", "parallel"),
    )(dpre, dpre, w, dzx)


def _ssd_common(dt_ref, dtt_ref, bias_ref, biast_ref, alog_ref, alogt_ref):
    ln = SSM_CHUNK
    raw, rawt = dt_ref[0] + bias_ref[0], dtt_ref[0] + biast_ref[0]
    dt, dtt = _softplus(raw), _softplus(rawt)
    a, at = -jnp.exp(alog_ref[0]), -jnp.exp(alogt_ref[0])
    tri = jnp.where(_iota((ln, ln), 0) >= _iota((ln, ln), 1), 1.0, 0.0).astype(BF16)
    return dict(raw=raw, rawt=rawt, dt=dt, dtt=dtt, a=a, at=at, tri=tri,
                acum=_xdot(tri, dt * a), acumt=_dot_x_nt(dtt * at, tri))


def _ssd_specs(nc, rev):
    cidx = (lambda c: nc - 1 - c) if rev else (lambda c: c)
    ln = SSM_CHUNK
    xs = pl.BlockSpec((ln, SSM_GN), lambda g, c: (cidx(c), g))
    bs = pl.BlockSpec((ln, SSM_STATE), lambda g, c: (cidx(c), SSM_D_INNER // SSM_STATE + g))
    cs = pl.BlockSpec((ln, SSM_STATE), lambda g, c: (cidx(c), SSM_D_INNER // SSM_STATE + SSM_GROUPS + g))
    dt = pl.BlockSpec((1, ln, SSM_HPG), lambda g, c: (g, cidx(c), 0))
    dtt = pl.BlockSpec((1, SSM_HPG, ln), lambda g, c: (g, 0, cidx(c)))
    row = pl.BlockSpec((1, 1, SSM_HPG), lambda g, c: (g, 0, 0))
    col = pl.BlockSpec((1, SSM_HPG, 1), lambda g, c: (g, 0, 0))
    st = pl.BlockSpec((None, None, SSM_GN, SSM_STATE), lambda g, c: (cidx(c), g, 0, 0))
    return xs, bs, cs, dt, dtt, row, col, st


def _head_expand():
    return jnp.where((_iota((SSM_HPG, SSM_GN), 1) >> 6) == _iota((SSM_HPG, SSM_GN), 0), 1.0, 0.0).astype(BF16)


def _head_expand_t():
    return jnp.where((_iota((SSM_GN, SSM_HPG), 0) >> 6) == _iota((SSM_GN, SSM_HPG), 1), 1.0, 0.0).astype(BF16)


def _dot_x_tn(x, m):
    hi, mid, lo = _split3(x)
    return _dot_tn(hi, m) + _dot_tn(mid, m) + _dot_tn(lo, m)


def _ssd_fwd(xbc, dt_g, dt_gt, bias_r, bias_c, alog_r, alog_c, d_r):
    s = xbc.shape[0]
    ln = SSM_CHUNK
    nc = s // ln

    def body(x_ref, b_ref, c_ref, dt_ref, dtt_ref, bias_ref, biast_ref, alog_ref, alogt_ref, d_ref,
             y_ref, st_ref, state):
        @pl.when(pl.program_id(1) == 0)
        def _():
            state[...] = jnp.zeros_like(state)

        cm = _ssd_common(dt_ref, dtt_ref, bias_ref, biast_ref, alog_ref, alogt_ref)
        acum, acumt = cm["acum"], cm["acumt"]
        ex = _head_expand()
        acum_x = _dot_x(acum, ex)
        xv = x_ref[...]
        xdt = xv * _dot_x(cm["dt"], ex)
        xdtb = xdt.astype(BF16)
        bb, cb = b_ref[...].astype(BF16), c_ref[...].astype(BF16)
        cbm = _dot_nt(cb, bb)
        causal = _iota((ln, ln), 0) >= _iota((ln, ln), 1)
        s2 = state[...]
        st_ref[...] = s2
        for r in range(SSM_HPG):
            sl = slice(r * SSM_P, (r + 1) * SSM_P)
            decay = jnp.exp(jnp.where(causal, acum[:, r:r + 1] - acumt[r:r + 1, :], -jnp.inf))
            y_ref[:, sl] = _dot((cbm * decay).astype(BF16), xdtb[:, sl])
        y_ref[...] = (y_ref[...] + _dot_nt(cb, s2.astype(BF16)) * jnp.exp(acum_x) + _dot_x(d_ref[0], ex) * xv)
        last_x = acum_x[ln - 1:ln, :]
        elast = jnp.exp(_xdot(_head_expand_t(), acumt[:, ln - 1:ln]))
        state[...] = s2 * elast + _dot_tn((xdt * jnp.exp(last_x - acum_x)).astype(BF16), bb)

    xs, bs, cs, dts, dtts, row, col, st = _ssd_specs(nc, False)
    return pl.pallas_call(
        body, name="ssd_fwd", grid=(SSM_GROUPS, nc),
        in_specs=[xs, bs, cs, dts, dtts, row, col, row, col, row],
        out_specs=[xs, st],
        out_shape=[jax.ShapeDtypeStruct((s, SSM_D_INNER), F32),
                   jax.ShapeDtypeStruct((nc, SSM_GROUPS, SSM_GN, SSM_STATE), F32)],
        scratch_shapes=[pltpu.VMEM((SSM_GN, SSM_STATE), F32)],
        compiler_params=_params("parallel", "arbitrary"),
    )(xbc, xbc, xbc, dt_g, dt_gt, bias_r, bias_c, alog_r, alog_c, d_r)


def _ssd_bwd(xbc, dt_g, dt_gt, bias_r, bias_c, alog_r, alog_c, d_r, states, dy):
    s = xbc.shape[0]
    ln = SSM_CHUNK
    nc = s // ln

    def body(x_ref, b_ref, c_ref, dt_ref, dtt_ref, bias_ref, biast_ref, alog_ref, alogt_ref, d_ref,
             st_ref, dy_ref, dx_ref, db_ref, dc_ref, ddt_ref, ddtt_ref, dbias_ref, dbiast_ref,
             dalog_ref, dalogt_ref, dd_ref, dstate):
        step = pl.program_id(1)

        @pl.when(step == 0)
        def _():
            dstate[...] = jnp.zeros_like(dstate)

        cm = _ssd_common(dt_ref, dtt_ref, bias_ref, biast_ref, alog_ref, alogt_ref)
        dt, acum, acumt = cm["dt"], cm["acum"], cm["acumt"]
        ex, ext = _head_expand(), _head_expand_t()
        dt_x, acum_x = _dot_x(dt, ex), _dot_x(acum, ex)
        eac_x, to_end_x = jnp.exp(acum_x), jnp.exp(acum_x[ln - 1:ln, :] - acum_x)
        xv, dyv = x_ref[...], dy_ref[...]
        xdt = xv * dt_x
        xdtb, dyb = xdt.astype(BF16), dyv.astype(BF16)
        dyeb = (dyv * eac_x).astype(BF16)
        bb, cb = b_ref[...].astype(BF16), c_ref[...].astype(BF16)
        cbm = _dot_nt(cb, bb)
        s2, ds2 = st_ref[...], dstate[...]
        s2b, ds2b = s2.astype(BF16), ds2.astype(BF16)
        dxdt_state = _dot_nt(bb, ds2b) * to_end_x
        yoff = _dot_nt(cb, s2b) * eac_x
        dc_acc = _dot(dyeb, s2b)
        db_acc = _dot((xdt * to_end_x).astype(BF16), ds2b)
        f_rows = _dot_x_nt(xdt * dxdt_state, ex)
        elast = jnp.exp(acum[ln - 1:ln, :])
        dlast = (jnp.sum(f_rows, axis=0, keepdims=True)
                 + elast * jnp.sum(_dot_x_tn(ds2 * s2, ext), axis=0, keepdims=True))
        is_last = _iota((ln, 1), 0) == ln - 1
        dac_rows = _dot_x_nt(dyv * yoff, ex) - f_rows + jnp.where(is_last, dlast, 0.0)
        dstate[...] = ds2 * jnp.exp(_xdot(ext, acumt[:, ln - 1:ln])) + _dot_tn(dyeb, cb)
        causal = _iota((ln, ln), 0) >= _iota((ln, ln), 1)
        lane8 = _iota((ln, SSM_HPG), 1)
        sub8 = _iota((SSM_HPG, ln), 0)
        dcb = jnp.zeros((ln, ln), F32)
        dac_cols = jnp.zeros((SSM_HPG, ln), F32)
        for r in range(SSM_HPG):
            sl = slice(r * SSM_P, (r + 1) * SSM_P)
            decay = jnp.exp(jnp.where(causal, acum[:, r:r + 1] - acumt[r:r + 1, :], -jnp.inf))
            dx_ref[:, sl] = _dot_tn((cbm * decay).astype(BF16), dyb[:, sl])
            dcb_r = _dot_nt(dyb[:, sl], xdtb[:, sl]) * decay
            dcb = dcb + dcb_r
            e = dcb_r * cbm
            dac_rows = dac_rows + jnp.where(lane8 == r, jnp.sum(e, axis=-1, keepdims=True), 0.0)
            dac_cols = dac_cols + jnp.where(sub8 == r, jnp.sum(e, axis=0, keepdims=True), 0.0)
        dxdt = dx_ref[...] + dxdt_state
        ddt_all = _dot_x_nt(dxdt * xv, ex)
        dd_all = jnp.sum(_dot_x_nt(dyv * xv, ex), axis=0, keepdims=True)
        dx_ref[...] = dxdt * dt_x + _dot_x(d_ref[0], ex) * dyv
        dcbb = dcb.astype(BF16)
        dc_ref[...] = dc_acc + _dot(dcbb, bb)
        db_ref[...] = db_acc + _dot_tn(dcbb, cb)
        triu = jnp.where(_iota((ln, ln), 0) <= _iota((ln, ln), 1), 1.0, 0.0).astype(BF16)
        g_rows = _xdot(triu, dac_rows)
        g_cols = _dot_x(dac_cols, cm["tri"])
        d_rows = (ddt_all + g_rows * cm["a"]) * _sigmoid(cm["raw"])
        d_cols = -(g_cols * cm["at"]) * _sigmoid(cm["rawt"])
        ddt_ref[0] = d_rows
        ddtt_ref[0] = d_cols
        parts = (jnp.sum(d_rows, axis=0, keepdims=True), jnp.sum(d_cols, axis=1, keepdims=True),
                 jnp.sum(g_rows * dt, axis=0, keepdims=True) * cm["a"],
                 -jnp.sum(g_cols * cm["dtt"], axis=1, keepdims=True) * cm["at"], dd_all)
        outs = (dbias_ref, dbiast_ref, dalog_ref, dalogt_ref, dd_ref)

        @pl.when(step == 0)
        def _():
            for o_ref, p in zip(outs, parts):
                o_ref[0] = p

        @pl.when(step > 0)
        def _():
            for o_ref, p in zip(outs, parts):
                o_ref[0] += p

    xs, bs, cs, dts, dtts, row, col, st = _ssd_specs(nc, True)
    grp = pl.BlockSpec((ln, SSM_STATE), lambda g, c: (nc - 1 - c, g))
    rows = jax.ShapeDtypeStruct((SSM_GROUPS, 1, SSM_HPG), F32)
    cols = jax.ShapeDtypeStruct((SSM_GROUPS, SSM_HPG, 1), F32)
    return pl.pallas_call(
        body, name="ssd_bwd", grid=(SSM_GROUPS, nc),
        in_specs=[xs, bs, cs, dts, dtts, row, col, row, col, row, st, xs],
        out_specs=[xs, grp, grp, dts, dtts, row, col, row, col, row],
        out_shape=[jax.ShapeDtypeStruct((s, SSM_D_INNER), F32),
                   jax.ShapeDtypeStruct((s, SSM_GROUPS * SSM_STATE), F32),
                   jax.ShapeDtypeStruct((s, SSM_GROUPS * SSM_STATE), F32),
                   jax.ShapeDtypeStruct((SSM_GROUPS, s, SSM_HPG), F32),
                   jax.ShapeDtypeStruct((SSM_GROUPS, SSM_HPG, s), F32), rows, cols, rows, cols, rows],
        scratch_shapes=[pltpu.VMEM((SSM_GN, SSM_STATE), F32)],
        compiler_params=_params("parallel", "arbitrary"),
    )(xbc, xbc, xbc, dt_g, dt_gt, bias_r, bias_c, alog_r, alog_c, d_r, states, dy)


def _gate_norm_fwd(y, zx, g):
    s = y.shape[0]
    ts = _tile(s, 512, 8)

    def body(y_ref, z_ref, g_ref, o_ref):
        zv = z_ref[...]
        yg = y_ref[...] * (zv * _sigmoid(zv))
        r = lax.rsqrt(jnp.mean(yg * yg, axis=-1, keepdims=True) + EPS)
        o_ref[...] = (yg * r * g_ref[...]).astype(BF16)

    blk = pl.BlockSpec((ts, SSM_GN), lambda j, i: (i, j))
    vec = pl.BlockSpec((1, SSM_GN), lambda j, i: (0, j))
    return pl.pallas_call(
        body, name="gate_norm_fwd", grid=(SSM_GROUPS, s // ts), in_specs=[blk, blk, vec], out_specs=blk,
        out_shape=jax.ShapeDtypeStruct((s, SSM_D_INNER), BF16), compiler_params=_params("parallel", "parallel"),
    )(y, zx, g)


def _gate_norm_bwd(y, zx, g, dout):
    s = y.shape[0]
    ts = _tile(s, 512, 8)

    def body(y_ref, z_ref, g_ref, do_ref, dy_ref, dz_ref, dg_ref):
        yv, zv, dov = y_ref[...], z_ref[...], do_ref[...].astype(F32)
        sg = _sigmoid(zv)
        silu = zv * sg
        yg = yv * silu
        r = lax.rsqrt(jnp.mean(yg * yg, axis=-1, keepdims=True) + EPS)
        ygn = yg * r
        part = jnp.sum(dov * ygn, axis=0, keepdims=True)

        @pl.when(pl.program_id(1) == 0)
        def _():
            dg_ref[...] = part

        @pl.when(pl.program_id(1) > 0)
        def _():
            dg_ref[...] += part

        dn = dov * g_ref[...]
        dyg = r * (dn - ygn * jnp.mean(dn * ygn, axis=-1, keepdims=True))
        dy_ref[...] = dyg * silu
        dz_ref[...] = (dyg * yv * sg * (1.0 + zv * (1.0 - sg))).astype(BF16)

    blk = pl.BlockSpec((ts, SSM_GN), lambda j, i: (i, j))
    vec = pl.BlockSpec((1, SSM_GN), lambda j, i: (0, j))
    return pl.pallas_call(
        body, name="gate_norm_bwd", grid=(SSM_GROUPS, s // ts), in_specs=[blk, blk, vec, blk],
        out_specs=[blk, blk, vec],
        out_shape=[jax.ShapeDtypeStruct((s, SSM_D_INNER), F32), jax.ShapeDtypeStruct((s, SSM_IN_PAD), BF16),
                   jax.ShapeDtypeStruct((1, SSM_D_INNER), F32)],
        compiler_params=_params("parallel", "arbitrary"),
    )(y, zx, g, dout)


def _rope_tables(positions):
    inv_freq = ROPE_THETA ** (-jnp.arange(0, ATT_HEAD_DIM, 2, dtype=F32) / ATT_HEAD_DIM)
    ang = positions.astype(F32)[:, None] * inv_freq
    return jnp.tile(jnp.cos(ang), (1, 4)), jnp.tile(jnp.sin(ang), (1, 4))


def _group_views(v):
    return v.reshape(SSM_GROUPS, 1, SSM_HPG), v.reshape(SSM_GROUPS, SSM_HPG, 1)


def _ffn_fwd(run, x, norm_g, wg, wu, wd, tag):
    h = _rms_fwd(x, norm_g, f"ffn_norm_{tag}")
    g = run(f"ffn_gate_{tag}", _mm, h, wg, "nn", b_cols=True)
    u, a = run(f"ffn_up_{tag}", _mm, h, wu, "nn", b_cols=True, rows=FUSED_ROWS,
               fuse=(lambda uv, gv: (uv, gv * _sigmoid(gv) * uv), [g], [F32, BF16]))
    return run(f"ffn_down_{tag}", _mm, a, wd, "nn", add=x), (h, g, u, a)


def _ffn_bwd(run, mats, x, norm_g, wg, wu, wd, saved, dout, tag):
    h, g, u, a = saved

    def act_bwd(da, gv, uv):
        sg = _sigmoid(gv)
        return da * uv * sg * (1.0 + gv * (1.0 - sg)), da * (gv * sg)

    dg, du = run(f"ffn_down_dx_{tag}", _mm, dout, wd, "nt", rows=FUSED_ROWS,
                 fuse=(act_bwd, [g, u], [BF16, BF16]))
    dwd = run(f"ffn_down_dw_{tag}", _mm, a, dout, "tn", out_dtype=BF16)
    mats[("ffn_w_down", tag)] = dwd.reshape(N_SHARDS, dwd.shape[0] // N_SHARDS, dwd.shape[1])
    mats[("ffn_w_gate", tag)] = run(f"ffn_gate_dw_{tag}", _mm, h, dg, "tn", out_dtype=BF16, out_cols=True)
    mats[("ffn_w_up", tag)] = run(f"ffn_up_dw_{tag}", _mm, h, du, "tn", out_dtype=BF16, out_cols=True)
    dh = run(f"ffn_gate_dx_{tag}", _mm, dg, wg, "nt", b_cols=True)
    dh = run(f"ffn_up_dx_{tag}", _mm, du, wu, "nt", add=dh, b_cols=True)
    return _rms_bwd(x, norm_g, dh, dout, f"ffn_norm_bwd_{tag}")


class _Hook:
    def __init__(self, make, done):
        self.make, self.done = make, done


class _SemView:
    def __init__(self, sems, off):
        self.sems, self.off, self.at = sems, off, self

    def __getitem__(self, k):
        return self.sems.at[self.off + k]


def _both(h1, h2):
    split = {}

    def make():
        a, b = h1.make(), h2.make()
        na_in, na_out, na_sems = len(a["arrays"]), len(a["out_shapes"]), a["n_sems"]
        split["n"] = na_out

        def build(cin, cout, send_sems, recv_sems):
            return (a["build"](cin[:na_in], cout[:na_out], send_sems, recv_sems)
                    + b["build"](cin[na_in:], cout[na_out:], _SemView(send_sems, na_sems), _SemView(recv_sems, na_sems)))

        aliases = dict(a.get("aliases", {}))
        aliases.update({na_in + i: na_out + o for i, o in b.get("aliases", {}).items()})
        return dict(build=build, arrays=list(a["arrays"]) + list(b["arrays"]),
                    out_shapes=list(a["out_shapes"]) + list(b["out_shapes"]), n_sems=na_sems + b["n_sems"],
                    aliases=aliases)

    def done(res):
        h1.done(res[:split["n"]])
        h2.done(res[split["n"]:])

    return _Hook(make, done)


def _local_step(x, positions, target, w, hooks=None, mats=None):
    hooks = {} if hooks is None else hooks
    mats = {} if mats is None else mats

    def run(name, fn, *args, **kw):
        hook = hooks.get(name)
        if hook is None:
            return fn(*args, name=name, **kw)
        res, carried = fn(*args, name=name, carry=hook.make(), **kw)
        hook.done(carried)
        return res

    cos, sin = _rope_tables(positions)
    row = lambda v: v.reshape(1, -1)
    gq, gk = jnp.tile(row(w["attn_q_norm"]), (1, 2)), jnp.tile(row(w["attn_k_norm"]), (1, 2))
    sinks = w["attn_sinks"].reshape(-1)
    s = x.shape[0]
    row_stack = lambda g: g.reshape(N_SHARDS, g.shape[0] // N_SHARDS, g.shape[1])

    h0 = _rms_fwd(x, row(w["mixer_norm"][0]), "mixer_norm_0")
    qkv = run("attn_qkv", _mm, h0, w["attn_w_qkv"], "nn", b_cols=True)
    q, kk, vlo, vhi = _attn_prep(qkv, cos, sin, gq, gk)
    o = run("attn_fwd", _attn_fwd, q, kk, vlo, vhi, sinks)
    x1 = run("attn_out", _mm, o, w["attn_w_o"], "nn", add=x)
    ffn_w = lambda l: (row(w["ffn_norm"][l]), w["ffn_w_gate"][l], w["ffn_w_up"][l], w["ffn_w_down"][l])
    x2, ffn0 = _ffn_fwd(run, x1, *ffn_w(0), 0)

    h2 = _rms_fwd(x2, row(w["mixer_norm"][1]), "mixer_norm_1")
    zx = run("ssm_in", _mm, h2, w["ssm_w_in"], "nn")
    dt_g = zx[:, DT_OFF:DT_OFF + SSM_HEADS].reshape(s, SSM_GROUPS, SSM_HPG).transpose(1, 0, 2)
    dt_gt = dt_g.transpose(0, 2, 1)
    bias_r, bias_c = _group_views(w["ssm_dt_bias"].reshape(-1))
    alog_r, alog_c = _group_views(w["ssm_a_log"].reshape(-1))
    d_r, _ = _group_views(w["ssm_d"].reshape(-1))
    xbc = _conv_fwd(zx, w["ssm_conv_w"], row(w["ssm_conv_b"]))
    ssd_args = (xbc, dt_g, dt_gt, bias_r, bias_c, alog_r, alog_c, d_r)
    y, states = _ssd_fwd(*ssd_args)
    yn = _gate_norm_fwd(y, zx, row(w["ssm_norm"]))
    x3 = run("ssm_out", _mm, yn, w["ssm_w_out"], "nn", add=x2)
    x4, ffn1 = _ffn_fwd(run, x3, *ffn_w(1), 1)

    loss_row, dx4 = _loss_fwd_bwd(x4, target)

    dx3, dfn1 = _ffn_bwd(run, mats, x3, *ffn_w(1), ffn1, dx4, 1)
    dyn = run("ssm_out_dx", _mm, dx3, w["ssm_w_out"], "nt")
    mats[("ssm_w_out", 0)] = row_stack(run("ssm_out_dw", _mm, yn, dx3, "tn", out_dtype=BF16))
    dy, dzx, dssm_norm = _gate_norm_bwd(y, zx, row(w["ssm_norm"]), dyn)
    dxs, db, dc, ddt_g, ddt_gt, dbias, dbias_t, dalog, dalog_t, dd = _ssd_bwd(*ssd_args, states, dy)
    ddt_g = ddt_g + ddt_gt.transpose(0, 2, 1)
    dpre, dconv_w, dconv_b = _conv_bwd_pre(zx, w["ssm_conv_w"], row(w["ssm_conv_b"]), dxs, db, dc)
    dzx = _conv_bwd_in(dpre, w["ssm_conv_w"], dzx)
    ddt_pad = jnp.pad(ddt_g.transpose(1, 0, 2).reshape(s, SSM_HEADS), ((0, 0), (0, SSM_IN_PAD - SSM_IN)))
    dzx = lax.dynamic_update_slice(dzx, ddt_pad.astype(BF16), (0, DT_OFF))
    dw_in = run("ssm_in_dw", _mm, h2, dzx, "tn", out_dtype=BF16)
    in_shard = SSM_IN // N_SHARDS
    mats[("ssm_w_in", 0)] = jnp.stack([dw_in[:, i * in_shard:(i + 1) * in_shard] for i in range(N_SHARDS)])
    dh2 = run("ssm_in_dx", _mm, dzx, w["ssm_w_in"], "nt")
    dx2, dmn1 = _rms_bwd(x2, row(w["mixer_norm"][1]), dh2, dx3, "mixer_norm_bwd_1")

    dx1, dfn0 = _ffn_bwd(run, mats, x1, *ffn_w(0), ffn0, dx2, 0)
    do = run("attn_out_dx", _mm, dx1, w["attn_w_o"], "nt", out_dtype=BF16)
    mats[("attn_w_o", 0)] = row_stack(run("attn_out_dw", _mm, o, dx1, "tn", out_dtype=BF16))
    dq, dkc, dkp, dvloc, dvlop, dvhic, dvhip, dsink = run("attn_bwd", _attn_bwd, q, kk, vlo, vhi, sinks, do)
    dqkv, dgq, dgk = _attn_prep_bwd(qkv, cos, sin, gq, gk, dq, (dkc, dkp), (dvloc, dvlop), (dvhic, dvhip))
    mats[("attn_w_qkv", 0)] = run("attn_qkv_dw", _mm, h0, dqkv, "tn", out_dtype=BF16, out_cols=True)
    dh0 = run("attn_qkv_dx", _mm, dqkv, w["attn_w_qkv"], "nt", b_cols=True)
    dx0, dmn0 = _rms_bwd(x, row(w["mixer_norm"][0]), dh0, dx1, "mixer_norm_bwd_0")

    fold = lambda v: v[0, :ATT_HEAD_DIM] + v[0, ATT_HEAD_DIM:]
    grads = {
        "mixer_norm": jnp.concatenate([dmn0, dmn1], axis=0),
        "ffn_norm": jnp.concatenate([dfn0, dfn1], axis=0),
        "attn_q_norm": fold(dgq), "attn_k_norm": fold(dgk),
        "attn_sinks": dsink[:, :, 0].reshape(-1),
        "ssm_conv_w": dconv_w, "ssm_conv_b": dconv_b.reshape(-1),
        "ssm_dt_bias": dbias.reshape(-1) + dbias_t.reshape(-1),
        "ssm_a_log": dalog.reshape(-1) + dalog_t.reshape(-1), "ssm_d": dd.reshape(-1),
        "ssm_norm": dssm_norm.reshape(-1),
    }
    return loss_row[0, 0], dx0, grads


OTHER_CHIPS = ((1, 0), (0, 1), (1, 1))


def _position():
    return lax.axis_index("x"), lax.axis_index("y"), lax.axis_index("c")


def _sems(n):
    return pltpu.SemaphoreType.DMA((n,))


def _gather_shards(weights, layers):
    n_in, n_mat = len(weights), len(layers)

    def body(*refs):
        p, out = refs[:n_in], refs[n_in:n_in + n_mat]
        send_sems, recv_sems = refs[n_in + n_mat:]
        x, y, c = _position()
        me, sibling = (x, y, c), (x, y, 1 - c)
        chips = [(x ^ fx, y ^ fy) for fx, fy in OTHER_CHIPS]

        def rows(e, px, py, pc):
            half = out[e].shape[1] // 2
            return out[e].at[2 * px + py, pl.ds(pc * half, half), :]

        def copy(k, e, block, to, src=None):
            return pltpu.make_async_remote_copy(
                src_ref=rows(e, *block) if src is None else src, dst_ref=rows(e, *block),
                send_sem=send_sems.at[k * n_mat + e], recv_sem=recv_sems.at[k * n_mat + e],
                device_id=to, device_id_type=MESH)

        def own(e):
            i, l = layers[e]
            return pltpu.make_async_remote_copy(
                src_ref=p[i].at[l], dst_ref=out[e].at[2 * x + y], send_sem=send_sems.at[6 * n_mat + e],
                recv_sem=recv_sems.at[6 * n_mat + e], device_id=sibling, device_id_type=MESH)

        first, passed = [], []
        for e, (i, l) in enumerate(layers):
            half = out[e].shape[1] // 2
            first.append([copy(j, e, me, (*chip, c), src=p[i].at[l, pl.ds(c * half, half), :])
                          for j, chip in enumerate(chips)])
            for cp in first[-1]:
                cp.start()
        for e in range(n_mat):
            own(e).start()
        for e in range(n_mat):
            passed.append([copy(3 + j, e, (*chip, c), sibling) for j, chip in enumerate(chips)])
            for j, chip in enumerate(chips):
                copy(j, e, (*chip, c), me).wait_recv()
                passed[e][j].start()
        for e in range(n_mat):
            own(e).wait()
            for j, chip in enumerate(chips):
                copy(3 + j, e, (*chip, 1 - c), me).wait_recv()
        for e in range(n_mat):
            for cp in first[e] + passed[e]:
                cp.wait_send()

    return pl.pallas_call(
        body, name="gather_weights", in_specs=[ANY] * n_in, out_specs=[ANY] * n_mat,
        out_shape=[jax.ShapeDtypeStruct((N_SHARDS,) + weights[i].shape[1:], weights[i].dtype) for i, _ in layers],
        scratch_shapes=[_sems(7 * n_mat), _sems(7 * n_mat)],
    )(*weights)


def _all_gather8(block, name):
    m_per, n = block.shape

    def body(x_ref, out_ref, send_sems, recv_sems, local_sem):
        x, y, c = _position()
        me, sibling = (x, y, c), (x, y, 1 - c)
        chips = [(x ^ fx, y ^ fy) for fx, fy in OTHER_CHIPS]

        def rows(px, py, pc):
            return out_ref.at[pl.ds((4 * px + 2 * py + pc) * m_per, m_per), :]

        def copy(k, blk, to, src=None):
            return pltpu.make_async_remote_copy(
                src_ref=rows(*blk) if src is None else src, dst_ref=rows(*blk),
                send_sem=send_sems.at[k], recv_sem=recv_sems.at[k], device_id=to, device_id_type=MESH)

        mine = pltpu.make_async_copy(x_ref, rows(*me), local_sem)
        mine.start()
        first = [copy(0, me, sibling, src=x_ref)]
        first += [copy(1 + j, me, (*chip, c), src=x_ref) for j, chip in enumerate(chips)]
        for cp in first:
            cp.start()
        passed = [copy(4 + j, (*chip, c), sibling) for j, chip in enumerate(chips)]
        for j, chip in enumerate(chips):
            copy(1 + j, (*chip, c), me).wait_recv()
            passed[j].start()
        copy(0, sibling, me).wait_recv()
        for j, chip in enumerate(chips):
            copy(4 + j, (*chip, 1 - c), me).wait_recv()
        for cp in first + passed:
            cp.wait_send()
        mine.wait()

    return pl.pallas_call(
        body, name=name, out_shape=jax.ShapeDtypeStruct((N_DEV * m_per, n), block.dtype),
        in_specs=[pl.BlockSpec(memory_space=pltpu.VMEM)], out_specs=pl.BlockSpec(memory_space=pltpu.VMEM),
        scratch_shapes=[_sems(7), _sems(7), pltpu.SemaphoreType.DMA],
    )(block)


def _exchange(carry, name):
    n_in, n_out = len(carry["arrays"]), len(carry["out_shapes"])

    def body(*refs):
        copies = carry["build"](refs[:n_in], refs[n_in:n_in + n_out], refs[-2], refs[-1])
        for cp in copies:
            cp.start()
        for cp in copies:
            cp.wait()

    return pl.pallas_call(
        body, name=name, in_specs=[ANY] * n_in, out_specs=[ANY] * n_out, out_shape=list(carry["out_shapes"]),
        input_output_aliases=dict(carry.get("aliases", {})),
        scratch_shapes=[_sems(carry["n_sems"]), _sems(carry["n_sems"])],
    )(*carry["arrays"])


def _remote(src, dst, send_sems, recv_sems, k, to):
    return pltpu.make_async_remote_copy(src_ref=src, dst_ref=dst, send_sem=send_sems.at[k], recv_sem=recv_sems.at[k],
                                        device_id=to, device_id_type=MESH)


def _gather_over_ici(blocks, layers):
    def build(p, out, send_sems, recv_sems):
        x, y, c = _position()
        copies = []
        for e, l in enumerate(layers):
            half = out[e].shape[1] // 2
            rows = pl.ds(c * half, half)
            for j, (fx, fy) in enumerate(OTHER_CHIPS):
                copies.append(_remote(p[e].at[l, rows, :], out[e].at[2 * x + y, rows, :], send_sems, recv_sems,
                                      3 * e + j, (x ^ fx, y ^ fy, c)))
        return copies

    shapes = [jax.ShapeDtypeStruct((N_SHARDS,) + b.shape[1:], b.dtype) for b in blocks]
    return dict(build=build, arrays=list(blocks), out_shapes=shapes, n_sems=3 * len(layers))


def _gather_over_d2d(stacks, blocks, layers):
    n = len(stacks)

    def build(refs, out, send_sems, recv_sems):
        p = refs[n:]
        x, y, c = _position()
        sibling = (x, y, 1 - c)
        copies = []
        for e, l in enumerate(layers):
            half = out[e].shape[1] // 2
            for j, (fx, fy) in enumerate(OTHER_CHIPS):
                rows = out[e].at[2 * (x ^ fx) + (y ^ fy), pl.ds(c * half, half), :]
                copies.append(_remote(rows, rows, send_sems, recv_sems, 4 * e + j, sibling))
            copies.append(_remote(p[e].at[l], out[e].at[2 * x + y], send_sems, recv_sems, 4 * e + 3, sibling))
        return copies

    shapes = [jax.ShapeDtypeStruct(s.shape, s.dtype) for s in stacks]
    return dict(build=build, arrays=list(stacks) + list(blocks), out_shapes=shapes, n_sems=4 * n,
                aliases={i: i for i in range(n)})


def _grads_to_sibling(stacks):
    def build(g, out, send_sems, recv_sems):
        x, y, c = _position()
        copies = []
        for e in range(len(stacks)):
            half = g[e].shape[1] // 2
            copies.append(_remote(g[e].at[:, pl.ds((1 - c) * half, half), :], out[e], send_sems, recv_sems, e,
                                  (x, y, 1 - c)))
        return copies

    shapes = [jax.ShapeDtypeStruct((N_SHARDS, g.shape[1] // 2, g.shape[2]), g.dtype) for g in stacks]
    return dict(build=build, arrays=list(stacks), out_shapes=shapes, n_sems=len(stacks))


def _grads_to_owners(partials):
    def build(p, out, send_sems, recv_sems):
        x, y, c = _position()
        copies = []
        for e in range(len(partials)):
            for k, (fx, fy) in enumerate(OTHER_CHIPS):
                px, py = x ^ fx, y ^ fy
                copies.append(_remote(p[e].at[2 * px + py], out[e].at[k], send_sems, recv_sems, 3 * e + k,
                                      (px, py, c)))
        return copies

    shapes = [jax.ShapeDtypeStruct((len(OTHER_CHIPS),) + p.shape[1:], p.dtype) for p in partials]
    return dict(build=build, arrays=list(partials), out_shapes=shapes, n_sems=3 * len(partials))


def _share_halves(grads, layers):
    n = len(grads)

    def body(*refs):
        out, send_sems, recv_sems = refs[n:2 * n], refs[-2], refs[-1]
        x, y, c = _position()
        copies = []
        for e, (i, l) in enumerate(layers):
            half = out[i].shape[1] // 2
            rows = out[i].at[l, pl.ds(c * half, half), :]
            copies.append(pltpu.make_async_remote_copy(
                src_ref=rows, dst_ref=rows, send_sem=send_sems.at[e], recv_sem=recv_sems.at[e],
                device_id=(x, y, 1 - c), device_id_type=MESH))
            copies[-1].start()
        for cp in copies:
            cp.wait()

    return pl.pallas_call(
        body, name="grads_share_halves", in_specs=[ANY] * n, out_specs=[ANY] * n,
        out_shape=[jax.ShapeDtypeStruct(g.shape, g.dtype) for g in grads],
        input_output_aliases={i: i for i in range(n)},
        scratch_shapes=[_sems(len(layers)), _sems(len(layers))],
    )(*grads)


ADD_BLOCK_ELEMS = 1 << 19


def _add_rows(half, cols):
    return _tile(half, max(16, ADD_BLOCK_ELEMS // cols // 16 * 16), 16)


def _add_pair(stack, recv, c_idx, name):
    _, half, cols = recv.shape
    tr = _add_rows(half, cols)
    nt = half // tr

    def body(c_ref, a_ref, b_ref, o_ref):
        o_ref[...] = (a_ref[...].astype(F32) + b_ref[...].astype(F32)).astype(o_ref.dtype)

    blk = pl.BlockSpec((None, tr, cols), lambda s, i, c_ref: (s, i, 0))
    return pl.pallas_call(
        body, name=name,
        grid_spec=pltpu.PrefetchScalarGridSpec(
            num_scalar_prefetch=1, grid=(N_SHARDS, nt),
            in_specs=[pl.BlockSpec((None, tr, cols), lambda s, i, c_ref: (s, c_ref[0] * nt + i, 0)), blk],
            out_specs=blk),
        out_shape=jax.ShapeDtypeStruct(recv.shape, recv.dtype),
        compiler_params=_params("parallel", "parallel"),
    )(c_idx, stack, recv)


def _add_owned(partial, recv, sc_idx, layer, shape, into, name):
    _, half, cols = partial.shape
    tr = _add_rows(half, cols)
    nt = half // tr

    def body(sc_ref, a_ref, r0_ref, r1_ref, r2_ref, *rest):
        o_ref = rest[-1]
        o_ref[...] = (((a_ref[...].astype(F32) + r0_ref[...].astype(F32)) + r1_ref[...].astype(F32))
                      + r2_ref[...].astype(F32))

    slot = lambda k: pl.BlockSpec((None, tr, cols), lambda i, sc_ref: (k, i, 0))
    has_into = into is not None
    return pl.pallas_call(
        body, name=name,
        grid_spec=pltpu.PrefetchScalarGridSpec(
            num_scalar_prefetch=1, grid=(nt,),
            in_specs=[pl.BlockSpec((None, tr, cols), lambda i, sc_ref: (sc_ref[0], i, 0)), slot(0), slot(1), slot(2)]
            + ([ANY] if has_into else []),
            out_specs=pl.BlockSpec((None, tr, cols), lambda i, sc_ref: (layer, sc_ref[1] * nt + i, 0))),
        out_shape=jax.ShapeDtypeStruct(shape, F32),
        input_output_aliases={5: 0} if has_into else {},
        compiler_params=_params("parallel"),
    )(*((sc_idx, partial, recv, recv, recv) + ((into,) if has_into else ())))


def _sum8(gathered):
    m = gathered.shape[0] // N_DEV

    def body(g_ref, o_ref):
        total = g_ref[0:m, :]
        for d in range(1, N_DEV):
            total = total + g_ref[d * m:(d + 1) * m, :]
        o_ref[...] = total

    return pl.pallas_call(
        body, name="small_grads_sum", out_shape=jax.ShapeDtypeStruct((m, LANES), F32),
        in_specs=[pl.BlockSpec(memory_space=pltpu.VMEM)], out_specs=pl.BlockSpec(memory_space=pltpu.VMEM),
    )(gathered)


ADAMW_BLOCK_ELEMS = 1 << 18


def _adamw(w, g, m, v, name):
    l, r, cols = w.shape
    tr = _tile(r, max(8, ADAMW_BLOCK_ELEMS // cols // 8 * 8), 8)

    def body(w_ref, g_ref, m_ref, v_ref, d_ref, nm_ref, nv_ref):
        gv = g_ref[...]
        nm = ADAM_B1 * m_ref[...] + (1.0 - ADAM_B1) * gv
        nv = ADAM_B2 * v_ref[...] + (1.0 - ADAM_B2) * jnp.square(gv)
        m_hat = nm / (1.0 - ADAM_B1 ** ADAM_STEP)
        v_hat = nv / (1.0 - ADAM_B2 ** ADAM_STEP)
        d_ref[...] = -ADAM_LR * (m_hat / (jnp.sqrt(v_hat) + ADAM_EPS) + ADAM_WD * w_ref[...])
        nm_ref[...] = nm
        nv_ref[...] = nv

    blk = pl.BlockSpec((None, tr, cols), lambda a, i: (a, i, 0))
    return pl.pallas_call(
        body, name=name, grid=(l, r // tr), in_specs=[blk] * 4, out_specs=[blk] * 3,
        out_shape=[jax.ShapeDtypeStruct(w.shape, F32)] * 3, compiler_params=_params("parallel", "parallel"),
    )(w, g, m, v)


WEIGHTS = ("mixer_norm", "ffn_norm", "attn_w_qkv", "attn_q_norm", "attn_k_norm", "attn_sinks", "attn_w_o",
           "ssm_w_in", "ssm_conv_w", "ssm_conv_b", "ssm_dt_bias", "ssm_a_log", "ssm_d", "ssm_norm", "ssm_w_out",
           "ffn_w_gate", "ffn_w_up", "ffn_w_down")
BIG = ("attn_w_qkv", "attn_w_o", "ffn_w_gate", "ffn_w_up", "ffn_w_down", "ssm_w_in", "ssm_w_out")
MATRICES = (("attn_w_qkv", 0), ("attn_w_o", 0), ("ffn_w_gate", 0), ("ffn_w_up", 0), ("ffn_w_down", 0),
            ("ssm_w_in", 0), ("ssm_w_out", 0), ("ffn_w_gate", 1), ("ffn_w_up", 1), ("ffn_w_down", 1))
MATRIX_LAYERS = tuple((BIG.index(n), l) for n, l in MATRICES)
GROUPS = {"attn": MATRICES[0:2], "ffn0": MATRICES[2:5], "ssm": MATRICES[5:7], "ffn1": MATRICES[7:10]}
SMALL_SHARDED = ("ssm_conv_w", "ssm_conv_b", "ssm_norm")
SMALL = tuple(n for n in WEIGHTS if n not in BIG)


def _pack_rows(parts, row_unit=8):
    flat = jnp.concatenate([p.reshape(-1) for p in parts])
    pad = (-flat.shape[0]) % (LANES * row_unit)
    return jnp.pad(flat, (0, pad)).reshape(-1, LANES)


def _unpack(flat, shapes):
    out, off = [], 0
    for shp in shapes:
        size = math.prod(shp)
        out.append(flat[off:off + size].reshape(shp))
        off += size
    return out


def kernel(x, positions, mixer_norm, ffn_norm, attn_w_qkv, attn_q_norm, attn_k_norm, attn_sinks, attn_w_o, ssm_w_in, ssm_conv_w, ssm_conv_b, ssm_dt_bias, ssm_a_log, ssm_d, ssm_norm, ssm_w_out, ffn_w_gate, ffn_w_up, ffn_w_down, loss_target, m_mixer_norm, m_ffn_norm, m_attn_w_qkv, m_attn_q_norm, m_attn_k_norm, m_attn_sinks, m_attn_w_o, m_ssm_w_in, m_ssm_conv_w, m_ssm_conv_b, m_ssm_dt_bias, m_ssm_a_log, m_ssm_d, m_ssm_norm, m_ssm_w_out, m_ffn_w_gate, m_ffn_w_up, m_ffn_w_down, v_mixer_norm, v_ffn_norm, v_attn_w_qkv, v_attn_q_norm, v_attn_k_norm, v_attn_sinks, v_attn_w_o, v_ssm_w_in, v_ssm_conv_w, v_ssm_conv_b, v_ssm_dt_bias, v_ssm_a_log, v_ssm_d, v_ssm_norm, v_ssm_w_out, v_ffn_w_gate, v_ffn_w_up, v_ffn_w_down):
    args = locals()
    w = {n: args[n] for n in WEIGHTS}
    m = {n: args["m_" + n] for n in WEIGHTS}
    v = {n: args["v_" + n] for n in WEIGHTS}
    ax, ay, ac = lax.axis_index("x"), lax.axis_index("y"), lax.axis_index("c")
    shard = 2 * ax + ay

    wb = {n: w[n].astype(BF16) for n in BIG}
    wl, hooks = {"ffn_w_gate": [None, None], "ffn_w_up": [None, None], "ffn_w_down": [None, None]}, {}

    def gathered(keys, stacks):
        for (n, l), st in zip(keys, stacks):
            if n == "ssm_w_in":
                wl[n] = jnp.concatenate([st[i] for i in range(N_SHARDS)]
                                        + [jnp.zeros((st.shape[1], SSM_IN_PAD - SSM_IN), BF16)], axis=1)
            elif n in ("ffn_w_gate", "ffn_w_up"):
                wl[n][l] = st
            elif n == "ffn_w_down":
                wl[n][l] = st.reshape(st.shape[0] * st.shape[1], st.shape[2])
            elif n == "attn_w_qkv":
                wl[n] = st
            else:
                wl[n] = st.reshape(st.shape[0] * st.shape[1], st.shape[2])

    def behind(name, hook):
        hooks[name] = _both(hooks[name], hook) if name in hooks else hook

    def gather_behind(keys, first_leg, second_leg):
        blocks, layers, got = [wb[n] for n, _ in keys], [l for _, l in keys], {}
        behind(first_leg, _Hook(lambda: _gather_over_ici(blocks, layers), lambda res: got.update(stacks=res)))
        behind(second_leg, _Hook(lambda: _gather_over_d2d(got["stacks"], blocks, layers),
                                 lambda res: gathered(keys, res)))

    gathered(GROUPS["attn"], _gather_shards([wb[n] for n, _ in GROUPS["attn"]],
                                            [(e, l) for e, (_, l) in enumerate(GROUPS["attn"])]))
    gather_behind(GROUPS["ffn0"], "attn_fwd", "attn_out")
    gather_behind(GROUPS["ssm"][:1], "ffn_gate_0", "ffn_up_0")
    gather_behind(GROUPS["ssm"][1:], "ffn_up_0", "ffn_down_0")
    gather_behind(GROUPS["ffn1"], "ssm_in", "ssm_out")
    small_shapes = [w[n].shape for n in SMALL_SHARDED]
    small_all = _all_gather8(_pack_rows([w[n] for n in SMALL_SHARDED]), "gather_small_params")
    small_all = small_all.reshape(N_DEV, -1)[::2]
    full, off = {}, 0
    for n, shp in zip(SMALL_SHARDED, small_shapes):
        size = math.prod(shp)
        seg = small_all[:, off:off + size].reshape((N_SHARDS,) + shp)
        full[n] = jnp.moveaxis(seg, 0, -2).reshape(shp[:-1] + (N_SHARDS * shp[-1],))
        off += size
    wl.update({
        "mixer_norm": mixer_norm, "ffn_norm": ffn_norm,
        "attn_q_norm": attn_q_norm[0], "attn_k_norm": attn_k_norm[0], "attn_sinks": attn_sinks[0],
        "ssm_conv_w": full["ssm_conv_w"][0], "ssm_conv_b": full["ssm_conv_b"][0],
        "ssm_dt_bias": ssm_dt_bias[0], "ssm_a_log": ssm_a_log[0], "ssm_d": ssm_d[0],
        "ssm_norm": full["ssm_norm"][0],
    })

    c_idx = ac.reshape(1).astype(jnp.int32)
    sc_idx = jnp.stack([shard, ac]).astype(jnp.int32)
    mats, halves = {}, {n: None for n in BIG}

    def pair_sums(keys, recv):
        return [_add_pair(mats[k], r, c_idx, f"grads_add_pair_{k[0]}_{k[1]}") for k, r in zip(keys, recv)]

    def owner_sums(keys, partials, recv):
        for (n, l), p, r in zip(keys, partials, recv):
            halves[n] = _add_owned(p, r, sc_idx, l, w[n].shape, halves[n], f"grads_add_owned_{n}_{l}")

    def reduce_behind(keys, first_leg, second_leg):
        got = {}
        behind(first_leg, _Hook(lambda: _grads_to_sibling([mats[k] for k in keys]),
                                lambda res: got.update(partials=pair_sums(keys, res))))
        behind(second_leg, _Hook(lambda: _grads_to_owners(got["partials"]),
                                 lambda res: owner_sums(keys, got["partials"], res)))

    reduce_behind(GROUPS["ffn1"], "ssm_out_dx", "ssm_in_dw")
    reduce_behind(GROUPS["ssm"], "ssm_in_dx", "ffn_down_dx_0")
    reduce_behind(GROUPS["ffn0"], "attn_out_dx", "attn_bwd")
    loss_part, dx, g_full = _local_step(x[0], positions[0], loss_target[0], wl, hooks, mats)
    keys = GROUPS["attn"]
    partials = pair_sums(keys, _exchange(_grads_to_sibling([mats[k] for k in keys]), "grads_to_sibling"))
    owner_sums(keys, partials, _exchange(_grads_to_owners(partials), "grads_to_owners"))
    grads = dict(zip(BIG, _share_halves([halves[n] for n in BIG], MATRIX_LAYERS)))

    small_full_shapes = [g_full[n].shape for n in SMALL] + [(1,)]
    small_g = _pack_rows([g_full[n] for n in SMALL] + [loss_part.reshape(1)])
    small_sum = _sum8(_all_gather8(small_g, "gather_small_grads")).reshape(-1)
    *small_list, loss = _unpack(small_sum, small_full_shapes)
    for n, g in zip(SMALL, small_list):
        if n in SMALL_SHARDED:
            width = w[n].shape[-1]
            g = lax.dynamic_slice_in_dim(g, shard * width, width, axis=g.ndim - 1)
        grads[n] = g.reshape(w[n].shape)

    delta, new_m, new_v = {}, {}, {}
    for n in BIG:
        delta[n], new_m[n], new_v[n] = _adamw(w[n], grads[n], m[n], v[n], "adamw_" + n)
    small_local = [w[n].shape for n in SMALL]
    pk = lambda t: _pack_rows([t[n] for n in SMALL])[None]
    outs = _adamw(pk(w), pk(grads), pk(m), pk(v), "adamw_small")
    for res, o in zip((delta, new_m, new_v), outs):
        for n, a in zip(SMALL, _unpack(o.reshape(-1), small_local)):
            res[n] = a

    return (loss.reshape(()), dx[None], *[grads[n] for n in WEIGHTS], *[delta[n] for n in WEIGHTS],
            *[new_m[n] for n in WEIGHTS], *[new_v[n] for n in WEIGHTS])
```

```python
import math

import jax
import jax.numpy as jnp
from jax import lax
from jax.experimental import pallas as pl
from jax.experimental.pallas import tpu as pltpu

F32 = jnp.float32
BF16 = jnp.bfloat16

D_MODEL = 2048
EPS = 1e-6
ATT_HEAD_DIM = 64
ATT_Q_HEADS = 32
ATT_KV_HEADS = 4
ATT_GROUP = 8
ATT_BLOCK = 128
ROPE_THETA = 10000.0
Q_WIDTH = ATT_Q_HEADS * ATT_HEAD_DIM
KV_WIDTH = ATT_KV_HEADS * ATT_HEAD_DIM
SSM_D_INNER = 4096
SSM_HEADS = 64
SSM_GROUPS = 8
SSM_HPG = 8
SSM_P = 64
SSM_STATE = 128
SSM_CONV = 4
SSM_CHUNK = 256
SSM_CONV_DIM = 6144
SSM_GN = SSM_D_INNER // SSM_GROUPS
SSM_IN = SSM_D_INNER + SSM_CONV_DIM + SSM_HEADS
LANES = 128
SSM_IN_PAD = -(-SSM_IN // LANES) * LANES
N_SHARDS = 4
N_DEV = 8

ADAM_LR = 0.001
ADAM_B1 = 0.9
ADAM_B2 = 0.999
ADAM_EPS = 1e-08
ADAM_WD = 0.01
ADAM_STEP = 10

VMEM_LIMIT = 56 * 1024 * 1024
MESH = pl.DeviceIdType.MESH
ANY = pl.BlockSpec(memory_space=pl.ANY)


def _params(*sem):
    return pltpu.CompilerParams(dimension_semantics=sem, vmem_limit_bytes=VMEM_LIMIT)


def _sems(n):
    return pltpu.SemaphoreType.DMA((n,))


def _call(body, carry, name, grid, in_specs, out_specs, out_shape, scratch_shapes, sem, args):
    if carry is None:
        return pl.pallas_call(body, name=name, grid=grid, in_specs=in_specs, out_specs=out_specs,
                              out_shape=out_shape, scratch_shapes=scratch_shapes,
                              compiler_params=_params(*sem))(*args)
    n_in, n_out, n_scr = len(in_specs), len(out_specs), len(scratch_shapes)
    c_arrays, c_shapes = list(carry["arrays"]), list(carry["out_shapes"])
    n_cin, n_cout = len(c_arrays), len(c_shapes)

    def carrying(*refs):
        ins, refs = refs[:n_in], refs[n_in:]
        cin, refs = refs[:n_cin], refs[n_cin:]
        outs, refs = refs[:n_out], refs[n_out:]
        cout, refs = refs[:n_cout], refs[n_cout:]
        scratch, (send_sems, recv_sems) = refs[:n_scr], refs[n_scr:]
        copies = carry["build"](cin, cout, send_sems, recv_sems)
        ids = [pl.program_id(d) for d in range(len(grid))]
        first, last = ids[0] == 0, ids[0] == grid[0] - 1
        for d in range(1, len(grid)):
            first = jnp.logical_and(first, ids[d] == 0)
            last = jnp.logical_and(last, ids[d] == grid[d] - 1)

        @pl.when(first)
        def _():
            for cp in copies:
                cp.start()

        body(*ins, *outs, *scratch)

        @pl.when(last)
        def _():
            for cp in copies:
                cp.wait()

    aliases = {n_in + i: n_out + o for i, o in carry.get("aliases", {}).items()}
    return pl.pallas_call(
        carrying, name=name, grid=grid, in_specs=list(in_specs) + [ANY] * n_cin,
        out_specs=list(out_specs) + [ANY] * n_cout, out_shape=list(out_shape) + c_shapes,
        scratch_shapes=list(scratch_shapes) + [_sems(carry["n_sems"]), _sems(carry["n_sems"])],
        input_output_aliases=aliases, compiler_params=_params(*(["arbitrary"] * len(grid))))(*args, *c_arrays)


def _tile(dim, target, unit=LANES):
    if dim <= target:
        return dim
    t = (target // unit) * unit
    while t >= unit:
        if dim % t == 0:
            return t
        t -= unit
    return dim


def _dot(a, b):
    return lax.dot_general(a, b, (((1,), (0,)), ((), ())), preferred_element_type=F32)


def _dot_nt(a, b):
    return lax.dot_general(a, b, (((1,), (1,)), ((), ())), preferred_element_type=F32)


def _dot_tn(a, b):
    return lax.dot_general(a, b, (((0,), (0,)), ((), ())), preferred_element_type=F32)


def _split3(x):
    hi = x.astype(BF16)
    r1 = x - hi.astype(F32)
    mid = r1.astype(BF16)
    lo = (r1 - mid.astype(F32)).astype(BF16)
    return hi, mid, lo


def _dot_x(x, m):
    hi, mid, lo = _split3(x)
    return _dot(hi, m) + _dot(mid, m) + _dot(lo, m)


def _dot_x2(x, m):
    hi = x.astype(BF16)
    return _dot(hi, m) + _dot((x - hi.astype(F32)).astype(BF16), m)


def _xdot(m, x):
    hi, mid, lo = _split3(x)
    return _dot(m, hi) + _dot(m, mid) + _dot(m, lo)


def _dot_x_nt(x, m):
    hi, mid, lo = _split3(x)
    return _dot_nt(hi, m) + _dot_nt(mid, m) + _dot_nt(lo, m)


def _iota(shape, dim):
    return lax.broadcasted_iota(jnp.int32, shape, dim)


def _sigmoid(x):
    return 1.0 / (1.0 + jnp.exp(-x))


def _softplus(x):
    return jnp.maximum(x, 0.0) + jnp.log(1.0 + jnp.exp(-jnp.abs(x)))


MM_ROWS = 1024
MM_TILE = 1408
MM_DEPTH = 3456
FUSED_ROWS = 512


def _mm(a, b, mode, name, add=None, out_dtype=F32, b_cols=False, out_cols=False, fuse=None, rows=MM_ROWS,
        carry=None):
    bs = b.shape[-2:]
    if b_cols:
        bs = (bs[0], N_SHARDS * bs[1])
    if mode == "nn":
        (m, k), (k2, n) = a.shape, bs
    elif mode == "nt":
        (m, k), (n, k2) = a.shape, bs
    else:
        (k, m), (k2, n) = a.shape, bs
    assert k == k2, (a.shape, b.shape, mode)
    split_n = (b_cols and mode == "nn") or out_cols
    split_k = b_cols and mode == "nt"
    tm = _tile(m, MM_TILE if mode == "tn" else rows)
    tn = _tile(n // N_SHARDS if split_n else n, MM_TILE)
    tk = _tile(k // N_SHARDS if split_k else k, MM_DEPTH)
    nk = k // tk
    nj, nq = (n // N_SHARDS) // tn, (k // N_SHARDS) // tk
    if mode == "tn":
        a_spec = pl.BlockSpec((tk, tm), lambda i, j, q: (q, i))
    else:
        a_spec = pl.BlockSpec((tm, tk), lambda i, j, q: (i, q))
    if mode == "nt":
        if b_cols:
            b_spec = pl.BlockSpec((None, tn, tk), lambda i, j, q: (q // nq, j, q % nq))
        else:
            b_spec = pl.BlockSpec((tn, tk), lambda i, j, q: (j, q))
    elif b_cols:
        b_spec = pl.BlockSpec((None, tk, tn), lambda i, j, q: (j // nj, q, j % nj))
    else:
        b_spec = pl.BlockSpec((tk, tn), lambda i, j, q: (q, j))
    add_spec = pl.BlockSpec((tm, tn), lambda i, j, q: (i, j))
    if out_cols:
        o_spec = pl.BlockSpec((None, tm, tn), lambda i, j, q: (j // nj, i, j % nj))
        o_shape = (N_SHARDS, m, n // N_SHARDS)
    else:
        o_spec, o_shape = add_spec, (m, n)
    dot = {"nn": _dot, "nt": _dot_nt, "tn": _dot_tn}[mode]
    has_add = add is not None
    fuse_fn, extra, out_dtypes = fuse if fuse is not None else (None, [], [out_dtype])
    n_in, n_out = 2 + has_add + len(extra), len(out_dtypes)

    def body(*refs):
        a_ref, b_ref = refs[:2]
        add_ref = refs[2] if has_add else None
        extra_refs = refs[2 + has_add:n_in]
        o_refs, acc_ref = refs[n_in:n_in + n_out], refs[n_in + n_out]
        part = dot(a_ref[...].astype(BF16), b_ref[...].astype(BF16))

        def finish(total):
            if has_add:
                total = total + add_ref[...].astype(F32)
            outs = (total,) if fuse_fn is None else fuse_fn(total, *[r[...] for r in extra_refs])
            for o_ref, val in zip(o_refs, outs):
                o_ref[...] = val.astype(o_ref.dtype)

        if nk == 1:
            finish(part)
        else:
            q = pl.program_id(2)

            @pl.when(q == 0)
            def _():
                acc_ref[...] = part

            @pl.when(jnp.logical_and(q > 0, q < nk - 1))
            def _():
                acc_ref[...] += part

            @pl.when(q == nk - 1)
            def _():
                finish(acc_ref[...] + part)

    in_specs = [a_spec, b_spec] + [add_spec] * (has_add + len(extra))
    args = (a, b) + ((add,) if has_add else ()) + tuple(extra)
    res = _call(body, carry, name, (m // tm, n // tn, nk), in_specs, [o_spec] * n_out,
                [jax.ShapeDtypeStruct(o_shape, dt) for dt in out_dtypes],
                [pltpu.VMEM((tm, tn) if nk > 1 else (8, LANES), F32)], ("parallel", "parallel", "arbitrary"), args)
    main = res[0] if fuse is None else res[:n_out]
    return main if carry is None else (main, res[n_out:])


def _rms_fwd(x, g, name):
    s, d = x.shape
    ts = _tile(s, 512, 8)

    def body(x_ref, g_ref, o_ref):
        xv = x_ref[...]
        r = lax.rsqrt(jnp.mean(xv * xv, axis=-1, keepdims=True) + EPS)
        o_ref[...] = (xv * r * g_ref[...]).astype(BF16)

    return pl.pallas_call(
        body, name=name, grid=(s // ts,),
        in_specs=[pl.BlockSpec((ts, d), lambda i: (i, 0)), pl.BlockSpec((1, d), lambda i: (0, 0))],
        out_specs=pl.BlockSpec((ts, d), lambda i: (i, 0)),
        out_shape=jax.ShapeDtypeStruct((s, d), BF16),
        compiler_params=_params("parallel"),
    )(x, g)


def _rms_bwd(x, g, dh, dres, name):
    s, d = x.shape
    ts = _tile(s, 512, 8)

    def body(x_ref, g_ref, dh_ref, dres_ref, dx_ref, dg_ref):
        xv = x_ref[...]
        r = lax.rsqrt(jnp.mean(xv * xv, axis=-1, keepdims=True) + EPS)
        xhat = xv * r
        dhv = dh_ref[...].astype(F32)
        part = jnp.sum(dhv * xhat, axis=0, keepdims=True)

        @pl.when(pl.program_id(0) == 0)
        def _():
            dg_ref[...] = part

        @pl.when(pl.program_id(0) > 0)
        def _():
            dg_ref[...] += part

        dxh = dhv * g_ref[...]
        dx = r * (dxh - xhat * jnp.mean(dxh * xhat, axis=-1, keepdims=True))
        dx_ref[...] = dres_ref[...] + dx

    row = pl.BlockSpec((ts, d), lambda i: (i, 0))
    vec = pl.BlockSpec((1, d), lambda i: (0, 0))
    return pl.pallas_call(
        body, name=name, grid=(s // ts,),
        in_specs=[row, vec, row, row], out_specs=[row, vec],
        out_shape=[jax.ShapeDtypeStruct((s, d), F32), jax.ShapeDtypeStruct((1, d), F32)],
        compiler_params=_params("arbitrary"),
    )(x, g, dh, dres)


def _act_fwd(g, u, name):
    s, f = g.shape
    ts, tf = _tile(s, 512, 8), _tile(f, 1408)

    def body(g_ref, u_ref, o_ref):
        gv = g_ref[...]
        o_ref[...] = (gv * _sigmoid(gv) * u_ref[...]).astype(BF16)

    blk = pl.BlockSpec((ts, tf), lambda i, j: (i, j))
    return pl.pallas_call(
        body, name=name, grid=(s // ts, f // tf), in_specs=[blk, blk], out_specs=blk,
        out_shape=jax.ShapeDtypeStruct((s, f), BF16), compiler_params=_params("parallel", "parallel"),
    )(g, u)


def _act_bwd(g, u, da, name):
    s, f = g.shape
    ts, tf = _tile(s, 512, 8), _tile(f, 1408)

    def body(g_ref, u_ref, da_ref, dg_ref, du_ref):
        gv, uv, dav = g_ref[...], u_ref[...], da_ref[...].astype(F32)
        sg = _sigmoid(gv)
        silu = gv * sg
        du_ref[...] = (dav * silu).astype(BF16)
        dg_ref[...] = (dav * uv * sg * (1.0 + gv * (1.0 - sg))).astype(BF16)

    blk = pl.BlockSpec((ts, tf), lambda i, j: (i, j))
    return pl.pallas_call(
        body, name=name, grid=(s // ts, f // tf), in_specs=[blk, blk, blk], out_specs=[blk, blk],
        out_shape=[jax.ShapeDtypeStruct((s, f), BF16)] * 2, compiler_params=_params("parallel", "parallel"),
    )(g, u, da)


def _loss_fwd_bwd(y, target):
    s, d = y.shape
    ts = _tile(s, 512, 8)

    def body(y_ref, t_ref, l_ref, dy_ref):
        diff = y_ref[...] - t_ref[...]
        dy_ref[...] = diff * (1.0 / d)
        part = jnp.full((1, LANES), 0.5 * jnp.sum(jnp.mean(diff * diff, axis=-1, keepdims=True)), F32)

        @pl.when(pl.program_id(0) == 0)
        def _():
            l_ref[...] = part

        @pl.when(pl.program_id(0) > 0)
        def _():
            l_ref[...] += part

    row = pl.BlockSpec((ts, d), lambda i: (i, 0))
    acc = pl.BlockSpec((1, LANES), lambda i: (0, 0))
    return pl.pallas_call(
        body, name="loss", grid=(s // ts,), in_specs=[row, row], out_specs=[acc, row],
        out_shape=[jax.ShapeDtypeStruct((1, LANES), F32), jax.ShapeDtypeStruct((s, d), F32)],
        compiler_params=_params("arbitrary"),
    )(y, target)


def _lane_consts():
    r, c = _iota((LANES, LANES), 0), _iota((LANES, LANES), 1)
    same = (r >> 6) == (c >> 6)
    rin, cin = r & 63, c & 63
    one = lambda cond: jnp.where(cond, 1.0, 0.0).astype(BF16)
    return dict(
        seg=one(same),
        rot=(jnp.where(same & (rin == cin + 32), -1.0, 0.0)
             + jnp.where(same & (cin == rin + 32), 1.0, 0.0)).astype(BF16),
        dup_lo=one(r == cin), dup_hi=one(r == cin + 64),
        up=one((c >= 64) & (r == c - 64)), down=one((c < 64) & (r == c + 64)),
        fold_lo=one((c < 64) & (rin == c)), fold_hi=one((c >= 64) & (rin == c - 64)),
    )


def _norm_rope(xc, gain, cos, sin, k):
    ss = _dot_x2(xc * xc, k["seg"])
    rinv = lax.rsqrt(ss * (1.0 / ATT_HEAD_DIM) + EPS)
    xhat = xc * rinv
    y = xhat * gain
    return y * cos + _dot_x2(y, k["rot"]) * sin, xhat, rinv


def _norm_rope_bwd(dr, xhat, rinv, gain, cos, sin, k):
    dy = dr * cos - _dot_x2(dr * sin, k["rot"])
    dgain = jnp.sum(dy * xhat, axis=0, keepdims=True)
    dxh = dy * gain
    dx = rinv * (dxh - xhat * (_dot_x2(dxh * xhat, k["seg"]) * (1.0 / ATT_HEAD_DIM)))
    return dx, dgain


def _attn_prep(qkv, cos, sin, gq, gk):
    s = qkv.shape[0]
    tr = _tile(s, 256, 8)

    def body(x_ref, cos_ref, sin_ref, gq_ref, gk_ref, q_ref, kk_ref, vlo_ref, vhi_ref):
        k = _lane_consts()
        cosv, sinv = cos_ref[...], sin_ref[...]
        lane = _iota((tr, LANES), 1)
        for j in range(Q_WIDTH // LANES):
            r, _, _ = _norm_rope(x_ref[:, j * LANES:(j + 1) * LANES], gq_ref[...], cosv, sinv, k)
            q_ref[:, j * LANES:(j + 1) * LANES] = r.astype(BF16)
        for i in range(KV_WIDTH // LANES):
            off = Q_WIDTH + i * LANES
            r, _, _ = _norm_rope(x_ref[:, off:off + LANES], gk_ref[...], cosv, sinv, k)
            rb = r.astype(BF16)
            kk_ref[:, (2 * i) * LANES:(2 * i + 1) * LANES] = _dot(rb, k["dup_lo"]).astype(BF16)
            kk_ref[:, (2 * i + 1) * LANES:(2 * i + 2) * LANES] = _dot(rb, k["dup_hi"]).astype(BF16)
            off = Q_WIDTH + KV_WIDTH + i * LANES
            vb = x_ref[:, off:off + LANES].astype(BF16)
            zero = jnp.zeros_like(vb)
            vlo_ref[:, (2 * i) * LANES:(2 * i + 1) * LANES] = jnp.where(lane < 64, vb, zero)
            vhi_ref[:, (2 * i) * LANES:(2 * i + 1) * LANES] = _dot(vb, k["up"]).astype(BF16)
            vlo_ref[:, (2 * i + 1) * LANES:(2 * i + 2) * LANES] = _dot(vb, k["down"]).astype(BF16)
            vhi_ref[:, (2 * i + 1) * LANES:(2 * i + 2) * LANES] = jnp.where(lane >= 64, vb, zero)

    w = qkv.shape[1]
    row = lambda width: pl.BlockSpec((tr, width), lambda i: (i, 0))
    vec = pl.BlockSpec((1, LANES), lambda i: (0, 0))
    kw = ATT_KV_HEADS * LANES
    return pl.pallas_call(
        body, name="attn_prep", grid=(s // tr,),
        in_specs=[row(w), row(LANES), row(LANES), vec, vec],
        out_specs=[row(Q_WIDTH), row(kw), row(kw), row(kw)],
        out_shape=[jax.ShapeDtypeStruct((s, Q_WIDTH), BF16)] + [jax.ShapeDtypeStruct((s, kw), BF16)] * 3,
        compiler_params=_params("parallel"),
    )(qkv, cos, sin, gq, gk)


def _band_mask(n):
    qi = _iota((ATT_BLOCK, 2 * ATT_BLOCK), 0)
    kj = _iota((ATT_BLOCK, 2 * ATT_BLOCK), 1)
    band = (kj > qi) & (kj <= qi + ATT_BLOCK)
    return band & ((kj >= ATT_BLOCK) | (n > 0))


def _softmax_sink(s, valid, sink):
    s = jnp.where(valid, s, -jnp.inf)
    m = jnp.maximum(jnp.max(s, axis=-1, keepdims=True), sink)
    p = jnp.exp(s - m)
    esink = jnp.exp(sink - m)
    inv = 1.0 / (jnp.sum(p, axis=-1, keepdims=True) + esink)
    return p * inv, esink * inv


def _attn_specs(order):
    if order == "nh":
        cur = lambda n, h: (n, h)
        prev = lambda n, h: (jnp.maximum(n - 1, 0), h)
    else:
        cur = lambda h, n: (n, h)
        prev = lambda h, n: (jnp.maximum(n - 1, 0), h)
    qs = pl.BlockSpec((ATT_BLOCK, ATT_GROUP * ATT_HEAD_DIM), cur)
    kc = pl.BlockSpec((ATT_BLOCK, LANES), cur)
    kp = pl.BlockSpec((ATT_BLOCK, LANES), prev)
    return qs, kc, kp


def _pair_rows(qp):
    lane = _iota((ATT_BLOCK, LANES), 1)
    zero = jnp.zeros_like(qp)
    return jnp.concatenate([jnp.where(lane < 64, qp, zero), jnp.where(lane >= 64, qp, zero)], axis=0)


def _pair_masks(n):
    qi = _iota((2 * ATT_BLOCK, 2 * ATT_BLOCK), 0) & (ATT_BLOCK - 1)
    kj = _iota((2 * ATT_BLOCK, 2 * ATT_BLOCK), 1)
    valid = (kj > qi) & (kj <= qi + ATT_BLOCK) & ((kj >= ATT_BLOCK) | (n > 0))
    return valid, _iota((2 * ATT_BLOCK, 1), 0) >= ATT_BLOCK


def _attn_fwd(q, kk, vlo, vhi, sinks, name="attn_fwd", carry=None):
    s = q.shape[0]
    nb = s // ATT_BLOCK
    scale = ATT_HEAD_DIM ** -0.5

    def body(sink_ref, q_ref, kc_ref, kp_ref, vloc_ref, vlop_ref, vhic_ref, vhip_ref, o_ref):
        n, h = pl.program_id(0), pl.program_id(1)
        valid, upper = _pair_masks(n)
        kw = jnp.concatenate([kp_ref[...], kc_ref[...]], axis=0)
        vcat = jnp.concatenate([vlop_ref[...], vloc_ref[...], vhip_ref[...], vhic_ref[...]], axis=0)
        for jp in range(ATT_GROUP // 2):
            q2 = _pair_rows(q_ref[:, jp * LANES:(jp + 1) * LANES])
            sink = jnp.where(upper, sink_ref[h * ATT_GROUP + 2 * jp + 1], sink_ref[h * ATT_GROUP + 2 * jp])
            probs, _ = _softmax_sink(_dot_nt(q2, kw) * scale, valid, sink)
            pcat = jnp.concatenate([probs[:ATT_BLOCK], probs[ATT_BLOCK:]], axis=1).astype(BF16)
            o_ref[:, jp * LANES:(jp + 1) * LANES] = _dot(pcat, vcat).astype(BF16)

    qs, kc, kp = _attn_specs("nh")
    res = _call(body, carry, name, (nb, ATT_KV_HEADS),
                [pl.BlockSpec(memory_space=pltpu.SMEM), qs, kc, kp, kc, kp, kc, kp], [qs],
                [jax.ShapeDtypeStruct((s, Q_WIDTH), BF16)], [], ("parallel", "parallel"),
                (sinks, q, kk, kk, vlo, vlo, vhi, vhi))
    return res[0] if carry is None else (res[0], res[1:])


def _attn_bwd(q, kk, vlo, vhi, sinks, do, name="attn_bwd", carry=None):
    s = q.shape[0]
    nb = s // ATT_BLOCK
    scale = ATT_HEAD_DIM ** -0.5

    def body(sink_ref, q_ref, kc_ref, kp_ref, vloc_ref, vlop_ref, vhic_ref, vhip_ref, do_ref,
             dq_ref, dkc_ref, dkp_ref, dvloc_ref, dvlop_ref, dvhic_ref, dvhip_ref, dsink_ref, dkk_acc, dv_acc):
        h, n = pl.program_id(0), pl.program_id(1)
        valid, upper = _pair_masks(n)
        kw = jnp.concatenate([kp_ref[...], kc_ref[...]], axis=0)
        vcat = jnp.concatenate([vlop_ref[...], vloc_ref[...], vhip_ref[...], vhic_ref[...]], axis=0)
        lane = _iota((ATT_BLOCK, LANES), 1)
        sub = _iota((ATT_GROUP, LANES), 0)
        dsink = jnp.zeros((ATT_GROUP, LANES), F32)
        for jp in range(ATT_GROUP // 2):
            q2 = _pair_rows(q_ref[:, jp * LANES:(jp + 1) * LANES])
            dop = do_ref[:, jp * LANES:(jp + 1) * LANES]
            sink = jnp.where(upper, sink_ref[h * ATT_GROUP + 2 * jp + 1], sink_ref[h * ATT_GROUP + 2 * jp])
            probs, psink = _softmax_sink(_dot_nt(q2, kw) * scale, valid, sink)
            pcat = jnp.concatenate([probs[:ATT_BLOCK], probs[ATT_BLOCK:]], axis=1).astype(BF16)
            dpc = _dot_nt(dop, vcat)
            dprobs = jnp.concatenate([dpc[:, :2 * ATT_BLOCK], dpc[:, 2 * ATT_BLOCK:]], axis=0)
            dv_part = _dot_tn(pcat, dop)
            delta = jnp.sum(probs * dprobs, axis=-1, keepdims=True)
            ds = (probs * (dprobs - delta) * scale).astype(BF16)
            sd = psink * delta
            dsink = (dsink + jnp.where(sub == 2 * jp, -jnp.sum(sd[:ATT_BLOCK]), 0.0)
                     + jnp.where(sub == 2 * jp + 1, -jnp.sum(sd[ATT_BLOCK:]), 0.0))
            dq2 = _dot(ds, kw)
            dq_ref[:, jp * LANES:(jp + 1) * LANES] = jnp.where(lane < 64, dq2[:ATT_BLOCK], dq2[ATT_BLOCK:])
            dkk_part = _dot_tn(ds, q2)
            if jp == 0:
                dkk_acc[...], dv_acc[...] = dkk_part, dv_part
            else:
                dkk_acc[...] += dkk_part
                dv_acc[...] += dv_part
        blk = ATT_BLOCK
        dkp_ref[...], dkc_ref[...] = dkk_acc[:blk], dkk_acc[blk:]
        dvlop_ref[...], dvloc_ref[...] = dv_acc[:blk], dv_acc[blk:2 * blk]
        dvhip_ref[...], dvhic_ref[...] = dv_acc[2 * blk:3 * blk], dv_acc[3 * blk:]

        @pl.when(n == 0)
        def _():
            dsink_ref[0] = dsink

        @pl.when(n > 0)
        def _():
            dsink_ref[0] += dsink

    qs, kc, kp = _attn_specs("hn")
    kw_shape = jax.ShapeDtypeStruct((s, ATT_KV_HEADS * LANES), F32)
    res = _call(body, carry, name, (ATT_KV_HEADS, nb),
                [pl.BlockSpec(memory_space=pltpu.SMEM), qs, kc, kp, kc, kp, kc, kp, qs],
                [qs] + [kc] * 6 + [pl.BlockSpec((1, ATT_GROUP, LANES), lambda h, n: (h, 0, 0))],
                [jax.ShapeDtypeStruct((s, Q_WIDTH), F32)] + [kw_shape] * 6
                + [jax.ShapeDtypeStruct((ATT_KV_HEADS, ATT_GROUP, LANES), F32)],
                [pltpu.VMEM((2 * ATT_BLOCK, LANES), F32), pltpu.VMEM((4 * ATT_BLOCK, LANES), F32)],
                ("parallel", "arbitrary"), (sinks, q, kk, kk, vlo, vlo, vhi, vhi, do))
    return res if carry is None else (res[:8], res[8:])


def _attn_prep_bwd(qkv, cos, sin, gq, gk, dq, dks, dvlos, dvhis):
    s, w = qkv.shape
    tr = ATT_BLOCK
    nb = s // tr

    def body(x_ref, cos_ref, sin_ref, gq_ref, gk_ref, dq_ref, dkc_ref, dkn_ref, dvloc_ref, dvlon_ref,
             dvhic_ref, dvhin_ref, dx_ref, dgq_ref, dgk_ref):
        n = pl.program_id(0)
        k = _lane_consts()
        cosv, sinv = cos_ref[...], sin_ref[...]
        nxt = jnp.where(n < nb - 1, 1.0, 0.0)
        lane = _iota((tr, LANES), 1)
        dgq = jnp.zeros((1, LANES), F32)
        dgk = jnp.zeros((1, LANES), F32)
        for j in range(Q_WIDTH // LANES):
            sl = slice(j * LANES, (j + 1) * LANES)
            _, xhat, rinv = _norm_rope(x_ref[:, sl], gq_ref[...], cosv, sinv, k)
            dx, dg = _norm_rope_bwd(dq_ref[:, sl], xhat, rinv, gq_ref[...], cosv, sinv, k)
            dx_ref[:, sl] = dx.astype(BF16)
            dgq = dgq + dg
        for i in range(KV_WIDTH // LANES):
            a, b = slice(2 * i * LANES, (2 * i + 1) * LANES), slice((2 * i + 1) * LANES, (2 * i + 2) * LANES)
            dr = (_dot_x2(dkc_ref[:, a] + nxt * dkn_ref[:, a], k["fold_lo"])
                  + _dot_x2(dkc_ref[:, b] + nxt * dkn_ref[:, b], k["fold_hi"]))
            sl = slice(Q_WIDTH + i * LANES, Q_WIDTH + (i + 1) * LANES)
            _, xhat, rinv = _norm_rope(x_ref[:, sl], gk_ref[...], cosv, sinv, k)
            dx, dg = _norm_rope_bwd(dr, xhat, rinv, gk_ref[...], cosv, sinv, k)
            dx_ref[:, sl] = dx.astype(BF16)
            dgk = dgk + dg
            ta = jnp.where(lane < 64, dvloc_ref[:, a] + nxt * dvlon_ref[:, a], dvhic_ref[:, a] + nxt * dvhin_ref[:, a])
            tb = jnp.where(lane < 64, dvloc_ref[:, b] + nxt * dvlon_ref[:, b], dvhic_ref[:, b] + nxt * dvhin_ref[:, b])
            sl = slice(Q_WIDTH + KV_WIDTH + i * LANES, Q_WIDTH + KV_WIDTH + (i + 1) * LANES)
            dx_ref[:, sl] = (_dot_x2(ta, k["fold_lo"]) + _dot_x2(tb, k["fold_hi"])).astype(BF16)

        @pl.when(n == 0)
        def _():
            dgq_ref[...] = dgq
            dgk_ref[...] = dgk

        @pl.when(n > 0)
        def _():
            dgq_ref[...] += dgq
            dgk_ref[...] += dgk

    row = lambda width: pl.BlockSpec((tr, width), lambda i: (i, 0))
    nxt_row = pl.BlockSpec((tr, ATT_KV_HEADS * LANES), lambda i: (jnp.minimum(i + 1, nb - 1), 0))
    vec = pl.BlockSpec((1, LANES), lambda i: (0, 0))
    kw = ATT_KV_HEADS * LANES
    return pl.pallas_call(
        body, name="attn_prep_bwd", grid=(nb,),
        in_specs=[row(w), row(LANES), row(LANES), vec, vec, row(Q_WIDTH),
                  row(kw), nxt_row, row(kw), nxt_row, row(kw), nxt_row],
        out_specs=[row(w), vec, vec],
        out_shape=[jax.ShapeDtypeStruct((s, w), BF16), jax.ShapeDtypeStruct((1, LANES), F32),
                   jax.ShapeDtypeStruct((1, LANES), F32)],
        compiler_params=_params("arbitrary"),
    )(qkv, cos, sin, gq, gk, dq, dks[0], dks[1], dvlos[0], dvlos[1], dvhis[0], dvhis[1])


CONV_HALO = 8
CONV_TC = 512
XBC_OFF = SSM_D_INNER // CONV_TC
DT_OFF = SSM_D_INNER + SSM_CONV_DIM


def _conv_pre(ext, w_ref, b_ref, ts):
    pre = b_ref[...] + w_ref[SSM_CONV - 1:SSM_CONV, :] * ext[CONV_HALO:]
    for kk in range(SSM_CONV - 1):
        pre = pre + w_ref[kk:kk + 1, :] * pltpu.roll(ext, SSM_CONV - 1 - kk, 0)[CONV_HALO:]
    return pre


def _conv_specs(ts):
    tc = CONV_TC
    src = pl.BlockSpec((ts, tc), lambda j, i: (i, XBC_OFF + j))
    halo = pl.BlockSpec((CONV_HALO, tc), lambda j, i: (jnp.maximum(i * (ts // CONV_HALO) - 1, 0), XBC_OFF + j))
    blk = pl.BlockSpec((ts, tc), lambda j, i: (i, j))
    wspec = pl.BlockSpec((SSM_CONV, tc), lambda j, i: (0, j))
    bspec = pl.BlockSpec((1, tc), lambda j, i: (0, j))
    return src, halo, blk, wspec, bspec


def _conv_fwd(zx, w, b):
    s, c = zx.shape[0], SSM_CONV_DIM
    ts = _tile(s, 512, 8)

    def body(u_ref, halo_ref, w_ref, b_ref, o_ref):
        halo = jnp.where(pl.program_id(1) > 0, halo_ref[...], 0.0)
        pre = _conv_pre(jnp.concatenate([halo, u_ref[...]], axis=0), w_ref, b_ref, ts)
        o_ref[...] = pre * _sigmoid(pre)

    src, halo, blk, wspec, bspec = _conv_specs(ts)
    return pl.pallas_call(
        body, name="conv_fwd", grid=(c // CONV_TC, s // ts),
        in_specs=[src, halo, wspec, bspec], out_specs=blk, out_shape=jax.ShapeDtypeStruct((s, c), F32),
        compiler_params=_params("parallel", "parallel"),
    )(zx, zx, w, b)


def _conv_bwd_pre(zx, w, b, dxs, dbm, dcm):
    s, c = zx.shape[0], SSM_CONV_DIM
    ts = _tile(s, 512, 8)
    nx, nb = dxs.shape[1] // CONV_TC, dbm.shape[1] // CONV_TC

    def body(u_ref, halo_ref, w_ref, b_ref, dx_ref, dbm_ref, dcm_ref, dpre_ref, dw_ref, db_ref):
        j, i = pl.program_id(0), pl.program_id(1)
        halo = jnp.where(i > 0, halo_ref[...], 0.0)
        ext = jnp.concatenate([halo, u_ref[...]], axis=0)
        pre = _conv_pre(ext, w_ref, b_ref, ts)
        sg = _sigmoid(pre)
        da = jnp.where(j < nx, dx_ref[...], jnp.where(j < nx + nb, dbm_ref[...], dcm_ref[...]))
        dpre = da * sg * (1.0 + pre * (1.0 - sg))
        dpre_ref[...] = dpre
        rows = [jnp.sum(dpre * pltpu.roll(ext, SSM_CONV - 1 - kk, 0)[CONV_HALO:], axis=0, keepdims=True)
                for kk in range(SSM_CONV - 1)]
        rows.append(jnp.sum(dpre * ext[CONV_HALO:], axis=0, keepdims=True))
        dwp = jnp.concatenate(rows, axis=0)
        dbp = jnp.sum(dpre, axis=0, keepdims=True)

        @pl.when(i == 0)
        def _():
            dw_ref[...] = dwp
            db_ref[...] = dbp

        @pl.when(i > 0)
        def _():
            dw_ref[...] += dwp
            db_ref[...] += dbp

    src, halo, blk, wspec, bspec = _conv_specs(ts)

    def part(lo, n):
        return pl.BlockSpec((ts, CONV_TC), lambda j, i: (jnp.where((j >= lo) & (j < lo + n), i, 0),
                                                         jnp.clip(j - lo, 0, n - 1)))

    return pl.pallas_call(
        body, name="conv_bwd_pre", grid=(c // CONV_TC, s // ts),
        in_specs=[src, halo, wspec, bspec, part(0, nx), part(nx, nb), part(nx + nb, nb)],
        out_specs=[blk, wspec, bspec],
        out_shape=[jax.ShapeDtypeStruct((s, c), F32), jax.ShapeDtypeStruct((SSM_CONV, c), F32),
                   jax.ShapeDtypeStruct((1, c), F32)],
        compiler_params=_params("parallel", "arbitrary"),
    )(zx, zx, w, b, dxs, dbm, dcm)


def _conv_bwd_in(dpre, w, dzx):
    s, c = dpre.shape
    ts, tc = _tile(s, 512, 8), CONV_TC
    ns = s // ts

    def body(d_ref, halo_ref, w_ref, dzx_ref, o_ref):
        del dzx_ref
        halo = jnp.where(pl.program_id(1) < ns - 1, halo_ref[...], 0.0)
        ext = jnp.concatenate([d_ref[...], halo], axis=0)
        du = w_ref[SSM_CONV - 1:SSM_CONV, :] * ext[:ts]
        for kk in range(SSM_CONV - 1):
            du = du + w_ref[kk:kk + 1, :] * pltpu.roll(ext, ts + CONV_HALO - (SSM_CONV - 1 - kk), 0)[:ts]
        o_ref[...] = du.astype(BF16)

    blk = pl.BlockSpec((ts, tc), lambda j, i: (i, j))
    halo = pl.BlockSpec((CONV_HALO, tc), lambda j, i: (jnp.minimum((i + 1) * (ts // CONV_HALO), s // CONV_HALO - 1), j))
    return pl.pallas_call(
        body, name="conv_bwd_in", grid=(c // tc, ns),
        in_specs=[blk, halo, pl.BlockSpec((SSM_CONV, tc), lambda j, i: (0, j)), ANY],
        out_specs=pl.BlockSpec((ts, tc), lambda j, i: (i, XBC_OFF + j)),
        out_shape=jax.ShapeDtypeStruct(dzx.shape, BF16), input_output_aliases={3: 0},
        compiler_params=_params("parallel", "parallel"),
    )(dpre, dpre, w, dzx)


def _ssd_common(dt_ref, dtt_ref, bias_ref, biast_ref, alog_ref, alogt_ref):
    ln = SSM_CHUNK
    raw, rawt = dt_ref[0] + bias_ref[0], dtt_ref[0] + biast_ref[0]
    dt, dtt = _softplus(raw), _softplus(rawt)
    a, at = -jnp.exp(alog_ref[0]), -jnp.exp(alogt_ref[0])
    tri = jnp.where(_iota((ln, ln), 0) >= _iota((ln, ln), 1), 1.0, 0.0).astype(BF16)
    return dict(raw=raw, rawt=rawt, dt=dt, dtt=dtt, a=a, at=at, tri=tri,
                acum=_xdot(tri, dt * a), acumt=_dot_x_nt(dtt * at, tri))


def _ssd_specs(nc, rev):
    cidx = (lambda c: nc - 1 - c) if rev else (lambda c: c)
    ln = SSM_CHUNK
    xs = pl.BlockSpec((ln, SSM_GN), lambda g, c: (cidx(c), g))
    bs = pl.BlockSpec((ln, SSM_STATE), lambda g, c: (cidx(c), SSM_D_INNER // SSM_STATE + g))
    cs = pl.BlockSpec((ln, SSM_STATE), lambda g, c: (cidx(c), SSM_D_INNER // SSM_STATE + SSM_GROUPS + g))
    dt = pl.BlockSpec((1, ln, SSM_HPG), lambda g, c: (g, cidx(c), 0))
    dtt = pl.BlockSpec((1, SSM_HPG, ln), lambda g, c: (g, 0, cidx(c)))
    row = pl.BlockSpec((1, 1, SSM_HPG), lambda g, c: (g, 0, 0))
    col = pl.BlockSpec((1, SSM_HPG, 1), lambda g, c: (g, 0, 0))
    st = pl.BlockSpec((None, None, SSM_GN, SSM_STATE), lambda g, c: (cidx(c), g, 0, 0))
    return xs, bs, cs, dt, dtt, row, col, st


def _head_expand():
    return jnp.where((_iota((SSM_HPG, SSM_GN), 1) >> 6) == _iota((SSM_HPG, SSM_GN), 0), 1.0, 0.0).astype(BF16)


def _head_expand_t():
    return jnp.where((_iota((SSM_GN, SSM_HPG), 0) >> 6) == _iota((SSM_GN, SSM_HPG), 1), 1.0, 0.0).astype(BF16)


def _dot_x_tn(x, m):
    hi, mid, lo = _split3(x)
    return _dot_tn(hi, m) + _dot_tn(mid, m) + _dot_tn(lo, m)


def _ssd_fwd(xbc, dt_g, dt_gt, bias_r, bias_c, alog_r, alog_c, d_r):
    s = xbc.shape[0]
    ln = SSM_CHUNK
    nc = s // ln

    def body(x_ref, b_ref, c_ref, dt_ref, dtt_ref, bias_ref, biast_ref, alog_ref, alogt_ref, d_ref,
             y_ref, st_ref, state):
        @pl.when(pl.program_id(1) == 0)
        def _():
            state[...] = jnp.zeros_like(state)

        cm = _ssd_common(dt_ref, dtt_ref, bias_ref, biast_ref, alog_ref, alogt_ref)
        acum, acumt = cm["acum"], cm["acumt"]
        ex = _head_expand()
        acum_x = _dot_x(acum, ex)
        xv = x_ref[...]
        xdt = xv * _dot_x(cm["dt"], ex)
        xdtb = xdt.astype(BF16)
        bb, cb = b_ref[...].astype(BF16), c_ref[...].astype(BF16)
        cbm = _dot_nt(cb, bb)
        causal = _iota((ln, ln), 0) >= _iota((ln, ln), 1)
        s2 = state[...]
        st_ref[...] = s2
        for r in range(SSM_HPG):
            sl = slice(r * SSM_P, (r + 1) * SSM_P)
            decay = jnp.exp(jnp.where(causal, acum[:, r:r + 1] - acumt[r:r + 1, :], -jnp.inf))
            y_ref[:, sl] = _dot((cbm * decay).astype(BF16), xdtb[:, sl])
        y_ref[...] = (y_ref[...] + _dot_nt(cb, s2.astype(BF16)) * jnp.exp(acum_x) + _dot_x(d_ref[0], ex) * xv)
        last_x = acum_x[ln - 1:ln, :]
        elast = jnp.exp(_xdot(_head_expand_t(), acumt[:, ln - 1:ln]))
        state[...] = s2 * elast + _dot_tn((xdt * jnp.exp(last_x - acum_x)).astype(BF16), bb)

    xs, bs, cs, dts, dtts, row, col, st = _ssd_specs(nc, False)
    return pl.pallas_call(
        body, name="ssd_fwd", grid=(SSM_GROUPS, nc),
        in_specs=[xs, bs, cs, dts, dtts, row, col, row, col, row],
        out_specs=[xs, st],
        out_shape=[jax.ShapeDtypeStruct((s, SSM_D_INNER), F32),
                   jax.ShapeDtypeStruct((nc, SSM_GROUPS, SSM_GN, SSM_STATE), F32)],
        scratch_shapes=[pltpu.VMEM((SSM_GN, SSM_STATE), F32)],
        compiler_params=_params("parallel", "arbitrary"),
    )(xbc, xbc, xbc, dt_g, dt_gt, bias_r, bias_c, alog_r, alog_c, d_r)


def _ssd_bwd(xbc, dt_g, dt_gt, bias_r, bias_c, alog_r, alog_c, d_r, states, dy):
    s = xbc.shape[0]
    ln = SSM_CHUNK
    nc = s // ln

    def body(x_ref, b_ref, c_ref, dt_ref, dtt_ref, bias_ref, biast_ref, alog_ref, alogt_ref, d_ref,
             st_ref, dy_ref, dx_ref, db_ref, dc_ref, ddt_ref, ddtt_ref, dbias_ref, dbiast_ref,
             dalog_ref, dalogt_ref, dd_ref, dstate):
        step = pl.program_id(1)

        @pl.when(step == 0)
        def _():
            dstate[...] = jnp.zeros_like(dstate)

        cm = _ssd_common(dt_ref, dtt_ref, bias_ref, biast_ref, alog_ref, alogt_ref)
        dt, acum, acumt = cm["dt"], cm["acum"], cm["acumt"]
        ex, ext = _head_expand(), _head_expand_t()
        dt_x, acum_x = _dot_x(dt, ex), _dot_x(acum, ex)
        eac_x, to_end_x = jnp.exp(acum_x), jnp.exp(acum_x[ln - 1:ln, :] - acum_x)
        xv, dyv = x_ref[...], dy_ref[...]
        xdt = xv * dt_x
        xdtb, dyb = xdt.astype(BF16), dyv.astype(BF16)
        dyeb = (dyv * eac_x).astype(BF16)
        bb, cb = b_ref[...].astype(BF16), c_ref[...].astype(BF16)
        cbm = _dot_nt(cb, bb)
        s2, ds2 = st_ref[...], dstate[...]
        s2b, ds2b = s2.astype(BF16), ds2.astype(BF16)
        dxdt_state = _dot_nt(bb, ds2b) * to_end_x
        yoff = _dot_nt(cb, s2b) * eac_x
        dc_acc = _dot(dyeb, s2b)
        db_acc = _dot((xdt * to_end_x).astype(BF16), ds2b)
        f_rows = _dot_x_nt(xdt * dxdt_state, ex)
        elast = jnp.exp(acum[ln - 1:ln, :])
        dlast = (jnp.sum(f_rows, axis=0, keepdims=True)
                 + elast * jnp.sum(_dot_x_tn(ds2 * s2, ext), axis=0, keepdims=True))
        is_last = _iota((ln, 1), 0) == ln - 1
        dac_rows = _dot_x_nt(dyv * yoff, ex) - f_rows + jnp.where(is_last, dlast, 0.0)
        dstate[...] = ds2 * jnp.exp(_xdot(ext, acumt[:, ln - 1:ln])) + _dot_tn(dyeb, cb)
        causal = _iota((ln, ln), 0) >= _iota((ln, ln), 1)
        lane8 = _iota((ln, SSM_HPG), 1)
        sub8 = _iota((SSM_HPG, ln), 0)
        dcb = jnp.zeros((ln, ln), F32)
        dac_cols = jnp.zeros((SSM_HPG, ln), F32)
        for r in range(SSM_HPG):
            sl = slice(r * SSM_P, (r + 1) * SSM_P)
            decay = jnp.exp(jnp.where(causal, acum[:, r:r + 1] - acumt[r:r + 1, :], -jnp.inf))
            dx_ref[:, sl] = _dot_tn((cbm * decay).astype(BF16), dyb[:, sl])
            dcb_r = _dot_nt(dyb[:, sl], xdtb[:, sl]) * decay
            dcb = dcb + dcb_r
            e = dcb_r * cbm
            dac_rows = dac_rows + jnp.where(lane8 == r, jnp.sum(e, axis=-1, keepdims=True), 0.0)
            dac_cols = dac_cols + jnp.where(sub8 == r, jnp.sum(e, axis=0, keepdims=True), 0.0)
        dxdt = dx_ref[...] + dxdt_state
        ddt_all = _dot_x_nt(dxdt * xv, ex)
        dd_all = jnp.sum(_dot_x_nt(dyv * xv, ex), axis=0, keepdims=True)
        dx_ref[...] = dxdt * dt_x + _dot_x(d_ref[0], ex) * dyv
        dcbb = dcb.astype(BF16)
        dc_ref[...] = dc_acc + _dot(dcbb, bb)
        db_ref[...] = db_acc + _dot_tn(dcbb, cb)
        triu = jnp.where(_iota((ln, ln), 0) <= _iota((ln, ln), 1), 1.0, 0.0).astype(BF16)
        g_rows = _xdot(triu, dac_rows)
        g_cols = _dot_x(dac_cols, cm["tri"])
        d_rows = (ddt_all + g_rows * cm["a"]) * _sigmoid(cm["raw"])
        d_cols = -(g_cols * cm["at"]) * _sigmoid(cm["rawt"])
        ddt_ref[0] = d_rows
        ddtt_ref[0] = d_cols
        parts = (jnp.sum(d_rows, axis=0, keepdims=True), jnp.sum(d_cols, axis=1, keepdims=True),
                 jnp.sum(g_rows * dt, axis=0, keepdims=True) * cm["a"],
                 -jnp.sum(g_cols * cm["dtt"], axis=1, keepdims=True) * cm["at"], dd_all)
        outs = (dbias_ref, dbiast_ref, dalog_ref, dalogt_ref, dd_ref)

        @pl.when(step == 0)
        def _():
            for o_ref, p in zip(outs, parts):
                o_ref[0] = p

        @pl.when(step > 0)
        def _():
            for o_ref, p in zip(outs, parts):
                o_ref[0] += p

    xs, bs, cs, dts, dtts, row, col, st = _ssd_specs(nc, True)
    grp = pl.BlockSpec((ln, SSM_STATE), lambda g, c: (nc - 1 - c, g))
    rows = jax.ShapeDtypeStruct((SSM_GROUPS, 1, SSM_HPG), F32)
    cols = jax.ShapeDtypeStruct((SSM_GROUPS, SSM_HPG, 1), F32)
    return pl.pallas_call(
        body, name="ssd_bwd", grid=(SSM_GROUPS, nc),
        in_specs=[xs, bs, cs, dts, dtts, row, col, row, col, row, st, xs],
        out_specs=[xs, grp, grp, dts, dtts, row, col, row, col, row],
        out_shape=[jax.ShapeDtypeStruct((s, SSM_D_INNER), F32),
                   jax.ShapeDtypeStruct((s, SSM_GROUPS * SSM_STATE), F32),
                   jax.ShapeDtypeStruct((s, SSM_GROUPS * SSM_STATE), F32),
                   jax.ShapeDtypeStruct((SSM_GROUPS, s, SSM_HPG), F32),
                   jax.ShapeDtypeStruct((SSM_GROUPS, SSM_HPG, s), F32), rows, cols, rows, cols, rows],
        scratch_shapes=[pltpu.VMEM((SSM_GN, SSM_STATE), F32)],
        compiler_params=_params("parallel", "arbitrary"),
    )(xbc, xbc, xbc, dt_g, dt_gt, bias_r, bias_c, alog_r, alog_c, d_r, states, dy)


def _gate_norm_fwd(y, zx, g):
    s = y.shape[0]
    ts = _tile(s, 512, 8)

    def body(y_ref, z_ref, g_ref, o_ref):
        zv = z_ref[...]
        yg = y_ref[...] * (zv * _sigmoid(zv))
        r = lax.rsqrt(jnp.mean(yg * yg, axis=-1, keepdims=True) + EPS)
        o_ref[...] = (yg * r * g_ref[...]).astype(BF16)

    blk = pl.BlockSpec((ts, SSM_GN), lambda j, i: (i, j))
    vec = pl.BlockSpec((1, SSM_GN), lambda j, i: (0, j))
    return pl.pallas_call(
        body, name="gate_norm_fwd", grid=(SSM_GROUPS, s // ts), in_specs=[blk, blk, vec], out_specs=blk,
        out_shape=jax.ShapeDtypeStruct((s, SSM_D_INNER), BF16), compiler_params=_params("parallel", "parallel"),
    )(y, zx, g)


def _gate_norm_bwd(y, zx, g, dout):
    s = y.shape[0]
    ts = _tile(s, 512, 8)

    def body(y_ref, z_ref, g_ref, do_ref, dy_ref, dz_ref, dg_ref):
        yv, zv, dov = y_ref[...], z_ref[...], do_ref[...].astype(F32)
        sg = _sigmoid(zv)
        silu = zv * sg
        yg = yv * silu
        r = lax.rsqrt(jnp.mean(yg * yg, axis=-1, keepdims=True) + EPS)
        ygn = yg * r
        part = jnp.sum(dov * ygn, axis=0, keepdims=True)

        @pl.when(pl.program_id(1) == 0)
        def _():
            dg_ref[...] = part

        @pl.when(pl.program_id(1) > 0)
        def _():
            dg_ref[...] += part

        dn = dov * g_ref[...]
        dyg = r * (dn - ygn * jnp.mean(dn * ygn, axis=-1, keepdims=True))
        dy_ref[...] = dyg * silu
        dz_ref[...] = (dyg * yv * sg * (1.0 + zv * (1.0 - sg))).astype(BF16)

    blk = pl.BlockSpec((ts, SSM_GN), lambda j, i: (i, j))
    vec = pl.BlockSpec((1, SSM_GN), lambda j, i: (0, j))
    return pl.pallas_call(
        body, name="gate_norm_bwd", grid=(SSM_GROUPS, s // ts), in_specs=[blk, blk, vec, blk],
        out_specs=[blk, blk, vec],
        out_shape=[jax.ShapeDtypeStruct((s, SSM_D_INNER), F32), jax.ShapeDtypeStruct((s, SSM_IN_PAD), BF16),
                   jax.ShapeDtypeStruct((1, SSM_D_INNER), F32)],
        compiler_params=_params("parallel", "arbitrary"),
    )(y, zx, g, dout)


def _rope_tables(positions):
    inv_freq = ROPE_THETA ** (-jnp.arange(0, ATT_HEAD_DIM, 2, dtype=F32) / ATT_HEAD_DIM)
    ang = positions.astype(F32)[:, None] * inv_freq
    return jnp.tile(jnp.cos(ang), (1, 4)), jnp.tile(jnp.sin(ang), (1, 4))


def _group_views(v):
    return v.reshape(SSM_GROUPS, 1, SSM_HPG), v.reshape(SSM_GROUPS, SSM_HPG, 1)


def _ffn_fwd(run, x, norm_g, wg, wu, wd, tag):
    h = _rms_fwd(x, norm_g, f"ffn_norm_{tag}")
    g = run(f"ffn_gate_{tag}", _mm, h, wg, "nn", b_cols=True, out_dtype=BF16)

    def act(uv, gv):
        gv = gv.astype(F32)
        return uv, gv * _sigmoid(gv) * uv

    u, a = run(f"ffn_up_{tag}", _mm, h, wu, "nn", b_cols=True, rows=FUSED_ROWS, fuse=(act, [g], [BF16, BF16]))
    return run(f"ffn_down_{tag}", _mm, a, wd, "nn", add=x), (h, g, u, a)


def _ffn_bwd(run, mats, x, norm_g, wg, wu, wd, saved, dout, tag):
    h, g, u, a = saved

    def act_bwd(da, gv, uv):
        gv, uv = gv.astype(F32), uv.astype(F32)
        sg = _sigmoid(gv)
        return da * uv * sg * (1.0 + gv * (1.0 - sg)), da * (gv * sg)

    dg, du = run(f"ffn_down_dx_{tag}", _mm, dout, wd, "nt", rows=FUSED_ROWS,
                 fuse=(act_bwd, [g, u], [BF16, BF16]))
    dwd = run(f"ffn_down_dw_{tag}", _mm, a, dout, "tn", out_dtype=BF16)
    mats[("ffn_w_down", tag)] = dwd.reshape(N_SHARDS, dwd.shape[0] // N_SHARDS, dwd.shape[1])
    mats[("ffn_w_gate", tag)] = run(f"ffn_gate_dw_{tag}", _mm, h, dg, "tn", out_dtype=BF16, out_cols=True)
    mats[("ffn_w_up", tag)] = run(f"ffn_up_dw_{tag}", _mm, h, du, "tn", out_dtype=BF16, out_cols=True)
    dh = run(f"ffn_gate_dx_{tag}", _mm, dg, wg, "nt", b_cols=True)
    dh = run(f"ffn_up_dx_{tag}", _mm, du, wu, "nt", add=dh, b_cols=True)
    return _rms_bwd(x, norm_g, dh, dout, f"ffn_norm_bwd_{tag}")


class _Hook:
    def __init__(self, make, done):
        self.make, self.done = make, done


class _SemView:
    def __init__(self, sems, off):
        self.sems, self.off, self.at = sems, off, self

    def __getitem__(self, k):
        return self.sems.at[self.off + k]


def _both(h1, h2):
    split = {}

    def make():
        a, b = h1.make(), h2.make()
        na_in, na_out, na_sems = len(a["arrays"]), len(a["out_shapes"]), a["n_sems"]
        split["n"] = na_out

        def build(cin, cout, send_sems, recv_sems):
            return (a["build"](cin[:na_in], cout[:na_out], send_sems, recv_sems)
                    + b["build"](cin[na_in:], cout[na_out:], _SemView(send_sems, na_sems), _SemView(recv_sems, na_sems)))

        aliases = dict(a.get("aliases", {}))
        aliases.update({na_in + i: na_out + o for i, o in b.get("aliases", {}).items()})
        return dict(build=build, arrays=list(a["arrays"]) + list(b["arrays"]),
                    out_shapes=list(a["out_shapes"]) + list(b["out_shapes"]), n_sems=na_sems + b["n_sems"],
                    aliases=aliases)

    def done(res):
        h1.done(res[:split["n"]])
        h2.done(res[split["n"]:])

    return _Hook(make, done)


def _local_step(x, positions, target, w, hooks=None, mats=None):
    hooks = {} if hooks is None else hooks
    mats = {} if mats is None else mats

    def run(name, fn, *args, **kw):
        hook = hooks.get(name)
        if hook is None:
            return fn(*args, name=name, **kw)
        res, carried = fn(*args, name=name, carry=hook.make(), **kw)
        hook.done(carried)
        return res

    cos, sin = _rope_tables(positions)
    row = lambda v: v.reshape(1, -1)
    gq, gk = jnp.tile(row(w["attn_q_norm"]), (1, 2)), jnp.tile(row(w["attn_k_norm"]), (1, 2))
    sinks = w["attn_sinks"].reshape(-1)
    s = x.shape[0]
    row_stack = lambda g: g.reshape(N_SHARDS, g.shape[0] // N_SHARDS, g.shape[1])

    h0 = _rms_fwd(x, row(w["mixer_norm"][0]), "mixer_norm_0")
    qkv = run("attn_qkv", _mm, h0, w["attn_w_qkv"], "nn", b_cols=True)
    q, kk, vlo, vhi = _attn_prep(qkv, cos, sin, gq, gk)
    o = run("attn_fwd", _attn_fwd, q, kk, vlo, vhi, sinks)
    x1 = run("attn_out", _mm, o, w["attn_w_o"], "nn", add=x)
    ffn_w = lambda l: (row(w["ffn_norm"][l]), w["ffn_w_gate"][l], w["ffn_w_up"][l], w["ffn_w_down"][l])
    x2, ffn0 = _ffn_fwd(run, x1, *ffn_w(0), 0)

    h2 = _rms_fwd(x2, row(w["mixer_norm"][1]), "mixer_norm_1")
    zx = run("ssm_in", _mm, h2, w["ssm_w_in"], "nn")
    dt_g = zx[:, DT_OFF:DT_OFF + SSM_HEADS].reshape(s, SSM_GROUPS, SSM_HPG).transpose(1, 0, 2)
    dt_gt = dt_g.transpose(0, 2, 1)
    bias_r, bias_c = _group_views(w["ssm_dt_bias"].reshape(-1))
    alog_r, alog_c = _group_views(w["ssm_a_log"].reshape(-1))
    d_r, _ = _group_views(w["ssm_d"].reshape(-1))
    xbc = _conv_fwd(zx, w["ssm_conv_w"], row(w["ssm_conv_b"]))
    ssd_args = (xbc, dt_g, dt_gt, bias_r, bias_c, alog_r, alog_c, d_r)
    y, states = _ssd_fwd(*ssd_args)
    yn = _gate_norm_fwd(y, zx, row(w["ssm_norm"]))
    x3 = run("ssm_out", _mm, yn, w["ssm_w_out"], "nn", add=x2)
    x4, ffn1 = _ffn_fwd(run, x3, *ffn_w(1), 1)

    loss_row, dx4 = _loss_fwd_bwd(x4, target)

    dx3, dfn1 = _ffn_bwd(run, mats, x3, *ffn_w(1), ffn1, dx4, 1)
    dyn = run("ssm_out_dx", _mm, dx3, w["ssm_w_out"], "nt")
    mats[("ssm_w_out", 0)] = row_stack(run("ssm_out_dw", _mm, yn, dx3, "tn", out_dtype=BF16))
    dy, dzx, dssm_norm = _gate_norm_bwd(y, zx, row(w["ssm_norm"]), dyn)
    dxs, db, dc, ddt_g, ddt_gt, dbias, dbias_t, dalog, dalog_t, dd = _ssd_bwd(*ssd_args, states, dy)
    ddt_g = ddt_g + ddt_gt.transpose(0, 2, 1)
    dpre, dconv_w, dconv_b = _conv_bwd_pre(zx, w["ssm_conv_w"], row(w["ssm_conv_b"]), dxs, db, dc)
    dzx = _conv_bwd_in(dpre, w["ssm_conv_w"], dzx)
    ddt_pad = jnp.pad(ddt_g.transpose(1, 0, 2).reshape(s, SSM_HEADS), ((0, 0), (0, SSM_IN_PAD - SSM_IN)))
    dzx = lax.dynamic_update_slice(dzx, ddt_pad.astype(BF16), (0, DT_OFF))
    dw_in = run("ssm_in_dw", _mm, h2, dzx, "tn", out_dtype=BF16)
    in_shard = SSM_IN // N_SHARDS
    mats[("ssm_w_in", 0)] = jnp.stack([dw_in[:, i * in_shard:(i + 1) * in_shard] for i in range(N_SHARDS)])
    dh2 = run("ssm_in_dx", _mm, dzx, w["ssm_w_in"], "nt")
    dx2, dmn1 = _rms_bwd(x2, row(w["mixer_norm"][1]), dh2, dx3, "mixer_norm_bwd_1")

    dx1, dfn0 = _ffn_bwd(run, mats, x1, *ffn_w(0), ffn0, dx2, 0)
    do = run("attn_out_dx", _mm, dx1, w["attn_w_o"], "nt", out_dtype=BF16)
    mats[("attn_w_o", 0)] = row_stack(run("attn_out_dw", _mm, o, dx1, "tn", out_dtype=BF16))
    dq, dkc, dkp, dvloc, dvlop, dvhic, dvhip, dsink = run("attn_bwd", _attn_bwd, q, kk, vlo, vhi, sinks, do)
    dqkv, dgq, dgk = _attn_prep_bwd(qkv, cos, sin, gq, gk, dq, (dkc, dkp), (dvloc, dvlop), (dvhic, dvhip))
    mats[("attn_w_qkv", 0)] = run("attn_qkv_dw", _mm, h0, dqkv, "tn", out_dtype=BF16, out_cols=True)
    dh0 = run("attn_qkv_dx", _mm, dqkv, w["attn_w_qkv"], "nt", b_cols=True)
    dx0, dmn0 = _rms_bwd(x, row(w["mixer_norm"][0]), dh0, dx1, "mixer_norm_bwd_0")

    fold = lambda v: v[0, :ATT_HEAD_DIM] + v[0, ATT_HEAD_DIM:]
    grads = {
        "mixer_norm": jnp.concatenate([dmn0, dmn1], axis=0),
        "ffn_norm": jnp.concatenate([dfn0, dfn1], axis=0),
        "attn_q_norm": fold(dgq), "attn_k_norm": fold(dgk),
        "attn_sinks": dsink[:, :, 0].reshape(-1),
        "ssm_conv_w": dconv_w, "ssm_conv_b": dconv_b.reshape(-1),
        "ssm_dt_bias": dbias.reshape(-1) + dbias_t.reshape(-1),
        "ssm_a_log": dalog.reshape(-1) + dalog_t.reshape(-1), "ssm_d": dd.reshape(-1),
        "ssm_norm": dssm_norm.reshape(-1),
    }
    return loss_row[0, 0], dx0, grads


OTHER_CHIPS = ((1, 0), (0, 1), (1, 1))


def _position():
    return lax.axis_index("x"), lax.axis_index("y"), lax.axis_index("c")


def _sems(n):
    return pltpu.SemaphoreType.DMA((n,))


def _gather_shards(weights, layers):
    n_in, n_mat = len(weights), len(layers)

    def body(*refs):
        p, out = refs[:n_in], refs[n_in:n_in + n_mat]
        send_sems, recv_sems = refs[n_in + n_mat:]
        x, y, c = _position()
        me, sibling = (x, y, c), (x, y, 1 - c)
        chips = [(x ^ fx, y ^ fy) for fx, fy in OTHER_CHIPS]

        def rows(e, px, py, pc):
            half = out[e].shape[1] // 2
            return out[e].at[2 * px + py, pl.ds(pc * half, half), :]

        def copy(k, e, block, to, src=None):
            return pltpu.make_async_remote_copy(
                src_ref=rows(e, *block) if src is None else src, dst_ref=rows(e, *block),
                send_sem=send_sems.at[k * n_mat + e], recv_sem=recv_sems.at[k * n_mat + e],
                device_id=to, device_id_type=MESH)

        def own(e):
            i, l = layers[e]
            return pltpu.make_async_remote_copy(
                src_ref=p[i].at[l], dst_ref=out[e].at[2 * x + y], send_sem=send_sems.at[6 * n_mat + e],
                recv_sem=recv_sems.at[6 * n_mat + e], device_id=sibling, device_id_type=MESH)

        first, passed = [], []
        for e, (i, l) in enumerate(layers):
            half = out[e].shape[1] // 2
            first.append([copy(j, e, me, (*chip, c), src=p[i].at[l, pl.ds(c * half, half), :])
                          for j, chip in enumerate(chips)])
            for cp in first[-1]:
                cp.start()
        for e in range(n_mat):
            own(e).start()
        for e in range(n_mat):
            passed.append([copy(3 + j, e, (*chip, c), sibling) for j, chip in enumerate(chips)])
            for j, chip in enumerate(chips):
                copy(j, e, (*chip, c), me).wait_recv()
                passed[e][j].start()
        for e in range(n_mat):
            own(e).wait()
            for j, chip in enumerate(chips):
                copy(3 + j, e, (*chip, 1 - c), me).wait_recv()
        for e in range(n_mat):
            for cp in first[e] + passed[e]:
                cp.wait_send()

    return pl.pallas_call(
        body, name="gather_weights", in_specs=[ANY] * n_in, out_specs=[ANY] * n_mat,
        out_shape=[jax.ShapeDtypeStruct((N_SHARDS,) + weights[i].shape[1:], weights[i].dtype) for i, _ in layers],
        scratch_shapes=[_sems(7 * n_mat), _sems(7 * n_mat)],
    )(*weights)


def _all_gather8(block, name):
    m_per, n = block.shape

    def body(x_ref, out_ref, send_sems, recv_sems, local_sem):
        x, y, c = _position()
        me, sibling = (x, y, c), (x, y, 1 - c)
        chips = [(x ^ fx, y ^ fy) for fx, fy in OTHER_CHIPS]

        def rows(px, py, pc):
            return out_ref.at[pl.ds((4 * px + 2 * py + pc) * m_per, m_per), :]

        def copy(k, blk, to, src=None):
            return pltpu.make_async_remote_copy(
                src_ref=rows(*blk) if src is None else src, dst_ref=rows(*blk),
                send_sem=send_sems.at[k], recv_sem=recv_sems.at[k], device_id=to, device_id_type=MESH)

        mine = pltpu.make_async_copy(x_ref, rows(*me), local_sem)
        mine.start()
        first = [copy(0, me, sibling, src=x_ref)]
        first += [copy(1 + j, me, (*chip, c), src=x_ref) for j, chip in enumerate(chips)]
        for cp in first:
            cp.start()
        passed = [copy(4 + j, (*chip, c), sibling) for j, chip in enumerate(chips)]
        for j, chip in enumerate(chips):
            copy(1 + j, (*chip, c), me).wait_recv()
            passed[j].start()
        copy(0, sibling, me).wait_recv()
        for j, chip in enumerate(chips):
            copy(4 + j, (*chip, 1 - c), me).wait_recv()
        for cp in first + passed:
            cp.wait_send()
        mine.wait()

    return pl.pallas_call(
        body, name=name, out_shape=jax.ShapeDtypeStruct((N_DEV * m_per, n), block.dtype),
        in_specs=[pl.BlockSpec(memory_space=pltpu.VMEM)], out_specs=pl.BlockSpec(memory_space=pltpu.VMEM),
        scratch_shapes=[_sems(7), _sems(7), pltpu.SemaphoreType.DMA],
    )(block)


def _exchange(carry, name):
    n_in, n_out = len(carry["arrays"]), len(carry["out_shapes"])

    def body(*refs):
        copies = carry["build"](refs[:n_in], refs[n_in:n_in + n_out], refs[-2], refs[-1])
        for cp in copies:
            cp.start()
        for cp in copies:
            cp.wait()

    return pl.pallas_call(
        body, name=name, in_specs=[ANY] * n_in, out_specs=[ANY] * n_out, out_shape=list(carry["out_shapes"]),
        input_output_aliases=dict(carry.get("aliases", {})),
        scratch_shapes=[_sems(carry["n_sems"]), _sems(carry["n_sems"])],
    )(*carry["arrays"])


def _remote(src, dst, send_sems, recv_sems, k, to):
    return pltpu.make_async_remote_copy(src_ref=src, dst_ref=dst, send_sem=send_sems.at[k], recv_sem=recv_sems.at[k],
                                        device_id=to, device_id_type=MESH)


def _gather_over_ici(blocks, layers):
    def build(p, out, send_sems, recv_sems):
        x, y, c = _position()
        copies = []
        for e, l in enumerate(layers):
            half = out[e].shape[1] // 2
            rows = pl.ds(c * half, half)
            for j, (fx, fy) in enumerate(OTHER_CHIPS):
                copies.append(_remote(p[e].at[l, rows, :], out[e].at[2 * x + y, rows, :], send_sems, recv_sems,
                                      3 * e + j, (x ^ fx, y ^ fy, c)))
        return copies

    shapes = [jax.ShapeDtypeStruct((N_SHARDS,) + b.shape[1:], b.dtype) for b in blocks]
    return dict(build=build, arrays=list(blocks), out_shapes=shapes, n_sems=3 * len(layers))


def _gather_over_d2d(stacks, blocks, layers):
    n = len(stacks)

    def build(refs, out, send_sems, recv_sems):
        p = refs[n:]
        x, y, c = _position()
        sibling = (x, y, 1 - c)
        copies = []
        for e, l in enumerate(layers):
            half = out[e].shape[1] // 2
            for j, (fx, fy) in enumerate(OTHER_CHIPS):
                rows = out[e].at[2 * (x ^ fx) + (y ^ fy), pl.ds(c * half, half), :]
                copies.append(_remote(rows, rows, send_sems, recv_sems, 4 * e + j, sibling))
            copies.append(_remote(p[e].at[l], out[e].at[2 * x + y], send_sems, recv_sems, 4 * e + 3, sibling))
        return copies

    shapes = [jax.ShapeDtypeStruct(s.shape, s.dtype) for s in stacks]
    return dict(build=build, arrays=list(stacks) + list(blocks), out_shapes=shapes, n_sems=4 * n,
                aliases={i: i for i in range(n)})


def _grads_to_sibling(stacks):
    def build(g, out, send_sems, recv_sems):
        x, y, c = _position()
        copies = []
        for e in range(len(stacks)):
            half = g[e].shape[1] // 2
            copies.append(_remote(g[e].at[:, pl.ds((1 - c) * half, half), :], out[e], send_sems, recv_sems, e,
                                  (x, y, 1 - c)))
        return copies

    shapes = [jax.ShapeDtypeStruct((N_SHARDS, g.shape[1] // 2, g.shape[2]), g.dtype) for g in stacks]
    return dict(build=build, arrays=list(stacks), out_shapes=shapes, n_sems=len(stacks))


def _grads_to_owners(partials):
    def build(p, out, send_sems, recv_sems):
        x, y, c = _position()
        copies = []
        for e in range(len(partials)):
            for k, (fx, fy) in enumerate(OTHER_CHIPS):
                px, py = x ^ fx, y ^ fy
                copies.append(_remote(p[e].at[2 * px + py], out[e].at[k], send_sems, recv_sems, 3 * e + k,
                                      (px, py, c)))
        return copies

    shapes = [jax.ShapeDtypeStruct((len(OTHER_CHIPS),) + p.shape[1:], p.dtype) for p in partials]
    return dict(build=build, arrays=list(partials), out_shapes=shapes, n_sems=3 * len(partials))


def _share_halves(grads, layers):
    def build(_, out, send_sems, recv_sems):
        x, y, c = _position()
        copies = []
        for e, (i, l) in enumerate(layers):
            half = out[i].shape[1] // 2
            rows = out[i].at[l, pl.ds(c * half, half), :]
            copies.append(_remote(rows, rows, send_sems, recv_sems, e, (x, y, 1 - c)))
        return copies

    return dict(build=build, arrays=list(grads), out_shapes=[jax.ShapeDtypeStruct(g.shape, g.dtype) for g in grads],
                n_sems=len(layers), aliases={i: i for i in range(len(grads))})


ADD_BLOCK_ELEMS = 1 << 19


def _add_rows(half, cols):
    return _tile(half, max(16, ADD_BLOCK_ELEMS // cols // 16 * 16), 16)


def _add_pair(stack, recv, c_idx, name):
    _, half, cols = recv.shape
    tr = _add_rows(half, cols)
    nt = half // tr

    def body(c_ref, a_ref, b_ref, o_ref):
        o_ref[...] = (a_ref[...].astype(F32) + b_ref[...].astype(F32)).astype(o_ref.dtype)

    blk = pl.BlockSpec((None, tr, cols), lambda s, i, c_ref: (s, i, 0))
    return pl.pallas_call(
        body, name=name,
        grid_spec=pltpu.PrefetchScalarGridSpec(
            num_scalar_prefetch=1, grid=(N_SHARDS, nt),
            in_specs=[pl.BlockSpec((None, tr, cols), lambda s, i, c_ref: (s, c_ref[0] * nt + i, 0)), blk],
            out_specs=blk),
        out_shape=jax.ShapeDtypeStruct(recv.shape, recv.dtype),
        compiler_params=_params("parallel", "parallel"),
    )(c_idx, stack, recv)


def _add_owned(partial, recv, sc_idx, layer, shape, into, name):
    _, half, cols = partial.shape
    tr = _add_rows(half, cols)
    nt = half // tr

    def body(sc_ref, a_ref, r0_ref, r1_ref, r2_ref, *rest):
        o_ref = rest[-1]
        o_ref[...] = (((a_ref[...].astype(F32) + r0_ref[...].astype(F32)) + r1_ref[...].astype(F32))
                      + r2_ref[...].astype(F32))

    slot = lambda k: pl.BlockSpec((None, tr, cols), lambda i, sc_ref: (k, i, 0))
    has_into = into is not None
    return pl.pallas_call(
        body, name=name,
        grid_spec=pltpu.PrefetchScalarGridSpec(
            num_scalar_prefetch=1, grid=(nt,),
            in_specs=[pl.BlockSpec((None, tr, cols), lambda i, sc_ref: (sc_ref[0], i, 0)), slot(0), slot(1), slot(2)]
            + ([ANY] if has_into else []),
            out_specs=pl.BlockSpec((None, tr, cols), lambda i, sc_ref: (layer, sc_ref[1] * nt + i, 0))),
        out_shape=jax.ShapeDtypeStruct(shape, F32),
        input_output_aliases={5: 0} if has_into else {},
        compiler_params=_params("parallel"),
    )(*((sc_idx, partial, recv, recv, recv) + ((into,) if has_into else ())))


def _sum8(gathered):
    m = gathered.shape[0] // N_DEV

    def body(g_ref, o_ref):
        total = g_ref[0:m, :]
        for d in range(1, N_DEV):
            total = total + g_ref[d * m:(d + 1) * m, :]
        o_ref[...] = total

    return pl.pallas_call(
        body, name="small_grads_sum", out_shape=jax.ShapeDtypeStruct((m, LANES), F32),
        in_specs=[pl.BlockSpec(memory_space=pltpu.VMEM)], out_specs=pl.BlockSpec(memory_space=pltpu.VMEM),
    )(gathered)


ADAMW_BLOCK_ELEMS = 1 << 18


def _adamw(w, g, m, v, name):
    l, r, cols = w.shape
    tr = _tile(r, max(8, ADAMW_BLOCK_ELEMS // cols // 8 * 8), 8)

    def body(w_ref, g_ref, m_ref, v_ref, d_ref, nm_ref, nv_ref):
        gv = g_ref[...]
        nm = ADAM_B1 * m_ref[...] + (1.0 - ADAM_B1) * gv
        nv = ADAM_B2 * v_ref[...] + (1.0 - ADAM_B2) * jnp.square(gv)
        m_hat = nm / (1.0 - ADAM_B1 ** ADAM_STEP)
        v_hat = nv / (1.0 - ADAM_B2 ** ADAM_STEP)
        d_ref[...] = -ADAM_LR * (m_hat / (jnp.sqrt(v_hat) + ADAM_EPS) + ADAM_WD * w_ref[...])
        nm_ref[...] = nm
        nv_ref[...] = nv

    blk = pl.BlockSpec((None, tr, cols), lambda a, i: (a, i, 0))
    return pl.pallas_call(
        body, name=name, grid=(l, r // tr), in_specs=[blk] * 4, out_specs=[blk] * 3,
        out_shape=[jax.ShapeDtypeStruct(w.shape, F32)] * 3, compiler_params=_params("parallel", "parallel"),
    )(w, g, m, v)


WEIGHTS = ("mixer_norm", "ffn_norm", "attn_w_qkv", "attn_q_norm", "attn_k_norm", "attn_sinks", "attn_w_o",
           "ssm_w_in", "ssm_conv_w", "ssm_conv_b", "ssm_dt_bias", "ssm_a_log", "ssm_d", "ssm_norm", "ssm_w_out",
           "ffn_w_gate", "ffn_w_up", "ffn_w_down")
BIG = ("attn_w_qkv", "attn_w_o", "ffn_w_gate", "ffn_w_up", "ffn_w_down", "ssm_w_in", "ssm_w_out")
MATRICES = (("attn_w_qkv", 0), ("attn_w_o", 0), ("ffn_w_gate", 0), ("ffn_w_up", 0), ("ffn_w_down", 0),
            ("ssm_w_in", 0), ("ssm_w_out", 0), ("ffn_w_gate", 1), ("ffn_w_up", 1), ("ffn_w_down", 1))
MATRIX_LAYERS = tuple((BIG.index(n), l) for n, l in MATRICES)
GROUPS = {"attn": MATRICES[0:2], "ffn0": MATRICES[2:5], "ssm": MATRICES[5:7], "ffn1": MATRICES[7:10]}
SMALL_SHARDED = ("ssm_conv_w", "ssm_conv_b", "ssm_norm")
SMALL = tuple(n for n in WEIGHTS if n not in BIG)


def _pack_rows(parts, row_unit=8):
    flat = jnp.concatenate([p.reshape(-1) for p in parts])
    pad = (-flat.shape[0]) % (LANES * row_unit)
    return jnp.pad(flat, (0, pad)).reshape(-1, LANES)


def _unpack(flat, shapes):
    out, off = [], 0
    for shp in shapes:
        size = math.prod(shp)
        out.append(flat[off:off + size].reshape(shp))
        off += size
    return out


def kernel(x, positions, mixer_norm, ffn_norm, attn_w_qkv, attn_q_norm, attn_k_norm, attn_sinks, attn_w_o, ssm_w_in, ssm_conv_w, ssm_conv_b, ssm_dt_bias, ssm_a_log, ssm_d, ssm_norm, ssm_w_out, ffn_w_gate, ffn_w_up, ffn_w_down, loss_target, m_mixer_norm, m_ffn_norm, m_attn_w_qkv, m_attn_q_norm, m_attn_k_norm, m_attn_sinks, m_attn_w_o, m_ssm_w_in, m_ssm_conv_w, m_ssm_conv_b, m_ssm_dt_bias, m_ssm_a_log, m_ssm_d, m_ssm_norm, m_ssm_w_out, m_ffn_w_gate, m_ffn_w_up, m_ffn_w_down, v_mixer_norm, v_ffn_norm, v_attn_w_qkv, v_attn_q_norm, v_attn_k_norm, v_attn_sinks, v_attn_w_o, v_ssm_w_in, v_ssm_conv_w, v_ssm_conv_b, v_ssm_dt_bias, v_ssm_a_log, v_ssm_d, v_ssm_norm, v_ssm_w_out, v_ffn_w_gate, v_ffn_w_up, v_ffn_w_down):
    args = locals()
    w = {n: args[n] for n in WEIGHTS}
    m = {n: args["m_" + n] for n in WEIGHTS}
    v = {n: args["v_" + n] for n in WEIGHTS}
    ax, ay, ac = lax.axis_index("x"), lax.axis_index("y"), lax.axis_index("c")
    shard = 2 * ax + ay

    wb = {n: w[n].astype(BF16) for n in BIG}
    wl, hooks = {"ffn_w_gate": [None, None], "ffn_w_up": [None, None], "ffn_w_down": [None, None]}, {}

    def gathered(keys, stacks):
        for (n, l), st in zip(keys, stacks):
            if n == "ssm_w_in":
                wl[n] = jnp.concatenate([st[i] for i in range(N_SHARDS)]
                                        + [jnp.zeros((st.shape[1], SSM_IN_PAD - SSM_IN), BF16)], axis=1)
            elif n in ("ffn_w_gate", "ffn_w_up"):
                wl[n][l] = st
            elif n == "ffn_w_down":
                wl[n][l] = st.reshape(st.shape[0] * st.shape[1], st.shape[2])
            elif n == "attn_w_qkv":
                wl[n] = st
            else:
                wl[n] = st.reshape(st.shape[0] * st.shape[1], st.shape[2])

    def behind(name, hook):
        hooks[name] = _both(hooks[name], hook) if name in hooks else hook

    def gather_behind(keys, first_leg, second_leg):
        blocks, layers, got = [wb[n] for n, _ in keys], [l for _, l in keys], {}
        behind(first_leg, _Hook(lambda: _gather_over_ici(blocks, layers), lambda res: got.update(stacks=res)))
        behind(second_leg, _Hook(lambda: _gather_over_d2d(got["stacks"], blocks, layers),
                                 lambda res: gathered(keys, res)))

    gathered(GROUPS["attn"], _gather_shards([wb[n] for n, _ in GROUPS["attn"]],
                                            [(e, l) for e, (_, l) in enumerate(GROUPS["attn"])]))
    gather_behind(GROUPS["ffn0"], "attn_fwd", "attn_out")
    gather_behind(GROUPS["ssm"][:1], "ffn_gate_0", "ffn_up_0")
    gather_behind(GROUPS["ssm"][1:], "ffn_up_0", "ffn_down_0")
    gather_behind(GROUPS["ffn1"], "ssm_in", "ssm_out")
    small_shapes = [w[n].shape for n in SMALL_SHARDED]
    small_all = _all_gather8(_pack_rows([w[n] for n in SMALL_SHARDED]), "gather_small_params")
    small_all = small_all.reshape(N_DEV, -1)[::2]
    full, off = {}, 0
    for n, shp in zip(SMALL_SHARDED, small_shapes):
        size = math.prod(shp)
        seg = small_all[:, off:off + size].reshape((N_SHARDS,) + shp)
        full[n] = jnp.moveaxis(seg, 0, -2).reshape(shp[:-1] + (N_SHARDS * shp[-1],))
        off += size
    wl.update({
        "mixer_norm": mixer_norm, "ffn_norm": ffn_norm,
        "attn_q_norm": attn_q_norm[0], "attn_k_norm": attn_k_norm[0], "attn_sinks": attn_sinks[0],
        "ssm_conv_w": full["ssm_conv_w"][0], "ssm_conv_b": full["ssm_conv_b"][0],
        "ssm_dt_bias": ssm_dt_bias[0], "ssm_a_log": ssm_a_log[0], "ssm_d": ssm_d[0],
        "ssm_norm": full["ssm_norm"][0],
    })

    c_idx = ac.reshape(1).astype(jnp.int32)
    sc_idx = jnp.stack([shard, ac]).astype(jnp.int32)
    mats, halves = {}, {n: None for n in BIG}

    def pair_sums(keys, recv):
        return [_add_pair(mats[k], r, c_idx, f"grads_add_pair_{k[0]}_{k[1]}") for k, r in zip(keys, recv)]

    def owner_sums(keys, partials, recv):
        for (n, l), p, r in zip(keys, partials, recv):
            halves[n] = _add_owned(p, r, sc_idx, l, w[n].shape, halves[n], f"grads_add_owned_{n}_{l}")

    def reduce_behind(keys, first_leg, second_leg):
        got = {}
        behind(first_leg, _Hook(lambda: _grads_to_sibling([mats[k] for k in keys]),
                                lambda res: got.update(partials=pair_sums(keys, res))))
        behind(second_leg, _Hook(lambda: _grads_to_owners(got["partials"]),
                                 lambda res: owner_sums(keys, got["partials"], res)))

    reduce_behind(GROUPS["ffn1"], "ssm_out_dx", "ssm_in_dw")
    reduce_behind(GROUPS["ssm"], "ssm_in_dx", "ffn_down_dx_0")
    reduce_behind(GROUPS["ffn0"], "attn_out_dx", "attn_bwd")
    reduce_behind(GROUPS["attn"][1:], "attn_bwd", "attn_qkv_dw")
    last, early, tail = GROUPS["attn"][0], [n for n in BIG if n != "attn_w_qkv"], {}
    behind("attn_qkv_dx", _Hook(lambda: _grads_to_sibling([mats[last]]),
                                lambda res: tail.update(partials=pair_sums([last], res))))
    behind("attn_qkv_dx", _Hook(
        lambda: _share_halves([halves[n] for n in early], [(early.index(n), l) for n, l in MATRICES if n in early]),
        lambda res: tail.update(grads=dict(zip(early, res)))))
    loss_part, dx, g_full = _local_step(x[0], positions[0], loss_target[0], wl, hooks, mats)
    owner_sums([last], tail["partials"], _exchange(_grads_to_owners(tail["partials"]), "grads_to_owners"))
    grads = tail["grads"]
    grads[last[0]], = _exchange(_share_halves([halves[last[0]]], [(0, 0)]), "grads_share_halves")

    small_full_shapes = [g_full[n].shape for n in SMALL] + [(1,)]
    small_g = _pack_rows([g_full[n] for n in SMALL] + [loss_part.reshape(1)])
    small_sum = _sum8(_all_gather8(small_g, "gather_small_grads")).reshape(-1)
    *small_list, loss = _unpack(small_sum, small_full_shapes)
    for n, g in zip(SMALL, small_list):
        if n in SMALL_SHARDED:
            width = w[n].shape[-1]
            g = lax.dynamic_slice_in_dim(g, shard * width, width, axis=g.ndim - 1)
        grads[n] = g.reshape(w[n].shape)

    delta, new_m, new_v = {}, {}, {}
    for n in BIG:
        delta[n], new_m[n], new_v[n] = _adamw(w[n], grads[n], m[n], v[n], "adamw_" + n)
    small_local = [w[n].shape for n in SMALL]
    pk = lambda t: _pack_rows([t[n] for n in SMALL])[None]
    outs = _adamw(pk(w), pk(grads), pk(m), pk(v), "adamw_small")
    for res, o in zip((delta, new_m, new_v), outs):
        for n, a in zip(SMALL, _unpack(o.reshape(-1), small_local)):
            res[n] = a

    return (loss.reshape(()), dx[None], *[grads[n] for n in WEIGHTS], *[delta[n] for n in WEIGHTS],
            *[new_m[n] for n in WEIGHTS], *[new_v[n] for n in WEIGHTS])
```

```python
import math

import jax
import jax.numpy as jnp
from jax import lax
from jax.experimental import pallas as pl
from jax.experimental.pallas import tpu as pltpu

F32 = jnp.float32
BF16 = jnp.bfloat16

D_MODEL = 2048
EPS = 1e-6
ATT_HEAD_DIM = 64
ATT_Q_HEADS = 32
ATT_KV_HEADS = 4
ATT_GROUP = 8
ATT_BLOCK = 128
ROPE_THETA = 10000.0
Q_WIDTH = ATT_Q_HEADS * ATT_HEAD_DIM
KV_WIDTH = ATT_KV_HEADS * ATT_HEAD_DIM
SSM_D_INNER = 4096
SSM_HEADS = 64
SSM_GROUPS = 8
SSM_HPG = 8
SSM_P = 64
SSM_STATE = 128
SSM_CONV = 4
SSM_CHUNK = 256
SSM_CONV_DIM = 6144
SSM_GN = SSM_D_INNER // SSM_GROUPS
SSM_IN = SSM_D_INNER + SSM_CONV_DIM + SSM_HEADS
LANES = 128
SSM_IN_PAD = -(-SSM_IN // LANES) * LANES
N_SHARDS = 4
N_DEV = 8

ADAM_LR = 0.001
ADAM_B1 = 0.9
ADAM_B2 = 0.999
ADAM_EPS = 1e-08
ADAM_WD = 0.01
ADAM_STEP = 10

VMEM_LIMIT = 56 * 1024 * 1024
MESH = pl.DeviceIdType.MESH
ANY = pl.BlockSpec(memory_space=pl.ANY)


def _params(*sem):
    return pltpu.CompilerParams(dimension_semantics=sem, vmem_limit_bytes=VMEM_LIMIT)


def _sems(n):
    return pltpu.SemaphoreType.DMA((n,))


def _call(body, carry, name, grid, in_specs, out_specs, out_shape, scratch_shapes, sem, args):
    if carry is None:
        return pl.pallas_call(body, name=name, grid=grid, in_specs=in_specs, out_specs=out_specs,
                              out_shape=out_shape, scratch_shapes=scratch_shapes,
                              compiler_params=_params(*sem))(*args)
    n_in, n_out, n_scr = len(in_specs), len(out_specs), len(scratch_shapes)
    c_arrays, c_shapes = list(carry["arrays"]), list(carry["out_shapes"])
    n_cin, n_cout = len(c_arrays), len(c_shapes)

    def carrying(*refs):
        ins, refs = refs[:n_in], refs[n_in:]
        cin, refs = refs[:n_cin], refs[n_cin:]
        outs, refs = refs[:n_out], refs[n_out:]
        cout, refs = refs[:n_cout], refs[n_cout:]
        scratch, (send_sems, recv_sems) = refs[:n_scr], refs[n_scr:]
        copies = carry["build"](cin, cout, send_sems, recv_sems)
        ids = [pl.program_id(d) for d in range(len(grid))]
        first, last = ids[0] == 0, ids[0] == grid[0] - 1
        for d in range(1, len(grid)):
            first = jnp.logical_and(first, ids[d] == 0)
            last = jnp.logical_and(last, ids[d] == grid[d] - 1)

        @pl.when(first)
        def _():
            for cp in copies:
                cp.start()

        body(*ins, *outs, *scratch)

        @pl.when(last)
        def _():
            for cp in copies:
                cp.wait()

    aliases = {n_in + i: n_out + o for i, o in carry.get("aliases", {}).items()}
    return pl.pallas_call(
        carrying, name=name, grid=grid, in_specs=list(in_specs) + [ANY] * n_cin,
        out_specs=list(out_specs) + [ANY] * n_cout, out_shape=list(out_shape) + c_shapes,
        scratch_shapes=list(scratch_shapes) + [_sems(carry["n_sems"]), _sems(carry["n_sems"])],
        input_output_aliases=aliases, compiler_params=_params(*(["arbitrary"] * len(grid))))(*args, *c_arrays)


def _tile(dim, target, unit=LANES):
    if dim <= target:
        return dim
    t = (target // unit) * unit
    while t >= unit:
        if dim % t == 0:
            return t
        t -= unit
    return dim


def _dot(a, b):
    return lax.dot_general(a, b, (((1,), (0,)), ((), ())), preferred_element_type=F32)


def _dot_nt(a, b):
    return lax.dot_general(a, b, (((1,), (1,)), ((), ())), preferred_element_type=F32)


def _dot_tn(a, b):
    return lax.dot_general(a, b, (((0,), (0,)), ((), ())), preferred_element_type=F32)


def _split3(x):
    hi = x.astype(BF16)
    r1 = x - hi.astype(F32)
    mid = r1.astype(BF16)
    lo = (r1 - mid.astype(F32)).astype(BF16)
    return hi, mid, lo


def _dot_x(x, m):
    hi, mid, lo = _split3(x)
    return _dot(hi, m) + _dot(mid, m) + _dot(lo, m)


def _dot_x2(x, m):
    hi = x.astype(BF16)
    return _dot(hi, m) + _dot((x - hi.astype(F32)).astype(BF16), m)


def _xdot(m, x):
    hi, mid, lo = _split3(x)
    return _dot(m, hi) + _dot(m, mid) + _dot(m, lo)


def _dot_x2_nt(x, m):
    hi = x.astype(BF16)
    return _dot_nt(hi, m) + _dot_nt((x - hi.astype(F32)).astype(BF16), m)


def _dot_x_nt(x, m):
    hi, mid, lo = _split3(x)
    return _dot_nt(hi, m) + _dot_nt(mid, m) + _dot_nt(lo, m)


def _iota(shape, dim):
    return lax.broadcasted_iota(jnp.int32, shape, dim)


def _sigmoid(x):
    return 0.5 * jnp.tanh(0.5 * x) + 0.5


def _softplus(x):
    return jnp.maximum(x, 0.0) + jnp.log(1.0 + jnp.exp(-jnp.abs(x)))


MM_ROWS = 1024
MM_TILE = 1408
MM_DEPTH = 3456
FUSED_ROWS = 512


def _mm(a, b, mode, name, add=None, out_dtype=F32, b_cols=False, out_cols=False, fuse=None, rows=MM_ROWS,
        carry=None):
    bs = b.shape[-2:]
    if b_cols:
        bs = (bs[0], N_SHARDS * bs[1])
    if mode == "nn":
        (m, k), (k2, n) = a.shape, bs
    elif mode == "nt":
        (m, k), (n, k2) = a.shape, bs
    else:
        (k, m), (k2, n) = a.shape, bs
    assert k == k2, (a.shape, b.shape, mode)
    split_n = (b_cols and mode == "nn") or out_cols
    split_k = b_cols and mode == "nt"
    tm = _tile(m, MM_TILE if mode == "tn" else rows)
    tn = _tile(n // N_SHARDS if split_n else n, MM_TILE)
    tk = _tile(k // N_SHARDS if split_k else k, MM_DEPTH)
    nk = k // tk
    nj, nq = (n // N_SHARDS) // tn, (k // N_SHARDS) // tk
    two_shards = split_k and nq == 1
    if two_shards:
        nk = N_SHARDS // 2
    if mode == "tn":
        a_spec = pl.BlockSpec((tk, tm), lambda i, j, q: (q, i))
    else:
        a_spec = pl.BlockSpec((tm, 2 * tk if two_shards else tk), lambda i, j, q: (i, q))
    if mode == "nt":
        if two_shards:
            b_spec = pl.BlockSpec((2, tn, tk), lambda i, j, q: (q, j, 0))
        elif b_cols:
            b_spec = pl.BlockSpec((None, tn, tk), lambda i, j, q: (q // nq, j, q % nq))
        else:
            b_spec = pl.BlockSpec((tn, tk), lambda i, j, q: (j, q))
    elif b_cols:
        b_spec = pl.BlockSpec((None, tk, tn), lambda i, j, q: (j // nj, q, j % nj))
    else:
        b_spec = pl.BlockSpec((tk, tn), lambda i, j, q: (q, j))
    add_spec = pl.BlockSpec((tm, tn), lambda i, j, q: (i, j))
    if out_cols:
        o_spec = pl.BlockSpec((None, tm, tn), lambda i, j, q: (j // nj, i, j % nj))
        o_shape = (N_SHARDS, m, n // N_SHARDS)
    else:
        o_spec, o_shape = add_spec, (m, n)
    dot = {"nn": _dot, "nt": _dot_nt, "tn": _dot_tn}[mode]
    has_add = add is not None
    fuse_fn, extra, out_dtypes = fuse if fuse is not None else (None, [], [out_dtype])
    n_in, n_out = 2 + has_add + len(extra), len(out_dtypes)

    def body(*refs):
        a_ref, b_ref = refs[:2]
        add_ref = refs[2] if has_add else None
        extra_refs = refs[2 + has_add:n_in]
        o_refs, acc_ref = refs[n_in:n_in + n_out], refs[n_in + n_out]
        if two_shards:
            part = (dot(a_ref[:, :tk].astype(BF16), b_ref[0].astype(BF16))
                    + dot(a_ref[:, tk:].astype(BF16), b_ref[1].astype(BF16)))
        else:
            part = dot(a_ref[...].astype(BF16), b_ref[...].astype(BF16))

        def finish(total):
            if has_add:
                total = total + add_ref[...].astype(F32)
            outs = (total,) if fuse_fn is None else fuse_fn(total, *[r[...] for r in extra_refs])
            for o_ref, val in zip(o_refs, outs):
                o_ref[...] = val.astype(o_ref.dtype)

        if nk == 1:
            finish(part)
        else:
            q = pl.program_id(2)

            @pl.when(q == 0)
            def _():
                acc_ref[...] = part

            @pl.when(jnp.logical_and(q > 0, q < nk - 1))
            def _():
                acc_ref[...] += part

            @pl.when(q == nk - 1)
            def _():
                finish(acc_ref[...] + part)

    in_specs = [a_spec, b_spec] + [add_spec] * (has_add + len(extra))
    args = (a, b) + ((add,) if has_add else ()) + tuple(extra)
    res = _call(body, carry, name, (m // tm, n // tn, nk), in_specs, [o_spec] * n_out,
                [jax.ShapeDtypeStruct(o_shape, dt) for dt in out_dtypes],
                [pltpu.VMEM((tm, tn) if nk > 1 else (8, LANES), F32)], ("parallel", "parallel", "arbitrary"), args)
    main = res[0] if fuse is None else res[:n_out]
    return main if carry is None else (main, res[n_out:])


def _rms_fwd(x, g, name):
    s, d = x.shape
    ts = _tile(s, 512, 8)

    def body(x_ref, g_ref, o_ref):
        xv = x_ref[...]
        r = lax.rsqrt(jnp.mean(xv * xv, axis=-1, keepdims=True) + EPS)
        o_ref[...] = (xv * r * g_ref[...]).astype(BF16)

    return pl.pallas_call(
        body, name=name, grid=(s // ts,),
        in_specs=[pl.BlockSpec((ts, d), lambda i: (i, 0)), pl.BlockSpec((1, d), lambda i: (0, 0))],
        out_specs=pl.BlockSpec((ts, d), lambda i: (i, 0)),
        out_shape=jax.ShapeDtypeStruct((s, d), BF16),
        compiler_params=_params("parallel"),
    )(x, g)


def _rms_bwd(x, g, dh, dres, name):
    s, d = x.shape
    ts = _tile(s, 512, 8)

    def body(x_ref, g_ref, dh_ref, dres_ref, dx_ref, dg_ref):
        xv = x_ref[...]
        r = lax.rsqrt(jnp.mean(xv * xv, axis=-1, keepdims=True) + EPS)
        xhat = xv * r
        dhv = dh_ref[...].astype(F32)
        part = jnp.sum(dhv * xhat, axis=0, keepdims=True)

        @pl.when(pl.program_id(0) == 0)
        def _():
            dg_ref[...] = part

        @pl.when(pl.program_id(0) > 0)
        def _():
            dg_ref[...] += part

        dxh = dhv * g_ref[...]
        dx = r * (dxh - xhat * jnp.mean(dxh * xhat, axis=-1, keepdims=True))
        dx_ref[...] = dres_ref[...] + dx

    row = pl.BlockSpec((ts, d), lambda i: (i, 0))
    vec = pl.BlockSpec((1, d), lambda i: (0, 0))
    return pl.pallas_call(
        body, name=name, grid=(s // ts,),
        in_specs=[row, vec, row, row], out_specs=[row, vec],
        out_shape=[jax.ShapeDtypeStruct((s, d), F32), jax.ShapeDtypeStruct((1, d), F32)],
        compiler_params=_params("arbitrary"),
    )(x, g, dh, dres)


def _act_fwd(g, u, name):
    s, f = g.shape
    ts, tf = _tile(s, 512, 8), _tile(f, 1408)

    def body(g_ref, u_ref, o_ref):
        gv = g_ref[...]
        o_ref[...] = (gv * _sigmoid(gv) * u_ref[...]).astype(BF16)

    blk = pl.BlockSpec((ts, tf), lambda i, j: (i, j))
    return pl.pallas_call(
        body, name=name, grid=(s // ts, f // tf), in_specs=[blk, blk], out_specs=blk,
        out_shape=jax.ShapeDtypeStruct((s, f), BF16), compiler_params=_params("parallel", "parallel"),
    )(g, u)


def _act_bwd(g, u, da, name):
    s, f = g.shape
    ts, tf = _tile(s, 512, 8), _tile(f, 1408)

    def body(g_ref, u_ref, da_ref, dg_ref, du_ref):
        gv, uv, dav = g_ref[...], u_ref[...], da_ref[...].astype(F32)
        sg = _sigmoid(gv)
        silu = gv * sg
        du_ref[...] = (dav * silu).astype(BF16)
        dg_ref[...] = (dav * uv * sg * (1.0 + gv * (1.0 - sg))).astype(BF16)

    blk = pl.BlockSpec((ts, tf), lambda i, j: (i, j))
    return pl.pallas_call(
        body, name=name, grid=(s // ts, f // tf), in_specs=[blk, blk, blk], out_specs=[blk, blk],
        out_shape=[jax.ShapeDtypeStruct((s, f), BF16)] * 2, compiler_params=_params("parallel", "parallel"),
    )(g, u, da)


def _loss_fwd_bwd(y, target):
    s, d = y.shape
    ts = _tile(s, 512, 8)

    def body(y_ref, t_ref, l_ref, dy_ref):
        diff = y_ref[...] - t_ref[...]
        dy_ref[...] = diff * (1.0 / d)
        part = jnp.full((1, LANES), 0.5 * jnp.sum(jnp.mean(diff * diff, axis=-1, keepdims=True)), F32)

        @pl.when(pl.program_id(0) == 0)
        def _():
            l_ref[...] = part

        @pl.when(pl.program_id(0) > 0)
        def _():
            l_ref[...] += part

    row = pl.BlockSpec((ts, d), lambda i: (i, 0))
    acc = pl.BlockSpec((1, LANES), lambda i: (0, 0))
    return pl.pallas_call(
        body, name="loss", grid=(s // ts,), in_specs=[row, row], out_specs=[acc, row],
        out_shape=[jax.ShapeDtypeStruct((1, LANES), F32), jax.ShapeDtypeStruct((s, d), F32)],
        compiler_params=_params("arbitrary"),
    )(y, target)


def _lane_consts():
    r, c = _iota((LANES, LANES), 0), _iota((LANES, LANES), 1)
    same = (r >> 6) == (c >> 6)
    rin, cin = r & 63, c & 63
    one = lambda cond: jnp.where(cond, 1.0, 0.0).astype(BF16)
    return dict(
        seg=one(same),
        rot=(jnp.where(same & (rin == cin + 32), -1.0, 0.0)
             + jnp.where(same & (cin == rin + 32), 1.0, 0.0)).astype(BF16),
        dup_lo=one(r == cin), dup_hi=one(r == cin + 64),
        up=one((c >= 64) & (r == c - 64)), down=one((c < 64) & (r == c + 64)),
        fold_lo=one((c < 64) & (rin == c)), fold_hi=one((c >= 64) & (rin == c - 64)),
    )


def _norm_rope(xc, gain, cos, sin, k):
    ss = _dot_x2(xc * xc, k["seg"])
    rinv = lax.rsqrt(ss * (1.0 / ATT_HEAD_DIM) + EPS)
    xhat = xc * rinv
    y = xhat * gain
    return y * cos + _dot_x2(y, k["rot"]) * sin, xhat, rinv


def _norm_rope_bwd(dr, xhat, rinv, gain, cos, sin, k):
    dy = dr * cos - _dot_x2(dr * sin, k["rot"])
    dgain = jnp.sum(dy * xhat, axis=0, keepdims=True)
    dxh = dy * gain
    dx = rinv * (dxh - xhat * (_dot_x2(dxh * xhat, k["seg"]) * (1.0 / ATT_HEAD_DIM)))
    return dx, dgain


def _attn_prep(qkv, cos, sin, gq, gk):
    s = qkv.shape[0]
    tr = _tile(s, 256, 8)

    def body(x_ref, cos_ref, sin_ref, gq_ref, gk_ref, q_ref, kk_ref, vlo_ref, vhi_ref):
        k = _lane_consts()
        cosv, sinv = cos_ref[...], sin_ref[...]
        lane = _iota((tr, LANES), 1)
        for j in range(Q_WIDTH // LANES):
            r, _, _ = _norm_rope(x_ref[:, j * LANES:(j + 1) * LANES], gq_ref[...], cosv, sinv, k)
            q_ref[:, j * LANES:(j + 1) * LANES] = r.astype(BF16)
        for i in range(KV_WIDTH // LANES):
            off = Q_WIDTH + i * LANES
            r, _, _ = _norm_rope(x_ref[:, off:off + LANES], gk_ref[...], cosv, sinv, k)
            rb = r.astype(BF16)
            kk_ref[:, (2 * i) * LANES:(2 * i + 1) * LANES] = _dot(rb, k["dup_lo"]).astype(BF16)
            kk_ref[:, (2 * i + 1) * LANES:(2 * i + 2) * LANES] = _dot(rb, k["dup_hi"]).astype(BF16)
            off = Q_WIDTH + KV_WIDTH + i * LANES
            vb = x_ref[:, off:off + LANES].astype(BF16)
            zero = jnp.zeros_like(vb)
            vlo_ref[:, (2 * i) * LANES:(2 * i + 1) * LANES] = jnp.where(lane < 64, vb, zero)
            vhi_ref[:, (2 * i) * LANES:(2 * i + 1) * LANES] = _dot(vb, k["up"]).astype(BF16)
            vlo_ref[:, (2 * i + 1) * LANES:(2 * i + 2) * LANES] = _dot(vb, k["down"]).astype(BF16)
            vhi_ref[:, (2 * i + 1) * LANES:(2 * i + 2) * LANES] = jnp.where(lane >= 64, vb, zero)

    w = qkv.shape[1]
    row = lambda width: pl.BlockSpec((tr, width), lambda i: (i, 0))
    vec = pl.BlockSpec((1, LANES), lambda i: (0, 0))
    kw = ATT_KV_HEADS * LANES
    return pl.pallas_call(
        body, name="attn_prep", grid=(s // tr,),
        in_specs=[row(w), row(LANES), row(LANES), vec, vec],
        out_specs=[row(Q_WIDTH), row(kw), row(kw), row(kw)],
        out_shape=[jax.ShapeDtypeStruct((s, Q_WIDTH), BF16)] + [jax.ShapeDtypeStruct((s, kw), BF16)] * 3,
        compiler_params=_params("parallel"),
    )(qkv, cos, sin, gq, gk)


def _band_mask(n):
    qi = _iota((ATT_BLOCK, 2 * ATT_BLOCK), 0)
    kj = _iota((ATT_BLOCK, 2 * ATT_BLOCK), 1)
    band = (kj > qi) & (kj <= qi + ATT_BLOCK)
    return band & ((kj >= ATT_BLOCK) | (n > 0))


def _softmax_sink(s, valid, sink):
    s = jnp.where(valid, s, -jnp.inf)
    m = jnp.maximum(jnp.max(s, axis=-1, keepdims=True), sink)
    p = jnp.exp(s - m)
    esink = jnp.exp(sink - m)
    inv = 1.0 / (jnp.sum(p, axis=-1, keepdims=True) + esink)
    return p * inv, esink * inv


def _attn_specs(order):
    if order == "nh":
        cur = lambda n, h: (n, h)
        prev = lambda n, h: (jnp.maximum(n - 1, 0), h)
    else:
        cur = lambda h, n: (n, h)
        prev = lambda h, n: (jnp.maximum(n - 1, 0), h)
    qs = pl.BlockSpec((ATT_BLOCK, ATT_GROUP * ATT_HEAD_DIM), cur)
    kc = pl.BlockSpec((ATT_BLOCK, LANES), cur)
    kp = pl.BlockSpec((ATT_BLOCK, LANES), prev)
    return qs, kc, kp


def _pair_rows(qp):
    lane = _iota((ATT_BLOCK, LANES), 1)
    zero = jnp.zeros_like(qp)
    return jnp.concatenate([jnp.where(lane < 64, qp, zero), jnp.where(lane >= 64, qp, zero)], axis=0)


def _pair_masks(n):
    qi = _iota((2 * ATT_BLOCK, 2 * ATT_BLOCK), 0) & (ATT_BLOCK - 1)
    kj = _iota((2 * ATT_BLOCK, 2 * ATT_BLOCK), 1)
    valid = (kj > qi) & (kj <= qi + ATT_BLOCK) & ((kj >= ATT_BLOCK) | (n > 0))
    return valid, _iota((2 * ATT_BLOCK, 1), 0) >= ATT_BLOCK


def _attn_fwd(q, kk, vlo, vhi, sinks, name="attn_fwd", carry=None):
    s = q.shape[0]
    nb = s // ATT_BLOCK
    scale = ATT_HEAD_DIM ** -0.5

    def body(sink_ref, q_ref, kc_ref, kp_ref, vloc_ref, vlop_ref, vhic_ref, vhip_ref, o_ref):
        n, h = pl.program_id(0), pl.program_id(1)
        valid, upper = _pair_masks(n)
        kw = jnp.concatenate([kp_ref[...], kc_ref[...]], axis=0)
        vcat = jnp.concatenate([vlop_ref[...], vloc_ref[...], vhip_ref[...], vhic_ref[...]], axis=0)
        for jp in range(ATT_GROUP // 2):
            q2 = _pair_rows(q_ref[:, jp * LANES:(jp + 1) * LANES])
            sink = jnp.where(upper, sink_ref[h * ATT_GROUP + 2 * jp + 1], sink_ref[h * ATT_GROUP + 2 * jp])
            probs, _ = _softmax_sink(_dot_nt(q2, kw) * scale, valid, sink)
            pcat = jnp.concatenate([probs[:ATT_BLOCK], probs[ATT_BLOCK:]], axis=1).astype(BF16)
            o_ref[:, jp * LANES:(jp + 1) * LANES] = _dot(pcat, vcat).astype(BF16)

    qs, kc, kp = _attn_specs("nh")
    res = _call(body, carry, name, (nb, ATT_KV_HEADS),
                [pl.BlockSpec(memory_space=pltpu.SMEM), qs, kc, kp, kc, kp, kc, kp], [qs],
                [jax.ShapeDtypeStruct((s, Q_WIDTH), BF16)], [], ("parallel", "parallel"),
                (sinks, q, kk, kk, vlo, vlo, vhi, vhi))
    return res[0] if carry is None else (res[0], res[1:])


def _attn_bwd(q, kk, vlo, vhi, sinks, do, name="attn_bwd", carry=None):
    s = q.shape[0]
    nb = s // ATT_BLOCK
    scale = ATT_HEAD_DIM ** -0.5

    def body(sink_ref, q_ref, kc_ref, kp_ref, vloc_ref, vlop_ref, vhic_ref, vhip_ref, do_ref,
             dq_ref, dkc_ref, dkp_ref, dvloc_ref, dvlop_ref, dvhic_ref, dvhip_ref, dsink_ref, dkk_acc, dv_acc):
        h, n = pl.program_id(0), pl.program_id(1)
        valid, upper = _pair_masks(n)
        kw = jnp.concatenate([kp_ref[...], kc_ref[...]], axis=0)
        vcat = jnp.concatenate([vlop_ref[...], vloc_ref[...], vhip_ref[...], vhic_ref[...]], axis=0)
        lane = _iota((ATT_BLOCK, LANES), 1)
        sub = _iota((ATT_GROUP, LANES), 0)
        dsink = jnp.zeros((ATT_GROUP, LANES), F32)
        for jp in range(ATT_GROUP // 2):
            q2 = _pair_rows(q_ref[:, jp * LANES:(jp + 1) * LANES])
            dop = do_ref[:, jp * LANES:(jp + 1) * LANES]
            sink = jnp.where(upper, sink_ref[h * ATT_GROUP + 2 * jp + 1], sink_ref[h * ATT_GROUP + 2 * jp])
            probs, psink = _softmax_sink(_dot_nt(q2, kw) * scale, valid, sink)
            pcat = jnp.concatenate([probs[:ATT_BLOCK], probs[ATT_BLOCK:]], axis=1).astype(BF16)
            dpc = _dot_nt(dop, vcat)
            dprobs = jnp.concatenate([dpc[:, :2 * ATT_BLOCK], dpc[:, 2 * ATT_BLOCK:]], axis=0)
            dv_part = _dot_tn(pcat, dop)
            delta = jnp.sum(probs * dprobs, axis=-1, keepdims=True)
            ds = (probs * (dprobs - delta) * scale).astype(BF16)
            sd = psink * delta
            dsink = (dsink + jnp.where(sub == 2 * jp, -jnp.sum(sd[:ATT_BLOCK]), 0.0)
                     + jnp.where(sub == 2 * jp + 1, -jnp.sum(sd[ATT_BLOCK:]), 0.0))
            dq2 = _dot(ds, kw)
            dq_ref[:, jp * LANES:(jp + 1) * LANES] = jnp.where(lane < 64, dq2[:ATT_BLOCK], dq2[ATT_BLOCK:])
            dkk_part = _dot_tn(ds, q2)
            if jp == 0:
                dkk_acc[...], dv_acc[...] = dkk_part, dv_part
            else:
                dkk_acc[...] += dkk_part
                dv_acc[...] += dv_part
        blk = ATT_BLOCK
        dkp_ref[...], dkc_ref[...] = dkk_acc[:blk], dkk_acc[blk:]
        dvlop_ref[...], dvloc_ref[...] = dv_acc[:blk], dv_acc[blk:2 * blk]
        dvhip_ref[...], dvhic_ref[...] = dv_acc[2 * blk:3 * blk], dv_acc[3 * blk:]

        @pl.when(n == 0)
        def _():
            dsink_ref[0] = dsink

        @pl.when(n > 0)
        def _():
            dsink_ref[0] += dsink

    qs, kc, kp = _attn_specs("hn")
    kw_shape = jax.ShapeDtypeStruct((s, ATT_KV_HEADS * LANES), F32)
    res = _call(body, carry, name, (ATT_KV_HEADS, nb),
                [pl.BlockSpec(memory_space=pltpu.SMEM), qs, kc, kp, kc, kp, kc, kp, qs],
                [qs] + [kc] * 6 + [pl.BlockSpec((1, ATT_GROUP, LANES), lambda h, n: (h, 0, 0))],
                [jax.ShapeDtypeStruct((s, Q_WIDTH), F32)] + [kw_shape] * 6
                + [jax.ShapeDtypeStruct((ATT_KV_HEADS, ATT_GROUP, LANES), F32)],
                [pltpu.VMEM((2 * ATT_BLOCK, LANES), F32), pltpu.VMEM((4 * ATT_BLOCK, LANES), F32)],
                ("parallel", "arbitrary"), (sinks, q, kk, kk, vlo, vlo, vhi, vhi, do))
    return res if carry is None else (res[:8], res[8:])


def _attn_prep_bwd(qkv, cos, sin, gq, gk, dq, dks, dvlos, dvhis):
    s, w = qkv.shape
    tr = ATT_BLOCK
    nb = s // tr

    def body(x_ref, cos_ref, sin_ref, gq_ref, gk_ref, dq_ref, dkc_ref, dkn_ref, dvloc_ref, dvlon_ref,
             dvhic_ref, dvhin_ref, dx_ref, dgq_ref, dgk_ref):
        n = pl.program_id(0)
        k = _lane_consts()
        cosv, sinv = cos_ref[...], sin_ref[...]
        nxt = jnp.where(n < nb - 1, 1.0, 0.0)
        lane = _iota((tr, LANES), 1)
        dgq = jnp.zeros((1, LANES), F32)
        dgk = jnp.zeros((1, LANES), F32)
        for j in range(Q_WIDTH // LANES):
            sl = slice(j * LANES, (j + 1) * LANES)
            _, xhat, rinv = _norm_rope(x_ref[:, sl], gq_ref[...], cosv, sinv, k)
            dx, dg = _norm_rope_bwd(dq_ref[:, sl], xhat, rinv, gq_ref[...], cosv, sinv, k)
            dx_ref[:, sl] = dx.astype(BF16)
            dgq = dgq + dg
        for i in range(KV_WIDTH // LANES):
            a, b = slice(2 * i * LANES, (2 * i + 1) * LANES), slice((2 * i + 1) * LANES, (2 * i + 2) * LANES)
            dr = (_dot_x2(dkc_ref[:, a] + nxt * dkn_ref[:, a], k["fold_lo"])
                  + _dot_x2(dkc_ref[:, b] + nxt * dkn_ref[:, b], k["fold_hi"]))
            sl = slice(Q_WIDTH + i * LANES, Q_WIDTH + (i + 1) * LANES)
            _, xhat, rinv = _norm_rope(x_ref[:, sl], gk_ref[...], cosv, sinv, k)
            dx, dg = _norm_rope_bwd(dr, xhat, rinv, gk_ref[...], cosv, sinv, k)
            dx_ref[:, sl] = dx.astype(BF16)
            dgk = dgk + dg
            ta = jnp.where(lane < 64, dvloc_ref[:, a] + nxt * dvlon_ref[:, a], dvhic_ref[:, a] + nxt * dvhin_ref[:, a])
            tb = jnp.where(lane < 64, dvloc_ref[:, b] + nxt * dvlon_ref[:, b], dvhic_ref[:, b] + nxt * dvhin_ref[:, b])
            sl = slice(Q_WIDTH + KV_WIDTH + i * LANES, Q_WIDTH + KV_WIDTH + (i + 1) * LANES)
            dx_ref[:, sl] = (_dot_x2(ta, k["fold_lo"]) + _dot_x2(tb, k["fold_hi"])).astype(BF16)

        @pl.when(n == 0)
        def _():
            dgq_ref[...] = dgq
            dgk_ref[...] = dgk

        @pl.when(n > 0)
        def _():
            dgq_ref[...] += dgq
            dgk_ref[...] += dgk

    row = lambda width: pl.BlockSpec((tr, width), lambda i: (i, 0))
    nxt_row = pl.BlockSpec((tr, ATT_KV_HEADS * LANES), lambda i: (jnp.minimum(i + 1, nb - 1), 0))
    vec = pl.BlockSpec((1, LANES), lambda i: (0, 0))
    kw = ATT_KV_HEADS * LANES
    return pl.pallas_call(
        body, name="attn_prep_bwd", grid=(nb,),
        in_specs=[row(w), row(LANES), row(LANES), vec, vec, row(Q_WIDTH),
                  row(kw), nxt_row, row(kw), nxt_row, row(kw), nxt_row],
        out_specs=[row(w), vec, vec],
        out_shape=[jax.ShapeDtypeStruct((s, w), BF16), jax.ShapeDtypeStruct((1, LANES), F32),
                   jax.ShapeDtypeStruct((1, LANES), F32)],
        compiler_params=_params("arbitrary"),
    )(qkv, cos, sin, gq, gk, dq, dks[0], dks[1], dvlos[0], dvlos[1], dvhis[0], dvhis[1])


CONV_HALO = 8
CONV_TC = 512
XBC_OFF = SSM_D_INNER // CONV_TC
DT_OFF = SSM_D_INNER + SSM_CONV_DIM


def _conv_pre(ext, w_ref, b_ref, ts):
    pre = b_ref[...] + w_ref[SSM_CONV - 1:SSM_CONV, :] * ext[CONV_HALO:]
    for kk in range(SSM_CONV - 1):
        pre = pre + w_ref[kk:kk + 1, :] * pltpu.roll(ext, SSM_CONV - 1 - kk, 0)[CONV_HALO:]
    return pre


def _conv_specs(ts):
    tc = CONV_TC
    src = pl.BlockSpec((ts, tc), lambda j, i: (i, XBC_OFF + j))
    halo = pl.BlockSpec((CONV_HALO, tc), lambda j, i: (jnp.maximum(i * (ts // CONV_HALO) - 1, 0), XBC_OFF + j))
    blk = pl.BlockSpec((ts, tc), lambda j, i: (i, j))
    wspec = pl.BlockSpec((SSM_CONV, tc), lambda j, i: (0, j))
    bspec = pl.BlockSpec((1, tc), lambda j, i: (0, j))
    return src, halo, blk, wspec, bspec


def _conv_fwd(zx, w, b):
    s, c = zx.shape[0], SSM_CONV_DIM
    ts = _tile(s, 512, 8)

    def body(u_ref, halo_ref, w_ref, b_ref, o_ref):
        halo = jnp.where(pl.program_id(1) > 0, halo_ref[...], 0.0)
        pre = _conv_pre(jnp.concatenate([halo, u_ref[...]], axis=0), w_ref, b_ref, ts)
        o_ref[...] = pre * _sigmoid(pre)

    src, halo, blk, wspec, bspec = _conv_specs(ts)
    return pl.pallas_call(
        body, name="conv_fwd", grid=(c // CONV_TC, s // ts),
        in_specs=[src, halo, wspec, bspec], out_specs=blk, out_shape=jax.ShapeDtypeStruct((s, c), F32),
        compiler_params=_params("parallel", "parallel"),
    )(zx, zx, w, b)


def _conv_bwd_pre(zx, w, b, dxs, dbm, dcm):
    s, c = zx.shape[0], SSM_CONV_DIM
    ts = _tile(s, 512, 8)
    nx, nb = dxs.shape[1] // CONV_TC, dbm.shape[1] // CONV_TC

    def body(u_ref, halo_ref, w_ref, b_ref, dx_ref, dbm_ref, dcm_ref, dpre_ref, dw_ref, db_ref):
        j, i = pl.program_id(0), pl.program_id(1)
        halo = jnp.where(i > 0, halo_ref[...], 0.0)
        ext = jnp.concatenate([halo, u_ref[...]], axis=0)
        pre = _conv_pre(ext, w_ref, b_ref, ts)
        sg = _sigmoid(pre)
        da = jnp.where(j < nx, dx_ref[...], jnp.where(j < nx + nb, dbm_ref[...], dcm_ref[...]))
        dpre = da * sg * (1.0 + pre * (1.0 - sg))
        dpre_ref[...] = dpre
        rows = [jnp.sum(dpre * pltpu.roll(ext, SSM_CONV - 1 - kk, 0)[CONV_HALO:], axis=0, keepdims=True)
                for kk in range(SSM_CONV - 1)]
        rows.append(jnp.sum(dpre * ext[CONV_HALO:], axis=0, keepdims=True))
        dwp = jnp.concatenate(rows, axis=0)
        dbp = jnp.sum(dpre, axis=0, keepdims=True)

        @pl.when(i == 0)
        def _():
            dw_ref[...] = dwp
            db_ref[...] = dbp

        @pl.when(i > 0)
        def _():
            dw_ref[...] += dwp
            db_ref[...] += dbp

    src, halo, blk, wspec, bspec = _conv_specs(ts)

    def part(lo, n):
        return pl.BlockSpec((ts, CONV_TC), lambda j, i: (jnp.where((j >= lo) & (j < lo + n), i, 0),
                                                         jnp.clip(j - lo, 0, n - 1)))

    return pl.pallas_call(
        body, name="conv_bwd_pre", grid=(c // CONV_TC, s // ts),
        in_specs=[src, halo, wspec, bspec, part(0, nx), part(nx, nb), part(nx + nb, nb)],
        out_specs=[blk, wspec, bspec],
        out_shape=[jax.ShapeDtypeStruct((s, c), F32), jax.ShapeDtypeStruct((SSM_CONV, c), F32),
                   jax.ShapeDtypeStruct((1, c), F32)],
        compiler_params=_params("parallel", "arbitrary"),
    )(zx, zx, w, b, dxs, dbm, dcm)


def _conv_bwd_in(dpre, w, dzx):
    s, c = dpre.shape
    ts, tc = _tile(s, 512, 8), CONV_TC
    ns = s // ts

    def body(d_ref, halo_ref, w_ref, dzx_ref, o_ref):
        del dzx_ref
        halo = jnp.where(pl.program_id(1) < ns - 1, halo_ref[...], 0.0)
        ext = jnp.concatenate([d_ref[...], halo], axis=0)
        du = w_ref[SSM_CONV - 1:SSM_CONV, :] * ext[:ts]
        for kk in range(SSM_CONV - 1):
            du = du + w_ref[kk:kk + 1, :] * pltpu.roll(ext, ts + CONV_HALO - (SSM_CONV - 1 - kk), 0)[:ts]
        o_ref[...] = du.astype(BF16)

    blk = pl.BlockSpec((ts, tc), lambda j, i: (i, j))
    halo = pl.BlockSpec((CONV_HALO, tc), lambda j, i: (jnp.minimum((i + 1) * (ts // CONV_HALO), s // CONV_HALO - 1), j))
    return pl.pallas_call(
        body, name="conv_bwd_in", grid=(c // tc, ns),
        in_specs=[blk, halo, pl.BlockSpec((SSM_CONV, tc), lambda j, i: (0, j)), ANY],
        out_specs=pl.BlockSpec((ts, tc), lambda j, i: (i, XBC_OFF + j)),
        out_shape=jax.ShapeDtypeStruct(dzx.shape, BF16), input_output_aliases={3: 0},
        compiler_params=_params("parallel", "parallel"),
    )(dpre, dpre, w, dzx)


def _ssd_common(dt_ref, dtt_ref, bias_ref, biast_ref, alog_ref, alogt_ref):
    ln = SSM_CHUNK
    raw, rawt = dt_ref[0] + bias_ref[0], dtt_ref[0] + biast_ref[0]
    dt, dtt = _softplus(raw), _softplus(rawt)
    a, at = -jnp.exp(alog_ref[0]), -jnp.exp(alogt_ref[0])
    tri = jnp.where(_iota((ln, ln), 0) >= _iota((ln, ln), 1), 1.0, 0.0).astype(BF16)
    return dict(raw=raw, rawt=rawt, dt=dt, dtt=dtt, a=a, at=at, tri=tri,
                acum=_xdot(tri, dt * a), acumt=_dot_x_nt(dtt * at, tri))


def _ssd_specs(nc, rev):
    cidx = (lambda c: nc - 1 - c) if rev else (lambda c: c)
    ln = SSM_CHUNK
    xs = pl.BlockSpec((ln, SSM_GN), lambda g, c: (cidx(c), g))
    bs = pl.BlockSpec((ln, SSM_STATE), lambda g, c: (cidx(c), SSM_D_INNER // SSM_STATE + g))
    cs = pl.BlockSpec((ln, SSM_STATE), lambda g, c: (cidx(c), SSM_D_INNER // SSM_STATE + SSM_GROUPS + g))
    dt = pl.BlockSpec((1, ln, SSM_HPG), lambda g, c: (g, cidx(c), 0))
    dtt = pl.BlockSpec((1, SSM_HPG, ln), lambda g, c: (g, 0, cidx(c)))
    row = pl.BlockSpec((1, 1, SSM_HPG), lambda g, c: (g, 0, 0))
    col = pl.BlockSpec((1, SSM_HPG, 1), lambda g, c: (g, 0, 0))
    st = pl.BlockSpec((None, None, SSM_GN, SSM_STATE), lambda g, c: (cidx(c), g, 0, 0))
    return xs, bs, cs, dt, dtt, row, col, st


def _head_expand():
    return jnp.where((_iota((SSM_HPG, SSM_GN), 1) >> 6) == _iota((SSM_HPG, SSM_GN), 0), 1.0, 0.0).astype(BF16)


def _head_expand_t():
    return jnp.where((_iota((SSM_GN, SSM_HPG), 0) >> 6) == _iota((SSM_GN, SSM_HPG), 1), 1.0, 0.0).astype(BF16)


def _dot_x_tn(x, m):
    hi, mid, lo = _split3(x)
    return _dot_tn(hi, m) + _dot_tn(mid, m) + _dot_tn(lo, m)


def _ssd_fwd(xbc, dt_g, dt_gt, bias_r, bias_c, alog_r, alog_c, d_r):
    s = xbc.shape[0]
    ln = SSM_CHUNK
    nc = s // ln

    def body(x_ref, b_ref, c_ref, dt_ref, dtt_ref, bias_ref, biast_ref, alog_ref, alogt_ref, d_ref,
             y_ref, st_ref, state):
        @pl.when(pl.program_id(1) == 0)
        def _():
            state[...] = jnp.zeros_like(state)

        cm = _ssd_common(dt_ref, dtt_ref, bias_ref, biast_ref, alog_ref, alogt_ref)
        acum, acumt = cm["acum"], cm["acumt"]
        ex = _head_expand()
        acum_x = _dot_x(acum, ex)
        xv = x_ref[...]
        xdt = xv * _dot_x(cm["dt"], ex)
        xdtb = xdt.astype(BF16)
        bb, cb = b_ref[...].astype(BF16), c_ref[...].astype(BF16)
        cbm = _dot_nt(cb, bb)
        causal = _iota((ln, ln), 0) >= _iota((ln, ln), 1)
        s2 = state[...]
        st_ref[...] = s2
        for r in range(SSM_HPG):
            sl = slice(r * SSM_P, (r + 1) * SSM_P)
            decay = jnp.exp(jnp.where(causal, acum[:, r:r + 1] - acumt[r:r + 1, :], -jnp.inf))
            y_ref[:, sl] = _dot((cbm * decay).astype(BF16), xdtb[:, sl])
        y_ref[...] = (y_ref[...] + _dot_nt(cb, s2.astype(BF16)) * jnp.exp(acum_x) + _dot_x(d_ref[0], ex) * xv)
        last_x = acum_x[ln - 1:ln, :]
        elast = jnp.exp(_xdot(_head_expand_t(), acumt[:, ln - 1:ln]))
        state[...] = s2 * elast + _dot_tn((xdt * jnp.exp(last_x - acum_x)).astype(BF16), bb)

    xs, bs, cs, dts, dtts, row, col, st = _ssd_specs(nc, False)
    return pl.pallas_call(
        body, name="ssd_fwd", grid=(SSM_GROUPS, nc),
        in_specs=[xs, bs, cs, dts, dtts, row, col, row, col, row],
        out_specs=[xs, st],
        out_shape=[jax.ShapeDtypeStruct((s, SSM_D_INNER), F32),
                   jax.ShapeDtypeStruct((nc, SSM_GROUPS, SSM_GN, SSM_STATE), F32)],
        scratch_shapes=[pltpu.VMEM((SSM_GN, SSM_STATE), F32)],
        compiler_params=_params("parallel", "arbitrary"),
    )(xbc, xbc, xbc, dt_g, dt_gt, bias_r, bias_c, alog_r, alog_c, d_r)


def _ssd_bwd(xbc, dt_g, dt_gt, bias_r, bias_c, alog_r, alog_c, d_r, states, dy):
    s = xbc.shape[0]
    ln = SSM_CHUNK
    nc = s // ln

    def body(x_ref, b_ref, c_ref, dt_ref, dtt_ref, bias_ref, biast_ref, alog_ref, alogt_ref, d_ref,
             st_ref, dy_ref, dx_ref, db_ref, dc_ref, ddt_ref, ddtt_ref, dbias_ref, dbiast_ref,
             dalog_ref, dalogt_ref, dd_ref, dstate):
        step = pl.program_id(1)

        @pl.when(step == 0)
        def _():
            dstate[...] = jnp.zeros_like(dstate)

        cm = _ssd_common(dt_ref, dtt_ref, bias_ref, biast_ref, alog_ref, alogt_ref)
        dt, acum, acumt = cm["dt"], cm["acum"], cm["acumt"]
        ex, ext = _head_expand(), _head_expand_t()
        dt_x, acum_x = _dot_x(dt, ex), _dot_x(acum, ex)
        eac_x, to_end_x = jnp.exp(acum_x), jnp.exp(acum_x[ln - 1:ln, :] - acum_x)
        xv, dyv = x_ref[...], dy_ref[...]
        xdt = xv * dt_x
        xdtb, dyb = xdt.astype(BF16), dyv.astype(BF16)
        dyeb = (dyv * eac_x).astype(BF16)
        bb, cb = b_ref[...].astype(BF16), c_ref[...].astype(BF16)
        cbm = _dot_nt(cb, bb)
        s2, ds2 = st_ref[...], dstate[...]
        s2b, ds2b = s2.astype(BF16), ds2.astype(BF16)
        dxdt_state = _dot_nt(bb, ds2b) * to_end_x
        yoff = _dot_nt(cb, s2b) * eac_x
        dc_acc = _dot(dyeb, s2b)
        db_acc = _dot((xdt * to_end_x).astype(BF16), ds2b)
        f_rows = _dot_x2_nt(xdt * dxdt_state, ex)
        elast = jnp.exp(acum[ln - 1:ln, :])
        dlast = (jnp.sum(f_rows, axis=0, keepdims=True)
                 + elast * jnp.sum(_dot_x_tn(ds2 * s2, ext), axis=0, keepdims=True))
        is_last = _iota((ln, 1), 0) == ln - 1
        dac_rows = _dot_x2_nt(dyv * yoff, ex) - f_rows + jnp.where(is_last, dlast, 0.0)
        dstate[...] = ds2 * jnp.exp(_xdot(ext, acumt[:, ln - 1:ln])) + _dot_tn(dyeb, cb)
        causal = _iota((ln, ln), 0) >= _iota((ln, ln), 1)
        lane8 = _iota((ln, SSM_HPG), 1)
        sub8 = _iota((SSM_HPG, ln), 0)
        dcb = jnp.zeros((ln, ln), F32)
        dac_cols = jnp.zeros((SSM_HPG, ln), F32)
        for r in range(SSM_HPG):
            sl = slice(r * SSM_P, (r + 1) * SSM_P)
            decay = jnp.exp(jnp.where(causal, acum[:, r:r + 1] - acumt[r:r + 1, :], -jnp.inf))
            dx_ref[:, sl] = _dot_tn((cbm * decay).astype(BF16), dyb[:, sl])
            dcb_r = _dot_nt(dyb[:, sl], xdtb[:, sl]) * decay
            dcb = dcb + dcb_r
            e = dcb_r * cbm
            dac_rows = dac_rows + jnp.where(lane8 == r, jnp.sum(e, axis=-1, keepdims=True), 0.0)
            dac_cols = dac_cols + jnp.where(sub8 == r, jnp.sum(e, axis=0, keepdims=True), 0.0)
        dxdt = dx_ref[...] + dxdt_state
        ddt_all = _dot_x2_nt(dxdt * xv, ex)
        dd_all = jnp.sum(_dot_x2_nt(dyv * xv, ex), axis=0, keepdims=True)
        dx_ref[...] = dxdt * dt_x + _dot_x(d_ref[0], ex) * dyv
        dcbb = dcb.astype(BF16)
        dc_ref[...] = dc_acc + _dot(dcbb, bb)
        db_ref[...] = db_acc + _dot_tn(dcbb, cb)
        triu = jnp.where(_iota((ln, ln), 0) <= _iota((ln, ln), 1), 1.0, 0.0).astype(BF16)
        g_rows = _xdot(triu, dac_rows)
        g_cols = _dot_x(dac_cols, cm["tri"])
        d_rows = (ddt_all + g_rows * cm["a"]) * _sigmoid(cm["raw"])
        d_cols = -(g_cols * cm["at"]) * _sigmoid(cm["rawt"])
        ddt_ref[0] = d_rows
        ddtt_ref[0] = d_cols
        parts = (jnp.sum(d_rows, axis=0, keepdims=True), jnp.sum(d_cols, axis=1, keepdims=True),
                 jnp.sum(g_rows * dt, axis=0, keepdims=True) * cm["a"],
                 -jnp.sum(g_cols * cm["dtt"], axis=1, keepdims=True) * cm["at"], dd_all)
        outs = (dbias_ref, dbiast_ref, dalog_ref, dalogt_ref, dd_ref)

        @pl.when(step == 0)
        def _():
            for o_ref, p in zip(outs, parts):
                o_ref[0] = p

        @pl.when(step > 0)
        def _():
            for o_ref, p in zip(outs, parts):
                o_ref[0] += p

    xs, bs, cs, dts, dtts, row, col, st = _ssd_specs(nc, True)
    grp = pl.BlockSpec((ln, SSM_STATE), lambda g, c: (nc - 1 - c, g))
    rows = jax.ShapeDtypeStruct((SSM_GROUPS, 1, SSM_HPG), F32)
    cols = jax.ShapeDtypeStruct((SSM_GROUPS, SSM_HPG, 1), F32)
    return pl.pallas_call(
        body, name="ssd_bwd", grid=(SSM_GROUPS, nc),
        in_specs=[xs, bs, cs, dts, dtts, row, col, row, col, row, st, xs],
        out_specs=[xs, grp, grp, dts, dtts, row, col, row, col, row],
        out_shape=[jax.ShapeDtypeStruct((s, SSM_D_INNER), F32),
                   jax.ShapeDtypeStruct((s, SSM_GROUPS * SSM_STATE), F32),
                   jax.ShapeDtypeStruct((s, SSM_GROUPS * SSM_STATE), F32),
                   jax.ShapeDtypeStruct((SSM_GROUPS, s, SSM_HPG), F32),
                   jax.ShapeDtypeStruct((SSM_GROUPS, SSM_HPG, s), F32), rows, cols, rows, cols, rows],
        scratch_shapes=[pltpu.VMEM((SSM_GN, SSM_STATE), F32)],
        compiler_params=_params("parallel", "arbitrary"),
    )(xbc, xbc, xbc, dt_g, dt_gt, bias_r, bias_c, alog_r, alog_c, d_r, states, dy)


def _gate_norm_fwd(y, zx, g):
    s = y.shape[0]
    ts = _tile(s, 512, 8)

    def body(y_ref, z_ref, g_ref, o_ref):
        zv = z_ref[...]
        yg = y_ref[...] * (zv * _sigmoid(zv))
        r = lax.rsqrt(jnp.mean(yg * yg, axis=-1, keepdims=True) + EPS)
        o_ref[...] = (yg * r * g_ref[...]).astype(BF16)

    blk = pl.BlockSpec((ts, SSM_GN), lambda j, i: (i, j))
    vec = pl.BlockSpec((1, SSM_GN), lambda j, i: (0, j))
    return pl.pallas_call(
        body, name="gate_norm_fwd", grid=(SSM_GROUPS, s // ts), in_specs=[blk, blk, vec], out_specs=blk,
        out_shape=jax.ShapeDtypeStruct((s, SSM_D_INNER), BF16), compiler_params=_params("parallel", "parallel"),
    )(y, zx, g)


def _gate_norm_bwd(y, zx, g, dout):
    s = y.shape[0]
    ts = _tile(s, 512, 8)

    def body(y_ref, z_ref, g_ref, do_ref, dy_ref, dz_ref, dg_ref):
        yv, zv, dov = y_ref[...], z_ref[...], do_ref[...].astype(F32)
        sg = _sigmoid(zv)
        silu = zv * sg
        yg = yv * silu
        r = lax.rsqrt(jnp.mean(yg * yg, axis=-1, keepdims=True) + EPS)
        ygn = yg * r
        part = jnp.sum(dov * ygn, axis=0, keepdims=True)

        @pl.when(pl.program_id(1) == 0)
        def _():
            dg_ref[...] = part

        @pl.when(pl.program_id(1) > 0)
        def _():
            dg_ref[...] += part

        dn = dov * g_ref[...]
        dyg = r * (dn - ygn * jnp.mean(dn * ygn, axis=-1, keepdims=True))
        dy_ref[...] = dyg * silu
        dz_ref[...] = (dyg * yv * sg * (1.0 + zv * (1.0 - sg))).astype(BF16)

    blk = pl.BlockSpec((ts, SSM_GN), lambda j, i: (i, j))
    vec = pl.BlockSpec((1, SSM_GN), lambda j, i: (0, j))
    return pl.pallas_call(
        body, name="gate_norm_bwd", grid=(SSM_GROUPS, s // ts), in_specs=[blk, blk, vec, blk],
        out_specs=[blk, blk, vec],
        out_shape=[jax.ShapeDtypeStruct((s, SSM_D_INNER), F32), jax.ShapeDtypeStruct((s, SSM_IN_PAD), BF16),
                   jax.ShapeDtypeStruct((1, SSM_D_INNER), F32)],
        compiler_params=_params("parallel", "arbitrary"),
    )(y, zx, g, dout)


def _rope_tables(positions):
    inv_freq = ROPE_THETA ** (-jnp.arange(0, ATT_HEAD_DIM, 2, dtype=F32) / ATT_HEAD_DIM)
    ang = positions.astype(F32)[:, None] * inv_freq
    return jnp.tile(jnp.cos(ang), (1, 4)), jnp.tile(jnp.sin(ang), (1, 4))


def _group_views(v):
    return v.reshape(SSM_GROUPS, 1, SSM_HPG), v.reshape(SSM_GROUPS, SSM_HPG, 1)


def _ffn_fwd(run, x, norm_g, wg, wu, wd, tag):
    h = _rms_fwd(x, norm_g, f"ffn_norm_{tag}")
    g = run(f"ffn_gate_{tag}", _mm, h, wg, "nn", b_cols=True, out_dtype=BF16)

    def act(uv, gv):
        gv = gv.astype(F32)
        return uv, gv * _sigmoid(gv) * uv

    u, a = run(f"ffn_up_{tag}", _mm, h, wu, "nn", b_cols=True, rows=FUSED_ROWS, fuse=(act, [g], [BF16, BF16]))
    return run(f"ffn_down_{tag}", _mm, a, wd, "nn", add=x), (h, g, u, a)


def _ffn_bwd(run, mats, x, norm_g, wg, wu, wd, saved, dout, tag):
    h, g, u, a = saved

    def act_bwd(da, gv, uv):
        gv, uv = gv.astype(F32), uv.astype(F32)
        sg = _sigmoid(gv)
        return da * uv * sg * (1.0 + gv * (1.0 - sg)), da * (gv * sg)

    dg, du = run(f"ffn_down_dx_{tag}", _mm, dout, wd, "nt", rows=FUSED_ROWS,
                 fuse=(act_bwd, [g, u], [BF16, BF16]))
    dwd = run(f"ffn_down_dw_{tag}", _mm, a, dout, "tn", out_dtype=BF16)
    mats[("ffn_w_down", tag)] = dwd.reshape(N_SHARDS, dwd.shape[0] // N_SHARDS, dwd.shape[1])
    mats[("ffn_w_gate", tag)] = run(f"ffn_gate_dw_{tag}", _mm, h, dg, "tn", out_dtype=BF16, out_cols=True)
    mats[("ffn_w_up", tag)] = run(f"ffn_up_dw_{tag}", _mm, h, du, "tn", out_dtype=BF16, out_cols=True)
    dh = run(f"ffn_gate_dx_{tag}", _mm, dg, wg, "nt", b_cols=True)
    dh = run(f"ffn_up_dx_{tag}", _mm, du, wu, "nt", add=dh, b_cols=True)
    return _rms_bwd(x, norm_g, dh, dout, f"ffn_norm_bwd_{tag}")


class _Hook:
    def __init__(self, make, done):
        self.make, self.done = make, done


class _SemView:
    def __init__(self, sems, off):
        self.sems, self.off, self.at = sems, off, self

    def __getitem__(self, k):
        return self.sems.at[self.off + k]


def _both(h1, h2):
    split = {}

    def make():
        a, b = h1.make(), h2.make()
        na_in, na_out, na_sems = len(a["arrays"]), len(a["out_shapes"]), a["n_sems"]
        split["n"] = na_out

        def build(cin, cout, send_sems, recv_sems):
            return (a["build"](cin[:na_in], cout[:na_out], send_sems, recv_sems)
                    + b["build"](cin[na_in:], cout[na_out:], _SemView(send_sems, na_sems), _SemView(recv_sems, na_sems)))

        aliases = dict(a.get("aliases", {}))
        aliases.update({na_in + i: na_out + o for i, o in b.get("aliases", {}).items()})
        return dict(build=build, arrays=list(a["arrays"]) + list(b["arrays"]),
                    out_shapes=list(a["out_shapes"]) + list(b["out_shapes"]), n_sems=na_sems + b["n_sems"],
                    aliases=aliases)

    def done(res):
        h1.done(res[:split["n"]])
        h2.done(res[split["n"]:])

    return _Hook(make, done)


def _local_step(x, positions, target, w, hooks=None, mats=None):
    hooks = {} if hooks is None else hooks
    mats = {} if mats is None else mats

    def run(name, fn, *args, **kw):
        hook = hooks.get(name)
        if hook is None:
            return fn(*args, name=name, **kw)
        res, carried = fn(*args, name=name, carry=hook.make(), **kw)
        hook.done(carried)
        return res

    cos, sin = _rope_tables(positions)
    row = lambda v: v.reshape(1, -1)
    gq, gk = jnp.tile(row(w["attn_q_norm"]), (1, 2)), jnp.tile(row(w["attn_k_norm"]), (1, 2))
    sinks = w["attn_sinks"].reshape(-1)
    s = x.shape[0]
    row_stack = lambda g: g.reshape(N_SHARDS, g.shape[0] // N_SHARDS, g.shape[1])

    h0 = _rms_fwd(x, row(w["mixer_norm"][0]), "mixer_norm_0")
    qkv = run("attn_qkv", _mm, h0, w["attn_w_qkv"], "nn", b_cols=True)
    q, kk, vlo, vhi = _attn_prep(qkv, cos, sin, gq, gk)
    o = run("attn_fwd", _attn_fwd, q, kk, vlo, vhi, sinks)
    x1 = run("attn_out", _mm, o, w["attn_w_o"], "nn", add=x)
    ffn_w = lambda l: (row(w["ffn_norm"][l]), w["ffn_w_gate"][l], w["ffn_w_up"][l], w["ffn_w_down"][l])
    x2, ffn0 = _ffn_fwd(run, x1, *ffn_w(0), 0)

    h2 = _rms_fwd(x2, row(w["mixer_norm"][1]), "mixer_norm_1")
    zx = run("ssm_in", _mm, h2, w["ssm_w_in"], "nn")
    dt_g = zx[:, DT_OFF:DT_OFF + SSM_HEADS].reshape(s, SSM_GROUPS, SSM_HPG).transpose(1, 0, 2)
    dt_gt = dt_g.transpose(0, 2, 1)
    bias_r, bias_c = _group_views(w["ssm_dt_bias"].reshape(-1))
    alog_r, alog_c = _group_views(w["ssm_a_log"].reshape(-1))
    d_r, _ = _group_views(w["ssm_d"].reshape(-1))
    xbc = _conv_fwd(zx, w["ssm_conv_w"], row(w["ssm_conv_b"]))
    ssd_args = (xbc, dt_g, dt_gt, bias_r, bias_c, alog_r, alog_c, d_r)
    y, states = _ssd_fwd(*ssd_args)
    yn = _gate_norm_fwd(y, zx, row(w["ssm_norm"]))
    x3 = run("ssm_out", _mm, yn, w["ssm_w_out"], "nn", add=x2)
    x4, ffn1 = _ffn_fwd(run, x3, *ffn_w(1), 1)

    loss_row, dx4 = _loss_fwd_bwd(x4, target)

    dx3, dfn1 = _ffn_bwd(run, mats, x3, *ffn_w(1), ffn1, dx4, 1)
    dyn = run("ssm_out_dx", _mm, dx3, w["ssm_w_out"], "nt")
    mats[("ssm_w_out", 0)] = row_stack(run("ssm_out_dw", _mm, yn, dx3, "tn", out_dtype=BF16))
    dy, dzx, dssm_norm = _gate_norm_bwd(y, zx, row(w["ssm_norm"]), dyn)
    dxs, db, dc, ddt_g, ddt_gt, dbias, dbias_t, dalog, dalog_t, dd = _ssd_bwd(*ssd_args, states, dy)
    ddt_g = ddt_g + ddt_gt.transpose(0, 2, 1)
    dpre, dconv_w, dconv_b = _conv_bwd_pre(zx, w["ssm_conv_w"], row(w["ssm_conv_b"]), dxs, db, dc)
    dzx = _conv_bwd_in(dpre, w["ssm_conv_w"], dzx)
    ddt_pad = jnp.pad(ddt_g.transpose(1, 0, 2).reshape(s, SSM_HEADS), ((0, 0), (0, SSM_IN_PAD - SSM_IN)))
    dzx = lax.dynamic_update_slice(dzx, ddt_pad.astype(BF16), (0, DT_OFF))
    dw_in = run("ssm_in_dw", _mm, h2, dzx, "tn", out_dtype=BF16)
    in_shard = SSM_IN // N_SHARDS
    mats[("ssm_w_in", 0)] = jnp.stack([dw_in[:, i * in_shard:(i + 1) * in_shard] for i in range(N_SHARDS)])
    dh2 = run("ssm_in_dx", _mm, dzx, w["ssm_w_in"], "nt")
    dx2, dmn1 = _rms_bwd(x2, row(w["mixer_norm"][1]), dh2, dx3, "mixer_norm_bwd_1")

    dx1, dfn0 = _ffn_bwd(run, mats, x1, *ffn_w(0), ffn0, dx2, 0)
    do = run("attn_out_dx", _mm, dx1, w["attn_w_o"], "nt", out_dtype=BF16)
    mats[("attn_w_o", 0)] = row_stack(run("attn_out_dw", _mm, o, dx1, "tn", out_dtype=BF16))
    dq, dkc, dkp, dvloc, dvlop, dvhic, dvhip, dsink = run("attn_bwd", _attn_bwd, q, kk, vlo, vhi, sinks, do)
    dqkv, dgq, dgk = _attn_prep_bwd(qkv, cos, sin, gq, gk, dq, (dkc, dkp), (dvloc, dvlop), (dvhic, dvhip))
    mats[("attn_w_qkv", 0)] = run("attn_qkv_dw", _mm, h0, dqkv, "tn", out_dtype=BF16, out_cols=True)
    dh0 = run("attn_qkv_dx", _mm, dqkv, w["attn_w_qkv"], "nt", b_cols=True)
    dx0, dmn0 = _rms_bwd(x, row(w["mixer_norm"][0]), dh0, dx1, "mixer_norm_bwd_0")

    fold = lambda v: v[0, :ATT_HEAD_DIM] + v[0, ATT_HEAD_DIM:]
    grads = {
        "mixer_norm": jnp.concatenate([dmn0, dmn1], axis=0),
        "ffn_norm": jnp.concatenate([dfn0, dfn1], axis=0),
        "attn_q_norm": fold(dgq), "attn_k_norm": fold(dgk),
        "attn_sinks": dsink[:, :, 0].reshape(-1),
        "ssm_conv_w": dconv_w, "ssm_conv_b": dconv_b.reshape(-1),
        "ssm_dt_bias": dbias.reshape(-1) + dbias_t.reshape(-1),
        "ssm_a_log": dalog.reshape(-1) + dalog_t.reshape(-1), "ssm_d": dd.reshape(-1),
        "ssm_norm": dssm_norm.reshape(-1),
    }
    return loss_row[0, 0], dx0, grads


OTHER_CHIPS = ((1, 0), (0, 1), (1, 1))


def _position():
    return lax.axis_index("x"), lax.axis_index("y"), lax.axis_index("c")


def _sems(n):
    return pltpu.SemaphoreType.DMA((n,))


def _gather_shards(weights, layers):
    n_in, n_mat = len(weights), len(layers)

    def body(*refs):
        p, out = refs[:n_in], refs[n_in:n_in + n_mat]
        send_sems, recv_sems = refs[n_in + n_mat:]
        x, y, c = _position()
        me, sibling = (x, y, c), (x, y, 1 - c)
        chips = [(x ^ fx, y ^ fy) for fx, fy in OTHER_CHIPS]

        def rows(e, px, py, pc):
            half = out[e].shape[1] // 2
            return out[e].at[2 * px + py, pl.ds(pc * half, half), :]

        def copy(k, e, block, to, src=None):
            return pltpu.make_async_remote_copy(
                src_ref=rows(e, *block) if src is None else src, dst_ref=rows(e, *block),
                send_sem=send_sems.at[k * n_mat + e], recv_sem=recv_sems.at[k * n_mat + e],
                device_id=to, device_id_type=MESH)

        def own(e):
            i, l = layers[e]
            return pltpu.make_async_remote_copy(
                src_ref=p[i].at[l], dst_ref=out[e].at[2 * x + y], send_sem=send_sems.at[6 * n_mat + e],
                recv_sem=recv_sems.at[6 * n_mat + e], device_id=sibling, device_id_type=MESH)

        first, passed = [], []
        for e, (i, l) in enumerate(layers):
            half = out[e].shape[1] // 2
            first.append([copy(j, e, me, (*chip, c), src=p[i].at[l, pl.ds(c * half, half), :])
                          for j, chip in enumerate(chips)])
            for cp in first[-1]:
                cp.start()
        for e in range(n_mat):
            own(e).start()
        for e in range(n_mat):
            passed.append([copy(3 + j, e, (*chip, c), sibling) for j, chip in enumerate(chips)])
            for j, chip in enumerate(chips):
                copy(j, e, (*chip, c), me).wait_recv()
                passed[e][j].start()
        for e in range(n_mat):
            own(e).wait()
            for j, chip in enumerate(chips):
                copy(3 + j, e, (*chip, 1 - c), me).wait_recv()
        for e in range(n_mat):
            for cp in first[e] + passed[e]:
                cp.wait_send()

    return pl.pallas_call(
        body, name="gather_weights", in_specs=[ANY] * n_in, out_specs=[ANY] * n_mat,
        out_shape=[jax.ShapeDtypeStruct((N_SHARDS,) + weights[i].shape[1:], weights[i].dtype) for i, _ in layers],
        scratch_shapes=[_sems(7 * n_mat), _sems(7 * n_mat)],
    )(*weights)


def _all_gather8(block, name):
    m_per, n = block.shape

    def body(x_ref, out_ref, send_sems, recv_sems, local_sem):
        x, y, c = _position()
        me, sibling = (x, y, c), (x, y, 1 - c)
        chips = [(x ^ fx, y ^ fy) for fx, fy in OTHER_CHIPS]

        def rows(px, py, pc):
            return out_ref.at[pl.ds((4 * px + 2 * py + pc) * m_per, m_per), :]

        def copy(k, blk, to, src=None):
            return pltpu.make_async_remote_copy(
                src_ref=rows(*blk) if src is None else src, dst_ref=rows(*blk),
                send_sem=send_sems.at[k], recv_sem=recv_sems.at[k], device_id=to, device_id_type=MESH)

        mine = pltpu.make_async_copy(x_ref, rows(*me), local_sem)
        mine.start()
        first = [copy(0, me, sibling, src=x_ref)]
        first += [copy(1 + j, me, (*chip, c), src=x_ref) for j, chip in enumerate(chips)]
        for cp in first:
            cp.start()
        passed = [copy(4 + j, (*chip, c), sibling) for j, chip in enumerate(chips)]
        for j, chip in enumerate(chips):
            copy(1 + j, (*chip, c), me).wait_recv()
            passed[j].start()
        copy(0, sibling, me).wait_recv()
        for j, chip in enumerate(chips):
            copy(4 + j, (*chip, 1 - c), me).wait_recv()
        for cp in first + passed:
            cp.wait_send()
        mine.wait()

    return pl.pallas_call(
        body, name=name, out_shape=jax.ShapeDtypeStruct((N_DEV * m_per, n), block.dtype),
        in_specs=[pl.BlockSpec(memory_space=pltpu.VMEM)], out_specs=pl.BlockSpec(memory_space=pltpu.VMEM),
        scratch_shapes=[_sems(7), _sems(7), pltpu.SemaphoreType.DMA],
    )(block)


def _exchange(carry, name):
    n_in, n_out = len(carry["arrays"]), len(carry["out_shapes"])

    def body(*refs):
        copies = carry["build"](refs[:n_in], refs[n_in:n_in + n_out], refs[-2], refs[-1])
        for cp in copies:
            cp.start()
        for cp in copies:
            cp.wait()

    return pl.pallas_call(
        body, name=name, in_specs=[ANY] * n_in, out_specs=[ANY] * n_out, out_shape=list(carry["out_shapes"]),
        input_output_aliases=dict(carry.get("aliases", {})),
        scratch_shapes=[_sems(carry["n_sems"]), _sems(carry["n_sems"])],
    )(*carry["arrays"])


def _remote(src, dst, send_sems, recv_sems, k, to):
    return pltpu.make_async_remote_copy(src_ref=src, dst_ref=dst, send_sem=send_sems.at[k], recv_sem=recv_sems.at[k],
                                        device_id=to, device_id_type=MESH)


def _gather_over_ici(blocks, layers):
    def build(p, out, send_sems, recv_sems):
        x, y, c = _position()
        copies = []
        for e, l in enumerate(layers):
            half = out[e].shape[1] // 2
            rows = pl.ds(c * half, half)
            for j, (fx, fy) in enumerate(OTHER_CHIPS):
                copies.append(_remote(p[e].at[l, rows, :], out[e].at[2 * x + y, rows, :], send_sems, recv_sems,
                                      3 * e + j, (x ^ fx, y ^ fy, c)))
        return copies

    shapes = [jax.ShapeDtypeStruct((N_SHARDS,) + b.shape[1:], b.dtype) for b in blocks]
    return dict(build=build, arrays=list(blocks), out_shapes=shapes, n_sems=3 * len(layers))


def _gather_over_d2d(stacks, blocks, layers):
    n = len(stacks)

    def build(refs, out, send_sems, recv_sems):
        p = refs[n:]
        x, y, c = _position()
        sibling = (x, y, 1 - c)
        copies = []
        for e, l in enumerate(layers):
            half = out[e].shape[1] // 2
            for j, (fx, fy) in enumerate(OTHER_CHIPS):
                rows = out[e].at[2 * (x ^ fx) + (y ^ fy), pl.ds(c * half, half), :]
                copies.append(_remote(rows, rows, send_sems, recv_sems, 4 * e + j, sibling))
            copies.append(_remote(p[e].at[l], out[e].at[2 * x + y], send_sems, recv_sems, 4 * e + 3, sibling))
        return copies

    shapes = [jax.ShapeDtypeStruct(s.shape, s.dtype) for s in stacks]
    return dict(build=build, arrays=list(stacks) + list(blocks), out_shapes=shapes, n_sems=4 * n,
                aliases={i: i for i in range(n)})


def _grads_to_sibling(stacks):
    def build(g, out, send_sems, recv_sems):
        x, y, c = _position()
        copies = []
        for e in range(len(stacks)):
            half = g[e].shape[1] // 2
            copies.append(_remote(g[e].at[:, pl.ds((1 - c) * half, half), :], out[e], send_sems, recv_sems, e,
                                  (x, y, 1 - c)))
        return copies

    shapes = [jax.ShapeDtypeStruct((N_SHARDS, g.shape[1] // 2, g.shape[2]), g.dtype) for g in stacks]
    return dict(build=build, arrays=list(stacks), out_shapes=shapes, n_sems=len(stacks))


def _grads_to_owners(partials):
    def build(p, out, send_sems, recv_sems):
        x, y, c = _position()
        copies = []
        for e in range(len(partials)):
            for k, (fx, fy) in enumerate(OTHER_CHIPS):
                px, py = x ^ fx, y ^ fy
                copies.append(_remote(p[e].at[2 * px + py], out[e].at[k], send_sems, recv_sems, 3 * e + k,
                                      (px, py, c)))
        return copies

    shapes = [jax.ShapeDtypeStruct((len(OTHER_CHIPS),) + p.shape[1:], p.dtype) for p in partials]
    return dict(build=build, arrays=list(partials), out_shapes=shapes, n_sems=3 * len(partials))


def _share_halves(grads, layers):
    def build(_, out, send_sems, recv_sems):
        x, y, c = _position()
        copies = []
        for e, (i, l) in enumerate(layers):
            half = out[i].shape[1] // 2
            rows = out[i].at[l, pl.ds(c * half, half), :]
            copies.append(_remote(rows, rows, send_sems, recv_sems, e, (x, y, 1 - c)))
        return copies

    return dict(build=build, arrays=list(grads), out_shapes=[jax.ShapeDtypeStruct(g.shape, g.dtype) for g in grads],
                n_sems=len(layers), aliases={i: i for i in range(len(grads))})


ADD_BLOCK_ELEMS = 1 << 19


def _add_rows(half, cols):
    return _tile(half, max(16, ADD_BLOCK_ELEMS // cols // 16 * 16), 16)


def _add_pair(stack, recv, c_idx, name):
    _, half, cols = recv.shape
    tr = _add_rows(half, cols)
    nt = half // tr

    def body(c_ref, a_ref, b_ref, o_ref):
        o_ref[...] = (a_ref[...].astype(F32) + b_ref[...].astype(F32)).astype(o_ref.dtype)

    blk = pl.BlockSpec((None, tr, cols), lambda s, i, c_ref: (s, i, 0))
    return pl.pallas_call(
        body, name=name,
        grid_spec=pltpu.PrefetchScalarGridSpec(
            num_scalar_prefetch=1, grid=(N_SHARDS, nt),
            in_specs=[pl.BlockSpec((None, tr, cols), lambda s, i, c_ref: (s, c_ref[0] * nt + i, 0)), blk],
            out_specs=blk),
        out_shape=jax.ShapeDtypeStruct(recv.shape, recv.dtype),
        compiler_params=_params("parallel", "parallel"),
    )(c_idx, stack, recv)


def _add_owned(partial, recv, sc_idx, layer, shape, into, name):
    _, half, cols = partial.shape
    tr = _add_rows(half, cols)
    nt = half // tr

    def body(sc_ref, a_ref, r0_ref, r1_ref, r2_ref, *rest):
        o_ref = rest[-1]
        o_ref[...] = (((a_ref[...].astype(F32) + r0_ref[...].astype(F32)) + r1_ref[...].astype(F32))
                      + r2_ref[...].astype(F32))

    slot = lambda k: pl.BlockSpec((None, tr, cols), lambda i, sc_ref: (k, i, 0))
    has_into = into is not None
    return pl.pallas_call(
        body, name=name,
        grid_spec=pltpu.PrefetchScalarGridSpec(
            num_scalar_prefetch=1, grid=(nt,),
            in_specs=[pl.BlockSpec((None, tr, cols), lambda i, sc_ref: (sc_ref[0], i, 0)), slot(0), slot(1), slot(2)]
            + ([ANY] if has_into else []),
            out_specs=pl.BlockSpec((None, tr, cols), lambda i, sc_ref: (layer, sc_ref[1] * nt + i, 0))),
        out_shape=jax.ShapeDtypeStruct(shape, F32),
        input_output_aliases={5: 0} if has_into else {},
        compiler_params=_params("parallel"),
    )(*((sc_idx, partial, recv, recv, recv) + ((into,) if has_into else ())))


def _sum8(gathered):
    m = gathered.shape[0] // N_DEV

    def body(g_ref, o_ref):
        total = g_ref[0:m, :]
        for d in range(1, N_DEV):
            total = total + g_ref[d * m:(d + 1) * m, :]
        o_ref[...] = total

    return pl.pallas_call(
        body, name="small_grads_sum", out_shape=jax.ShapeDtypeStruct((m, LANES), F32),
        in_specs=[pl.BlockSpec(memory_space=pltpu.VMEM)], out_specs=pl.BlockSpec(memory_space=pltpu.VMEM),
    )(gathered)


ADAMW_BLOCK_ELEMS = 1 << 18


def _adamw(w, g, m, v, name):
    l, r, cols = w.shape
    tr = _tile(r, max(8, ADAMW_BLOCK_ELEMS // cols // 8 * 8), 8)

    def body(w_ref, g_ref, m_ref, v_ref, d_ref, nm_ref, nv_ref):
        gv = g_ref[...]
        nm = ADAM_B1 * m_ref[...] + (1.0 - ADAM_B1) * gv
        nv = ADAM_B2 * v_ref[...] + (1.0 - ADAM_B2) * jnp.square(gv)
        m_hat = nm / (1.0 - ADAM_B1 ** ADAM_STEP)
        v_hat = nv / (1.0 - ADAM_B2 ** ADAM_STEP)
        d_ref[...] = -ADAM_LR * (m_hat / (jnp.sqrt(v_hat) + ADAM_EPS) + ADAM_WD * w_ref[...])
        nm_ref[...] = nm
        nv_ref[...] = nv

    blk = pl.BlockSpec((None, tr, cols), lambda a, i: (a, i, 0))
    return pl.pallas_call(
        body, name=name, grid=(l, r // tr), in_specs=[blk] * 4, out_specs=[blk] * 3,
        out_shape=[jax.ShapeDtypeStruct(w.shape, F32)] * 3, compiler_params=_params("parallel", "parallel"),
    )(w, g, m, v)


WEIGHTS = ("mixer_norm", "ffn_norm", "attn_w_qkv", "attn_q_norm", "attn_k_norm", "attn_sinks", "attn_w_o",
           "ssm_w_in", "ssm_conv_w", "ssm_conv_b", "ssm_dt_bias", "ssm_a_log", "ssm_d", "ssm_norm", "ssm_w_out",
           "ffn_w_gate", "ffn_w_up", "ffn_w_down")
BIG = ("attn_w_qkv", "attn_w_o", "ffn_w_gate", "ffn_w_up", "ffn_w_down", "ssm_w_in", "ssm_w_out")
MATRICES = (("attn_w_qkv", 0), ("attn_w_o", 0), ("ffn_w_gate", 0), ("ffn_w_up", 0), ("ffn_w_down", 0),
            ("ssm_w_in", 0), ("ssm_w_out", 0), ("ffn_w_gate", 1), ("ffn_w_up", 1), ("ffn_w_down", 1))
MATRIX_LAYERS = tuple((BIG.index(n), l) for n, l in MATRICES)
GROUPS = {"attn": MATRICES[0:2], "ffn0": MATRICES[2:5], "ssm": MATRICES[5:7], "ffn1": MATRICES[7:10]}
SMALL_SHARDED = ("ssm_conv_w", "ssm_conv_b", "ssm_norm")
SMALL = tuple(n for n in WEIGHTS if n not in BIG)


def _pack_rows(parts, row_unit=8):
    flat = jnp.concatenate([p.reshape(-1) for p in parts])
    pad = (-flat.shape[0]) % (LANES * row_unit)
    return jnp.pad(flat, (0, pad)).reshape(-1, LANES)


def _unpack(flat, shapes):
    out, off = [], 0
    for shp in shapes:
        size = math.prod(shp)
        out.append(flat[off:off + size].reshape(shp))
        off += size
    return out


def kernel(x, positions, mixer_norm, ffn_norm, attn_w_qkv, attn_q_norm, attn_k_norm, attn_sinks, attn_w_o, ssm_w_in, ssm_conv_w, ssm_conv_b, ssm_dt_bias, ssm_a_log, ssm_d, ssm_norm, ssm_w_out, ffn_w_gate, ffn_w_up, ffn_w_down, loss_target, m_mixer_norm, m_ffn_norm, m_attn_w_qkv, m_attn_q_norm, m_attn_k_norm, m_attn_sinks, m_attn_w_o, m_ssm_w_in, m_ssm_conv_w, m_ssm_conv_b, m_ssm_dt_bias, m_ssm_a_log, m_ssm_d, m_ssm_norm, m_ssm_w_out, m_ffn_w_gate, m_ffn_w_up, m_ffn_w_down, v_mixer_norm, v_ffn_norm, v_attn_w_qkv, v_attn_q_norm, v_attn_k_norm, v_attn_sinks, v_attn_w_o, v_ssm_w_in, v_ssm_conv_w, v_ssm_conv_b, v_ssm_dt_bias, v_ssm_a_log, v_ssm_d, v_ssm_norm, v_ssm_w_out, v_ffn_w_gate, v_ffn_w_up, v_ffn_w_down):
    args = locals()
    w = {n: args[n] for n in WEIGHTS}
    m = {n: args["m_" + n] for n in WEIGHTS}
    v = {n: args["v_" + n] for n in WEIGHTS}
    ax, ay, ac = lax.axis_index("x"), lax.axis_index("y"), lax.axis_index("c")
    shard = 2 * ax + ay

    wb = {n: w[n].astype(BF16) for n in BIG}
    wl, hooks = {"ffn_w_gate": [None, None], "ffn_w_up": [None, None], "ffn_w_down": [None, None]}, {}

    def gathered(keys, stacks):
        for (n, l), st in zip(keys, stacks):
            if n == "ssm_w_in":
                wl[n] = jnp.concatenate([st[i] for i in range(N_SHARDS)]
                                        + [jnp.zeros((st.shape[1], SSM_IN_PAD - SSM_IN), BF16)], axis=1)
            elif n in ("ffn_w_gate", "ffn_w_up"):
                wl[n][l] = st
            elif n == "ffn_w_down":
                wl[n][l] = st.reshape(st.shape[0] * st.shape[1], st.shape[2])
            elif n == "attn_w_qkv":
                wl[n] = st
            else:
                wl[n] = st.reshape(st.shape[0] * st.shape[1], st.shape[2])

    def behind(name, hook):
        hooks[name] = _both(hooks[name], hook) if name in hooks else hook

    def gather_behind(keys, first_leg, second_leg):
        blocks, layers, got = [wb[n] for n, _ in keys], [l for _, l in keys], {}
        behind(first_leg, _Hook(lambda: _gather_over_ici(blocks, layers), lambda res: got.update(stacks=res)))
        behind(second_leg, _Hook(lambda: _gather_over_d2d(got["stacks"], blocks, layers),
                                 lambda res: gathered(keys, res)))

    gathered(GROUPS["attn"], _gather_shards([wb[n] for n, _ in GROUPS["attn"]],
                                            [(e, l) for e, (_, l) in enumerate(GROUPS["attn"])]))
    gather_behind(GROUPS["ffn0"], "attn_fwd", "attn_out")
    gather_behind(GROUPS["ssm"][:1], "ffn_gate_0", "ffn_up_0")
    gather_behind(GROUPS["ssm"][1:], "ffn_up_0", "ffn_down_0")
    gather_behind(GROUPS["ffn1"], "ssm_in", "ssm_out")
    small_shapes = [w[n].shape for n in SMALL_SHARDED]
    small_all = _all_gather8(_pack_rows([w[n] for n in SMALL_SHARDED]), "gather_small_params")
    small_all = small_all.reshape(N_DEV, -1)[::2]
    full, off = {}, 0
    for n, shp in zip(SMALL_SHARDED, small_shapes):
        size = math.prod(shp)
        seg = small_all[:, off:off + size].reshape((N_SHARDS,) + shp)
        full[n] = jnp.moveaxis(seg, 0, -2).reshape(shp[:-1] + (N_SHARDS * shp[-1],))
        off += size
    wl.update({
        "mixer_norm": mixer_norm, "ffn_norm": ffn_norm,
        "attn_q_norm": attn_q_norm[0], "attn_k_norm": attn_k_norm[0], "attn_sinks": attn_sinks[0],
        "ssm_conv_w": full["ssm_conv_w"][0], "ssm_conv_b": full["ssm_conv_b"][0],
        "ssm_dt_bias": ssm_dt_bias[0], "ssm_a_log": ssm_a_log[0], "ssm_d": ssm_d[0],
        "ssm_norm": full["ssm_norm"][0],
    })

    c_idx = ac.reshape(1).astype(jnp.int32)
    sc_idx = jnp.stack([shard, ac]).astype(jnp.int32)
    mats, halves = {}, {n: None for n in BIG}

    def pair_sums(keys, recv):
        return [_add_pair(mats[k], r, c_idx, f"grads_add_pair_{k[0]}_{k[1]}") for k, r in zip(keys, recv)]

    def owner_sums(keys, partials, recv):
        for (n, l), p, r in zip(keys, partials, recv):
            halves[n] = _add_owned(p, r, sc_idx, l, w[n].shape, halves[n], f"grads_add_owned_{n}_{l}")

    def reduce_behind(keys, first_leg, second_leg):
        got = {}
        behind(first_leg, _Hook(lambda: _grads_to_sibling([mats[k] for k in keys]),
                                lambda res: got.update(partials=pair_sums(keys, res))))
        behind(second_leg, _Hook(lambda: _grads_to_owners(got["partials"]),
                                 lambda res: owner_sums(keys, got["partials"], res)))

    reduce_behind(GROUPS["ffn1"], "ssm_out_dx", "ssm_in_dw")
    reduce_behind(GROUPS["ssm"], "ssm_in_dx", "ffn_down_dx_0")
    reduce_behind(GROUPS["ffn0"], "attn_out_dx", "attn_bwd")
    reduce_behind(GROUPS["attn"][1:], "attn_bwd", "attn_qkv_dw")
    last, early, tail = GROUPS["attn"][0], [n for n in BIG if n != "attn_w_qkv"], {}
    behind("attn_qkv_dx", _Hook(lambda: _grads_to_sibling([mats[last]]),
                                lambda res: tail.update(partials=pair_sums([last], res))))
    behind("attn_qkv_dx", _Hook(
        lambda: _share_halves([halves[n] for n in early], [(early.index(n), l) for n, l in MATRICES if n in early]),
        lambda res: tail.update(grads=dict(zip(early, res)))))
    loss_part, dx, g_full = _local_step(x[0], positions[0], loss_target[0], wl, hooks, mats)
    owner_sums([last], tail["partials"], _exchange(_grads_to_owners(tail["partials"]), "grads_to_owners"))
    grads = tail["grads"]
    grads[last[0]], = _exchange(_share_halves([halves[last[0]]], [(0, 0)]), "grads_share_halves")

    small_full_shapes = [g_full[n].shape for n in SMALL] + [(1,)]
    small_g = _pack_rows([g_full[n] for n in SMALL] + [loss_part.reshape(1)])
    small_sum = _sum8(_all_gather8(small_g, "gather_small_grads")).reshape(-1)
    *small_list, loss = _unpack(small_sum, small_full_shapes)
    for n, g in zip(SMALL, small_list):
        if n in SMALL_SHARDED:
            width = w[n].shape[-1]
            g = lax.dynamic_slice_in_dim(g, shard * width, width, axis=g.ndim - 1)
        grads[n] = g.reshape(w[n].shape)

    delta, new_m, new_v = {}, {}, {}
    for n in BIG:
        delta[n], new_m[n], new_v[n] = _adamw(w[n], grads[n], m[n], v[n], "adamw_" + n)
    small_local = [w[n].shape for n in SMALL]
    pk = lambda t: _pack_rows([t[n] for n in SMALL])[None]
    outs = _adamw(pk(w), pk(grads), pk(m), pk(v), "adamw_small")
    for res, o in zip((delta, new_m, new_v), outs):
        for n, a in zip(SMALL, _unpack(o.reshape(-1), small_local)):
            res[n] = a

    return (loss.reshape(()), dx[None], *[grads[n] for n in WEIGHTS], *[delta[n] for n in WEIGHTS],
            *[new_m[n] for n in WEIGHTS], *[new_v[n] for n in WEIGHTS])
```

```python
import math

import jax
import jax.numpy as jnp
from jax import lax
from jax.experimental import pallas as pl
from jax.experimental.pallas import tpu as pltpu

F32 = jnp.float32
BF16 = jnp.bfloat16

D_MODEL = 2048
EPS = 1e-6
ATT_HEAD_DIM = 64
ATT_Q_HEADS = 32
ATT_KV_HEADS = 4
ATT_GROUP = 8
ATT_BLOCK = 128
ROPE_THETA = 10000.0
Q_WIDTH = ATT_Q_HEADS * ATT_HEAD_DIM
KV_WIDTH = ATT_KV_HEADS * ATT_HEAD_DIM
SSM_D_INNER = 4096
SSM_HEADS = 64
SSM_GROUPS = 8
SSM_HPG = 8
SSM_P = 64
SSM_STATE = 128
SSM_CONV = 4
SSM_CHUNK = 256
SSM_CONV_DIM = 6144
SSM_GN = SSM_D_INNER // SSM_GROUPS
SSM_IN = SSM_D_INNER + SSM_CONV_DIM + SSM_HEADS
LANES = 128
SSM_IN_PAD = -(-SSM_IN // LANES) * LANES
N_SHARDS = 4
N_DEV = 8

ADAM_LR = 0.001
ADAM_B1 = 0.9
ADAM_B2 = 0.999
ADAM_EPS = 1e-08
ADAM_WD = 0.01
ADAM_STEP = 10

VMEM_LIMIT = 56 * 1024 * 1024
MESH = pl.DeviceIdType.MESH
ANY = pl.BlockSpec(memory_space=pl.ANY)


def _params(*sem):
    return pltpu.CompilerParams(dimension_semantics=sem, vmem_limit_bytes=VMEM_LIMIT)


def _sems(n):
    return pltpu.SemaphoreType.DMA((n,))


def _call(body, carry, name, grid, in_specs, out_specs, out_shape, scratch_shapes, sem, args):
    if carry is None:
        return pl.pallas_call(body, name=name, grid=grid, in_specs=in_specs, out_specs=out_specs,
                              out_shape=out_shape, scratch_shapes=scratch_shapes,
                              compiler_params=_params(*sem))(*args)
    n_in, n_out, n_scr = len(in_specs), len(out_specs), len(scratch_shapes)
    c_arrays, c_shapes = list(carry["arrays"]), list(carry["out_shapes"])
    n_cin, n_cout = len(c_arrays), len(c_shapes)

    def carrying(*refs):
        ins, refs = refs[:n_in], refs[n_in:]
        cin, refs = refs[:n_cin], refs[n_cin:]
        outs, refs = refs[:n_out], refs[n_out:]
        cout, refs = refs[:n_cout], refs[n_cout:]
        scratch, (send_sems, recv_sems) = refs[:n_scr], refs[n_scr:]
        copies = carry["build"](cin, cout, send_sems, recv_sems)
        ids = [pl.program_id(d) for d in range(len(grid))]
        first, last = ids[0] == 0, ids[0] == grid[0] - 1
        for d in range(1, len(grid)):
            first = jnp.logical_and(first, ids[d] == 0)
            last = jnp.logical_and(last, ids[d] == grid[d] - 1)

        @pl.when(first)
        def _():
            for cp in copies:
                cp.start()

        body(*ins, *outs, *scratch)

        @pl.when(last)
        def _():
            for cp in copies:
                cp.wait()

    aliases = {n_in + i: n_out + o for i, o in carry.get("aliases", {}).items()}
    return pl.pallas_call(
        carrying, name=name, grid=grid, in_specs=list(in_specs) + [ANY] * n_cin,
        out_specs=list(out_specs) + [ANY] * n_cout, out_shape=list(out_shape) + c_shapes,
        scratch_shapes=list(scratch_shapes) + [_sems(carry["n_sems"]), _sems(carry["n_sems"])],
        input_output_aliases=aliases, compiler_params=_params(*(["arbitrary"] * len(grid))))(*args, *c_arrays)


def _tile(dim, target, unit=LANES):
    if dim <= target:
        return dim
    t = (target // unit) * unit
    while t >= unit:
        if dim % t == 0:
            return t
        t -= unit
    return dim


def _dot(a, b):
    return lax.dot_general(a, b, (((1,), (0,)), ((), ())), preferred_element_type=F32)


def _dot_nt(a, b):
    return lax.dot_general(a, b, (((1,), (1,)), ((), ())), preferred_element_type=F32)


def _dot_tn(a, b):
    return lax.dot_general(a, b, (((0,), (0,)), ((), ())), preferred_element_type=F32)


def _split3(x):
    hi = x.astype(BF16)
    r1 = x - hi.astype(F32)
    mid = r1.astype(BF16)
    lo = (r1 - mid.astype(F32)).astype(BF16)
    return hi, mid, lo


def _dot_x(x, m):
    hi, mid, lo = _split3(x)
    return _dot(hi, m) + _dot(mid, m) + _dot(lo, m)


def _dot_x2(x, m):
    hi = x.astype(BF16)
    return _dot(hi, m) + _dot((x - hi.astype(F32)).astype(BF16), m)


def _xdot(m, x):
    hi, mid, lo = _split3(x)
    return _dot(m, hi) + _dot(m, mid) + _dot(m, lo)


def _dot_x2_nt(x, m):
    hi = x.astype(BF16)
    return _dot_nt(hi, m) + _dot_nt((x - hi.astype(F32)).astype(BF16), m)


def _dot_x_nt(x, m):
    hi, mid, lo = _split3(x)
    return _dot_nt(hi, m) + _dot_nt(mid, m) + _dot_nt(lo, m)


def _iota(shape, dim):
    return lax.broadcasted_iota(jnp.int32, shape, dim)


def _sigmoid(x):
    return 0.5 * jnp.tanh(0.5 * x) + 0.5


def _softplus(x):
    return jnp.maximum(x, 0.0) + jnp.log(1.0 + jnp.exp(-jnp.abs(x)))


MM_ROWS = 1024
MM_TILE = 1408
MM_DEPTH = 3456
FUSED_ROWS = 512


def _mm(a, b, mode, name, add=None, out_dtype=F32, b_cols=False, out_cols=False, fuse=None, rows=MM_ROWS,
        carry=None):
    bs = b.shape[-2:]
    if b_cols:
        bs = (bs[0], N_SHARDS * bs[1])
    if mode == "nn":
        (m, k), (k2, n) = a.shape, bs
    elif mode == "nt":
        (m, k), (n, k2) = a.shape, bs
    else:
        (k, m), (k2, n) = a.shape, bs
    assert k == k2, (a.shape, b.shape, mode)
    split_n = (b_cols and mode == "nn") or out_cols
    split_k = b_cols and mode == "nt"
    tm = _tile(m, MM_TILE if mode == "tn" else rows)
    tn = _tile(n // N_SHARDS if split_n else n, MM_TILE)
    tk = _tile(k // N_SHARDS if split_k else k, MM_DEPTH)
    nk = k // tk
    nj, nq = (n // N_SHARDS) // tn, (k // N_SHARDS) // tk
    two_shards = split_k and nq == 1
    if two_shards:
        nk = N_SHARDS // 2
    if mode == "tn":
        a_spec = pl.BlockSpec((tk, tm), lambda i, j, q: (q, i))
    else:
        a_spec = pl.BlockSpec((tm, 2 * tk if two_shards else tk), lambda i, j, q: (i, q))
    if mode == "nt":
        if two_shards:
            b_spec = pl.BlockSpec((2, tn, tk), lambda i, j, q: (q, j, 0))
        elif b_cols:
            b_spec = pl.BlockSpec((None, tn, tk), lambda i, j, q: (q // nq, j, q % nq))
        else:
            b_spec = pl.BlockSpec((tn, tk), lambda i, j, q: (j, q))
    elif b_cols:
        b_spec = pl.BlockSpec((None, tk, tn), lambda i, j, q: (j // nj, q, j % nj))
    else:
        b_spec = pl.BlockSpec((tk, tn), lambda i, j, q: (q, j))
    add_spec = pl.BlockSpec((tm, tn), lambda i, j, q: (i, j))
    if out_cols:
        o_spec = pl.BlockSpec((None, tm, tn), lambda i, j, q: (j // nj, i, j % nj))
        o_shape = (N_SHARDS, m, n // N_SHARDS)
    else:
        o_spec, o_shape = add_spec, (m, n)
    dot = {"nn": _dot, "nt": _dot_nt, "tn": _dot_tn}[mode]
    has_add = add is not None
    fuse_fn, extra, out_dtypes = fuse if fuse is not None else (None, [], [out_dtype])
    n_in, n_out = 2 + has_add + len(extra), len(out_dtypes)

    def body(*refs):
        a_ref, b_ref = refs[:2]
        add_ref = refs[2] if has_add else None
        extra_refs = refs[2 + has_add:n_in]
        o_refs, acc_ref = refs[n_in:n_in + n_out], refs[n_in + n_out]
        if two_shards:
            part = (dot(a_ref[:, :tk].astype(BF16), b_ref[0].astype(BF16))
                    + dot(a_ref[:, tk:].astype(BF16), b_ref[1].astype(BF16)))
        else:
            part = dot(a_ref[...].astype(BF16), b_ref[...].astype(BF16))

        def finish(total):
            if has_add:
                total = total + add_ref[...].astype(F32)
            outs = (total,) if fuse_fn is None else fuse_fn(total, *[r[...] for r in extra_refs])
            for o_ref, val in zip(o_refs, outs):
                o_ref[...] = val.astype(o_ref.dtype)

        if nk == 1:
            finish(part)
        else:
            q = pl.program_id(2)

            @pl.when(q == 0)
            def _():
                acc_ref[...] = part

            @pl.when(jnp.logical_and(q > 0, q < nk - 1))
            def _():
                acc_ref[...] += part

            @pl.when(q == nk - 1)
            def _():
                finish(acc_ref[...] + part)

    in_specs = [a_spec, b_spec] + [add_spec] * (has_add + len(extra))
    args = (a, b) + ((add,) if has_add else ()) + tuple(extra)
    res = _call(body, carry, name, (m // tm, n // tn, nk), in_specs, [o_spec] * n_out,
                [jax.ShapeDtypeStruct(o_shape, dt) for dt in out_dtypes],
                [pltpu.VMEM((tm, tn) if nk > 1 else (8, LANES), F32)], ("parallel", "parallel", "arbitrary"), args)
    main = res[0] if fuse is None else res[:n_out]
    return main if carry is None else (main, res[n_out:])


def _rms_fwd(x, g, name):
    s, d = x.shape
    ts = _tile(s, 512, 8)

    def body(x_ref, g_ref, o_ref):
        xv = x_ref[...]
        r = lax.rsqrt(jnp.mean(xv * xv, axis=-1, keepdims=True) + EPS)
        o_ref[...] = (xv * r * g_ref[...]).astype(BF16)

    return pl.pallas_call(
        body, name=name, grid=(s // ts,),
        in_specs=[pl.BlockSpec((ts, d), lambda i: (i, 0)), pl.BlockSpec((1, d), lambda i: (0, 0))],
        out_specs=pl.BlockSpec((ts, d), lambda i: (i, 0)),
        out_shape=jax.ShapeDtypeStruct((s, d), BF16),
        compiler_params=_params("parallel"),
    )(x, g)


def _rms_bwd(x, g, dh, dres, name):
    s, d = x.shape
    ts = _tile(s, 512, 8)

    def body(x_ref, g_ref, dh_ref, dres_ref, dx_ref, dg_ref):
        xv = x_ref[...]
        r = lax.rsqrt(jnp.mean(xv * xv, axis=-1, keepdims=True) + EPS)
        xhat = xv * r
        dhv = dh_ref[...].astype(F32)
        part = jnp.sum(dhv * xhat, axis=0, keepdims=True)

        @pl.when(pl.program_id(0) == 0)
        def _():
            dg_ref[...] = part

        @pl.when(pl.program_id(0) > 0)
        def _():
            dg_ref[...] += part

        dxh = dhv * g_ref[...]
        dx = r * (dxh - xhat * jnp.mean(dxh * xhat, axis=-1, keepdims=True))
        dx_ref[...] = dres_ref[...] + dx

    row = pl.BlockSpec((ts, d), lambda i: (i, 0))
    vec = pl.BlockSpec((1, d), lambda i: (0, 0))
    return pl.pallas_call(
        body, name=name, grid=(s // ts,),
        in_specs=[row, vec, row, row], out_specs=[row, vec],
        out_shape=[jax.ShapeDtypeStruct((s, d), F32), jax.ShapeDtypeStruct((1, d), F32)],
        compiler_params=_params("arbitrary"),
    )(x, g, dh, dres)


def _loss_fwd_bwd(y, target):
    s, d = y.shape
    ts = _tile(s, 512, 8)

    def body(y_ref, t_ref, l_ref, dy_ref):
        diff = y_ref[...] - t_ref[...]
        dy_ref[...] = diff * (1.0 / d)
        part = jnp.full((1, LANES), 0.5 * jnp.sum(jnp.mean(diff * diff, axis=-1, keepdims=True)), F32)

        @pl.when(pl.program_id(0) == 0)
        def _():
            l_ref[...] = part

        @pl.when(pl.program_id(0) > 0)
        def _():
            l_ref[...] += part

    row = pl.BlockSpec((ts, d), lambda i: (i, 0))
    acc = pl.BlockSpec((1, LANES), lambda i: (0, 0))
    return pl.pallas_call(
        body, name="loss", grid=(s // ts,), in_specs=[row, row], out_specs=[acc, row],
        out_shape=[jax.ShapeDtypeStruct((1, LANES), F32), jax.ShapeDtypeStruct((s, d), F32)],
        compiler_params=_params("arbitrary"),
    )(y, target)


def _lane_consts():
    r, c = _iota((LANES, LANES), 0), _iota((LANES, LANES), 1)
    same = (r >> 6) == (c >> 6)
    rin, cin = r & 63, c & 63
    one = lambda cond: jnp.where(cond, 1.0, 0.0).astype(BF16)
    return dict(
        seg=one(same),
        rot=(jnp.where(same & (rin == cin + 32), -1.0, 0.0)
             + jnp.where(same & (cin == rin + 32), 1.0, 0.0)).astype(BF16),
        dup_lo=one(r == cin), dup_hi=one(r == cin + 64),
        up=one((c >= 64) & (r == c - 64)), down=one((c < 64) & (r == c + 64)),
        fold_lo=one((c < 64) & (rin == c)), fold_hi=one((c >= 64) & (rin == c - 64)),
    )


def _norm_rope(xc, gain, cos, sin, k):
    ss = _dot_x2(xc * xc, k["seg"])
    rinv = lax.rsqrt(ss * (1.0 / ATT_HEAD_DIM) + EPS)
    xhat = xc * rinv
    y = xhat * gain
    return y * cos + _dot_x2(y, k["rot"]) * sin, xhat, rinv


def _norm_rope_bwd(dr, xhat, rinv, gain, cos, sin, k):
    dy = dr * cos - _dot_x2(dr * sin, k["rot"])
    dgain = jnp.sum(dy * xhat, axis=0, keepdims=True)
    dxh = dy * gain
    dx = rinv * (dxh - xhat * (_dot_x2(dxh * xhat, k["seg"]) * (1.0 / ATT_HEAD_DIM)))
    return dx, dgain


def _attn_prep(qkv, cos, sin, gq, gk):
    s = qkv.shape[0]
    tr = _tile(s, 256, 8)

    def body(x_ref, cos_ref, sin_ref, gq_ref, gk_ref, q_ref, kk_ref, vlo_ref, vhi_ref):
        k = _lane_consts()
        cosv, sinv = cos_ref[...], sin_ref[...]
        lane = _iota((tr, LANES), 1)
        for j in range(Q_WIDTH // LANES):
            r, _, _ = _norm_rope(x_ref[:, j * LANES:(j + 1) * LANES], gq_ref[...], cosv, sinv, k)
            q_ref[:, j * LANES:(j + 1) * LANES] = r.astype(BF16)
        for i in range(KV_WIDTH // LANES):
            off = Q_WIDTH + i * LANES
            r, _, _ = _norm_rope(x_ref[:, off:off + LANES], gk_ref[...], cosv, sinv, k)
            rb = r.astype(BF16)
            kk_ref[:, (2 * i) * LANES:(2 * i + 1) * LANES] = _dot(rb, k["dup_lo"]).astype(BF16)
            kk_ref[:, (2 * i + 1) * LANES:(2 * i + 2) * LANES] = _dot(rb, k["dup_hi"]).astype(BF16)
            off = Q_WIDTH + KV_WIDTH + i * LANES
            vb = x_ref[:, off:off + LANES].astype(BF16)
            zero = jnp.zeros_like(vb)
            vlo_ref[:, (2 * i) * LANES:(2 * i + 1) * LANES] = jnp.where(lane < 64, vb, zero)
            vhi_ref[:, (2 * i) * LANES:(2 * i + 1) * LANES] = _dot(vb, k["up"]).astype(BF16)
            vlo_ref[:, (2 * i + 1) * LANES:(2 * i + 2) * LANES] = _dot(vb, k["down"]).astype(BF16)
            vhi_ref[:, (2 * i + 1) * LANES:(2 * i + 2) * LANES] = jnp.where(lane >= 64, vb, zero)

    w = qkv.shape[1]
    row = lambda width: pl.BlockSpec((tr, width), lambda i: (i, 0))
    vec = pl.BlockSpec((1, LANES), lambda i: (0, 0))
    kw = ATT_KV_HEADS * LANES
    return pl.pallas_call(
        body, name="attn_prep", grid=(s // tr,),
        in_specs=[row(w), row(LANES), row(LANES), vec, vec],
        out_specs=[row(Q_WIDTH), row(kw), row(kw), row(kw)],
        out_shape=[jax.ShapeDtypeStruct((s, Q_WIDTH), BF16)] + [jax.ShapeDtypeStruct((s, kw), BF16)] * 3,
        compiler_params=_params("parallel"),
    )(qkv, cos, sin, gq, gk)


def _softmax_sink(s, valid, sink):
    s = jnp.where(valid, s, -jnp.inf)
    m = jnp.maximum(jnp.max(s, axis=-1, keepdims=True), sink)
    p = jnp.exp(s - m)
    esink = jnp.exp(sink - m)
    inv = 1.0 / (jnp.sum(p, axis=-1, keepdims=True) + esink)
    return p * inv, esink * inv


def _attn_specs(order):
    if order == "nh":
        cur = lambda n, h: (n, h)
        prev = lambda n, h: (jnp.maximum(n - 1, 0), h)
    else:
        cur = lambda h, n: (n, h)
        prev = lambda h, n: (jnp.maximum(n - 1, 0), h)
    qs = pl.BlockSpec((ATT_BLOCK, ATT_GROUP * ATT_HEAD_DIM), cur)
    kc = pl.BlockSpec((ATT_BLOCK, LANES), cur)
    kp = pl.BlockSpec((ATT_BLOCK, LANES), prev)
    return qs, kc, kp


def _pair_rows(qp):
    lane = _iota((ATT_BLOCK, LANES), 1)
    zero = jnp.zeros_like(qp)
    return jnp.concatenate([jnp.where(lane < 64, qp, zero), jnp.where(lane >= 64, qp, zero)], axis=0)


def _pair_masks(n):
    qi = _iota((2 * ATT_BLOCK, 2 * ATT_BLOCK), 0) & (ATT_BLOCK - 1)
    kj = _iota((2 * ATT_BLOCK, 2 * ATT_BLOCK), 1)
    valid = (kj > qi) & (kj <= qi + ATT_BLOCK) & ((kj >= ATT_BLOCK) | (n > 0))
    return valid, _iota((2 * ATT_BLOCK, 1), 0) >= ATT_BLOCK


def _attn_fwd(q, kk, vlo, vhi, sinks, name="attn_fwd", carry=None):
    s = q.shape[0]
    nb = s // ATT_BLOCK
    scale = ATT_HEAD_DIM ** -0.5

    def body(sink_ref, q_ref, kc_ref, kp_ref, vloc_ref, vlop_ref, vhic_ref, vhip_ref, o_ref):
        n, h = pl.program_id(0), pl.program_id(1)
        valid, upper = _pair_masks(n)
        kw = jnp.concatenate([kp_ref[...], kc_ref[...]], axis=0)
        vcat = jnp.concatenate([vlop_ref[...], vloc_ref[...], vhip_ref[...], vhic_ref[...]], axis=0)
        for jp in range(ATT_GROUP // 2):
            q2 = _pair_rows(q_ref[:, jp * LANES:(jp + 1) * LANES])
            sink = jnp.where(upper, sink_ref[h * ATT_GROUP + 2 * jp + 1], sink_ref[h * ATT_GROUP + 2 * jp])
            probs, _ = _softmax_sink(_dot_nt(q2, kw) * scale, valid, sink)
            pcat = jnp.concatenate([probs[:ATT_BLOCK], probs[ATT_BLOCK:]], axis=1).astype(BF16)
            o_ref[:, jp * LANES:(jp + 1) * LANES] = _dot(pcat, vcat).astype(BF16)

    qs, kc, kp = _attn_specs("nh")
    res = _call(body, carry, name, (nb, ATT_KV_HEADS),
                [pl.BlockSpec(memory_space=pltpu.SMEM), qs, kc, kp, kc, kp, kc, kp], [qs],
                [jax.ShapeDtypeStruct((s, Q_WIDTH), BF16)], [], ("parallel", "parallel"),
                (sinks, q, kk, kk, vlo, vlo, vhi, vhi))
    return res[0] if carry is None else (res[0], res[1:])


def _attn_bwd(q, kk, vlo, vhi, sinks, do, name="attn_bwd", carry=None):
    s = q.shape[0]
    nb = s // ATT_BLOCK
    scale = ATT_HEAD_DIM ** -0.5

    def body(sink_ref, q_ref, kc_ref, kp_ref, vloc_ref, vlop_ref, vhic_ref, vhip_ref, do_ref,
             dq_ref, dkc_ref, dkp_ref, dvloc_ref, dvlop_ref, dvhic_ref, dvhip_ref, dsink_ref, dkk_acc, dv_acc):
        h, n = pl.program_id(0), pl.program_id(1)
        valid, upper = _pair_masks(n)
        kw = jnp.concatenate([kp_ref[...], kc_ref[...]], axis=0)
        vcat = jnp.concatenate([vlop_ref[...], vloc_ref[...], vhip_ref[...], vhic_ref[...]], axis=0)
        lane = _iota((ATT_BLOCK, LANES), 1)
        sub = _iota((ATT_GROUP, LANES), 0)
        dsink = jnp.zeros((ATT_GROUP, LANES), F32)
        for jp in range(ATT_GROUP // 2):
            q2 = _pair_rows(q_ref[:, jp * LANES:(jp + 1) * LANES])
            dop = do_ref[:, jp * LANES:(jp + 1) * LANES]
            sink = jnp.where(upper, sink_ref[h * ATT_GROUP + 2 * jp + 1], sink_ref[h * ATT_GROUP + 2 * jp])
            probs, psink = _softmax_sink(_dot_nt(q2, kw) * scale, valid, sink)
            pcat = jnp.concatenate([probs[:ATT_BLOCK], probs[ATT_BLOCK:]], axis=1).astype(BF16)
            dpc = _dot_nt(dop, vcat)
            dprobs = jnp.concatenate([dpc[:, :2 * ATT_BLOCK], dpc[:, 2 * ATT_BLOCK:]], axis=0)
            dv_part = _dot_tn(pcat, dop)
            delta = jnp.sum(probs * dprobs, axis=-1, keepdims=True)
            ds = (probs * (dprobs - delta) * scale).astype(BF16)
            sd = psink * delta
            dsink = (dsink + jnp.where(sub == 2 * jp, -jnp.sum(sd[:ATT_BLOCK]), 0.0)
                     + jnp.where(sub == 2 * jp + 1, -jnp.sum(sd[ATT_BLOCK:]), 0.0))
            dq2 = _dot(ds, kw)
            dq_ref[:, jp * LANES:(jp + 1) * LANES] = jnp.where(lane < 64, dq2[:ATT_BLOCK], dq2[ATT_BLOCK:])
            dkk_part = _dot_tn(ds, q2)
            if jp == 0:
                dkk_acc[...], dv_acc[...] = dkk_part, dv_part
            else:
                dkk_acc[...] += dkk_part
                dv_acc[...] += dv_part
        blk = ATT_BLOCK
        dkp_ref[...], dkc_ref[...] = dkk_acc[:blk], dkk_acc[blk:]
        dvlop_ref[...], dvloc_ref[...] = dv_acc[:blk], dv_acc[blk:2 * blk]
        dvhip_ref[...], dvhic_ref[...] = dv_acc[2 * blk:3 * blk], dv_acc[3 * blk:]

        @pl.when(n == 0)
        def _():
            dsink_ref[0] = dsink

        @pl.when(n > 0)
        def _():
            dsink_ref[0] += dsink

    qs, kc, kp = _attn_specs("hn")
    kw_shape = jax.ShapeDtypeStruct((s, ATT_KV_HEADS * LANES), F32)
    res = _call(body, carry, name, (ATT_KV_HEADS, nb),
                [pl.BlockSpec(memory_space=pltpu.SMEM), qs, kc, kp, kc, kp, kc, kp, qs],
                [qs] + [kc] * 6 + [pl.BlockSpec((1, ATT_GROUP, LANES), lambda h, n: (h, 0, 0))],
                [jax.ShapeDtypeStruct((s, Q_WIDTH), F32)] + [kw_shape] * 6
                + [jax.ShapeDtypeStruct((ATT_KV_HEADS, ATT_GROUP, LANES), F32)],
                [pltpu.VMEM((2 * ATT_BLOCK, LANES), F32), pltpu.VMEM((4 * ATT_BLOCK, LANES), F32)],
                ("parallel", "arbitrary"), (sinks, q, kk, kk, vlo, vlo, vhi, vhi, do))
    return res if carry is None else (res[:8], res[8:])


def _attn_prep_bwd(qkv, cos, sin, gq, gk, dq, dks, dvlos, dvhis):
    s, w = qkv.shape
    tr = ATT_BLOCK
    nb = s // tr

    def body(x_ref, cos_ref, sin_ref, gq_ref, gk_ref, dq_ref, dkc_ref, dkn_ref, dvloc_ref, dvlon_ref,
             dvhic_ref, dvhin_ref, dx_ref, dgq_ref, dgk_ref):
        n = pl.program_id(0)
        k = _lane_consts()
        cosv, sinv = cos_ref[...], sin_ref[...]
        nxt = jnp.where(n < nb - 1, 1.0, 0.0)
        lane = _iota((tr, LANES), 1)
        dgq = jnp.zeros((1, LANES), F32)
        dgk = jnp.zeros((1, LANES), F32)
        for j in range(Q_WIDTH // LANES):
            sl = slice(j * LANES, (j + 1) * LANES)
            _, xhat, rinv = _norm_rope(x_ref[:, sl], gq_ref[...], cosv, sinv, k)
            dx, dg = _norm_rope_bwd(dq_ref[:, sl], xhat, rinv, gq_ref[...], cosv, sinv, k)
            dx_ref[:, sl] = dx.astype(BF16)
            dgq = dgq + dg
        for i in range(KV_WIDTH // LANES):
            a, b = slice(2 * i * LANES, (2 * i + 1) * LANES), slice((2 * i + 1) * LANES, (2 * i + 2) * LANES)
            dr = (_dot_x2(dkc_ref[:, a] + nxt * dkn_ref[:, a], k["fold_lo"])
                  + _dot_x2(dkc_ref[:, b] + nxt * dkn_ref[:, b], k["fold_hi"]))
            sl = slice(Q_WIDTH + i * LANES, Q_WIDTH + (i + 1) * LANES)
            _, xhat, rinv = _norm_rope(x_ref[:, sl], gk_ref[...], cosv, sinv, k)
            dx, dg = _norm_rope_bwd(dr, xhat, rinv, gk_ref[...], cosv, sinv, k)
            dx_ref[:, sl] = dx.astype(BF16)
            dgk = dgk + dg
            ta = jnp.where(lane < 64, dvloc_ref[:, a] + nxt * dvlon_ref[:, a], dvhic_ref[:, a] + nxt * dvhin_ref[:, a])
            tb = jnp.where(lane < 64, dvloc_ref[:, b] + nxt * dvlon_ref[:, b], dvhic_ref[:, b] + nxt * dvhin_ref[:, b])
            sl = slice(Q_WIDTH + KV_WIDTH + i * LANES, Q_WIDTH + KV_WIDTH + (i + 1) * LANES)
            dx_ref[:, sl] = (_dot_x2(ta, k["fold_lo"]) + _dot_x2(tb, k["fold_hi"])).astype(BF16)

        @pl.when(n == 0)
        def _():
            dgq_ref[...] = dgq
            dgk_ref[...] = dgk

        @pl.when(n > 0)
        def _():
            dgq_ref[...] += dgq
            dgk_ref[...] += dgk

    row = lambda width: pl.BlockSpec((tr, width), lambda i: (i, 0))
    nxt_row = pl.BlockSpec((tr, ATT_KV_HEADS * LANES), lambda i: (jnp.minimum(i + 1, nb - 1), 0))
    vec = pl.BlockSpec((1, LANES), lambda i: (0, 0))
    kw = ATT_KV_HEADS * LANES
    return pl.pallas_call(
        body, name="attn_prep_bwd", grid=(nb,),
        in_specs=[row(w), row(LANES), row(LANES), vec, vec, row(Q_WIDTH),
                  row(kw), nxt_row, row(kw), nxt_row, row(kw), nxt_row],
        out_specs=[row(w), vec, vec],
        out_shape=[jax.ShapeDtypeStruct((s, w), BF16), jax.ShapeDtypeStruct((1, LANES), F32),
                   jax.ShapeDtypeStruct((1, LANES), F32)],
        compiler_params=_params("arbitrary"),
    )(qkv, cos, sin, gq, gk, dq, dks[0], dks[1], dvlos[0], dvlos[1], dvhis[0], dvhis[1])


CONV_HALO = 8
CONV_TC = 1024
XBC_OFF = SSM_D_INNER // CONV_TC
DT_OFF = SSM_D_INNER + SSM_CONV_DIM


def _conv_pre(ext, w_ref, b_ref, ts):
    pre = b_ref[...] + w_ref[SSM_CONV - 1:SSM_CONV, :] * ext[CONV_HALO:]
    for kk in range(SSM_CONV - 1):
        pre = pre + w_ref[kk:kk + 1, :] * pltpu.roll(ext, SSM_CONV - 1 - kk, 0)[CONV_HALO:]
    return pre


def _conv_specs(ts):
    tc = CONV_TC
    src = pl.BlockSpec((ts, tc), lambda j, i: (i, XBC_OFF + j))
    halo = pl.BlockSpec((CONV_HALO, tc), lambda j, i: (jnp.maximum(i * (ts // CONV_HALO) - 1, 0), XBC_OFF + j))
    blk = pl.BlockSpec((ts, tc), lambda j, i: (i, j))
    wspec = pl.BlockSpec((SSM_CONV, tc), lambda j, i: (0, j))
    bspec = pl.BlockSpec((1, tc), lambda j, i: (0, j))
    return src, halo, blk, wspec, bspec


def _conv_fwd(zx, w, b):
    s, c = zx.shape[0], SSM_CONV_DIM
    ts = _tile(s, 512, 8)

    def body(u_ref, halo_ref, w_ref, b_ref, o_ref):
        halo = jnp.where(pl.program_id(1) > 0, halo_ref[...], 0.0)
        pre = _conv_pre(jnp.concatenate([halo, u_ref[...]], axis=0), w_ref, b_ref, ts)
        o_ref[...] = pre * _sigmoid(pre)

    src, halo, blk, wspec, bspec = _conv_specs(ts)
    return pl.pallas_call(
        body, name="conv_fwd", grid=(c // CONV_TC, s // ts),
        in_specs=[src, halo, wspec, bspec], out_specs=blk, out_shape=jax.ShapeDtypeStruct((s, c), F32),
        compiler_params=_params("parallel", "parallel"),
    )(zx, zx, w, b)


def _conv_bwd_pre(zx, w, b, dxs, dbm, dcm):
    s, c = zx.shape[0], SSM_CONV_DIM
    ts = _tile(s, 512, 8)
    nx, nb = dxs.shape[1] // CONV_TC, dbm.shape[1] // CONV_TC

    def body(u_ref, halo_ref, w_ref, b_ref, dx_ref, dbm_ref, dcm_ref, dpre_ref, dw_ref, db_ref):
        j, i = pl.program_id(0), pl.program_id(1)
        halo = jnp.where(i > 0, halo_ref[...], 0.0)
        ext = jnp.concatenate([halo, u_ref[...]], axis=0)
        pre = _conv_pre(ext, w_ref, b_ref, ts)
        sg = _sigmoid(pre)
        da = jnp.where(j < nx, dx_ref[...], jnp.where(j < nx + nb, dbm_ref[...], dcm_ref[...]))
        dpre = da * sg * (1.0 + pre * (1.0 - sg))
        dpre_ref[...] = dpre
        rows = [jnp.sum(dpre * pltpu.roll(ext, SSM_CONV - 1 - kk, 0)[CONV_HALO:], axis=0, keepdims=True)
                for kk in range(SSM_CONV - 1)]
        rows.append(jnp.sum(dpre * ext[CONV_HALO:], axis=0, keepdims=True))
        dwp = jnp.concatenate(rows, axis=0)
        dbp = jnp.sum(dpre, axis=0, keepdims=True)

        @pl.when(i == 0)
        def _():
            dw_ref[...] = dwp
            db_ref[...] = dbp

        @pl.when(i > 0)
        def _():
            dw_ref[...] += dwp
            db_ref[...] += dbp

    src, halo, blk, wspec, bspec = _conv_specs(ts)

    def part(lo, n):
        return pl.BlockSpec((ts, CONV_TC), lambda j, i: (jnp.where((j >= lo) & (j < lo + n), i, 0),
                                                         jnp.clip(j - lo, 0, n - 1)))

    return pl.pallas_call(
        body, name="conv_bwd_pre", grid=(c // CONV_TC, s // ts),
        in_specs=[src, halo, wspec, bspec, part(0, nx), part(nx, nb), part(nx + nb, nb)],
        out_specs=[blk, wspec, bspec],
        out_shape=[jax.ShapeDtypeStruct((s, c), F32), jax.ShapeDtypeStruct((SSM_CONV, c), F32),
                   jax.ShapeDtypeStruct((1, c), F32)],
        compiler_params=_params("parallel", "arbitrary"),
    )(zx, zx, w, b, dxs, dbm, dcm)


def _conv_bwd_in(dpre, w, dzx):
    s, c = dpre.shape
    ts, tc = _tile(s, 512, 8), CONV_TC
    ns = s // ts

    def body(d_ref, halo_ref, w_ref, dzx_ref, o_ref):
        del dzx_ref
        halo = jnp.where(pl.program_id(1) < ns - 1, halo_ref[...], 0.0)
        ext = jnp.concatenate([d_ref[...], halo], axis=0)
        du = w_ref[SSM_CONV - 1:SSM_CONV, :] * ext[:ts]
        for kk in range(SSM_CONV - 1):
            du = du + w_ref[kk:kk + 1, :] * pltpu.roll(ext, ts + CONV_HALO - (SSM_CONV - 1 - kk), 0)[:ts]
        o_ref[...] = du.astype(BF16)

    blk = pl.BlockSpec((ts, tc), lambda j, i: (i, j))
    halo = pl.BlockSpec((CONV_HALO, tc), lambda j, i: (jnp.minimum((i + 1) * (ts // CONV_HALO), s // CONV_HALO - 1), j))
    return pl.pallas_call(
        body, name="conv_bwd_in", grid=(c // tc, ns),
        in_specs=[blk, halo, pl.BlockSpec((SSM_CONV, tc), lambda j, i: (0, j)), ANY],
        out_specs=pl.BlockSpec((ts, tc), lambda j, i: (i, XBC_OFF + j)),
        out_shape=jax.ShapeDtypeStruct(dzx.shape, BF16), input_output_aliases={3: 0},
        compiler_params=_params("parallel", "parallel"),
    )(dpre, dpre, w, dzx)


def _ssd_common(dt_ref, dtt_ref, bias_ref, biast_ref, alog_ref, alogt_ref):
    ln = SSM_CHUNK
    raw, rawt = dt_ref[0] + bias_ref[0], dtt_ref[0] + biast_ref[0]
    dt, dtt = _softplus(raw), _softplus(rawt)
    a, at = -jnp.exp(alog_ref[0]), -jnp.exp(alogt_ref[0])
    tri = jnp.where(_iota((ln, ln), 0) >= _iota((ln, ln), 1), 1.0, 0.0).astype(BF16)
    return dict(raw=raw, rawt=rawt, dt=dt, dtt=dtt, a=a, at=at, tri=tri,
                acum=_xdot(tri, dt * a), acumt=_dot_x_nt(dtt * at, tri))


def _ssd_specs(nc, rev):
    cidx = (lambda c: nc - 1 - c) if rev else (lambda c: c)
    ln = SSM_CHUNK
    xs = pl.BlockSpec((ln, SSM_GN), lambda g, c: (cidx(c), g))
    bs = pl.BlockSpec((ln, SSM_STATE), lambda g, c: (cidx(c), SSM_D_INNER // SSM_STATE + g))
    cs = pl.BlockSpec((ln, SSM_STATE), lambda g, c: (cidx(c), SSM_D_INNER // SSM_STATE + SSM_GROUPS + g))
    dt = pl.BlockSpec((1, ln, SSM_HPG), lambda g, c: (g, cidx(c), 0))
    dtt = pl.BlockSpec((1, SSM_HPG, ln), lambda g, c: (g, 0, cidx(c)))
    row = pl.BlockSpec((1, 1, SSM_HPG), lambda g, c: (g, 0, 0))
    col = pl.BlockSpec((1, SSM_HPG, 1), lambda g, c: (g, 0, 0))
    st = pl.BlockSpec((None, None, SSM_GN, SSM_STATE), lambda g, c: (cidx(c), g, 0, 0))
    return xs, bs, cs, dt, dtt, row, col, st


def _head_expand():
    return jnp.where((_iota((SSM_HPG, SSM_GN), 1) >> 6) == _iota((SSM_HPG, SSM_GN), 0), 1.0, 0.0).astype(BF16)


def _head_expand_t():
    return jnp.where((_iota((SSM_GN, SSM_HPG), 0) >> 6) == _iota((SSM_GN, SSM_HPG), 1), 1.0, 0.0).astype(BF16)


def _dot_x_tn(x, m):
    hi, mid, lo = _split3(x)
    return _dot_tn(hi, m) + _dot_tn(mid, m) + _dot_tn(lo, m)


def _ssd_fwd(xbc, dt_g, dt_gt, bias_r, bias_c, alog_r, alog_c, d_r):
    s = xbc.shape[0]
    ln = SSM_CHUNK
    nc = s // ln

    def body(x_ref, b_ref, c_ref, dt_ref, dtt_ref, bias_ref, biast_ref, alog_ref, alogt_ref, d_ref,
             y_ref, st_ref, state):
        @pl.when(pl.program_id(1) == 0)
        def _():
            state[...] = jnp.zeros_like(state)

        cm = _ssd_common(dt_ref, dtt_ref, bias_ref, biast_ref, alog_ref, alogt_ref)
        acum, acumt = cm["acum"], cm["acumt"]
        ex = _head_expand()
        acum_x = _dot_x(acum, ex)
        xv = x_ref[...]
        xdt = xv * _dot_x(cm["dt"], ex)
        xdtb = xdt.astype(BF16)
        bb, cb = b_ref[...].astype(BF16), c_ref[...].astype(BF16)
        cbm = _dot_nt(cb, bb)
        causal = _iota((ln, ln), 0) >= _iota((ln, ln), 1)
        s2 = state[...]
        st_ref[...] = s2
        for r in range(SSM_HPG):
            sl = slice(r * SSM_P, (r + 1) * SSM_P)
            decay = jnp.exp(jnp.where(causal, acum[:, r:r + 1] - acumt[r:r + 1, :], -jnp.inf))
            y_ref[:, sl] = _dot((cbm * decay).astype(BF16), xdtb[:, sl])
        y_ref[...] = (y_ref[...] + _dot_nt(cb, s2.astype(BF16)) * jnp.exp(acum_x) + _dot_x(d_ref[0], ex) * xv)
        last_x = acum_x[ln - 1:ln, :]
        elast = jnp.exp(_xdot(_head_expand_t(), acumt[:, ln - 1:ln]))
        state[...] = s2 * elast + _dot_tn((xdt * jnp.exp(last_x - acum_x)).astype(BF16), bb)

    xs, bs, cs, dts, dtts, row, col, st = _ssd_specs(nc, False)
    return pl.pallas_call(
        body, name="ssd_fwd", grid=(SSM_GROUPS, nc),
        in_specs=[xs, bs, cs, dts, dtts, row, col, row, col, row],
        out_specs=[xs, st],
        out_shape=[jax.ShapeDtypeStruct((s, SSM_D_INNER), F32),
                   jax.ShapeDtypeStruct((nc, SSM_GROUPS, SSM_GN, SSM_STATE), F32)],
        scratch_shapes=[pltpu.VMEM((SSM_GN, SSM_STATE), F32)],
        compiler_params=_params("parallel", "arbitrary"),
    )(xbc, xbc, xbc, dt_g, dt_gt, bias_r, bias_c, alog_r, alog_c, d_r)


def _ssd_bwd(xbc, dt_g, dt_gt, bias_r, bias_c, alog_r, alog_c, d_r, states, dy):
    s = xbc.shape[0]
    ln = SSM_CHUNK
    nc = s // ln

    def body(x_ref, b_ref, c_ref, dt_ref, dtt_ref, bias_ref, biast_ref, alog_ref, alogt_ref, d_ref,
             st_ref, dy_ref, dx_ref, db_ref, dc_ref, ddt_ref, ddtt_ref, dbias_ref, dbiast_ref,
             dalog_ref, dalogt_ref, dd_ref, dstate):
        step = pl.program_id(1)

        @pl.when(step == 0)
        def _():
            dstate[...] = jnp.zeros_like(dstate)

        cm = _ssd_common(dt_ref, dtt_ref, bias_ref, biast_ref, alog_ref, alogt_ref)
        dt, acum, acumt = cm["dt"], cm["acum"], cm["acumt"]
        ex, ext = _head_expand(), _head_expand_t()
        dt_x, acum_x = _dot_x(dt, ex), _dot_x(acum, ex)
        eac_x, to_end_x = jnp.exp(acum_x), jnp.exp(acum_x[ln - 1:ln, :] - acum_x)
        xv, dyv = x_ref[...], dy_ref[...]
        xdt = xv * dt_x
        xdtb, dyb = xdt.astype(BF16), dyv.astype(BF16)
        dyeb = (dyv * eac_x).astype(BF16)
        bb, cb = b_ref[...].astype(BF16), c_ref[...].astype(BF16)
        cbm = _dot_nt(cb, bb)
        s2, ds2 = st_ref[...], dstate[...]
        s2b, ds2b = s2.astype(BF16), ds2.astype(BF16)
        dxdt_state = _dot_nt(bb, ds2b) * to_end_x
        yoff = _dot_nt(cb, s2b) * eac_x
        dc_acc = _dot(dyeb, s2b)
        db_acc = _dot((xdt * to_end_x).astype(BF16), ds2b)
        f_rows = _dot_x2_nt(xdt * dxdt_state, ex)
        elast = jnp.exp(acum[ln - 1:ln, :])
        dlast = (jnp.sum(f_rows, axis=0, keepdims=True)
                 + elast * jnp.sum(_dot_x_tn(ds2 * s2, ext), axis=0, keepdims=True))
        is_last = _iota((ln, 1), 0) == ln - 1
        dac_rows = _dot_x2_nt(dyv * yoff, ex) - f_rows + jnp.where(is_last, dlast, 0.0)
        dstate[...] = ds2 * jnp.exp(_xdot(ext, acumt[:, ln - 1:ln])) + _dot_tn(dyeb, cb)
        causal = _iota((ln, ln), 0) >= _iota((ln, ln), 1)
        lane8 = _iota((ln, SSM_HPG), 1)
        sub8 = _iota((SSM_HPG, ln), 0)
        dcb = jnp.zeros((ln, ln), F32)
        dac_cols = jnp.zeros((SSM_HPG, ln), F32)
        for r in range(SSM_HPG):
            sl = slice(r * SSM_P, (r + 1) * SSM_P)
            decay = jnp.exp(jnp.where(causal, acum[:, r:r + 1] - acumt[r:r + 1, :], -jnp.inf))
            dx_ref[:, sl] = _dot_tn((cbm * decay).astype(BF16), dyb[:, sl])
            dcb_r = _dot_nt(dyb[:, sl], xdtb[:, sl]) * decay
            dcb = dcb + dcb_r
            e = dcb_r * cbm
            dac_rows = dac_rows + jnp.where(lane8 == r, jnp.sum(e, axis=-1, keepdims=True), 0.0)
            dac_cols = dac_cols + jnp.where(sub8 == r, jnp.sum(e, axis=0, keepdims=True), 0.0)
        dxdt = dx_ref[...] + dxdt_state
        ddt_all = _dot_x2_nt(dxdt * xv, ex)
        dd_all = jnp.sum(_dot_x2_nt(dyv * xv, ex), axis=0, keepdims=True)
        dx_ref[...] = dxdt * dt_x + _dot_x(d_ref[0], ex) * dyv
        dcbb = dcb.astype(BF16)
        dc_ref[...] = dc_acc + _dot(dcbb, bb)
        db_ref[...] = db_acc + _dot_tn(dcbb, cb)
        triu = jnp.where(_iota((ln, ln), 0) <= _iota((ln, ln), 1), 1.0, 0.0).astype(BF16)
        g_rows = _xdot(triu, dac_rows)
        g_cols = _dot_x(dac_cols, cm["tri"])
        d_rows = (ddt_all + g_rows * cm["a"]) * _sigmoid(cm["raw"])
        d_cols = -(g_cols * cm["at"]) * _sigmoid(cm["rawt"])
        ddt_ref[0] = d_rows
        ddtt_ref[0] = d_cols
        parts = (jnp.sum(d_rows, axis=0, keepdims=True), jnp.sum(d_cols, axis=1, keepdims=True),
                 jnp.sum(g_rows * dt, axis=0, keepdims=True) * cm["a"],
                 -jnp.sum(g_cols * cm["dtt"], axis=1, keepdims=True) * cm["at"], dd_all)
        outs = (dbias_ref, dbiast_ref, dalog_ref, dalogt_ref, dd_ref)

        @pl.when(step == 0)
        def _():
            for o_ref, p in zip(outs, parts):
                o_ref[0] = p

        @pl.when(step > 0)
        def _():
            for o_ref, p in zip(outs, parts):
                o_ref[0] += p

    xs, bs, cs, dts, dtts, row, col, st = _ssd_specs(nc, True)
    grp = pl.BlockSpec((ln, SSM_STATE), lambda g, c: (nc - 1 - c, g))
    rows = jax.ShapeDtypeStruct((SSM_GROUPS, 1, SSM_HPG), F32)
    cols = jax.ShapeDtypeStruct((SSM_GROUPS, SSM_HPG, 1), F32)
    return pl.pallas_call(
        body, name="ssd_bwd", grid=(SSM_GROUPS, nc),
        in_specs=[xs, bs, cs, dts, dtts, row, col, row, col, row, st, xs],
        out_specs=[xs, grp, grp, dts, dtts, row, col, row, col, row],
        out_shape=[jax.ShapeDtypeStruct((s, SSM_D_INNER), F32),
                   jax.ShapeDtypeStruct((s, SSM_GROUPS * SSM_STATE), F32),
                   jax.ShapeDtypeStruct((s, SSM_GROUPS * SSM_STATE), F32),
                   jax.ShapeDtypeStruct((SSM_GROUPS, s, SSM_HPG), F32),
                   jax.ShapeDtypeStruct((SSM_GROUPS, SSM_HPG, s), F32), rows, cols, rows, cols, rows],
        scratch_shapes=[pltpu.VMEM((SSM_GN, SSM_STATE), F32)],
        compiler_params=_params("parallel", "arbitrary"),
    )(xbc, xbc, xbc, dt_g, dt_gt, bias_r, bias_c, alog_r, alog_c, d_r, states, dy)


def _gate_norm_fwd(y, zx, g):
    s = y.shape[0]
    ts = _tile(s, 512, 8)

    def body(y_ref, z_ref, g_ref, o_ref):
        zv = z_ref[...]
        yg = y_ref[...] * (zv * _sigmoid(zv))
        r = lax.rsqrt(jnp.mean(yg * yg, axis=-1, keepdims=True) + EPS)
        o_ref[...] = (yg * r * g_ref[...]).astype(BF16)

    blk = pl.BlockSpec((ts, SSM_GN), lambda j, i: (i, j))
    vec = pl.BlockSpec((1, SSM_GN), lambda j, i: (0, j))
    return pl.pallas_call(
        body, name="gate_norm_fwd", grid=(SSM_GROUPS, s // ts), in_specs=[blk, blk, vec], out_specs=blk,
        out_shape=jax.ShapeDtypeStruct((s, SSM_D_INNER), BF16), compiler_params=_params("parallel", "parallel"),
    )(y, zx, g)


def _gate_norm_bwd(y, zx, g, dout):
    s = y.shape[0]
    ts = _tile(s, 512, 8)

    def body(y_ref, z_ref, g_ref, do_ref, dy_ref, dz_ref, dg_ref):
        yv, zv, dov = y_ref[...], z_ref[...], do_ref[...].astype(F32)
        sg = _sigmoid(zv)
        silu = zv * sg
        yg = yv * silu
        r = lax.rsqrt(jnp.mean(yg * yg, axis=-1, keepdims=True) + EPS)
        ygn = yg * r
        part = jnp.sum(dov * ygn, axis=0, keepdims=True)

        @pl.when(pl.program_id(1) == 0)
        def _():
            dg_ref[...] = part

        @pl.when(pl.program_id(1) > 0)
        def _():
            dg_ref[...] += part

        dn = dov * g_ref[...]
        dyg = r * (dn - ygn * jnp.mean(dn * ygn, axis=-1, keepdims=True))
        dy_ref[...] = dyg * silu
        dz_ref[...] = (dyg * yv * sg * (1.0 + zv * (1.0 - sg))).astype(BF16)

    blk = pl.BlockSpec((ts, SSM_GN), lambda j, i: (i, j))
    vec = pl.BlockSpec((1, SSM_GN), lambda j, i: (0, j))
    return pl.pallas_call(
        body, name="gate_norm_bwd", grid=(SSM_GROUPS, s // ts), in_specs=[blk, blk, vec, blk],
        out_specs=[blk, blk, vec],
        out_shape=[jax.ShapeDtypeStruct((s, SSM_D_INNER), F32), jax.ShapeDtypeStruct((s, SSM_IN_PAD), BF16),
                   jax.ShapeDtypeStruct((1, SSM_D_INNER), F32)],
        compiler_params=_params("parallel", "arbitrary"),
    )(y, zx, g, dout)


def _rope_tables(positions):
    inv_freq = ROPE_THETA ** (-jnp.arange(0, ATT_HEAD_DIM, 2, dtype=F32) / ATT_HEAD_DIM)
    ang = positions.astype(F32)[:, None] * inv_freq
    return jnp.tile(jnp.cos(ang), (1, 4)), jnp.tile(jnp.sin(ang), (1, 4))


def _group_views(v):
    return v.reshape(SSM_GROUPS, 1, SSM_HPG), v.reshape(SSM_GROUPS, SSM_HPG, 1)


def _ffn_fwd(run, x, norm_g, wg, wu, wd, tag):
    h = _rms_fwd(x, norm_g, f"ffn_norm_{tag}")
    g = run(f"ffn_gate_{tag}", _mm, h, wg, "nn", b_cols=True, out_dtype=BF16)

    def act(uv, gv):
        gv = gv.astype(F32)
        return uv, gv * _sigmoid(gv) * uv

    u, a = run(f"ffn_up_{tag}", _mm, h, wu, "nn", b_cols=True, fuse=(act, [g], [BF16, BF16]))
    return run(f"ffn_down_{tag}", _mm, a, wd, "nn", add=x), (h, g, u, a)


def _ffn_bwd(run, mats, x, norm_g, wg, wu, wd, saved, dout, tag):
    h, g, u, a = saved

    def act_bwd(da, gv, uv):
        gv, uv = gv.astype(F32), uv.astype(F32)
        sg = _sigmoid(gv)
        return da * uv * sg * (1.0 + gv * (1.0 - sg)), da * (gv * sg)

    dg, du = run(f"ffn_down_dx_{tag}", _mm, dout, wd, "nt", rows=FUSED_ROWS,
                 fuse=(act_bwd, [g, u], [BF16, BF16]))
    dwd = run(f"ffn_down_dw_{tag}", _mm, a, dout, "tn", out_dtype=BF16)
    mats[("ffn_w_down", tag)] = dwd.reshape(N_SHARDS, dwd.shape[0] // N_SHARDS, dwd.shape[1])
    mats[("ffn_w_gate", tag)] = run(f"ffn_gate_dw_{tag}", _mm, h, dg, "tn", out_dtype=BF16, out_cols=True)
    mats[("ffn_w_up", tag)] = run(f"ffn_up_dw_{tag}", _mm, h, du, "tn", out_dtype=BF16, out_cols=True)
    dh = run(f"ffn_gate_dx_{tag}", _mm, dg, wg, "nt", b_cols=True)
    dh = run(f"ffn_up_dx_{tag}", _mm, du, wu, "nt", add=dh, b_cols=True)
    return _rms_bwd(x, norm_g, dh, dout, f"ffn_norm_bwd_{tag}")


class _Hook:
    def __init__(self, make, done):
        self.make, self.done = make, done


class _SemView:
    def __init__(self, sems, off):
        self.sems, self.off, self.at = sems, off, self

    def __getitem__(self, k):
        return self.sems.at[self.off + k]


def _both(h1, h2):
    split = {}

    def make():
        a, b = h1.make(), h2.make()
        na_in, na_out, na_sems = len(a["arrays"]), len(a["out_shapes"]), a["n_sems"]
        split["n"] = na_out

        def build(cin, cout, send_sems, recv_sems):
            return (a["build"](cin[:na_in], cout[:na_out], send_sems, recv_sems)
                    + b["build"](cin[na_in:], cout[na_out:], _SemView(send_sems, na_sems), _SemView(recv_sems, na_sems)))

        aliases = dict(a.get("aliases", {}))
        aliases.update({na_in + i: na_out + o for i, o in b.get("aliases", {}).items()})
        return dict(build=build, arrays=list(a["arrays"]) + list(b["arrays"]),
                    out_shapes=list(a["out_shapes"]) + list(b["out_shapes"]), n_sems=na_sems + b["n_sems"],
                    aliases=aliases)

    def done(res):
        h1.done(res[:split["n"]])
        h2.done(res[split["n"]:])

    return _Hook(make, done)


def _local_step(x, positions, target, w, hooks=None, mats=None):
    hooks = {} if hooks is None else hooks
    mats = {} if mats is None else mats

    def run(name, fn, *args, **kw):
        hook = hooks.get(name)
        if hook is None:
            return fn(*args, name=name, **kw)
        res, carried = fn(*args, name=name, carry=hook.make(), **kw)
        hook.done(carried)
        return res

    cos, sin = _rope_tables(positions)
    row = lambda v: v.reshape(1, -1)
    gq, gk = jnp.tile(row(w["attn_q_norm"]), (1, 2)), jnp.tile(row(w["attn_k_norm"]), (1, 2))
    sinks = w["attn_sinks"].reshape(-1)
    s = x.shape[0]
    row_stack = lambda g: g.reshape(N_SHARDS, g.shape[0] // N_SHARDS, g.shape[1])

    h0 = _rms_fwd(x, row(w["mixer_norm"][0]), "mixer_norm_0")
    qkv = run("attn_qkv", _mm, h0, w["attn_w_qkv"], "nn", b_cols=True)
    q, kk, vlo, vhi = _attn_prep(qkv, cos, sin, gq, gk)
    o = run("attn_fwd", _attn_fwd, q, kk, vlo, vhi, sinks)
    x1 = run("attn_out", _mm, o, w["attn_w_o"], "nn", add=x)
    ffn_w = lambda l: (row(w["ffn_norm"][l]), w["ffn_w_gate"][l], w["ffn_w_up"][l], w["ffn_w_down"][l])
    x2, ffn0 = _ffn_fwd(run, x1, *ffn_w(0), 0)

    h2 = _rms_fwd(x2, row(w["mixer_norm"][1]), "mixer_norm_1")
    zx = run("ssm_in", _mm, h2, w["ssm_w_in"], "nn")
    dt_g = zx[:, DT_OFF:DT_OFF + SSM_HEADS].reshape(s, SSM_GROUPS, SSM_HPG).transpose(1, 0, 2)
    dt_gt = dt_g.transpose(0, 2, 1)
    bias_r, bias_c = _group_views(w["ssm_dt_bias"].reshape(-1))
    alog_r, alog_c = _group_views(w["ssm_a_log"].reshape(-1))
    d_r, _ = _group_views(w["ssm_d"].reshape(-1))
    xbc = _conv_fwd(zx, w["ssm_conv_w"], row(w["ssm_conv_b"]))
    ssd_args = (xbc, dt_g, dt_gt, bias_r, bias_c, alog_r, alog_c, d_r)
    y, states = _ssd_fwd(*ssd_args)
    yn = _gate_norm_fwd(y, zx, row(w["ssm_norm"]))
    x3 = run("ssm_out", _mm, yn, w["ssm_w_out"], "nn", add=x2)
    x4, ffn1 = _ffn_fwd(run, x3, *ffn_w(1), 1)

    loss_row, dx4 = _loss_fwd_bwd(x4, target)

    dx3, dfn1 = _ffn_bwd(run, mats, x3, *ffn_w(1), ffn1, dx4, 1)
    dyn = run("ssm_out_dx", _mm, dx3, w["ssm_w_out"], "nt")
    mats[("ssm_w_out", 0)] = row_stack(run("ssm_out_dw", _mm, yn, dx3, "tn", out_dtype=BF16))
    dy, dzx, dssm_norm = _gate_norm_bwd(y, zx, row(w["ssm_norm"]), dyn)
    dxs, db, dc, ddt_g, ddt_gt, dbias, dbias_t, dalog, dalog_t, dd = _ssd_bwd(*ssd_args, states, dy)
    ddt_g = ddt_g + ddt_gt.transpose(0, 2, 1)
    dpre, dconv_w, dconv_b = _conv_bwd_pre(zx, w["ssm_conv_w"], row(w["ssm_conv_b"]), dxs, db, dc)
    dzx = _conv_bwd_in(dpre, w["ssm_conv_w"], dzx)
    ddt_pad = jnp.pad(ddt_g.transpose(1, 0, 2).reshape(s, SSM_HEADS), ((0, 0), (0, SSM_IN_PAD - SSM_IN)))
    dzx = lax.dynamic_update_slice(dzx, ddt_pad.astype(BF16), (0, DT_OFF))
    dw_in = run("ssm_in_dw", _mm, h2, dzx, "tn", out_dtype=BF16)
    in_shard = SSM_IN // N_SHARDS
    mats[("ssm_w_in", 0)] = jnp.stack([dw_in[:, i * in_shard:(i + 1) * in_shard] for i in range(N_SHARDS)])
    dh2 = run("ssm_in_dx", _mm, dzx, w["ssm_w_in"], "nt")
    dx2, dmn1 = _rms_bwd(x2, row(w["mixer_norm"][1]), dh2, dx3, "mixer_norm_bwd_1")

    dx1, dfn0 = _ffn_bwd(run, mats, x1, *ffn_w(0), ffn0, dx2, 0)
    do = run("attn_out_dx", _mm, dx1, w["attn_w_o"], "nt", out_dtype=BF16)
    mats[("attn_w_o", 0)] = row_stack(run("attn_out_dw", _mm, o, dx1, "tn", out_dtype=BF16))
    dq, dkc, dkp, dvloc, dvlop, dvhic, dvhip, dsink = run("attn_bwd", _attn_bwd, q, kk, vlo, vhi, sinks, do)
    dqkv, dgq, dgk = _attn_prep_bwd(qkv, cos, sin, gq, gk, dq, (dkc, dkp), (dvloc, dvlop), (dvhic, dvhip))
    mats[("attn_w_qkv", 0)] = run("attn_qkv_dw", _mm, h0, dqkv, "tn", out_dtype=BF16, out_cols=True)
    dh0 = run("attn_qkv_dx", _mm, dqkv, w["attn_w_qkv"], "nt", b_cols=True)
    dx0, dmn0 = _rms_bwd(x, row(w["mixer_norm"][0]), dh0, dx1, "mixer_norm_bwd_0")

    fold = lambda v: v[0, :ATT_HEAD_DIM] + v[0, ATT_HEAD_DIM:]
    grads = {
        "mixer_norm": jnp.concatenate([dmn0, dmn1], axis=0),
        "ffn_norm": jnp.concatenate([dfn0, dfn1], axis=0),
        "attn_q_norm": fold(dgq), "attn_k_norm": fold(dgk),
        "attn_sinks": dsink[:, :, 0].reshape(-1),
        "ssm_conv_w": dconv_w, "ssm_conv_b": dconv_b.reshape(-1),
        "ssm_dt_bias": dbias.reshape(-1) + dbias_t.reshape(-1),
        "ssm_a_log": dalog.reshape(-1) + dalog_t.reshape(-1), "ssm_d": dd.reshape(-1),
        "ssm_norm": dssm_norm.reshape(-1),
    }
    return loss_row[0, 0], dx0, grads


OTHER_CHIPS = ((1, 0), (0, 1), (1, 1))


def _position():
    return lax.axis_index("x"), lax.axis_index("y"), lax.axis_index("c")


def _gather_shards(weights, layers):
    n_in, n_mat = len(weights), len(layers)

    def body(*refs):
        p, out = refs[:n_in], refs[n_in:n_in + n_mat]
        send_sems, recv_sems = refs[n_in + n_mat:]
        x, y, c = _position()
        me, sibling = (x, y, c), (x, y, 1 - c)
        chips = [(x ^ fx, y ^ fy) for fx, fy in OTHER_CHIPS]

        def rows(e, px, py, pc):
            half = out[e].shape[1] // 2
            return out[e].at[2 * px + py, pl.ds(pc * half, half), :]

        def copy(k, e, block, to, src=None):
            return pltpu.make_async_remote_copy(
                src_ref=rows(e, *block) if src is None else src, dst_ref=rows(e, *block),
                send_sem=send_sems.at[k * n_mat + e], recv_sem=recv_sems.at[k * n_mat + e],
                device_id=to, device_id_type=MESH)

        def own(e):
            i, l = layers[e]
            return pltpu.make_async_remote_copy(
                src_ref=p[i].at[l], dst_ref=out[e].at[2 * x + y], send_sem=send_sems.at[6 * n_mat + e],
                recv_sem=recv_sems.at[6 * n_mat + e], device_id=sibling, device_id_type=MESH)

        first, passed = [], []
        for e, (i, l) in enumerate(layers):
            half = out[e].shape[1] // 2
            first.append([copy(j, e, me, (*chip, c), src=p[i].at[l, pl.ds(c * half, half), :])
                          for j, chip in enumerate(chips)])
            for cp in first[-1]:
                cp.start()
        for e in range(n_mat):
            own(e).start()
        for e in range(n_mat):
            passed.append([copy(3 + j, e, (*chip, c), sibling) for j, chip in enumerate(chips)])
            for j, chip in enumerate(chips):
                copy(j, e, (*chip, c), me).wait_recv()
                passed[e][j].start()
        for e in range(n_mat):
            own(e).wait()
            for j, chip in enumerate(chips):
                copy(3 + j, e, (*chip, 1 - c), me).wait_recv()
        for e in range(n_mat):
            for cp in first[e] + passed[e]:
                cp.wait_send()

    return pl.pallas_call(
        body, name="gather_weights", in_specs=[ANY] * n_in, out_specs=[ANY] * n_mat,
        out_shape=[jax.ShapeDtypeStruct((N_SHARDS,) + weights[i].shape[1:], weights[i].dtype) for i, _ in layers],
        scratch_shapes=[_sems(7 * n_mat), _sems(7 * n_mat)],
    )(*weights)


def _all_gather8(block, name):
    m_per, n = block.shape

    def body(x_ref, out_ref, send_sems, recv_sems, local_sem):
        x, y, c = _position()
        me, sibling = (x, y, c), (x, y, 1 - c)
        chips = [(x ^ fx, y ^ fy) for fx, fy in OTHER_CHIPS]

        def rows(px, py, pc):
            return out_ref.at[pl.ds((4 * px + 2 * py + pc) * m_per, m_per), :]

        def copy(k, blk, to, src=None):
            return pltpu.make_async_remote_copy(
                src_ref=rows(*blk) if src is None else src, dst_ref=rows(*blk),
                send_sem=send_sems.at[k], recv_sem=recv_sems.at[k], device_id=to, device_id_type=MESH)

        mine = pltpu.make_async_copy(x_ref, rows(*me), local_sem)
        mine.start()
        first = [copy(0, me, sibling, src=x_ref)]
        first += [copy(1 + j, me, (*chip, c), src=x_ref) for j, chip in enumerate(chips)]
        for cp in first:
            cp.start()
        passed = [copy(4 + j, (*chip, c), sibling) for j, chip in enumerate(chips)]
        for j, chip in enumerate(chips):
            copy(1 + j, (*chip, c), me).wait_recv()
            passed[j].start()
        copy(0, sibling, me).wait_recv()
        for j, chip in enumerate(chips):
            copy(4 + j, (*chip, 1 - c), me).wait_recv()
        for cp in first + passed:
            cp.wait_send()
        mine.wait()

    return pl.pallas_call(
        body, name=name, out_shape=jax.ShapeDtypeStruct((N_DEV * m_per, n), block.dtype),
        in_specs=[pl.BlockSpec(memory_space=pltpu.VMEM)], out_specs=pl.BlockSpec(memory_space=pltpu.VMEM),
        scratch_shapes=[_sems(7), _sems(7), pltpu.SemaphoreType.DMA],
    )(block)


def _exchange(carry, name):
    n_in, n_out = len(carry["arrays"]), len(carry["out_shapes"])

    def body(*refs):
        copies = carry["build"](refs[:n_in], refs[n_in:n_in + n_out], refs[-2], refs[-1])
        for cp in copies:
            cp.start()
        for cp in copies:
            cp.wait()

    return pl.pallas_call(
        body, name=name, in_specs=[ANY] * n_in, out_specs=[ANY] * n_out, out_shape=list(carry["out_shapes"]),
        input_output_aliases=dict(carry.get("aliases", {})),
        scratch_shapes=[_sems(carry["n_sems"]), _sems(carry["n_sems"])],
    )(*carry["arrays"])


def _remote(src, dst, send_sems, recv_sems, k, to):
    return pltpu.make_async_remote_copy(src_ref=src, dst_ref=dst, send_sem=send_sems.at[k], recv_sem=recv_sems.at[k],
                                        device_id=to, device_id_type=MESH)


def _gather_over_ici(blocks, layers):
    def build(p, out, send_sems, recv_sems):
        x, y, c = _position()
        copies = []
        for e, l in enumerate(layers):
            half = out[e].shape[1] // 2
            rows = pl.ds(c * half, half)
            for j, (fx, fy) in enumerate(OTHER_CHIPS):
                copies.append(_remote(p[e].at[l, rows, :], out[e].at[2 * x + y, rows, :], send_sems, recv_sems,
                                      3 * e + j, (x ^ fx, y ^ fy, c)))
        return copies

    shapes = [jax.ShapeDtypeStruct((N_SHARDS,) + b.shape[1:], b.dtype) for b in blocks]
    return dict(build=build, arrays=list(blocks), out_shapes=shapes, n_sems=3 * len(layers))


def _gather_over_d2d(stacks, blocks, layers):
    n = len(stacks)

    def build(refs, out, send_sems, recv_sems):
        p = refs[n:]
        x, y, c = _position()
        sibling = (x, y, 1 - c)
        copies = []
        for e, l in enumerate(layers):
            half = out[e].shape[1] // 2
            for j, (fx, fy) in enumerate(OTHER_CHIPS):
                rows = out[e].at[2 * (x ^ fx) + (y ^ fy), pl.ds(c * half, half), :]
                copies.append(_remote(rows, rows, send_sems, recv_sems, 4 * e + j, sibling))
            copies.append(_remote(p[e].at[l], out[e].at[2 * x + y], send_sems, recv_sems, 4 * e + 3, sibling))
        return copies

    shapes = [jax.ShapeDtypeStruct(s.shape, s.dtype) for s in stacks]
    return dict(build=build, arrays=list(stacks) + list(blocks), out_shapes=shapes, n_sems=4 * n,
                aliases={i: i for i in range(n)})


def _grads_to_sibling(stacks):
    def build(g, out, send_sems, recv_sems):
        x, y, c = _position()
        copies = []
        for e in range(len(stacks)):
            half = g[e].shape[1] // 2
            copies.append(_remote(g[e].at[:, pl.ds((1 - c) * half, half), :], out[e], send_sems, recv_sems, e,
                                  (x, y, 1 - c)))
        return copies

    shapes = [jax.ShapeDtypeStruct((N_SHARDS, g.shape[1] // 2, g.shape[2]), g.dtype) for g in stacks]
    return dict(build=build, arrays=list(stacks), out_shapes=shapes, n_sems=len(stacks))


def _grads_to_owners(partials):
    def build(p, out, send_sems, recv_sems):
        x, y, c = _position()
        copies = []
        for e in range(len(partials)):
            for k, (fx, fy) in enumerate(OTHER_CHIPS):
                px, py = x ^ fx, y ^ fy
                copies.append(_remote(p[e].at[2 * px + py], out[e].at[k], send_sems, recv_sems, 3 * e + k,
                                      (px, py, c)))
        return copies

    shapes = [jax.ShapeDtypeStruct((len(OTHER_CHIPS),) + p.shape[1:], p.dtype) for p in partials]
    return dict(build=build, arrays=list(partials), out_shapes=shapes, n_sems=3 * len(partials))


def _share_halves(grads, layers):
    def build(_, out, send_sems, recv_sems):
        x, y, c = _position()
        copies = []
        for e, (i, l) in enumerate(layers):
            half = out[i].shape[1] // 2
            rows = out[i].at[l, pl.ds(c * half, half), :]
            copies.append(_remote(rows, rows, send_sems, recv_sems, e, (x, y, 1 - c)))
        return copies

    return dict(build=build, arrays=list(grads), out_shapes=[jax.ShapeDtypeStruct(g.shape, g.dtype) for g in grads],
                n_sems=len(layers), aliases={i: i for i in range(len(grads))})


ADD_BLOCK_ELEMS = 1 << 19


def _add_rows(half, cols):
    return _tile(half, max(16, ADD_BLOCK_ELEMS // cols // 16 * 16), 16)


def _add_pair(stack, recv, c_idx, name):
    _, half, cols = recv.shape
    tr = _add_rows(half, cols)
    nt = half // tr

    def body(c_ref, a_ref, b_ref, o_ref):
        o_ref[...] = (a_ref[...].astype(F32) + b_ref[...].astype(F32)).astype(o_ref.dtype)

    blk = pl.BlockSpec((None, tr, cols), lambda s, i, c_ref: (s, i, 0))
    return pl.pallas_call(
        body, name=name,
        grid_spec=pltpu.PrefetchScalarGridSpec(
            num_scalar_prefetch=1, grid=(N_SHARDS, nt),
            in_specs=[pl.BlockSpec((None, tr, cols), lambda s, i, c_ref: (s, c_ref[0] * nt + i, 0)), blk],
            out_specs=blk),
        out_shape=jax.ShapeDtypeStruct(recv.shape, recv.dtype),
        compiler_params=_params("parallel", "parallel"),
    )(c_idx, stack, recv)


def _add_owned(partial, recv, sc_idx, layer, shape, into, name):
    _, half, cols = partial.shape
    tr = _add_rows(half, cols)
    nt = half // tr

    def body(sc_ref, a_ref, r0_ref, r1_ref, r2_ref, *rest):
        o_ref = rest[-1]
        o_ref[...] = (((a_ref[...].astype(F32) + r0_ref[...].astype(F32)) + r1_ref[...].astype(F32))
                      + r2_ref[...].astype(F32))

    slot = lambda k: pl.BlockSpec((None, tr, cols), lambda i, sc_ref: (k, i, 0))
    has_into = into is not None
    return pl.pallas_call(
        body, name=name,
        grid_spec=pltpu.PrefetchScalarGridSpec(
            num_scalar_prefetch=1, grid=(nt,),
            in_specs=[pl.BlockSpec((None, tr, cols), lambda i, sc_ref: (sc_ref[0], i, 0)), slot(0), slot(1), slot(2)]
            + ([ANY] if has_into else []),
            out_specs=pl.BlockSpec((None, tr, cols), lambda i, sc_ref: (layer, sc_ref[1] * nt + i, 0))),
        out_shape=jax.ShapeDtypeStruct(shape, F32),
        input_output_aliases={5: 0} if has_into else {},
        compiler_params=_params("parallel"),
    )(*((sc_idx, partial, recv, recv, recv) + ((into,) if has_into else ())))


def _sum8(gathered):
    m = gathered.shape[0] // N_DEV

    def body(g_ref, o_ref):
        total = g_ref[0:m, :]
        for d in range(1, N_DEV):
            total = total + g_ref[d * m:(d + 1) * m, :]
        o_ref[...] = total

    return pl.pallas_call(
        body, name="small_grads_sum", out_shape=jax.ShapeDtypeStruct((m, LANES), F32),
        in_specs=[pl.BlockSpec(memory_space=pltpu.VMEM)], out_specs=pl.BlockSpec(memory_space=pltpu.VMEM),
    )(gathered)


ADAMW_BLOCK_ELEMS = 1 << 18


def _adamw(w, g, m, v, name):
    l, r, cols = w.shape
    tr = _tile(r, max(8, ADAMW_BLOCK_ELEMS // cols // 8 * 8), 8)

    def body(w_ref, g_ref, m_ref, v_ref, go_ref, d_ref, nm_ref, nv_ref):
        gv = g_ref[...]
        go_ref[...] = gv
        nm = ADAM_B1 * m_ref[...] + (1.0 - ADAM_B1) * gv
        nv = ADAM_B2 * v_ref[...] + (1.0 - ADAM_B2) * jnp.square(gv)
        m_hat = nm / (1.0 - ADAM_B1 ** ADAM_STEP)
        v_hat = nv / (1.0 - ADAM_B2 ** ADAM_STEP)
        d_ref[...] = -ADAM_LR * (m_hat / (jnp.sqrt(v_hat) + ADAM_EPS) + ADAM_WD * w_ref[...])
        nm_ref[...] = nm
        nv_ref[...] = nv

    blk = pl.BlockSpec((None, tr, cols), lambda a, i: (a, i, 0))
    return pl.pallas_call(
        body, name=name, grid=(l, r // tr), in_specs=[blk] * 4, out_specs=[blk] * 4,
        out_shape=[jax.ShapeDtypeStruct(w.shape, F32)] * 4, compiler_params=_params("parallel", "parallel"),
    )(w, g, m, v)


WEIGHTS = ("mixer_norm", "ffn_norm", "attn_w_qkv", "attn_q_norm", "attn_k_norm", "attn_sinks", "attn_w_o",
           "ssm_w_in", "ssm_conv_w", "ssm_conv_b", "ssm_dt_bias", "ssm_a_log", "ssm_d", "ssm_norm", "ssm_w_out",
           "ffn_w_gate", "ffn_w_up", "ffn_w_down")
BIG = ("attn_w_qkv", "attn_w_o", "ffn_w_gate", "ffn_w_up", "ffn_w_down", "ssm_w_in", "ssm_w_out")
MATRICES = (("attn_w_qkv", 0), ("attn_w_o", 0), ("ffn_w_gate", 0), ("ffn_w_up", 0), ("ffn_w_down", 0),
            ("ssm_w_in", 0), ("ssm_w_out", 0), ("ffn_w_gate", 1), ("ffn_w_up", 1), ("ffn_w_down", 1))
MATRIX_LAYERS = tuple((BIG.index(n), l) for n, l in MATRICES)
GROUPS = {"attn": MATRICES[0:2], "ffn0": MATRICES[2:5], "ssm": MATRICES[5:7], "ffn1": MATRICES[7:10]}
SMALL_SHARDED = ("ssm_conv_w", "ssm_conv_b", "ssm_norm")
SMALL = tuple(n for n in WEIGHTS if n not in BIG)


def _pack_rows(parts, row_unit=8):
    flat = jnp.concatenate([p.reshape(-1) for p in parts])
    pad = (-flat.shape[0]) % (LANES * row_unit)
    return jnp.pad(flat, (0, pad)).reshape(-1, LANES)


def _unpack(flat, shapes):
    out, off = [], 0
    for shp in shapes:
        size = math.prod(shp)
        out.append(flat[off:off + size].reshape(shp))
        off += size
    return out


def kernel(x, positions, mixer_norm, ffn_norm, attn_w_qkv, attn_q_norm, attn_k_norm, attn_sinks, attn_w_o, ssm_w_in, ssm_conv_w, ssm_conv_b, ssm_dt_bias, ssm_a_log, ssm_d, ssm_norm, ssm_w_out, ffn_w_gate, ffn_w_up, ffn_w_down, loss_target, m_mixer_norm, m_ffn_norm, m_attn_w_qkv, m_attn_q_norm, m_attn_k_norm, m_attn_sinks, m_attn_w_o, m_ssm_w_in, m_ssm_conv_w, m_ssm_conv_b, m_ssm_dt_bias, m_ssm_a_log, m_ssm_d, m_ssm_norm, m_ssm_w_out, m_ffn_w_gate, m_ffn_w_up, m_ffn_w_down, v_mixer_norm, v_ffn_norm, v_attn_w_qkv, v_attn_q_norm, v_attn_k_norm, v_attn_sinks, v_attn_w_o, v_ssm_w_in, v_ssm_conv_w, v_ssm_conv_b, v_ssm_dt_bias, v_ssm_a_log, v_ssm_d, v_ssm_norm, v_ssm_w_out, v_ffn_w_gate, v_ffn_w_up, v_ffn_w_down):
    args = locals()
    w = {n: args[n] for n in WEIGHTS}
    m = {n: args["m_" + n] for n in WEIGHTS}
    v = {n: args["v_" + n] for n in WEIGHTS}
    ax, ay, ac = lax.axis_index("x"), lax.axis_index("y"), lax.axis_index("c")
    shard = 2 * ax + ay

    wb = {n: w[n].astype(BF16) for n in BIG}
    wl, hooks = {"ffn_w_gate": [None, None], "ffn_w_up": [None, None], "ffn_w_down": [None, None]}, {}

    def gathered(keys, stacks):
        for (n, l), st in zip(keys, stacks):
            if n == "ssm_w_in":
                wl[n] = jnp.concatenate([st[i] for i in range(N_SHARDS)]
                                        + [jnp.zeros((st.shape[1], SSM_IN_PAD - SSM_IN), BF16)], axis=1)
            elif n in ("ffn_w_gate", "ffn_w_up"):
                wl[n][l] = st
            elif n == "ffn_w_down":
                wl[n][l] = st.reshape(st.shape[0] * st.shape[1], st.shape[2])
            elif n == "attn_w_qkv":
                wl[n] = st
            else:
                wl[n] = st.reshape(st.shape[0] * st.shape[1], st.shape[2])

    def behind(name, hook):
        hooks[name] = _both(hooks[name], hook) if name in hooks else hook

    def gather_behind(keys, first_leg, second_leg):
        blocks, layers, got = [wb[n] for n, _ in keys], [l for _, l in keys], {}
        behind(first_leg, _Hook(lambda: _gather_over_ici(blocks, layers), lambda res: got.update(stacks=res)))
        behind(second_leg, _Hook(lambda: _gather_over_d2d(got["stacks"], blocks, layers),
                                 lambda res: gathered(keys, res)))

    gathered(GROUPS["attn"], _gather_shards([wb[n] for n, _ in GROUPS["attn"]],
                                            [(e, l) for e, (_, l) in enumerate(GROUPS["attn"])]))
    gather_behind(GROUPS["ffn0"], "attn_fwd", "attn_out")
    gather_behind(GROUPS["ssm"][:1], "ffn_gate_0", "ffn_up_0")
    gather_behind(GROUPS["ssm"][1:], "ffn_up_0", "ffn_down_0")
    gather_behind(GROUPS["ffn1"], "ssm_in", "ssm_out")
    small_shapes = [w[n].shape for n in SMALL_SHARDED]
    small_all = _all_gather8(_pack_rows([w[n] for n in SMALL_SHARDED]), "gather_small_params")
    small_all = small_all.reshape(N_DEV, -1)[::2]
    full, off = {}, 0
    for n, shp in zip(SMALL_SHARDED, small_shapes):
        size = math.prod(shp)
        seg = small_all[:, off:off + size].reshape((N_SHARDS,) + shp)
        full[n] = jnp.moveaxis(seg, 0, -2).reshape(shp[:-1] + (N_SHARDS * shp[-1],))
        off += size
    wl.update({
        "mixer_norm": mixer_norm, "ffn_norm": ffn_norm,
        "attn_q_norm": attn_q_norm[0], "attn_k_norm": attn_k_norm[0], "attn_sinks": attn_sinks[0],
        "ssm_conv_w": full["ssm_conv_w"][0], "ssm_conv_b": full["ssm_conv_b"][0],
        "ssm_dt_bias": ssm_dt_bias[0], "ssm_a_log": ssm_a_log[0], "ssm_d": ssm_d[0],
        "ssm_norm": full["ssm_norm"][0],
    })

    c_idx = ac.reshape(1).astype(jnp.int32)
    sc_idx = jnp.stack([shard, ac]).astype(jnp.int32)
    mats, halves = {}, {n: None for n in BIG}

    def pair_sums(keys, recv):
        return [_add_pair(mats[k], r, c_idx, f"grads_add_pair_{k[0]}_{k[1]}") for k, r in zip(keys, recv)]

    def owner_sums(keys, partials, recv):
        for (n, l), p, r in zip(keys, partials, recv):
            halves[n] = _add_owned(p, r, sc_idx, l, w[n].shape, halves[n], f"grads_add_owned_{n}_{l}")

    def reduce_behind(keys, first_leg, second_leg):
        got = {}
        behind(first_leg, _Hook(lambda: _grads_to_sibling([mats[k] for k in keys]),
                                lambda res: got.update(partials=pair_sums(keys, res))))
        behind(second_leg, _Hook(lambda: _grads_to_owners(got["partials"]),
                                 lambda res: owner_sums(keys, got["partials"], res)))

    reduce_behind(GROUPS["ffn1"], "ssm_out_dx", "ssm_in_dw")
    reduce_behind(GROUPS["ssm"], "ssm_in_dx", "ffn_down_dx_0")
    reduce_behind(GROUPS["ffn0"], "attn_out_dx", "attn_bwd")
    reduce_behind(GROUPS["attn"][1:], "attn_bwd", "attn_qkv_dw")
    last, early, tail = GROUPS["attn"][0], [n for n in BIG if n != "attn_w_qkv"], {}
    behind("attn_qkv_dx", _Hook(lambda: _grads_to_sibling([mats[last]]),
                                lambda res: tail.update(partials=pair_sums([last], res))))
    behind("attn_qkv_dx", _Hook(
        lambda: _share_halves([halves[n] for n in early], [(early.index(n), l) for n, l in MATRICES if n in early]),
        lambda res: tail.update(grads=dict(zip(early, res)))))
    loss_part, dx, g_full = _local_step(x[0], positions[0], loss_target[0], wl, hooks, mats)
    owner_sums([last], tail["partials"], _exchange(_grads_to_owners(tail["partials"]), "grads_to_owners"))
    grads = tail["grads"]
    grads[last[0]], = _exchange(_share_halves([halves[last[0]]], [(0, 0)]), "grads_share_halves")

    small_full_shapes = [g_full[n].shape for n in SMALL] + [(1,)]
    small_g = _pack_rows([g_full[n] for n in SMALL] + [loss_part.reshape(1)])
    small_sum = _sum8(_all_gather8(small_g, "gather_small_grads")).reshape(-1)
    *small_list, loss = _unpack(small_sum, small_full_shapes)
    for n, g in zip(SMALL, small_list):
        if n in SMALL_SHARDED:
            width = w[n].shape[-1]
            g = lax.dynamic_slice_in_dim(g, shard * width, width, axis=g.ndim - 1)
        grads[n] = g.reshape(w[n].shape)

    delta, new_m, new_v = {}, {}, {}
    for n in BIG:
        grads[n], delta[n], new_m[n], new_v[n] = _adamw(w[n], grads[n], m[n], v[n], "adamw_" + n)
    small_local = [w[n].shape for n in SMALL]
    pk = lambda t: _pack_rows([t[n] for n in SMALL])[None]
    outs = _adamw(pk(w), pk(grads), pk(m), pk(v), "adamw_small")
    for res, o in zip((delta, new_m, new_v), outs[1:]):
        for n, a in zip(SMALL, _unpack(o.reshape(-1), small_local)):
            res[n] = a

    return (loss.reshape(()), dx[None], *[grads[n] for n in WEIGHTS], *[delta[n] for n in WEIGHTS],
            *[new_m[n] for n in WEIGHTS], *[new_v[n] for n in WEIGHTS])
```

```python
import math

import jax
import jax.numpy as jnp
from jax import lax
from jax.experimental import pallas as pl
from jax.experimental.pallas import tpu as pltpu

F32 = jnp.float32
BF16 = jnp.bfloat16

D_MODEL = 2048
EPS = 1e-6
ATT_HEAD_DIM = 64
ATT_Q_HEADS = 32
ATT_KV_HEADS = 4
ATT_GROUP = 8
ATT_BLOCK = 128
ROPE_THETA = 10000.0
Q_WIDTH = ATT_Q_HEADS * ATT_HEAD_DIM
KV_WIDTH = ATT_KV_HEADS * ATT_HEAD_DIM
SSM_D_INNER = 4096
SSM_HEADS = 64
SSM_GROUPS = 8
SSM_HPG = 8
SSM_P = 64
SSM_STATE = 128
SSM_CONV = 4
SSM_CHUNK = 256
SSM_CONV_DIM = 6144
SSM_GN = SSM_D_INNER // SSM_GROUPS
SSM_IN = SSM_D_INNER + SSM_CONV_DIM + SSM_HEADS
LANES = 128
SSM_IN_PAD = -(-SSM_IN // LANES) * LANES
N_SHARDS = 4
N_DEV = 8

ADAM_LR = 0.001
ADAM_B1 = 0.9
ADAM_B2 = 0.999
ADAM_EPS = 1e-08
ADAM_WD = 0.01
ADAM_STEP = 10

VMEM_LIMIT = 56 * 1024 * 1024
MESH = pl.DeviceIdType.MESH
ANY = pl.BlockSpec(memory_space=pl.ANY)


def _params(*sem):
    return pltpu.CompilerParams(dimension_semantics=sem, vmem_limit_bytes=VMEM_LIMIT)


def _sems(n):
    return pltpu.SemaphoreType.DMA((n,))


def _call(body, carry, name, grid, in_specs, out_specs, out_shape, scratch_shapes, sem, args):
    if carry is None:
        return pl.pallas_call(body, name=name, grid=grid, in_specs=in_specs, out_specs=out_specs,
                              out_shape=out_shape, scratch_shapes=scratch_shapes,
                              compiler_params=_params(*sem))(*args)
    n_in, n_out, n_scr = len(in_specs), len(out_specs), len(scratch_shapes)
    c_arrays, c_shapes = list(carry["arrays"]), list(carry["out_shapes"])
    n_cin, n_cout = len(c_arrays), len(c_shapes)

    def carrying(*refs):
        ins, refs = refs[:n_in], refs[n_in:]
        cin, refs = refs[:n_cin], refs[n_cin:]
        outs, refs = refs[:n_out], refs[n_out:]
        cout, refs = refs[:n_cout], refs[n_cout:]
        scratch, (send_sems, recv_sems) = refs[:n_scr], refs[n_scr:]
        copies = carry["build"](cin, cout, send_sems, recv_sems)
        ids = [pl.program_id(d) for d in range(len(grid))]
        first, last = ids[0] == 0, ids[0] == grid[0] - 1
        for d in range(1, len(grid)):
            first = jnp.logical_and(first, ids[d] == 0)
            last = jnp.logical_and(last, ids[d] == grid[d] - 1)

        @pl.when(first)
        def _():
            for cp in copies:
                cp.start()

        body(*ins, *outs, *scratch)

        @pl.when(last)
        def _():
            for cp in copies:
                cp.wait()

    aliases = {n_in + i: n_out + o for i, o in carry.get("aliases", {}).items()}
    return pl.pallas_call(
        carrying, name=name, grid=grid, in_specs=list(in_specs) + [ANY] * n_cin,
        out_specs=list(out_specs) + [ANY] * n_cout, out_shape=list(out_shape) + c_shapes,
        scratch_shapes=list(scratch_shapes) + [_sems(carry["n_sems"]), _sems(carry["n_sems"])],
        input_output_aliases=aliases, compiler_params=_params(*(["arbitrary"] * len(grid))))(*args, *c_arrays)


def _tile(dim, target, unit=LANES):
    if dim <= target:
        return dim
    t = (target // unit) * unit
    while t >= unit:
        if dim % t == 0:
            return t
        t -= unit
    return dim


def _dot(a, b):
    return lax.dot_general(a, b, (((1,), (0,)), ((), ())), preferred_element_type=F32)


def _dot_nt(a, b):
    return lax.dot_general(a, b, (((1,), (1,)), ((), ())), preferred_element_type=F32)


def _dot_tn(a, b):
    return lax.dot_general(a, b, (((0,), (0,)), ((), ())), preferred_element_type=F32)


def _split3(x):
    hi = x.astype(BF16)
    r1 = x - hi.astype(F32)
    mid = r1.astype(BF16)
    lo = (r1 - mid.astype(F32)).astype(BF16)
    return hi, mid, lo


def _dot_x(x, m):
    hi, mid, lo = _split3(x)
    return _dot(hi, m) + _dot(mid, m) + _dot(lo, m)


def _dot_x2(x, m):
    hi = x.astype(BF16)
    return _dot(hi, m) + _dot((x - hi.astype(F32)).astype(BF16), m)


def _xdot(m, x):
    hi, mid, lo = _split3(x)
    return _dot(m, hi) + _dot(m, mid) + _dot(m, lo)


def _dot_x2_nt(x, m):
    hi = x.astype(BF16)
    return _dot_nt(hi, m) + _dot_nt((x - hi.astype(F32)).astype(BF16), m)


def _dot_x_nt(x, m):
    hi, mid, lo = _split3(x)
    return _dot_nt(hi, m) + _dot_nt(mid, m) + _dot_nt(lo, m)


def _iota(shape, dim):
    return lax.broadcasted_iota(jnp.int32, shape, dim)


def _sigmoid(x):
    return 0.5 * jnp.tanh(0.5 * x) + 0.5


def _softplus(x):
    return jnp.maximum(x, 0.0) + jnp.log(1.0 + jnp.exp(-jnp.abs(x)))


MM_ROWS = 1024
MM_TILE = 1408
MM_DEPTH = 3456


def _mm(a, b, mode, name, add=None, out_dtype=F32, b_cols=False, out_cols=False, fuse=None, rows=MM_ROWS,
        carry=None):
    bs = b.shape[-2:]
    if b_cols:
        bs = (bs[0], N_SHARDS * bs[1])
    if mode == "nn":
        (m, k), (k2, n) = a.shape, bs
    elif mode == "nt":
        (m, k), (n, k2) = a.shape, bs
    else:
        (k, m), (k2, n) = a.shape, bs
    assert k == k2, (a.shape, b.shape, mode)
    split_n = (b_cols and mode == "nn") or out_cols
    split_k = b_cols and mode == "nt"
    tm = _tile(m, MM_TILE if mode == "tn" else rows)
    tn = _tile(n // N_SHARDS if split_n else n, MM_TILE)
    tk = _tile(k // N_SHARDS if split_k else k, MM_DEPTH)
    nk = k // tk
    nj, nq = (n // N_SHARDS) // tn, (k // N_SHARDS) // tk
    two_shards = split_k and nq == 1
    if two_shards:
        nk = N_SHARDS // 2
    if mode == "tn":
        a_spec = pl.BlockSpec((tk, tm), lambda i, j, q: (q, i))
    else:
        a_spec = pl.BlockSpec((tm, 2 * tk if two_shards else tk), lambda i, j, q: (i, q))
    if mode == "nt":
        if two_shards:
            b_spec = pl.BlockSpec((2, tn, tk), lambda i, j, q: (q, j, 0))
        elif b_cols:
            b_spec = pl.BlockSpec((None, tn, tk), lambda i, j, q: (q // nq, j, q % nq))
        else:
            b_spec = pl.BlockSpec((tn, tk), lambda i, j, q: (j, q))
    elif b_cols:
        b_spec = pl.BlockSpec((None, tk, tn), lambda i, j, q: (j // nj, q, j % nj))
    else:
        b_spec = pl.BlockSpec((tk, tn), lambda i, j, q: (q, j))
    add_spec = pl.BlockSpec((tm, tn), lambda i, j, q: (i, j))
    if out_cols:
        o_spec = pl.BlockSpec((None, tm, tn), lambda i, j, q: (j // nj, i, j % nj))
        o_shape = (N_SHARDS, m, n // N_SHARDS)
    else:
        o_spec, o_shape = add_spec, (m, n)
    dot = {"nn": _dot, "nt": _dot_nt, "tn": _dot_tn}[mode]
    has_add = add is not None
    fuse_fn, extra, out_dtypes = fuse if fuse is not None else (None, [], [out_dtype])
    n_in, n_out = 2 + has_add + len(extra), len(out_dtypes)

    def body(*refs):
        a_ref, b_ref = refs[:2]
        add_ref = refs[2] if has_add else None
        extra_refs = refs[2 + has_add:n_in]
        o_refs, acc_ref = refs[n_in:n_in + n_out], refs[n_in + n_out]
        if two_shards:
            part = (dot(a_ref[:, :tk].astype(BF16), b_ref[0].astype(BF16))
                    + dot(a_ref[:, tk:].astype(BF16), b_ref[1].astype(BF16)))
        else:
            part = dot(a_ref[...].astype(BF16), b_ref[...].astype(BF16))

        def finish(total):
            if has_add:
                total = total + add_ref[...].astype(F32)
            outs = (total,) if fuse_fn is None else fuse_fn(total, *[r[...] for r in extra_refs])
            for o_ref, val in zip(o_refs, outs):
                o_ref[...] = val.astype(o_ref.dtype)

        if nk == 1:
            finish(part)
        else:
            q = pl.program_id(2)

            @pl.when(q == 0)
            def _():
                acc_ref[...] = part

            @pl.when(jnp.logical_and(q > 0, q < nk - 1))
            def _():
                acc_ref[...] += part

            @pl.when(q == nk - 1)
            def _():
                finish(acc_ref[...] + part)

    in_specs = [a_spec, b_spec] + [add_spec] * (has_add + len(extra))
    args = (a, b) + ((add,) if has_add else ()) + tuple(extra)
    res = _call(body, carry, name, (m // tm, n // tn, nk), in_specs, [o_spec] * n_out,
                [jax.ShapeDtypeStruct(o_shape, dt) for dt in out_dtypes],
                [pltpu.VMEM((tm, tn) if nk > 1 else (8, LANES), F32)], ("parallel", "parallel", "arbitrary"), args)
    main = res[0] if fuse is None else res[:n_out]
    return main if carry is None else (main, res[n_out:])


def _rms_fwd(x, g, name):
    s, d = x.shape
    ts = _tile(s, 512, 8)

    def body(x_ref, g_ref, o_ref):
        xv = x_ref[...]
        r = lax.rsqrt(jnp.mean(xv * xv, axis=-1, keepdims=True) + EPS)
        o_ref[...] = (xv * r * g_ref[...]).astype(BF16)

    return pl.pallas_call(
        body, name=name, grid=(s // ts,),
        in_specs=[pl.BlockSpec((ts, d), lambda i: (i, 0)), pl.BlockSpec((1, d), lambda i: (0, 0))],
        out_specs=pl.BlockSpec((ts, d), lambda i: (i, 0)),
        out_shape=jax.ShapeDtypeStruct((s, d), BF16),
        compiler_params=_params("parallel"),
    )(x, g)


def _rms_bwd(x, g, dh, dres, name):
    s, d = x.shape
    ts = _tile(s, 512, 8)

    def body(x_ref, g_ref, dh_ref, dres_ref, dx_ref, dxb_ref, dg_ref):
        xv = x_ref[...]
        r = lax.rsqrt(jnp.mean(xv * xv, axis=-1, keepdims=True) + EPS)
        xhat = xv * r
        dhv = dh_ref[...].astype(F32)
        part = jnp.sum(dhv * xhat, axis=0, keepdims=True)

        @pl.when(pl.program_id(0) == 0)
        def _():
            dg_ref[...] = part

        @pl.when(pl.program_id(0) > 0)
        def _():
            dg_ref[...] += part

        dxh = dhv * g_ref[...]
        dx = r * (dxh - xhat * jnp.mean(dxh * xhat, axis=-1, keepdims=True))
        total = dres_ref[...] + dx
        dx_ref[...] = total
        dxb_ref[...] = total.astype(BF16)

    row = pl.BlockSpec((ts, d), lambda i: (i, 0))
    vec = pl.BlockSpec((1, d), lambda i: (0, 0))
    return pl.pallas_call(
        body, name=name, grid=(s // ts,),
        in_specs=[row, vec, row, row], out_specs=[row, row, vec],
        out_shape=[jax.ShapeDtypeStruct((s, d), F32), jax.ShapeDtypeStruct((s, d), BF16),
                   jax.ShapeDtypeStruct((1, d), F32)],
        compiler_params=_params("arbitrary"),
    )(x, g, dh, dres)


def _loss_fwd_bwd(y, target):
    s, d = y.shape
    ts = _tile(s, 512, 8)

    def body(y_ref, t_ref, l_ref, dy_ref, dyb_ref):
        diff = y_ref[...] - t_ref[...]
        dy_ref[...] = diff * (1.0 / d)
        dyb_ref[...] = (diff * (1.0 / d)).astype(BF16)
        part = jnp.full((1, LANES), 0.5 * jnp.sum(jnp.mean(diff * diff, axis=-1, keepdims=True)), F32)

        @pl.when(pl.program_id(0) == 0)
        def _():
            l_ref[...] = part

        @pl.when(pl.program_id(0) > 0)
        def _():
            l_ref[...] += part

    row = pl.BlockSpec((ts, d), lambda i: (i, 0))
    acc = pl.BlockSpec((1, LANES), lambda i: (0, 0))
    return pl.pallas_call(
        body, name="loss", grid=(s // ts,), in_specs=[row, row], out_specs=[acc, row, row],
        out_shape=[jax.ShapeDtypeStruct((1, LANES), F32), jax.ShapeDtypeStruct((s, d), F32),
                   jax.ShapeDtypeStruct((s, d), BF16)],
        compiler_params=_params("arbitrary"),
    )(y, target)


def _lane_consts():
    r, c = _iota((LANES, LANES), 0), _iota((LANES, LANES), 1)
    same = (r >> 6) == (c >> 6)
    rin, cin = r & 63, c & 63
    one = lambda cond: jnp.where(cond, 1.0, 0.0).astype(BF16)
    return dict(
        seg=one(same),
        rot=(jnp.where(same & (rin == cin + 32), -1.0, 0.0)
             + jnp.where(same & (cin == rin + 32), 1.0, 0.0)).astype(BF16),
        dup_lo=one(r == cin), dup_hi=one(r == cin + 64),
        up=one((c >= 64) & (r == c - 64)), down=one((c < 64) & (r == c + 64)),
        fold_lo=one((c < 64) & (rin == c)), fold_hi=one((c >= 64) & (rin == c - 64)),
    )


def _norm_rope(xc, gain, cos, sin, k):
    ss = _dot_x2(xc * xc, k["seg"])
    rinv = lax.rsqrt(ss * (1.0 / ATT_HEAD_DIM) + EPS)
    xhat = xc * rinv
    y = xhat * gain
    return y * cos + _dot_x2(y, k["rot"]) * sin, xhat, rinv


def _norm_rope_bwd(dr, xhat, rinv, gain, cos, sin, k):
    dy = dr * cos - _dot_x2(dr * sin, k["rot"])
    dgain = jnp.sum(dy * xhat, axis=0, keepdims=True)
    dxh = dy * gain
    dx = rinv * (dxh - xhat * (_dot_x2(dxh * xhat, k["seg"]) * (1.0 / ATT_HEAD_DIM)))
    return dx, dgain


def _attn_prep(qkv, cos, sin, gq, gk):
    s = qkv.shape[0]
    tr = _tile(s, 256, 8)

    def body(x_ref, cos_ref, sin_ref, gq_ref, gk_ref, q_ref, kk_ref, vlo_ref, vhi_ref):
        k = _lane_consts()
        cosv, sinv = cos_ref[...], sin_ref[...]
        lane = _iota((tr, LANES), 1)
        for j in range(Q_WIDTH // LANES):
            r, _, _ = _norm_rope(x_ref[:, j * LANES:(j + 1) * LANES], gq_ref[...], cosv, sinv, k)
            q_ref[:, j * LANES:(j + 1) * LANES] = r.astype(BF16)
        for i in range(KV_WIDTH // LANES):
            off = Q_WIDTH + i * LANES
            r, _, _ = _norm_rope(x_ref[:, off:off + LANES], gk_ref[...], cosv, sinv, k)
            rb = r.astype(BF16)
            kk_ref[:, (2 * i) * LANES:(2 * i + 1) * LANES] = _dot(rb, k["dup_lo"]).astype(BF16)
            kk_ref[:, (2 * i + 1) * LANES:(2 * i + 2) * LANES] = _dot(rb, k["dup_hi"]).astype(BF16)
            off = Q_WIDTH + KV_WIDTH + i * LANES
            vb = x_ref[:, off:off + LANES].astype(BF16)
            zero = jnp.zeros_like(vb)
            vlo_ref[:, (2 * i) * LANES:(2 * i + 1) * LANES] = jnp.where(lane < 64, vb, zero)
            vhi_ref[:, (2 * i) * LANES:(2 * i + 1) * LANES] = _dot(vb, k["up"]).astype(BF16)
            vlo_ref[:, (2 * i + 1) * LANES:(2 * i + 2) * LANES] = _dot(vb, k["down"]).astype(BF16)
            vhi_ref[:, (2 * i + 1) * LANES:(2 * i + 2) * LANES] = jnp.where(lane >= 64, vb, zero)

    w = qkv.shape[1]
    row = lambda width: pl.BlockSpec((tr, width), lambda i: (i, 0))
    vec = pl.BlockSpec((1, LANES), lambda i: (0, 0))
    kw = ATT_KV_HEADS * LANES
    return pl.pallas_call(
        body, name="attn_prep", grid=(s // tr,),
        in_specs=[row(w), row(LANES), row(LANES), vec, vec],
        out_specs=[row(Q_WIDTH), row(kw), row(kw), row(kw)],
        out_shape=[jax.ShapeDtypeStruct((s, Q_WIDTH), BF16)] + [jax.ShapeDtypeStruct((s, kw), BF16)] * 3,
        compiler_params=_params("parallel"),
    )(qkv, cos, sin, gq, gk)


def _softmax_sink(s, valid, sink):
    s = jnp.where(valid, s, -jnp.inf)
    m = jnp.maximum(jnp.max(s, axis=-1, keepdims=True), sink)
    p = jnp.exp(s - m)
    esink = jnp.exp(sink - m)
    inv = 1.0 / (jnp.sum(p, axis=-1, keepdims=True) + esink)
    return p * inv, esink * inv


def _attn_specs(order):
    if order == "nh":
        cur = lambda n, h: (n, h)
        prev = lambda n, h: (jnp.maximum(n - 1, 0), h)
    else:
        cur = lambda h, n: (n, h)
        prev = lambda h, n: (jnp.maximum(n - 1, 0), h)
    qs = pl.BlockSpec((ATT_BLOCK, ATT_GROUP * ATT_HEAD_DIM), cur)
    kc = pl.BlockSpec((ATT_BLOCK, LANES), cur)
    kp = pl.BlockSpec((ATT_BLOCK, LANES), prev)
    return qs, kc, kp


def _pair_rows(qp):
    lane = _iota((ATT_BLOCK, LANES), 1)
    zero = jnp.zeros_like(qp)
    return jnp.concatenate([jnp.where(lane < 64, qp, zero), jnp.where(lane >= 64, qp, zero)], axis=0)


def _pair_masks(n):
    qi = _iota((2 * ATT_BLOCK, 2 * ATT_BLOCK), 0) & (ATT_BLOCK - 1)
    kj = _iota((2 * ATT_BLOCK, 2 * ATT_BLOCK), 1)
    valid = (kj > qi) & (kj <= qi + ATT_BLOCK) & ((kj >= ATT_BLOCK) | (n > 0))
    return valid, _iota((2 * ATT_BLOCK, 1), 0) >= ATT_BLOCK


def _attn_fwd(q, kk, vlo, vhi, sinks, name="attn_fwd", carry=None):
    s = q.shape[0]
    nb = s // ATT_BLOCK
    scale = ATT_HEAD_DIM ** -0.5

    def body(sink_ref, q_ref, kc_ref, kp_ref, vloc_ref, vlop_ref, vhic_ref, vhip_ref, o_ref):
        n, h = pl.program_id(0), pl.program_id(1)
        valid, upper = _pair_masks(n)
        kw = jnp.concatenate([kp_ref[...], kc_ref[...]], axis=0)
        vcat = jnp.concatenate([vlop_ref[...], vloc_ref[...], vhip_ref[...], vhic_ref[...]], axis=0)
        for jp in range(ATT_GROUP // 2):
            q2 = _pair_rows(q_ref[:, jp * LANES:(jp + 1) * LANES])
            sink = jnp.where(upper, sink_ref[h * ATT_GROUP + 2 * jp + 1], sink_ref[h * ATT_GROUP + 2 * jp])
            probs, _ = _softmax_sink(_dot_nt(q2, kw) * scale, valid, sink)
            pcat = jnp.concatenate([probs[:ATT_BLOCK], probs[ATT_BLOCK:]], axis=1).astype(BF16)
            o_ref[:, jp * LANES:(jp + 1) * LANES] = _dot(pcat, vcat).astype(BF16)

    qs, kc, kp = _attn_specs("nh")
    res = _call(body, carry, name, (nb, ATT_KV_HEADS),
                [pl.BlockSpec(memory_space=pltpu.SMEM), qs, kc, kp, kc, kp, kc, kp], [qs],
                [jax.ShapeDtypeStruct((s, Q_WIDTH), BF16)], [], ("parallel", "parallel"),
                (sinks, q, kk, kk, vlo, vlo, vhi, vhi))
    return res[0] if carry is None else (res[0], res[1:])


def _attn_bwd(q, kk, vlo, vhi, sinks, do, name="attn_bwd", carry=None):
    s = q.shape[0]
    nb = s // ATT_BLOCK
    scale = ATT_HEAD_DIM ** -0.5

    def body(sink_ref, q_ref, kc_ref, kp_ref, vloc_ref, vlop_ref, vhic_ref, vhip_ref, do_ref,
             dq_ref, dkc_ref, dkp_ref, dvloc_ref, dvlop_ref, dvhic_ref, dvhip_ref, dsink_ref, dkk_acc, dv_acc):
        h, n = pl.program_id(0), pl.program_id(1)
        valid, upper = _pair_masks(n)
        kw = jnp.concatenate([kp_ref[...], kc_ref[...]], axis=0)
        vcat = jnp.concatenate([vlop_ref[...], vloc_ref[...], vhip_ref[...], vhic_ref[...]], axis=0)
        lane = _iota((ATT_BLOCK, LANES), 1)
        sub = _iota((ATT_GROUP, LANES), 0)
        dsink = jnp.zeros((ATT_GROUP, LANES), F32)
        for jp in range(ATT_GROUP // 2):
            q2 = _pair_rows(q_ref[:, jp * LANES:(jp + 1) * LANES])
            dop = do_ref[:, jp * LANES:(jp + 1) * LANES]
            sink = jnp.where(upper, sink_ref[h * ATT_GROUP + 2 * jp + 1], sink_ref[h * ATT_GROUP + 2 * jp])
            probs, psink = _softmax_sink(_dot_nt(q2, kw) * scale, valid, sink)
            pcat = jnp.concatenate([probs[:ATT_BLOCK], probs[ATT_BLOCK:]], axis=1).astype(BF16)
            dpc = _dot_nt(dop, vcat)
            dprobs = jnp.concatenate([dpc[:, :2 * ATT_BLOCK], dpc[:, 2 * ATT_BLOCK:]], axis=0)
            dv_part = _dot_tn(pcat, dop)
            delta = jnp.sum(probs * dprobs, axis=-1, keepdims=True)
            ds = (probs * (dprobs - delta) * scale).astype(BF16)
            sd = psink * delta
            dsink = (dsink + jnp.where(sub == 2 * jp, -jnp.sum(sd[:ATT_BLOCK]), 0.0)
                     + jnp.where(sub == 2 * jp + 1, -jnp.sum(sd[ATT_BLOCK:]), 0.0))
            dq2 = _dot(ds, kw)
            dq_ref[:, jp * LANES:(jp + 1) * LANES] = jnp.where(lane < 64, dq2[:ATT_BLOCK], dq2[ATT_BLOCK:])
            dkk_part = _dot_tn(ds, q2)
            if jp == 0:
                dkk_acc[...], dv_acc[...] = dkk_part, dv_part
            else:
                dkk_acc[...] += dkk_part
                dv_acc[...] += dv_part
        blk = ATT_BLOCK
        dkp_ref[...], dkc_ref[...] = dkk_acc[:blk], dkk_acc[blk:]
        dvlop_ref[...], dvloc_ref[...] = dv_acc[:blk], dv_acc[blk:2 * blk]
        dvhip_ref[...], dvhic_ref[...] = dv_acc[2 * blk:3 * blk], dv_acc[3 * blk:]

        @pl.when(n == 0)
        def _():
            dsink_ref[0] = dsink

        @pl.when(n > 0)
        def _():
            dsink_ref[0] += dsink

    qs, kc, kp = _attn_specs("hn")
    kw_shape = jax.ShapeDtypeStruct((s, ATT_KV_HEADS * LANES), F32)
    res = _call(body, carry, name, (ATT_KV_HEADS, nb),
                [pl.BlockSpec(memory_space=pltpu.SMEM), qs, kc, kp, kc, kp, kc, kp, qs],
                [qs] + [kc] * 6 + [pl.BlockSpec((1, ATT_GROUP, LANES), lambda h, n: (h, 0, 0))],
                [jax.ShapeDtypeStruct((s, Q_WIDTH), F32)] + [kw_shape] * 6
                + [jax.ShapeDtypeStruct((ATT_KV_HEADS, ATT_GROUP, LANES), F32)],
                [pltpu.VMEM((2 * ATT_BLOCK, LANES), F32), pltpu.VMEM((4 * ATT_BLOCK, LANES), F32)],
                ("parallel", "arbitrary"), (sinks, q, kk, kk, vlo, vlo, vhi, vhi, do))
    return res if carry is None else (res[:8], res[8:])


def _attn_prep_bwd(qkv, cos, sin, gq, gk, dq, dks, dvlos, dvhis):
    s, w = qkv.shape
    tr = ATT_BLOCK
    nb = s // tr

    def body(x_ref, cos_ref, sin_ref, gq_ref, gk_ref, dq_ref, dkc_ref, dkn_ref, dvloc_ref, dvlon_ref,
             dvhic_ref, dvhin_ref, dx_ref, dgq_ref, dgk_ref):
        n = pl.program_id(0)
        k = _lane_consts()
        cosv, sinv = cos_ref[...], sin_ref[...]
        nxt = jnp.where(n < nb - 1, 1.0, 0.0)
        lane = _iota((tr, LANES), 1)
        dgq = jnp.zeros((1, LANES), F32)
        dgk = jnp.zeros((1, LANES), F32)
        for j in range(Q_WIDTH // LANES):
            sl = slice(j * LANES, (j + 1) * LANES)
            _, xhat, rinv = _norm_rope(x_ref[:, sl], gq_ref[...], cosv, sinv, k)
            dx, dg = _norm_rope_bwd(dq_ref[:, sl], xhat, rinv, gq_ref[...], cosv, sinv, k)
            dx_ref[:, sl] = dx.astype(BF16)
            dgq = dgq + dg
        for i in range(KV_WIDTH // LANES):
            a, b = slice(2 * i * LANES, (2 * i + 1) * LANES), slice((2 * i + 1) * LANES, (2 * i + 2) * LANES)
            dr = (_dot_x2(dkc_ref[:, a] + nxt * dkn_ref[:, a], k["fold_lo"])
                  + _dot_x2(dkc_ref[:, b] + nxt * dkn_ref[:, b], k["fold_hi"]))
            sl = slice(Q_WIDTH + i * LANES, Q_WIDTH + (i + 1) * LANES)
            _, xhat, rinv = _norm_rope(x_ref[:, sl], gk_ref[...], cosv, sinv, k)
            dx, dg = _norm_rope_bwd(dr, xhat, rinv, gk_ref[...], cosv, sinv, k)
            dx_ref[:, sl] = dx.astype(BF16)
            dgk = dgk + dg
            ta = jnp.where(lane < 64, dvloc_ref[:, a] + nxt * dvlon_ref[:, a], dvhic_ref[:, a] + nxt * dvhin_ref[:, a])
            tb = jnp.where(lane < 64, dvloc_ref[:, b] + nxt * dvlon_ref[:, b], dvhic_ref[:, b] + nxt * dvhin_ref[:, b])
            sl = slice(Q_WIDTH + KV_WIDTH + i * LANES, Q_WIDTH + KV_WIDTH + (i + 1) * LANES)
            dx_ref[:, sl] = (_dot_x2(ta, k["fold_lo"]) + _dot_x2(tb, k["fold_hi"])).astype(BF16)

        @pl.when(n == 0)
        def _():
            dgq_ref[...] = dgq
            dgk_ref[...] = dgk

        @pl.when(n > 0)
        def _():
            dgq_ref[...] += dgq
            dgk_ref[...] += dgk

    row = lambda width: pl.BlockSpec((tr, width), lambda i: (i, 0))
    nxt_row = pl.BlockSpec((tr, ATT_KV_HEADS * LANES), lambda i: (jnp.minimum(i + 1, nb - 1), 0))
    vec = pl.BlockSpec((1, LANES), lambda i: (0, 0))
    kw = ATT_KV_HEADS * LANES
    return pl.pallas_call(
        body, name="attn_prep_bwd", grid=(nb,),
        in_specs=[row(w), row(LANES), row(LANES), vec, vec, row(Q_WIDTH),
                  row(kw), nxt_row, row(kw), nxt_row, row(kw), nxt_row],
        out_specs=[row(w), vec, vec],
        out_shape=[jax.ShapeDtypeStruct((s, w), BF16), jax.ShapeDtypeStruct((1, LANES), F32),
                   jax.ShapeDtypeStruct((1, LANES), F32)],
        compiler_params=_params("arbitrary"),
    )(qkv, cos, sin, gq, gk, dq, dks[0], dks[1], dvlos[0], dvlos[1], dvhis[0], dvhis[1])


CONV_HALO = 8
CONV_TC = 1024
XBC_OFF = SSM_D_INNER // CONV_TC
DT_OFF = SSM_D_INNER + SSM_CONV_DIM


def _conv_pre(ext, w_ref, b_ref, ts):
    pre = b_ref[...] + w_ref[SSM_CONV - 1:SSM_CONV, :] * ext[CONV_HALO:]
    for kk in range(SSM_CONV - 1):
        pre = pre + w_ref[kk:kk + 1, :] * pltpu.roll(ext, SSM_CONV - 1 - kk, 0)[CONV_HALO:]
    return pre


def _conv_specs(ts):
    tc = CONV_TC
    src = pl.BlockSpec((ts, tc), lambda j, i: (i, XBC_OFF + j))
    halo = pl.BlockSpec((CONV_HALO, tc), lambda j, i: (jnp.maximum(i * (ts // CONV_HALO) - 1, 0), XBC_OFF + j))
    blk = pl.BlockSpec((ts, tc), lambda j, i: (i, j))
    wspec = pl.BlockSpec((SSM_CONV, tc), lambda j, i: (0, j))
    bspec = pl.BlockSpec((1, tc), lambda j, i: (0, j))
    return src, halo, blk, wspec, bspec


def _conv_fwd(zx, w, b):
    s, c = zx.shape[0], SSM_CONV_DIM
    ts = _tile(s, 512, 8)

    def body(u_ref, halo_ref, w_ref, b_ref, o_ref):
        halo = jnp.where(pl.program_id(1) > 0, halo_ref[...], 0.0)
        pre = _conv_pre(jnp.concatenate([halo, u_ref[...]], axis=0), w_ref, b_ref, ts)
        o_ref[...] = pre * _sigmoid(pre)

    src, halo, blk, wspec, bspec = _conv_specs(ts)
    return pl.pallas_call(
        body, name="conv_fwd", grid=(c // CONV_TC, s // ts),
        in_specs=[src, halo, wspec, bspec], out_specs=blk, out_shape=jax.ShapeDtypeStruct((s, c), F32),
        compiler_params=_params("parallel", "parallel"),
    )(zx, zx, w, b)


def _conv_bwd_pre(zx, w, b, dxs, dbm, dcm):
    s, c = zx.shape[0], SSM_CONV_DIM
    ts = _tile(s, 512, 8)
    nx, nb = dxs.shape[1] // CONV_TC, dbm.shape[1] // CONV_TC

    def body(u_ref, halo_ref, w_ref, b_ref, dx_ref, dbm_ref, dcm_ref, dpre_ref, dw_ref, db_ref):
        j, i = pl.program_id(0), pl.program_id(1)
        halo = jnp.where(i > 0, halo_ref[...], 0.0)
        ext = jnp.concatenate([halo, u_ref[...]], axis=0)
        pre = _conv_pre(ext, w_ref, b_ref, ts)
        sg = _sigmoid(pre)
        da = jnp.where(j < nx, dx_ref[...], jnp.where(j < nx + nb, dbm_ref[...], dcm_ref[...]))
        dpre = da * sg * (1.0 + pre * (1.0 - sg))
        dpre_ref[...] = dpre
        rows = [jnp.sum(dpre * pltpu.roll(ext, SSM_CONV - 1 - kk, 0)[CONV_HALO:], axis=0, keepdims=True)
                for kk in range(SSM_CONV - 1)]
        rows.append(jnp.sum(dpre * ext[CONV_HALO:], axis=0, keepdims=True))
        dwp = jnp.concatenate(rows, axis=0)
        dbp = jnp.sum(dpre, axis=0, keepdims=True)

        @pl.when(i == 0)
        def _():
            dw_ref[...] = dwp
            db_ref[...] = dbp

        @pl.when(i > 0)
        def _():
            dw_ref[...] += dwp
            db_ref[...] += dbp

    src, halo, blk, wspec, bspec = _conv_specs(ts)

    def part(lo, n):
        return pl.BlockSpec((ts, CONV_TC), lambda j, i: (jnp.where((j >= lo) & (j < lo + n), i, 0),
                                                         jnp.clip(j - lo, 0, n - 1)))

    return pl.pallas_call(
        body, name="conv_bwd_pre", grid=(c // CONV_TC, s // ts),
        in_specs=[src, halo, wspec, bspec, part(0, nx), part(nx, nb), part(nx + nb, nb)],
        out_specs=[blk, wspec, bspec],
        out_shape=[jax.ShapeDtypeStruct((s, c), F32), jax.ShapeDtypeStruct((SSM_CONV, c), F32),
                   jax.ShapeDtypeStruct((1, c), F32)],
        compiler_params=_params("parallel", "arbitrary"),
    )(zx, zx, w, b, dxs, dbm, dcm)


def _conv_bwd_in(dpre, w, dzx):
    s, c = dpre.shape
    ts, tc = _tile(s, 512, 8), CONV_TC
    ns = s // ts

    def body(d_ref, halo_ref, w_ref, dzx_ref, o_ref):
        del dzx_ref
        halo = jnp.where(pl.program_id(1) < ns - 1, halo_ref[...], 0.0)
        ext = jnp.concatenate([d_ref[...], halo], axis=0)
        du = w_ref[SSM_CONV - 1:SSM_CONV, :] * ext[:ts]
        for kk in range(SSM_CONV - 1):
            du = du + w_ref[kk:kk + 1, :] * pltpu.roll(ext, ts + CONV_HALO - (SSM_CONV - 1 - kk), 0)[:ts]
        o_ref[...] = du.astype(BF16)

    blk = pl.BlockSpec((ts, tc), lambda j, i: (i, j))
    halo = pl.BlockSpec((CONV_HALO, tc), lambda j, i: (jnp.minimum((i + 1) * (ts // CONV_HALO), s // CONV_HALO - 1), j))
    return pl.pallas_call(
        body, name="conv_bwd_in", grid=(c // tc, ns),
        in_specs=[blk, halo, pl.BlockSpec((SSM_CONV, tc), lambda j, i: (0, j)), ANY],
        out_specs=pl.BlockSpec((ts, tc), lambda j, i: (i, XBC_OFF + j)),
        out_shape=jax.ShapeDtypeStruct(dzx.shape, BF16), input_output_aliases={3: 0},
        compiler_params=_params("parallel", "parallel"),
    )(dpre, dpre, w, dzx)


def _ssd_common(dt_ref, dtt_ref, bias_ref, biast_ref, alog_ref, alogt_ref):
    ln = SSM_CHUNK
    raw, rawt = dt_ref[0] + bias_ref[0], dtt_ref[0] + biast_ref[0]
    dt, dtt = _softplus(raw), _softplus(rawt)
    a, at = -jnp.exp(alog_ref[0]), -jnp.exp(alogt_ref[0])
    tri = jnp.where(_iota((ln, ln), 0) >= _iota((ln, ln), 1), 1.0, 0.0).astype(BF16)
    return dict(raw=raw, rawt=rawt, dt=dt, dtt=dtt, a=a, at=at, tri=tri,
                acum=_xdot(tri, dt * a), acumt=_dot_x_nt(dtt * at, tri))


def _ssd_specs(nc, rev):
    cidx = (lambda c: nc - 1 - c) if rev else (lambda c: c)
    ln = SSM_CHUNK
    xs = pl.BlockSpec((ln, SSM_GN), lambda g, c: (cidx(c), g))
    bs = pl.BlockSpec((ln, SSM_STATE), lambda g, c: (cidx(c), SSM_D_INNER // SSM_STATE + g))
    cs = pl.BlockSpec((ln, SSM_STATE), lambda g, c: (cidx(c), SSM_D_INNER // SSM_STATE + SSM_GROUPS + g))
    dt = pl.BlockSpec((1, ln, SSM_HPG), lambda g, c: (g, cidx(c), 0))
    dtt = pl.BlockSpec((1, SSM_HPG, ln), lambda g, c: (g, 0, cidx(c)))
    row = pl.BlockSpec((1, 1, SSM_HPG), lambda g, c: (g, 0, 0))
    col = pl.BlockSpec((1, SSM_HPG, 1), lambda g, c: (g, 0, 0))
    st = pl.BlockSpec((None, None, SSM_GN, SSM_STATE), lambda g, c: (cidx(c), g, 0, 0))
    return xs, bs, cs, dt, dtt, row, col, st


def _head_expand():
    return jnp.where((_iota((SSM_HPG, SSM_GN), 1) >> 6) == _iota((SSM_HPG, SSM_GN), 0), 1.0, 0.0).astype(BF16)


def _head_expand_t():
    return jnp.where((_iota((SSM_GN, SSM_HPG), 0) >> 6) == _iota((SSM_GN, SSM_HPG), 1), 1.0, 0.0).astype(BF16)


def _dot_x_tn(x, m):
    hi, mid, lo = _split3(x)
    return _dot_tn(hi, m) + _dot_tn(mid, m) + _dot_tn(lo, m)


def _ssd_fwd(xbc, dt_g, dt_gt, bias_r, bias_c, alog_r, alog_c, d_r):
    s = xbc.shape[0]
    ln = SSM_CHUNK
    nc = s // ln

    def body(x_ref, b_ref, c_ref, dt_ref, dtt_ref, bias_ref, biast_ref, alog_ref, alogt_ref, d_ref,
             y_ref, st_ref, state):
        @pl.when(pl.program_id(1) == 0)
        def _():
            state[...] = jnp.zeros_like(state)

        cm = _ssd_common(dt_ref, dtt_ref, bias_ref, biast_ref, alog_ref, alogt_ref)
        acum, acumt = cm["acum"], cm["acumt"]
        ex = _head_expand()
        acum_x = _dot_x(acum, ex)
        xv = x_ref[...]
        xdt = xv * _dot_x(cm["dt"], ex)
        xdtb = xdt.astype(BF16)
        bb, cb = b_ref[...].astype(BF16), c_ref[...].astype(BF16)
        cbm = _dot_nt(cb, bb)
        causal = _iota((ln, ln), 0) >= _iota((ln, ln), 1)
        s2 = state[...]
        st_ref[...] = s2
        for r in range(SSM_HPG):
            sl = slice(r * SSM_P, (r + 1) * SSM_P)
            decay = jnp.exp(jnp.where(causal, acum[:, r:r + 1] - acumt[r:r + 1, :], -jnp.inf))
            y_ref[:, sl] = _dot((cbm * decay).astype(BF16), xdtb[:, sl])
        y_ref[...] = (y_ref[...] + _dot_nt(cb, s2.astype(BF16)) * jnp.exp(acum_x) + _dot_x(d_ref[0], ex) * xv)
        last_x = acum_x[ln - 1:ln, :]
        elast = jnp.exp(_xdot(_head_expand_t(), acumt[:, ln - 1:ln]))
        state[...] = s2 * elast + _dot_tn((xdt * jnp.exp(last_x - acum_x)).astype(BF16), bb)

    xs, bs, cs, dts, dtts, row, col, st = _ssd_specs(nc, False)
    return pl.pallas_call(
        body, name="ssd_fwd", grid=(SSM_GROUPS, nc),
        in_specs=[xs, bs, cs, dts, dtts, row, col, row, col, row],
        out_specs=[xs, st],
        out_shape=[jax.ShapeDtypeStruct((s, SSM_D_INNER), F32),
                   jax.ShapeDtypeStruct((nc, SSM_GROUPS, SSM_GN, SSM_STATE), F32)],
        scratch_shapes=[pltpu.VMEM((SSM_GN, SSM_STATE), F32)],
        compiler_params=_params("parallel", "arbitrary"),
    )(xbc, xbc, xbc, dt_g, dt_gt, bias_r, bias_c, alog_r, alog_c, d_r)


def _ssd_bwd(xbc, dt_g, dt_gt, bias_r, bias_c, alog_r, alog_c, d_r, states, dy):
    s = xbc.shape[0]
    ln = SSM_CHUNK
    nc = s // ln

    def body(x_ref, b_ref, c_ref, dt_ref, dtt_ref, bias_ref, biast_ref, alog_ref, alogt_ref, d_ref,
             st_ref, dy_ref, dx_ref, db_ref, dc_ref, ddt_ref, ddtt_ref, dbias_ref, dbiast_ref,
             dalog_ref, dalogt_ref, dd_ref, dstate):
        step = pl.program_id(1)

        @pl.when(step == 0)
        def _():
            dstate[...] = jnp.zeros_like(dstate)

        cm = _ssd_common(dt_ref, dtt_ref, bias_ref, biast_ref, alog_ref, alogt_ref)
        dt, acum, acumt = cm["dt"], cm["acum"], cm["acumt"]
        ex, ext = _head_expand(), _head_expand_t()
        dt_x, acum_x = _dot_x(dt, ex), _dot_x(acum, ex)
        eac_x, to_end_x = jnp.exp(acum_x), jnp.exp(acum_x[ln - 1:ln, :] - acum_x)
        xv, dyv = x_ref[...], dy_ref[...]
        xdt = xv * dt_x
        xdtb, dyb = xdt.astype(BF16), dyv.astype(BF16)
        dyeb = (dyv * eac_x).astype(BF16)
        bb, cb = b_ref[...].astype(BF16), c_ref[...].astype(BF16)
        cbm = _dot_nt(cb, bb)
        s2, ds2 = st_ref[...], dstate[...]
        s2b, ds2b = s2.astype(BF16), ds2.astype(BF16)
        dxdt_state = _dot_nt(bb, ds2b) * to_end_x
        yoff = _dot_nt(cb, s2b) * eac_x
        dc_acc = _dot(dyeb, s2b)
        db_acc = _dot((xdt * to_end_x).astype(BF16), ds2b)
        f_rows = _dot_x2_nt(xdt * dxdt_state, ex)
        elast = jnp.exp(acum[ln - 1:ln, :])
        dlast = (jnp.sum(f_rows, axis=0, keepdims=True)
                 + elast * jnp.sum(_dot_x_tn(ds2 * s2, ext), axis=0, keepdims=True))
        is_last = _iota((ln, 1), 0) == ln - 1
        dac_rows = _dot_x2_nt(dyv * yoff, ex) - f_rows + jnp.where(is_last, dlast, 0.0)
        dstate[...] = ds2 * jnp.exp(_xdot(ext, acumt[:, ln - 1:ln])) + _dot_tn(dyeb, cb)
        causal = _iota((ln, ln), 0) >= _iota((ln, ln), 1)
        lane8 = _iota((ln, SSM_HPG), 1)
        sub8 = _iota((SSM_HPG, ln), 0)
        dcb = jnp.zeros((ln, ln), F32)
        dac_cols = jnp.zeros((SSM_HPG, ln), F32)
        for r in range(SSM_HPG):
            sl = slice(r * SSM_P, (r + 1) * SSM_P)
            decay = jnp.exp(jnp.where(causal, acum[:, r:r + 1] - acumt[r:r + 1, :], -jnp.inf))
            dx_ref[:, sl] = _dot_tn((cbm * decay).astype(BF16), dyb[:, sl])
            dcb_r = _dot_nt(dyb[:, sl], xdtb[:, sl]) * decay
            dcb = dcb + dcb_r
            e = dcb_r * cbm
            dac_rows = dac_rows + jnp.where(lane8 == r, jnp.sum(e, axis=-1, keepdims=True), 0.0)
            dac_cols = dac_cols + jnp.where(sub8 == r, jnp.sum(e, axis=0, keepdims=True), 0.0)
        dxdt = dx_ref[...] + dxdt_state
        ddt_all = _dot_x2_nt(dxdt * xv, ex)
        dd_all = jnp.sum(_dot_x2_nt(dyv * xv, ex), axis=0, keepdims=True)
        dx_ref[...] = dxdt * dt_x + _dot_x(d_ref[0], ex) * dyv
        dcbb = dcb.astype(BF16)
        dc_ref[...] = dc_acc + _dot(dcbb, bb)
        db_ref[...] = db_acc + _dot_tn(dcbb, cb)
        triu = jnp.where(_iota((ln, ln), 0) <= _iota((ln, ln), 1), 1.0, 0.0).astype(BF16)
        g_rows = _xdot(triu, dac_rows)
        g_cols = _dot_x(dac_cols, cm["tri"])
        d_rows = (ddt_all + g_rows * cm["a"]) * _sigmoid(cm["raw"])
        d_cols = -(g_cols * cm["at"]) * _sigmoid(cm["rawt"])
        ddt_ref[0] = d_rows
        ddtt_ref[0] = d_cols
        parts = (jnp.sum(d_rows, axis=0, keepdims=True), jnp.sum(d_cols, axis=1, keepdims=True),
                 jnp.sum(g_rows * dt, axis=0, keepdims=True) * cm["a"],
                 -jnp.sum(g_cols * cm["dtt"], axis=1, keepdims=True) * cm["at"], dd_all)
        outs = (dbias_ref, dbiast_ref, dalog_ref, dalogt_ref, dd_ref)

        @pl.when(step == 0)
        def _():
            for o_ref, p in zip(outs, parts):
                o_ref[0] = p

        @pl.when(step > 0)
        def _():
            for o_ref, p in zip(outs, parts):
                o_ref[0] += p

    xs, bs, cs, dts, dtts, row, col, st = _ssd_specs(nc, True)
    grp = pl.BlockSpec((ln, SSM_STATE), lambda g, c: (nc - 1 - c, g))
    rows = jax.ShapeDtypeStruct((SSM_GROUPS, 1, SSM_HPG), F32)
    cols = jax.ShapeDtypeStruct((SSM_GROUPS, SSM_HPG, 1), F32)
    return pl.pallas_call(
        body, name="ssd_bwd", grid=(SSM_GROUPS, nc),
        in_specs=[xs, bs, cs, dts, dtts, row, col, row, col, row, st, xs],
        out_specs=[xs, grp, grp, dts, dtts, row, col, row, col, row],
        out_shape=[jax.ShapeDtypeStruct((s, SSM_D_INNER), F32),
                   jax.ShapeDtypeStruct((s, SSM_GROUPS * SSM_STATE), F32),
                   jax.ShapeDtypeStruct((s, SSM_GROUPS * SSM_STATE), F32),
                   jax.ShapeDtypeStruct((SSM_GROUPS, s, SSM_HPG), F32),
                   jax.ShapeDtypeStruct((SSM_GROUPS, SSM_HPG, s), F32), rows, cols, rows, cols, rows],
        scratch_shapes=[pltpu.VMEM((SSM_GN, SSM_STATE), F32)],
        compiler_params=_params("parallel", "arbitrary"),
    )(xbc, xbc, xbc, dt_g, dt_gt, bias_r, bias_c, alog_r, alog_c, d_r, states, dy)


GN_PER_BLOCK = 2
GN_WIDTH = GN_PER_BLOCK * SSM_GN


def _gate_norm_fwd(y, zx, g):
    s = y.shape[0]
    ts = _tile(s, 512, 8)

    def body(y_ref, z_ref, g_ref, o_ref):
        for k in range(GN_PER_BLOCK):
            sl = slice(k * SSM_GN, (k + 1) * SSM_GN)
            zv = z_ref[:, sl]
            yg = y_ref[:, sl] * (zv * _sigmoid(zv))
            r = lax.rsqrt(jnp.mean(yg * yg, axis=-1, keepdims=True) + EPS)
            o_ref[:, sl] = (yg * r * g_ref[:, sl]).astype(BF16)

    blk = pl.BlockSpec((ts, GN_WIDTH), lambda j, i: (i, j))
    vec = pl.BlockSpec((1, GN_WIDTH), lambda j, i: (0, j))
    return pl.pallas_call(
        body, name="gate_norm_fwd", grid=(SSM_D_INNER // GN_WIDTH, s // ts), in_specs=[blk, blk, vec],
        out_specs=blk,
        out_shape=jax.ShapeDtypeStruct((s, SSM_D_INNER), BF16), compiler_params=_params("parallel", "parallel"),
    )(y, zx, g)


def _gate_norm_bwd(y, zx, g, dout):
    s = y.shape[0]
    ts = _tile(s, 512, 8)

    def body(y_ref, z_ref, g_ref, do_ref, dy_ref, dz_ref, dg_ref):
        parts = []
        for k in range(GN_PER_BLOCK):
            sl = slice(k * SSM_GN, (k + 1) * SSM_GN)
            yv, zv, dov = y_ref[:, sl], z_ref[:, sl], do_ref[:, sl].astype(F32)
            sg = _sigmoid(zv)
            silu = zv * sg
            yg = yv * silu
            r = lax.rsqrt(jnp.mean(yg * yg, axis=-1, keepdims=True) + EPS)
            ygn = yg * r
            parts.append(jnp.sum(dov * ygn, axis=0, keepdims=True))
            dn = dov * g_ref[:, sl]
            dyg = r * (dn - ygn * jnp.mean(dn * ygn, axis=-1, keepdims=True))
            dy_ref[:, sl] = dyg * silu
            dz_ref[:, sl] = (dyg * yv * sg * (1.0 + zv * (1.0 - sg))).astype(BF16)
        part = jnp.concatenate(parts, axis=1)

        @pl.when(pl.program_id(1) == 0)
        def _():
            dg_ref[...] = part

        @pl.when(pl.program_id(1) > 0)
        def _():
            dg_ref[...] += part

    blk = pl.BlockSpec((ts, GN_WIDTH), lambda j, i: (i, j))
    vec = pl.BlockSpec((1, GN_WIDTH), lambda j, i: (0, j))
    return pl.pallas_call(
        body, name="gate_norm_bwd", grid=(SSM_D_INNER // GN_WIDTH, s // ts), in_specs=[blk, blk, vec, blk],
        out_specs=[blk, blk, vec],
        out_shape=[jax.ShapeDtypeStruct((s, SSM_D_INNER), F32), jax.ShapeDtypeStruct((s, SSM_IN_PAD), BF16),
                   jax.ShapeDtypeStruct((1, SSM_D_INNER), F32)],
        compiler_params=_params("parallel", "arbitrary"),
    )(y, zx, g, dout)


def _rope_tables(positions):
    inv_freq = ROPE_THETA ** (-jnp.arange(0, ATT_HEAD_DIM, 2, dtype=F32) / ATT_HEAD_DIM)
    ang = positions.astype(F32)[:, None] * inv_freq
    return jnp.tile(jnp.cos(ang), (1, 4)), jnp.tile(jnp.sin(ang), (1, 4))


def _group_views(v):
    return v.reshape(SSM_GROUPS, 1, SSM_HPG), v.reshape(SSM_GROUPS, SSM_HPG, 1)


def _ffn_fwd(run, x, norm_g, wg, wu, wd, tag):
    h = _rms_fwd(x, norm_g, f"ffn_norm_{tag}")
    g = run(f"ffn_gate_{tag}", _mm, h, wg, "nn", b_cols=True, out_dtype=BF16)

    def act(uv, gv):
        gv = gv.astype(F32)
        return uv, gv * _sigmoid(gv) * uv

    u, a = run(f"ffn_up_{tag}", _mm, h, wu, "nn", b_cols=True, fuse=(act, [g], [BF16, BF16]))
    return run(f"ffn_down_{tag}", _mm, a, wd, "nn", add=x), (h, g, u, a)


def _ffn_bwd(run, mats, x, norm_g, wg, wu, wd, saved, dout, dout_b, tag):
    h, g, u, a = saved

    def act_bwd(da, gv, uv):
        gv, uv = gv.astype(F32), uv.astype(F32)
        sg = _sigmoid(gv)
        return da * uv * sg * (1.0 + gv * (1.0 - sg)), da * (gv * sg)

    dg, du = run(f"ffn_down_dx_{tag}", _mm, dout_b, wd, "nt", fuse=(act_bwd, [g, u], [BF16, BF16]))
    dwd = run(f"ffn_down_dw_{tag}", _mm, a, dout_b, "tn", out_dtype=BF16)
    mats[("ffn_w_down", tag)] = dwd.reshape(N_SHARDS, dwd.shape[0] // N_SHARDS, dwd.shape[1])
    mats[("ffn_w_gate", tag)] = run(f"ffn_gate_dw_{tag}", _mm, h, dg, "tn", out_dtype=BF16, out_cols=True)
    mats[("ffn_w_up", tag)] = run(f"ffn_up_dw_{tag}", _mm, h, du, "tn", out_dtype=BF16, out_cols=True)
    dh = run(f"ffn_gate_dx_{tag}", _mm, dg, wg, "nt", b_cols=True)
    dh = run(f"ffn_up_dx_{tag}", _mm, du, wu, "nt", add=dh, b_cols=True)
    return _rms_bwd(x, norm_g, dh, dout, f"ffn_norm_bwd_{tag}")


class _Hook:
    def __init__(self, make, done):
        self.make, self.done = make, done


class _SemView:
    def __init__(self, sems, off):
        self.sems, self.off, self.at = sems, off, self

    def __getitem__(self, k):
        return self.sems.at[self.off + k]


def _both(h1, h2):
    split = {}

    def make():
        a, b = h1.make(), h2.make()
        na_in, na_out, na_sems = len(a["arrays"]), len(a["out_shapes"]), a["n_sems"]
        split["n"] = na_out

        def build(cin, cout, send_sems, recv_sems):
            return (a["build"](cin[:na_in], cout[:na_out], send_sems, recv_sems)
                    + b["build"](cin[na_in:], cout[na_out:], _SemView(send_sems, na_sems), _SemView(recv_sems, na_sems)))

        aliases = dict(a.get("aliases", {}))
        aliases.update({na_in + i: na_out + o for i, o in b.get("aliases", {}).items()})
        return dict(build=build, arrays=list(a["arrays"]) + list(b["arrays"]),
                    out_shapes=list(a["out_shapes"]) + list(b["out_shapes"]), n_sems=na_sems + b["n_sems"],
                    aliases=aliases)

    def done(res):
        h1.done(res[:split["n"]])
        h2.done(res[split["n"]:])

    return _Hook(make, done)


def _local_step(x, positions, target, w, hooks=None, mats=None):
    hooks = {} if hooks is None else hooks
    mats = {} if mats is None else mats

    def run(name, fn, *args, **kw):
        hook = hooks.get(name)
        if hook is None:
            return fn(*args, name=name, **kw)
        res, carried = fn(*args, name=name, carry=hook.make(), **kw)
        hook.done(carried)
        return res

    cos, sin = _rope_tables(positions)
    row = lambda v: v.reshape(1, -1)
    gq, gk = jnp.tile(row(w["attn_q_norm"]), (1, 2)), jnp.tile(row(w["attn_k_norm"]), (1, 2))
    sinks = w["attn_sinks"].reshape(-1)
    s = x.shape[0]
    row_stack = lambda g: g.reshape(N_SHARDS, g.shape[0] // N_SHARDS, g.shape[1])

    h0 = _rms_fwd(x, row(w["mixer_norm"][0]), "mixer_norm_0")
    qkv = run("attn_qkv", _mm, h0, w["attn_w_qkv"], "nn", b_cols=True)
    q, kk, vlo, vhi = _attn_prep(qkv, cos, sin, gq, gk)
    o = run("attn_fwd", _attn_fwd, q, kk, vlo, vhi, sinks)
    x1 = run("attn_out", _mm, o, w["attn_w_o"], "nn", add=x)
    ffn_w = lambda l: (row(w["ffn_norm"][l]), w["ffn_w_gate"][l], w["ffn_w_up"][l], w["ffn_w_down"][l])
    x2, ffn0 = _ffn_fwd(run, x1, *ffn_w(0), 0)

    h2 = _rms_fwd(x2, row(w["mixer_norm"][1]), "mixer_norm_1")
    zx = run("ssm_in", _mm, h2, w["ssm_w_in"], "nn")
    dt_g = zx[:, DT_OFF:DT_OFF + SSM_HEADS].reshape(s, SSM_GROUPS, SSM_HPG).transpose(1, 0, 2)
    dt_gt = dt_g.transpose(0, 2, 1)
    bias_r, bias_c = _group_views(w["ssm_dt_bias"].reshape(-1))
    alog_r, alog_c = _group_views(w["ssm_a_log"].reshape(-1))
    d_r, _ = _group_views(w["ssm_d"].reshape(-1))
    xbc = _conv_fwd(zx, w["ssm_conv_w"], row(w["ssm_conv_b"]))
    ssd_args = (xbc, dt_g, dt_gt, bias_r, bias_c, alog_r, alog_c, d_r)
    y, states = _ssd_fwd(*ssd_args)
    yn = _gate_norm_fwd(y, zx, row(w["ssm_norm"]))
    x3 = run("ssm_out", _mm, yn, w["ssm_w_out"], "nn", add=x2)
    x4, ffn1 = _ffn_fwd(run, x3, *ffn_w(1), 1)

    loss_row, dx4, dx4b = _loss_fwd_bwd(x4, target)

    dx3, dx3b, dfn1 = _ffn_bwd(run, mats, x3, *ffn_w(1), ffn1, dx4, dx4b, 1)
    dyn = run("ssm_out_dx", _mm, dx3b, w["ssm_w_out"], "nt")
    mats[("ssm_w_out", 0)] = row_stack(run("ssm_out_dw", _mm, yn, dx3b, "tn", out_dtype=BF16))
    dy, dzx, dssm_norm = _gate_norm_bwd(y, zx, row(w["ssm_norm"]), dyn)
    dxs, db, dc, ddt_g, ddt_gt, dbias, dbias_t, dalog, dalog_t, dd = _ssd_bwd(*ssd_args, states, dy)
    ddt_g = ddt_g + ddt_gt.transpose(0, 2, 1)
    dpre, dconv_w, dconv_b = _conv_bwd_pre(zx, w["ssm_conv_w"], row(w["ssm_conv_b"]), dxs, db, dc)
    dzx = _conv_bwd_in(dpre, w["ssm_conv_w"], dzx)
    ddt_pad = jnp.pad(ddt_g.transpose(1, 0, 2).reshape(s, SSM_HEADS), ((0, 0), (0, SSM_IN_PAD - SSM_IN)))
    dzx = lax.dynamic_update_slice(dzx, ddt_pad.astype(BF16), (0, DT_OFF))
    dw_in = run("ssm_in_dw", _mm, h2, dzx, "tn", out_dtype=BF16)
    in_shard = SSM_IN // N_SHARDS
    mats[("ssm_w_in", 0)] = jnp.stack([dw_in[:, i * in_shard:(i + 1) * in_shard] for i in range(N_SHARDS)])
    dh2 = run("ssm_in_dx", _mm, dzx, w["ssm_w_in"], "nt")
    dx2, dx2b, dmn1 = _rms_bwd(x2, row(w["mixer_norm"][1]), dh2, dx3, "mixer_norm_bwd_1")

    dx1, dx1b, dfn0 = _ffn_bwd(run, mats, x1, *ffn_w(0), ffn0, dx2, dx2b, 0)
    do = run("attn_out_dx", _mm, dx1b, w["attn_w_o"], "nt", out_dtype=BF16)
    mats[("attn_w_o", 0)] = row_stack(run("attn_out_dw", _mm, o, dx1b, "tn", out_dtype=BF16))
    dq, dkc, dkp, dvloc, dvlop, dvhic, dvhip, dsink = run("attn_bwd", _attn_bwd, q, kk, vlo, vhi, sinks, do)
    dqkv, dgq, dgk = _attn_prep_bwd(qkv, cos, sin, gq, gk, dq, (dkc, dkp), (dvloc, dvlop), (dvhic, dvhip))
    mats[("attn_w_qkv", 0)] = run("attn_qkv_dw", _mm, h0, dqkv, "tn", out_dtype=BF16, out_cols=True)
    dh0 = run("attn_qkv_dx", _mm, dqkv, w["attn_w_qkv"], "nt", b_cols=True)
    dx0, _, dmn0 = _rms_bwd(x, row(w["mixer_norm"][0]), dh0, dx1, "mixer_norm_bwd_0")

    fold = lambda v: v[0, :ATT_HEAD_DIM] + v[0, ATT_HEAD_DIM:]
    grads = {
        "mixer_norm": jnp.concatenate([dmn0, dmn1], axis=0),
        "ffn_norm": jnp.concatenate([dfn0, dfn1], axis=0),
        "attn_q_norm": fold(dgq), "attn_k_norm": fold(dgk),
        "attn_sinks": dsink[:, :, 0].reshape(-1),
        "ssm_conv_w": dconv_w, "ssm_conv_b": dconv_b.reshape(-1),
        "ssm_dt_bias": dbias.reshape(-1) + dbias_t.reshape(-1),
        "ssm_a_log": dalog.reshape(-1) + dalog_t.reshape(-1), "ssm_d": dd.reshape(-1),
        "ssm_norm": dssm_norm.reshape(-1),
    }
    return loss_row[0, 0], dx0, grads


OTHER_CHIPS = ((1, 0), (0, 1), (1, 1))


def _position():
    return lax.axis_index("x"), lax.axis_index("y"), lax.axis_index("c")


def _gather_shards(weights, layers):
    n_in, n_mat = len(weights), len(layers)

    def body(*refs):
        p, out = refs[:n_in], refs[n_in:n_in + n_mat]
        send_sems, recv_sems = refs[n_in + n_mat:]
        x, y, c = _position()
        me, sibling = (x, y, c), (x, y, 1 - c)
        chips = [(x ^ fx, y ^ fy) for fx, fy in OTHER_CHIPS]

        def rows(e, px, py, pc):
            half = out[e].shape[1] // 2
            return out[e].at[2 * px + py, pl.ds(pc * half, half), :]

        def copy(k, e, block, to, src=None):
            return pltpu.make_async_remote_copy(
                src_ref=rows(e, *block) if src is None else src, dst_ref=rows(e, *block),
                send_sem=send_sems.at[k * n_mat + e], recv_sem=recv_sems.at[k * n_mat + e],
                device_id=to, device_id_type=MESH)

        def own(e):
            i, l = layers[e]
            return pltpu.make_async_remote_copy(
                src_ref=p[i].at[l], dst_ref=out[e].at[2 * x + y], send_sem=send_sems.at[6 * n_mat + e],
                recv_sem=recv_sems.at[6 * n_mat + e], device_id=sibling, device_id_type=MESH)

        first, passed = [], []
        for e, (i, l) in enumerate(layers):
            half = out[e].shape[1] // 2
            first.append([copy(j, e, me, (*chip, c), src=p[i].at[l, pl.ds(c * half, half), :])
                          for j, chip in enumerate(chips)])
            for cp in first[-1]:
                cp.start()
        for e in range(n_mat):
            own(e).start()
        for e in range(n_mat):
            passed.append([copy(3 + j, e, (*chip, c), sibling) for j, chip in enumerate(chips)])
            for j, chip in enumerate(chips):
                copy(j, e, (*chip, c), me).wait_recv()
                passed[e][j].start()
        for e in range(n_mat):
            own(e).wait()
            for j, chip in enumerate(chips):
                copy(3 + j, e, (*chip, 1 - c), me).wait_recv()
        for e in range(n_mat):
            for cp in first[e] + passed[e]:
                cp.wait_send()

    return pl.pallas_call(
        body, name="gather_weights", in_specs=[ANY] * n_in, out_specs=[ANY] * n_mat,
        out_shape=[jax.ShapeDtypeStruct((N_SHARDS,) + weights[i].shape[1:], weights[i].dtype) for i, _ in layers],
        scratch_shapes=[_sems(7 * n_mat), _sems(7 * n_mat)],
    )(*weights)


def _all_gather8(block, name):
    m_per, n = block.shape

    def body(x_ref, out_ref, send_sems, recv_sems, local_sem):
        x, y, c = _position()
        me, sibling = (x, y, c), (x, y, 1 - c)
        chips = [(x ^ fx, y ^ fy) for fx, fy in OTHER_CHIPS]

        def rows(px, py, pc):
            return out_ref.at[pl.ds((4 * px + 2 * py + pc) * m_per, m_per), :]

        def copy(k, blk, to, src=None):
            return pltpu.make_async_remote_copy(
                src_ref=rows(*blk) if src is None else src, dst_ref=rows(*blk),
                send_sem=send_sems.at[k], recv_sem=recv_sems.at[k], device_id=to, device_id_type=MESH)

        mine = pltpu.make_async_copy(x_ref, rows(*me), local_sem)
        mine.start()
        first = [copy(0, me, sibling, src=x_ref)]
        first += [copy(1 + j, me, (*chip, c), src=x_ref) for j, chip in enumerate(chips)]
        for cp in first:
            cp.start()
        passed = [copy(4 + j, (*chip, c), sibling) for j, chip in enumerate(chips)]
        for j, chip in enumerate(chips):
            copy(1 + j, (*chip, c), me).wait_recv()
            passed[j].start()
        copy(0, sibling, me).wait_recv()
        for j, chip in enumerate(chips):
            copy(4 + j, (*chip, 1 - c), me).wait_recv()
        for cp in first + passed:
            cp.wait_send()
        mine.wait()

    return pl.pallas_call(
        body, name=name, out_shape=jax.ShapeDtypeStruct((N_DEV * m_per, n), block.dtype),
        in_specs=[pl.BlockSpec(memory_space=pltpu.VMEM)], out_specs=pl.BlockSpec(memory_space=pltpu.VMEM),
        scratch_shapes=[_sems(7), _sems(7), pltpu.SemaphoreType.DMA],
    )(block)


def _exchange(carry, name):
    n_in, n_out = len(carry["arrays"]), len(carry["out_shapes"])

    def body(*refs):
        copies = carry["build"](refs[:n_in], refs[n_in:n_in + n_out], refs[-2], refs[-1])
        for cp in copies:
            cp.start()
        for cp in copies:
            cp.wait()

    return pl.pallas_call(
        body, name=name, in_specs=[ANY] * n_in, out_specs=[ANY] * n_out, out_shape=list(carry["out_shapes"]),
        input_output_aliases=dict(carry.get("aliases", {})),
        scratch_shapes=[_sems(carry["n_sems"]), _sems(carry["n_sems"])],
    )(*carry["arrays"])


def _remote(src, dst, send_sems, recv_sems, k, to):
    return pltpu.make_async_remote_copy(src_ref=src, dst_ref=dst, send_sem=send_sems.at[k], recv_sem=recv_sems.at[k],
                                        device_id=to, device_id_type=MESH)


def _gather_over_ici(blocks, layers):
    def build(p, out, send_sems, recv_sems):
        x, y, c = _position()
        copies = []
        for e, l in enumerate(layers):
            half = out[e].shape[1] // 2
            rows = pl.ds(c * half, half)
            for j, (fx, fy) in enumerate(OTHER_CHIPS):
                copies.append(_remote(p[e].at[l, rows, :], out[e].at[2 * x + y, rows, :], send_sems, recv_sems,
                                      3 * e + j, (x ^ fx, y ^ fy, c)))
        return copies

    shapes = [jax.ShapeDtypeStruct((N_SHARDS,) + b.shape[1:], b.dtype) for b in blocks]
    return dict(build=build, arrays=list(blocks), out_shapes=shapes, n_sems=3 * len(layers))


def _gather_over_d2d(stacks, blocks, layers):
    n = len(stacks)

    def build(refs, out, send_sems, recv_sems):
        p = refs[n:]
        x, y, c = _position()
        sibling = (x, y, 1 - c)
        copies = []
        for e, l in enumerate(layers):
            half = out[e].shape[1] // 2
            for j, (fx, fy) in enumerate(OTHER_CHIPS):
                rows = out[e].at[2 * (x ^ fx) + (y ^ fy), pl.ds(c * half, half), :]
                copies.append(_remote(rows, rows, send_sems, recv_sems, 4 * e + j, sibling))
            copies.append(_remote(p[e].at[l], out[e].at[2 * x + y], send_sems, recv_sems, 4 * e + 3, sibling))
        return copies

    shapes = [jax.ShapeDtypeStruct(s.shape, s.dtype) for s in stacks]
    return dict(build=build, arrays=list(stacks) + list(blocks), out_shapes=shapes, n_sems=4 * n,
                aliases={i: i for i in range(n)})


def _grads_to_sibling(stacks):
    def build(g, out, send_sems, recv_sems):
        x, y, c = _position()
        copies = []
        for e in range(len(stacks)):
            half = g[e].shape[1] // 2
            copies.append(_remote(g[e].at[:, pl.ds((1 - c) * half, half), :], out[e], send_sems, recv_sems, e,
                                  (x, y, 1 - c)))
        return copies

    shapes = [jax.ShapeDtypeStruct((N_SHARDS, g.shape[1] // 2, g.shape[2]), g.dtype) for g in stacks]
    return dict(build=build, arrays=list(stacks), out_shapes=shapes, n_sems=len(stacks))


def _grads_to_owners(partials):
    def build(p, out, send_sems, recv_sems):
        x, y, c = _position()
        copies = []
        for e in range(len(partials)):
            for k, (fx, fy) in enumerate(OTHER_CHIPS):
                px, py = x ^ fx, y ^ fy
                copies.append(_remote(p[e].at[2 * px + py], out[e].at[k], send_sems, recv_sems, 3 * e + k,
                                      (px, py, c)))
        return copies

    shapes = [jax.ShapeDtypeStruct((len(OTHER_CHIPS),) + p.shape[1:], p.dtype) for p in partials]
    return dict(build=build, arrays=list(partials), out_shapes=shapes, n_sems=3 * len(partials))


def _share_halves(grads, layers):
    def build(_, out, send_sems, recv_sems):
        x, y, c = _position()
        copies = []
        for e, (i, l) in enumerate(layers):
            half = out[i].shape[1] // 2
            rows = out[i].at[l, pl.ds(c * half, half), :]
            copies.append(_remote(rows, rows, send_sems, recv_sems, e, (x, y, 1 - c)))
        return copies

    return dict(build=build, arrays=list(grads), out_shapes=[jax.ShapeDtypeStruct(g.shape, g.dtype) for g in grads],
                n_sems=len(layers), aliases={i: i for i in range(len(grads))})


ADD_BLOCK_ELEMS = 1 << 19


def _add_rows(half, cols):
    return _tile(half, max(16, ADD_BLOCK_ELEMS // cols // 16 * 16), 16)


def _add_pair(stack, recv, c_idx, name):
    _, half, cols = recv.shape
    tr = _add_rows(half, cols)
    nt = half // tr

    def body(c_ref, a_ref, b_ref, o_ref):
        o_ref[...] = (a_ref[...].astype(F32) + b_ref[...].astype(F32)).astype(o_ref.dtype)

    blk = pl.BlockSpec((None, tr, cols), lambda s, i, c_ref: (s, i, 0))
    return pl.pallas_call(
        body, name=name,
        grid_spec=pltpu.PrefetchScalarGridSpec(
            num_scalar_prefetch=1, grid=(N_SHARDS, nt),
            in_specs=[pl.BlockSpec((None, tr, cols), lambda s, i, c_ref: (s, c_ref[0] * nt + i, 0)), blk],
            out_specs=blk),
        out_shape=jax.ShapeDtypeStruct(recv.shape, recv.dtype),
        compiler_params=_params("parallel", "parallel"),
    )(c_idx, stack, recv)


def _add_owned(partial, recv, sc_idx, layer, shape, into, name):
    _, half, cols = partial.shape
    tr = _add_rows(half, cols)
    nt = half // tr

    def body(sc_ref, a_ref, r0_ref, r1_ref, r2_ref, *rest):
        o_ref = rest[-1]
        o_ref[...] = (((a_ref[...].astype(F32) + r0_ref[...].astype(F32)) + r1_ref[...].astype(F32))
                      + r2_ref[...].astype(F32))

    slot = lambda k: pl.BlockSpec((None, tr, cols), lambda i, sc_ref: (k, i, 0))
    has_into = into is not None
    return pl.pallas_call(
        body, name=name,
        grid_spec=pltpu.PrefetchScalarGridSpec(
            num_scalar_prefetch=1, grid=(nt,),
            in_specs=[pl.BlockSpec((None, tr, cols), lambda i, sc_ref: (sc_ref[0], i, 0)), slot(0), slot(1), slot(2)]
            + ([ANY] if has_into else []),
            out_specs=pl.BlockSpec((None, tr, cols), lambda i, sc_ref: (layer, sc_ref[1] * nt + i, 0))),
        out_shape=jax.ShapeDtypeStruct(shape, F32),
        input_output_aliases={5: 0} if has_into else {},
        compiler_params=_params("parallel"),
    )(*((sc_idx, partial, recv, recv, recv) + ((into,) if has_into else ())))


def _sum8(gathered):
    m = gathered.shape[0] // N_DEV

    def body(g_ref, o_ref):
        total = g_ref[0:m, :]
        for d in range(1, N_DEV):
            total = total + g_ref[d * m:(d + 1) * m, :]
        o_ref[...] = total

    return pl.pallas_call(
        body, name="small_grads_sum", out_shape=jax.ShapeDtypeStruct((m, LANES), F32),
        in_specs=[pl.BlockSpec(memory_space=pltpu.VMEM)], out_specs=pl.BlockSpec(memory_space=pltpu.VMEM),
    )(gathered)


ADAMW_BLOCK_ELEMS = 1 << 18


def _adamw(w, g, m, v, name):
    l, r, cols = w.shape
    tr = _tile(r, max(8, ADAMW_BLOCK_ELEMS // cols // 8 * 8), 8)

    def body(w_ref, g_ref, m_ref, v_ref, go_ref, d_ref, nm_ref, nv_ref):
        gv = g_ref[...]
        go_ref[...] = gv
        nm = ADAM_B1 * m_ref[...] + (1.0 - ADAM_B1) * gv
        nv = ADAM_B2 * v_ref[...] + (1.0 - ADAM_B2) * jnp.square(gv)
        m_hat = nm / (1.0 - ADAM_B1 ** ADAM_STEP)
        v_hat = nv / (1.0 - ADAM_B2 ** ADAM_STEP)
        d_ref[...] = -ADAM_LR * (m_hat / (jnp.sqrt(v_hat) + ADAM_EPS) + ADAM_WD * w_ref[...])
        nm_ref[...] = nm
        nv_ref[...] = nv

    blk = pl.BlockSpec((None, tr, cols), lambda a, i: (a, i, 0))
    return pl.pallas_call(
        body, name=name, grid=(l, r // tr), in_specs=[blk] * 4, out_specs=[blk] * 4,
        out_shape=[jax.ShapeDtypeStruct(w.shape, F32)] * 4, compiler_params=_params("parallel", "parallel"),
    )(w, g, m, v)


WEIGHTS = ("mixer_norm", "ffn_norm", "attn_w_qkv", "attn_q_norm", "attn_k_norm", "attn_sinks", "attn_w_o",
           "ssm_w_in", "ssm_conv_w", "ssm_conv_b", "ssm_dt_bias", "ssm_a_log", "ssm_d", "ssm_norm", "ssm_w_out",
           "ffn_w_gate", "ffn_w_up", "ffn_w_down")
BIG = ("attn_w_qkv", "attn_w_o", "ffn_w_gate", "ffn_w_up", "ffn_w_down", "ssm_w_in", "ssm_w_out")
MATRICES = (("attn_w_qkv", 0), ("attn_w_o", 0), ("ffn_w_gate", 0), ("ffn_w_up", 0), ("ffn_w_down", 0),
            ("ssm_w_in", 0), ("ssm_w_out", 0), ("ffn_w_gate", 1), ("ffn_w_up", 1), ("ffn_w_down", 1))
MATRIX_LAYERS = tuple((BIG.index(n), l) for n, l in MATRICES)
GROUPS = {"attn": MATRICES[0:2], "ffn0": MATRICES[2:5], "ssm": MATRICES[5:7], "ffn1": MATRICES[7:10]}
SMALL_SHARDED = ("ssm_conv_w", "ssm_conv_b", "ssm_norm")
SMALL = tuple(n for n in WEIGHTS if n not in BIG)


def _pack_rows(parts, row_unit=8):
    flat = jnp.concatenate([p.reshape(-1) for p in parts])
    pad = (-flat.shape[0]) % (LANES * row_unit)
    return jnp.pad(flat, (0, pad)).reshape(-1, LANES)


def _unpack(flat, shapes):
    out, off = [], 0
    for shp in shapes:
        size = math.prod(shp)
        out.append(flat[off:off + size].reshape(shp))
        off += size
    return out


def kernel(x, positions, mixer_norm, ffn_norm, attn_w_qkv, attn_q_norm, attn_k_norm, attn_sinks, attn_w_o, ssm_w_in, ssm_conv_w, ssm_conv_b, ssm_dt_bias, ssm_a_log, ssm_d, ssm_norm, ssm_w_out, ffn_w_gate, ffn_w_up, ffn_w_down, loss_target, m_mixer_norm, m_ffn_norm, m_attn_w_qkv, m_attn_q_norm, m_attn_k_norm, m_attn_sinks, m_attn_w_o, m_ssm_w_in, m_ssm_conv_w, m_ssm_conv_b, m_ssm_dt_bias, m_ssm_a_log, m_ssm_d, m_ssm_norm, m_ssm_w_out, m_ffn_w_gate, m_ffn_w_up, m_ffn_w_down, v_mixer_norm, v_ffn_norm, v_attn_w_qkv, v_attn_q_norm, v_attn_k_norm, v_attn_sinks, v_attn_w_o, v_ssm_w_in, v_ssm_conv_w, v_ssm_conv_b, v_ssm_dt_bias, v_ssm_a_log, v_ssm_d, v_ssm_norm, v_ssm_w_out, v_ffn_w_gate, v_ffn_w_up, v_ffn_w_down):
    args = locals()
    w = {n: args[n] for n in WEIGHTS}
    m = {n: args["m_" + n] for n in WEIGHTS}
    v = {n: args["v_" + n] for n in WEIGHTS}
    ax, ay, ac = lax.axis_index("x"), lax.axis_index("y"), lax.axis_index("c")
    shard = 2 * ax + ay

    wb = {n: w[n].astype(BF16) for n in BIG}
    wl, hooks = {"ffn_w_gate": [None, None], "ffn_w_up": [None, None], "ffn_w_down": [None, None]}, {}

    def gathered(keys, stacks):
        for (n, l), st in zip(keys, stacks):
            if n == "ssm_w_in":
                wl[n] = jnp.concatenate([st[i] for i in range(N_SHARDS)]
                                        + [jnp.zeros((st.shape[1], SSM_IN_PAD - SSM_IN), BF16)], axis=1)
            elif n in ("ffn_w_gate", "ffn_w_up"):
                wl[n][l] = st
            elif n == "ffn_w_down":
                wl[n][l] = st.reshape(st.shape[0] * st.shape[1], st.shape[2])
            elif n == "attn_w_qkv":
                wl[n] = st
            else:
                wl[n] = st.reshape(st.shape[0] * st.shape[1], st.shape[2])

    def behind(name, hook):
        hooks[name] = _both(hooks[name], hook) if name in hooks else hook

    def gather_behind(keys, first_leg, second_leg):
        blocks, layers, got = [wb[n] for n, _ in keys], [l for _, l in keys], {}
        behind(first_leg, _Hook(lambda: _gather_over_ici(blocks, layers), lambda res: got.update(stacks=res)))
        behind(second_leg, _Hook(lambda: _gather_over_d2d(got["stacks"], blocks, layers),
                                 lambda res: gathered(keys, res)))

    gathered(GROUPS["attn"], _gather_shards([wb[n] for n, _ in GROUPS["attn"]],
                                            [(e, l) for e, (_, l) in enumerate(GROUPS["attn"])]))
    gather_behind(GROUPS["ffn0"], "attn_fwd", "attn_out")
    gather_behind(GROUPS["ssm"][:1], "ffn_gate_0", "ffn_up_0")
    gather_behind(GROUPS["ssm"][1:], "ffn_up_0", "ffn_down_0")
    gather_behind(GROUPS["ffn1"], "ssm_in", "ssm_out")
    small_shapes = [w[n].shape for n in SMALL_SHARDED]
    small_all = _all_gather8(_pack_rows([w[n] for n in SMALL_SHARDED]), "gather_small_params")
    small_all = small_all.reshape(N_DEV, -1)[::2]
    full, off = {}, 0
    for n, shp in zip(SMALL_SHARDED, small_shapes):
        size = math.prod(shp)
        seg = small_all[:, off:off + size].reshape((N_SHARDS,) + shp)
        full[n] = jnp.moveaxis(seg, 0, -2).reshape(shp[:-1] + (N_SHARDS * shp[-1],))
        off += size
    wl.update({
        "mixer_norm": mixer_norm, "ffn_norm": ffn_norm,
        "attn_q_norm": attn_q_norm[0], "attn_k_norm": attn_k_norm[0], "attn_sinks": attn_sinks[0],
        "ssm_conv_w": full["ssm_conv_w"][0], "ssm_conv_b": full["ssm_conv_b"][0],
        "ssm_dt_bias": ssm_dt_bias[0], "ssm_a_log": ssm_a_log[0], "ssm_d": ssm_d[0],
        "ssm_norm": full["ssm_norm"][0],
    })

    c_idx = ac.reshape(1).astype(jnp.int32)
    sc_idx = jnp.stack([shard, ac]).astype(jnp.int32)
    mats, halves = {}, {n: None for n in BIG}

    def pair_sums(keys, recv):
        return [_add_pair(mats[k], r, c_idx, f"grads_add_pair_{k[0]}_{k[1]}") for k, r in zip(keys, recv)]

    def owner_sums(keys, partials, recv):
        for (n, l), p, r in zip(keys, partials, recv):
            halves[n] = _add_owned(p, r, sc_idx, l, w[n].shape, halves[n], f"grads_add_owned_{n}_{l}")

    def reduce_behind(keys, first_leg, second_leg):
        got = {}
        behind(first_leg, _Hook(lambda: _grads_to_sibling([mats[k] for k in keys]),
                                lambda res: got.update(partials=pair_sums(keys, res))))
        behind(second_leg, _Hook(lambda: _grads_to_owners(got["partials"]),
                                 lambda res: owner_sums(keys, got["partials"], res)))

    reduce_behind(GROUPS["ffn1"], "ssm_out_dx", "ssm_in_dw")
    reduce_behind(GROUPS["ssm"], "ssm_in_dx", "ffn_down_dx_0")
    reduce_behind(GROUPS["ffn0"], "attn_out_dx", "attn_bwd")
    reduce_behind(GROUPS["attn"][1:], "attn_bwd", "attn_qkv_dw")
    last, early, tail = GROUPS["attn"][0], [n for n in BIG if n != "attn_w_qkv"], {}
    behind("attn_qkv_dx", _Hook(lambda: _grads_to_sibling([mats[last]]),
                                lambda res: tail.update(partials=pair_sums([last], res))))
    behind("attn_qkv_dx", _Hook(
        lambda: _share_halves([halves[n] for n in early], [(early.index(n), l) for n, l in MATRICES if n in early]),
        lambda res: tail.update(grads=dict(zip(early, res)))))
    loss_part, dx, g_full = _local_step(x[0], positions[0], loss_target[0], wl, hooks, mats)
    owner_sums([last], tail["partials"], _exchange(_grads_to_owners(tail["partials"]), "grads_to_owners"))
    grads = tail["grads"]
    grads[last[0]], = _exchange(_share_halves([halves[last[0]]], [(0, 0)]), "grads_share_halves")

    small_full_shapes = [g_full[n].shape for n in SMALL] + [(1,)]
    small_g = _pack_rows([g_full[n] for n in SMALL] + [loss_part.reshape(1)])
    small_sum = _sum8(_all_gather8(small_g, "gather_small_grads")).reshape(-1)
    *small_list, loss = _unpack(small_sum, small_full_shapes)
    for n, g in zip(SMALL, small_list):
        if n in SMALL_SHARDED:
            width = w[n].shape[-1]
            g = lax.dynamic_slice_in_dim(g, shard * width, width, axis=g.ndim - 1)
        grads[n] = g.reshape(w[n].shape)

    delta, new_m, new_v = {}, {}, {}
    for n in BIG:
        grads[n], delta[n], new_m[n], new_v[n] = _adamw(w[n], grads[n], m[n], v[n], "adamw_" + n)
    small_local = [w[n].shape for n in SMALL]
    pk = lambda t: _pack_rows([t[n] for n in SMALL])[None]
    outs = _adamw(pk(w), pk(grads), pk(m), pk(v), "adamw_small")
    for res, o in zip((delta, new_m, new_v), outs[1:]):
        for n, a in zip(SMALL, _unpack(o.reshape(-1), small_local)):
            res[n] = a

    return (loss.reshape(()), dx[None], *[grads[n] for n in WEIGHTS], *[delta[n] for n in WEIGHTS],
            *[new_m[n] for n in WEIGHTS], *[new_v[n] for n in WEIGHTS])
```

```python
import math

import jax
import jax.numpy as jnp
from jax import lax
from jax.experimental import pallas as pl
from jax.experimental.pallas import tpu as pltpu

F32 = jnp.float32
BF16 = jnp.bfloat16

D_MODEL = 2048
EPS = 1e-6
ATT_HEAD_DIM = 64
ATT_Q_HEADS = 32
ATT_KV_HEADS = 4
ATT_GROUP = 8
ATT_BLOCK = 128
ROPE_THETA = 10000.0
Q_WIDTH = ATT_Q_HEADS * ATT_HEAD_DIM
KV_WIDTH = ATT_KV_HEADS * ATT_HEAD_DIM
SSM_D_INNER = 4096
SSM_HEADS = 64
SSM_GROUPS = 8
SSM_HPG = 8
SSM_P = 64
SSM_STATE = 128
SSM_CONV = 4
SSM_CHUNK = 256
SSM_CONV_DIM = 6144
SSM_GN = SSM_D_INNER // SSM_GROUPS
SSM_IN = SSM_D_INNER + SSM_CONV_DIM + SSM_HEADS
LANES = 128
SSM_IN_PAD = -(-SSM_IN // LANES) * LANES
N_SHARDS = 4
N_DEV = 8

ADAM_LR = 0.001
ADAM_B1 = 0.9
ADAM_B2 = 0.999
ADAM_EPS = 1e-08
ADAM_WD = 0.01
ADAM_STEP = 10

VMEM_LIMIT = 56 * 1024 * 1024
MESH = pl.DeviceIdType.MESH
ANY = pl.BlockSpec(memory_space=pl.ANY)


def _params(*sem):
    return pltpu.CompilerParams(dimension_semantics=sem, vmem_limit_bytes=VMEM_LIMIT)


def _sems(n):
    return pltpu.SemaphoreType.DMA((n,))


def _call(body, carry, name, grid, in_specs, out_specs, out_shape, scratch_shapes, sem, args):
    if carry is None:
        return pl.pallas_call(body, name=name, grid=grid, in_specs=in_specs, out_specs=out_specs,
                              out_shape=out_shape, scratch_shapes=scratch_shapes,
                              compiler_params=_params(*sem))(*args)
    n_in, n_out, n_scr = len(in_specs), len(out_specs), len(scratch_shapes)
    c_arrays, c_shapes = list(carry["arrays"]), list(carry["out_shapes"])
    n_cin, n_cout = len(c_arrays), len(c_shapes)

    def carrying(*refs):
        ins, refs = refs[:n_in], refs[n_in:]
        cin, refs = refs[:n_cin], refs[n_cin:]
        outs, refs = refs[:n_out], refs[n_out:]
        cout, refs = refs[:n_cout], refs[n_cout:]
        scratch, (send_sems, recv_sems) = refs[:n_scr], refs[n_scr:]
        copies = carry["build"](cin, cout, send_sems, recv_sems)
        ids = [pl.program_id(d) for d in range(len(grid))]
        first, last = ids[0] == 0, ids[0] == grid[0] - 1
        for d in range(1, len(grid)):
            first = jnp.logical_and(first, ids[d] == 0)
            last = jnp.logical_and(last, ids[d] == grid[d] - 1)

        @pl.when(first)
        def _():
            for cp in copies:
                cp.start()

        body(*ins, *outs, *scratch)

        @pl.when(last)
        def _():
            for cp in copies:
                cp.wait()

    aliases = {n_in + i: n_out + o for i, o in carry.get("aliases", {}).items()}
    return pl.pallas_call(
        carrying, name=name, grid=grid, in_specs=list(in_specs) + [ANY] * n_cin,
        out_specs=list(out_specs) + [ANY] * n_cout, out_shape=list(out_shape) + c_shapes,
        scratch_shapes=list(scratch_shapes) + [_sems(carry["n_sems"]), _sems(carry["n_sems"])],
        input_output_aliases=aliases, compiler_params=_params(*(["arbitrary"] * len(grid))))(*args, *c_arrays)


def _tile(dim, target, unit=LANES):
    if dim <= target:
        return dim
    t = (target // unit) * unit
    while t >= unit:
        if dim % t == 0:
            return t
        t -= unit
    return dim


def _dot(a, b):
    return lax.dot_general(a, b, (((1,), (0,)), ((), ())), preferred_element_type=F32)


def _dot_nt(a, b):
    return lax.dot_general(a, b, (((1,), (1,)), ((), ())), preferred_element_type=F32)


def _dot_tn(a, b):
    return lax.dot_general(a, b, (((0,), (0,)), ((), ())), preferred_element_type=F32)


def _split3(x):
    hi = x.astype(BF16)
    r1 = x - hi.astype(F32)
    mid = r1.astype(BF16)
    lo = (r1 - mid.astype(F32)).astype(BF16)
    return hi, mid, lo


def _dot_x(x, m):
    hi, mid, lo = _split3(x)
    return _dot(hi, m) + _dot(mid, m) + _dot(lo, m)


def _dot_x2(x, m):
    hi = x.astype(BF16)
    return _dot(hi, m) + _dot((x - hi.astype(F32)).astype(BF16), m)


def _xdot(m, x):
    hi, mid, lo = _split3(x)
    return _dot(m, hi) + _dot(m, mid) + _dot(m, lo)


def _dot_x2_nt(x, m):
    hi = x.astype(BF16)
    return _dot_nt(hi, m) + _dot_nt((x - hi.astype(F32)).astype(BF16), m)


def _dot_x_nt(x, m):
    hi, mid, lo = _split3(x)
    return _dot_nt(hi, m) + _dot_nt(mid, m) + _dot_nt(lo, m)


def _iota(shape, dim):
    return lax.broadcasted_iota(jnp.int32, shape, dim)


def _sigmoid(x):
    return 0.5 * jnp.tanh(0.5 * x) + 0.5


def _softplus(x):
    return jnp.maximum(x, 0.0) + jnp.log(1.0 + jnp.exp(-jnp.abs(x)))


MM_ROWS = 1024
MM_TILE = 1408
MM_DEPTH = 3456


def _mm(a, b, mode, name, add=None, out_dtype=F32, b_cols=False, out_cols=False, fuse=None, rows=MM_ROWS,
        carry=None):
    bs = b.shape[-2:]
    if b_cols:
        bs = (bs[0], N_SHARDS * bs[1])
    if mode == "nn":
        (m, k), (k2, n) = a.shape, bs
    elif mode == "nt":
        (m, k), (n, k2) = a.shape, bs
    else:
        (k, m), (k2, n) = a.shape, bs
    assert k == k2, (a.shape, b.shape, mode)
    split_n = (b_cols and mode == "nn") or out_cols
    split_k = b_cols and mode == "nt"
    tm = _tile(m, MM_TILE if mode == "tn" else rows)
    tn = _tile(n // N_SHARDS if split_n else n, MM_TILE)
    tk = _tile(k // N_SHARDS if split_k else k, MM_DEPTH)
    nk = k // tk
    nj, nq = (n // N_SHARDS) // tn, (k // N_SHARDS) // tk
    two_shards = split_k and nq == 1
    if two_shards:
        nk = N_SHARDS // 2
    if mode == "tn":
        a_spec = pl.BlockSpec((tk, tm), lambda i, j, q: (q, i))
    else:
        a_spec = pl.BlockSpec((tm, 2 * tk if two_shards else tk), lambda i, j, q: (i, q))
    if mode == "nt":
        if two_shards:
            b_spec = pl.BlockSpec((2, tn, tk), lambda i, j, q: (q, j, 0))
        elif b_cols:
            b_spec = pl.BlockSpec((None, tn, tk), lambda i, j, q: (q // nq, j, q % nq))
        else:
            b_spec = pl.BlockSpec((tn, tk), lambda i, j, q: (j, q))
    elif b_cols:
        b_spec = pl.BlockSpec((None, tk, tn), lambda i, j, q: (j // nj, q, j % nj))
    else:
        b_spec = pl.BlockSpec((tk, tn), lambda i, j, q: (q, j))
    add_spec = pl.BlockSpec((tm, tn), lambda i, j, q: (i, j))
    if out_cols:
        o_spec = pl.BlockSpec((None, tm, tn), lambda i, j, q: (j // nj, i, j % nj))
        o_shape = (N_SHARDS, m, n // N_SHARDS)
    else:
        o_spec, o_shape = add_spec, (m, n)
    dot = {"nn": _dot, "nt": _dot_nt, "tn": _dot_tn}[mode]
    has_add = add is not None
    fuse_fn, extra, out_dtypes = fuse if fuse is not None else (None, [], [out_dtype])
    n_in, n_out = 2 + has_add + len(extra), len(out_dtypes)

    def body(*refs):
        a_ref, b_ref = refs[:2]
        add_ref = refs[2] if has_add else None
        extra_refs = refs[2 + has_add:n_in]
        o_refs, acc_ref = refs[n_in:n_in + n_out], refs[n_in + n_out]
        if two_shards:
            part = (dot(a_ref[:, :tk].astype(BF16), b_ref[0].astype(BF16))
                    + dot(a_ref[:, tk:].astype(BF16), b_ref[1].astype(BF16)))
        else:
            part = dot(a_ref[...].astype(BF16), b_ref[...].astype(BF16))

        def finish(total):
            if has_add:
                total = total + add_ref[...].astype(F32)
            outs = (total,) if fuse_fn is None else fuse_fn(total, *[r[...] for r in extra_refs])
            for o_ref, val in zip(o_refs, outs):
                o_ref[...] = val.astype(o_ref.dtype)

        if nk == 1:
            finish(part)
        else:
            q = pl.program_id(2)

            @pl.when(q == 0)
            def _():
                acc_ref[...] = part

            @pl.when(jnp.logical_and(q > 0, q < nk - 1))
            def _():
                acc_ref[...] += part

            @pl.when(q == nk - 1)
            def _():
                finish(acc_ref[...] + part)

    in_specs = [a_spec, b_spec] + [add_spec] * (has_add + len(extra))
    args = (a, b) + ((add,) if has_add else ()) + tuple(extra)
    res = _call(body, carry, name, (m // tm, n // tn, nk), in_specs, [o_spec] * n_out,
                [jax.ShapeDtypeStruct(o_shape, dt) for dt in out_dtypes],
                [pltpu.VMEM((tm, tn) if nk > 1 else (8, LANES), F32)], ("parallel", "parallel", "arbitrary"), args)
    main = res[0] if fuse is None else res[:n_out]
    return main if carry is None else (main, res[n_out:])


def _rms_fwd(x, g, name):
    s, d = x.shape
    ts = _tile(s, 512, 8)

    def body(x_ref, g_ref, o_ref):
        xv = x_ref[...]
        r = lax.rsqrt(jnp.mean(xv * xv, axis=-1, keepdims=True) + EPS)
        o_ref[...] = (xv * r * g_ref[...]).astype(BF16)

    return pl.pallas_call(
        body, name=name, grid=(s // ts,),
        in_specs=[pl.BlockSpec((ts, d), lambda i: (i, 0)), pl.BlockSpec((1, d), lambda i: (0, 0))],
        out_specs=pl.BlockSpec((ts, d), lambda i: (i, 0)),
        out_shape=jax.ShapeDtypeStruct((s, d), BF16),
        compiler_params=_params("parallel"),
    )(x, g)


def _rms_bwd(x, g, dh, dres, name):
    s, d = x.shape
    ts = _tile(s, 512, 8)

    def body(x_ref, g_ref, dh_ref, dres_ref, dx_ref, dxb_ref, dg_ref):
        xv = x_ref[...]
        r = lax.rsqrt(jnp.mean(xv * xv, axis=-1, keepdims=True) + EPS)
        xhat = xv * r
        dhv = dh_ref[...].astype(F32)
        part = jnp.sum(dhv * xhat, axis=0, keepdims=True)

        @pl.when(pl.program_id(0) == 0)
        def _():
            dg_ref[...] = part

        @pl.when(pl.program_id(0) > 0)
        def _():
            dg_ref[...] += part

        dxh = dhv * g_ref[...]
        dx = r * (dxh - xhat * jnp.mean(dxh * xhat, axis=-1, keepdims=True))
        total = dres_ref[...] + dx
        dx_ref[...] = total
        dxb_ref[...] = total.astype(BF16)

    row = pl.BlockSpec((ts, d), lambda i: (i, 0))
    vec = pl.BlockSpec((1, d), lambda i: (0, 0))
    return pl.pallas_call(
        body, name=name, grid=(s // ts,),
        in_specs=[row, vec, row, row], out_specs=[row, row, vec],
        out_shape=[jax.ShapeDtypeStruct((s, d), F32), jax.ShapeDtypeStruct((s, d), BF16),
                   jax.ShapeDtypeStruct((1, d), F32)],
        compiler_params=_params("arbitrary"),
    )(x, g, dh, dres)


def _loss_fwd_bwd(y, target):
    s, d = y.shape
    ts = _tile(s, 512, 8)

    def body(y_ref, t_ref, l_ref, dy_ref, dyb_ref):
        diff = y_ref[...] - t_ref[...]
        dy_ref[...] = diff * (1.0 / d)
        dyb_ref[...] = (diff * (1.0 / d)).astype(BF16)
        part = jnp.full((1, LANES), 0.5 * jnp.sum(jnp.mean(diff * diff, axis=-1, keepdims=True)), F32)

        @pl.when(pl.program_id(0) == 0)
        def _():
            l_ref[...] = part

        @pl.when(pl.program_id(0) > 0)
        def _():
            l_ref[...] += part

    row = pl.BlockSpec((ts, d), lambda i: (i, 0))
    acc = pl.BlockSpec((1, LANES), lambda i: (0, 0))
    return pl.pallas_call(
        body, name="loss", grid=(s // ts,), in_specs=[row, row], out_specs=[acc, row, row],
        out_shape=[jax.ShapeDtypeStruct((1, LANES), F32), jax.ShapeDtypeStruct((s, d), F32),
                   jax.ShapeDtypeStruct((s, d), BF16)],
        compiler_params=_params("arbitrary"),
    )(y, target)


def _lane_consts():
    r, c = _iota((LANES, LANES), 0), _iota((LANES, LANES), 1)
    same = (r >> 6) == (c >> 6)
    rin, cin = r & 63, c & 63
    one = lambda cond: jnp.where(cond, 1.0, 0.0).astype(BF16)
    return dict(
        seg=one(same),
        rot=(jnp.where(same & (rin == cin + 32), -1.0, 0.0)
             + jnp.where(same & (cin == rin + 32), 1.0, 0.0)).astype(BF16),
        dup_lo=one(r == cin), dup_hi=one(r == cin + 64),
        up=one((c >= 64) & (r == c - 64)), down=one((c < 64) & (r == c + 64)),
        fold_lo=one((c < 64) & (rin == c)), fold_hi=one((c >= 64) & (rin == c - 64)),
    )


def _norm_rope(xc, gain, cos, sin, k):
    ss = _dot_x2(xc * xc, k["seg"])
    rinv = lax.rsqrt(ss * (1.0 / ATT_HEAD_DIM) + EPS)
    xhat = xc * rinv
    y = xhat * gain
    return y * cos + _dot_x2(y, k["rot"]) * sin, xhat, rinv


def _norm_rope_bwd(dr, xhat, rinv, gain, cos, sin, k):
    dy = dr * cos - _dot_x2(dr * sin, k["rot"])
    dgain = jnp.sum(dy * xhat, axis=0, keepdims=True)
    dxh = dy * gain
    dx = rinv * (dxh - xhat * (_dot_x2(dxh * xhat, k["seg"]) * (1.0 / ATT_HEAD_DIM)))
    return dx, dgain


def _attn_prep(qkv, cos, sin, gq, gk):
    s = qkv.shape[0]
    tr = _tile(s, 256, 8)

    def body(x_ref, cos_ref, sin_ref, gq_ref, gk_ref, q_ref, kk_ref, vlo_ref, vhi_ref):
        k = _lane_consts()
        cosv, sinv = cos_ref[...], sin_ref[...]
        lane = _iota((tr, LANES), 1)
        for j in range(Q_WIDTH // LANES):
            r, _, _ = _norm_rope(x_ref[:, j * LANES:(j + 1) * LANES], gq_ref[...], cosv, sinv, k)
            q_ref[:, j * LANES:(j + 1) * LANES] = r.astype(BF16)
        for i in range(KV_WIDTH // LANES):
            off = Q_WIDTH + i * LANES
            r, _, _ = _norm_rope(x_ref[:, off:off + LANES], gk_ref[...], cosv, sinv, k)
            rb = r.astype(BF16)
            kk_ref[:, (2 * i) * LANES:(2 * i + 1) * LANES] = _dot(rb, k["dup_lo"]).astype(BF16)
            kk_ref[:, (2 * i + 1) * LANES:(2 * i + 2) * LANES] = _dot(rb, k["dup_hi"]).astype(BF16)
            off = Q_WIDTH + KV_WIDTH + i * LANES
            vb = x_ref[:, off:off + LANES].astype(BF16)
            zero = jnp.zeros_like(vb)
            vlo_ref[:, (2 * i) * LANES:(2 * i + 1) * LANES] = jnp.where(lane < 64, vb, zero)
            vhi_ref[:, (2 * i) * LANES:(2 * i + 1) * LANES] = _dot(vb, k["up"]).astype(BF16)
            vlo_ref[:, (2 * i + 1) * LANES:(2 * i + 2) * LANES] = _dot(vb, k["down"]).astype(BF16)
            vhi_ref[:, (2 * i + 1) * LANES:(2 * i + 2) * LANES] = jnp.where(lane >= 64, vb, zero)

    w = qkv.shape[1]
    row = lambda width: pl.BlockSpec((tr, width), lambda i: (i, 0))
    vec = pl.BlockSpec((1, LANES), lambda i: (0, 0))
    kw = ATT_KV_HEADS * LANES
    return pl.pallas_call(
        body, name="attn_prep", grid=(s // tr,),
        in_specs=[row(w), row(LANES), row(LANES), vec, vec],
        out_specs=[row(Q_WIDTH), row(kw), row(kw), row(kw)],
        out_shape=[jax.ShapeDtypeStruct((s, Q_WIDTH), BF16)] + [jax.ShapeDtypeStruct((s, kw), BF16)] * 3,
        compiler_params=_params("parallel"),
    )(qkv, cos, sin, gq, gk)


def _softmax_sink(s, valid, sink):
    s = jnp.where(valid, s, -jnp.inf)
    m = jnp.maximum(jnp.max(s, axis=-1, keepdims=True), sink)
    p = jnp.exp(s - m)
    esink = jnp.exp(sink - m)
    inv = 1.0 / (jnp.sum(p, axis=-1, keepdims=True) + esink)
    return p * inv, esink * inv


def _attn_specs(order):
    if order == "nh":
        cur = lambda n, h: (n, h)
        prev = lambda n, h: (jnp.maximum(n - 1, 0), h)
    else:
        cur = lambda h, n: (n, h)
        prev = lambda h, n: (jnp.maximum(n - 1, 0), h)
    qs = pl.BlockSpec((ATT_BLOCK, ATT_GROUP * ATT_HEAD_DIM), cur)
    kc = pl.BlockSpec((ATT_BLOCK, LANES), cur)
    kp = pl.BlockSpec((ATT_BLOCK, LANES), prev)
    return qs, kc, kp


def _pair_rows(qp):
    lane = _iota((ATT_BLOCK, LANES), 1)
    zero = jnp.zeros_like(qp)
    return jnp.concatenate([jnp.where(lane < 64, qp, zero), jnp.where(lane >= 64, qp, zero)], axis=0)


def _pair_masks(n):
    qi = _iota((2 * ATT_BLOCK, 2 * ATT_BLOCK), 0) & (ATT_BLOCK - 1)
    kj = _iota((2 * ATT_BLOCK, 2 * ATT_BLOCK), 1)
    valid = (kj > qi) & (kj <= qi + ATT_BLOCK) & ((kj >= ATT_BLOCK) | (n > 0))
    return valid, _iota((2 * ATT_BLOCK, 1), 0) >= ATT_BLOCK


def _attn_fwd(q, kk, vlo, vhi, sinks, name="attn_fwd", carry=None):
    s = q.shape[0]
    nb = s // ATT_BLOCK
    scale = ATT_HEAD_DIM ** -0.5

    def body(sink_ref, q_ref, kc_ref, kp_ref, vloc_ref, vlop_ref, vhic_ref, vhip_ref, o_ref):
        n, h = pl.program_id(0), pl.program_id(1)
        valid, upper = _pair_masks(n)
        kw = jnp.concatenate([kp_ref[...], kc_ref[...]], axis=0)
        vcat = jnp.concatenate([vlop_ref[...], vloc_ref[...], vhip_ref[...], vhic_ref[...]], axis=0)
        for jp in range(ATT_GROUP // 2):
            q2 = _pair_rows(q_ref[:, jp * LANES:(jp + 1) * LANES])
            sink = jnp.where(upper, sink_ref[h * ATT_GROUP + 2 * jp + 1], sink_ref[h * ATT_GROUP + 2 * jp])
            probs, _ = _softmax_sink(_dot_nt(q2, kw) * scale, valid, sink)
            pcat = jnp.concatenate([probs[:ATT_BLOCK], probs[ATT_BLOCK:]], axis=1).astype(BF16)
            o_ref[:, jp * LANES:(jp + 1) * LANES] = _dot(pcat, vcat).astype(BF16)

    qs, kc, kp = _attn_specs("nh")
    res = _call(body, carry, name, (nb, ATT_KV_HEADS),
                [pl.BlockSpec(memory_space=pltpu.SMEM), qs, kc, kp, kc, kp, kc, kp], [qs],
                [jax.ShapeDtypeStruct((s, Q_WIDTH), BF16)], [], ("parallel", "parallel"),
                (sinks, q, kk, kk, vlo, vlo, vhi, vhi))
    return res[0] if carry is None else (res[0], res[1:])


def _attn_bwd(q, kk, vlo, vhi, sinks, do, name="attn_bwd", carry=None):
    s = q.shape[0]
    nb = s // ATT_BLOCK
    scale = ATT_HEAD_DIM ** -0.5

    def body(sink_ref, q_ref, kc_ref, kp_ref, vloc_ref, vlop_ref, vhic_ref, vhip_ref, do_ref,
             dq_ref, dkc_ref, dkp_ref, dvloc_ref, dvlop_ref, dvhic_ref, dvhip_ref, dsink_ref, dkk_acc, dv_acc):
        h, n = pl.program_id(0), pl.program_id(1)
        valid, upper = _pair_masks(n)
        kw = jnp.concatenate([kp_ref[...], kc_ref[...]], axis=0)
        vcat = jnp.concatenate([vlop_ref[...], vloc_ref[...], vhip_ref[...], vhic_ref[...]], axis=0)
        lane = _iota((ATT_BLOCK, LANES), 1)
        sub = _iota((ATT_GROUP, LANES), 0)
        dsink = jnp.zeros((ATT_GROUP, LANES), F32)
        for jp in range(ATT_GROUP // 2):
            q2 = _pair_rows(q_ref[:, jp * LANES:(jp + 1) * LANES])
            dop = do_ref[:, jp * LANES:(jp + 1) * LANES]
            sink = jnp.where(upper, sink_ref[h * ATT_GROUP + 2 * jp + 1], sink_ref[h * ATT_GROUP + 2 * jp])
            probs, psink = _softmax_sink(_dot_nt(q2, kw) * scale, valid, sink)
            pcat = jnp.concatenate([probs[:ATT_BLOCK], probs[ATT_BLOCK:]], axis=1).astype(BF16)
            dpc = _dot_nt(dop, vcat)
            dprobs = jnp.concatenate([dpc[:, :2 * ATT_BLOCK], dpc[:, 2 * ATT_BLOCK:]], axis=0)
            dv_part = _dot_tn(pcat, dop)
            delta = jnp.sum(probs * dprobs, axis=-1, keepdims=True)
            ds = (probs * (dprobs - delta) * scale).astype(BF16)
            sd = psink * delta
            dsink = (dsink + jnp.where(sub == 2 * jp, -jnp.sum(sd[:ATT_BLOCK]), 0.0)
                     + jnp.where(sub == 2 * jp + 1, -jnp.sum(sd[ATT_BLOCK:]), 0.0))
            dq2 = _dot(ds, kw)
            dq_ref[:, jp * LANES:(jp + 1) * LANES] = jnp.where(lane < 64, dq2[:ATT_BLOCK], dq2[ATT_BLOCK:])
            dkk_part = _dot_tn(ds, q2)
            if jp == 0:
                dkk_acc[...], dv_acc[...] = dkk_part, dv_part
            else:
                dkk_acc[...] += dkk_part
                dv_acc[...] += dv_part
        blk = ATT_BLOCK
        dkp_ref[...], dkc_ref[...] = dkk_acc[:blk], dkk_acc[blk:]
        dvlop_ref[...], dvloc_ref[...] = dv_acc[:blk], dv_acc[blk:2 * blk]
        dvhip_ref[...], dvhic_ref[...] = dv_acc[2 * blk:3 * blk], dv_acc[3 * blk:]

        @pl.when(n == 0)
        def _():
            dsink_ref[0] = dsink

        @pl.when(n > 0)
        def _():
            dsink_ref[0] += dsink

    qs, kc, kp = _attn_specs("hn")
    kw_shape = jax.ShapeDtypeStruct((s, ATT_KV_HEADS * LANES), F32)
    res = _call(body, carry, name, (ATT_KV_HEADS, nb),
                [pl.BlockSpec(memory_space=pltpu.SMEM), qs, kc, kp, kc, kp, kc, kp, qs],
                [qs] + [kc] * 6 + [pl.BlockSpec((1, ATT_GROUP, LANES), lambda h, n: (h, 0, 0))],
                [jax.ShapeDtypeStruct((s, Q_WIDTH), F32)] + [kw_shape] * 6
                + [jax.ShapeDtypeStruct((ATT_KV_HEADS, ATT_GROUP, LANES), F32)],
                [pltpu.VMEM((2 * ATT_BLOCK, LANES), F32), pltpu.VMEM((4 * ATT_BLOCK, LANES), F32)],
                ("parallel", "arbitrary"), (sinks, q, kk, kk, vlo, vlo, vhi, vhi, do))
    return res if carry is None else (res[:8], res[8:])


def _attn_prep_bwd(qkv, cos, sin, gq, gk, dq, dks, dvlos, dvhis):
    s, w = qkv.shape
    tr = ATT_BLOCK
    nb = s // tr

    def body(x_ref, cos_ref, sin_ref, gq_ref, gk_ref, dq_ref, dkc_ref, dkn_ref, dvloc_ref, dvlon_ref,
             dvhic_ref, dvhin_ref, dx_ref, dgq_ref, dgk_ref):
        n = pl.program_id(0)
        k = _lane_consts()
        cosv, sinv = cos_ref[...], sin_ref[...]
        nxt = jnp.where(n < nb - 1, 1.0, 0.0)
        lane = _iota((tr, LANES), 1)
        dgq = jnp.zeros((1, LANES), F32)
        dgk = jnp.zeros((1, LANES), F32)
        for j in range(Q_WIDTH // LANES):
            sl = slice(j * LANES, (j + 1) * LANES)
            _, xhat, rinv = _norm_rope(x_ref[:, sl], gq_ref[...], cosv, sinv, k)
            dx, dg = _norm_rope_bwd(dq_ref[:, sl], xhat, rinv, gq_ref[...], cosv, sinv, k)
            dx_ref[:, sl] = dx.astype(BF16)
            dgq = dgq + dg
        for i in range(KV_WIDTH // LANES):
            a, b = slice(2 * i * LANES, (2 * i + 1) * LANES), slice((2 * i + 1) * LANES, (2 * i + 2) * LANES)
            dr = (_dot_x2(dkc_ref[:, a] + nxt * dkn_ref[:, a], k["fold_lo"])
                  + _dot_x2(dkc_ref[:, b] + nxt * dkn_ref[:, b], k["fold_hi"]))
            sl = slice(Q_WIDTH + i * LANES, Q_WIDTH + (i + 1) * LANES)
            _, xhat, rinv = _norm_rope(x_ref[:, sl], gk_ref[...], cosv, sinv, k)
            dx, dg = _norm_rope_bwd(dr, xhat, rinv, gk_ref[...], cosv, sinv, k)
            dx_ref[:, sl] = dx.astype(BF16)
            dgk = dgk + dg
            ta = jnp.where(lane < 64, dvloc_ref[:, a] + nxt * dvlon_ref[:, a], dvhic_ref[:, a] + nxt * dvhin_ref[:, a])
            tb = jnp.where(lane < 64, dvloc_ref[:, b] + nxt * dvlon_ref[:, b], dvhic_ref[:, b] + nxt * dvhin_ref[:, b])
            sl = slice(Q_WIDTH + KV_WIDTH + i * LANES, Q_WIDTH + KV_WIDTH + (i + 1) * LANES)
            dx_ref[:, sl] = (_dot_x2(ta, k["fold_lo"]) + _dot_x2(tb, k["fold_hi"])).astype(BF16)

        @pl.when(n == 0)
        def _():
            dgq_ref[...] = dgq
            dgk_ref[...] = dgk

        @pl.when(n > 0)
        def _():
            dgq_ref[...] += dgq
            dgk_ref[...] += dgk

    row = lambda width: pl.BlockSpec((tr, width), lambda i: (i, 0))
    nxt_row = pl.BlockSpec((tr, ATT_KV_HEADS * LANES), lambda i: (jnp.minimum(i + 1, nb - 1), 0))
    vec = pl.BlockSpec((1, LANES), lambda i: (0, 0))
    kw = ATT_KV_HEADS * LANES
    return pl.pallas_call(
        body, name="attn_prep_bwd", grid=(nb,),
        in_specs=[row(w), row(LANES), row(LANES), vec, vec, row(Q_WIDTH),
                  row(kw), nxt_row, row(kw), nxt_row, row(kw), nxt_row],
        out_specs=[row(w), vec, vec],
        out_shape=[jax.ShapeDtypeStruct((s, w), BF16), jax.ShapeDtypeStruct((1, LANES), F32),
                   jax.ShapeDtypeStruct((1, LANES), F32)],
        compiler_params=_params("arbitrary"),
    )(qkv, cos, sin, gq, gk, dq, dks[0], dks[1], dvlos[0], dvlos[1], dvhis[0], dvhis[1])


CONV_HALO = 8
CONV_TC = 1024
XBC_OFF = SSM_D_INNER // CONV_TC
DT_OFF = SSM_D_INNER + SSM_CONV_DIM


def _conv_pre(ext, w_ref, b_ref, ts):
    pre = b_ref[...] + w_ref[SSM_CONV - 1:SSM_CONV, :] * ext[CONV_HALO:]
    for kk in range(SSM_CONV - 1):
        pre = pre + w_ref[kk:kk + 1, :] * pltpu.roll(ext, SSM_CONV - 1 - kk, 0)[CONV_HALO:]
    return pre


def _conv_specs(ts):
    tc = CONV_TC
    src = pl.BlockSpec((ts, tc), lambda j, i: (i, XBC_OFF + j))
    halo = pl.BlockSpec((CONV_HALO, tc), lambda j, i: (jnp.maximum(i * (ts // CONV_HALO) - 1, 0), XBC_OFF + j))
    blk = pl.BlockSpec((ts, tc), lambda j, i: (i, j))
    wspec = pl.BlockSpec((SSM_CONV, tc), lambda j, i: (0, j))
    bspec = pl.BlockSpec((1, tc), lambda j, i: (0, j))
    return src, halo, blk, wspec, bspec


def _conv_fwd(zx, w, b):
    s, c = zx.shape[0], SSM_CONV_DIM
    ts = _tile(s, 512, 8)

    def body(u_ref, halo_ref, w_ref, b_ref, o_ref):
        halo = jnp.where(pl.program_id(1) > 0, halo_ref[...], 0.0)
        pre = _conv_pre(jnp.concatenate([halo, u_ref[...]], axis=0), w_ref, b_ref, ts)
        o_ref[...] = pre * _sigmoid(pre)

    src, halo, blk, wspec, bspec = _conv_specs(ts)
    return pl.pallas_call(
        body, name="conv_fwd", grid=(c // CONV_TC, s // ts),
        in_specs=[src, halo, wspec, bspec], out_specs=blk, out_shape=jax.ShapeDtypeStruct((s, c), F32),
        compiler_params=_params("parallel", "parallel"),
    )(zx, zx, w, b)


def _conv_bwd_pre(zx, w, b, dxs, dbm, dcm):
    s, c = zx.shape[0], SSM_CONV_DIM
    ts = _tile(s, 512, 8)
    nx, nb = dxs.shape[1] // CONV_TC, dbm.shape[1] // CONV_TC

    def body(u_ref, halo_ref, w_ref, b_ref, dx_ref, dbm_ref, dcm_ref, dpre_ref, dw_ref, db_ref):
        j, i = pl.program_id(0), pl.program_id(1)
        halo = jnp.where(i > 0, halo_ref[...], 0.0)
        ext = jnp.concatenate([halo, u_ref[...]], axis=0)
        pre = _conv_pre(ext, w_ref, b_ref, ts)
        sg = _sigmoid(pre)
        da = jnp.where(j < nx, dx_ref[...], jnp.where(j < nx + nb, dbm_ref[...], dcm_ref[...]))
        dpre = da * sg * (1.0 + pre * (1.0 - sg))
        dpre_ref[...] = dpre
        rows = [jnp.sum(dpre * pltpu.roll(ext, SSM_CONV - 1 - kk, 0)[CONV_HALO:], axis=0, keepdims=True)
                for kk in range(SSM_CONV - 1)]
        rows.append(jnp.sum(dpre * ext[CONV_HALO:], axis=0, keepdims=True))
        dwp = jnp.concatenate(rows, axis=0)
        dbp = jnp.sum(dpre, axis=0, keepdims=True)

        @pl.when(i == 0)
        def _():
            dw_ref[...] = dwp
            db_ref[...] = dbp

        @pl.when(i > 0)
        def _():
            dw_ref[...] += dwp
            db_ref[...] += dbp

    src, halo, blk, wspec, bspec = _conv_specs(ts)

    def part(lo, n):
        return pl.BlockSpec((ts, CONV_TC), lambda j, i: (jnp.where((j >= lo) & (j < lo + n), i, 0),
                                                         jnp.clip(j - lo, 0, n - 1)))

    return pl.pallas_call(
        body, name="conv_bwd_pre", grid=(c // CONV_TC, s // ts),
        in_specs=[src, halo, wspec, bspec, part(0, nx), part(nx, nb), part(nx + nb, nb)],
        out_specs=[blk, wspec, bspec],
        out_shape=[jax.ShapeDtypeStruct((s, c), F32), jax.ShapeDtypeStruct((SSM_CONV, c), F32),
                   jax.ShapeDtypeStruct((1, c), F32)],
        compiler_params=_params("parallel", "arbitrary"),
    )(zx, zx, w, b, dxs, dbm, dcm)


def _conv_bwd_in(dpre, w, dzx):
    s, c = dpre.shape
    ts, tc = _tile(s, 512, 8), CONV_TC
    ns = s // ts

    def body(d_ref, halo_ref, w_ref, dzx_ref, o_ref):
        del dzx_ref
        halo = jnp.where(pl.program_id(1) < ns - 1, halo_ref[...], 0.0)
        ext = jnp.concatenate([d_ref[...], halo], axis=0)
        du = w_ref[SSM_CONV - 1:SSM_CONV, :] * ext[:ts]
        for kk in range(SSM_CONV - 1):
            du = du + w_ref[kk:kk + 1, :] * pltpu.roll(ext, ts + CONV_HALO - (SSM_CONV - 1 - kk), 0)[:ts]
        o_ref[...] = du.astype(BF16)

    blk = pl.BlockSpec((ts, tc), lambda j, i: (i, j))
    halo = pl.BlockSpec((CONV_HALO, tc), lambda j, i: (jnp.minimum((i + 1) * (ts // CONV_HALO), s // CONV_HALO - 1), j))
    return pl.pallas_call(
        body, name="conv_bwd_in", grid=(c // tc, ns),
        in_specs=[blk, halo, pl.BlockSpec((SSM_CONV, tc), lambda j, i: (0, j)), ANY],
        out_specs=pl.BlockSpec((ts, tc), lambda j, i: (i, XBC_OFF + j)),
        out_shape=jax.ShapeDtypeStruct(dzx.shape, BF16), input_output_aliases={3: 0},
        compiler_params=_params("parallel", "parallel"),
    )(dpre, dpre, w, dzx)


def _ssd_common(dt_ref, dtt_ref, bias_ref, biast_ref, alog_ref, alogt_ref):
    ln = SSM_CHUNK
    raw, rawt = dt_ref[0] + bias_ref[0], dtt_ref[0] + biast_ref[0]
    dt, dtt = _softplus(raw), _softplus(rawt)
    a, at = -jnp.exp(alog_ref[0]), -jnp.exp(alogt_ref[0])
    tri = jnp.where(_iota((ln, ln), 0) >= _iota((ln, ln), 1), 1.0, 0.0).astype(BF16)
    return dict(raw=raw, rawt=rawt, dt=dt, dtt=dtt, a=a, at=at, tri=tri,
                acum=_xdot(tri, dt * a), acumt=_dot_x_nt(dtt * at, tri))


def _ssd_specs(nc, rev):
    cidx = (lambda c: nc - 1 - c) if rev else (lambda c: c)
    ln = SSM_CHUNK
    xs = pl.BlockSpec((ln, SSM_GN), lambda g, c: (cidx(c), g))
    bs = pl.BlockSpec((ln, SSM_STATE), lambda g, c: (cidx(c), SSM_D_INNER // SSM_STATE + g))
    cs = pl.BlockSpec((ln, SSM_STATE), lambda g, c: (cidx(c), SSM_D_INNER // SSM_STATE + SSM_GROUPS + g))
    dt = pl.BlockSpec((1, ln, SSM_HPG), lambda g, c: (g, cidx(c), 0))
    dtt = pl.BlockSpec((1, SSM_HPG, ln), lambda g, c: (g, 0, cidx(c)))
    row = pl.BlockSpec((1, 1, SSM_HPG), lambda g, c: (g, 0, 0))
    col = pl.BlockSpec((1, SSM_HPG, 1), lambda g, c: (g, 0, 0))
    st = pl.BlockSpec((None, None, SSM_GN, SSM_STATE), lambda g, c: (cidx(c), g, 0, 0))
    return xs, bs, cs, dt, dtt, row, col, st


def _head_expand():
    return jnp.where((_iota((SSM_HPG, SSM_GN), 1) >> 6) == _iota((SSM_HPG, SSM_GN), 0), 1.0, 0.0).astype(BF16)


def _head_expand_t():
    return jnp.where((_iota((SSM_GN, SSM_HPG), 0) >> 6) == _iota((SSM_GN, SSM_HPG), 1), 1.0, 0.0).astype(BF16)


def _dot_x_tn(x, m):
    hi, mid, lo = _split3(x)
    return _dot_tn(hi, m) + _dot_tn(mid, m) + _dot_tn(lo, m)


def _ssd_fwd(xbc, dt_g, dt_gt, bias_r, bias_c, alog_r, alog_c, d_r):
    s = xbc.shape[0]
    ln = SSM_CHUNK
    nc = s // ln

    def body(x_ref, b_ref, c_ref, dt_ref, dtt_ref, bias_ref, biast_ref, alog_ref, alogt_ref, d_ref,
             y_ref, st_ref, state):
        @pl.when(pl.program_id(1) == 0)
        def _():
            state[...] = jnp.zeros_like(state)

        cm = _ssd_common(dt_ref, dtt_ref, bias_ref, biast_ref, alog_ref, alogt_ref)
        acum, acumt = cm["acum"], cm["acumt"]
        ex = _head_expand()
        acum_x = _dot_x(acum, ex)
        xv = x_ref[...]
        xdt = xv * _dot_x(cm["dt"], ex)
        xdtb = xdt.astype(BF16)
        bb, cb = b_ref[...].astype(BF16), c_ref[...].astype(BF16)
        cbm = _dot_nt(cb, bb)
        causal = _iota((ln, ln), 0) >= _iota((ln, ln), 1)
        s2 = state[...]
        st_ref[...] = s2
        for r in range(SSM_HPG):
            sl = slice(r * SSM_P, (r + 1) * SSM_P)
            decay = jnp.exp(jnp.where(causal, acum[:, r:r + 1] - acumt[r:r + 1, :], -jnp.inf))
            y_ref[:, sl] = _dot((cbm * decay).astype(BF16), xdtb[:, sl])
        y_ref[...] = (y_ref[...] + _dot_nt(cb, s2.astype(BF16)) * jnp.exp(acum_x) + _dot_x(d_ref[0], ex) * xv)
        last_x = acum_x[ln - 1:ln, :]
        elast = jnp.exp(_xdot(_head_expand_t(), acumt[:, ln - 1:ln]))
        state[...] = s2 * elast + _dot_tn((xdt * jnp.exp(last_x - acum_x)).astype(BF16), bb)

    xs, bs, cs, dts, dtts, row, col, st = _ssd_specs(nc, False)
    return pl.pallas_call(
        body, name="ssd_fwd", grid=(SSM_GROUPS, nc),
        in_specs=[xs, bs, cs, dts, dtts, row, col, row, col, row],
        out_specs=[xs, st],
        out_shape=[jax.ShapeDtypeStruct((s, SSM_D_INNER), F32),
                   jax.ShapeDtypeStruct((nc, SSM_GROUPS, SSM_GN, SSM_STATE), F32)],
        scratch_shapes=[pltpu.VMEM((SSM_GN, SSM_STATE), F32)],
        compiler_params=_params("parallel", "arbitrary"),
    )(xbc, xbc, xbc, dt_g, dt_gt, bias_r, bias_c, alog_r, alog_c, d_r)


def _ssd_bwd(xbc, dt_g, dt_gt, bias_r, bias_c, alog_r, alog_c, d_r, states, dy):
    s = xbc.shape[0]
    ln = SSM_CHUNK
    nc = s // ln

    def body(x_ref, b_ref, c_ref, dt_ref, dtt_ref, bias_ref, biast_ref, alog_ref, alogt_ref, d_ref,
             st_ref, dy_ref, dx_ref, db_ref, dc_ref, ddt_ref, ddtt_ref, dbias_ref, dbiast_ref,
             dalog_ref, dalogt_ref, dd_ref, dstate):
        step = pl.program_id(1)

        @pl.when(step == 0)
        def _():
            dstate[...] = jnp.zeros_like(dstate)

        cm = _ssd_common(dt_ref, dtt_ref, bias_ref, biast_ref, alog_ref, alogt_ref)
        dt, acum, acumt = cm["dt"], cm["acum"], cm["acumt"]
        ex, ext = _head_expand(), _head_expand_t()
        dt_x, acum_x = _dot_x(dt, ex), _dot_x(acum, ex)
        eac_x, to_end_x = jnp.exp(acum_x), jnp.exp(acum_x[ln - 1:ln, :] - acum_x)
        xv, dyv = x_ref[...], dy_ref[...]
        xdt = xv * dt_x
        xdtb, dyb = xdt.astype(BF16), dyv.astype(BF16)
        dyeb = (dyv * eac_x).astype(BF16)
        bb, cb = b_ref[...].astype(BF16), c_ref[...].astype(BF16)
        cbm = _dot_nt(cb, bb)
        s2, ds2 = st_ref[...], dstate[...]
        s2b, ds2b = s2.astype(BF16), ds2.astype(BF16)
        dxdt_state = _dot_nt(bb, ds2b) * to_end_x
        yoff = _dot_nt(cb, s2b) * eac_x
        dc_acc = _dot(dyeb, s2b)
        db_acc = _dot((xdt * to_end_x).astype(BF16), ds2b)
        f_rows = _dot_x2_nt(xdt * dxdt_state, ex)
        elast = jnp.exp(acum[ln - 1:ln, :])
        dlast = (jnp.sum(f_rows, axis=0, keepdims=True)
                 + elast * jnp.sum(_dot_x_tn(ds2 * s2, ext), axis=0, keepdims=True))
        is_last = _iota((ln, 1), 0) == ln - 1
        dac_rows = _dot_x2_nt(dyv * yoff, ex) - f_rows + jnp.where(is_last, dlast, 0.0)
        dstate[...] = ds2 * jnp.exp(_xdot(ext, acumt[:, ln - 1:ln])) + _dot_tn(dyeb, cb)
        causal = _iota((ln, ln), 0) >= _iota((ln, ln), 1)
        lane8 = _iota((ln, SSM_HPG), 1)
        sub8 = _iota((SSM_HPG, ln), 0)
        dcb = jnp.zeros((ln, ln), F32)
        dac_cols = jnp.zeros((SSM_HPG, ln), F32)
        for r in range(SSM_HPG):
            sl = slice(r * SSM_P, (r + 1) * SSM_P)
            decay = jnp.exp(jnp.where(causal, acum[:, r:r + 1] - acumt[r:r + 1, :], -jnp.inf))
            dx_ref[:, sl] = _dot_tn((cbm * decay).astype(BF16), dyb[:, sl])
            dcb_r = _dot_nt(dyb[:, sl], xdtb[:, sl]) * decay
            dcb = dcb + dcb_r
            e = dcb_r * cbm
            dac_rows = dac_rows + jnp.where(lane8 == r, jnp.sum(e, axis=-1, keepdims=True), 0.0)
            dac_cols = dac_cols + jnp.where(sub8 == r, jnp.sum(e, axis=0, keepdims=True), 0.0)
        dxdt = dx_ref[...] + dxdt_state
        ddt_all = _dot_x2_nt(dxdt * xv, ex)
        dd_all = jnp.sum(_dot_x2_nt(dyv * xv, ex), axis=0, keepdims=True)
        dx_ref[...] = dxdt * dt_x + _dot_x(d_ref[0], ex) * dyv
        dcbb = dcb.astype(BF16)
        dc_ref[...] = dc_acc + _dot(dcbb, bb)
        db_ref[...] = db_acc + _dot_tn(dcbb, cb)
        triu = jnp.where(_iota((ln, ln), 0) <= _iota((ln, ln), 1), 1.0, 0.0).astype(BF16)
        g_rows = _xdot(triu, dac_rows)
        g_cols = _dot_x(dac_cols, cm["tri"])
        d_rows = (ddt_all + g_rows * cm["a"]) * _sigmoid(cm["raw"])
        d_cols = -(g_cols * cm["at"]) * _sigmoid(cm["rawt"])
        ddt_ref[0] = d_rows
        ddtt_ref[0] = d_cols
        parts = (jnp.sum(d_rows, axis=0, keepdims=True), jnp.sum(d_cols, axis=1, keepdims=True),
                 jnp.sum(g_rows * dt, axis=0, keepdims=True) * cm["a"],
                 -jnp.sum(g_cols * cm["dtt"], axis=1, keepdims=True) * cm["at"], dd_all)
        outs = (dbias_ref, dbiast_ref, dalog_ref, dalogt_ref, dd_ref)

        @pl.when(step == 0)
        def _():
            for o_ref, p in zip(outs, parts):
                o_ref[0] = p

        @pl.when(step > 0)
        def _():
            for o_ref, p in zip(outs, parts):
                o_ref[0] += p

    xs, bs, cs, dts, dtts, row, col, st = _ssd_specs(nc, True)
    grp = pl.BlockSpec((ln, SSM_STATE), lambda g, c: (nc - 1 - c, g))
    rows = jax.ShapeDtypeStruct((SSM_GROUPS, 1, SSM_HPG), F32)
    cols = jax.ShapeDtypeStruct((SSM_GROUPS, SSM_HPG, 1), F32)
    return pl.pallas_call(
        body, name="ssd_bwd", grid=(SSM_GROUPS, nc),
        in_specs=[xs, bs, cs, dts, dtts, row, col, row, col, row, st, xs],
        out_specs=[xs, grp, grp, dts, dtts, row, col, row, col, row],
        out_shape=[jax.ShapeDtypeStruct((s, SSM_D_INNER), F32),
                   jax.ShapeDtypeStruct((s, SSM_GROUPS * SSM_STATE), F32),
                   jax.ShapeDtypeStruct((s, SSM_GROUPS * SSM_STATE), F32),
                   jax.ShapeDtypeStruct((SSM_GROUPS, s, SSM_HPG), F32),
                   jax.ShapeDtypeStruct((SSM_GROUPS, SSM_HPG, s), F32), rows, cols, rows, cols, rows],
        scratch_shapes=[pltpu.VMEM((SSM_GN, SSM_STATE), F32)],
        compiler_params=_params("parallel", "arbitrary"),
    )(xbc, xbc, xbc, dt_g, dt_gt, bias_r, bias_c, alog_r, alog_c, d_r, states, dy)


GN_PER_BLOCK = 2
GN_WIDTH = GN_PER_BLOCK * SSM_GN


def _gate_norm_fwd(y, zx, g):
    s = y.shape[0]
    ts = _tile(s, 512, 8)

    def body(y_ref, z_ref, g_ref, o_ref):
        for k in range(GN_PER_BLOCK):
            sl = slice(k * SSM_GN, (k + 1) * SSM_GN)
            zv = z_ref[:, sl]
            yg = y_ref[:, sl] * (zv * _sigmoid(zv))
            r = lax.rsqrt(jnp.mean(yg * yg, axis=-1, keepdims=True) + EPS)
            o_ref[:, sl] = (yg * r * g_ref[:, sl]).astype(BF16)

    blk = pl.BlockSpec((ts, GN_WIDTH), lambda j, i: (i, j))
    vec = pl.BlockSpec((1, GN_WIDTH), lambda j, i: (0, j))
    return pl.pallas_call(
        body, name="gate_norm_fwd", grid=(SSM_D_INNER // GN_WIDTH, s // ts), in_specs=[blk, blk, vec],
        out_specs=blk,
        out_shape=jax.ShapeDtypeStruct((s, SSM_D_INNER), BF16), compiler_params=_params("parallel", "parallel"),
    )(y, zx, g)


def _gate_norm_bwd(y, zx, g, dout):
    s = y.shape[0]
    ts = _tile(s, 512, 8)

    def body(y_ref, z_ref, g_ref, do_ref, dy_ref, dz_ref, dg_ref):
        parts = []
        for k in range(GN_PER_BLOCK):
            sl = slice(k * SSM_GN, (k + 1) * SSM_GN)
            yv, zv, dov = y_ref[:, sl], z_ref[:, sl], do_ref[:, sl].astype(F32)
            sg = _sigmoid(zv)
            silu = zv * sg
            yg = yv * silu
            r = lax.rsqrt(jnp.mean(yg * yg, axis=-1, keepdims=True) + EPS)
            ygn = yg * r
            parts.append(jnp.sum(dov * ygn, axis=0, keepdims=True))
            dn = dov * g_ref[:, sl]
            dyg = r * (dn - ygn * jnp.mean(dn * ygn, axis=-1, keepdims=True))
            dy_ref[:, sl] = dyg * silu
            dz_ref[:, sl] = (dyg * yv * sg * (1.0 + zv * (1.0 - sg))).astype(BF16)
        part = jnp.concatenate(parts, axis=1)

        @pl.when(pl.program_id(1) == 0)
        def _():
            dg_ref[...] = part

        @pl.when(pl.program_id(1) > 0)
        def _():
            dg_ref[...] += part

    blk = pl.BlockSpec((ts, GN_WIDTH), lambda j, i: (i, j))
    vec = pl.BlockSpec((1, GN_WIDTH), lambda j, i: (0, j))
    return pl.pallas_call(
        body, name="gate_norm_bwd", grid=(SSM_D_INNER // GN_WIDTH, s // ts), in_specs=[blk, blk, vec, blk],
        out_specs=[blk, blk, vec],
        out_shape=[jax.ShapeDtypeStruct((s, SSM_D_INNER), F32), jax.ShapeDtypeStruct((s, SSM_IN_PAD), BF16),
                   jax.ShapeDtypeStruct((1, SSM_D_INNER), F32)],
        compiler_params=_params("parallel", "arbitrary"),
    )(y, zx, g, dout)


def _rope_tables(positions):
    inv_freq = ROPE_THETA ** (-jnp.arange(0, ATT_HEAD_DIM, 2, dtype=F32) / ATT_HEAD_DIM)
    ang = positions.astype(F32)[:, None] * inv_freq
    return jnp.tile(jnp.cos(ang), (1, 4)), jnp.tile(jnp.sin(ang), (1, 4))


def _group_views(v):
    return v.reshape(SSM_GROUPS, 1, SSM_HPG), v.reshape(SSM_GROUPS, SSM_HPG, 1)


def _ffn_fwd(run, x, norm_g, wg, wu, wd, tag):
    h = _rms_fwd(x, norm_g, f"ffn_norm_{tag}")
    g = run(f"ffn_gate_{tag}", _mm, h, wg, "nn", b_cols=True, out_dtype=BF16)

    def act(uv, gv):
        gv = gv.astype(F32)
        return uv, gv * _sigmoid(gv) * uv

    u, a = run(f"ffn_up_{tag}", _mm, h, wu, "nn", b_cols=True, fuse=(act, [g], [BF16, BF16]))
    return run(f"ffn_down_{tag}", _mm, a, wd, "nn", add=x), (h, g, u, a)


def _ffn_bwd(run, mats, x, norm_g, wg, wu, wd, saved, dout, dout_b, tag):
    h, g, u, a = saved

    def act_bwd(da, gv, uv):
        gv, uv = gv.astype(F32), uv.astype(F32)
        sg = _sigmoid(gv)
        return da * uv * sg * (1.0 + gv * (1.0 - sg)), da * (gv * sg)

    dg, du = run(f"ffn_down_dx_{tag}", _mm, dout_b, wd, "nt", fuse=(act_bwd, [g, u], [BF16, BF16]))
    dwd = run(f"ffn_down_dw_{tag}", _mm, a, dout_b, "tn", out_dtype=BF16)
    mats[("ffn_w_down", tag)] = dwd.reshape(N_SHARDS, dwd.shape[0] // N_SHARDS, dwd.shape[1])
    mats[("ffn_w_gate", tag)] = run(f"ffn_gate_dw_{tag}", _mm, h, dg, "tn", out_dtype=BF16, out_cols=True)
    mats[("ffn_w_up", tag)] = run(f"ffn_up_dw_{tag}", _mm, h, du, "tn", out_dtype=BF16, out_cols=True)
    dh = run(f"ffn_gate_dx_{tag}", _mm, dg, wg, "nt", b_cols=True)
    dh = run(f"ffn_up_dx_{tag}", _mm, du, wu, "nt", add=dh, b_cols=True)
    return _rms_bwd(x, norm_g, dh, dout, f"ffn_norm_bwd_{tag}")


class _Hook:
    def __init__(self, make, done):
        self.make, self.done = make, done


class _SemView:
    def __init__(self, sems, off):
        self.sems, self.off, self.at = sems, off, self

    def __getitem__(self, k):
        return self.sems.at[self.off + k]


def _both(h1, h2):
    split = {}

    def make():
        a, b = h1.make(), h2.make()
        na_in, na_out, na_sems = len(a["arrays"]), len(a["out_shapes"]), a["n_sems"]
        split["n"] = na_out

        def build(cin, cout, send_sems, recv_sems):
            return (a["build"](cin[:na_in], cout[:na_out], send_sems, recv_sems)
                    + b["build"](cin[na_in:], cout[na_out:], _SemView(send_sems, na_sems), _SemView(recv_sems, na_sems)))

        aliases = dict(a.get("aliases", {}))
        aliases.update({na_in + i: na_out + o for i, o in b.get("aliases", {}).items()})
        return dict(build=build, arrays=list(a["arrays"]) + list(b["arrays"]),
                    out_shapes=list(a["out_shapes"]) + list(b["out_shapes"]), n_sems=na_sems + b["n_sems"],
                    aliases=aliases)

    def done(res):
        h1.done(res[:split["n"]])
        h2.done(res[split["n"]:])

    return _Hook(make, done)


def _local_step(x, positions, target, w, hooks=None, mats=None):
    hooks = {} if hooks is None else hooks
    mats = {} if mats is None else mats

    def run(name, fn, *args, **kw):
        hook = hooks.get(name)
        if hook is None:
            return fn(*args, name=name, **kw)
        res, carried = fn(*args, name=name, carry=hook.make(), **kw)
        hook.done(carried)
        return res

    cos, sin = _rope_tables(positions)
    row = lambda v: v.reshape(1, -1)
    gq, gk = jnp.tile(row(w["attn_q_norm"]), (1, 2)), jnp.tile(row(w["attn_k_norm"]), (1, 2))
    sinks = w["attn_sinks"].reshape(-1)
    s = x.shape[0]
    row_stack = lambda g: g.reshape(N_SHARDS, g.shape[0] // N_SHARDS, g.shape[1])

    h0 = _rms_fwd(x, row(w["mixer_norm"][0]), "mixer_norm_0")
    qkv = run("attn_qkv", _mm, h0, w["attn_w_qkv"], "nn", b_cols=True)
    q, kk, vlo, vhi = _attn_prep(qkv, cos, sin, gq, gk)
    o = run("attn_fwd", _attn_fwd, q, kk, vlo, vhi, sinks)
    x1 = run("attn_out", _mm, o, w["attn_w_o"], "nn", add=x)
    ffn_w = lambda l: (row(w["ffn_norm"][l]), w["ffn_w_gate"][l], w["ffn_w_up"][l], w["ffn_w_down"][l])
    x2, ffn0 = _ffn_fwd(run, x1, *ffn_w(0), 0)

    h2 = _rms_fwd(x2, row(w["mixer_norm"][1]), "mixer_norm_1")
    zx = run("ssm_in", _mm, h2, w["ssm_w_in"], "nn")
    dt_g = zx[:, DT_OFF:DT_OFF + SSM_HEADS].reshape(s, SSM_GROUPS, SSM_HPG).transpose(1, 0, 2)
    dt_gt = dt_g.transpose(0, 2, 1)
    bias_r, bias_c = _group_views(w["ssm_dt_bias"].reshape(-1))
    alog_r, alog_c = _group_views(w["ssm_a_log"].reshape(-1))
    d_r, _ = _group_views(w["ssm_d"].reshape(-1))
    xbc = _conv_fwd(zx, w["ssm_conv_w"], row(w["ssm_conv_b"]))
    ssd_args = (xbc, dt_g, dt_gt, bias_r, bias_c, alog_r, alog_c, d_r)
    y, states = _ssd_fwd(*ssd_args)
    yn = _gate_norm_fwd(y, zx, row(w["ssm_norm"]))
    x3 = run("ssm_out", _mm, yn, w["ssm_w_out"], "nn", add=x2)
    x4, ffn1 = _ffn_fwd(run, x3, *ffn_w(1), 1)

    loss_row, dx4, dx4b = _loss_fwd_bwd(x4, target)

    dx3, dx3b, dfn1 = _ffn_bwd(run, mats, x3, *ffn_w(1), ffn1, dx4, dx4b, 1)
    dyn = run("ssm_out_dx", _mm, dx3b, w["ssm_w_out"], "nt")
    mats[("ssm_w_out", 0)] = row_stack(run("ssm_out_dw", _mm, yn, dx3b, "tn", out_dtype=BF16))
    dy, dzx, dssm_norm = _gate_norm_bwd(y, zx, row(w["ssm_norm"]), dyn)
    dxs, db, dc, ddt_g, ddt_gt, dbias, dbias_t, dalog, dalog_t, dd = _ssd_bwd(*ssd_args, states, dy)
    ddt_g = ddt_g + ddt_gt.transpose(0, 2, 1)
    dpre, dconv_w, dconv_b = _conv_bwd_pre(zx, w["ssm_conv_w"], row(w["ssm_conv_b"]), dxs, db, dc)
    dzx = _conv_bwd_in(dpre, w["ssm_conv_w"], dzx)
    ddt_pad = jnp.pad(ddt_g.transpose(1, 0, 2).reshape(s, SSM_HEADS), ((0, 0), (0, SSM_IN_PAD - SSM_IN)))
    dzx = lax.dynamic_update_slice(dzx, ddt_pad.astype(BF16), (0, DT_OFF))
    dw_in = run("ssm_in_dw", _mm, h2, dzx, "tn", out_dtype=BF16)
    in_shard = SSM_IN // N_SHARDS
    mats[("ssm_w_in", 0)] = jnp.stack([dw_in[:, i * in_shard:(i + 1) * in_shard] for i in range(N_SHARDS)])
    dh2 = run("ssm_in_dx", _mm, dzx, w["ssm_w_in"], "nt")
    dx2, dx2b, dmn1 = _rms_bwd(x2, row(w["mixer_norm"][1]), dh2, dx3, "mixer_norm_bwd_1")

    dx1, dx1b, dfn0 = _ffn_bwd(run, mats, x1, *ffn_w(0), ffn0, dx2, dx2b, 0)
    do = run("attn_out_dx", _mm, dx1b, w["attn_w_o"], "nt", out_dtype=BF16)
    mats[("attn_w_o", 0)] = row_stack(run("attn_out_dw", _mm, o, dx1b, "tn", out_dtype=BF16))
    dq, dkc, dkp, dvloc, dvlop, dvhic, dvhip, dsink = run("attn_bwd", _attn_bwd, q, kk, vlo, vhi, sinks, do)
    dqkv, dgq, dgk = _attn_prep_bwd(qkv, cos, sin, gq, gk, dq, (dkc, dkp), (dvloc, dvlop), (dvhic, dvhip))
    mats[("attn_w_qkv", 0)] = run("attn_qkv_dw", _mm, h0, dqkv, "tn", out_dtype=BF16, out_cols=True)
    dh0 = run("attn_qkv_dx", _mm, dqkv, w["attn_w_qkv"], "nt", b_cols=True)
    dx0, _, dmn0 = _rms_bwd(x, row(w["mixer_norm"][0]), dh0, dx1, "mixer_norm_bwd_0")

    fold = lambda v: v[0, :ATT_HEAD_DIM] + v[0, ATT_HEAD_DIM:]
    grads = {
        "mixer_norm": jnp.concatenate([dmn0, dmn1], axis=0),
        "ffn_norm": jnp.concatenate([dfn0, dfn1], axis=0),
        "attn_q_norm": fold(dgq), "attn_k_norm": fold(dgk),
        "attn_sinks": dsink[:, :, 0].reshape(-1),
        "ssm_conv_w": dconv_w, "ssm_conv_b": dconv_b.reshape(-1),
        "ssm_dt_bias": dbias.reshape(-1) + dbias_t.reshape(-1),
        "ssm_a_log": dalog.reshape(-1) + dalog_t.reshape(-1), "ssm_d": dd.reshape(-1),
        "ssm_norm": dssm_norm.reshape(-1),
    }
    return loss_row[0, 0], dx0, grads


OTHER_CHIPS = ((1, 0), (0, 1), (1, 1))


def _position():
    return lax.axis_index("x"), lax.axis_index("y"), lax.axis_index("c")


def _gather_shards(weights, layers):
    n_in, n_mat = len(weights), len(layers)

    def body(*refs):
        p, out = refs[:n_in], refs[n_in:n_in + n_mat]
        send_sems, recv_sems = refs[n_in + n_mat:]
        x, y, c = _position()
        me, sibling = (x, y, c), (x, y, 1 - c)
        chips = [(x ^ fx, y ^ fy) for fx, fy in OTHER_CHIPS]

        def rows(e, px, py, pc):
            half = out[e].shape[1] // 2
            return out[e].at[2 * px + py, pl.ds(pc * half, half), :]

        def copy(k, e, block, to, src=None):
            return pltpu.make_async_remote_copy(
                src_ref=rows(e, *block) if src is None else src, dst_ref=rows(e, *block),
                send_sem=send_sems.at[k * n_mat + e], recv_sem=recv_sems.at[k * n_mat + e],
                device_id=to, device_id_type=MESH)

        def own(e):
            i, l = layers[e]
            return pltpu.make_async_remote_copy(
                src_ref=p[i].at[l], dst_ref=out[e].at[2 * x + y], send_sem=send_sems.at[6 * n_mat + e],
                recv_sem=recv_sems.at[6 * n_mat + e], device_id=sibling, device_id_type=MESH)

        first, passed = [], []
        for e, (i, l) in enumerate(layers):
            half = out[e].shape[1] // 2
            first.append([copy(j, e, me, (*chip, c), src=p[i].at[l, pl.ds(c * half, half), :])
                          for j, chip in enumerate(chips)])
            for cp in first[-1]:
                cp.start()
        for e in range(n_mat):
            own(e).start()
        for e in range(n_mat):
            passed.append([copy(3 + j, e, (*chip, c), sibling) for j, chip in enumerate(chips)])
            for j, chip in enumerate(chips):
                copy(j, e, (*chip, c), me).wait_recv()
                passed[e][j].start()
        for e in range(n_mat):
            own(e).wait()
            for j, chip in enumerate(chips):
                copy(3 + j, e, (*chip, 1 - c), me).wait_recv()
        for e in range(n_mat):
            for cp in first[e] + passed[e]:
                cp.wait_send()

    return pl.pallas_call(
        body, name="gather_weights", in_specs=[ANY] * n_in, out_specs=[ANY] * n_mat,
        out_shape=[jax.ShapeDtypeStruct((N_SHARDS,) + weights[i].shape[1:], weights[i].dtype) for i, _ in layers],
        scratch_shapes=[_sems(7 * n_mat), _sems(7 * n_mat)],
    )(*weights)


def _all_gather8(block, name):
    m_per, n = block.shape

    def body(x_ref, out_ref, send_sems, recv_sems, local_sem):
        x, y, c = _position()
        me, sibling = (x, y, c), (x, y, 1 - c)
        chips = [(x ^ fx, y ^ fy) for fx, fy in OTHER_CHIPS]

        def rows(px, py, pc):
            return out_ref.at[pl.ds((4 * px + 2 * py + pc) * m_per, m_per), :]

        def copy(k, blk, to, src=None):
            return pltpu.make_async_remote_copy(
                src_ref=rows(*blk) if src is None else src, dst_ref=rows(*blk),
                send_sem=send_sems.at[k], recv_sem=recv_sems.at[k], device_id=to, device_id_type=MESH)

        mine = pltpu.make_async_copy(x_ref, rows(*me), local_sem)
        mine.start()
        first = [copy(0, me, sibling, src=x_ref)]
        first += [copy(1 + j, me, (*chip, c), src=x_ref) for j, chip in enumerate(chips)]
        for cp in first:
            cp.start()
        passed = [copy(4 + j, (*chip, c), sibling) for j, chip in enumerate(chips)]
        for j, chip in enumerate(chips):
            copy(1 + j, (*chip, c), me).wait_recv()
            passed[j].start()
        copy(0, sibling, me).wait_recv()
        for j, chip in enumerate(chips):
            copy(4 + j, (*chip, 1 - c), me).wait_recv()
        for cp in first + passed:
            cp.wait_send()
        mine.wait()

    return pl.pallas_call(
        body, name=name, out_shape=jax.ShapeDtypeStruct((N_DEV * m_per, n), block.dtype),
        in_specs=[pl.BlockSpec(memory_space=pltpu.VMEM)], out_specs=pl.BlockSpec(memory_space=pltpu.VMEM),
        scratch_shapes=[_sems(7), _sems(7), pltpu.SemaphoreType.DMA],
    )(block)


def _exchange(carry, name):
    n_in, n_out = len(carry["arrays"]), len(carry["out_shapes"])

    def body(*refs):
        copies = carry["build"](refs[:n_in], refs[n_in:n_in + n_out], refs[-2], refs[-1])
        for cp in copies:
            cp.start()
        for cp in copies:
            cp.wait()

    return pl.pallas_call(
        body, name=name, in_specs=[ANY] * n_in, out_specs=[ANY] * n_out, out_shape=list(carry["out_shapes"]),
        input_output_aliases=dict(carry.get("aliases", {})),
        scratch_shapes=[_sems(carry["n_sems"]), _sems(carry["n_sems"])],
    )(*carry["arrays"])


def _remote(src, dst, send_sems, recv_sems, k, to):
    return pltpu.make_async_remote_copy(src_ref=src, dst_ref=dst, send_sem=send_sems.at[k], recv_sem=recv_sems.at[k],
                                        device_id=to, device_id_type=MESH)


def _gather_over_ici(blocks, layers):
    def build(p, out, send_sems, recv_sems):
        x, y, c = _position()
        copies = []
        for e, l in enumerate(layers):
            half = out[e].shape[1] // 2
            rows = pl.ds(c * half, half)
            for j, (fx, fy) in enumerate(OTHER_CHIPS):
                copies.append(_remote(p[e].at[l, rows, :], out[e].at[2 * x + y, rows, :], send_sems, recv_sems,
                                      3 * e + j, (x ^ fx, y ^ fy, c)))
        return copies

    shapes = [jax.ShapeDtypeStruct((N_SHARDS,) + b.shape[1:], b.dtype) for b in blocks]
    return dict(build=build, arrays=list(blocks), out_shapes=shapes, n_sems=3 * len(layers))


def _gather_over_d2d(stacks, blocks, layers):
    n = len(stacks)

    def build(refs, out, send_sems, recv_sems):
        p = refs[n:]
        x, y, c = _position()
        sibling = (x, y, 1 - c)
        copies = []
        for e, l in enumerate(layers):
            half = out[e].shape[1] // 2
            for j, (fx, fy) in enumerate(OTHER_CHIPS):
                rows = out[e].at[2 * (x ^ fx) + (y ^ fy), pl.ds(c * half, half), :]
                copies.append(_remote(rows, rows, send_sems, recv_sems, 4 * e + j, sibling))
            copies.append(_remote(p[e].at[l], out[e].at[2 * x + y], send_sems, recv_sems, 4 * e + 3, sibling))
        return copies

    shapes = [jax.ShapeDtypeStruct(s.shape, s.dtype) for s in stacks]
    return dict(build=build, arrays=list(stacks) + list(blocks), out_shapes=shapes, n_sems=4 * n,
                aliases={i: i for i in range(n)})


def _grads_to_sibling(stacks):
    def build(g, out, send_sems, recv_sems):
        x, y, c = _position()
        copies = []
        for e in range(len(stacks)):
            half = g[e].shape[1] // 2
            copies.append(_remote(g[e].at[:, pl.ds((1 - c) * half, half), :], out[e], send_sems, recv_sems, e,
                                  (x, y, 1 - c)))
        return copies

    shapes = [jax.ShapeDtypeStruct((N_SHARDS, g.shape[1] // 2, g.shape[2]), g.dtype) for g in stacks]
    return dict(build=build, arrays=list(stacks), out_shapes=shapes, n_sems=len(stacks))


def _grads_to_owners(partials):
    def build(p, out, send_sems, recv_sems):
        x, y, c = _position()
        copies = []
        for e in range(len(partials)):
            for k, (fx, fy) in enumerate(OTHER_CHIPS):
                px, py = x ^ fx, y ^ fy
                copies.append(_remote(p[e].at[2 * px + py], out[e].at[k], send_sems, recv_sems, 3 * e + k,
                                      (px, py, c)))
        return copies

    shapes = [jax.ShapeDtypeStruct((len(OTHER_CHIPS),) + p.shape[1:], p.dtype) for p in partials]
    return dict(build=build, arrays=list(partials), out_shapes=shapes, n_sems=3 * len(partials))


def _share_halves(grads, layers):
    def build(_, out, send_sems, recv_sems):
        x, y, c = _position()
        copies = []
        for e, (i, l) in enumerate(layers):
            half = out[i].shape[1] // 2
            rows = out[i].at[l, pl.ds(c * half, half), :]
            copies.append(_remote(rows, rows, send_sems, recv_sems, e, (x, y, 1 - c)))
        return copies

    return dict(build=build, arrays=list(grads), out_shapes=[jax.ShapeDtypeStruct(g.shape, g.dtype) for g in grads],
                n_sems=len(layers), aliases={i: i for i in range(len(grads))})


ADD_BLOCK_ELEMS = 1 << 19


def _add_rows(half, cols):
    return _tile(half, max(16, ADD_BLOCK_ELEMS // cols // 16 * 16), 16)


def _add_pair(stack, recv, c_idx, name):
    _, half, cols = recv.shape
    tr = _add_rows(half, cols)
    nt = half // tr

    def body(c_ref, a_ref, b_ref, o_ref):
        o_ref[...] = (a_ref[...].astype(F32) + b_ref[...].astype(F32)).astype(o_ref.dtype)

    blk = pl.BlockSpec((None, tr, cols), lambda s, i, c_ref: (s, i, 0))
    return pl.pallas_call(
        body, name=name,
        grid_spec=pltpu.PrefetchScalarGridSpec(
            num_scalar_prefetch=1, grid=(N_SHARDS, nt),
            in_specs=[pl.BlockSpec((None, tr, cols), lambda s, i, c_ref: (s, c_ref[0] * nt + i, 0)), blk],
            out_specs=blk),
        out_shape=jax.ShapeDtypeStruct(recv.shape, recv.dtype),
        compiler_params=_params("parallel", "parallel"),
    )(c_idx, stack, recv)


def _add_owned(partial, recv, sc_idx, layer, shape, into, name):
    _, half, cols = partial.shape
    tr = _add_rows(half, cols)
    nt = half // tr

    def body(sc_ref, a_ref, r0_ref, r1_ref, r2_ref, *rest):
        o_ref = rest[-1]
        o_ref[...] = (((a_ref[...].astype(F32) + r0_ref[...].astype(F32)) + r1_ref[...].astype(F32))
                      + r2_ref[...].astype(F32))

    slot = lambda k: pl.BlockSpec((None, tr, cols), lambda i, sc_ref: (k, i, 0))
    has_into = into is not None
    return pl.pallas_call(
        body, name=name,
        grid_spec=pltpu.PrefetchScalarGridSpec(
            num_scalar_prefetch=1, grid=(nt,),
            in_specs=[pl.BlockSpec((None, tr, cols), lambda i, sc_ref: (sc_ref[0], i, 0)), slot(0), slot(1), slot(2)]
            + ([ANY] if has_into else []),
            out_specs=pl.BlockSpec((None, tr, cols), lambda i, sc_ref: (layer, sc_ref[1] * nt + i, 0))),
        out_shape=jax.ShapeDtypeStruct(shape, F32),
        input_output_aliases={5: 0} if has_into else {},
        compiler_params=_params("parallel"),
    )(*((sc_idx, partial, recv, recv, recv) + ((into,) if has_into else ())))


def _sum8(gathered):
    m = gathered.shape[0] // N_DEV

    def body(g_ref, o_ref):
        total = g_ref[0:m, :]
        for d in range(1, N_DEV):
            total = total + g_ref[d * m:(d + 1) * m, :]
        o_ref[...] = total

    return pl.pallas_call(
        body, name="small_grads_sum", out_shape=jax.ShapeDtypeStruct((m, LANES), F32),
        in_specs=[pl.BlockSpec(memory_space=pltpu.VMEM)], out_specs=pl.BlockSpec(memory_space=pltpu.VMEM),
    )(gathered)


ADAMW_BLOCK_ELEMS = 1 << 18


def _adamw(w, g, m, v, name):
    l, r, cols = w.shape
    tr = _tile(r, max(8, ADAMW_BLOCK_ELEMS // cols // 8 * 8), 8)

    def body(w_ref, g_ref, m_ref, v_ref, go_ref, d_ref, nm_ref, nv_ref):
        gv = g_ref[...]
        go_ref[...] = gv
        nm = ADAM_B1 * m_ref[...] + (1.0 - ADAM_B1) * gv
        nv = ADAM_B2 * v_ref[...] + (1.0 - ADAM_B2) * jnp.square(gv)
        m_hat = nm / (1.0 - ADAM_B1 ** ADAM_STEP)
        v_hat = nv / (1.0 - ADAM_B2 ** ADAM_STEP)
        d_ref[...] = -ADAM_LR * (m_hat / (jnp.sqrt(v_hat) + ADAM_EPS) + ADAM_WD * w_ref[...])
        nm_ref[...] = nm
        nv_ref[...] = nv

    blk = pl.BlockSpec((None, tr, cols), lambda a, i: (a, i, 0))
    return pl.pallas_call(
        body, name=name, grid=(l, r // tr), in_specs=[blk] * 4, out_specs=[blk] * 4,
        out_shape=[jax.ShapeDtypeStruct(w.shape, F32)] * 4, compiler_params=_params("parallel", "parallel"),
    )(w, g, m, v)


WEIGHTS = ("mixer_norm", "ffn_norm", "attn_w_qkv", "attn_q_norm", "attn_k_norm", "attn_sinks", "attn_w_o",
           "ssm_w_in", "ssm_conv_w", "ssm_conv_b", "ssm_dt_bias", "ssm_a_log", "ssm_d", "ssm_norm", "ssm_w_out",
           "ffn_w_gate", "ffn_w_up", "ffn_w_down")
BIG = ("attn_w_qkv", "attn_w_o", "ffn_w_gate", "ffn_w_up", "ffn_w_down", "ssm_w_in", "ssm_w_out")
MATRICES = (("attn_w_qkv", 0), ("attn_w_o", 0), ("ffn_w_gate", 0), ("ffn_w_up", 0), ("ffn_w_down", 0),
            ("ssm_w_in", 0), ("ssm_w_out", 0), ("ffn_w_gate", 1), ("ffn_w_up", 1), ("ffn_w_down", 1))
MATRIX_LAYERS = tuple((BIG.index(n), l) for n, l in MATRICES)
GROUPS = {"attn": MATRICES[0:2], "ffn0": MATRICES[2:5], "ssm": MATRICES[5:7], "ffn1": MATRICES[7:10]}
SMALL_SHARDED = ("ssm_conv_w", "ssm_conv_b", "ssm_norm")
SMALL = tuple(n for n in WEIGHTS if n not in BIG)


def _pack_rows(parts, row_unit=8):
    flat = jnp.concatenate([p.reshape(-1) for p in parts])
    pad = (-flat.shape[0]) % (LANES * row_unit)
    return jnp.pad(flat, (0, pad)).reshape(-1, LANES)


def _unpack(flat, shapes):
    out, off = [], 0
    for shp in shapes:
        size = math.prod(shp)
        out.append(flat[off:off + size].reshape(shp))
        off += size
    return out


def kernel(x, positions, mixer_norm, ffn_norm, attn_w_qkv, attn_q_norm, attn_k_norm, attn_sinks, attn_w_o, ssm_w_in, ssm_conv_w, ssm_conv_b, ssm_dt_bias, ssm_a_log, ssm_d, ssm_norm, ssm_w_out, ffn_w_gate, ffn_w_up, ffn_w_down, loss_target, m_mixer_norm, m_ffn_norm, m_attn_w_qkv, m_attn_q_norm, m_attn_k_norm, m_attn_sinks, m_attn_w_o, m_ssm_w_in, m_ssm_conv_w, m_ssm_conv_b, m_ssm_dt_bias, m_ssm_a_log, m_ssm_d, m_ssm_norm, m_ssm_w_out, m_ffn_w_gate, m_ffn_w_up, m_ffn_w_down, v_mixer_norm, v_ffn_norm, v_attn_w_qkv, v_attn_q_norm, v_attn_k_norm, v_attn_sinks, v_attn_w_o, v_ssm_w_in, v_ssm_conv_w, v_ssm_conv_b, v_ssm_dt_bias, v_ssm_a_log, v_ssm_d, v_ssm_norm, v_ssm_w_out, v_ffn_w_gate, v_ffn_w_up, v_ffn_w_down):
    args = locals()
    w = {n: args[n] for n in WEIGHTS}
    m = {n: args["m_" + n] for n in WEIGHTS}
    v = {n: args["v_" + n] for n in WEIGHTS}
    ax, ay, ac = lax.axis_index("x"), lax.axis_index("y"), lax.axis_index("c")
    shard = 2 * ax + ay

    wb = {n: w[n].astype(BF16) for n in BIG}
    wl, hooks = {"ffn_w_gate": [None, None], "ffn_w_up": [None, None], "ffn_w_down": [None, None]}, {}

    def gathered(keys, stacks):
        for (n, l), st in zip(keys, stacks):
            if n == "ssm_w_in":
                wl[n] = jnp.concatenate([st[i] for i in range(N_SHARDS)]
                                        + [jnp.zeros((st.shape[1], SSM_IN_PAD - SSM_IN), BF16)], axis=1)
            elif n in ("ffn_w_gate", "ffn_w_up"):
                wl[n][l] = st
            elif n == "ffn_w_down":
                wl[n][l] = st.reshape(st.shape[0] * st.shape[1], st.shape[2])
            elif n == "attn_w_qkv":
                wl[n] = st
            else:
                wl[n] = st.reshape(st.shape[0] * st.shape[1], st.shape[2])

    def behind(name, hook):
        hooks[name] = _both(hooks[name], hook) if name in hooks else hook

    def gather_behind(keys, first_leg, second_leg):
        blocks, layers, got = [wb[n] for n, _ in keys], [l for _, l in keys], {}
        behind(first_leg, _Hook(lambda: _gather_over_ici(blocks, layers), lambda res: got.update(stacks=res)))
        behind(second_leg, _Hook(lambda: _gather_over_d2d(got["stacks"], blocks, layers),
                                 lambda res: gathered(keys, res)))

    gathered(GROUPS["attn"], _gather_shards([wb[n] for n, _ in GROUPS["attn"]],
                                            [(e, l) for e, (_, l) in enumerate(GROUPS["attn"])]))
    gather_behind(GROUPS["ffn0"], "attn_fwd", "attn_out")
    gather_behind(GROUPS["ssm"][:1], "ffn_gate_0", "ffn_up_0")
    gather_behind(GROUPS["ssm"][1:], "ffn_up_0", "ffn_down_0")
    gather_behind(GROUPS["ffn1"], "ssm_in", "ssm_out")
    small_shapes = [w[n].shape for n in SMALL_SHARDED]
    small_all = _all_gather8(_pack_rows([w[n] for n in SMALL_SHARDED]), "gather_small_params")
    small_all = small_all.reshape(N_DEV, -1)[::2]
    full, off = {}, 0
    for n, shp in zip(SMALL_SHARDED, small_shapes):
        size = math.prod(shp)
        seg = small_all[:, off:off + size].reshape((N_SHARDS,) + shp)
        full[n] = jnp.moveaxis(seg, 0, -2).reshape(shp[:-1] + (N_SHARDS * shp[-1],))
        off += size
    wl.update({
        "mixer_norm": mixer_norm, "ffn_norm": ffn_norm,
        "attn_q_norm": attn_q_norm[0], "attn_k_norm": attn_k_norm[0], "attn_sinks": attn_sinks[0],
        "ssm_conv_w": full["ssm_conv_w"][0], "ssm_conv_b": full["ssm_conv_b"][0],
        "ssm_dt_bias": ssm_dt_bias[0], "ssm_a_log": ssm_a_log[0], "ssm_d": ssm_d[0],
        "ssm_norm": full["ssm_norm"][0],
    })

    c_idx = ac.reshape(1).astype(jnp.int32)
    sc_idx = jnp.stack([shard, ac]).astype(jnp.int32)
    mats, halves = {}, {n: None for n in BIG}

    def pair_sums(keys, recv):
        return [_add_pair(mats[k], r, c_idx, f"grads_add_pair_{k[0]}_{k[1]}") for k, r in zip(keys, recv)]

    def owner_sums(keys, partials, recv):
        for (n, l), p, r in zip(keys, partials, recv):
            halves[n] = _add_owned(p, r, sc_idx, l, w[n].shape, halves[n], f"grads_add_owned_{n}_{l}")

    def reduce_behind(keys, first_leg, second_leg):
        got = {}
        behind(first_leg, _Hook(lambda: _grads_to_sibling([mats[k] for k in keys]),
                                lambda res: got.update(partials=pair_sums(keys, res))))
        behind(second_leg, _Hook(lambda: _grads_to_owners(got["partials"]),
                                 lambda res: owner_sums(keys, got["partials"], res)))

    reduce_behind(GROUPS["ffn1"], "ssm_out_dx", "ssm_in_dw")
    reduce_behind(GROUPS["ssm"][:1], "ssm_in_dx", "ffn_down_dx_0")
    reduce_behind(GROUPS["ssm"][1:], "ssm_in_dx", "ffn_down_dw_0")
    reduce_behind(GROUPS["ffn0"], "attn_out_dx", "attn_bwd")
    reduce_behind(GROUPS["attn"][1:], "attn_bwd", "attn_qkv_dw")
    last, early, tail = GROUPS["attn"][0], [n for n in BIG if n != "attn_w_qkv"], {}
    behind("attn_qkv_dx", _Hook(lambda: _grads_to_sibling([mats[last]]),
                                lambda res: tail.update(partials=pair_sums([last], res))))
    behind("attn_qkv_dx", _Hook(
        lambda: _share_halves([halves[n] for n in early], [(early.index(n), l) for n, l in MATRICES if n in early]),
        lambda res: tail.update(grads=dict(zip(early, res)))))
    loss_part, dx, g_full = _local_step(x[0], positions[0], loss_target[0], wl, hooks, mats)
    owner_sums([last], tail["partials"], _exchange(_grads_to_owners(tail["partials"]), "grads_to_owners"))
    grads = tail["grads"]
    grads[last[0]], = _exchange(_share_halves([halves[last[0]]], [(0, 0)]), "grads_share_halves")

    small_full_shapes = [g_full[n].shape for n in SMALL] + [(1,)]
    small_g = _pack_rows([g_full[n] for n in SMALL] + [loss_part.reshape(1)])
    small_sum = _sum8(_all_gather8(small_g, "gather_small_grads")).reshape(-1)
    *small_list, loss = _unpack(small_sum, small_full_shapes)
    for n, g in zip(SMALL, small_list):
        if n in SMALL_SHARDED:
            width = w[n].shape[-1]
            g = lax.dynamic_slice_in_dim(g, shard * width, width, axis=g.ndim - 1)
        grads[n] = g.reshape(w[n].shape)

    delta, new_m, new_v = {}, {}, {}
    for n in BIG:
        grads[n], delta[n], new_m[n], new_v[n] = _adamw(w[n], grads[n], m[n], v[n], "adamw_" + n)
    small_local = [w[n].shape for n in SMALL]
    pk = lambda t: _pack_rows([t[n] for n in SMALL])[None]
    outs = _adamw(pk(w), pk(grads), pk(m), pk(v), "adamw_small")
    for res, o in zip((delta, new_m, new_v), outs[1:]):
        for n, a in zip(SMALL, _unpack(o.reshape(-1), small_local)):
            res[n] = a

    return (loss.reshape(()), dx[None], *[grads[n] for n in WEIGHTS], *[delta[n] for n in WEIGHTS],
            *[new_m[n] for n in WEIGHTS], *[new_v[n] for n in WEIGHTS])
```

```python
import math

import jax
import jax.numpy as jnp
from jax import lax
from jax.experimental import pallas as pl
from jax.experimental.pallas import tpu as pltpu

F32 = jnp.float32
BF16 = jnp.bfloat16

D_MODEL = 2048
EPS = 1e-6
ATT_HEAD_DIM = 64
ATT_Q_HEADS = 32
ATT_KV_HEADS = 4
ATT_GROUP = 8
ATT_BLOCK = 128
ROPE_THETA = 10000.0
Q_WIDTH = ATT_Q_HEADS * ATT_HEAD_DIM
KV_WIDTH = ATT_KV_HEADS * ATT_HEAD_DIM
SSM_D_INNER = 4096
SSM_HEADS = 64
SSM_GROUPS = 8
SSM_HPG = 8
SSM_P = 64
SSM_STATE = 128
SSM_CONV = 4
SSM_CHUNK = 256
SSM_CONV_DIM = 6144
SSM_GN = SSM_D_INNER // SSM_GROUPS
SSM_IN = SSM_D_INNER + SSM_CONV_DIM + SSM_HEADS
LANES = 128
SSM_IN_PAD = -(-SSM_IN // LANES) * LANES
N_SHARDS = 4
N_DEV = 8

ADAM_LR = 0.001
ADAM_B1 = 0.9
ADAM_B2 = 0.999
ADAM_EPS = 1e-08
ADAM_WD = 0.01
ADAM_STEP = 10

VMEM_LIMIT = 56 * 1024 * 1024
MESH = pl.DeviceIdType.MESH
ANY = pl.BlockSpec(memory_space=pl.ANY)


def _params(*sem):
    return pltpu.CompilerParams(dimension_semantics=sem, vmem_limit_bytes=VMEM_LIMIT)


def _sems(n):
    return pltpu.SemaphoreType.DMA((n,))


def _call(body, carry, name, grid, in_specs, out_specs, out_shape, scratch_shapes, sem, args):
    if carry is None:
        return pl.pallas_call(body, name=name, grid=grid, in_specs=in_specs, out_specs=out_specs,
                              out_shape=out_shape, scratch_shapes=scratch_shapes,
                              compiler_params=_params(*sem))(*args)
    n_in, n_out, n_scr = len(in_specs), len(out_specs), len(scratch_shapes)
    c_arrays, c_shapes = list(carry["arrays"]), list(carry["out_shapes"])
    n_cin, n_cout = len(c_arrays), len(c_shapes)

    def carrying(*refs):
        ins, refs = refs[:n_in], refs[n_in:]
        cin, refs = refs[:n_cin], refs[n_cin:]
        outs, refs = refs[:n_out], refs[n_out:]
        cout, refs = refs[:n_cout], refs[n_cout:]
        scratch, (send_sems, recv_sems) = refs[:n_scr], refs[n_scr:]
        copies = carry["build"](cin, cout, send_sems, recv_sems)
        ids = [pl.program_id(d) for d in range(len(grid))]
        first, last = ids[0] == 0, ids[0] == grid[0] - 1
        for d in range(1, len(grid)):
            first = jnp.logical_and(first, ids[d] == 0)
            last = jnp.logical_and(last, ids[d] == grid[d] - 1)

        @pl.when(first)
        def _():
            for cp in copies:
                cp.start()

        body(*ins, *outs, *scratch)

        @pl.when(last)
        def _():
            for cp in copies:
                cp.wait()

    aliases = {n_in + i: n_out + o for i, o in carry.get("aliases", {}).items()}
    return pl.pallas_call(
        carrying, name=name, grid=grid, in_specs=list(in_specs) + [ANY] * n_cin,
        out_specs=list(out_specs) + [ANY] * n_cout, out_shape=list(out_shape) + c_shapes,
        scratch_shapes=list(scratch_shapes) + [_sems(carry["n_sems"]), _sems(carry["n_sems"])],
        input_output_aliases=aliases, compiler_params=_params(*(["arbitrary"] * len(grid))))(*args, *c_arrays)


def _tile(dim, target, unit=LANES):
    if dim <= target:
        return dim
    t = (target // unit) * unit
    while t >= unit:
        if dim % t == 0:
            return t
        t -= unit
    return dim


def _dot(a, b):
    return lax.dot_general(a, b, (((1,), (0,)), ((), ())), preferred_element_type=F32)


def _dot_nt(a, b):
    return lax.dot_general(a, b, (((1,), (1,)), ((), ())), preferred_element_type=F32)


def _dot_tn(a, b):
    return lax.dot_general(a, b, (((0,), (0,)), ((), ())), preferred_element_type=F32)


def _split3(x):
    hi = x.astype(BF16)
    r1 = x - hi.astype(F32)
    mid = r1.astype(BF16)
    lo = (r1 - mid.astype(F32)).astype(BF16)
    return hi, mid, lo


def _dot_x(x, m):
    hi, mid, lo = _split3(x)
    return _dot(hi, m) + _dot(mid, m) + _dot(lo, m)


def _dot_x2(x, m):
    hi = x.astype(BF16)
    return _dot(hi, m) + _dot((x - hi.astype(F32)).astype(BF16), m)


def _xdot(m, x):
    hi, mid, lo = _split3(x)
    return _dot(m, hi) + _dot(m, mid) + _dot(m, lo)


def _dot_x2_nt(x, m):
    hi = x.astype(BF16)
    return _dot_nt(hi, m) + _dot_nt((x - hi.astype(F32)).astype(BF16), m)


def _dot_x_nt(x, m):
    hi, mid, lo = _split3(x)
    return _dot_nt(hi, m) + _dot_nt(mid, m) + _dot_nt(lo, m)


def _iota(shape, dim):
    return lax.broadcasted_iota(jnp.int32, shape, dim)


def _sigmoid(x):
    return 0.5 * jnp.tanh(0.5 * x) + 0.5


def _softplus(x):
    return jnp.maximum(x, 0.0) + jnp.log(1.0 + jnp.exp(-jnp.abs(x)))


MM_ROWS = 1024
MM_TILE = 1408
MM_DEPTH = 3456


def _mm(a, b, mode, name, add=None, out_dtype=F32, b_cols=False, out_cols=False, fuse=None, rows=MM_ROWS,
        carry=None):
    bs = b.shape[-2:]
    if b_cols:
        bs = (bs[0], N_SHARDS * bs[1])
    if mode == "nn":
        (m, k), (k2, n) = a.shape, bs
    elif mode == "nt":
        (m, k), (n, k2) = a.shape, bs
    else:
        (k, m), (k2, n) = a.shape, bs
    assert k == k2, (a.shape, b.shape, mode)
    split_n = (b_cols and mode == "nn") or out_cols
    split_k = b_cols and mode == "nt"
    tm = _tile(m, MM_TILE if mode == "tn" else rows)
    tn = _tile(n // N_SHARDS if split_n else n, MM_TILE)
    tk = _tile(k // N_SHARDS if split_k else k, MM_DEPTH)
    nk = k // tk
    nj, nq = (n // N_SHARDS) // tn, (k // N_SHARDS) // tk
    two_shards = split_k and nq == 1
    if two_shards:
        nk = N_SHARDS // 2
    if mode == "tn":
        a_spec = pl.BlockSpec((tk, tm), lambda i, j, q: (q, i))
    else:
        a_spec = pl.BlockSpec((tm, 2 * tk if two_shards else tk), lambda i, j, q: (i, q))
    if mode == "nt":
        if two_shards:
            b_spec = pl.BlockSpec((2, tn, tk), lambda i, j, q: (q, j, 0))
        elif b_cols:
            b_spec = pl.BlockSpec((None, tn, tk), lambda i, j, q: (q // nq, j, q % nq))
        else:
            b_spec = pl.BlockSpec((tn, tk), lambda i, j, q: (j, q))
    elif b_cols:
        b_spec = pl.BlockSpec((None, tk, tn), lambda i, j, q: (j // nj, q, j % nj))
    else:
        b_spec = pl.BlockSpec((tk, tn), lambda i, j, q: (q, j))
    add_spec = pl.BlockSpec((tm, tn), lambda i, j, q: (i, j))
    if out_cols:
        o_spec = pl.BlockSpec((None, tm, tn), lambda i, j, q: (j // nj, i, j % nj))
        o_shape = (N_SHARDS, m, n // N_SHARDS)
    else:
        o_spec, o_shape = add_spec, (m, n)
    dot = {"nn": _dot, "nt": _dot_nt, "tn": _dot_tn}[mode]
    has_add = add is not None
    fuse_fn, extra, out_dtypes = fuse if fuse is not None else (None, [], [out_dtype])
    n_in, n_out = 2 + has_add + len(extra), len(out_dtypes)

    def body(*refs):
        a_ref, b_ref = refs[:2]
        add_ref = refs[2] if has_add else None
        extra_refs = refs[2 + has_add:n_in]
        o_refs, acc_ref = refs[n_in:n_in + n_out], refs[n_in + n_out]
        if two_shards:
            part = (dot(a_ref[:, :tk].astype(BF16), b_ref[0].astype(BF16))
                    + dot(a_ref[:, tk:].astype(BF16), b_ref[1].astype(BF16)))
        else:
            part = dot(a_ref[...].astype(BF16), b_ref[...].astype(BF16))

        def finish(total):
            if has_add:
                total = total + add_ref[...].astype(F32)
            outs = (total,) if fuse_fn is None else fuse_fn(total, *[r[...] for r in extra_refs])
            for o_ref, val in zip(o_refs, outs):
                o_ref[...] = val.astype(o_ref.dtype)

        if nk == 1:
            finish(part)
        else:
            q = pl.program_id(2)

            @pl.when(q == 0)
            def _():
                acc_ref[...] = part

            @pl.when(jnp.logical_and(q > 0, q < nk - 1))
            def _():
                acc_ref[...] += part

            @pl.when(q == nk - 1)
            def _():
                finish(acc_ref[...] + part)

    in_specs = [a_spec, b_spec] + [add_spec] * (has_add + len(extra))
    args = (a, b) + ((add,) if has_add else ()) + tuple(extra)
    res = _call(body, carry, name, (m // tm, n // tn, nk), in_specs, [o_spec] * n_out,
                [jax.ShapeDtypeStruct(o_shape, dt) for dt in out_dtypes],
                [pltpu.VMEM((tm, tn) if nk > 1 else (8, LANES), F32)], ("parallel", "parallel", "arbitrary"), args)
    main = res[0] if fuse is None else res[:n_out]
    return main if carry is None else (main, res[n_out:])


def _rms_fwd(x, g, name):
    s, d = x.shape
    ts = _tile(s, 512, 8)

    def body(x_ref, g_ref, o_ref):
        xv = x_ref[...]
        r = lax.rsqrt(jnp.mean(xv * xv, axis=-1, keepdims=True) + EPS)
        o_ref[...] = (xv * r * g_ref[...]).astype(BF16)

    return pl.pallas_call(
        body, name=name, grid=(s // ts,),
        in_specs=[pl.BlockSpec((ts, d), lambda i: (i, 0)), pl.BlockSpec((1, d), lambda i: (0, 0))],
        out_specs=pl.BlockSpec((ts, d), lambda i: (i, 0)),
        out_shape=jax.ShapeDtypeStruct((s, d), BF16),
        compiler_params=_params("parallel"),
    )(x, g)


def _rms_bwd(x, g, dh, dres, name):
    s, d = x.shape
    ts = _tile(s, 512, 8)

    def body(x_ref, g_ref, dh_ref, dres_ref, dx_ref, dxb_ref, dg_ref):
        xv = x_ref[...]
        r = lax.rsqrt(jnp.mean(xv * xv, axis=-1, keepdims=True) + EPS)
        xhat = xv * r
        dhv = dh_ref[...].astype(F32)
        part = jnp.sum(dhv * xhat, axis=0, keepdims=True)

        @pl.when(pl.program_id(0) == 0)
        def _():
            dg_ref[...] = part

        @pl.when(pl.program_id(0) > 0)
        def _():
            dg_ref[...] += part

        dxh = dhv * g_ref[...]
        dx = r * (dxh - xhat * jnp.mean(dxh * xhat, axis=-1, keepdims=True))
        total = dres_ref[...] + dx
        dx_ref[...] = total
        dxb_ref[...] = total.astype(BF16)

    row = pl.BlockSpec((ts, d), lambda i: (i, 0))
    vec = pl.BlockSpec((1, d), lambda i: (0, 0))
    return pl.pallas_call(
        body, name=name, grid=(s // ts,),
        in_specs=[row, vec, row, row], out_specs=[row, row, vec],
        out_shape=[jax.ShapeDtypeStruct((s, d), F32), jax.ShapeDtypeStruct((s, d), BF16),
                   jax.ShapeDtypeStruct((1, d), F32)],
        compiler_params=_params("arbitrary"),
    )(x, g, dh, dres)


def _loss_fwd_bwd(y, target):
    s, d = y.shape
    ts = _tile(s, 512, 8)

    def body(y_ref, t_ref, l_ref, dy_ref, dyb_ref):
        diff = y_ref[...] - t_ref[...]
        dy_ref[...] = diff * (1.0 / d)
        dyb_ref[...] = (diff * (1.0 / d)).astype(BF16)
        part = jnp.full((1, LANES), 0.5 * jnp.sum(jnp.mean(diff * diff, axis=-1, keepdims=True)), F32)

        @pl.when(pl.program_id(0) == 0)
        def _():
            l_ref[...] = part

        @pl.when(pl.program_id(0) > 0)
        def _():
            l_ref[...] += part

    row = pl.BlockSpec((ts, d), lambda i: (i, 0))
    acc = pl.BlockSpec((1, LANES), lambda i: (0, 0))
    return pl.pallas_call(
        body, name="loss", grid=(s // ts,), in_specs=[row, row], out_specs=[acc, row, row],
        out_shape=[jax.ShapeDtypeStruct((1, LANES), F32), jax.ShapeDtypeStruct((s, d), F32),
                   jax.ShapeDtypeStruct((s, d), BF16)],
        compiler_params=_params("arbitrary"),
    )(y, target)


def _lane_consts():
    r, c = _iota((LANES, LANES), 0), _iota((LANES, LANES), 1)
    same = (r >> 6) == (c >> 6)
    rin, cin = r & 63, c & 63
    one = lambda cond: jnp.where(cond, 1.0, 0.0).astype(BF16)
    return dict(
        seg=one(same),
        rot=(jnp.where(same & (rin == cin + 32), -1.0, 0.0)
             + jnp.where(same & (cin == rin + 32), 1.0, 0.0)).astype(BF16),
        dup_lo=one(r == cin), dup_hi=one(r == cin + 64),
        up=one((c >= 64) & (r == c - 64)), down=one((c < 64) & (r == c + 64)),
        fold_lo=one((c < 64) & (rin == c)), fold_hi=one((c >= 64) & (rin == c - 64)),
    )


def _norm_rope(xc, gain, cos, sin, k):
    ss = _dot_x2(xc * xc, k["seg"])
    rinv = lax.rsqrt(ss * (1.0 / ATT_HEAD_DIM) + EPS)
    xhat = xc * rinv
    y = xhat * gain
    return y * cos + _dot_x2(y, k["rot"]) * sin, xhat, rinv


def _norm_rope_bwd(dr, xhat, rinv, gain, cos, sin, k):
    dy = dr * cos - _dot_x2(dr * sin, k["rot"])
    dgain = jnp.sum(dy * xhat, axis=0, keepdims=True)
    dxh = dy * gain
    dx = rinv * (dxh - xhat * (_dot_x2(dxh * xhat, k["seg"]) * (1.0 / ATT_HEAD_DIM)))
    return dx, dgain


def _attn_prep(qkv, cos, sin, gq, gk):
    s = qkv.shape[0]
    tr = _tile(s, 256, 8)

    def body(x_ref, cos_ref, sin_ref, gq_ref, gk_ref, q_ref, kk_ref, vlo_ref, vhi_ref):
        k = _lane_consts()
        cosv, sinv = cos_ref[...], sin_ref[...]
        lane = _iota((tr, LANES), 1)
        for j in range(Q_WIDTH // LANES):
            r, _, _ = _norm_rope(x_ref[:, j * LANES:(j + 1) * LANES], gq_ref[...], cosv, sinv, k)
            q_ref[:, j * LANES:(j + 1) * LANES] = r.astype(BF16)
        for i in range(KV_WIDTH // LANES):
            off = Q_WIDTH + i * LANES
            r, _, _ = _norm_rope(x_ref[:, off:off + LANES], gk_ref[...], cosv, sinv, k)
            rb = r.astype(BF16)
            kk_ref[:, (2 * i) * LANES:(2 * i + 1) * LANES] = _dot(rb, k["dup_lo"]).astype(BF16)
            kk_ref[:, (2 * i + 1) * LANES:(2 * i + 2) * LANES] = _dot(rb, k["dup_hi"]).astype(BF16)
            off = Q_WIDTH + KV_WIDTH + i * LANES
            vb = x_ref[:, off:off + LANES].astype(BF16)
            zero = jnp.zeros_like(vb)
            vlo_ref[:, (2 * i) * LANES:(2 * i + 1) * LANES] = jnp.where(lane < 64, vb, zero)
            vhi_ref[:, (2 * i) * LANES:(2 * i + 1) * LANES] = _dot(vb, k["up"]).astype(BF16)
            vlo_ref[:, (2 * i + 1) * LANES:(2 * i + 2) * LANES] = _dot(vb, k["down"]).astype(BF16)
            vhi_ref[:, (2 * i + 1) * LANES:(2 * i + 2) * LANES] = jnp.where(lane >= 64, vb, zero)

    w = qkv.shape[1]
    row = lambda width: pl.BlockSpec((tr, width), lambda i: (i, 0))
    vec = pl.BlockSpec((1, LANES), lambda i: (0, 0))
    kw = ATT_KV_HEADS * LANES
    return pl.pallas_call(
        body, name="attn_prep", grid=(s // tr,),
        in_specs=[row(w), row(LANES), row(LANES), vec, vec],
        out_specs=[row(Q_WIDTH), row(kw), row(kw), row(kw)],
        out_shape=[jax.ShapeDtypeStruct((s, Q_WIDTH), BF16)] + [jax.ShapeDtypeStruct((s, kw), BF16)] * 3,
        compiler_params=_params("parallel"),
    )(qkv, cos, sin, gq, gk)


def _softmax_sink(s, valid, sink):
    s = jnp.where(valid, s, -jnp.inf)
    m = jnp.maximum(jnp.max(s, axis=-1, keepdims=True), sink)
    p = jnp.exp(s - m)
    esink = jnp.exp(sink - m)
    inv = 1.0 / (jnp.sum(p, axis=-1, keepdims=True) + esink)
    return p * inv, esink * inv


def _attn_specs(order):
    if order == "nh":
        cur = lambda n, h: (n, h)
        prev = lambda n, h: (jnp.maximum(n - 1, 0), h)
    else:
        cur = lambda h, n: (n, h)
        prev = lambda h, n: (jnp.maximum(n - 1, 0), h)
    qs = pl.BlockSpec((ATT_BLOCK, ATT_GROUP * ATT_HEAD_DIM), cur)
    kc = pl.BlockSpec((ATT_BLOCK, LANES), cur)
    kp = pl.BlockSpec((ATT_BLOCK, LANES), prev)
    return qs, kc, kp


def _pair_rows(qp):
    lane = _iota((ATT_BLOCK, LANES), 1)
    zero = jnp.zeros_like(qp)
    return jnp.concatenate([jnp.where(lane < 64, qp, zero), jnp.where(lane >= 64, qp, zero)], axis=0)


def _pair_masks(n):
    qi = _iota((2 * ATT_BLOCK, 2 * ATT_BLOCK), 0) & (ATT_BLOCK - 1)
    kj = _iota((2 * ATT_BLOCK, 2 * ATT_BLOCK), 1)
    valid = (kj > qi) & (kj <= qi + ATT_BLOCK) & ((kj >= ATT_BLOCK) | (n > 0))
    return valid, _iota((2 * ATT_BLOCK, 1), 0) >= ATT_BLOCK


def _attn_fwd(q, kk, vlo, vhi, sinks, name="attn_fwd", carry=None):
    s = q.shape[0]
    nb = s // ATT_BLOCK
    scale = ATT_HEAD_DIM ** -0.5

    def body(sink_ref, q_ref, kc_ref, kp_ref, vloc_ref, vlop_ref, vhic_ref, vhip_ref, o_ref):
        n, h = pl.program_id(0), pl.program_id(1)
        valid, upper = _pair_masks(n)
        kw = jnp.concatenate([kp_ref[...], kc_ref[...]], axis=0)
        vcat = jnp.concatenate([vlop_ref[...], vloc_ref[...], vhip_ref[...], vhic_ref[...]], axis=0)
        for jp in range(ATT_GROUP // 2):
            q2 = _pair_rows(q_ref[:, jp * LANES:(jp + 1) * LANES])
            sink = jnp.where(upper, sink_ref[h * ATT_GROUP + 2 * jp + 1], sink_ref[h * ATT_GROUP + 2 * jp])
            probs, _ = _softmax_sink(_dot_nt(q2, kw) * scale, valid, sink)
            pcat = jnp.concatenate([probs[:ATT_BLOCK], probs[ATT_BLOCK:]], axis=1).astype(BF16)
            o_ref[:, jp * LANES:(jp + 1) * LANES] = _dot(pcat, vcat).astype(BF16)

    qs, kc, kp = _attn_specs("nh")
    res = _call(body, carry, name, (nb, ATT_KV_HEADS),
                [pl.BlockSpec(memory_space=pltpu.SMEM), qs, kc, kp, kc, kp, kc, kp], [qs],
                [jax.ShapeDtypeStruct((s, Q_WIDTH), BF16)], [], ("parallel", "parallel"),
                (sinks, q, kk, kk, vlo, vlo, vhi, vhi))
    return res[0] if carry is None else (res[0], res[1:])


def _attn_bwd(q, kk, vlo, vhi, sinks, do, name="attn_bwd", carry=None):
    s = q.shape[0]
    nb = s // ATT_BLOCK
    scale = ATT_HEAD_DIM ** -0.5

    def body(sink_ref, q_ref, kc_ref, kp_ref, vloc_ref, vlop_ref, vhic_ref, vhip_ref, do_ref,
             dq_ref, dkc_ref, dkp_ref, dvloc_ref, dvlop_ref, dvhic_ref, dvhip_ref, dsink_ref, dkk_acc, dv_acc):
        h, n = pl.program_id(0), pl.program_id(1)
        valid, upper = _pair_masks(n)
        kw = jnp.concatenate([kp_ref[...], kc_ref[...]], axis=0)
        vcat = jnp.concatenate([vlop_ref[...], vloc_ref[...], vhip_ref[...], vhic_ref[...]], axis=0)
        lane = _iota((ATT_BLOCK, LANES), 1)
        sub = _iota((ATT_GROUP, LANES), 0)
        dsink = jnp.zeros((ATT_GROUP, LANES), F32)
        for jp in range(ATT_GROUP // 2):
            q2 = _pair_rows(q_ref[:, jp * LANES:(jp + 1) * LANES])
            dop = do_ref[:, jp * LANES:(jp + 1) * LANES]
            sink = jnp.where(upper, sink_ref[h * ATT_GROUP + 2 * jp + 1], sink_ref[h * ATT_GROUP + 2 * jp])
            probs, psink = _softmax_sink(_dot_nt(q2, kw) * scale, valid, sink)
            pcat = jnp.concatenate([probs[:ATT_BLOCK], probs[ATT_BLOCK:]], axis=1).astype(BF16)
            dpc = _dot_nt(dop, vcat)
            dprobs = jnp.concatenate([dpc[:, :2 * ATT_BLOCK], dpc[:, 2 * ATT_BLOCK:]], axis=0)
            dv_part = _dot_tn(pcat, dop)
            delta = jnp.sum(probs * dprobs, axis=-1, keepdims=True)
            ds = (probs * (dprobs - delta) * scale).astype(BF16)
            sd = psink * delta
            dsink = (dsink + jnp.where(sub == 2 * jp, -jnp.sum(sd[:ATT_BLOCK]), 0.0)
                     + jnp.where(sub == 2 * jp + 1, -jnp.sum(sd[ATT_BLOCK:]), 0.0))
            dq2 = _dot(ds, kw)
            dq_ref[:, jp * LANES:(jp + 1) * LANES] = jnp.where(lane < 64, dq2[:ATT_BLOCK], dq2[ATT_BLOCK:])
            dkk_part = _dot_tn(ds, q2)
            if jp == 0:
                dkk_acc[...], dv_acc[...] = dkk_part, dv_part
            else:
                dkk_acc[...] += dkk_part
                dv_acc[...] += dv_part
        blk = ATT_BLOCK
        dkp_ref[...], dkc_ref[...] = dkk_acc[:blk], dkk_acc[blk:]
        dvlop_ref[...], dvloc_ref[...] = dv_acc[:blk], dv_acc[blk:2 * blk]
        dvhip_ref[...], dvhic_ref[...] = dv_acc[2 * blk:3 * blk], dv_acc[3 * blk:]

        @pl.when(n == 0)
        def _():
            dsink_ref[0] = dsink

        @pl.when(n > 0)
        def _():
            dsink_ref[0] += dsink

    qs, kc, kp = _attn_specs("hn")
    kw_shape = jax.ShapeDtypeStruct((s, ATT_KV_HEADS * LANES), F32)
    res = _call(body, carry, name, (ATT_KV_HEADS, nb),
                [pl.BlockSpec(memory_space=pltpu.SMEM), qs, kc, kp, kc, kp, kc, kp, qs],
                [qs] + [kc] * 6 + [pl.BlockSpec((1, ATT_GROUP, LANES), lambda h, n: (h, 0, 0))],
                [jax.ShapeDtypeStruct((s, Q_WIDTH), F32)] + [kw_shape] * 6
                + [jax.ShapeDtypeStruct((ATT_KV_HEADS, ATT_GROUP, LANES), F32)],
                [pltpu.VMEM((2 * ATT_BLOCK, LANES), F32), pltpu.VMEM((4 * ATT_BLOCK, LANES), F32)],
                ("parallel", "arbitrary"), (sinks, q, kk, kk, vlo, vlo, vhi, vhi, do))
    return res if carry is None else (res[:8], res[8:])


def _attn_prep_bwd(qkv, cos, sin, gq, gk, dq, dks, dvlos, dvhis):
    s, w = qkv.shape
    tr = ATT_BLOCK
    nb = s // tr

    def body(x_ref, cos_ref, sin_ref, gq_ref, gk_ref, dq_ref, dkc_ref, dkn_ref, dvloc_ref, dvlon_ref,
             dvhic_ref, dvhin_ref, dx_ref, dgq_ref, dgk_ref):
        n = pl.program_id(0)
        k = _lane_consts()
        cosv, sinv = cos_ref[...], sin_ref[...]
        nxt = jnp.where(n < nb - 1, 1.0, 0.0)
        lane = _iota((tr, LANES), 1)
        dgq = jnp.zeros((1, LANES), F32)
        dgk = jnp.zeros((1, LANES), F32)
        for j in range(Q_WIDTH // LANES):
            sl = slice(j * LANES, (j + 1) * LANES)
            _, xhat, rinv = _norm_rope(x_ref[:, sl], gq_ref[...], cosv, sinv, k)
            dx, dg = _norm_rope_bwd(dq_ref[:, sl], xhat, rinv, gq_ref[...], cosv, sinv, k)
            dx_ref[:, sl] = dx.astype(BF16)
            dgq = dgq + dg
        for i in range(KV_WIDTH // LANES):
            a, b = slice(2 * i * LANES, (2 * i + 1) * LANES), slice((2 * i + 1) * LANES, (2 * i + 2) * LANES)
            dr = (_dot_x2(dkc_ref[:, a] + nxt * dkn_ref[:, a], k["fold_lo"])
                  + _dot_x2(dkc_ref[:, b] + nxt * dkn_ref[:, b], k["fold_hi"]))
            sl = slice(Q_WIDTH + i * LANES, Q_WIDTH + (i + 1) * LANES)
            _, xhat, rinv = _norm_rope(x_ref[:, sl], gk_ref[...], cosv, sinv, k)
            dx, dg = _norm_rope_bwd(dr, xhat, rinv, gk_ref[...], cosv, sinv, k)
            dx_ref[:, sl] = dx.astype(BF16)
            dgk = dgk + dg
            ta = jnp.where(lane < 64, dvloc_ref[:, a] + nxt * dvlon_ref[:, a], dvhic_ref[:, a] + nxt * dvhin_ref[:, a])
            tb = jnp.where(lane < 64, dvloc_ref[:, b] + nxt * dvlon_ref[:, b], dvhic_ref[:, b] + nxt * dvhin_ref[:, b])
            sl = slice(Q_WIDTH + KV_WIDTH + i * LANES, Q_WIDTH + KV_WIDTH + (i + 1) * LANES)
            dx_ref[:, sl] = (_dot_x2(ta, k["fold_lo"]) + _dot_x2(tb, k["fold_hi"])).astype(BF16)

        @pl.when(n == 0)
        def _():
            dgq_ref[...] = dgq
            dgk_ref[...] = dgk

        @pl.when(n > 0)
        def _():
            dgq_ref[...] += dgq
            dgk_ref[...] += dgk

    row = lambda width: pl.BlockSpec((tr, width), lambda i: (i, 0))
    nxt_row = pl.BlockSpec((tr, ATT_KV_HEADS * LANES), lambda i: (jnp.minimum(i + 1, nb - 1), 0))
    vec = pl.BlockSpec((1, LANES), lambda i: (0, 0))
    kw = ATT_KV_HEADS * LANES
    return pl.pallas_call(
        body, name="attn_prep_bwd", grid=(nb,),
        in_specs=[row(w), row(LANES), row(LANES), vec, vec, row(Q_WIDTH),
                  row(kw), nxt_row, row(kw), nxt_row, row(kw), nxt_row],
        out_specs=[row(w), vec, vec],
        out_shape=[jax.ShapeDtypeStruct((s, w), BF16), jax.ShapeDtypeStruct((1, LANES), F32),
                   jax.ShapeDtypeStruct((1, LANES), F32)],
        compiler_params=_params("arbitrary"),
    )(qkv, cos, sin, gq, gk, dq, dks[0], dks[1], dvlos[0], dvlos[1], dvhis[0], dvhis[1])


CONV_HALO = 8
CONV_TC = 1024
XBC_OFF = SSM_D_INNER // CONV_TC
DT_OFF = SSM_D_INNER + SSM_CONV_DIM


def _conv_pre(ext, w_ref, b_ref, ts):
    pre = b_ref[...] + w_ref[SSM_CONV - 1:SSM_CONV, :] * ext[CONV_HALO:]
    for kk in range(SSM_CONV - 1):
        pre = pre + w_ref[kk:kk + 1, :] * pltpu.roll(ext, SSM_CONV - 1 - kk, 0)[CONV_HALO:]
    return pre


def _conv_specs(ts):
    tc = CONV_TC
    src = pl.BlockSpec((ts, tc), lambda j, i: (i, XBC_OFF + j))
    halo = pl.BlockSpec((CONV_HALO, tc), lambda j, i: (jnp.maximum(i * (ts // CONV_HALO) - 1, 0), XBC_OFF + j))
    blk = pl.BlockSpec((ts, tc), lambda j, i: (i, j))
    wspec = pl.BlockSpec((SSM_CONV, tc), lambda j, i: (0, j))
    bspec = pl.BlockSpec((1, tc), lambda j, i: (0, j))
    return src, halo, blk, wspec, bspec


def _conv_fwd(zx, w, b):
    s, c = zx.shape[0], SSM_CONV_DIM
    ts = _tile(s, 512, 8)

    def body(u_ref, halo_ref, w_ref, b_ref, o_ref):
        halo = jnp.where(pl.program_id(1) > 0, halo_ref[...], 0.0)
        pre = _conv_pre(jnp.concatenate([halo, u_ref[...]], axis=0), w_ref, b_ref, ts)
        o_ref[...] = pre * _sigmoid(pre)

    src, halo, blk, wspec, bspec = _conv_specs(ts)
    return pl.pallas_call(
        body, name="conv_fwd", grid=(c // CONV_TC, s // ts),
        in_specs=[src, halo, wspec, bspec], out_specs=blk, out_shape=jax.ShapeDtypeStruct((s, c), F32),
        compiler_params=_params("parallel", "parallel"),
    )(zx, zx, w, b)


def _conv_bwd_pre(zx, w, b, dxs, dbm, dcm):
    s, c = zx.shape[0], SSM_CONV_DIM
    ts = _tile(s, 512, 8)
    nx, nb = dxs.shape[1] // CONV_TC, dbm.shape[1] // CONV_TC

    def body(u_ref, halo_ref, w_ref, b_ref, dx_ref, dbm_ref, dcm_ref, dpre_ref, dw_ref, db_ref):
        j, i = pl.program_id(0), pl.program_id(1)
        halo = jnp.where(i > 0, halo_ref[...], 0.0)
        ext = jnp.concatenate([halo, u_ref[...]], axis=0)
        pre = _conv_pre(ext, w_ref, b_ref, ts)
        sg = _sigmoid(pre)
        da = jnp.where(j < nx, dx_ref[...], jnp.where(j < nx + nb, dbm_ref[...], dcm_ref[...]))
        dpre = da * sg * (1.0 + pre * (1.0 - sg))
        dpre_ref[...] = dpre
        rows = [jnp.sum(dpre * pltpu.roll(ext, SSM_CONV - 1 - kk, 0)[CONV_HALO:], axis=0, keepdims=True)
                for kk in range(SSM_CONV - 1)]
        rows.append(jnp.sum(dpre * ext[CONV_HALO:], axis=0, keepdims=True))
        dwp = jnp.concatenate(rows, axis=0)
        dbp = jnp.sum(dpre, axis=0, keepdims=True)

        @pl.when(i == 0)
        def _():
            dw_ref[...] = dwp
            db_ref[...] = dbp

        @pl.when(i > 0)
        def _():
            dw_ref[...] += dwp
            db_ref[...] += dbp

    src, halo, blk, wspec, bspec = _conv_specs(ts)

    def part(lo, n):
        return pl.BlockSpec((ts, CONV_TC), lambda j, i: (jnp.where((j >= lo) & (j < lo + n), i, 0),
                                                         jnp.clip(j - lo, 0, n - 1)))

    return pl.pallas_call(
        body, name="conv_bwd_pre", grid=(c // CONV_TC, s // ts),
        in_specs=[src, halo, wspec, bspec, part(0, nx), part(nx, nb), part(nx + nb, nb)],
        out_specs=[blk, wspec, bspec],
        out_shape=[jax.ShapeDtypeStruct((s, c), F32), jax.ShapeDtypeStruct((SSM_CONV, c), F32),
                   jax.ShapeDtypeStruct((1, c), F32)],
        compiler_params=_params("parallel", "arbitrary"),
    )(zx, zx, w, b, dxs, dbm, dcm)


def _conv_bwd_in(dpre, w, dzx):
    s, c = dpre.shape
    ts, tc = _tile(s, 512, 8), CONV_TC
    ns = s // ts

    def body(d_ref, halo_ref, w_ref, dzx_ref, o_ref):
        del dzx_ref
        halo = jnp.where(pl.program_id(1) < ns - 1, halo_ref[...], 0.0)
        ext = jnp.concatenate([d_ref[...], halo], axis=0)
        du = w_ref[SSM_CONV - 1:SSM_CONV, :] * ext[:ts]
        for kk in range(SSM_CONV - 1):
            du = du + w_ref[kk:kk + 1, :] * pltpu.roll(ext, ts + CONV_HALO - (SSM_CONV - 1 - kk), 0)[:ts]
        o_ref[...] = du.astype(BF16)

    blk = pl.BlockSpec((ts, tc), lambda j, i: (i, j))
    halo = pl.BlockSpec((CONV_HALO, tc), lambda j, i: (jnp.minimum((i + 1) * (ts // CONV_HALO), s // CONV_HALO - 1), j))
    return pl.pallas_call(
        body, name="conv_bwd_in", grid=(c // tc, ns),
        in_specs=[blk, halo, pl.BlockSpec((SSM_CONV, tc), lambda j, i: (0, j)), ANY],
        out_specs=pl.BlockSpec((ts, tc), lambda j, i: (i, XBC_OFF + j)),
        out_shape=jax.ShapeDtypeStruct(dzx.shape, BF16), input_output_aliases={3: 0},
        compiler_params=_params("parallel", "parallel"),
    )(dpre, dpre, w, dzx)


def _ssd_common(dt_ref, dtt_ref, bias_ref, biast_ref, alog_ref, alogt_ref):
    ln = SSM_CHUNK
    raw, rawt = dt_ref[0] + bias_ref[0], dtt_ref[0] + biast_ref[0]
    dt, dtt = _softplus(raw), _softplus(rawt)
    a, at = -jnp.exp(alog_ref[0]), -jnp.exp(alogt_ref[0])
    tri = jnp.where(_iota((ln, ln), 0) >= _iota((ln, ln), 1), 1.0, 0.0).astype(BF16)
    return dict(raw=raw, rawt=rawt, dt=dt, dtt=dtt, a=a, at=at, tri=tri,
                acum=_xdot(tri, dt * a), acumt=_dot_x_nt(dtt * at, tri))


def _ssd_specs(nc, rev):
    cidx = (lambda c: nc - 1 - c) if rev else (lambda c: c)
    ln = SSM_CHUNK
    xs = pl.BlockSpec((ln, SSM_GN), lambda g, c: (cidx(c), g))
    bs = pl.BlockSpec((ln, SSM_STATE), lambda g, c: (cidx(c), SSM_D_INNER // SSM_STATE + g))
    cs = pl.BlockSpec((ln, SSM_STATE), lambda g, c: (cidx(c), SSM_D_INNER // SSM_STATE + SSM_GROUPS + g))
    dt = pl.BlockSpec((1, ln, SSM_HPG), lambda g, c: (g, cidx(c), 0))
    dtt = pl.BlockSpec((1, SSM_HPG, ln), lambda g, c: (g, 0, cidx(c)))
    row = pl.BlockSpec((1, 1, SSM_HPG), lambda g, c: (g, 0, 0))
    col = pl.BlockSpec((1, SSM_HPG, 1), lambda g, c: (g, 0, 0))
    st = pl.BlockSpec((None, None, SSM_GN, SSM_STATE), lambda g, c: (cidx(c), g, 0, 0))
    return xs, bs, cs, dt, dtt, row, col, st


def _head_expand():
    return jnp.where((_iota((SSM_HPG, SSM_GN), 1) >> 6) == _iota((SSM_HPG, SSM_GN), 0), 1.0, 0.0).astype(BF16)


def _head_expand_t():
    return jnp.where((_iota((SSM_GN, SSM_HPG), 0) >> 6) == _iota((SSM_GN, SSM_HPG), 1), 1.0, 0.0).astype(BF16)


def _dot_x_tn(x, m):
    hi, mid, lo = _split3(x)
    return _dot_tn(hi, m) + _dot_tn(mid, m) + _dot_tn(lo, m)


def _ssd_fwd(xbc, dt_g, dt_gt, bias_r, bias_c, alog_r, alog_c, d_r):
    s = xbc.shape[0]
    ln = SSM_CHUNK
    nc = s // ln

    def body(x_ref, b_ref, c_ref, dt_ref, dtt_ref, bias_ref, biast_ref, alog_ref, alogt_ref, d_ref,
             y_ref, st_ref, state):
        @pl.when(pl.program_id(1) == 0)
        def _():
            state[...] = jnp.zeros_like(state)

        cm = _ssd_common(dt_ref, dtt_ref, bias_ref, biast_ref, alog_ref, alogt_ref)
        acum, acumt = cm["acum"], cm["acumt"]
        ex = _head_expand()
        acum_x = _dot_x(acum, ex)
        xv = x_ref[...]
        xdt = xv * _dot_x(cm["dt"], ex)
        xdtb = xdt.astype(BF16)
        bb, cb = b_ref[...].astype(BF16), c_ref[...].astype(BF16)
        cbm = _dot_nt(cb, bb)
        causal = _iota((ln, ln), 0) >= _iota((ln, ln), 1)
        s2 = state[...]
        st_ref[...] = s2
        for r in range(SSM_HPG):
            sl = slice(r * SSM_P, (r + 1) * SSM_P)
            decay = jnp.exp(jnp.where(causal, acum[:, r:r + 1] - acumt[r:r + 1, :], -jnp.inf))
            y_ref[:, sl] = _dot((cbm * decay).astype(BF16), xdtb[:, sl])
        y_ref[...] = (y_ref[...] + _dot_nt(cb, s2.astype(BF16)) * jnp.exp(acum_x) + _dot_x(d_ref[0], ex) * xv)
        last_x = acum_x[ln - 1:ln, :]
        elast = jnp.exp(_xdot(_head_expand_t(), acumt[:, ln - 1:ln]))
        state[...] = s2 * elast + _dot_tn((xdt * jnp.exp(last_x - acum_x)).astype(BF16), bb)

    xs, bs, cs, dts, dtts, row, col, st = _ssd_specs(nc, False)
    return pl.pallas_call(
        body, name="ssd_fwd", grid=(SSM_GROUPS, nc),
        in_specs=[xs, bs, cs, dts, dtts, row, col, row, col, row],
        out_specs=[xs, st],
        out_shape=[jax.ShapeDtypeStruct((s, SSM_D_INNER), F32),
                   jax.ShapeDtypeStruct((nc, SSM_GROUPS, SSM_GN, SSM_STATE), F32)],
        scratch_shapes=[pltpu.VMEM((SSM_GN, SSM_STATE), F32)],
        compiler_params=_params("parallel", "arbitrary"),
    )(xbc, xbc, xbc, dt_g, dt_gt, bias_r, bias_c, alog_r, alog_c, d_r)


def _ssd_bwd(xbc, dt_g, dt_gt, bias_r, bias_c, alog_r, alog_c, d_r, states, dy):
    s = xbc.shape[0]
    ln = SSM_CHUNK
    nc = s // ln

    def body(x_ref, b_ref, c_ref, dt_ref, dtt_ref, bias_ref, biast_ref, alog_ref, alogt_ref, d_ref,
             st_ref, dy_ref, dx_ref, db_ref, dc_ref, ddt_ref, ddtt_ref, dbias_ref, dbiast_ref,
             dalog_ref, dalogt_ref, dd_ref, dstate):
        step = pl.program_id(1)

        @pl.when(step == 0)
        def _():
            dstate[...] = jnp.zeros_like(dstate)

        cm = _ssd_common(dt_ref, dtt_ref, bias_ref, biast_ref, alog_ref, alogt_ref)
        dt, acum, acumt = cm["dt"], cm["acum"], cm["acumt"]
        ex, ext = _head_expand(), _head_expand_t()
        dt_x, acum_x = _dot_x(dt, ex), _dot_x(acum, ex)
        eac_x, to_end_x = jnp.exp(acum_x), jnp.exp(acum_x[ln - 1:ln, :] - acum_x)
        xv, dyv = x_ref[...], dy_ref[...]
        xdt = xv * dt_x
        xdtb, dyb = xdt.astype(BF16), dyv.astype(BF16)
        dyeb = (dyv * eac_x).astype(BF16)
        bb, cb = b_ref[...].astype(BF16), c_ref[...].astype(BF16)
        cbm = _dot_nt(cb, bb)
        s2, ds2 = st_ref[...], dstate[...]
        s2b, ds2b = s2.astype(BF16), ds2.astype(BF16)
        dxdt_state = _dot_nt(bb, ds2b) * to_end_x
        yoff = _dot_nt(cb, s2b) * eac_x
        dc_acc = _dot(dyeb, s2b)
        db_acc = _dot((xdt * to_end_x).astype(BF16), ds2b)
        f_rows = _dot_x2_nt(xdt * dxdt_state, ex)
        elast = jnp.exp(acum[ln - 1:ln, :])
        dlast = (jnp.sum(f_rows, axis=0, keepdims=True)
                 + elast * jnp.sum(_dot_x_tn(ds2 * s2, ext), axis=0, keepdims=True))
        is_last = _iota((ln, 1), 0) == ln - 1
        dac_rows = _dot_x2_nt(dyv * yoff, ex) - f_rows + jnp.where(is_last, dlast, 0.0)
        dstate[...] = ds2 * jnp.exp(_xdot(ext, acumt[:, ln - 1:ln])) + _dot_tn(dyeb, cb)
        causal = _iota((ln, ln), 0) >= _iota((ln, ln), 1)
        lane8 = _iota((ln, SSM_HPG), 1)
        sub8 = _iota((SSM_HPG, ln), 0)
        dcb = jnp.zeros((ln, ln), F32)
        dac_cols = jnp.zeros((SSM_HPG, ln), F32)
        for r in range(SSM_HPG):
            sl = slice(r * SSM_P, (r + 1) * SSM_P)
            decay = jnp.exp(jnp.where(causal, acum[:, r:r + 1] - acumt[r:r + 1, :], -jnp.inf))
            dx_ref[:, sl] = _dot_tn((cbm * decay).astype(BF16), dyb[:, sl])
            dcb_r = _dot_nt(dyb[:, sl], xdtb[:, sl]) * decay
            dcb = dcb + dcb_r
            e = dcb_r * cbm
            dac_rows = dac_rows + jnp.where(lane8 == r, jnp.sum(e, axis=-1, keepdims=True), 0.0)
            dac_cols = dac_cols + jnp.where(sub8 == r, jnp.sum(e, axis=0, keepdims=True), 0.0)
        dxdt = dx_ref[...] + dxdt_state
        ddt_all = _dot_x2_nt(dxdt * xv, ex)
        dd_all = jnp.sum(_dot_x2_nt(dyv * xv, ex), axis=0, keepdims=True)
        dx_ref[...] = dxdt * dt_x + _dot_x(d_ref[0], ex) * dyv
        dcbb = dcb.astype(BF16)
        dc_ref[...] = dc_acc + _dot(dcbb, bb)
        db_ref[...] = db_acc + _dot_tn(dcbb, cb)
        triu = jnp.where(_iota((ln, ln), 0) <= _iota((ln, ln), 1), 1.0, 0.0).astype(BF16)
        g_rows = _xdot(triu, dac_rows)
        g_cols = _dot_x(dac_cols, cm["tri"])
        d_rows = (ddt_all + g_rows * cm["a"]) * _sigmoid(cm["raw"])
        d_cols = -(g_cols * cm["at"]) * _sigmoid(cm["rawt"])
        ddt_ref[0] = d_rows
        ddtt_ref[0] = d_cols
        parts = (jnp.sum(d_rows, axis=0, keepdims=True), jnp.sum(d_cols, axis=1, keepdims=True),
                 jnp.sum(g_rows * dt, axis=0, keepdims=True) * cm["a"],
                 -jnp.sum(g_cols * cm["dtt"], axis=1, keepdims=True) * cm["at"], dd_all)
        outs = (dbias_ref, dbiast_ref, dalog_ref, dalogt_ref, dd_ref)

        @pl.when(step == 0)
        def _():
            for o_ref, p in zip(outs, parts):
                o_ref[0] = p

        @pl.when(step > 0)
        def _():
            for o_ref, p in zip(outs, parts):
                o_ref[0] += p

    xs, bs, cs, dts, dtts, row, col, st = _ssd_specs(nc, True)
    grp = pl.BlockSpec((ln, SSM_STATE), lambda g, c: (nc - 1 - c, g))
    rows = jax.ShapeDtypeStruct((SSM_GROUPS, 1, SSM_HPG), F32)
    cols = jax.ShapeDtypeStruct((SSM_GROUPS, SSM_HPG, 1), F32)
    return pl.pallas_call(
        body, name="ssd_bwd", grid=(SSM_GROUPS, nc),
        in_specs=[xs, bs, cs, dts, dtts, row, col, row, col, row, st, xs],
        out_specs=[xs, grp, grp, dts, dtts, row, col, row, col, row],
        out_shape=[jax.ShapeDtypeStruct((s, SSM_D_INNER), F32),
                   jax.ShapeDtypeStruct((s, SSM_GROUPS * SSM_STATE), F32),
                   jax.ShapeDtypeStruct((s, SSM_GROUPS * SSM_STATE), F32),
                   jax.ShapeDtypeStruct((SSM_GROUPS, s, SSM_HPG), F32),
                   jax.ShapeDtypeStruct((SSM_GROUPS, SSM_HPG, s), F32), rows, cols, rows, cols, rows],
        scratch_shapes=[pltpu.VMEM((SSM_GN, SSM_STATE), F32)],
        compiler_params=_params("parallel", "arbitrary"),
    )(xbc, xbc, xbc, dt_g, dt_gt, bias_r, bias_c, alog_r, alog_c, d_r, states, dy)


GN_PER_BLOCK = 4
GN_WIDTH = GN_PER_BLOCK * SSM_GN


def _gate_norm_fwd(y, zx, g):
    s = y.shape[0]
    ts = _tile(s, 512, 8)

    def body(y_ref, z_ref, g_ref, o_ref):
        for k in range(GN_PER_BLOCK):
            sl = slice(k * SSM_GN, (k + 1) * SSM_GN)
            zv = z_ref[:, sl]
            yg = y_ref[:, sl] * (zv * _sigmoid(zv))
            r = lax.rsqrt(jnp.mean(yg * yg, axis=-1, keepdims=True) + EPS)
            o_ref[:, sl] = (yg * r * g_ref[:, sl]).astype(BF16)

    blk = pl.BlockSpec((ts, GN_WIDTH), lambda j, i: (i, j))
    vec = pl.BlockSpec((1, GN_WIDTH), lambda j, i: (0, j))
    return pl.pallas_call(
        body, name="gate_norm_fwd", grid=(SSM_D_INNER // GN_WIDTH, s // ts), in_specs=[blk, blk, vec],
        out_specs=blk,
        out_shape=jax.ShapeDtypeStruct((s, SSM_D_INNER), BF16), compiler_params=_params("parallel", "parallel"),
    )(y, zx, g)


def _gate_norm_bwd(y, zx, g, dout):
    s = y.shape[0]
    ts = _tile(s, 512, 8)

    def body(y_ref, z_ref, g_ref, do_ref, dy_ref, dz_ref, dg_ref):
        parts = []
        for k in range(GN_PER_BLOCK):
            sl = slice(k * SSM_GN, (k + 1) * SSM_GN)
            yv, zv, dov = y_ref[:, sl], z_ref[:, sl], do_ref[:, sl].astype(F32)
            sg = _sigmoid(zv)
            silu = zv * sg
            yg = yv * silu
            r = lax.rsqrt(jnp.mean(yg * yg, axis=-1, keepdims=True) + EPS)
            ygn = yg * r
            parts.append(jnp.sum(dov * ygn, axis=0, keepdims=True))
            dn = dov * g_ref[:, sl]
            dyg = r * (dn - ygn * jnp.mean(dn * ygn, axis=-1, keepdims=True))
            dy_ref[:, sl] = dyg * silu
            dz_ref[:, sl] = (dyg * yv * sg * (1.0 + zv * (1.0 - sg))).astype(BF16)
        part = jnp.concatenate(parts, axis=1)

        @pl.when(pl.program_id(1) == 0)
        def _():
            dg_ref[...] = part

        @pl.when(pl.program_id(1) > 0)
        def _():
            dg_ref[...] += part

    blk = pl.BlockSpec((ts, GN_WIDTH), lambda j, i: (i, j))
    vec = pl.BlockSpec((1, GN_WIDTH), lambda j, i: (0, j))
    return pl.pallas_call(
        body, name="gate_norm_bwd", grid=(SSM_D_INNER // GN_WIDTH, s // ts), in_specs=[blk, blk, vec, blk],
        out_specs=[blk, blk, vec],
        out_shape=[jax.ShapeDtypeStruct((s, SSM_D_INNER), F32), jax.ShapeDtypeStruct((s, SSM_IN_PAD), BF16),
                   jax.ShapeDtypeStruct((1, SSM_D_INNER), F32)],
        compiler_params=_params("parallel", "arbitrary"),
    )(y, zx, g, dout)


def _rope_tables(positions):
    inv_freq = ROPE_THETA ** (-jnp.arange(0, ATT_HEAD_DIM, 2, dtype=F32) / ATT_HEAD_DIM)
    ang = positions.astype(F32)[:, None] * inv_freq
    return jnp.tile(jnp.cos(ang), (1, 4)), jnp.tile(jnp.sin(ang), (1, 4))


def _group_views(v):
    return v.reshape(SSM_GROUPS, 1, SSM_HPG), v.reshape(SSM_GROUPS, SSM_HPG, 1)


def _ffn_fwd(run, x, norm_g, wg, wu, wd, tag):
    h = _rms_fwd(x, norm_g, f"ffn_norm_{tag}")
    g = run(f"ffn_gate_{tag}", _mm, h, wg, "nn", b_cols=True, out_dtype=BF16)

    def act(uv, gv):
        gv = gv.astype(F32)
        return uv, gv * _sigmoid(gv) * uv

    u, a = run(f"ffn_up_{tag}", _mm, h, wu, "nn", b_cols=True, fuse=(act, [g], [BF16, BF16]))
    return run(f"ffn_down_{tag}", _mm, a, wd, "nn", add=x), (h, g, u, a)


def _ffn_bwd(run, mats, x, norm_g, wg, wu, wd, saved, dout, dout_b, tag):
    h, g, u, a = saved

    def act_bwd(da, gv, uv):
        gv, uv = gv.astype(F32), uv.astype(F32)
        sg = _sigmoid(gv)
        return da * uv * sg * (1.0 + gv * (1.0 - sg)), da * (gv * sg)

    dg, du = run(f"ffn_down_dx_{tag}", _mm, dout_b, wd, "nt", fuse=(act_bwd, [g, u], [BF16, BF16]))
    dwd = run(f"ffn_down_dw_{tag}", _mm, a, dout_b, "tn", out_dtype=BF16)
    mats[("ffn_w_down", tag)] = dwd.reshape(N_SHARDS, dwd.shape[0] // N_SHARDS, dwd.shape[1])
    mats[("ffn_w_gate", tag)] = run(f"ffn_gate_dw_{tag}", _mm, h, dg, "tn", out_dtype=BF16, out_cols=True)
    mats[("ffn_w_up", tag)] = run(f"ffn_up_dw_{tag}", _mm, h, du, "tn", out_dtype=BF16, out_cols=True)
    dh = run(f"ffn_gate_dx_{tag}", _mm, dg, wg, "nt", b_cols=True)
    dh = run(f"ffn_up_dx_{tag}", _mm, du, wu, "nt", add=dh, b_cols=True)
    return _rms_bwd(x, norm_g, dh, dout, f"ffn_norm_bwd_{tag}")


class _Hook:
    def __init__(self, make, done):
        self.make, self.done = make, done


class _SemView:
    def __init__(self, sems, off):
        self.sems, self.off, self.at = sems, off, self

    def __getitem__(self, k):
        return self.sems.at[self.off + k]


def _both(h1, h2):
    split = {}

    def make():
        a, b = h1.make(), h2.make()
        na_in, na_out, na_sems = len(a["arrays"]), len(a["out_shapes"]), a["n_sems"]
        split["n"] = na_out

        def build(cin, cout, send_sems, recv_sems):
            return (a["build"](cin[:na_in], cout[:na_out], send_sems, recv_sems)
                    + b["build"](cin[na_in:], cout[na_out:], _SemView(send_sems, na_sems), _SemView(recv_sems, na_sems)))

        aliases = dict(a.get("aliases", {}))
        aliases.update({na_in + i: na_out + o for i, o in b.get("aliases", {}).items()})
        return dict(build=build, arrays=list(a["arrays"]) + list(b["arrays"]),
                    out_shapes=list(a["out_shapes"]) + list(b["out_shapes"]), n_sems=na_sems + b["n_sems"],
                    aliases=aliases)

    def done(res):
        h1.done(res[:split["n"]])
        h2.done(res[split["n"]:])

    return _Hook(make, done)


def _local_step(x, positions, target, w, hooks=None, mats=None):
    hooks = {} if hooks is None else hooks
    mats = {} if mats is None else mats

    def run(name, fn, *args, **kw):
        hook = hooks.get(name)
        if hook is None:
            return fn(*args, name=name, **kw)
        res, carried = fn(*args, name=name, carry=hook.make(), **kw)
        hook.done(carried)
        return res

    cos, sin = _rope_tables(positions)
    row = lambda v: v.reshape(1, -1)
    gq, gk = jnp.tile(row(w["attn_q_norm"]), (1, 2)), jnp.tile(row(w["attn_k_norm"]), (1, 2))
    sinks = w["attn_sinks"].reshape(-1)
    s = x.shape[0]
    row_stack = lambda g: g.reshape(N_SHARDS, g.shape[0] // N_SHARDS, g.shape[1])

    h0 = _rms_fwd(x, row(w["mixer_norm"][0]), "mixer_norm_0")
    qkv = run("attn_qkv", _mm, h0, w["attn_w_qkv"], "nn", b_cols=True)
    q, kk, vlo, vhi = _attn_prep(qkv, cos, sin, gq, gk)
    o = run("attn_fwd", _attn_fwd, q, kk, vlo, vhi, sinks)
    x1 = run("attn_out", _mm, o, w["attn_w_o"], "nn", add=x)
    ffn_w = lambda l: (row(w["ffn_norm"][l]), w["ffn_w_gate"][l], w["ffn_w_up"][l], w["ffn_w_down"][l])
    x2, ffn0 = _ffn_fwd(run, x1, *ffn_w(0), 0)

    h2 = _rms_fwd(x2, row(w["mixer_norm"][1]), "mixer_norm_1")
    zx = run("ssm_in", _mm, h2, w["ssm_w_in"], "nn")
    dt_g = zx[:, DT_OFF:DT_OFF + SSM_HEADS].reshape(s, SSM_GROUPS, SSM_HPG).transpose(1, 0, 2)
    dt_gt = dt_g.transpose(0, 2, 1)
    bias_r, bias_c = _group_views(w["ssm_dt_bias"].reshape(-1))
    alog_r, alog_c = _group_views(w["ssm_a_log"].reshape(-1))
    d_r, _ = _group_views(w["ssm_d"].reshape(-1))
    xbc = _conv_fwd(zx, w["ssm_conv_w"], row(w["ssm_conv_b"]))
    ssd_args = (xbc, dt_g, dt_gt, bias_r, bias_c, alog_r, alog_c, d_r)
    y, states = _ssd_fwd(*ssd_args)
    yn = _gate_norm_fwd(y, zx, row(w["ssm_norm"]))
    x3 = run("ssm_out", _mm, yn, w["ssm_w_out"], "nn", add=x2)
    x4, ffn1 = _ffn_fwd(run, x3, *ffn_w(1), 1)

    loss_row, dx4, dx4b = _loss_fwd_bwd(x4, target)

    dx3, dx3b, dfn1 = _ffn_bwd(run, mats, x3, *ffn_w(1), ffn1, dx4, dx4b, 1)
    dyn = run("ssm_out_dx", _mm, dx3b, w["ssm_w_out"], "nt")
    mats[("ssm_w_out", 0)] = row_stack(run("ssm_out_dw", _mm, yn, dx3b, "tn", out_dtype=BF16))
    dy, dzx, dssm_norm = _gate_norm_bwd(y, zx, row(w["ssm_norm"]), dyn)
    dxs, db, dc, ddt_g, ddt_gt, dbias, dbias_t, dalog, dalog_t, dd = _ssd_bwd(*ssd_args, states, dy)
    ddt_g = ddt_g + ddt_gt.transpose(0, 2, 1)
    dpre, dconv_w, dconv_b = _conv_bwd_pre(zx, w["ssm_conv_w"], row(w["ssm_conv_b"]), dxs, db, dc)
    dzx = _conv_bwd_in(dpre, w["ssm_conv_w"], dzx)
    ddt_pad = jnp.pad(ddt_g.transpose(1, 0, 2).reshape(s, SSM_HEADS), ((0, 0), (0, SSM_IN_PAD - SSM_IN)))
    dzx = lax.dynamic_update_slice(dzx, ddt_pad.astype(BF16), (0, DT_OFF))
    dw_in = run("ssm_in_dw", _mm, h2, dzx, "tn", out_dtype=BF16)
    in_shard = SSM_IN // N_SHARDS
    mats[("ssm_w_in", 0)] = jnp.stack([dw_in[:, i * in_shard:(i + 1) * in_shard] for i in range(N_SHARDS)])
    dh2 = run("ssm_in_dx", _mm, dzx, w["ssm_w_in"], "nt")
    dx2, dx2b, dmn1 = _rms_bwd(x2, row(w["mixer_norm"][1]), dh2, dx3, "mixer_norm_bwd_1")

    dx1, dx1b, dfn0 = _ffn_bwd(run, mats, x1, *ffn_w(0), ffn0, dx2, dx2b, 0)
    do = run("attn_out_dx", _mm, dx1b, w["attn_w_o"], "nt", out_dtype=BF16)
    mats[("attn_w_o", 0)] = row_stack(run("attn_out_dw", _mm, o, dx1b, "tn", out_dtype=BF16))
    dq, dkc, dkp, dvloc, dvlop, dvhic, dvhip, dsink = run("attn_bwd", _attn_bwd, q, kk, vlo, vhi, sinks, do)
    dqkv, dgq, dgk = _attn_prep_bwd(qkv, cos, sin, gq, gk, dq, (dkc, dkp), (dvloc, dvlop), (dvhic, dvhip))
    mats[("attn_w_qkv", 0)] = run("attn_qkv_dw", _mm, h0, dqkv, "tn", out_dtype=BF16, out_cols=True)
    dh0 = run("attn_qkv_dx", _mm, dqkv, w["attn_w_qkv"], "nt", b_cols=True)
    dx0, _, dmn0 = _rms_bwd(x, row(w["mixer_norm"][0]), dh0, dx1, "mixer_norm_bwd_0")

    fold = lambda v: v[0, :ATT_HEAD_DIM] + v[0, ATT_HEAD_DIM:]
    grads = {
        "mixer_norm": jnp.concatenate([dmn0, dmn1], axis=0),
        "ffn_norm": jnp.concatenate([dfn0, dfn1], axis=0),
        "attn_q_norm": fold(dgq), "attn_k_norm": fold(dgk),
        "attn_sinks": dsink[:, :, 0].reshape(-1),
        "ssm_conv_w": dconv_w, "ssm_conv_b": dconv_b.reshape(-1),
        "ssm_dt_bias": dbias.reshape(-1) + dbias_t.reshape(-1),
        "ssm_a_log": dalog.reshape(-1) + dalog_t.reshape(-1), "ssm_d": dd.reshape(-1),
        "ssm_norm": dssm_norm.reshape(-1),
    }
    return loss_row[0, 0], dx0, grads


OTHER_CHIPS = ((1, 0), (0, 1), (1, 1))


def _position():
    return lax.axis_index("x"), lax.axis_index("y"), lax.axis_index("c")


def _gather_shards(weights, layers):
    n_in, n_mat = len(weights), len(layers)

    def body(*refs):
        p, out = refs[:n_in], refs[n_in:n_in + n_mat]
        send_sems, recv_sems = refs[n_in + n_mat:]
        x, y, c = _position()
        me, sibling = (x, y, c), (x, y, 1 - c)
        chips = [(x ^ fx, y ^ fy) for fx, fy in OTHER_CHIPS]

        def rows(e, px, py, pc):
            half = out[e].shape[1] // 2
            return out[e].at[2 * px + py, pl.ds(pc * half, half), :]

        def copy(k, e, block, to, src=None):
            return pltpu.make_async_remote_copy(
                src_ref=rows(e, *block) if src is None else src, dst_ref=rows(e, *block),
                send_sem=send_sems.at[k * n_mat + e], recv_sem=recv_sems.at[k * n_mat + e],
                device_id=to, device_id_type=MESH)

        def own(e):
            i, l = layers[e]
            return pltpu.make_async_remote_copy(
                src_ref=p[i].at[l], dst_ref=out[e].at[2 * x + y], send_sem=send_sems.at[6 * n_mat + e],
                recv_sem=recv_sems.at[6 * n_mat + e], device_id=sibling, device_id_type=MESH)

        first, passed = [], []
        for e, (i, l) in enumerate(layers):
            half = out[e].shape[1] // 2
            first.append([copy(j, e, me, (*chip, c), src=p[i].at[l, pl.ds(c * half, half), :])
                          for j, chip in enumerate(chips)])
            for cp in first[-1]:
                cp.start()
        for e in range(n_mat):
            own(e).start()
        for e in range(n_mat):
            passed.append([copy(3 + j, e, (*chip, c), sibling) for j, chip in enumerate(chips)])
            for j, chip in enumerate(chips):
                copy(j, e, (*chip, c), me).wait_recv()
                passed[e][j].start()
        for e in range(n_mat):
            own(e).wait()
            for j, chip in enumerate(chips):
                copy(3 + j, e, (*chip, 1 - c), me).wait_recv()
        for e in range(n_mat):
            for cp in first[e] + passed[e]:
                cp.wait_send()

    return pl.pallas_call(
        body, name="gather_weights", in_specs=[ANY] * n_in, out_specs=[ANY] * n_mat,
        out_shape=[jax.ShapeDtypeStruct((N_SHARDS,) + weights[i].shape[1:], weights[i].dtype) for i, _ in layers],
        scratch_shapes=[_sems(7 * n_mat), _sems(7 * n_mat)],
    )(*weights)


def _all_gather8(block, name):
    m_per, n = block.shape

    def body(x_ref, out_ref, send_sems, recv_sems, local_sem):
        x, y, c = _position()
        me, sibling = (x, y, c), (x, y, 1 - c)
        chips = [(x ^ fx, y ^ fy) for fx, fy in OTHER_CHIPS]

        def rows(px, py, pc):
            return out_ref.at[pl.ds((4 * px + 2 * py + pc) * m_per, m_per), :]

        def copy(k, blk, to, src=None):
            return pltpu.make_async_remote_copy(
                src_ref=rows(*blk) if src is None else src, dst_ref=rows(*blk),
                send_sem=send_sems.at[k], recv_sem=recv_sems.at[k], device_id=to, device_id_type=MESH)

        mine = pltpu.make_async_copy(x_ref, rows(*me), local_sem)
        mine.start()
        first = [copy(0, me, sibling, src=x_ref)]
        first += [copy(1 + j, me, (*chip, c), src=x_ref) for j, chip in enumerate(chips)]
        for cp in first:
            cp.start()
        passed = [copy(4 + j, (*chip, c), sibling) for j, chip in enumerate(chips)]
        for j, chip in enumerate(chips):
            copy(1 + j, (*chip, c), me).wait_recv()
            passed[j].start()
        copy(0, sibling, me).wait_recv()
        for j, chip in enumerate(chips):
            copy(4 + j, (*chip, 1 - c), me).wait_recv()
        for cp in first + passed:
            cp.wait_send()
        mine.wait()

    return pl.pallas_call(
        body, name=name, out_shape=jax.ShapeDtypeStruct((N_DEV * m_per, n), block.dtype),
        in_specs=[pl.BlockSpec(memory_space=pltpu.VMEM)], out_specs=pl.BlockSpec(memory_space=pltpu.VMEM),
        scratch_shapes=[_sems(7), _sems(7), pltpu.SemaphoreType.DMA],
    )(block)


def _exchange(carry, name):
    n_in, n_out = len(carry["arrays"]), len(carry["out_shapes"])

    def body(*refs):
        copies = carry["build"](refs[:n_in], refs[n_in:n_in + n_out], refs[-2], refs[-1])
        for cp in copies:
            cp.start()
        for cp in copies:
            cp.wait()

    return pl.pallas_call(
        body, name=name, in_specs=[ANY] * n_in, out_specs=[ANY] * n_out, out_shape=list(carry["out_shapes"]),
        input_output_aliases=dict(carry.get("aliases", {})),
        scratch_shapes=[_sems(carry["n_sems"]), _sems(carry["n_sems"])],
    )(*carry["arrays"])


def _remote(src, dst, send_sems, recv_sems, k, to):
    return pltpu.make_async_remote_copy(src_ref=src, dst_ref=dst, send_sem=send_sems.at[k], recv_sem=recv_sems.at[k],
                                        device_id=to, device_id_type=MESH)


def _gather_over_ici(blocks, layers):
    def build(p, out, send_sems, recv_sems):
        x, y, c = _position()
        copies = []
        for e, l in enumerate(layers):
            half = out[e].shape[1] // 2
            rows = pl.ds(c * half, half)
            for j, (fx, fy) in enumerate(OTHER_CHIPS):
                copies.append(_remote(p[e].at[l, rows, :], out[e].at[2 * x + y, rows, :], send_sems, recv_sems,
                                      3 * e + j, (x ^ fx, y ^ fy, c)))
        return copies

    shapes = [jax.ShapeDtypeStruct((N_SHARDS,) + b.shape[1:], b.dtype) for b in blocks]
    return dict(build=build, arrays=list(blocks), out_shapes=shapes, n_sems=3 * len(layers))


def _gather_over_d2d(stacks, blocks, layers):
    n = len(stacks)

    def build(refs, out, send_sems, recv_sems):
        p = refs[n:]
        x, y, c = _position()
        sibling = (x, y, 1 - c)
        copies = []
        for e, l in enumerate(layers):
            half = out[e].shape[1] // 2
            for j, (fx, fy) in enumerate(OTHER_CHIPS):
                rows = out[e].at[2 * (x ^ fx) + (y ^ fy), pl.ds(c * half, half), :]
                copies.append(_remote(rows, rows, send_sems, recv_sems, 4 * e + j, sibling))
            copies.append(_remote(p[e].at[l], out[e].at[2 * x + y], send_sems, recv_sems, 4 * e + 3, sibling))
        return copies

    shapes = [jax.ShapeDtypeStruct(s.shape, s.dtype) for s in stacks]
    return dict(build=build, arrays=list(stacks) + list(blocks), out_shapes=shapes, n_sems=4 * n,
                aliases={i: i for i in range(n)})


def _grads_to_sibling(stacks):
    def build(g, out, send_sems, recv_sems):
        x, y, c = _position()
        copies = []
        for e in range(len(stacks)):
            half = g[e].shape[1] // 2
            copies.append(_remote(g[e].at[:, pl.ds((1 - c) * half, half), :], out[e], send_sems, recv_sems, e,
                                  (x, y, 1 - c)))
        return copies

    shapes = [jax.ShapeDtypeStruct((N_SHARDS, g.shape[1] // 2, g.shape[2]), g.dtype) for g in stacks]
    return dict(build=build, arrays=list(stacks), out_shapes=shapes, n_sems=len(stacks))


def _grads_to_owners(partials):
    def build(p, out, send_sems, recv_sems):
        x, y, c = _position()
        copies = []
        for e in range(len(partials)):
            for k, (fx, fy) in enumerate(OTHER_CHIPS):
                px, py = x ^ fx, y ^ fy
                copies.append(_remote(p[e].at[2 * px + py], out[e].at[k], send_sems, recv_sems, 3 * e + k,
                                      (px, py, c)))
        return copies

    shapes = [jax.ShapeDtypeStruct((len(OTHER_CHIPS),) + p.shape[1:], p.dtype) for p in partials]
    return dict(build=build, arrays=list(partials), out_shapes=shapes, n_sems=3 * len(partials))


def _share_halves(grads, layers):
    def build(_, out, send_sems, recv_sems):
        x, y, c = _position()
        copies = []
        for e, (i, l) in enumerate(layers):
            half = out[i].shape[1] // 2
            rows = out[i].at[l, pl.ds(c * half, half), :]
            copies.append(_remote(rows, rows, send_sems, recv_sems, e, (x, y, 1 - c)))
        return copies

    return dict(build=build, arrays=list(grads), out_shapes=[jax.ShapeDtypeStruct(g.shape, g.dtype) for g in grads],
                n_sems=len(layers), aliases={i: i for i in range(len(grads))})


ADD_BLOCK_ELEMS = 1 << 19


def _add_rows(half, cols):
    return _tile(half, max(16, ADD_BLOCK_ELEMS // cols // 16 * 16), 16)


def _add_pair(stack, recv, c_idx, name):
    _, half, cols = recv.shape
    tr = _add_rows(half, cols)
    nt = half // tr

    def body(c_ref, a_ref, b_ref, o_ref):
        o_ref[...] = (a_ref[...].astype(F32) + b_ref[...].astype(F32)).astype(o_ref.dtype)

    blk = pl.BlockSpec((None, tr, cols), lambda s, i, c_ref: (s, i, 0))
    return pl.pallas_call(
        body, name=name,
        grid_spec=pltpu.PrefetchScalarGridSpec(
            num_scalar_prefetch=1, grid=(N_SHARDS, nt),
            in_specs=[pl.BlockSpec((None, tr, cols), lambda s, i, c_ref: (s, c_ref[0] * nt + i, 0)), blk],
            out_specs=blk),
        out_shape=jax.ShapeDtypeStruct(recv.shape, recv.dtype),
        compiler_params=_params("parallel", "parallel"),
    )(c_idx, stack, recv)


def _add_owned(partial, recv, sc_idx, layer, shape, into, name):
    _, half, cols = partial.shape
    tr = _add_rows(half, cols)
    nt = half // tr

    def body(sc_ref, a_ref, r0_ref, r1_ref, r2_ref, *rest):
        o_ref = rest[-1]
        o_ref[...] = (((a_ref[...].astype(F32) + r0_ref[...].astype(F32)) + r1_ref[...].astype(F32))
                      + r2_ref[...].astype(F32))

    slot = lambda k: pl.BlockSpec((None, tr, cols), lambda i, sc_ref: (k, i, 0))
    has_into = into is not None
    return pl.pallas_call(
        body, name=name,
        grid_spec=pltpu.PrefetchScalarGridSpec(
            num_scalar_prefetch=1, grid=(nt,),
            in_specs=[pl.BlockSpec((None, tr, cols), lambda i, sc_ref: (sc_ref[0], i, 0)), slot(0), slot(1), slot(2)]
            + ([ANY] if has_into else []),
            out_specs=pl.BlockSpec((None, tr, cols), lambda i, sc_ref: (layer, sc_ref[1] * nt + i, 0))),
        out_shape=jax.ShapeDtypeStruct(shape, F32),
        input_output_aliases={5: 0} if has_into else {},
        compiler_params=_params("parallel"),
    )(*((sc_idx, partial, recv, recv, recv) + ((into,) if has_into else ())))


def _sum8(gathered):
    m = gathered.shape[0] // N_DEV

    def body(g_ref, o_ref):
        total = g_ref[0:m, :]
        for d in range(1, N_DEV):
            total = total + g_ref[d * m:(d + 1) * m, :]
        o_ref[...] = total

    return pl.pallas_call(
        body, name="small_grads_sum", out_shape=jax.ShapeDtypeStruct((m, LANES), F32),
        in_specs=[pl.BlockSpec(memory_space=pltpu.VMEM)], out_specs=pl.BlockSpec(memory_space=pltpu.VMEM),
    )(gathered)


ADAMW_BLOCK_ELEMS = 1 << 18


def _adamw(w, g, m, v, name):
    l, r, cols = w.shape
    tr = _tile(r, max(8, ADAMW_BLOCK_ELEMS // cols // 8 * 8), 8)

    def body(w_ref, g_ref, m_ref, v_ref, go_ref, d_ref, nm_ref, nv_ref):
        gv = g_ref[...]
        go_ref[...] = gv
        nm = ADAM_B1 * m_ref[...] + (1.0 - ADAM_B1) * gv
        nv = ADAM_B2 * v_ref[...] + (1.0 - ADAM_B2) * jnp.square(gv)
        m_hat = nm / (1.0 - ADAM_B1 ** ADAM_STEP)
        v_hat = nv / (1.0 - ADAM_B2 ** ADAM_STEP)
        d_ref[...] = -ADAM_LR * (m_hat / (jnp.sqrt(v_hat) + ADAM_EPS) + ADAM_WD * w_ref[...])
        nm_ref[...] = nm
        nv_ref[...] = nv

    blk = pl.BlockSpec((None, tr, cols), lambda a, i: (a, i, 0))
    return pl.pallas_call(
        body, name=name, grid=(l, r // tr), in_specs=[blk] * 4, out_specs=[blk] * 4,
        out_shape=[jax.ShapeDtypeStruct(w.shape, F32)] * 4, compiler_params=_params("parallel", "parallel"),
    )(w, g, m, v)


WEIGHTS = ("mixer_norm", "ffn_norm", "attn_w_qkv", "attn_q_norm", "attn_k_norm", "attn_sinks", "attn_w_o",
           "ssm_w_in", "ssm_conv_w", "ssm_conv_b", "ssm_dt_bias", "ssm_a_log", "ssm_d", "ssm_norm", "ssm_w_out",
           "ffn_w_gate", "ffn_w_up", "ffn_w_down")
BIG = ("attn_w_qkv", "attn_w_o", "ffn_w_gate", "ffn_w_up", "ffn_w_down", "ssm_w_in", "ssm_w_out")
MATRICES = (("attn_w_qkv", 0), ("attn_w_o", 0), ("ffn_w_gate", 0), ("ffn_w_up", 0), ("ffn_w_down", 0),
            ("ssm_w_in", 0), ("ssm_w_out", 0), ("ffn_w_gate", 1), ("ffn_w_up", 1), ("ffn_w_down", 1))
MATRIX_LAYERS = tuple((BIG.index(n), l) for n, l in MATRICES)
GROUPS = {"attn": MATRICES[0:2], "ffn0": MATRICES[2:5], "ssm": MATRICES[5:7], "ffn1": MATRICES[7:10]}
SMALL_SHARDED = ("ssm_conv_w", "ssm_conv_b", "ssm_norm")
SMALL = tuple(n for n in WEIGHTS if n not in BIG)


def _pack_rows(parts, row_unit=8):
    flat = jnp.concatenate([p.reshape(-1) for p in parts])
    pad = (-flat.shape[0]) % (LANES * row_unit)
    return jnp.pad(flat, (0, pad)).reshape(-1, LANES)


def _unpack(flat, shapes):
    out, off = [], 0
    for shp in shapes:
        size = math.prod(shp)
        out.append(flat[off:off + size].reshape(shp))
        off += size
    return out


def kernel(x, positions, mixer_norm, ffn_norm, attn_w_qkv, attn_q_norm, attn_k_norm, attn_sinks, attn_w_o, ssm_w_in, ssm_conv_w, ssm_conv_b, ssm_dt_bias, ssm_a_log, ssm_d, ssm_norm, ssm_w_out, ffn_w_gate, ffn_w_up, ffn_w_down, loss_target, m_mixer_norm, m_ffn_norm, m_attn_w_qkv, m_attn_q_norm, m_attn_k_norm, m_attn_sinks, m_attn_w_o, m_ssm_w_in, m_ssm_conv_w, m_ssm_conv_b, m_ssm_dt_bias, m_ssm_a_log, m_ssm_d, m_ssm_norm, m_ssm_w_out, m_ffn_w_gate, m_ffn_w_up, m_ffn_w_down, v_mixer_norm, v_ffn_norm, v_attn_w_qkv, v_attn_q_norm, v_attn_k_norm, v_attn_sinks, v_attn_w_o, v_ssm_w_in, v_ssm_conv_w, v_ssm_conv_b, v_ssm_dt_bias, v_ssm_a_log, v_ssm_d, v_ssm_norm, v_ssm_w_out, v_ffn_w_gate, v_ffn_w_up, v_ffn_w_down):
    args = locals()
    w = {n: args[n] for n in WEIGHTS}
    m = {n: args["m_" + n] for n in WEIGHTS}
    v = {n: args["v_" + n] for n in WEIGHTS}
    ax, ay, ac = lax.axis_index("x"), lax.axis_index("y"), lax.axis_index("c")
    shard = 2 * ax + ay

    wb = {n: w[n].astype(BF16) for n in BIG}
    wl, hooks = {"ffn_w_gate": [None, None], "ffn_w_up": [None, None], "ffn_w_down": [None, None]}, {}

    def gathered(keys, stacks):
        for (n, l), st in zip(keys, stacks):
            if n == "ssm_w_in":
                wl[n] = jnp.concatenate([st[i] for i in range(N_SHARDS)]
                                        + [jnp.zeros((st.shape[1], SSM_IN_PAD - SSM_IN), BF16)], axis=1)
            elif n in ("ffn_w_gate", "ffn_w_up"):
                wl[n][l] = st
            elif n == "ffn_w_down":
                wl[n][l] = st.reshape(st.shape[0] * st.shape[1], st.shape[2])
            elif n == "attn_w_qkv":
                wl[n] = st
            else:
                wl[n] = st.reshape(st.shape[0] * st.shape[1], st.shape[2])

    def behind(name, hook):
        hooks[name] = _both(hooks[name], hook) if name in hooks else hook

    def gather_behind(keys, first_leg, second_leg):
        blocks, layers, got = [wb[n] for n, _ in keys], [l for _, l in keys], {}
        behind(first_leg, _Hook(lambda: _gather_over_ici(blocks, layers), lambda res: got.update(stacks=res)))
        behind(second_leg, _Hook(lambda: _gather_over_d2d(got["stacks"], blocks, layers),
                                 lambda res: gathered(keys, res)))

    gathered(GROUPS["attn"], _gather_shards([wb[n] for n, _ in GROUPS["attn"]],
                                            [(e, l) for e, (_, l) in enumerate(GROUPS["attn"])]))
    gather_behind(GROUPS["ffn0"][:2], "attn_fwd", "attn_out")
    gather_behind(GROUPS["ffn0"][2:], "attn_qkv", "attn_fwd")
    gather_behind(GROUPS["ssm"][:1], "ffn_gate_0", "ffn_up_0")
    gather_behind(GROUPS["ssm"][1:], "ffn_up_0", "ffn_down_0")
    gather_behind(GROUPS["ffn1"], "ssm_in", "ssm_out")
    small_shapes = [w[n].shape for n in SMALL_SHARDED]
    small_all = _all_gather8(_pack_rows([w[n] for n in SMALL_SHARDED]), "gather_small_params")
    small_all = small_all.reshape(N_DEV, -1)[::2]
    full, off = {}, 0
    for n, shp in zip(SMALL_SHARDED, small_shapes):
        size = math.prod(shp)
        seg = small_all[:, off:off + size].reshape((N_SHARDS,) + shp)
        full[n] = jnp.moveaxis(seg, 0, -2).reshape(shp[:-1] + (N_SHARDS * shp[-1],))
        off += size
    wl.update({
        "mixer_norm": mixer_norm, "ffn_norm": ffn_norm,
        "attn_q_norm": attn_q_norm[0], "attn_k_norm": attn_k_norm[0], "attn_sinks": attn_sinks[0],
        "ssm_conv_w": full["ssm_conv_w"][0], "ssm_conv_b": full["ssm_conv_b"][0],
        "ssm_dt_bias": ssm_dt_bias[0], "ssm_a_log": ssm_a_log[0], "ssm_d": ssm_d[0],
        "ssm_norm": full["ssm_norm"][0],
    })

    c_idx = ac.reshape(1).astype(jnp.int32)
    sc_idx = jnp.stack([shard, ac]).astype(jnp.int32)
    mats, halves = {}, {n: None for n in BIG}

    def pair_sums(keys, recv):
        return [_add_pair(mats[k], r, c_idx, f"grads_add_pair_{k[0]}_{k[1]}") for k, r in zip(keys, recv)]

    def owner_sums(keys, partials, recv):
        for (n, l), p, r in zip(keys, partials, recv):
            halves[n] = _add_owned(p, r, sc_idx, l, w[n].shape, halves[n], f"grads_add_owned_{n}_{l}")

    def reduce_behind(keys, first_leg, second_leg):
        got = {}
        behind(first_leg, _Hook(lambda: _grads_to_sibling([mats[k] for k in keys]),
                                lambda res: got.update(partials=pair_sums(keys, res))))
        behind(second_leg, _Hook(lambda: _grads_to_owners(got["partials"]),
                                 lambda res: owner_sums(keys, got["partials"], res)))

    reduce_behind(GROUPS["ffn1"], "ssm_out_dx", "ssm_in_dw")
    reduce_behind(GROUPS["ssm"][:1], "ssm_in_dx", "ffn_down_dx_0")
    reduce_behind(GROUPS["ssm"][1:], "ssm_in_dx", "ffn_down_dw_0")
    reduce_behind(GROUPS["ffn0"], "attn_out_dx", "attn_bwd")
    reduce_behind(GROUPS["attn"][1:], "attn_bwd", "attn_qkv_dw")
    last, early, tail = GROUPS["attn"][0], [n for n in BIG if n != "attn_w_qkv"], {}
    behind("attn_qkv_dx", _Hook(lambda: _grads_to_sibling([mats[last]]),
                                lambda res: tail.update(partials=pair_sums([last], res))))
    behind("attn_qkv_dx", _Hook(
        lambda: _share_halves([halves[n] for n in early], [(early.index(n), l) for n, l in MATRICES if n in early]),
        lambda res: tail.update(grads=dict(zip(early, res)))))
    loss_part, dx, g_full = _local_step(x[0], positions[0], loss_target[0], wl, hooks, mats)
    owner_sums([last], tail["partials"], _exchange(_grads_to_owners(tail["partials"]), "grads_to_owners"))
    grads = tail["grads"]
    grads[last[0]], = _exchange(_share_halves([halves[last[0]]], [(0, 0)]), "grads_share_halves")

    small_full_shapes = [g_full[n].shape for n in SMALL] + [(1,)]
    small_g = _pack_rows([g_full[n] for n in SMALL] + [loss_part.reshape(1)])
    small_sum = _sum8(_all_gather8(small_g, "gather_small_grads")).reshape(-1)
    *small_list, loss = _unpack(small_sum, small_full_shapes)
    for n, g in zip(SMALL, small_list):
        if n in SMALL_SHARDED:
            width = w[n].shape[-1]
            g = lax.dynamic_slice_in_dim(g, shard * width, width, axis=g.ndim - 1)
        grads[n] = g.reshape(w[n].shape)

    delta, new_m, new_v = {}, {}, {}
    for n in BIG:
        grads[n], delta[n], new_m[n], new_v[n] = _adamw(w[n], grads[n], m[n], v[n], "adamw_" + n)
    small_local = [w[n].shape for n in SMALL]
    pk = lambda t: _pack_rows([t[n] for n in SMALL])[None]
    outs = _adamw(pk(w), pk(grads), pk(m), pk(v), "adamw_small")
    for res, o in zip((delta, new_m, new_v), outs[1:]):
        for n, a in zip(SMALL, _unpack(o.reshape(-1), small_local)):
            res[n] = a

    return (loss.reshape(()), dx[None], *[grads[n] for n in WEIGHTS], *[delta[n] for n in WEIGHTS],
            *[new_m[n] for n in WEIGHTS], *[new_v[n] for n in WEIGHTS])
```

```python
import math

import jax
import jax.numpy as jnp
from jax import lax
from jax.experimental import pallas as pl
from jax.experimental.pallas import tpu as pltpu

F32 = jnp.float32
BF16 = jnp.bfloat16

D_MODEL = 2048
EPS = 1e-6
ATT_HEAD_DIM = 64
ATT_Q_HEADS = 32
ATT_KV_HEADS = 4
ATT_GROUP = 8
ATT_BLOCK = 128
ROPE_THETA = 10000.0
Q_WIDTH = ATT_Q_HEADS * ATT_HEAD_DIM
KV_WIDTH = ATT_KV_HEADS * ATT_HEAD_DIM
SSM_D_INNER = 4096
SSM_HEADS = 64
SSM_GROUPS = 8
SSM_HPG = 8
SSM_P = 64
SSM_STATE = 128
SSM_CONV = 4
SSM_CHUNK = 256
SSM_CONV_DIM = 6144
SSM_GN = SSM_D_INNER // SSM_GROUPS
SSM_IN = SSM_D_INNER + SSM_CONV_DIM + SSM_HEADS
LANES = 128
SSM_IN_PAD = -(-SSM_IN // LANES) * LANES
N_SHARDS = 4
N_DEV = 8

ADAM_LR = 0.001
ADAM_B1 = 0.9
ADAM_B2 = 0.999
ADAM_EPS = 1e-08
ADAM_WD = 0.01
ADAM_STEP = 10

VMEM_LIMIT = 56 * 1024 * 1024
MESH = pl.DeviceIdType.MESH
ANY = pl.BlockSpec(memory_space=pl.ANY)


def _params(*sem):
    return pltpu.CompilerParams(dimension_semantics=sem, vmem_limit_bytes=VMEM_LIMIT)


def _sems(n):
    return pltpu.SemaphoreType.DMA((n,))


def _call(body, carry, name, grid, in_specs, out_specs, out_shape, scratch_shapes, sem, args):
    if carry is None:
        return pl.pallas_call(body, name=name, grid=grid, in_specs=in_specs, out_specs=out_specs,
                              out_shape=out_shape, scratch_shapes=scratch_shapes,
                              compiler_params=_params(*sem))(*args)
    n_in, n_out, n_scr = len(in_specs), len(out_specs), len(scratch_shapes)
    c_arrays, c_shapes = list(carry["arrays"]), list(carry["out_shapes"])
    n_cin, n_cout = len(c_arrays), len(c_shapes)

    def carrying(*refs):
        ins, refs = refs[:n_in], refs[n_in:]
        cin, refs = refs[:n_cin], refs[n_cin:]
        outs, refs = refs[:n_out], refs[n_out:]
        cout, refs = refs[:n_cout], refs[n_cout:]
        scratch, (send_sems, recv_sems) = refs[:n_scr], refs[n_scr:]
        copies = carry["build"](cin, cout, send_sems, recv_sems)
        ids = [pl.program_id(d) for d in range(len(grid))]
        first, last = ids[0] == 0, ids[0] == grid[0] - 1
        for d in range(1, len(grid)):
            first = jnp.logical_and(first, ids[d] == 0)
            last = jnp.logical_and(last, ids[d] == grid[d] - 1)

        @pl.when(first)
        def _():
            for cp in copies:
                cp.start()

        body(*ins, *outs, *scratch)

        @pl.when(last)
        def _():
            for cp in copies:
                cp.wait()

    aliases = {n_in + i: n_out + o for i, o in carry.get("aliases", {}).items()}
    return pl.pallas_call(
        carrying, name=name, grid=grid, in_specs=list(in_specs) + [ANY] * n_cin,
        out_specs=list(out_specs) + [ANY] * n_cout, out_shape=list(out_shape) + c_shapes,
        scratch_shapes=list(scratch_shapes) + [_sems(carry["n_sems"]), _sems(carry["n_sems"])],
        input_output_aliases=aliases, compiler_params=_params(*(["arbitrary"] * len(grid))))(*args, *c_arrays)


def _tile(dim, target, unit=LANES):
    if dim <= target:
        return dim
    t = (target // unit) * unit
    while t >= unit:
        if dim % t == 0:
            return t
        t -= unit
    return dim


def _dot(a, b):
    return lax.dot_general(a, b, (((1,), (0,)), ((), ())), preferred_element_type=F32)


def _dot_nt(a, b):
    return lax.dot_general(a, b, (((1,), (1,)), ((), ())), preferred_element_type=F32)


def _dot_tn(a, b):
    return lax.dot_general(a, b, (((0,), (0,)), ((), ())), preferred_element_type=F32)


def _split3(x):
    hi = x.astype(BF16)
    r1 = x - hi.astype(F32)
    mid = r1.astype(BF16)
    lo = (r1 - mid.astype(F32)).astype(BF16)
    return hi, mid, lo


def _dot_x(x, m):
    hi, mid, lo = _split3(x)
    return _dot(hi, m) + _dot(mid, m) + _dot(lo, m)


def _dot_x2(x, m):
    hi = x.astype(BF16)
    return _dot(hi, m) + _dot((x - hi.astype(F32)).astype(BF16), m)


def _xdot(m, x):
    hi, mid, lo = _split3(x)
    return _dot(m, hi) + _dot(m, mid) + _dot(m, lo)


def _dot_x2_nt(x, m):
    hi = x.astype(BF16)
    return _dot_nt(hi, m) + _dot_nt((x - hi.astype(F32)).astype(BF16), m)


def _dot_x_nt(x, m):
    hi, mid, lo = _split3(x)
    return _dot_nt(hi, m) + _dot_nt(mid, m) + _dot_nt(lo, m)


def _iota(shape, dim):
    return lax.broadcasted_iota(jnp.int32, shape, dim)


def _sigmoid(x):
    return 0.5 * jnp.tanh(0.5 * x) + 0.5


def _softplus(x):
    return jnp.maximum(x, 0.0) + jnp.log(1.0 + jnp.exp(-jnp.abs(x)))


MM_ROWS = 1024
MM_TILE = 1408
MM_DEPTH = 3456


def _mm(a, b, mode, name, add=None, out_dtype=F32, b_cols=False, out_cols=False, fuse=None, rows=MM_ROWS,
        carry=None):
    bs = b.shape[-2:]
    if b_cols:
        bs = (bs[0], N_SHARDS * bs[1])
    if mode == "nn":
        (m, k), (k2, n) = a.shape, bs
    elif mode == "nt":
        (m, k), (n, k2) = a.shape, bs
    else:
        (k, m), (k2, n) = a.shape, bs
    assert k == k2, (a.shape, b.shape, mode)
    split_n = (b_cols and mode == "nn") or out_cols
    split_k = b_cols and mode == "nt"
    tm = _tile(m, MM_TILE if mode == "tn" else rows)
    tn = _tile(n // N_SHARDS if split_n else n, MM_TILE)
    tk = _tile(k // N_SHARDS if split_k else k, MM_DEPTH)
    nk = k // tk
    nj, nq = (n // N_SHARDS) // tn, (k // N_SHARDS) // tk
    two_shards = split_k and nq == 1
    if two_shards:
        nk = N_SHARDS // 2
    if mode == "tn":
        a_spec = pl.BlockSpec((tk, tm), lambda i, j, q: (q, i))
    else:
        a_spec = pl.BlockSpec((tm, 2 * tk if two_shards else tk), lambda i, j, q: (i, q))
    if mode == "nt":
        if two_shards:
            b_spec = pl.BlockSpec((2, tn, tk), lambda i, j, q: (q, j, 0))
        elif b_cols:
            b_spec = pl.BlockSpec((None, tn, tk), lambda i, j, q: (q // nq, j, q % nq))
        else:
            b_spec = pl.BlockSpec((tn, tk), lambda i, j, q: (j, q))
    elif b_cols:
        b_spec = pl.BlockSpec((None, tk, tn), lambda i, j, q: (j // nj, q, j % nj))
    else:
        b_spec = pl.BlockSpec((tk, tn), lambda i, j, q: (q, j))
    add_spec = pl.BlockSpec((tm, tn), lambda i, j, q: (i, j))
    if out_cols:
        o_spec = pl.BlockSpec((None, tm, tn), lambda i, j, q: (j // nj, i, j % nj))
        o_shape = (N_SHARDS, m, n // N_SHARDS)
    else:
        o_spec, o_shape = add_spec, (m, n)
    dot = {"nn": _dot, "nt": _dot_nt, "tn": _dot_tn}[mode]
    has_add = add is not None
    fuse_fn, extra, out_dtypes = fuse if fuse is not None else (None, [], [out_dtype])
    n_in, n_out = 2 + has_add + len(extra), len(out_dtypes)

    def body(*refs):
        a_ref, b_ref = refs[:2]
        add_ref = refs[2] if has_add else None
        extra_refs = refs[2 + has_add:n_in]
        o_refs, acc_ref = refs[n_in:n_in + n_out], refs[n_in + n_out]
        if two_shards:
            part = (dot(a_ref[:, :tk].astype(BF16), b_ref[0].astype(BF16))
                    + dot(a_ref[:, tk:].astype(BF16), b_ref[1].astype(BF16)))
        else:
            part = dot(a_ref[...].astype(BF16), b_ref[...].astype(BF16))

        def finish(total):
            if has_add:
                total = total + add_ref[...].astype(F32)
            outs = (total,) if fuse_fn is None else fuse_fn(total, *[r[...] for r in extra_refs])
            for o_ref, val in zip(o_refs, outs):
                o_ref[...] = val.astype(o_ref.dtype)

        if nk == 1:
            finish(part)
        else:
            q = pl.program_id(2)

            @pl.when(q == 0)
            def _():
                acc_ref[...] = part

            @pl.when(jnp.logical_and(q > 0, q < nk - 1))
            def _():
                acc_ref[...] += part

            @pl.when(q == nk - 1)
            def _():
                finish(acc_ref[...] + part)

    in_specs = [a_spec, b_spec] + [add_spec] * (has_add + len(extra))
    args = (a, b) + ((add,) if has_add else ()) + tuple(extra)
    res = _call(body, carry, name, (m // tm, n // tn, nk), in_specs, [o_spec] * n_out,
                [jax.ShapeDtypeStruct(o_shape, dt) for dt in out_dtypes],
                [pltpu.VMEM((tm, tn) if nk > 1 else (8, LANES), F32)], ("parallel", "parallel", "arbitrary"), args)
    main = res[0] if fuse is None else res[:n_out]
    return main if carry is None else (main, res[n_out:])


def _rms_fwd(x, g, name):
    s, d = x.shape
    ts = _tile(s, 512, 8)

    def body(x_ref, g_ref, o_ref):
        xv = x_ref[...]
        r = lax.rsqrt(jnp.mean(xv * xv, axis=-1, keepdims=True) + EPS)
        o_ref[...] = (xv * r * g_ref[...]).astype(BF16)

    return pl.pallas_call(
        body, name=name, grid=(s // ts,),
        in_specs=[pl.BlockSpec((ts, d), lambda i: (i, 0)), pl.BlockSpec((1, d), lambda i: (0, 0))],
        out_specs=pl.BlockSpec((ts, d), lambda i: (i, 0)),
        out_shape=jax.ShapeDtypeStruct((s, d), BF16),
        compiler_params=_params("parallel"),
    )(x, g)


def _rms_bwd(x, g, dh, dres, name):
    s, d = x.shape
    ts = _tile(s, 512, 8)

    def body(x_ref, g_ref, dh_ref, dres_ref, dx_ref, dxb_ref, dg_ref):
        xv = x_ref[...]
        r = lax.rsqrt(jnp.mean(xv * xv, axis=-1, keepdims=True) + EPS)
        xhat = xv * r
        dhv = dh_ref[...].astype(F32)
        part = jnp.sum(dhv * xhat, axis=0, keepdims=True)

        @pl.when(pl.program_id(0) == 0)
        def _():
            dg_ref[...] = part

        @pl.when(pl.program_id(0) > 0)
        def _():
            dg_ref[...] += part

        dxh = dhv * g_ref[...]
        dx = r * (dxh - xhat * jnp.mean(dxh * xhat, axis=-1, keepdims=True))
        total = dres_ref[...] + dx
        dx_ref[...] = total
        dxb_ref[...] = total.astype(BF16)

    row = pl.BlockSpec((ts, d), lambda i: (i, 0))
    vec = pl.BlockSpec((1, d), lambda i: (0, 0))
    return pl.pallas_call(
        body, name=name, grid=(s // ts,),
        in_specs=[row, vec, row, row], out_specs=[row, row, vec],
        out_shape=[jax.ShapeDtypeStruct((s, d), F32), jax.ShapeDtypeStruct((s, d), BF16),
                   jax.ShapeDtypeStruct((1, d), F32)],
        compiler_params=_params("arbitrary"),
    )(x, g, dh, dres)


def _loss_fwd_bwd(y, target):
    s, d = y.shape
    ts = _tile(s, 512, 8)

    def body(y_ref, t_ref, l_ref, dy_ref, dyb_ref):
        diff = y_ref[...] - t_ref[...]
        dy_ref[...] = diff * (1.0 / d)
        dyb_ref[...] = (diff * (1.0 / d)).astype(BF16)
        part = jnp.full((1, LANES), 0.5 * jnp.sum(jnp.mean(diff * diff, axis=-1, keepdims=True)), F32)

        @pl.when(pl.program_id(0) == 0)
        def _():
            l_ref[...] = part

        @pl.when(pl.program_id(0) > 0)
        def _():
            l_ref[...] += part

    row = pl.BlockSpec((ts, d), lambda i: (i, 0))
    acc = pl.BlockSpec((1, LANES), lambda i: (0, 0))
    return pl.pallas_call(
        body, name="loss", grid=(s // ts,), in_specs=[row, row], out_specs=[acc, row, row],
        out_shape=[jax.ShapeDtypeStruct((1, LANES), F32), jax.ShapeDtypeStruct((s, d), F32),
                   jax.ShapeDtypeStruct((s, d), BF16)],
        compiler_params=_params("arbitrary"),
    )(y, target)


def _lane_consts():
    r, c = _iota((LANES, LANES), 0), _iota((LANES, LANES), 1)
    same = (r >> 6) == (c >> 6)
    rin, cin = r & 63, c & 63
    one = lambda cond: jnp.where(cond, 1.0, 0.0).astype(BF16)
    return dict(
        seg=one(same),
        rot=(jnp.where(same & (rin == cin + 32), -1.0, 0.0)
             + jnp.where(same & (cin == rin + 32), 1.0, 0.0)).astype(BF16),
        dup_lo=one(r == cin), dup_hi=one(r == cin + 64),
        up=one((c >= 64) & (r == c - 64)), down=one((c < 64) & (r == c + 64)),
        fold_lo=one((c < 64) & (rin == c)), fold_hi=one((c >= 64) & (rin == c - 64)),
    )


def _norm_rope(xc, gain, cos, sin, k):
    ss = _dot_x2(xc * xc, k["seg"])
    rinv = lax.rsqrt(ss * (1.0 / ATT_HEAD_DIM) + EPS)
    xhat = xc * rinv
    y = xhat * gain
    return y * cos + _dot_x2(y, k["rot"]) * sin, xhat, rinv


def _norm_rope_bwd(dr, xhat, rinv, gain, cos, sin, k):
    dy = dr * cos - _dot_x2(dr * sin, k["rot"])
    dgain = jnp.sum(dy * xhat, axis=0, keepdims=True)
    dxh = dy * gain
    dx = rinv * (dxh - xhat * (_dot_x2(dxh * xhat, k["seg"]) * (1.0 / ATT_HEAD_DIM)))
    return dx, dgain


def _attn_prep(qkv, cos, sin, gq, gk):
    s = qkv.shape[0]
    tr = _tile(s, 256, 8)

    def body(x_ref, cos_ref, sin_ref, gq_ref, gk_ref, q_ref, kk_ref, vlo_ref, vhi_ref):
        k = _lane_consts()
        cosv, sinv = cos_ref[...], sin_ref[...]
        lane = _iota((tr, LANES), 1)
        for j in range(Q_WIDTH // LANES):
            r, _, _ = _norm_rope(x_ref[:, j * LANES:(j + 1) * LANES], gq_ref[...], cosv, sinv, k)
            q_ref[:, j * LANES:(j + 1) * LANES] = r.astype(BF16)
        for i in range(KV_WIDTH // LANES):
            off = Q_WIDTH + i * LANES
            r, _, _ = _norm_rope(x_ref[:, off:off + LANES], gk_ref[...], cosv, sinv, k)
            rb = r.astype(BF16)
            kk_ref[:, (2 * i) * LANES:(2 * i + 1) * LANES] = _dot(rb, k["dup_lo"]).astype(BF16)
            kk_ref[:, (2 * i + 1) * LANES:(2 * i + 2) * LANES] = _dot(rb, k["dup_hi"]).astype(BF16)
            off = Q_WIDTH + KV_WIDTH + i * LANES
            vb = x_ref[:, off:off + LANES].astype(BF16)
            zero = jnp.zeros_like(vb)
            vlo_ref[:, (2 * i) * LANES:(2 * i + 1) * LANES] = jnp.where(lane < 64, vb, zero)
            vhi_ref[:, (2 * i) * LANES:(2 * i + 1) * LANES] = _dot(vb, k["up"]).astype(BF16)
            vlo_ref[:, (2 * i + 1) * LANES:(2 * i + 2) * LANES] = _dot(vb, k["down"]).astype(BF16)
            vhi_ref[:, (2 * i + 1) * LANES:(2 * i + 2) * LANES] = jnp.where(lane >= 64, vb, zero)

    w = qkv.shape[1]
    row = lambda width: pl.BlockSpec((tr, width), lambda i: (i, 0))
    vec = pl.BlockSpec((1, LANES), lambda i: (0, 0))
    kw = ATT_KV_HEADS * LANES
    return pl.pallas_call(
        body, name="attn_prep", grid=(s // tr,),
        in_specs=[row(w), row(LANES), row(LANES), vec, vec],
        out_specs=[row(Q_WIDTH), row(kw), row(kw), row(kw)],
        out_shape=[jax.ShapeDtypeStruct((s, Q_WIDTH), BF16)] + [jax.ShapeDtypeStruct((s, kw), BF16)] * 3,
        compiler_params=_params("parallel"),
    )(qkv, cos, sin, gq, gk)


def _softmax_sink(s, valid, sink):
    s = jnp.where(valid, s, -jnp.inf)
    m = jnp.maximum(jnp.max(s, axis=-1, keepdims=True), sink)
    p = jnp.exp(s - m)
    esink = jnp.exp(sink - m)
    inv = 1.0 / (jnp.sum(p, axis=-1, keepdims=True) + esink)
    return p * inv, esink * inv


def _attn_specs(order):
    if order == "nh":
        cur = lambda n, h: (n, h)
        prev = lambda n, h: (jnp.maximum(n - 1, 0), h)
    else:
        cur = lambda h, n: (n, h)
        prev = lambda h, n: (jnp.maximum(n - 1, 0), h)
    qs = pl.BlockSpec((ATT_BLOCK, ATT_GROUP * ATT_HEAD_DIM), cur)
    kc = pl.BlockSpec((ATT_BLOCK, LANES), cur)
    kp = pl.BlockSpec((ATT_BLOCK, LANES), prev)
    return qs, kc, kp


def _pair_rows(qp):
    lane = _iota((ATT_BLOCK, LANES), 1)
    zero = jnp.zeros_like(qp)
    return jnp.concatenate([jnp.where(lane < 64, qp, zero), jnp.where(lane >= 64, qp, zero)], axis=0)


def _pair_masks(n):
    qi = _iota((2 * ATT_BLOCK, 2 * ATT_BLOCK), 0) & (ATT_BLOCK - 1)
    kj = _iota((2 * ATT_BLOCK, 2 * ATT_BLOCK), 1)
    valid = (kj > qi) & (kj <= qi + ATT_BLOCK) & ((kj >= ATT_BLOCK) | (n > 0))
    return valid, _iota((2 * ATT_BLOCK, 1), 0) >= ATT_BLOCK


def _attn_fwd(q, kk, vlo, vhi, sinks, name="attn_fwd", carry=None):
    s = q.shape[0]
    nb = s // ATT_BLOCK
    scale = ATT_HEAD_DIM ** -0.5

    def body(sink_ref, q_ref, kc_ref, kp_ref, vloc_ref, vlop_ref, vhic_ref, vhip_ref, o_ref):
        n, h = pl.program_id(0), pl.program_id(1)
        valid, upper = _pair_masks(n)
        kw = jnp.concatenate([kp_ref[...], kc_ref[...]], axis=0)
        vcat = jnp.concatenate([vlop_ref[...], vloc_ref[...], vhip_ref[...], vhic_ref[...]], axis=0)
        for jp in range(ATT_GROUP // 2):
            q2 = _pair_rows(q_ref[:, jp * LANES:(jp + 1) * LANES])
            sink = jnp.where(upper, sink_ref[h * ATT_GROUP + 2 * jp + 1], sink_ref[h * ATT_GROUP + 2 * jp])
            probs, _ = _softmax_sink(_dot_nt(q2, kw) * scale, valid, sink)
            pcat = jnp.concatenate([probs[:ATT_BLOCK], probs[ATT_BLOCK:]], axis=1).astype(BF16)
            o_ref[:, jp * LANES:(jp + 1) * LANES] = _dot(pcat, vcat).astype(BF16)

    qs, kc, kp = _attn_specs("nh")
    res = _call(body, carry, name, (nb, ATT_KV_HEADS),
                [pl.BlockSpec(memory_space=pltpu.SMEM), qs, kc, kp, kc, kp, kc, kp], [qs],
                [jax.ShapeDtypeStruct((s, Q_WIDTH), BF16)], [], ("parallel", "parallel"),
                (sinks, q, kk, kk, vlo, vlo, vhi, vhi))
    return res[0] if carry is None else (res[0], res[1:])


def _attn_bwd(q, kk, vlo, vhi, sinks, do, name="attn_bwd", carry=None):
    s = q.shape[0]
    nb = s // ATT_BLOCK
    scale = ATT_HEAD_DIM ** -0.5

    def body(sink_ref, q_ref, kc_ref, kp_ref, vloc_ref, vlop_ref, vhic_ref, vhip_ref, do_ref,
             dq_ref, dkc_ref, dkp_ref, dvloc_ref, dvlop_ref, dvhic_ref, dvhip_ref, dsink_ref, dkk_acc, dv_acc):
        h, n = pl.program_id(0), pl.program_id(1)
        valid, upper = _pair_masks(n)
        kw = jnp.concatenate([kp_ref[...], kc_ref[...]], axis=0)
        vcat = jnp.concatenate([vlop_ref[...], vloc_ref[...], vhip_ref[...], vhic_ref[...]], axis=0)
        lane = _iota((ATT_BLOCK, LANES), 1)
        sub = _iota((ATT_GROUP, LANES), 0)
        dsink = jnp.zeros((ATT_GROUP, LANES), F32)
        for jp in range(ATT_GROUP // 2):
            q2 = _pair_rows(q_ref[:, jp * LANES:(jp + 1) * LANES])
            dop = do_ref[:, jp * LANES:(jp + 1) * LANES]
            sink = jnp.where(upper, sink_ref[h * ATT_GROUP + 2 * jp + 1], sink_ref[h * ATT_GROUP + 2 * jp])
            probs, psink = _softmax_sink(_dot_nt(q2, kw) * scale, valid, sink)
            pcat = jnp.concatenate([probs[:ATT_BLOCK], probs[ATT_BLOCK:]], axis=1).astype(BF16)
            dpc = _dot_nt(dop, vcat)
            dprobs = jnp.concatenate([dpc[:, :2 * ATT_BLOCK], dpc[:, 2 * ATT_BLOCK:]], axis=0)
            dv_part = _dot_tn(pcat, dop)
            delta = jnp.sum(probs * dprobs, axis=-1, keepdims=True)
            ds = (probs * (dprobs - delta) * scale).astype(BF16)
            sd = psink * delta
            dsink = (dsink + jnp.where(sub == 2 * jp, -jnp.sum(sd[:ATT_BLOCK]), 0.0)
                     + jnp.where(sub == 2 * jp + 1, -jnp.sum(sd[ATT_BLOCK:]), 0.0))
            dq2 = _dot(ds, kw)
            dq_ref[:, jp * LANES:(jp + 1) * LANES] = jnp.where(lane < 64, dq2[:ATT_BLOCK], dq2[ATT_BLOCK:])
            dkk_part = _dot_tn(ds, q2)
            if jp == 0:
                dkk_acc[...], dv_acc[...] = dkk_part, dv_part
            else:
                dkk_acc[...] += dkk_part
                dv_acc[...] += dv_part
        blk = ATT_BLOCK
        dkp_ref[...], dkc_ref[...] = dkk_acc[:blk], dkk_acc[blk:]
        dvlop_ref[...], dvloc_ref[...] = dv_acc[:blk], dv_acc[blk:2 * blk]
        dvhip_ref[...], dvhic_ref[...] = dv_acc[2 * blk:3 * blk], dv_acc[3 * blk:]

        @pl.when(n == 0)
        def _():
            dsink_ref[0] = dsink

        @pl.when(n > 0)
        def _():
            dsink_ref[0] += dsink

    qs, kc, kp = _attn_specs("hn")
    kw_shape = jax.ShapeDtypeStruct((s, ATT_KV_HEADS * LANES), F32)
    res = _call(body, carry, name, (ATT_KV_HEADS, nb),
                [pl.BlockSpec(memory_space=pltpu.SMEM), qs, kc, kp, kc, kp, kc, kp, qs],
                [qs] + [kc] * 6 + [pl.BlockSpec((1, ATT_GROUP, LANES), lambda h, n: (h, 0, 0))],
                [jax.ShapeDtypeStruct((s, Q_WIDTH), F32)] + [kw_shape] * 6
                + [jax.ShapeDtypeStruct((ATT_KV_HEADS, ATT_GROUP, LANES), F32)],
                [pltpu.VMEM((2 * ATT_BLOCK, LANES), F32), pltpu.VMEM((4 * ATT_BLOCK, LANES), F32)],
                ("parallel", "arbitrary"), (sinks, q, kk, kk, vlo, vlo, vhi, vhi, do))
    return res if carry is None else (res[:8], res[8:])


def _attn_prep_bwd(qkv, cos, sin, gq, gk, dq, dks, dvlos, dvhis):
    s, w = qkv.shape
    tr = ATT_BLOCK
    nb = s // tr

    def body(x_ref, cos_ref, sin_ref, gq_ref, gk_ref, dq_ref, dkc_ref, dkn_ref, dvloc_ref, dvlon_ref,
             dvhic_ref, dvhin_ref, dx_ref, dgq_ref, dgk_ref):
        n = pl.program_id(0)
        k = _lane_consts()
        cosv, sinv = cos_ref[...], sin_ref[...]
        nxt = jnp.where(n < nb - 1, 1.0, 0.0)
        lane = _iota((tr, LANES), 1)
        dgq = jnp.zeros((1, LANES), F32)
        dgk = jnp.zeros((1, LANES), F32)
        for j in range(Q_WIDTH // LANES):
            sl = slice(j * LANES, (j + 1) * LANES)
            _, xhat, rinv = _norm_rope(x_ref[:, sl], gq_ref[...], cosv, sinv, k)
            dx, dg = _norm_rope_bwd(dq_ref[:, sl], xhat, rinv, gq_ref[...], cosv, sinv, k)
            dx_ref[:, sl] = dx.astype(BF16)
            dgq = dgq + dg
        for i in range(KV_WIDTH // LANES):
            a, b = slice(2 * i * LANES, (2 * i + 1) * LANES), slice((2 * i + 1) * LANES, (2 * i + 2) * LANES)
            dr = (_dot_x2(dkc_ref[:, a] + nxt * dkn_ref[:, a], k["fold_lo"])
                  + _dot_x2(dkc_ref[:, b] + nxt * dkn_ref[:, b], k["fold_hi"]))
            sl = slice(Q_WIDTH + i * LANES, Q_WIDTH + (i + 1) * LANES)
            _, xhat, rinv = _norm_rope(x_ref[:, sl], gk_ref[...], cosv, sinv, k)
            dx, dg = _norm_rope_bwd(dr, xhat, rinv, gk_ref[...], cosv, sinv, k)
            dx_ref[:, sl] = dx.astype(BF16)
            dgk = dgk + dg
            ta = jnp.where(lane < 64, dvloc_ref[:, a] + nxt * dvlon_ref[:, a], dvhic_ref[:, a] + nxt * dvhin_ref[:, a])
            tb = jnp.where(lane < 64, dvloc_ref[:, b] + nxt * dvlon_ref[:, b], dvhic_ref[:, b] + nxt * dvhin_ref[:, b])
            sl = slice(Q_WIDTH + KV_WIDTH + i * LANES, Q_WIDTH + KV_WIDTH + (i + 1) * LANES)
            dx_ref[:, sl] = (_dot_x2(ta, k["fold_lo"]) + _dot_x2(tb, k["fold_hi"])).astype(BF16)

        @pl.when(n == 0)
        def _():
            dgq_ref[...] = dgq
            dgk_ref[...] = dgk

        @pl.when(n > 0)
        def _():
            dgq_ref[...] += dgq
            dgk_ref[...] += dgk

    row = lambda width: pl.BlockSpec((tr, width), lambda i: (i, 0))
    nxt_row = pl.BlockSpec((tr, ATT_KV_HEADS * LANES), lambda i: (jnp.minimum(i + 1, nb - 1), 0))
    vec = pl.BlockSpec((1, LANES), lambda i: (0, 0))
    kw = ATT_KV_HEADS * LANES
    return pl.pallas_call(
        body, name="attn_prep_bwd", grid=(nb,),
        in_specs=[row(w), row(LANES), row(LANES), vec, vec, row(Q_WIDTH),
                  row(kw), nxt_row, row(kw), nxt_row, row(kw), nxt_row],
        out_specs=[row(w), vec, vec],
        out_shape=[jax.ShapeDtypeStruct((s, w), BF16), jax.ShapeDtypeStruct((1, LANES), F32),
                   jax.ShapeDtypeStruct((1, LANES), F32)],
        compiler_params=_params("arbitrary"),
    )(qkv, cos, sin, gq, gk, dq, dks[0], dks[1], dvlos[0], dvlos[1], dvhis[0], dvhis[1])


CONV_HALO = 8
CONV_TC = 1024
CONV_ROWS = 1024
XBC_OFF = SSM_D_INNER // CONV_TC
DT_OFF = SSM_D_INNER + SSM_CONV_DIM


def _conv_pre(ext, w_ref, b_ref, ts):
    pre = b_ref[...] + w_ref[SSM_CONV - 1:SSM_CONV, :] * ext[CONV_HALO:]
    for kk in range(SSM_CONV - 1):
        pre = pre + w_ref[kk:kk + 1, :] * pltpu.roll(ext, SSM_CONV - 1 - kk, 0)[CONV_HALO:]
    return pre


def _conv_specs(ts):
    tc = CONV_TC
    src = pl.BlockSpec((ts, tc), lambda j, i: (i, XBC_OFF + j))
    halo = pl.BlockSpec((CONV_HALO, tc), lambda j, i: (jnp.maximum(i * (ts // CONV_HALO) - 1, 0), XBC_OFF + j))
    blk = pl.BlockSpec((ts, tc), lambda j, i: (i, j))
    wspec = pl.BlockSpec((SSM_CONV, tc), lambda j, i: (0, j))
    bspec = pl.BlockSpec((1, tc), lambda j, i: (0, j))
    return src, halo, blk, wspec, bspec


def _conv_fwd(zx, w, b):
    s, c = zx.shape[0], SSM_CONV_DIM
    ts = _tile(s, CONV_ROWS, 8)

    def body(u_ref, halo_ref, w_ref, b_ref, o_ref):
        halo = jnp.where(pl.program_id(1) > 0, halo_ref[...], 0.0)
        pre = _conv_pre(jnp.concatenate([halo, u_ref[...]], axis=0), w_ref, b_ref, ts)
        o_ref[...] = pre * _sigmoid(pre)

    src, halo, blk, wspec, bspec = _conv_specs(ts)
    return pl.pallas_call(
        body, name="conv_fwd", grid=(c // CONV_TC, s // ts),
        in_specs=[src, halo, wspec, bspec], out_specs=blk, out_shape=jax.ShapeDtypeStruct((s, c), F32),
        compiler_params=_params("parallel", "parallel"),
    )(zx, zx, w, b)


def _conv_bwd_pre(zx, w, b, dxs, dbm, dcm):
    s, c = zx.shape[0], SSM_CONV_DIM
    ts = _tile(s, 512, 8)
    nx, nb = dxs.shape[1] // CONV_TC, dbm.shape[1] // CONV_TC

    def body(u_ref, halo_ref, w_ref, b_ref, dx_ref, dbm_ref, dcm_ref, dpre_ref, dw_ref, db_ref):
        j, i = pl.program_id(0), pl.program_id(1)
        halo = jnp.where(i > 0, halo_ref[...], 0.0)
        ext = jnp.concatenate([halo, u_ref[...]], axis=0)
        pre = _conv_pre(ext, w_ref, b_ref, ts)
        sg = _sigmoid(pre)
        da = jnp.where(j < nx, dx_ref[...], jnp.where(j < nx + nb, dbm_ref[...], dcm_ref[...]))
        dpre = da * sg * (1.0 + pre * (1.0 - sg))
        dpre_ref[...] = dpre
        rows = [jnp.sum(dpre * pltpu.roll(ext, SSM_CONV - 1 - kk, 0)[CONV_HALO:], axis=0, keepdims=True)
                for kk in range(SSM_CONV - 1)]
        rows.append(jnp.sum(dpre * ext[CONV_HALO:], axis=0, keepdims=True))
        dwp = jnp.concatenate(rows, axis=0)
        dbp = jnp.sum(dpre, axis=0, keepdims=True)

        @pl.when(i == 0)
        def _():
            dw_ref[...] = dwp
            db_ref[...] = dbp

        @pl.when(i > 0)
        def _():
            dw_ref[...] += dwp
            db_ref[...] += dbp

    src, halo, blk, wspec, bspec = _conv_specs(ts)

    def part(lo, n):
        return pl.BlockSpec((ts, CONV_TC), lambda j, i: (jnp.where((j >= lo) & (j < lo + n), i, 0),
                                                         jnp.clip(j - lo, 0, n - 1)))

    return pl.pallas_call(
        body, name="conv_bwd_pre", grid=(c // CONV_TC, s // ts),
        in_specs=[src, halo, wspec, bspec, part(0, nx), part(nx, nb), part(nx + nb, nb)],
        out_specs=[blk, wspec, bspec],
        out_shape=[jax.ShapeDtypeStruct((s, c), F32), jax.ShapeDtypeStruct((SSM_CONV, c), F32),
                   jax.ShapeDtypeStruct((1, c), F32)],
        compiler_params=_params("parallel", "arbitrary"),
    )(zx, zx, w, b, dxs, dbm, dcm)


def _conv_bwd_in(dpre, w, dzx):
    s, c = dpre.shape
    ts, tc = _tile(s, CONV_ROWS, 8), CONV_TC
    ns = s // ts

    def body(d_ref, halo_ref, w_ref, dzx_ref, o_ref):
        del dzx_ref
        halo = jnp.where(pl.program_id(1) < ns - 1, halo_ref[...], 0.0)
        ext = jnp.concatenate([d_ref[...], halo], axis=0)
        du = w_ref[SSM_CONV - 1:SSM_CONV, :] * ext[:ts]
        for kk in range(SSM_CONV - 1):
            du = du + w_ref[kk:kk + 1, :] * pltpu.roll(ext, ts + CONV_HALO - (SSM_CONV - 1 - kk), 0)[:ts]
        o_ref[...] = du.astype(BF16)

    blk = pl.BlockSpec((ts, tc), lambda j, i: (i, j))
    halo = pl.BlockSpec((CONV_HALO, tc), lambda j, i: (jnp.minimum((i + 1) * (ts // CONV_HALO), s // CONV_HALO - 1), j))
    return pl.pallas_call(
        body, name="conv_bwd_in", grid=(c // tc, ns),
        in_specs=[blk, halo, pl.BlockSpec((SSM_CONV, tc), lambda j, i: (0, j)), ANY],
        out_specs=pl.BlockSpec((ts, tc), lambda j, i: (i, XBC_OFF + j)),
        out_shape=jax.ShapeDtypeStruct(dzx.shape, BF16), input_output_aliases={3: 0},
        compiler_params=_params("parallel", "parallel"),
    )(dpre, dpre, w, dzx)


def _ssd_common(dt_ref, dtt_ref, bias_ref, biast_ref, alog_ref, alogt_ref):
    ln = SSM_CHUNK
    raw, rawt = dt_ref[0] + bias_ref[0], dtt_ref[0] + biast_ref[0]
    dt, dtt = _softplus(raw), _softplus(rawt)
    a, at = -jnp.exp(alog_ref[0]), -jnp.exp(alogt_ref[0])
    tri = jnp.where(_iota((ln, ln), 0) >= _iota((ln, ln), 1), 1.0, 0.0).astype(BF16)
    return dict(raw=raw, rawt=rawt, dt=dt, dtt=dtt, a=a, at=at, tri=tri,
                acum=_xdot(tri, dt * a), acumt=_dot_x_nt(dtt * at, tri))


def _ssd_specs(nc, rev):
    cidx = (lambda c: nc - 1 - c) if rev else (lambda c: c)
    ln = SSM_CHUNK
    xs = pl.BlockSpec((ln, SSM_GN), lambda g, c: (cidx(c), g))
    bs = pl.BlockSpec((ln, SSM_STATE), lambda g, c: (cidx(c), SSM_D_INNER // SSM_STATE + g))
    cs = pl.BlockSpec((ln, SSM_STATE), lambda g, c: (cidx(c), SSM_D_INNER // SSM_STATE + SSM_GROUPS + g))
    dt = pl.BlockSpec((1, ln, SSM_HPG), lambda g, c: (g, cidx(c), 0))
    dtt = pl.BlockSpec((1, SSM_HPG, ln), lambda g, c: (g, 0, cidx(c)))
    row = pl.BlockSpec((1, 1, SSM_HPG), lambda g, c: (g, 0, 0))
    col = pl.BlockSpec((1, SSM_HPG, 1), lambda g, c: (g, 0, 0))
    st = pl.BlockSpec((None, None, SSM_GN, SSM_STATE), lambda g, c: (cidx(c), g, 0, 0))
    return xs, bs, cs, dt, dtt, row, col, st


def _head_expand():
    return jnp.where((_iota((SSM_HPG, SSM_GN), 1) >> 6) == _iota((SSM_HPG, SSM_GN), 0), 1.0, 0.0).astype(BF16)


def _head_expand_t():
    return jnp.where((_iota((SSM_GN, SSM_HPG), 0) >> 6) == _iota((SSM_GN, SSM_HPG), 1), 1.0, 0.0).astype(BF16)


def _dot_x_tn(x, m):
    hi, mid, lo = _split3(x)
    return _dot_tn(hi, m) + _dot_tn(mid, m) + _dot_tn(lo, m)


def _ssd_fwd(xbc, dt_g, dt_gt, bias_r, bias_c, alog_r, alog_c, d_r):
    s = xbc.shape[0]
    ln = SSM_CHUNK
    nc = s // ln

    def body(x_ref, b_ref, c_ref, dt_ref, dtt_ref, bias_ref, biast_ref, alog_ref, alogt_ref, d_ref,
             y_ref, st_ref, state):
        @pl.when(pl.program_id(1) == 0)
        def _():
            state[...] = jnp.zeros_like(state)

        cm = _ssd_common(dt_ref, dtt_ref, bias_ref, biast_ref, alog_ref, alogt_ref)
        acum, acumt = cm["acum"], cm["acumt"]
        ex = _head_expand()
        acum_x = _dot_x(acum, ex)
        xv = x_ref[...]
        xdt = xv * _dot_x(cm["dt"], ex)
        xdtb = xdt.astype(BF16)
        bb, cb = b_ref[...].astype(BF16), c_ref[...].astype(BF16)
        cbm = _dot_nt(cb, bb)
        causal = _iota((ln, ln), 0) >= _iota((ln, ln), 1)
        s2 = state[...]
        st_ref[...] = s2
        for r in range(SSM_HPG):
            sl = slice(r * SSM_P, (r + 1) * SSM_P)
            decay = jnp.exp(jnp.where(causal, acum[:, r:r + 1] - acumt[r:r + 1, :], -jnp.inf))
            y_ref[:, sl] = _dot((cbm * decay).astype(BF16), xdtb[:, sl])
        y_ref[...] = (y_ref[...] + _dot_nt(cb, s2.astype(BF16)) * jnp.exp(acum_x) + _dot_x(d_ref[0], ex) * xv)
        last_x = acum_x[ln - 1:ln, :]
        elast = jnp.exp(_xdot(_head_expand_t(), acumt[:, ln - 1:ln]))
        state[...] = s2 * elast + _dot_tn((xdt * jnp.exp(last_x - acum_x)).astype(BF16), bb)

    xs, bs, cs, dts, dtts, row, col, st = _ssd_specs(nc, False)
    return pl.pallas_call(
        body, name="ssd_fwd", grid=(SSM_GROUPS, nc),
        in_specs=[xs, bs, cs, dts, dtts, row, col, row, col, row],
        out_specs=[xs, st],
        out_shape=[jax.ShapeDtypeStruct((s, SSM_D_INNER), F32),
                   jax.ShapeDtypeStruct((nc, SSM_GROUPS, SSM_GN, SSM_STATE), F32)],
        scratch_shapes=[pltpu.VMEM((SSM_GN, SSM_STATE), F32)],
        compiler_params=_params("parallel", "arbitrary"),
    )(xbc, xbc, xbc, dt_g, dt_gt, bias_r, bias_c, alog_r, alog_c, d_r)


def _ssd_bwd(xbc, dt_g, dt_gt, bias_r, bias_c, alog_r, alog_c, d_r, states, dy):
    s = xbc.shape[0]
    ln = SSM_CHUNK
    nc = s // ln

    def body(x_ref, b_ref, c_ref, dt_ref, dtt_ref, bias_ref, biast_ref, alog_ref, alogt_ref, d_ref,
             st_ref, dy_ref, dx_ref, db_ref, dc_ref, ddt_ref, ddtt_ref, dbias_ref, dbiast_ref,
             dalog_ref, dalogt_ref, dd_ref, dstate):
        step = pl.program_id(1)

        @pl.when(step == 0)
        def _():
            dstate[...] = jnp.zeros_like(dstate)

        cm = _ssd_common(dt_ref, dtt_ref, bias_ref, biast_ref, alog_ref, alogt_ref)
        dt, acum, acumt = cm["dt"], cm["acum"], cm["acumt"]
        ex, ext = _head_expand(), _head_expand_t()
        dt_x, acum_x = _dot_x(dt, ex), _dot_x(acum, ex)
        eac_x, to_end_x = jnp.exp(acum_x), jnp.exp(acum_x[ln - 1:ln, :] - acum_x)
        xv, dyv = x_ref[...], dy_ref[...]
        xdt = xv * dt_x
        xdtb, dyb = xdt.astype(BF16), dyv.astype(BF16)
        dyeb = (dyv * eac_x).astype(BF16)
        bb, cb = b_ref[...].astype(BF16), c_ref[...].astype(BF16)
        cbm = _dot_nt(cb, bb)
        s2, ds2 = st_ref[...], dstate[...]
        s2b, ds2b = s2.astype(BF16), ds2.astype(BF16)
        dxdt_state = _dot_nt(bb, ds2b) * to_end_x
        yoff = _dot_nt(cb, s2b) * eac_x
        dc_acc = _dot(dyeb, s2b)
        db_acc = _dot((xdt * to_end_x).astype(BF16), ds2b)
        f_rows = _dot_x2_nt(xdt * dxdt_state, ex)
        elast = jnp.exp(acum[ln - 1:ln, :])
        dlast = (jnp.sum(f_rows, axis=0, keepdims=True)
                 + elast * jnp.sum(_dot_x_tn(ds2 * s2, ext), axis=0, keepdims=True))
        is_last = _iota((ln, 1), 0) == ln - 1
        dac_rows = _dot_x2_nt(dyv * yoff, ex) - f_rows + jnp.where(is_last, dlast, 0.0)
        dstate[...] = ds2 * jnp.exp(_xdot(ext, acumt[:, ln - 1:ln])) + _dot_tn(dyeb, cb)
        causal = _iota((ln, ln), 0) >= _iota((ln, ln), 1)
        lane8 = _iota((ln, SSM_HPG), 1)
        sub8 = _iota((SSM_HPG, ln), 0)
        dcb = jnp.zeros((ln, ln), F32)
        dac_cols = jnp.zeros((SSM_HPG, ln), F32)
        for r in range(SSM_HPG):
            sl = slice(r * SSM_P, (r + 1) * SSM_P)
            decay = jnp.exp(jnp.where(causal, acum[:, r:r + 1] - acumt[r:r + 1, :], -jnp.inf))
            dx_ref[:, sl] = _dot_tn((cbm * decay).astype(BF16), dyb[:, sl])
            dcb_r = _dot_nt(dyb[:, sl], xdtb[:, sl]) * decay
            dcb = dcb + dcb_r
            e = dcb_r * cbm
            dac_rows = dac_rows + jnp.where(lane8 == r, jnp.sum(e, axis=-1, keepdims=True), 0.0)
            dac_cols = dac_cols + jnp.where(sub8 == r, jnp.sum(e, axis=0, keepdims=True), 0.0)
        dxdt = dx_ref[...] + dxdt_state
        ddt_all = _dot_x2_nt(dxdt * xv, ex)
        dd_all = jnp.sum(_dot_x2_nt(dyv * xv, ex), axis=0, keepdims=True)
        dx_ref[...] = dxdt * dt_x + _dot_x(d_ref[0], ex) * dyv
        dcbb = dcb.astype(BF16)
        dc_ref[...] = dc_acc + _dot(dcbb, bb)
        db_ref[...] = db_acc + _dot_tn(dcbb, cb)
        triu = jnp.where(_iota((ln, ln), 0) <= _iota((ln, ln), 1), 1.0, 0.0).astype(BF16)
        g_rows = _xdot(triu, dac_rows)
        g_cols = _dot_x(dac_cols, cm["tri"])
        d_rows = (ddt_all + g_rows * cm["a"]) * _sigmoid(cm["raw"])
        d_cols = -(g_cols * cm["at"]) * _sigmoid(cm["rawt"])
        ddt_ref[0] = d_rows
        ddtt_ref[0] = d_cols
        parts = (jnp.sum(d_rows, axis=0, keepdims=True), jnp.sum(d_cols, axis=1, keepdims=True),
                 jnp.sum(g_rows * dt, axis=0, keepdims=True) * cm["a"],
                 -jnp.sum(g_cols * cm["dtt"], axis=1, keepdims=True) * cm["at"], dd_all)
        outs = (dbias_ref, dbiast_ref, dalog_ref, dalogt_ref, dd_ref)

        @pl.when(step == 0)
        def _():
            for o_ref, p in zip(outs, parts):
                o_ref[0] = p

        @pl.when(step > 0)
        def _():
            for o_ref, p in zip(outs, parts):
                o_ref[0] += p

    xs, bs, cs, dts, dtts, row, col, st = _ssd_specs(nc, True)
    grp = pl.BlockSpec((ln, SSM_STATE), lambda g, c: (nc - 1 - c, g))
    rows = jax.ShapeDtypeStruct((SSM_GROUPS, 1, SSM_HPG), F32)
    cols = jax.ShapeDtypeStruct((SSM_GROUPS, SSM_HPG, 1), F32)
    return pl.pallas_call(
        body, name="ssd_bwd", grid=(SSM_GROUPS, nc),
        in_specs=[xs, bs, cs, dts, dtts, row, col, row, col, row, st, xs],
        out_specs=[xs, grp, grp, dts, dtts, row, col, row, col, row],
        out_shape=[jax.ShapeDtypeStruct((s, SSM_D_INNER), F32),
                   jax.ShapeDtypeStruct((s, SSM_GROUPS * SSM_STATE), F32),
                   jax.ShapeDtypeStruct((s, SSM_GROUPS * SSM_STATE), F32),
                   jax.ShapeDtypeStruct((SSM_GROUPS, s, SSM_HPG), F32),
                   jax.ShapeDtypeStruct((SSM_GROUPS, SSM_HPG, s), F32), rows, cols, rows, cols, rows],
        scratch_shapes=[pltpu.VMEM((SSM_GN, SSM_STATE), F32)],
        compiler_params=_params("parallel", "arbitrary"),
    )(xbc, xbc, xbc, dt_g, dt_gt, bias_r, bias_c, alog_r, alog_c, d_r, states, dy)


GN_PER_BLOCK = 4
GN_WIDTH = GN_PER_BLOCK * SSM_GN


def _gate_norm_fwd(y, zx, g):
    s = y.shape[0]
    ts = _tile(s, 512, 8)

    def body(y_ref, z_ref, g_ref, o_ref):
        for k in range(GN_PER_BLOCK):
            sl = slice(k * SSM_GN, (k + 1) * SSM_GN)
            zv = z_ref[:, sl]
            yg = y_ref[:, sl] * (zv * _sigmoid(zv))
            r = lax.rsqrt(jnp.mean(yg * yg, axis=-1, keepdims=True) + EPS)
            o_ref[:, sl] = (yg * r * g_ref[:, sl]).astype(BF16)

    blk = pl.BlockSpec((ts, GN_WIDTH), lambda j, i: (i, j))
    vec = pl.BlockSpec((1, GN_WIDTH), lambda j, i: (0, j))
    return pl.pallas_call(
        body, name="gate_norm_fwd", grid=(SSM_D_INNER // GN_WIDTH, s // ts), in_specs=[blk, blk, vec],
        out_specs=blk,
        out_shape=jax.ShapeDtypeStruct((s, SSM_D_INNER), BF16), compiler_params=_params("parallel", "parallel"),
    )(y, zx, g)


def _gate_norm_bwd(y, zx, g, dout):
    s = y.shape[0]
    ts = _tile(s, 512, 8)

    def body(y_ref, z_ref, g_ref, do_ref, dy_ref, dz_ref, dg_ref):
        parts = []
        for k in range(GN_PER_BLOCK):
            sl = slice(k * SSM_GN, (k + 1) * SSM_GN)
            yv, zv, dov = y_ref[:, sl], z_ref[:, sl], do_ref[:, sl].astype(F32)
            sg = _sigmoid(zv)
            silu = zv * sg
            yg = yv * silu
            r = lax.rsqrt(jnp.mean(yg * yg, axis=-1, keepdims=True) + EPS)
            ygn = yg * r
            parts.append(jnp.sum(dov * ygn, axis=0, keepdims=True))
            dn = dov * g_ref[:, sl]
            dyg = r * (dn - ygn * jnp.mean(dn * ygn, axis=-1, keepdims=True))
            dy_ref[:, sl] = dyg * silu
            dz_ref[:, sl] = (dyg * yv * sg * (1.0 + zv * (1.0 - sg))).astype(BF16)
        part = jnp.concatenate(parts, axis=1)

        @pl.when(pl.program_id(1) == 0)
        def _():
            dg_ref[...] = part

        @pl.when(pl.program_id(1) > 0)
        def _():
            dg_ref[...] += part

    blk = pl.BlockSpec((ts, GN_WIDTH), lambda j, i: (i, j))
    vec = pl.BlockSpec((1, GN_WIDTH), lambda j, i: (0, j))
    return pl.pallas_call(
        body, name="gate_norm_bwd", grid=(SSM_D_INNER // GN_WIDTH, s // ts), in_specs=[blk, blk, vec, blk],
        out_specs=[blk, blk, vec],
        out_shape=[jax.ShapeDtypeStruct((s, SSM_D_INNER), F32), jax.ShapeDtypeStruct((s, SSM_IN_PAD), BF16),
                   jax.ShapeDtypeStruct((1, SSM_D_INNER), F32)],
        compiler_params=_params("parallel", "arbitrary"),
    )(y, zx, g, dout)


def _rope_tables(positions):
    inv_freq = ROPE_THETA ** (-jnp.arange(0, ATT_HEAD_DIM, 2, dtype=F32) / ATT_HEAD_DIM)
    ang = positions.astype(F32)[:, None] * inv_freq
    return jnp.tile(jnp.cos(ang), (1, 4)), jnp.tile(jnp.sin(ang), (1, 4))


def _group_views(v):
    return v.reshape(SSM_GROUPS, 1, SSM_HPG), v.reshape(SSM_GROUPS, SSM_HPG, 1)


def _ffn_fwd(run, x, norm_g, wg, wu, wd, tag):
    h = _rms_fwd(x, norm_g, f"ffn_norm_{tag}")
    g = run(f"ffn_gate_{tag}", _mm, h, wg, "nn", b_cols=True, out_dtype=BF16)

    def act(uv, gv):
        gv = gv.astype(F32)
        return uv, gv * _sigmoid(gv) * uv

    u, a = run(f"ffn_up_{tag}", _mm, h, wu, "nn", b_cols=True, fuse=(act, [g], [BF16, BF16]))
    return run(f"ffn_down_{tag}", _mm, a, wd, "nn", add=x), (h, g, u, a)


def _ffn_bwd(run, mats, x, norm_g, wg, wu, wd, saved, dout, dout_b, tag):
    h, g, u, a = saved

    def act_bwd(da, gv, uv):
        gv, uv = gv.astype(F32), uv.astype(F32)
        sg = _sigmoid(gv)
        return da * uv * sg * (1.0 + gv * (1.0 - sg)), da * (gv * sg)

    dg, du = run(f"ffn_down_dx_{tag}", _mm, dout_b, wd, "nt", fuse=(act_bwd, [g, u], [BF16, BF16]))
    dwd = run(f"ffn_down_dw_{tag}", _mm, a, dout_b, "tn", out_dtype=BF16)
    mats[("ffn_w_down", tag)] = dwd.reshape(N_SHARDS, dwd.shape[0] // N_SHARDS, dwd.shape[1])
    mats[("ffn_w_gate", tag)] = run(f"ffn_gate_dw_{tag}", _mm, h, dg, "tn", out_dtype=BF16, out_cols=True)
    mats[("ffn_w_up", tag)] = run(f"ffn_up_dw_{tag}", _mm, h, du, "tn", out_dtype=BF16, out_cols=True)
    dh = run(f"ffn_gate_dx_{tag}", _mm, dg, wg, "nt", b_cols=True)
    dh = run(f"ffn_up_dx_{tag}", _mm, du, wu, "nt", add=dh, b_cols=True)
    return _rms_bwd(x, norm_g, dh, dout, f"ffn_norm_bwd_{tag}")


class _Hook:
    def __init__(self, make, done):
        self.make, self.done = make, done


class _SemView:
    def __init__(self, sems, off):
        self.sems, self.off, self.at = sems, off, self

    def __getitem__(self, k):
        return self.sems.at[self.off + k]


def _both(h1, h2):
    split = {}

    def make():
        a, b = h1.make(), h2.make()
        na_in, na_out, na_sems = len(a["arrays"]), len(a["out_shapes"]), a["n_sems"]
        split["n"] = na_out

        def build(cin, cout, send_sems, recv_sems):
            return (a["build"](cin[:na_in], cout[:na_out], send_sems, recv_sems)
                    + b["build"](cin[na_in:], cout[na_out:], _SemView(send_sems, na_sems), _SemView(recv_sems, na_sems)))

        aliases = dict(a.get("aliases", {}))
        aliases.update({na_in + i: na_out + o for i, o in b.get("aliases", {}).items()})
        return dict(build=build, arrays=list(a["arrays"]) + list(b["arrays"]),
                    out_shapes=list(a["out_shapes"]) + list(b["out_shapes"]), n_sems=na_sems + b["n_sems"],
                    aliases=aliases)

    def done(res):
        h1.done(res[:split["n"]])
        h2.done(res[split["n"]:])

    return _Hook(make, done)


def _local_step(x, positions, target, w, hooks=None, mats=None):
    hooks = {} if hooks is None else hooks
    mats = {} if mats is None else mats

    def run(name, fn, *args, **kw):
        hook = hooks.get(name)
        if hook is None:
            return fn(*args, name=name, **kw)
        res, carried = fn(*args, name=name, carry=hook.make(), **kw)
        hook.done(carried)
        return res

    cos, sin = _rope_tables(positions)
    row = lambda v: v.reshape(1, -1)
    gq, gk = jnp.tile(row(w["attn_q_norm"]), (1, 2)), jnp.tile(row(w["attn_k_norm"]), (1, 2))
    sinks = w["attn_sinks"].reshape(-1)
    s = x.shape[0]
    row_stack = lambda g: g.reshape(N_SHARDS, g.shape[0] // N_SHARDS, g.shape[1])

    h0 = _rms_fwd(x, row(w["mixer_norm"][0]), "mixer_norm_0")
    qkv = run("attn_qkv", _mm, h0, w["attn_w_qkv"], "nn", b_cols=True)
    q, kk, vlo, vhi = _attn_prep(qkv, cos, sin, gq, gk)
    o = run("attn_fwd", _attn_fwd, q, kk, vlo, vhi, sinks)
    x1 = run("attn_out", _mm, o, w["attn_w_o"], "nn", add=x)
    ffn_w = lambda l: (row(w["ffn_norm"][l]), w["ffn_w_gate"][l], w["ffn_w_up"][l], w["ffn_w_down"][l])
    x2, ffn0 = _ffn_fwd(run, x1, *ffn_w(0), 0)

    h2 = _rms_fwd(x2, row(w["mixer_norm"][1]), "mixer_norm_1")
    zx = run("ssm_in", _mm, h2, w["ssm_w_in"], "nn")
    dt_g = zx[:, DT_OFF:DT_OFF + SSM_HEADS].reshape(s, SSM_GROUPS, SSM_HPG).transpose(1, 0, 2)
    dt_gt = dt_g.transpose(0, 2, 1)
    bias_r, bias_c = _group_views(w["ssm_dt_bias"].reshape(-1))
    alog_r, alog_c = _group_views(w["ssm_a_log"].reshape(-1))
    d_r, _ = _group_views(w["ssm_d"].reshape(-1))
    xbc = _conv_fwd(zx, w["ssm_conv_w"], row(w["ssm_conv_b"]))
    ssd_args = (xbc, dt_g, dt_gt, bias_r, bias_c, alog_r, alog_c, d_r)
    y, states = _ssd_fwd(*ssd_args)
    yn = _gate_norm_fwd(y, zx, row(w["ssm_norm"]))
    x3 = run("ssm_out", _mm, yn, w["ssm_w_out"], "nn", add=x2)
    x4, ffn1 = _ffn_fwd(run, x3, *ffn_w(1), 1)

    loss_row, dx4, dx4b = _loss_fwd_bwd(x4, target)

    dx3, dx3b, dfn1 = _ffn_bwd(run, mats, x3, *ffn_w(1), ffn1, dx4, dx4b, 1)
    dyn = run("ssm_out_dx", _mm, dx3b, w["ssm_w_out"], "nt")
    mats[("ssm_w_out", 0)] = row_stack(run("ssm_out_dw", _mm, yn, dx3b, "tn", out_dtype=BF16))
    dy, dzx, dssm_norm = _gate_norm_bwd(y, zx, row(w["ssm_norm"]), dyn)
    dxs, db, dc, ddt_g, ddt_gt, dbias, dbias_t, dalog, dalog_t, dd = _ssd_bwd(*ssd_args, states, dy)
    ddt_g = ddt_g + ddt_gt.transpose(0, 2, 1)
    dpre, dconv_w, dconv_b = _conv_bwd_pre(zx, w["ssm_conv_w"], row(w["ssm_conv_b"]), dxs, db, dc)
    dzx = _conv_bwd_in(dpre, w["ssm_conv_w"], dzx)
    ddt_pad = jnp.pad(ddt_g.transpose(1, 0, 2).reshape(s, SSM_HEADS), ((0, 0), (0, SSM_IN_PAD - SSM_IN)))
    dzx = lax.dynamic_update_slice(dzx, ddt_pad.astype(BF16), (0, DT_OFF))
    dw_in = run("ssm_in_dw", _mm, h2, dzx, "tn", out_dtype=BF16)
    in_shard = SSM_IN // N_SHARDS
    mats[("ssm_w_in", 0)] = jnp.stack([dw_in[:, i * in_shard:(i + 1) * in_shard] for i in range(N_SHARDS)])
    dh2 = run("ssm_in_dx", _mm, dzx, w["ssm_w_in"], "nt")
    dx2, dx2b, dmn1 = _rms_bwd(x2, row(w["mixer_norm"][1]), dh2, dx3, "mixer_norm_bwd_1")

    dx1, dx1b, dfn0 = _ffn_bwd(run, mats, x1, *ffn_w(0), ffn0, dx2, dx2b, 0)
    do = run("attn_out_dx", _mm, dx1b, w["attn_w_o"], "nt", out_dtype=BF16)
    mats[("attn_w_o", 0)] = row_stack(run("attn_out_dw", _mm, o, dx1b, "tn", out_dtype=BF16))
    dq, dkc, dkp, dvloc, dvlop, dvhic, dvhip, dsink = run("attn_bwd", _attn_bwd, q, kk, vlo, vhi, sinks, do)
    dqkv, dgq, dgk = _attn_prep_bwd(qkv, cos, sin, gq, gk, dq, (dkc, dkp), (dvloc, dvlop), (dvhic, dvhip))
    mats[("attn_w_qkv", 0)] = run("attn_qkv_dw", _mm, h0, dqkv, "tn", out_dtype=BF16, out_cols=True)
    dh0 = run("attn_qkv_dx", _mm, dqkv, w["attn_w_qkv"], "nt", b_cols=True)
    dx0, _, dmn0 = _rms_bwd(x, row(w["mixer_norm"][0]), dh0, dx1, "mixer_norm_bwd_0")

    fold = lambda v: v[0, :ATT_HEAD_DIM] + v[0, ATT_HEAD_DIM:]
    grads = {
        "mixer_norm": jnp.concatenate([dmn0, dmn1], axis=0),
        "ffn_norm": jnp.concatenate([dfn0, dfn1], axis=0),
        "attn_q_norm": fold(dgq), "attn_k_norm": fold(dgk),
        "attn_sinks": dsink[:, :, 0].reshape(-1),
        "ssm_conv_w": dconv_w, "ssm_conv_b": dconv_b.reshape(-1),
        "ssm_dt_bias": dbias.reshape(-1) + dbias_t.reshape(-1),
        "ssm_a_log": dalog.reshape(-1) + dalog_t.reshape(-1), "ssm_d": dd.reshape(-1),
        "ssm_norm": dssm_norm.reshape(-1),
    }
    return loss_row[0, 0], dx0, grads


OTHER_CHIPS = ((1, 0), (0, 1), (1, 1))


def _position():
    return lax.axis_index("x"), lax.axis_index("y"), lax.axis_index("c")


def _gather_shards(weights, layers):
    n_in, n_mat = len(weights), len(layers)

    def body(*refs):
        p, out = refs[:n_in], refs[n_in:n_in + n_mat]
        send_sems, recv_sems = refs[n_in + n_mat:]
        x, y, c = _position()
        me, sibling = (x, y, c), (x, y, 1 - c)
        chips = [(x ^ fx, y ^ fy) for fx, fy in OTHER_CHIPS]

        def rows(e, px, py, pc):
            half = out[e].shape[1] // 2
            return out[e].at[2 * px + py, pl.ds(pc * half, half), :]

        def copy(k, e, block, to, src=None):
            return pltpu.make_async_remote_copy(
                src_ref=rows(e, *block) if src is None else src, dst_ref=rows(e, *block),
                send_sem=send_sems.at[k * n_mat + e], recv_sem=recv_sems.at[k * n_mat + e],
                device_id=to, device_id_type=MESH)

        def own(e):
            i, l = layers[e]
            return pltpu.make_async_remote_copy(
                src_ref=p[i].at[l], dst_ref=out[e].at[2 * x + y], send_sem=send_sems.at[6 * n_mat + e],
                recv_sem=recv_sems.at[6 * n_mat + e], device_id=sibling, device_id_type=MESH)

        first, passed = [], []
        for e, (i, l) in enumerate(layers):
            half = out[e].shape[1] // 2
            first.append([copy(j, e, me, (*chip, c), src=p[i].at[l, pl.ds(c * half, half), :])
                          for j, chip in enumerate(chips)])
            for cp in first[-1]:
                cp.start()
        for e in range(n_mat):
            own(e).start()
        for e in range(n_mat):
            passed.append([copy(3 + j, e, (*chip, c), sibling) for j, chip in enumerate(chips)])
            for j, chip in enumerate(chips):
                copy(j, e, (*chip, c), me).wait_recv()
                passed[e][j].start()
        for e in range(n_mat):
            own(e).wait()
            for j, chip in enumerate(chips):
                copy(3 + j, e, (*chip, 1 - c), me).wait_recv()
        for e in range(n_mat):
            for cp in first[e] + passed[e]:
                cp.wait_send()

    return pl.pallas_call(
        body, name="gather_weights", in_specs=[ANY] * n_in, out_specs=[ANY] * n_mat,
        out_shape=[jax.ShapeDtypeStruct((N_SHARDS,) + weights[i].shape[1:], weights[i].dtype) for i, _ in layers],
        scratch_shapes=[_sems(7 * n_mat), _sems(7 * n_mat)],
    )(*weights)


def _all_gather8(block, name):
    m_per, n = block.shape

    def body(x_ref, out_ref, send_sems, recv_sems, local_sem):
        x, y, c = _position()
        me, sibling = (x, y, c), (x, y, 1 - c)
        chips = [(x ^ fx, y ^ fy) for fx, fy in OTHER_CHIPS]

        def rows(px, py, pc):
            return out_ref.at[pl.ds((4 * px + 2 * py + pc) * m_per, m_per), :]

        def copy(k, blk, to, src=None):
            return pltpu.make_async_remote_copy(
                src_ref=rows(*blk) if src is None else src, dst_ref=rows(*blk),
                send_sem=send_sems.at[k], recv_sem=recv_sems.at[k], device_id=to, device_id_type=MESH)

        mine = pltpu.make_async_copy(x_ref, rows(*me), local_sem)
        mine.start()
        first = [copy(0, me, sibling, src=x_ref)]
        first += [copy(1 + j, me, (*chip, c), src=x_ref) for j, chip in enumerate(chips)]
        for cp in first:
            cp.start()
        passed = [copy(4 + j, (*chip, c), sibling) for j, chip in enumerate(chips)]
        for j, chip in enumerate(chips):
            copy(1 + j, (*chip, c), me).wait_recv()
            passed[j].start()
        copy(0, sibling, me).wait_recv()
        for j, chip in enumerate(chips):
            copy(4 + j, (*chip, 1 - c), me).wait_recv()
        for cp in first + passed:
            cp.wait_send()
        mine.wait()

    return pl.pallas_call(
        body, name=name, out_shape=jax.ShapeDtypeStruct((N_DEV * m_per, n), block.dtype),
        in_specs=[pl.BlockSpec(memory_space=pltpu.VMEM)], out_specs=pl.BlockSpec(memory_space=pltpu.VMEM),
        scratch_shapes=[_sems(7), _sems(7), pltpu.SemaphoreType.DMA],
    )(block)


def _exchange(carry, name):
    n_in, n_out = len(carry["arrays"]), len(carry["out_shapes"])

    def body(*refs):
        copies = carry["build"](refs[:n_in], refs[n_in:n_in + n_out], refs[-2], refs[-1])
        for cp in copies:
            cp.start()
        for cp in copies:
            cp.wait()

    return pl.pallas_call(
        body, name=name, in_specs=[ANY] * n_in, out_specs=[ANY] * n_out, out_shape=list(carry["out_shapes"]),
        input_output_aliases=dict(carry.get("aliases", {})),
        scratch_shapes=[_sems(carry["n_sems"]), _sems(carry["n_sems"])],
    )(*carry["arrays"])


def _remote(src, dst, send_sems, recv_sems, k, to):
    return pltpu.make_async_remote_copy(src_ref=src, dst_ref=dst, send_sem=send_sems.at[k], recv_sem=recv_sems.at[k],
                                        device_id=to, device_id_type=MESH)


def _gather_over_ici(blocks, layers):
    def build(p, out, send_sems, recv_sems):
        x, y, c = _position()
        copies = []
        for e, l in enumerate(layers):
            half = out[e].shape[1] // 2
            rows = pl.ds(c * half, half)
            for j, (fx, fy) in enumerate(OTHER_CHIPS):
                copies.append(_remote(p[e].at[l, rows, :], out[e].at[2 * x + y, rows, :], send_sems, recv_sems,
                                      3 * e + j, (x ^ fx, y ^ fy, c)))
        return copies

    shapes = [jax.ShapeDtypeStruct((N_SHARDS,) + b.shape[1:], b.dtype) for b in blocks]
    return dict(build=build, arrays=list(blocks), out_shapes=shapes, n_sems=3 * len(layers))


def _gather_over_d2d(stacks, blocks, layers):
    n = len(stacks)

    def build(refs, out, send_sems, recv_sems):
        p = refs[n:]
        x, y, c = _position()
        sibling = (x, y, 1 - c)
        copies = []
        for e, l in enumerate(layers):
            half = out[e].shape[1] // 2
            for j, (fx, fy) in enumerate(OTHER_CHIPS):
                rows = out[e].at[2 * (x ^ fx) + (y ^ fy), pl.ds(c * half, half), :]
                copies.append(_remote(rows, rows, send_sems, recv_sems, 4 * e + j, sibling))
            copies.append(_remote(p[e].at[l], out[e].at[2 * x + y], send_sems, recv_sems, 4 * e + 3, sibling))
        return copies

    shapes = [jax.ShapeDtypeStruct(s.shape, s.dtype) for s in stacks]
    return dict(build=build, arrays=list(stacks) + list(blocks), out_shapes=shapes, n_sems=4 * n,
                aliases={i: i for i in range(n)})


def _grads_to_sibling(stacks):
    def build(g, out, send_sems, recv_sems):
        x, y, c = _position()
        copies = []
        for e in range(len(stacks)):
            half = g[e].shape[1] // 2
            copies.append(_remote(g[e].at[:, pl.ds((1 - c) * half, half), :], out[e], send_sems, recv_sems, e,
                                  (x, y, 1 - c)))
        return copies

    shapes = [jax.ShapeDtypeStruct((N_SHARDS, g.shape[1] // 2, g.shape[2]), g.dtype) for g in stacks]
    return dict(build=build, arrays=list(stacks), out_shapes=shapes, n_sems=len(stacks))


def _grads_to_owners(partials):
    def build(p, out, send_sems, recv_sems):
        x, y, c = _position()
        copies = []
        for e in range(len(partials)):
            for k, (fx, fy) in enumerate(OTHER_CHIPS):
                px, py = x ^ fx, y ^ fy
                copies.append(_remote(p[e].at[2 * px + py], out[e].at[k], send_sems, recv_sems, 3 * e + k,
                                      (px, py, c)))
        return copies

    shapes = [jax.ShapeDtypeStruct((len(OTHER_CHIPS),) + p.shape[1:], p.dtype) for p in partials]
    return dict(build=build, arrays=list(partials), out_shapes=shapes, n_sems=3 * len(partials))


def _share_halves(grads, layers):
    def build(_, out, send_sems, recv_sems):
        x, y, c = _position()
        copies = []
        for e, (i, l) in enumerate(layers):
            half = out[i].shape[1] // 2
            rows = out[i].at[l, pl.ds(c * half, half), :]
            copies.append(_remote(rows, rows, send_sems, recv_sems, e, (x, y, 1 - c)))
        return copies

    return dict(build=build, arrays=list(grads), out_shapes=[jax.ShapeDtypeStruct(g.shape, g.dtype) for g in grads],
                n_sems=len(layers), aliases={i: i for i in range(len(grads))})


ADD_BLOCK_ELEMS = 1 << 19


def _add_rows(half, cols):
    return _tile(half, max(16, ADD_BLOCK_ELEMS // cols // 16 * 16), 16)


def _add_pair(stack, recv, c_idx, name):
    _, half, cols = recv.shape
    tr = _add_rows(half, cols)
    nt = half // tr

    def body(c_ref, a_ref, b_ref, o_ref):
        o_ref[...] = (a_ref[...].astype(F32) + b_ref[...].astype(F32)).astype(o_ref.dtype)

    blk = pl.BlockSpec((None, tr, cols), lambda s, i, c_ref: (s, i, 0))
    return pl.pallas_call(
        body, name=name,
        grid_spec=pltpu.PrefetchScalarGridSpec(
            num_scalar_prefetch=1, grid=(N_SHARDS, nt),
            in_specs=[pl.BlockSpec((None, tr, cols), lambda s, i, c_ref: (s, c_ref[0] * nt + i, 0)), blk],
            out_specs=blk),
        out_shape=jax.ShapeDtypeStruct(recv.shape, recv.dtype),
        compiler_params=_params("parallel", "parallel"),
    )(c_idx, stack, recv)


def _add_owned(partial, recv, sc_idx, layer, shape, into, name):
    _, half, cols = partial.shape
    tr = _add_rows(half, cols)
    nt = half // tr

    def body(sc_ref, a_ref, r0_ref, r1_ref, r2_ref, *rest):
        o_ref = rest[-1]
        o_ref[...] = (((a_ref[...].astype(F32) + r0_ref[...].astype(F32)) + r1_ref[...].astype(F32))
                      + r2_ref[...].astype(F32))

    slot = lambda k: pl.BlockSpec((None, tr, cols), lambda i, sc_ref: (k, i, 0))
    has_into = into is not None
    return pl.pallas_call(
        body, name=name,
        grid_spec=pltpu.PrefetchScalarGridSpec(
            num_scalar_prefetch=1, grid=(nt,),
            in_specs=[pl.BlockSpec((None, tr, cols), lambda i, sc_ref: (sc_ref[0], i, 0)), slot(0), slot(1), slot(2)]
            + ([ANY] if has_into else []),
            out_specs=pl.BlockSpec((None, tr, cols), lambda i, sc_ref: (layer, sc_ref[1] * nt + i, 0))),
        out_shape=jax.ShapeDtypeStruct(shape, F32),
        input_output_aliases={5: 0} if has_into else {},
        compiler_params=_params("parallel"),
    )(*((sc_idx, partial, recv, recv, recv) + ((into,) if has_into else ())))


def _sum8(gathered):
    m = gathered.shape[0] // N_DEV

    def body(g_ref, o_ref):
        total = g_ref[0:m, :]
        for d in range(1, N_DEV):
            total = total + g_ref[d * m:(d + 1) * m, :]
        o_ref[...] = total

    return pl.pallas_call(
        body, name="small_grads_sum", out_shape=jax.ShapeDtypeStruct((m, LANES), F32),
        in_specs=[pl.BlockSpec(memory_space=pltpu.VMEM)], out_specs=pl.BlockSpec(memory_space=pltpu.VMEM),
    )(gathered)


ADAMW_BLOCK_ELEMS = 1 << 18


def _adamw(w, g, m, v, name):
    l, r, cols = w.shape
    tr = _tile(r, max(8, ADAMW_BLOCK_ELEMS // cols // 8 * 8), 8)

    def body(w_ref, g_ref, m_ref, v_ref, go_ref, d_ref, nm_ref, nv_ref):
        gv = g_ref[...]
        go_ref[...] = gv
        nm = ADAM_B1 * m_ref[...] + (1.0 - ADAM_B1) * gv
        nv = ADAM_B2 * v_ref[...] + (1.0 - ADAM_B2) * jnp.square(gv)
        m_hat = nm / (1.0 - ADAM_B1 ** ADAM_STEP)
        v_hat = nv / (1.0 - ADAM_B2 ** ADAM_STEP)
        d_ref[...] = -ADAM_LR * (m_hat / (jnp.sqrt(v_hat) + ADAM_EPS) + ADAM_WD * w_ref[...])
        nm_ref[...] = nm
        nv_ref[...] = nv

    blk = pl.BlockSpec((None, tr, cols), lambda a, i: (a, i, 0))
    return pl.pallas_call(
        body, name=name, grid=(l, r // tr), in_specs=[blk] * 4, out_specs=[blk] * 4,
        out_shape=[jax.ShapeDtypeStruct(w.shape, F32)] * 4, compiler_params=_params("parallel", "parallel"),
    )(w, g, m, v)


WEIGHTS = ("mixer_norm", "ffn_norm", "attn_w_qkv", "attn_q_norm", "attn_k_norm", "attn_sinks", "attn_w_o",
           "ssm_w_in", "ssm_conv_w", "ssm_conv_b", "ssm_dt_bias", "ssm_a_log", "ssm_d", "ssm_norm", "ssm_w_out",
           "ffn_w_gate", "ffn_w_up", "ffn_w_down")
BIG = ("attn_w_qkv", "attn_w_o", "ffn_w_gate", "ffn_w_up", "ffn_w_down", "ssm_w_in", "ssm_w_out")
MATRICES = (("attn_w_qkv", 0), ("attn_w_o", 0), ("ffn_w_gate", 0), ("ffn_w_up", 0), ("ffn_w_down", 0),
            ("ssm_w_in", 0), ("ssm_w_out", 0), ("ffn_w_gate", 1), ("ffn_w_up", 1), ("ffn_w_down", 1))
MATRIX_LAYERS = tuple((BIG.index(n), l) for n, l in MATRICES)
GROUPS = {"attn": MATRICES[0:2], "ffn0": MATRICES[2:5], "ssm": MATRICES[5:7], "ffn1": MATRICES[7:10]}
SMALL_SHARDED = ("ssm_conv_w", "ssm_conv_b", "ssm_norm")
SMALL = tuple(n for n in WEIGHTS if n not in BIG)


def _pack_rows(parts, row_unit=8):
    flat = jnp.concatenate([p.reshape(-1) for p in parts])
    pad = (-flat.shape[0]) % (LANES * row_unit)
    return jnp.pad(flat, (0, pad)).reshape(-1, LANES)


def _unpack(flat, shapes):
    out, off = [], 0
    for shp in shapes:
        size = math.prod(shp)
        out.append(flat[off:off + size].reshape(shp))
        off += size
    return out


def kernel(x, positions, mixer_norm, ffn_norm, attn_w_qkv, attn_q_norm, attn_k_norm, attn_sinks, attn_w_o, ssm_w_in, ssm_conv_w, ssm_conv_b, ssm_dt_bias, ssm_a_log, ssm_d, ssm_norm, ssm_w_out, ffn_w_gate, ffn_w_up, ffn_w_down, loss_target, m_mixer_norm, m_ffn_norm, m_attn_w_qkv, m_attn_q_norm, m_attn_k_norm, m_attn_sinks, m_attn_w_o, m_ssm_w_in, m_ssm_conv_w, m_ssm_conv_b, m_ssm_dt_bias, m_ssm_a_log, m_ssm_d, m_ssm_norm, m_ssm_w_out, m_ffn_w_gate, m_ffn_w_up, m_ffn_w_down, v_mixer_norm, v_ffn_norm, v_attn_w_qkv, v_attn_q_norm, v_attn_k_norm, v_attn_sinks, v_attn_w_o, v_ssm_w_in, v_ssm_conv_w, v_ssm_conv_b, v_ssm_dt_bias, v_ssm_a_log, v_ssm_d, v_ssm_norm, v_ssm_w_out, v_ffn_w_gate, v_ffn_w_up, v_ffn_w_down):
    args = locals()
    w = {n: args[n] for n in WEIGHTS}
    m = {n: args["m_" + n] for n in WEIGHTS}
    v = {n: args["v_" + n] for n in WEIGHTS}
    ax, ay, ac = lax.axis_index("x"), lax.axis_index("y"), lax.axis_index("c")
    shard = 2 * ax + ay

    wb = {n: w[n].astype(BF16) for n in BIG}
    wl, hooks = {"ffn_w_gate": [None, None], "ffn_w_up": [None, None], "ffn_w_down": [None, None]}, {}

    def gathered(keys, stacks):
        for (n, l), st in zip(keys, stacks):
            if n == "ssm_w_in":
                wl[n] = jnp.concatenate([st[i] for i in range(N_SHARDS)]
                                        + [jnp.zeros((st.shape[1], SSM_IN_PAD - SSM_IN), BF16)], axis=1)
            elif n in ("ffn_w_gate", "ffn_w_up"):
                wl[n][l] = st
            elif n == "ffn_w_down":
                wl[n][l] = st.reshape(st.shape[0] * st.shape[1], st.shape[2])
            elif n == "attn_w_qkv":
                wl[n] = st
            else:
                wl[n] = st.reshape(st.shape[0] * st.shape[1], st.shape[2])

    def behind(name, hook):
        hooks[name] = _both(hooks[name], hook) if name in hooks else hook

    def gather_behind(keys, first_leg, second_leg):
        blocks, layers, got = [wb[n] for n, _ in keys], [l for _, l in keys], {}
        behind(first_leg, _Hook(lambda: _gather_over_ici(blocks, layers), lambda res: got.update(stacks=res)))
        behind(second_leg, _Hook(lambda: _gather_over_d2d(got["stacks"], blocks, layers),
                                 lambda res: gathered(keys, res)))

    gathered(GROUPS["attn"], _gather_shards([wb[n] for n, _ in GROUPS["attn"]],
                                            [(e, l) for e, (_, l) in enumerate(GROUPS["attn"])]))
    gather_behind(GROUPS["ffn0"][:2], "attn_fwd", "attn_out")
    gather_behind(GROUPS["ffn0"][2:], "attn_qkv", "attn_fwd")
    gather_behind(GROUPS["ssm"][:1], "ffn_gate_0", "ffn_up_0")
    gather_behind(GROUPS["ssm"][1:], "ffn_up_0", "ffn_down_0")
    gather_behind(GROUPS["ffn1"], "ssm_in", "ssm_out")
    small_shapes = [w[n].shape for n in SMALL_SHARDED]
    small_all = _all_gather8(_pack_rows([w[n] for n in SMALL_SHARDED]), "gather_small_params")
    small_all = small_all.reshape(N_DEV, -1)[::2]
    full, off = {}, 0
    for n, shp in zip(SMALL_SHARDED, small_shapes):
        size = math.prod(shp)
        seg = small_all[:, off:off + size].reshape((N_SHARDS,) + shp)
        full[n] = jnp.moveaxis(seg, 0, -2).reshape(shp[:-1] + (N_SHARDS * shp[-1],))
        off += size
    wl.update({
        "mixer_norm": mixer_norm, "ffn_norm": ffn_norm,
        "attn_q_norm": attn_q_norm[0], "attn_k_norm": attn_k_norm[0], "attn_sinks": attn_sinks[0],
        "ssm_conv_w": full["ssm_conv_w"][0], "ssm_conv_b": full["ssm_conv_b"][0],
        "ssm_dt_bias": ssm_dt_bias[0], "ssm_a_log": ssm_a_log[0], "ssm_d": ssm_d[0],
        "ssm_norm": full["ssm_norm"][0],
    })

    c_idx = ac.reshape(1).astype(jnp.int32)
    sc_idx = jnp.stack([shard, ac]).astype(jnp.int32)
    mats, halves = {}, {n: None for n in BIG}

    def pair_sums(keys, recv):
        return [_add_pair(mats[k], r, c_idx, f"grads_add_pair_{k[0]}_{k[1]}") for k, r in zip(keys, recv)]

    def owner_sums(keys, partials, recv):
        for (n, l), p, r in zip(keys, partials, recv):
            halves[n] = _add_owned(p, r, sc_idx, l, w[n].shape, halves[n], f"grads_add_owned_{n}_{l}")

    def reduce_behind(keys, first_leg, second_leg):
        got = {}
        behind(first_leg, _Hook(lambda: _grads_to_sibling([mats[k] for k in keys]),
                                lambda res: got.update(partials=pair_sums(keys, res))))
        behind(second_leg, _Hook(lambda: _grads_to_owners(got["partials"]),
                                 lambda res: owner_sums(keys, got["partials"], res)))

    reduce_behind(GROUPS["ffn1"], "ssm_out_dx", "ssm_in_dw")
    reduce_behind(GROUPS["ssm"][:1], "ssm_in_dx", "ffn_down_dx_0")
    reduce_behind(GROUPS["ssm"][1:], "ssm_in_dx", "ffn_down_dw_0")
    reduce_behind(GROUPS["ffn0"], "attn_out_dx", "attn_bwd")
    reduce_behind(GROUPS["attn"][1:], "attn_bwd", "attn_qkv_dw")
    last, early, tail = GROUPS["attn"][0], [n for n in BIG if n != "attn_w_qkv"], {}
    behind("attn_qkv_dx", _Hook(lambda: _grads_to_sibling([mats[last]]),
                                lambda res: tail.update(partials=pair_sums([last], res))))
    behind("attn_qkv_dx", _Hook(
        lambda: _share_halves([halves[n] for n in early], [(early.index(n), l) for n, l in MATRICES if n in early]),
        lambda res: tail.update(grads=dict(zip(early, res)))))
    loss_part, dx, g_full = _local_step(x[0], positions[0], loss_target[0], wl, hooks, mats)
    owner_sums([last], tail["partials"], _exchange(_grads_to_owners(tail["partials"]), "grads_to_owners"))
    grads = tail["grads"]
    grads[last[0]], = _exchange(_share_halves([halves[last[0]]], [(0, 0)]), "grads_share_halves")

    small_full_shapes = [g_full[n].shape for n in SMALL] + [(1,)]
    small_g = _pack_rows([g_full[n] for n in SMALL] + [loss_part.reshape(1)])
    small_sum = _sum8(_all_gather8(small_g, "gather_small_grads")).reshape(-1)
    *small_list, loss = _unpack(small_sum, small_full_shapes)
    for n, g in zip(SMALL, small_list):
        if n in SMALL_SHARDED:
            width = w[n].shape[-1]
            g = lax.dynamic_slice_in_dim(g, shard * width, width, axis=g.ndim - 1)
        grads[n] = g.reshape(w[n].shape)

    delta, new_m, new_v = {}, {}, {}
    for n in BIG:
        grads[n], delta[n], new_m[n], new_v[n] = _adamw(w[n], grads[n], m[n], v[n], "adamw_" + n)
    small_local = [w[n].shape for n in SMALL]
    pk = lambda t: _pack_rows([t[n] for n in SMALL])[None]
    outs = _adamw(pk(w), pk(grads), pk(m), pk(v), "adamw_small")
    for res, o in zip((delta, new_m, new_v), outs[1:]):
        for n, a in zip(SMALL, _unpack(o.reshape(-1), small_local)):
            res[n] = a

    return (loss.reshape(()), dx[None], *[grads[n] for n in WEIGHTS], *[delta[n] for n in WEIGHTS],
            *[new_m[n] for n in WEIGHTS], *[new_v[n] for n in WEIGHTS])
```
